```python
import math
import jax, jax.numpy as jnp
from jax import lax
import numpy as np

D_MODEL = 1024
BATCH = 8
SEQ = 16384
DEPTH = 1

HEAD_DIM = 128
HEADS_PER_GROUP = 4
ATTN_GROUPS = ((128, 1), (512, 4), (2048, 16))
N_ATTN_GROUPS = len(ATTN_GROUPS)
ATTN_HEADS = HEADS_PER_GROUP * N_ATTN_GROUPS
ATTN_QK_WIDTH = ATTN_HEADS * HEAD_DIM
ATTN_OUT_WIDTH = HEADS_PER_GROUP * HEAD_DIM
BLOCK = 128
ROPE_THETA = 500000.0
ROPE_DIM = HEAD_DIM // 4
SSM_WIDTH = 512
SSM_GROUP = 16
SSM_GROUPS = SSM_WIDTH // SSM_GROUP
SSM_STATE = 64
DT_MIN = 0.001
DT_MAX = 0.1
D_FF = -(-8 * D_MODEL // (3 * 256)) * 256
PLE_DIM = 256
EPS = 1e-6
IN_WIDTH = 3 * ATTN_QK_WIDTH + SSM_WIDTH + 2 * D_MODEL

kernel_name = "hybrid_dilated_attn_s5_gated_block"


def rmsnorm(x, g):
    xf = x.astype(jnp.float32)
    y = xf * lax.rsqrt(jnp.mean(xf * xf, axis=-1, keepdims=True) + EPS)
    return (y * g.astype(jnp.float32)).astype(x.dtype)


def partial_rotary(x, positions):
    half = ROPE_DIM // 2
    inv_freq = ROPE_THETA ** (-jnp.arange(half, dtype=jnp.float32) * 2.0 / ROPE_DIM)
    ang = positions.astype(jnp.float32)[..., None] * inv_freq
    cos = jnp.cos(ang)[:, :, None, :]
    sin = jnp.sin(ang)[:, :, None, :]
    xr = x[..., :ROPE_DIM].astype(jnp.float32)
    x1, x2 = xr[..., :half], xr[..., half:]
    rot = jnp.concatenate([x1 * cos - x2 * sin, x2 * cos + x1 * sin], axis=-1).astype(x.dtype)
    return jnp.concatenate([rot, x[..., ROPE_DIM:]], axis=-1)


def dilated_band_attention(q, k, v, dilation, band):
    B, S, H, Dh = q.shape
    L = S // dilation
    Lp = -(-L // BLOCK) * BLOCK
    nb = Lp // BLOCK

    def to_sub(t):
        t = jnp.moveaxis(t.reshape(B, L, dilation, H, Dh), 2, 1)
        t = jnp.pad(t, ((0, 0), (0, 0), (0, Lp - L), (0, 0), (0, 0)))
        return t.reshape(B, dilation, nb, BLOCK, H, Dh)

    def with_prev(t):
        prev = jnp.pad(t, ((0, 0), (0, 0), (1, 0), (0, 0), (0, 0), (0, 0)))[:, :, :-1]
        return jnp.concatenate([prev, t], axis=3)

    def from_sub(t):
        rest = t.shape[4:]
        t = t.reshape((B, dilation, Lp) + rest)[:, :, :L]
        return jnp.moveaxis(t, 1, 2).reshape((B, S) + rest)

    qb = to_sub(q)
    kk = with_prev(to_sub(k))
    vv = with_prev(to_sub(v))
    scale = 1.0 / math.sqrt(Dh)
    s = jnp.einsum('brnqhd,brnkhd->brnhqk', qb, kk,
                   preferred_element_type=jnp.float32) * scale
    qi = jnp.arange(BLOCK)[:, None]
    kj = jnp.arange(2 * BLOCK)[None, :]
    rel = BLOCK + qi - kj
    in_band = (rel >= 0) & (rel <= band)
    not_first = (jnp.arange(nb) > 0)[:, None, None]
    valid = in_band[None] & (not_first | (kj >= BLOCK)[None])
    s = jnp.where(valid[None, None, :, None], s, -jnp.inf)
    m = jnp.max(s, axis=-1, keepdims=True)
    pexp = jnp.exp(s - m)
    den = jnp.sum(pexp, axis=-1, keepdims=True)
    o = jnp.einsum('brnhqk,brnkhd->brnqhd', pexp, vv.astype(jnp.float32))
    o = o / jnp.swapaxes(den[..., 0], -1, -2)[..., None]
    lse = jnp.swapaxes(m[..., 0] + jnp.log(den[..., 0]), -1, -2)
    return from_sub(o), from_sub(lse)


def _complex_affine_combine(e1, e2):
    a1r, a1i, b1r, b1i = e1
    a2r, a2i, b2r, b2i = e2
    ar = a1r * a2r - a1i * a2i
    ai = a1r * a2i + a1i * a2r
    br = a2r * b1r - a2i * b1i + b2r
    bi = a2r * b1i + a2i * b1r + b2i
    return (ar, ai, br, bi)


def s5_ssm(u, a_re, a_im, log_dt, b_re, b_im, c_re, c_im, d_skip):
    B, S, _ = u.shape
    uf = u.astype(jnp.float32).reshape(B, S, SSM_GROUPS, SSM_GROUP)
    dt = jnp.exp(log_dt.astype(jnp.float32))[:, None]
    lr = a_re.astype(jnp.float32)
    li = a_im.astype(jnp.float32)
    mag = jnp.exp(lr * dt)
    bar_re = mag * jnp.cos(li * dt)
    bar_im = mag * jnp.sin(li * dt)
    nr = bar_re - 1.0
    ni = bar_im
    den = lr * lr + li * li
    z_re = (nr * lr + ni * li) / den
    z_im = (ni * lr - nr * li) / den
    br_ = b_re.astype(jnp.float32)
    bi_ = b_im.astype(jnp.float32)
    bb_re = z_re[..., None] * br_ - z_im[..., None] * bi_
    bb_im = z_re[..., None] * bi_ + z_im[..., None] * br_
    bu_re = jnp.einsum('bsgc,gpc->bsgp', uf, bb_re)
    bu_im = jnp.einsum('bsgc,gpc->bsgp', uf, bb_im)
    ar = jnp.broadcast_to(bar_re, bu_re.shape)
    ai = jnp.broadcast_to(bar_im, bu_im.shape)
    _, _, h_re, h_im = lax.associative_scan(_complex_affine_combine, (ar, ai, bu_re, bu_im), axis=1)
    y = (jnp.einsum('bsgp,gcp->bsgc', h_re, c_re.astype(jnp.float32))
         - jnp.einsum('bsgp,gcp->bsgc', h_im, c_im.astype(jnp.float32))
         + d_skip.astype(jnp.float32) * uf)
    return y.reshape(B, S, SSM_WIDTH).astype(u.dtype)


def hybrid_layer(h, p_l, positions, g_mix, w_in, a_re, a_im, log_dt, b_re, b_im, c_re, c_im,
                 d_skip, w_attn_proj, w_glu_a, w_glu_b, w_out, g_ffn, w_ffn_gate, w_ffn_up,
                 w_ffn_down, w_ple_gate, w_ple_proj):
    B, S, _ = h.shape
    n = rmsnorm(h, g_mix)
    z = n @ w_in
    o0 = ATTN_QK_WIDTH
    q = z[..., 0:o0].reshape(B, S, ATTN_HEADS, HEAD_DIM)
    k = z[..., o0:2 * o0].reshape(B, S, ATTN_HEADS, HEAD_DIM)
    v = z[..., 2 * o0:3 * o0].reshape(B, S, ATTN_HEADS, HEAD_DIM)
    o1 = 3 * o0
    u = z[..., o1:o1 + SSM_WIDTH]
    o2 = o1 + SSM_WIDTH
    gate_attn = jax.nn.sigmoid(z[..., o2:o2 + D_MODEL])
    gate_ssm = jax.nn.sigmoid(z[..., o2 + D_MODEL:o2 + 2 * D_MODEL])

    q = partial_rotary(q, positions)
    k = partial_rotary(k, positions)
    outs = []
    lses = []
    for gi, (window, dilation) in enumerate(ATTN_GROUPS):
        hs = slice(gi * HEADS_PER_GROUP, (gi + 1) * HEADS_PER_GROUP)
        o_g, l_g = dilated_band_attention(q[:, :, hs], k[:, :, hs], v[:, :, hs],
                                          dilation, window // dilation)
        outs.append(o_g)
        lses.append(l_g)
    wts = jax.nn.softmax(jnp.stack(lses, axis=0), axis=0)
    attn = jnp.sum(wts[..., None] * jnp.stack(outs, axis=0), axis=0)
    attn_d = attn.reshape(B, S, ATTN_OUT_WIDTH).astype(h.dtype) @ w_attn_proj

    y = jax.nn.gelu(s5_ssm(u, a_re, a_im, log_dt, b_re, b_im, c_re, c_im, d_skip))
    ssm_d = (y @ w_glu_a) * jax.nn.sigmoid(y @ w_glu_b)

    h = h + (gate_attn * attn_d + gate_ssm * ssm_d) @ w_out

    n2 = rmsnorm(h, g_ffn)
    h = h + (jax.nn.silu(n2 @ w_ffn_gate) * (n2 @ w_ffn_up)) @ w_ffn_down

    h = h + jax.nn.sigmoid(h @ w_ple_gate) * (p_l.astype(h.dtype) @ w_ple_proj)
    return h


def _fwd_setup_inputs(seed: int = 0) -> dict:
    key = jax.random.key(seed)
    ks = jax.random.split(key, 26)
    f32 = jnp.float32

    def nrm(k, shape, fan_in):
        return jax.random.normal(k, shape, f32) * (fan_in ** -0.5)

    x = jax.random.normal(ks[0], (BATCH, SEQ, D_MODEL), f32)
    p = jax.random.normal(ks[1], (DEPTH, BATCH, SEQ, PLE_DIM), f32)
    positions = jnp.broadcast_to(jnp.arange(SEQ, dtype=jnp.int32)[None, :], (BATCH, SEQ))
    g_mix = 1.0 + 0.05 * jax.random.normal(ks[2], (DEPTH, D_MODEL), f32)
    w_in = nrm(ks[3], (DEPTH, D_MODEL, IN_WIDTH), D_MODEL)
    a_re = -0.5 + 0.01 * jax.random.normal(ks[4], (DEPTH, SSM_GROUPS, SSM_STATE), f32)
    a_im = (jnp.pi * jnp.arange(SSM_STATE, dtype=f32)[None, None, :]
            + 0.01 * jax.random.normal(ks[5], (DEPTH, SSM_GROUPS, SSM_STATE), f32))
    log_dt = jax.random.uniform(ks[6], (DEPTH, SSM_GROUPS), f32,
                                minval=math.log(DT_MIN), maxval=math.log(DT_MAX))
    b_re = nrm(ks[7], (DEPTH, SSM_GROUPS, SSM_STATE, SSM_GROUP), 2 * SSM_GROUP)
    b_im = nrm(ks[8], (DEPTH, SSM_GROUPS, SSM_STATE, SSM_GROUP), 2 * SSM_GROUP)
    c_re = nrm(ks[9], (DEPTH, SSM_GROUPS, SSM_GROUP, SSM_STATE), SSM_STATE)
    c_im = nrm(ks[10], (DEPTH, SSM_GROUPS, SSM_GROUP, SSM_STATE), SSM_STATE)
    d_skip = jax.random.normal(ks[11], (DEPTH, SSM_GROUPS, SSM_GROUP), f32)
    w_attn_proj = nrm(ks[12], (DEPTH, ATTN_OUT_WIDTH, D_MODEL), ATTN_OUT_WIDTH)
    w_glu_a = nrm(ks[13], (DEPTH, SSM_WIDTH, D_MODEL), SSM_WIDTH)
    w_glu_b = nrm(ks[14], (DEPTH, SSM_WIDTH, D_MODEL), SSM_WIDTH)
    w_out = nrm(ks[15], (DEPTH, D_MODEL, D_MODEL), D_MODEL)
    g_ffn = 1.0 + 0.05 * jax.random.normal(ks[16], (DEPTH, D_MODEL), f32)
    w_ffn_gate = nrm(ks[17], (DEPTH, D_MODEL, D_FF), D_MODEL)
    w_ffn_up = nrm(ks[18], (DEPTH, D_MODEL, D_FF), D_MODEL)
    w_ffn_down = nrm(ks[19], (DEPTH, D_FF, D_MODEL), D_FF)
    w_ple_gate = nrm(ks[20], (DEPTH, D_MODEL, D_MODEL), D_MODEL)
    w_ple_proj = nrm(ks[21], (DEPTH, PLE_DIM, D_MODEL), PLE_DIM)
    g_final = 1.0 + 0.05 * jax.random.normal(ks[22], (D_MODEL,), f32)
    return {"x": x, "p": p, "positions": positions, "g_mix": g_mix, "w_in": w_in,
            "a_re": a_re, "a_im": a_im, "log_dt": log_dt, "b_re": b_re, "b_im": b_im,
            "c_re": c_re, "c_im": c_im, "d_skip": d_skip, "w_attn_proj": w_attn_proj,
            "w_glu_a": w_glu_a, "w_glu_b": w_glu_b, "w_out": w_out, "g_ffn": g_ffn,
            "w_ffn_gate": w_ffn_gate, "w_ffn_up": w_ffn_up, "w_ffn_down": w_ffn_down,
            "w_ple_gate": w_ple_gate, "w_ple_proj": w_ple_proj, "g_final": g_final}


def _fwd_reference(x, p, positions, g_mix, w_in, a_re, a_im, log_dt, b_re, b_im, c_re, c_im, d_skip,
              w_attn_proj, w_glu_a, w_glu_b, w_out, g_ffn, w_ffn_gate, w_ffn_up, w_ffn_down,
              w_ple_gate, w_ple_proj, g_final):
    h = x
    for i in range(DEPTH):
        h = hybrid_layer(h, p[i], positions, g_mix[i], w_in[i], a_re[i], a_im[i], log_dt[i],
                         b_re[i], b_im[i], c_re[i], c_im[i], d_skip[i], w_attn_proj[i],
                         w_glu_a[i], w_glu_b[i], w_out[i], g_ffn[i], w_ffn_gate[i], w_ffn_up[i],
                         w_ffn_down[i], w_ple_gate[i], w_ple_proj[i])
    return rmsnorm(h, g_final)


import jax as _jax
import jax.numpy as _jnp

TWIN_FORMAT = 'train_step'
FWD_PARAMS = ['x', 'p', 'positions', 'g_mix', 'w_in', 'a_re', 'a_im', 'log_dt', 'b_re', 'b_im', 'c_re', 'c_im', 'd_skip', 'w_attn_proj', 'w_glu_a', 'w_glu_b', 'w_out', 'g_ffn', 'w_ffn_gate', 'w_ffn_up', 'w_ffn_down', 'w_ple_gate', 'w_ple_proj', 'g_final']
TWIN_WEIGHTS = ['g_mix', 'w_in', 'a_re', 'a_im', 'log_dt', 'b_re', 'b_im', 'c_re', 'c_im', 'd_skip', 'w_attn_proj', 'w_glu_a', 'w_glu_b', 'w_out', 'g_ffn', 'w_ffn_gate', 'w_ffn_up', 'w_ffn_down', 'w_ple_gate', 'w_ple_proj', 'g_final']
TWIN_DIFF_INPUT = 'x'
TWIN_INPUTS = ['x', 'p', 'positions', 'g_mix', 'w_in', 'a_re', 'a_im', 'log_dt', 'b_re', 'b_im', 'c_re', 'c_im', 'd_skip', 'w_attn_proj', 'w_glu_a', 'w_glu_b', 'w_out', 'g_ffn', 'w_ffn_gate', 'w_ffn_up', 'w_ffn_down', 'w_ple_gate', 'w_ple_proj', 'g_final', 'loss_target', 'm_g_mix', 'm_w_in', 'm_a_re', 'm_a_im', 'm_log_dt', 'm_b_re', 'm_b_im', 'm_c_re', 'm_c_im', 'm_d_skip', 'm_w_attn_proj', 'm_w_glu_a', 'm_w_glu_b', 'm_w_out', 'm_g_ffn', 'm_w_ffn_gate', 'm_w_ffn_up', 'm_w_ffn_down', 'm_w_ple_gate', 'm_w_ple_proj', 'm_g_final', 'v_g_mix', 'v_w_in', 'v_a_re', 'v_a_im', 'v_log_dt', 'v_b_re', 'v_b_im', 'v_c_re', 'v_c_im', 'v_d_skip', 'v_w_attn_proj', 'v_w_glu_a', 'v_w_glu_b', 'v_w_out', 'v_g_ffn', 'v_w_ffn_gate', 'v_w_ffn_up', 'v_w_ffn_down', 'v_w_ple_gate', 'v_w_ple_proj', 'v_g_final']
TWIN_OUTPUTS = ['loss', 'grad_x', 'grad_g_mix', 'grad_w_in', 'grad_a_re', 'grad_a_im', 'grad_log_dt', 'grad_b_re', 'grad_b_im', 'grad_c_re', 'grad_c_im', 'grad_d_skip', 'grad_w_attn_proj', 'grad_w_glu_a', 'grad_w_glu_b', 'grad_w_out', 'grad_g_ffn', 'grad_w_ffn_gate', 'grad_w_ffn_up', 'grad_w_ffn_down', 'grad_w_ple_gate', 'grad_w_ple_proj', 'grad_g_final', 'delta_g_mix', 'delta_w_in', 'delta_a_re', 'delta_a_im', 'delta_log_dt', 'delta_b_re', 'delta_b_im', 'delta_c_re', 'delta_c_im', 'delta_d_skip', 'delta_w_attn_proj', 'delta_w_glu_a', 'delta_w_glu_b', 'delta_w_out', 'delta_g_ffn', 'delta_w_ffn_gate', 'delta_w_ffn_up', 'delta_w_ffn_down', 'delta_w_ple_gate', 'delta_w_ple_proj', 'delta_g_final', 'new_m_g_mix', 'new_m_w_in', 'new_m_a_re', 'new_m_a_im', 'new_m_log_dt', 'new_m_b_re', 'new_m_b_im', 'new_m_c_re', 'new_m_c_im', 'new_m_d_skip', 'new_m_w_attn_proj', 'new_m_w_glu_a', 'new_m_w_glu_b', 'new_m_w_out', 'new_m_g_ffn', 'new_m_w_ffn_gate', 'new_m_w_ffn_up', 'new_m_w_ffn_down', 'new_m_w_ple_gate', 'new_m_w_ple_proj', 'new_m_g_final', 'new_v_g_mix', 'new_v_w_in', 'new_v_a_re', 'new_v_a_im', 'new_v_log_dt', 'new_v_b_re', 'new_v_b_im', 'new_v_c_re', 'new_v_c_im', 'new_v_d_skip', 'new_v_w_attn_proj', 'new_v_w_glu_a', 'new_v_w_glu_b', 'new_v_w_out', 'new_v_g_ffn', 'new_v_w_ffn_gate', 'new_v_w_ffn_up', 'new_v_w_ffn_down', 'new_v_w_ple_gate', 'new_v_w_ple_proj', 'new_v_g_final']
TWIN_LEAF_KINDS = {'loss': 'loss', 'grad_x': 'grad_x', 'grad_g_mix': 'grad_w', 'grad_w_in': 'grad_w', 'grad_a_re': 'grad_w', 'grad_a_im': 'grad_w', 'grad_log_dt': 'grad_w', 'grad_b_re': 'grad_w', 'grad_b_im': 'grad_w', 'grad_c_re': 'grad_w', 'grad_c_im': 'grad_w', 'grad_d_skip': 'grad_w', 'grad_w_attn_proj': 'grad_w', 'grad_w_glu_a': 'grad_w', 'grad_w_glu_b': 'grad_w', 'grad_w_out': 'grad_w', 'grad_g_ffn': 'grad_w', 'grad_w_ffn_gate': 'grad_w', 'grad_w_ffn_up': 'grad_w', 'grad_w_ffn_down': 'grad_w', 'grad_w_ple_gate': 'grad_w', 'grad_w_ple_proj': 'grad_w', 'grad_g_final': 'grad_w', 'delta_g_mix': 'delta_w', 'delta_w_in': 'delta_w', 'delta_a_re': 'delta_w', 'delta_a_im': 'delta_w', 'delta_log_dt': 'delta_w', 'delta_b_re': 'delta_w', 'delta_b_im': 'delta_w', 'delta_c_re': 'delta_w', 'delta_c_im': 'delta_w', 'delta_d_skip': 'delta_w', 'delta_w_attn_proj': 'delta_w', 'delta_w_glu_a': 'delta_w', 'delta_w_glu_b': 'delta_w', 'delta_w_out': 'delta_w', 'delta_g_ffn': 'delta_w', 'delta_w_ffn_gate': 'delta_w', 'delta_w_ffn_up': 'delta_w', 'delta_w_ffn_down': 'delta_w', 'delta_w_ple_gate': 'delta_w', 'delta_w_ple_proj': 'delta_w', 'delta_g_final': 'delta_w', 'new_m_g_mix': 'new_m', 'new_m_w_in': 'new_m', 'new_m_a_re': 'new_m', 'new_m_a_im': 'new_m', 'new_m_log_dt': 'new_m', 'new_m_b_re': 'new_m', 'new_m_b_im': 'new_m', 'new_m_c_re': 'new_m', 'new_m_c_im': 'new_m', 'new_m_d_skip': 'new_m', 'new_m_w_attn_proj': 'new_m', 'new_m_w_glu_a': 'new_m', 'new_m_w_glu_b': 'new_m', 'new_m_w_out': 'new_m', 'new_m_g_ffn': 'new_m', 'new_m_w_ffn_gate': 'new_m', 'new_m_w_ffn_up': 'new_m', 'new_m_w_ffn_down': 'new_m', 'new_m_w_ple_gate': 'new_m', 'new_m_w_ple_proj': 'new_m', 'new_m_g_final': 'new_m', 'new_v_g_mix': 'new_v', 'new_v_w_in': 'new_v', 'new_v_a_re': 'new_v', 'new_v_a_im': 'new_v', 'new_v_log_dt': 'new_v', 'new_v_b_re': 'new_v', 'new_v_b_im': 'new_v', 'new_v_c_re': 'new_v', 'new_v_c_im': 'new_v', 'new_v_d_skip': 'new_v', 'new_v_w_attn_proj': 'new_v', 'new_v_w_glu_a': 'new_v', 'new_v_w_glu_b': 'new_v', 'new_v_w_out': 'new_v', 'new_v_g_ffn': 'new_v', 'new_v_w_ffn_gate': 'new_v', 'new_v_w_ffn_up': 'new_v', 'new_v_w_ffn_down': 'new_v', 'new_v_w_ple_gate': 'new_v', 'new_v_w_ple_proj': 'new_v', 'new_v_g_final': 'new_v'}


def _forward(args):
    return _fwd_reference(*[args[k] for k in FWD_PARAMS])


def _output_shape():
    def fwd():
        inp = _fwd_setup_inputs(0)
        return _fwd_reference(*[inp[k] for k in FWD_PARAMS])
    out = _jax.eval_shape(fwd)
    return out.shape, out.dtype

N_MICROBATCH = 1
ADAM_LR = 0.001
ADAM_B1 = 0.9
ADAM_B2 = 0.999
ADAM_EPS = 1e-08
ADAM_WD = 0.01
ADAM_STEP = 10
PER_EXAMPLE_BATCH_AXIS = {'x': 0, 'p': 1, 'positions': 0, 'loss_target': 0}
SHARED_INPUTS = []
_WEIGHT_DTYPES = {'g_mix': _jnp.float32, 'w_in': _jnp.float32, 'a_re': _jnp.float32, 'a_im': _jnp.float32, 'log_dt': _jnp.float32, 'b_re': _jnp.float32, 'b_im': _jnp.float32, 'c_re': _jnp.float32, 'c_im': _jnp.float32, 'd_skip': _jnp.float32, 'w_attn_proj': _jnp.float32, 'w_glu_a': _jnp.float32, 'w_glu_b': _jnp.float32, 'w_out': _jnp.float32, 'g_ffn': _jnp.float32, 'w_ffn_gate': _jnp.float32, 'w_ffn_up': _jnp.float32, 'w_ffn_down': _jnp.float32, 'w_ple_gate': _jnp.float32, 'w_ple_proj': _jnp.float32, 'g_final': _jnp.float32}
MOMENT_SCALE = {'g_mix': 1.023371e-01, 'w_in': 3.796221e-02, 'a_re': 8.157405e-03, 'a_im': 7.482336e-03, 'log_dt': 5.860891e+00, 'b_re': 5.334753e-03, 'b_im': 5.243791e-03, 'c_re': 7.358136e-03, 'c_im': 7.361146e-03, 'd_skip': 1.057157e-01, 'w_attn_proj': 3.686167e-02, 'w_glu_a': 7.836800e-02, 'w_glu_b': 2.132087e-02, 'w_out': 7.938297e-02, 'g_ffn': 2.427536e-01, 'w_ffn_gate': 1.044427e-01, 'w_ffn_up': 1.019582e-01, 'w_ffn_down': 1.703203e-01, 'w_ple_gate': 6.974878e-02, 'w_ple_proj': 1.616174e-01, 'g_final': 1.282461e+02}


def _to_microbatches(a, axis):
    t = _jnp.moveaxis(a, axis, 0)
    t = t.reshape((N_MICROBATCH, t.shape[0] // N_MICROBATCH) + t.shape[1:])
    return _jnp.moveaxis(t, 1, axis + 1)


def setup_inputs(seed: int = 0) -> dict:
    inp = _fwd_setup_inputs(seed)
    key = _jax.random.fold_in(_jax.random.key(seed), 7919)
    shape, _ = _output_shape()
    out = dict(inp)
    out["loss_target"] = _jax.random.normal(_jax.random.fold_in(key, 0), shape, _jnp.float32)
    for i, name in enumerate(TWIN_WEIGHTS):
        w = inp[name].astype(_jnp.float32)
        if MOMENT_SCALE is None:
            s = _jnp.sqrt(_jnp.mean(_jnp.square(w)) + 1e-30)
        else:
            s = MOMENT_SCALE[name]
        km, kv = _jax.random.split(_jax.random.fold_in(key, i + 1))
        out[name] = w
        out["m_" + name] = s * _jax.random.normal(km, w.shape, _jnp.float32)
        out["v_" + name] = (s * s) * _jax.random.uniform(kv, w.shape, _jnp.float32, 0.5, 1.5)
    if N_MICROBATCH > 1:
        for name, axis in PER_EXAMPLE_BATCH_AXIS.items():
            out[name] = _to_microbatches(out[name], axis)
    return {'x': out['x'], 'p': out['p'], 'positions': out['positions'], 'g_mix': out['g_mix'], 'w_in': out['w_in'], 'a_re': out['a_re'], 'a_im': out['a_im'], 'log_dt': out['log_dt'], 'b_re': out['b_re'], 'b_im': out['b_im'], 'c_re': out['c_re'], 'c_im': out['c_im'], 'd_skip': out['d_skip'], 'w_attn_proj': out['w_attn_proj'], 'w_glu_a': out['w_glu_a'], 'w_glu_b': out['w_glu_b'], 'w_out': out['w_out'], 'g_ffn': out['g_ffn'], 'w_ffn_gate': out['w_ffn_gate'], 'w_ffn_up': out['w_ffn_up'], 'w_ffn_down': out['w_ffn_down'], 'w_ple_gate': out['w_ple_gate'], 'w_ple_proj': out['w_ple_proj'], 'g_final': out['g_final'], 'loss_target': out['loss_target'], 'm_g_mix': out['m_g_mix'], 'm_w_in': out['m_w_in'], 'm_a_re': out['m_a_re'], 'm_a_im': out['m_a_im'], 'm_log_dt': out['m_log_dt'], 'm_b_re': out['m_b_re'], 'm_b_im': out['m_b_im'], 'm_c_re': out['m_c_re'], 'm_c_im': out['m_c_im'], 'm_d_skip': out['m_d_skip'], 'm_w_attn_proj': out['m_w_attn_proj'], 'm_w_glu_a': out['m_w_glu_a'], 'm_w_glu_b': out['m_w_glu_b'], 'm_w_out': out['m_w_out'], 'm_g_ffn': out['m_g_ffn'], 'm_w_ffn_gate': out['m_w_ffn_gate'], 'm_w_ffn_up': out['m_w_ffn_up'], 'm_w_ffn_down': out['m_w_ffn_down'], 'm_w_ple_gate': out['m_w_ple_gate'], 'm_w_ple_proj': out['m_w_ple_proj'], 'm_g_final': out['m_g_final'], 'v_g_mix': out['v_g_mix'], 'v_w_in': out['v_w_in'], 'v_a_re': out['v_a_re'], 'v_a_im': out['v_a_im'], 'v_log_dt': out['v_log_dt'], 'v_b_re': out['v_b_re'], 'v_b_im': out['v_b_im'], 'v_c_re': out['v_c_re'], 'v_c_im': out['v_c_im'], 'v_d_skip': out['v_d_skip'], 'v_w_attn_proj': out['v_w_attn_proj'], 'v_w_glu_a': out['v_w_glu_a'], 'v_w_glu_b': out['v_w_glu_b'], 'v_w_out': out['v_w_out'], 'v_g_ffn': out['v_g_ffn'], 'v_w_ffn_gate': out['v_w_ffn_gate'], 'v_w_ffn_up': out['v_w_ffn_up'], 'v_w_ffn_down': out['v_w_ffn_down'], 'v_w_ple_gate': out['v_w_ple_gate'], 'v_w_ple_proj': out['v_w_ple_proj'], 'v_g_final': out['v_g_final']}


def _loss(weights, diff, rest, loss_target):
    with _jax.named_scope("forward"):
        args = {**rest, TWIN_DIFF_INPUT: diff, **{k: w.astype(_WEIGHT_DTYPES[k]) for k, w in weights.items()}}
        y = _forward(args)
    with _jax.named_scope("loss_head"):
        err = _jnp.square(y.astype(_jnp.float32) - loss_target)
        return 0.5 * _jnp.sum(_jnp.mean(err, axis=-1)) if err.ndim else 0.5 * err


def _adamw(w, g, m, v):
    m = ADAM_B1 * m + (1.0 - ADAM_B1) * g
    v = ADAM_B2 * v + (1.0 - ADAM_B2) * _jnp.square(g)
    m_hat = m / (1.0 - ADAM_B1 ** ADAM_STEP)
    v_hat = v / (1.0 - ADAM_B2 ** ADAM_STEP)
    delta = -ADAM_LR * (m_hat / (_jnp.sqrt(v_hat) + ADAM_EPS) + ADAM_WD * w)
    return delta, m, v


def reference(x, p, positions, g_mix, w_in, a_re, a_im, log_dt, b_re, b_im, c_re, c_im, d_skip, w_attn_proj, w_glu_a, w_glu_b, w_out, g_ffn, w_ffn_gate, w_ffn_up, w_ffn_down, w_ple_gate, w_ple_proj, g_final, loss_target, m_g_mix, m_w_in, m_a_re, m_a_im, m_log_dt, m_b_re, m_b_im, m_c_re, m_c_im, m_d_skip, m_w_attn_proj, m_w_glu_a, m_w_glu_b, m_w_out, m_g_ffn, m_w_ffn_gate, m_w_ffn_up, m_w_ffn_down, m_w_ple_gate, m_w_ple_proj, m_g_final, v_g_mix, v_w_in, v_a_re, v_a_im, v_log_dt, v_b_re, v_b_im, v_c_re, v_c_im, v_d_skip, v_w_attn_proj, v_w_glu_a, v_w_glu_b, v_w_out, v_g_ffn, v_w_ffn_gate, v_w_ffn_up, v_w_ffn_down, v_w_ple_gate, v_w_ple_proj, v_g_final):
    given = dict(x=x, p=p, positions=positions, g_mix=g_mix, w_in=w_in, a_re=a_re, a_im=a_im, log_dt=log_dt, b_re=b_re, b_im=b_im, c_re=c_re, c_im=c_im, d_skip=d_skip, w_attn_proj=w_attn_proj, w_glu_a=w_glu_a, w_glu_b=w_glu_b, w_out=w_out, g_ffn=g_ffn, w_ffn_gate=w_ffn_gate, w_ffn_up=w_ffn_up, w_ffn_down=w_ffn_down, w_ple_gate=w_ple_gate, w_ple_proj=w_ple_proj, g_final=g_final, loss_target=loss_target, m_g_mix=m_g_mix, m_w_in=m_w_in, m_a_re=m_a_re, m_a_im=m_a_im, m_log_dt=m_log_dt, m_b_re=m_b_re, m_b_im=m_b_im, m_c_re=m_c_re, m_c_im=m_c_im, m_d_skip=m_d_skip, m_w_attn_proj=m_w_attn_proj, m_w_glu_a=m_w_glu_a, m_w_glu_b=m_w_glu_b, m_w_out=m_w_out, m_g_ffn=m_g_ffn, m_w_ffn_gate=m_w_ffn_gate, m_w_ffn_up=m_w_ffn_up, m_w_ffn_down=m_w_ffn_down, m_w_ple_gate=m_w_ple_gate, m_w_ple_proj=m_w_ple_proj, m_g_final=m_g_final, v_g_mix=v_g_mix, v_w_in=v_w_in, v_a_re=v_a_re, v_a_im=v_a_im, v_log_dt=v_log_dt, v_b_re=v_b_re, v_b_im=v_b_im, v_c_re=v_c_re, v_c_im=v_c_im, v_d_skip=v_d_skip, v_w_attn_proj=v_w_attn_proj, v_w_glu_a=v_w_glu_a, v_w_glu_b=v_w_glu_b, v_w_out=v_w_out, v_g_ffn=v_g_ffn, v_w_ffn_gate=v_w_ffn_gate, v_w_ffn_up=v_w_ffn_up, v_w_ffn_down=v_w_ffn_down, v_w_ple_gate=v_w_ple_gate, v_w_ple_proj=v_w_ple_proj, v_g_final=v_g_final)
    weights = {n: given[n] for n in TWIN_WEIGHTS}
    shared = {n: given[n] for n in SHARED_INPUTS}
    per_example = {n: given[n] for n in ['x', 'p', 'positions']}
    grad_fn = _jax.value_and_grad(_loss, argnums=(0, 1))

    def one_microbatch(ex, loss_target):
        ex = dict(ex)
        diff = ex.pop(TWIN_DIFF_INPUT)
        return grad_fn(weights, diff, {**shared, **ex}, loss_target)

    if N_MICROBATCH == 1:
        loss, (grad_w, grad_x) = one_microbatch(per_example, given["loss_target"])
    else:
        def body(carry, xs):
            loss_sum, grad_sum = carry
            l_k, (gw_k, gx_k) = one_microbatch(xs[0], xs[1])
            with _jax.named_scope("update"):
                return (loss_sum + l_k, _jax.tree.map(_jnp.add, grad_sum, gw_k)), gx_k

        init = (_jnp.zeros((), _jnp.float32), _jax.tree.map(_jnp.zeros_like, weights))
        (loss, grad_w), grad_x = _jax.lax.scan(body, init, (per_example, given["loss_target"]))
    with _jax.named_scope("update"):
        delta_w, new_m, new_v = {}, {}, {}
        for n in TWIN_WEIGHTS:
            delta_w[n], new_m[n], new_v[n] = _adamw(weights[n], grad_w[n], given["m_" + n], given["v_" + n])
    return (loss, grad_x, *[grad_w[n] for n in TWIN_WEIGHTS], *[delta_w[n] for n in TWIN_WEIGHTS],
            *[new_m[n] for n in TWIN_WEIGHTS], *[new_v[n] for n in TWIN_WEIGHTS])
```

```python
import functools
import math

import jax
import jax.numpy as jnp
from jax import lax
from jax.experimental import pallas as pl
from jax.experimental.pallas import tpu as pltpu

F32 = jnp.float32
BF16 = jnp.bfloat16
SDS = jax.ShapeDtypeStruct

N_DEV = 8
HEAD_DIM = 128
HEADS_PER_GROUP = 4
GROUP_W = HEADS_PER_GROUP * HEAD_DIM
DILATIONS = (1, 4, 16)
N_GROUPS = len(DILATIONS)
QK_W = N_GROUPS * GROUP_W
BLK = 128
ROPE_THETA = 500000.0
ROPE_DIM = HEAD_DIM // 4
ROPE_HALF = ROPE_DIM // 2
SSM_W = 512
SSM_GROUP = 16
SSM_GROUPS = SSM_W // SSM_GROUP
SSM_STATE = 64
NSTATE = SSM_GROUPS * SSM_STATE
SSM_NB = 4
EPS = 1e-6
ADAM_LR, ADAM_B1, ADAM_B2, ADAM_EPS, ADAM_WD, ADAM_STEP = 0.001, 0.9, 0.999, 1e-08, 0.01, 10
NEG = -1e30

VMEM_LIMIT = 52 * 1024 * 1024
SCAN_ROWS = 256
SCAN_LANES = 512


def _cp(n):
    return pltpu.CompilerParams(dimension_semantics=("arbitrary",) * n, vmem_limit_bytes=VMEM_LIMIT)


def _sigmoid(x):
    return 1.0 / (1.0 + jnp.exp(-x))


_DNUMS = {"nn": (((1,), (0,)), ((), ())), "nt": (((1,), (1,)), ((), ())), "tn": (((0,), (0,)), ((), ()))}


def _mm(name, grid, mode, a, a_spec, b, b_spec, out_sds, out_spec, res=None, res_spec=None, out2_dtype=None):
    nk = grid[2]
    dn = _DNUMS[mode]
    n_in = 2 + (res is not None)
    n_out = 1 + (out2_dtype is not None)
    acc_shape = tuple(s for s in out_spec.block_shape if s is not None)

    def body(*refs):
        a_ref, b_ref = refs[0], refs[1]
        res_ref = refs[2] if res is not None else None
        o_ref = refs[n_in]
        o2_ref = refs[n_in + 1] if out2_dtype is not None else None
        p = lax.dot_general(a_ref[...].astype(BF16), b_ref[...].astype(BF16), dn, preferred_element_type=F32)

        def finish(r):
            if res_ref is not None:
                r = r + res_ref[...]
            o_ref[...] = r.astype(o_ref.dtype)
            if o2_ref is not None:
                o2_ref[...] = r.astype(o2_ref.dtype)

        if nk == 1:
            finish(p)
        else:
            acc_ref = refs[n_in + n_out]
            k = pl.program_id(2)

            @pl.when(k == 0)
            def _():
                acc_ref[...] = p

            @pl.when(k > 0)
            def _():
                acc_ref[...] += p

            @pl.when(k == nk - 1)
            def _():
                finish(acc_ref[...])

    ins = [a, b] + ([res] if res is not None else [])
    in_specs = [a_spec, b_spec] + ([res_spec] if res is not None else [])
    out_shape = out_sds if out2_dtype is None else (out_sds, SDS(out_sds.shape, out2_dtype))
    out_specs = out_spec if out2_dtype is None else (out_spec, out_spec)
    scratch = [pltpu.VMEM(acc_shape, F32)] if nk > 1 else []
    return pl.pallas_call(body, grid=grid, in_specs=in_specs, out_specs=out_specs, out_shape=out_shape,
                          scratch_shapes=scratch, compiler_params=_cp(3), name=name)(*ins)


def _bs(shape, fn):
    return pl.BlockSpec(shape, fn)


def _pick(n, cands):
    for c in cands:
        if n % c == 0:
            return c
    return n


def _my_index():
    return 4 * lax.axis_index("x") + 2 * lax.axis_index("y") + lax.axis_index("c")


def _peer(d):
    mx, my, mc = lax.axis_index("x"), lax.axis_index("y"), lax.axis_index("c")
    return (mx ^ ((d >> 2) & 1), my ^ ((d >> 1) & 1), mc ^ (d & 1))


def _win(ref, kind, j, n):
    if kind == "slot":
        return ref.at[j]
    if kind == "rows":
        return ref.at[pl.ds(pl.multiple_of(j * n, 8), n)]
    return ref.at[:, pl.ds(pl.multiple_of(j * n, 128), n)]


def _win7(ref, kind, n):
    if kind == "slot":
        return ref.at[pl.ds(0, 7)]
    if kind == "rows":
        return ref.at[pl.ds(0, 7 * n)]
    return ref.at[:, pl.ds(0, 7 * n)]


def _full_shape(shard_shape, kind):
    if kind == "slot":
        return (N_DEV,) + tuple(shard_shape)
    if kind == "rows":
        return (N_DEV * shard_shape[0],) + tuple(shard_shape[1:])
    return (shard_shape[0], N_DEV * shard_shape[1])


def _all_gather(shards, kinds):
    n = len(shards)
    sizes = [s.shape[0] if k == "rows" else s.shape[-1] for s, k in zip(shards, kinds)]

    def body(*refs):
        ins, outs = refs[:n], refs[n:2 * n]
        send_sems, recv_sems, local_sems = refs[2 * n:2 * n + 3]
        me = _my_index()
        my_id = (lax.axis_index("x"), lax.axis_index("y"), lax.axis_index("c"))
        local = []
        for a in range(n):
            cp = pltpu.make_async_copy(ins[a], _win(outs[a], kinds[a], me, sizes[a]), local_sems.at[a])
            cp.start()
            local.append(cp)
        for a in range(n):
            for d in range(1, N_DEV):
                pltpu.make_async_remote_copy(
                    src_ref=ins[a], dst_ref=_win(outs[a], kinds[a], me, sizes[a]), send_sem=send_sems.at[a],
                    recv_sem=recv_sems.at[a], device_id=_peer(d), device_id_type=pl.DeviceIdType.MESH).start()
        for a in range(n):
            seven = _win7(outs[a], kinds[a], sizes[a])
            pltpu.make_async_remote_copy(src_ref=seven, dst_ref=seven, send_sem=send_sems.at[a], recv_sem=recv_sems.at[a],
                                         device_id=my_id, device_id_type=pl.DeviceIdType.MESH).wait()
        for cp in local:
            cp.wait()

    any_spec = pl.BlockSpec(memory_space=pl.ANY)
    return pl.pallas_call(
        body, in_specs=[any_spec] * n, out_specs=[any_spec] * n,
        out_shape=[SDS(_full_shape(s.shape, k), s.dtype) for s, k in zip(shards, kinds)],
        scratch_shapes=[pltpu.SemaphoreType.DMA((n,)), pltpu.SemaphoreType.DMA((n,)), pltpu.SemaphoreType.DMA((n,))],
        name="all_gather_weights")(*shards)


def _exchange_grads(grads, kinds, sizes, small):
    n = len(grads)

    def shard_shape(g, kind, sz):
        if kind == "slot":
            return g.shape[1:]
        if kind == "rows":
            return (sz,) + g.shape[1:]
        return (g.shape[0], sz)

    def body(*refs):
        ins, small_in = refs[:n], refs[n]
        outs, small_out = refs[n + 1:2 * n + 1], refs[2 * n + 1]
        send_sems, recv_sems, local_sems = refs[2 * n + 2:2 * n + 5]
        me = _my_index()
        my_id = (lax.axis_index("x"), lax.axis_index("y"), lax.axis_index("c"))
        local = []
        for a in range(n):
            cp = pltpu.make_async_copy(_win(ins[a], kinds[a], me, sizes[a]), outs[a].at[me], local_sems.at[a])
            cp.start()
            local.append(cp)
        cp = pltpu.make_async_copy(small_in, small_out.at[me], local_sems.at[n])
        cp.start()
        local.append(cp)
        for a in range(n):
            for d in range(1, N_DEV):
                px, py, pc = _peer(d)
                pltpu.make_async_remote_copy(
                    src_ref=_win(ins[a], kinds[a], 4 * px + 2 * py + pc, sizes[a]), dst_ref=outs[a].at[me],
                    send_sem=send_sems.at[a], recv_sem=recv_sems.at[a], device_id=(px, py, pc),
                    device_id_type=pl.DeviceIdType.MESH).start()
        for d in range(1, N_DEV):
            pltpu.make_async_remote_copy(src_ref=small_in, dst_ref=small_out.at[me], send_sem=send_sems.at[n],
                                         recv_sem=recv_sems.at[n], device_id=_peer(d),
                                         device_id_type=pl.DeviceIdType.MESH).start()
        for a, out in enumerate(list(outs) + [small_out]):
            seven = out.at[pl.ds(0, 7)]
            pltpu.make_async_remote_copy(src_ref=seven, dst_ref=seven, send_sem=send_sems.at[a], recv_sem=recv_sems.at[a],
                                         device_id=my_id, device_id_type=pl.DeviceIdType.MESH).wait()
        for cp in local:
            cp.wait()

    any_spec = pl.BlockSpec(memory_space=pl.ANY)
    out_shape = [SDS((N_DEV,) + tuple(shard_shape(g, k, s)), F32) for g, k, s in zip(grads, kinds, sizes)]
    out_shape.append(SDS((N_DEV,) + small.shape, F32))
    return pl.pallas_call(
        body, in_specs=[any_spec] * (n + 1), out_specs=[any_spec] * (n + 1), out_shape=out_shape,
        scratch_shapes=[pltpu.SemaphoreType.DMA((n + 1,)), pltpu.SemaphoreType.DMA((n + 1,)),
                        pltpu.SemaphoreType.DMA((n + 1,))],
        name="exchange_grads")(*grads, small)


def _adamw(name, recv, w, m, v):
    rows, cols = w.shape
    tr = max(c for c in range(8, 257, 8) if rows % c == 0) if rows % 8 == 0 else rows

    def body(r_ref, w_ref, m_ref, v_ref, g_ref, d_ref, nm_ref, nv_ref):
        g = r_ref[0]
        for s in range(1, N_DEV):
            g = g + r_ref[s]
        nm = ADAM_B1 * m_ref[...] + (1.0 - ADAM_B1) * g
        nv = ADAM_B2 * v_ref[...] + (1.0 - ADAM_B2) * (g * g)
        m_hat = nm / (1.0 - ADAM_B1 ** ADAM_STEP)
        v_hat = nv / (1.0 - ADAM_B2 ** ADAM_STEP)
        g_ref[...] = g
        d_ref[...] = -ADAM_LR * (m_hat / (jnp.sqrt(v_hat) + ADAM_EPS) + ADAM_WD * w_ref[...])
        nm_ref[...] = nm
        nv_ref[...] = nv

    blk = _bs((tr, cols), lambda i: (i, 0))
    return pl.pallas_call(
        body, grid=(rows // tr,), in_specs=[_bs((N_DEV, tr, cols), lambda i: (0, i, 0)), blk, blk, blk],
        out_specs=[blk] * 4, out_shape=[SDS((rows, cols), F32)] * 4, compiler_params=_cp(1), name=name)(recv, w, m, v)


def _rms_fwd(name, x, g, tm):
    t, d = x.shape

    def body(x_ref, g_ref, n_ref):
        xv = x_ref[...]
        r = lax.rsqrt(jnp.mean(xv * xv, axis=-1, keepdims=True) + EPS)
        n_ref[...] = (xv * r * g_ref[...]).astype(BF16)

    return pl.pallas_call(body, grid=(t // tm,), in_specs=[_bs((tm, d), lambda i: (i, 0)), _bs((1, d), lambda i: (0, 0))],
                          out_specs=_bs((tm, d), lambda i: (i, 0)), out_shape=SDS((t, d), BF16), compiler_params=_cp(1),
                          name=name)(x, g)


def _rms_bwd(name, dy, x, g, dres, tm, want_bf16):
    t, d = x.shape

    def body(dy_ref, x_ref, g_ref, dres_ref, dx_ref, *rest):
        dg_ref = rest[-1]
        xv = x_ref[...]
        r = lax.rsqrt(jnp.mean(xv * xv, axis=-1, keepdims=True) + EPS)
        xh = xv * r
        dyv = dy_ref[...]
        dxh = dyv * g_ref[...]
        dx = dres_ref[...] + r * (dxh - xh * jnp.mean(dxh * xh, axis=-1, keepdims=True))
        dx_ref[...] = dx
        if want_bf16:
            rest[0][...] = dx.astype(BF16)
        part = jnp.sum(dyv * xh, axis=0, keepdims=True)

        @pl.when(pl.program_id(0) == 0)
        def _():
            dg_ref[...] = part

        @pl.when(pl.program_id(0) > 0)
        def _():
            dg_ref[...] += part

    row = _bs((tm, d), lambda i: (i, 0))
    vec = _bs((1, d), lambda i: (0, 0))
    out_specs = [row] + ([row] if want_bf16 else []) + [vec]
    out_shape = [SDS((t, d), F32)] + ([SDS((t, d), BF16)] if want_bf16 else []) + [SDS((1, d), F32)]
    return pl.pallas_call(body, grid=(t // tm,), in_specs=[row, row, vec, row], out_specs=out_specs, out_shape=out_shape,
                          compiler_params=_cp(1), name=name)(dy, x, g, dres)


def _strided(r, n, d):
    return pl.ds(r, n, stride=d) if d > 1 else pl.ds(0, n)


def _rope_tables(pos_ref, invf_ref, c_s, s1_s, s2_s, on):
    ang = pos_ref[...].astype(F32) * invf_ref[...]
    lane = lax.broadcasted_iota(jnp.int32, ang.shape, 1)
    sn = jnp.sin(ang)
    c_s[...] = jnp.where((lane < ROPE_DIM) & on, jnp.cos(ang), 1.0)
    s1_s[...] = jnp.where((lane < ROPE_HALF) & on, -sn, 0.0)
    s2_s[...] = jnp.where((lane >= ROPE_HALF) & (lane < ROPE_DIM) & on, sn, 0.0)


def _rope_dilate(z, pos, invf, tm):
    t = z.shape[0]

    def body(z_ref, pos_ref, invf_ref, o0, o1, o2, c_s, s1_s, s2_s, rot):
        _rope_tables(pos_ref, invf_ref, c_s, s1_s, s2_s, pl.program_id(1) < 2)
        cc, s1, s2 = c_s[...], s1_s[...], s2_s[...]
        for h in range(QK_W // HEAD_DIM):
            xv = z_ref[:, h * HEAD_DIM:(h + 1) * HEAD_DIM]
            rot[h] = xv * cc + pltpu.roll(xv, HEAD_DIM - ROPE_HALF, 1) * s1 + pltpu.roll(xv, ROPE_HALF, 1) * s2
        for g, (d, o_ref) in enumerate(zip(DILATIONS, (o0, o1, o2))):
            n = tm // d
            for r in range(d):
                for hh in range(HEADS_PER_GROUP):
                    oc = r * GROUP_W + hh * HEAD_DIM
                    o_ref[:, oc:oc + HEAD_DIM] = rot[g * HEADS_PER_GROUP + hh, _strided(r, n, d), :].astype(BF16)

    return pl.pallas_call(
        body, grid=(t // tm, 3),
        in_specs=[_bs((tm, QK_W), lambda i, c: (i, c)), _bs((tm, 1), lambda i, c: (i, 0)), _bs((1, HEAD_DIM), lambda i, c: (0, 0))],
        out_specs=[_bs((None, tm // d, d * GROUP_W), lambda i, c: (c, i, 0)) for d in DILATIONS],
        out_shape=[SDS((3, t // d, d * GROUP_W), BF16) for d in DILATIONS],
        scratch_shapes=[pltpu.VMEM((tm, HEAD_DIM), F32)] * 3 + [pltpu.VMEM((QK_W // HEAD_DIM, tm, HEAD_DIM), F32)],
        compiler_params=_cp(2), name="rope_dilate")(z, pos, invf)


def _band_mask(first):
    qi = lax.broadcasted_iota(jnp.int32, (BLK, 2 * BLK), 0)
    kj = lax.broadcasted_iota(jnp.int32, (BLK, 2 * BLK), 1)
    return (kj >= qi) & (kj <= qi + BLK) & ((kj >= BLK) | jnp.logical_not(first))


def _attn_fwd(qkv, d, qt):
    ell = qkv.shape[1]
    nsub = qt // BLK
    scale = 1.0 / math.sqrt(HEAD_DIM)

    def body(q_ref, kc_ref, kp_ref, vc_ref, vp_ref, o_ref, lse_ref, kcat, vcat):
        nb = pl.program_id(1)
        kcat[0:BLK, :] = kp_ref[...]
        kcat[BLK:, :] = kc_ref[...]
        vcat[0:BLK, :] = vp_ref[...]
        vcat[BLK:, :] = vc_ref[...]
        lane = lax.broadcasted_iota(jnp.int32, (BLK, HEAD_DIM), 1)
        for b in range(nsub):
            valid = _band_mask((nb == 0) if b == 0 else False)
            lse_t = jnp.zeros((BLK, HEAD_DIM), F32)
            for hh in range(HEADS_PER_GROUP):
                cs = slice(hh * HEAD_DIM, (hh + 1) * HEAD_DIM)
                qb = q_ref[b * BLK:(b + 1) * BLK, cs]
                kk = kcat[b * BLK:(b + 2) * BLK, cs]
                vv = vcat[b * BLK:(b + 2) * BLK, cs]
                s = lax.dot_general(qb, kk, _DNUMS["nt"], preferred_element_type=F32) * scale
                s = jnp.where(valid, s, NEG)
                mx = jnp.max(s, axis=-1, keepdims=True)
                p = jnp.exp(s - mx)
                den = jnp.sum(p, axis=-1, keepdims=True)
                o = jnp.dot(p.astype(BF16), vv, preferred_element_type=F32) / den
                o_ref[b * BLK:(b + 1) * BLK, cs] = o
                lse_t = jnp.where(lane == hh, mx + jnp.log(den), lse_t)
            lse_ref[b * BLK:(b + 1) * BLK, :] = lse_t

    cur = lambda c: _bs((None, qt, GROUP_W), lambda r, nb: (c, nb, r))
    prev = lambda c: _bs((None, BLK, GROUP_W), lambda r, nb: (c, jnp.maximum(nb * nsub - 1, 0), r))
    return pl.pallas_call(
        body, grid=(d, ell // qt), in_specs=[cur(0), cur(1), prev(1), cur(2), prev(2)],
        out_specs=[_bs((qt, GROUP_W), lambda r, nb: (nb, r)), _bs((None, qt, HEAD_DIM), lambda r, nb: (r, nb, 0))],
        out_shape=[SDS((ell, d * GROUP_W), F32), SDS((d, ell, HEAD_DIM), F32)],
        scratch_shapes=[pltpu.VMEM((qt + BLK, GROUP_W), BF16)] * 2, compiler_params=_cp(2), name=f"attn_fwd_d{d}")(
            qkv, qkv, qkv, qkv, qkv)


def _attn_merge(outs, lses, tm):
    t = outs[0].shape[0]

    def body(o0, o1, o2, l0, l1, l2, attn_ref, attn_bf_ref, t0, t1, t2, so, sl, lt_s):
        for g, (d, o_ref, l_ref) in enumerate(zip(DILATIONS, (o0, o1, o2), (l0, l1, l2))):
            n = tm // d
            for r in range(d):
                rows = _strided(r, n, d)
                for hh in range(HEADS_PER_GROUP):
                    oc = r * GROUP_W + hh * HEAD_DIM
                    so[g * HEADS_PER_GROUP + hh, rows, :] = o_ref[:, oc:oc + HEAD_DIM]
                sl[g, rows, :] = l_ref[r]
        ls = [sl[g] for g in range(N_GROUPS)]
        mx = jnp.maximum(jnp.maximum(ls[0], ls[1]), ls[2])
        es = [jnp.exp(l - mx) for l in ls]
        den = es[0] + es[1] + es[2]
        ws = [e / den for e in es]
        lt_s[...] = mx + jnp.log(den)
        for hh in range(HEADS_PER_GROUP):
            cs = slice(hh * HEAD_DIM, (hh + 1) * HEAD_DIM)
            a = ws[0][:, hh:hh + 1] * so[hh]
            for g in range(1, N_GROUPS):
                a = a + ws[g][:, hh:hh + 1] * so[g * HEADS_PER_GROUP + hh]
            attn_ref[:, cs] = a
            attn_bf_ref[:, cs] = a.astype(BF16)
        for d, t_ref in zip(DILATIONS, (t0, t1, t2)):
            n = tm // d
            for r in range(d):
                t_ref[r] = lt_s[_strided(r, n, d), :]

    dil = lambda d: _bs((tm // d, d * GROUP_W), lambda i: (i, 0))
    lsp = lambda d: _bs((d, tm // d, HEAD_DIM), lambda i: (0, i, 0))
    row = _bs((tm, GROUP_W), lambda i: (i, 0))
    return pl.pallas_call(
        body, grid=(t // tm,),
        in_specs=[dil(d) for d in DILATIONS] + [lsp(d) for d in DILATIONS],
        out_specs=[row, row] + [lsp(d) for d in DILATIONS],
        out_shape=[SDS((t, GROUP_W), F32), SDS((t, GROUP_W), BF16)] + [SDS(l.shape, F32) for l in lses],
        scratch_shapes=[pltpu.VMEM((N_GROUPS * HEADS_PER_GROUP, tm, HEAD_DIM), F32), pltpu.VMEM((N_GROUPS, tm, HEAD_DIM), F32),
                        pltpu.VMEM((tm, HEAD_DIM), F32)],
        compiler_params=_cp(1), name="attn_merge")(*outs, *lses)


def _attn_bwd_pre(d_attn, attn, tm):
    t = attn.shape[0]

    def body(da_ref, a_ref, g0, g1, g2, e0, e1, e2, dl_s, da_s):
        lane = lax.broadcasted_iota(jnp.int32, (tm, HEAD_DIM), 1)
        dl = jnp.zeros((tm, HEAD_DIM), F32)
        for hh in range(HEADS_PER_GROUP):
            cs = slice(hh * HEAD_DIM, (hh + 1) * HEAD_DIM)
            dav = da_ref[:, cs]
            da_s[hh] = dav
            dl = jnp.where(lane == hh, jnp.sum(dav * a_ref[:, cs], axis=-1, keepdims=True), dl)
        dl_s[...] = dl
        for d, g_ref, e_ref in zip(DILATIONS, (g0, g1, g2), (e0, e1, e2)):
            n = tm // d
            for r in range(d):
                rows = _strided(r, n, d)
                for hh in range(HEADS_PER_GROUP):
                    oc = r * GROUP_W + hh * HEAD_DIM
                    g_ref[:, oc:oc + HEAD_DIM] = da_s[hh, rows, :].astype(BF16)
                e_ref[r] = dl_s[rows, :]

    row = _bs((tm, GROUP_W), lambda i: (i, 0))
    return pl.pallas_call(
        body, grid=(t // tm,), in_specs=[row, row],
        out_specs=[_bs((tm // d, d * GROUP_W), lambda i: (i, 0)) for d in DILATIONS]
        + [_bs((d, tm // d, HEAD_DIM), lambda i: (0, i, 0)) for d in DILATIONS],
        out_shape=[SDS((t // d, d * GROUP_W), BF16) for d in DILATIONS]
        + [SDS((d, t // d, HEAD_DIM), F32) for d in DILATIONS],
        scratch_shapes=[pltpu.VMEM((tm, HEAD_DIM), F32), pltpu.VMEM((HEADS_PER_GROUP, tm, HEAD_DIM), F32)],
        compiler_params=_cp(1), name="attn_bwd_pre")(d_attn, attn)


def _attn_bwd(qkv, d_a, lt, delta, d, qt):
    ell = qkv.shape[1]
    nsub = qt // BLK
    ntile = ell // qt
    nblk = ell // BLK
    scale = 1.0 / math.sqrt(HEAD_DIM)

    def body(q_ref, qn_ref, kc_ref, kp_ref, vc_ref, vp_ref, da_ref, dan_ref, lt_ref, ltn_ref, dl_ref, dln_ref, o_ref,
             kcat, vcat, dk_acc, dv_acc):
        nb = pl.program_id(1)
        kcat[0:BLK, :] = kp_ref[...]
        kcat[BLK:, :] = kc_ref[...]
        vcat[0:BLK, :] = vp_ref[...]
        vcat[BLK:, :] = vc_ref[...]
        qi = lax.broadcasted_iota(jnp.int32, (BLK, BLK), 0)
        kj = lax.broadcasted_iota(jnp.int32, (BLK, BLK), 1)
        valid_next = (kj >= qi) & (nb < ntile - 1)
        for hh in range(HEADS_PER_GROUP):
            cs = slice(hh * HEAD_DIM, (hh + 1) * HEAD_DIM)
            dk_acc[...] = jnp.zeros_like(dk_acc)
            dv_acc[...] = jnp.zeros_like(dv_acc)
            for b in range(nsub):
                rs = slice(b * BLK, (b + 1) * BLK)
                ks = slice(b * BLK, (b + 2) * BLK)
                valid = _band_mask((nb == 0) if b == 0 else False)
                qb, kk, vv, dab = q_ref[rs, cs], kcat[ks, cs], vcat[ks, cs], da_ref[rs, cs]
                s = lax.dot_general(qb, kk, _DNUMS["nt"], preferred_element_type=F32) * scale
                p = jnp.where(valid, jnp.exp(s - lt_ref[rs, hh:hh + 1]), 0.0)
                dp = lax.dot_general(dab, vv, _DNUMS["nt"], preferred_element_type=F32)
                ds = (p * (dp - dl_ref[rs, hh:hh + 1])).astype(BF16)
                o_ref[0, rs, cs] = jnp.dot(ds, kk, preferred_element_type=F32) * scale
                dk_acc[ks, :] += lax.dot_general(ds, qb, _DNUMS["tn"], preferred_element_type=F32) * scale
                dv_acc[ks, :] += lax.dot_general(p.astype(BF16), dab, _DNUMS["tn"], preferred_element_type=F32)
            ks = slice(nsub * BLK, (nsub + 1) * BLK)
            qn, kl, vl, dan = qn_ref[:, cs], kcat[ks, cs], vcat[ks, cs], dan_ref[:, cs]
            s = lax.dot_general(qn, kl, _DNUMS["nt"], preferred_element_type=F32) * scale
            p = jnp.where(valid_next, jnp.exp(s - ltn_ref[:, hh:hh + 1]), 0.0)
            dp = lax.dot_general(dan, vl, _DNUMS["nt"], preferred_element_type=F32)
            ds = (p * (dp - dln_ref[:, hh:hh + 1])).astype(BF16)
            dk_acc[ks, :] += lax.dot_general(ds, qn, _DNUMS["tn"], preferred_element_type=F32) * scale
            dv_acc[ks, :] += lax.dot_general(p.astype(BF16), dan, _DNUMS["tn"], preferred_element_type=F32)
            o_ref[1, :, cs] = dk_acc[BLK:, :]
            o_ref[2, :, cs] = dv_acc[BLK:, :]

    nxt = lambda nb: jnp.minimum((nb + 1) * nsub, nblk - 1)
    prv = lambda nb: jnp.maximum(nb * nsub - 1, 0)
    cur3 = lambda c: _bs((None, qt, GROUP_W), lambda r, nb: (c, nb, r))
    in_specs = [
        cur3(0), _bs((None, BLK, GROUP_W), lambda r, nb: (0, nxt(nb), r)),
        cur3(1), _bs((None, BLK, GROUP_W), lambda r, nb: (1, prv(nb), r)),
        cur3(2), _bs((None, BLK, GROUP_W), lambda r, nb: (2, prv(nb), r)),
        _bs((qt, GROUP_W), lambda r, nb: (nb, r)), _bs((BLK, GROUP_W), lambda r, nb: (nxt(nb), r)),
        _bs((None, qt, HEAD_DIM), lambda r, nb: (r, nb, 0)), _bs((None, BLK, HEAD_DIM), lambda r, nb: (r, nxt(nb), 0)),
        _bs((None, qt, HEAD_DIM), lambda r, nb: (r, nb, 0)), _bs((None, BLK, HEAD_DIM), lambda r, nb: (r, nxt(nb), 0)),
    ]
    return pl.pallas_call(
        body, grid=(d, ntile), in_specs=in_specs, out_specs=_bs((3, qt, GROUP_W), lambda r, nb: (0, nb, r)),
        out_shape=SDS((3, ell, d * GROUP_W), F32),
        scratch_shapes=[pltpu.VMEM((qt + BLK, GROUP_W), BF16)] * 2 + [pltpu.VMEM((qt + BLK, HEAD_DIM), F32)] * 2,
        compiler_params=_cp(2), name=f"attn_bwd_d{d}")(qkv, qkv, qkv, qkv, qkv, qkv, d_a, d_a, lt, lt, delta, delta)


def _undilate_rope_bwd(dqkvs, pos, invf, dz, tm):
    t = dz.shape[0]

    def body(g0, g1, g2, pos_ref, invf_ref, dz_in, o_ref, c_s, s1_s, s2_s, nat):
        del dz_in
        _rope_tables(pos_ref, invf_ref, c_s, s1_s, s2_s, pl.program_id(1) < 2)
        for g, (d, g_ref) in enumerate(zip(DILATIONS, (g0, g1, g2))):
            n = tm // d
            for r in range(d):
                for hh in range(HEADS_PER_GROUP):
                    oc = r * GROUP_W + hh * HEAD_DIM
                    nat[g * HEADS_PER_GROUP + hh, _strided(r, n, d), :] = g_ref[:, oc:oc + HEAD_DIM]
        cc, s1, s2 = c_s[...], s1_s[...], s2_s[...]
        for h in range(QK_W // HEAD_DIM):
            xv = nat[h]
            y = xv * cc - pltpu.roll(xv, HEAD_DIM - ROPE_HALF, 1) * s1 - pltpu.roll(xv, ROPE_HALF, 1) * s2
            o_ref[:, h * HEAD_DIM:(h + 1) * HEAD_DIM] = y.astype(BF16)

    return pl.pallas_call(
        body, grid=(t // tm, 3),
        in_specs=[_bs((None, tm // d, d * GROUP_W), lambda i, c: (c, i, 0)) for d in DILATIONS]
        + [_bs((tm, 1), lambda i, c: (i, 0)), _bs((1, HEAD_DIM), lambda i, c: (0, 0)), pl.BlockSpec(memory_space=pl.ANY)],
        out_specs=_bs((tm, QK_W), lambda i, c: (i, c)), out_shape=SDS(dz.shape, BF16),
        scratch_shapes=[pltpu.VMEM((tm, HEAD_DIM), F32)] * 3 + [pltpu.VMEM((QK_W // HEAD_DIM, tm, HEAD_DIM), F32)],
        input_output_aliases={5: 0}, compiler_params=_cp(2), name="undilate_rope_bwd")(*dqkvs, pos, invf, dz)


def _cmul(ar, ai, br, bi):
    return ar * br - ai * bi, ar * bi + ai * br


def _ssm_disc(a_re, a_im, log_dt, nsq):
    def body(lr_ref, li_ref, ldt_ref, br_ref, bi_ref, zr_ref, zi_ref, pr_ref, pi_ref):
        lr, li = lr_ref[...], li_ref[...]
        dt = jnp.exp(ldt_ref[...])
        mag = jnp.exp(lr * dt)
        bar_re, bar_im = mag * jnp.cos(li * dt), mag * jnp.sin(li * dt)
        nr, ni = bar_re - 1.0, bar_im
        den = lr * lr + li * li
        br_ref[...], bi_ref[...] = bar_re, bar_im
        zr_ref[...] = (nr * lr + ni * li) / den
        zi_ref[...] = (ni * lr - nr * li) / den
        pr, pi = bar_re, bar_im
        for _ in range(nsq):
            pr, pi = _cmul(pr, pi, pr, pi)
        pr_ref[...], pi_ref[...] = pr, pi

    return pl.pallas_call(body, out_shape=[SDS(a_re.shape, F32)] * 6, name="ssm_discretise")(a_re, a_im, log_dt)


def _ssm_scale_b(z_re, z_im, b_re, b_im):
    def body(zr_ref, zi_ref, br_ref, bi_ref, or_ref, oi_ref):
        zr, zi, br, bi = zr_ref[...], zi_ref[...], br_ref[...], bi_ref[...]
        or_ref[...] = zr * br - zi * bi
        oi_ref[...] = zr * bi + zi * br

    return pl.pallas_call(body, out_shape=[SDS(b_re.shape, F32)] * 2, name="ssm_scale_b")(z_re, z_im, b_re, b_im)


def _ssm_scale_b_bwd(z_re, z_im, b_re, b_im, g_re, g_im):
    def body(zr_ref, zi_ref, br_ref, bi_ref, gr_ref, gi_ref, dbr_ref, dbi_ref, dzr_ref, dzi_ref):
        zr, zi, br, bi, gr, gi = zr_ref[...], zi_ref[...], br_ref[...], bi_ref[...], gr_ref[...], gi_ref[...]
        dbr_ref[...] = zr * gr + zi * gi
        dbi_ref[...] = zr * gi - zi * gr
        dzr_ref[...] = jnp.sum(br * gr + bi * gi, axis=-1, keepdims=True)
        dzi_ref[...] = jnp.sum(br * gi - bi * gr, axis=-1, keepdims=True)

    return pl.pallas_call(body, out_shape=[SDS(b_re.shape, F32)] * 2 + [SDS(z_re.shape, F32)] * 2,
                          name="ssm_scale_b_bwd")(z_re, z_im, b_re, b_im, g_re, g_im)


def _ssm_disc_bwd(a_re, a_im, log_dt, gb_re, gb_im, gz_re, gz_im):
    def body(lr_ref, li_ref, ldt_ref, gbr_ref, gbi_ref, gzr_ref, gzi_ref, dar_ref, dai_ref, dldt_ref):
        lr, li = lr_ref[...], li_ref[...]
        dt = jnp.exp(ldt_ref[...])
        mag = jnp.exp(lr * dt)
        bar_re, bar_im = mag * jnp.cos(li * dt), mag * jnp.sin(li * dt)
        nr, ni = bar_re - 1.0, bar_im
        den = lr * lr + li * li
        zr, zi = (nr * lr + ni * li) / den, (ni * lr - nr * li) / den
        gzr, gzi = gzr_ref[...], gzi_ref[...]
        gbr = gbr_ref[...] + (lr * gzr - li * gzi) / den
        gbi = gbi_ref[...] + (lr * gzi + li * gzr) / den
        qr, qi = (zr * lr + zi * li) / den, (zi * lr - zr * li) / den
        dar_ref[...] = dt * (bar_re * gbr + bar_im * gbi) - qr * gzr - qi * gzi
        dai_ref[...] = dt * (bar_re * gbi - bar_im * gbr) - qr * gzi + qi * gzr
        wr, wi = lr * bar_re - li * bar_im, lr * bar_im + li * bar_re
        dldt_ref[...] = dt * jnp.sum(wr * gbr + wi * gbi, axis=-1, keepdims=True)

    return pl.pallas_call(body, out_shape=[SDS(a_re.shape, F32)] * 2 + [SDS(log_dt.shape, F32)],
                          name="ssm_discretise_bwd")(a_re, a_im, log_dt, gb_re, gb_im, gz_re, gz_im)


def _permute_u(z, ucol_block, tm):
    t = z.shape[0]
    seg = t // N_DEV
    z3 = z.reshape(N_DEV, seg, z.shape[1])

    def body(z_ref, u_ref, ub_ref, tmp):
        for n in range(SSM_W // BLK):
            for j in range(N_DEV):
                tmp[n, pl.ds(j, tm // N_DEV, stride=N_DEV), :] = z_ref[j, :, n * BLK:(n + 1) * BLK]
            u_ref[:, n * BLK:(n + 1) * BLK] = tmp[n]
            ub_ref[:, n * BLK:(n + 1) * BLK] = tmp[n].astype(BF16)

    row = _bs((tm, SSM_W), lambda i: (i, 0))
    return pl.pallas_call(
        body, grid=(t // tm,), in_specs=[_bs((N_DEV, tm // N_DEV, SSM_W), lambda i: (0, i, ucol_block))],
        out_specs=[row, row], out_shape=[SDS((t, SSM_W), F32), SDS((t, SSM_W), BF16)],
        scratch_shapes=[pltpu.VMEM((SSM_W // BLK, tm, BLK), F32)], compiler_params=_cp(1), name="permute_u")(z3)


def _drive(src_ref, mat_ref, dst, mode):
    for kn in range(2 * SSM_NB):
        n = kn % SSM_NB
        a = src_ref[:, n * BLK:(n + 1) * BLK]
        dst[:, kn * 512:(kn + 1) * 512] = lax.dot_general(a, mat_ref[kn], _DNUMS[mode], preferred_element_type=F32)


def _scan_chunk(src, lam_ref, carry, *, reverse, store=None, h_ref=None, acc=None):
    steps = src.shape[0] // 8
    for c in range(NSTATE // SCAN_LANES):
        re = slice(c * SCAN_LANES, (c + 1) * SCAN_LANES)
        im = slice(NSTATE + c * SCAN_LANES, NSTATE + (c + 1) * SCAN_LANES)
        ar, ai = lam_ref[:, re], lam_ref[:, im]

        def step(s, val):
            i = (steps - 1 - s) if reverse else s
            rows = pl.ds(pl.multiple_of(i * 8, 8), 8)
            if acc is not None:
                hr, hi, dr, di = val
                pr, pi = h_ref[rows, re], h_ref[rows, im]
                dr = dr + hr * pr + hi * pi
                di = di + hi * pr - hr * pi
            else:
                hr, hi = val
            nr = ar * hr - ai * hi + src[rows, re]
            ni = ar * hi + ai * hr + src[rows, im]
            if store is not None:
                store[rows, re] = nr
                store[rows, im] = ni
            return (nr, ni, dr, di) if acc is not None else (nr, ni)

        init = (carry[:, re], carry[:, im])
        if acc is not None:
            init = init + (acc[:, re], acc[:, im])
        out = lax.fori_loop(0, steps, step, init, unroll=4)
        carry[:, re], carry[:, im] = out[0], out[1]
        if acc is not None:
            acc[:, re], acc[:, im] = out[2], out[3]


def _segment_carries(e_ref, pw_ref, out_ref, reverse):
    pr, pi = pw_ref[:, 0:NSTATE], pw_ref[:, NSTATE:]
    hr = jnp.zeros((1, NSTATE), F32)
    hi = jnp.zeros((1, NSTATE), F32)
    order = range(N_DEV - 1, -1, -1) if reverse else range(N_DEV)
    for j in order:
        out_ref[j:j + 1, 0:NSTATE] = hr
        out_ref[j:j + 1, NSTATE:] = hi
        tr, ti = _cmul(pr, pi, hr, hi)
        hr, hi = e_ref[j:j + 1, 0:NSTATE] + tr, e_ref[j:j + 1, NSTATE:] + ti


def _ssm_carries(name, src, mat, mode, lam8, pw, reverse):
    t = src.shape[0]
    nchunk = t // SCAN_ROWS

    def body(src_ref, mat_ref, lam_ref, pw_ref, out_ref, drive, carry):
        c = pl.program_id(0)

        @pl.when(c == 0)
        def _():
            carry[...] = jnp.zeros_like(carry)

        _drive(src_ref, mat_ref, drive, mode)
        _scan_chunk(drive, lam_ref, carry, reverse=reverse)

        @pl.when(c == nchunk - 1)
        def _():
            _segment_carries(carry, pw_ref, out_ref, reverse)

    blk = (lambda c: (nchunk - 1 - c, 0)) if reverse else (lambda c: (c, 0))
    return pl.pallas_call(
        body, grid=(nchunk,),
        in_specs=[_bs((SCAN_ROWS, SSM_W), blk), _bs(mat.shape, lambda c: (0, 0, 0)), _bs((8, 2 * NSTATE), lambda c: (0, 0)),
                  _bs((1, 2 * NSTATE), lambda c: (0, 0))],
        out_specs=_bs((8, 2 * NSTATE), lambda c: (0, 0)), out_shape=SDS((8, 2 * NSTATE), F32),
        scratch_shapes=[pltpu.VMEM((SCAN_ROWS, 2 * NSTATE), F32), pltpu.VMEM((8, 2 * NSTATE), F32)],
        compiler_params=_cp(1), name=name)(src, mat, lam8, pw)


def _ssm_fwd(u_bf, bd, cd, lam8, start):
    t = u_bf.shape[0]
    nchunk = t // SCAN_ROWS

    def body(u_ref, bd_ref, cd_ref, lam_ref, start_ref, h_ref, y_ref, drive, carry):
        @pl.when(pl.program_id(0) == 0)
        def _():
            carry[...] = start_ref[...]

        _drive(u_ref, bd_ref, drive, "nn")
        _scan_chunk(drive, lam_ref, carry, reverse=False, store=h_ref)
        for n in range(SSM_NB):
            hr = h_ref[:, n * 512:(n + 1) * 512].astype(BF16)
            hi = h_ref[:, NSTATE + n * 512:NSTATE + (n + 1) * 512].astype(BF16)
            y_ref[:, n * BLK:(n + 1) * BLK] = (jnp.dot(hr, cd_ref[n], preferred_element_type=F32)
                                              + jnp.dot(hi, cd_ref[SSM_NB + n], preferred_element_type=F32))

    return pl.pallas_call(
        body, grid=(nchunk,),
        in_specs=[_bs((SCAN_ROWS, SSM_W), lambda c: (c, 0)), _bs(bd.shape, lambda c: (0, 0, 0)), _bs(cd.shape, lambda c: (0, 0, 0)),
                  _bs((8, 2 * NSTATE), lambda c: (0, 0)), _bs((8, 2 * NSTATE), lambda c: (0, 0))],
        out_specs=[_bs((SCAN_ROWS, 2 * NSTATE), lambda c: (c, 0)), _bs((SCAN_ROWS, SSM_W), lambda c: (c, 0))],
        out_shape=[SDS((t, 2 * NSTATE), F32), SDS((t, SSM_W), F32)],
        scratch_shapes=[pltpu.VMEM((SCAN_ROWS, 2 * NSTATE), F32), pltpu.VMEM((8, 2 * NSTATE), F32)],
        compiler_params=_cp(1), name="ssm_scan_fwd")(u_bf, bd, cd, lam8, start)


def _ssm_bwd(dys_bf, u_bf, h, bd, cd, lamc8, start):
    t = u_bf.shape[0]
    nchunk = t // SCAN_ROWS

    def body(dys_ref, u_ref, h_ref, bd_ref, cd_ref, lam_ref, start_ref, du_ref, dlam_ref, dbd_ref, dcd_ref, drive, adj, carry):
        c = pl.program_id(0)

        @pl.when(c == 0)
        def _():
            carry[...] = start_ref[...]
            dlam_ref[...] = jnp.zeros_like(dlam_ref)
            dbd_ref[...] = jnp.zeros_like(dbd_ref)
            dcd_ref[...] = jnp.zeros_like(dcd_ref)

        _drive(dys_ref, cd_ref, drive, "nt")
        _scan_chunk(drive, lam_ref, carry, reverse=True, store=adj, h_ref=h_ref, acc=dlam_ref)
        for n in range(SSM_NB):
            cs = slice(n * BLK, (n + 1) * BLK)
            acc = None
            for k in range(2):
                kn = k * SSM_NB + n
                ss = slice(kn * 512, (kn + 1) * 512)
                lam_b = adj[:, ss].astype(BF16)
                part = lax.dot_general(lam_b, bd_ref[kn], _DNUMS["nt"], preferred_element_type=F32)
                acc = part if acc is None else acc + part
                dbd_ref[kn] += lax.dot_general(u_ref[:, cs], lam_b, _DNUMS["tn"], preferred_element_type=F32)
                dcd_ref[kn] += lax.dot_general(h_ref[:, ss].astype(BF16), dys_ref[:, cs], _DNUMS["tn"],
                                               preferred_element_type=F32)
            du_ref[:, cs] = acc

    rev = lambda c: (nchunk - 1 - c, 0)
    const2 = lambda c: (0, 0)
    const3 = lambda c: (0, 0, 0)
    return pl.pallas_call(
        body, grid=(nchunk,),
        in_specs=[_bs((SCAN_ROWS, SSM_W), rev), _bs((SCAN_ROWS, SSM_W), rev), _bs((SCAN_ROWS, 2 * NSTATE), rev),
                  _bs(bd.shape, const3), _bs(cd.shape, const3), _bs((8, 2 * NSTATE), const2), _bs((8, 2 * NSTATE), const2)],
        out_specs=[_bs((SCAN_ROWS, SSM_W), rev), _bs((8, 2 * NSTATE), const2), _bs(bd.shape, const3), _bs(cd.shape, const3)],
        out_shape=[SDS((t, SSM_W), F32), SDS((8, 2 * NSTATE), F32), SDS(bd.shape, F32), SDS(cd.shape, F32)],
        scratch_shapes=[pltpu.VMEM((SCAN_ROWS, 2 * NSTATE), F32), pltpu.VMEM((SCAN_ROWS, 2 * NSTATE), F32),
                        pltpu.VMEM((8, 2 * NSTATE), F32)],
        compiler_params=_cp(1), name="ssm_scan_bwd")(dys_bf, u_bf, h, bd, cd, lamc8, start)


def _gelu_parts(x):
    c0 = math.sqrt(2.0 / math.pi)
    inner = c0 * (x + 0.044715 * x * x * x)
    th = jnp.tanh(inner)
    val = 0.5 * x * (1.0 + th)
    grad = 0.5 * (1.0 + th) + 0.5 * x * (1.0 - th * th) * c0 * (1.0 + 3.0 * 0.044715 * x * x)
    return val, grad


def _ssm_out(y_raw, u, d_skip, tm):
    t = u.shape[0]
    seg = t // N_DEV

    def body(y_ref, u_ref, d_ref, ys_ref, yg_ref, tmp):
        ys = y_ref[...] + d_ref[...] * u_ref[...]
        ys_ref[...] = ys
        yg = _gelu_parts(ys)[0]
        for n in range(SSM_W // BLK):
            tmp[n] = yg[:, n * BLK:(n + 1) * BLK]
            for j in range(N_DEV):
                yg_ref[j, :, n * BLK:(n + 1) * BLK] = tmp[n, pl.ds(j, tm // N_DEV, stride=N_DEV), :].astype(BF16)

    row = _bs((tm, SSM_W), lambda i: (i, 0))
    ys, yg = pl.pallas_call(
        body, grid=(t // tm,), in_specs=[row, row, _bs((1, SSM_W), lambda i: (0, 0))],
        out_specs=[row, _bs((N_DEV, tm // N_DEV, SSM_W), lambda i: (0, i, 0))],
        out_shape=[SDS((t, SSM_W), F32), SDS((N_DEV, seg, SSM_W), BF16)],
        scratch_shapes=[pltpu.VMEM((SSM_W // BLK, tm, BLK), F32)], compiler_params=_cp(1), name="ssm_out")(y_raw, u, d_skip)
    return ys, yg.reshape(t, SSM_W)


def _ssm_out_bwd(d_yg, ys, u, tm):
    t = u.shape[0]
    seg = t // N_DEV

    def body(dg_ref, ys_ref, u_ref, dys_ref, dysb_ref, dd_ref, tmp):
        for n in range(SSM_W // BLK):
            for j in range(N_DEV):
                tmp[n, pl.ds(j, tm // N_DEV, stride=N_DEV), :] = dg_ref[j, :, n * BLK:(n + 1) * BLK]
        dyg = jnp.concatenate([tmp[n] for n in range(SSM_W // BLK)], axis=1)
        dys = dyg * _gelu_parts(ys_ref[...])[1]
        dys_ref[...] = dys
        dysb_ref[...] = dys.astype(BF16)
        part = jnp.sum(dys * u_ref[...], axis=0, keepdims=True)

        @pl.when(pl.program_id(0) == 0)
        def _():
            dd_ref[...] = part

        @pl.when(pl.program_id(0) > 0)
        def _():
            dd_ref[...] += part

    row = _bs((tm, SSM_W), lambda i: (i, 0))
    return pl.pallas_call(
        body, grid=(t // tm,), in_specs=[_bs((N_DEV, tm // N_DEV, SSM_W), lambda i: (0, i, 0)), row, row],
        out_specs=[row, row, _bs((1, SSM_W), lambda i: (0, 0))],
        out_shape=[SDS((t, SSM_W), F32), SDS((t, SSM_W), BF16), SDS((1, SSM_W), F32)],
        scratch_shapes=[pltpu.VMEM((SSM_W // BLK, tm, BLK), F32)], compiler_params=_cp(1), name="ssm_out_bwd")(
            d_yg.reshape(N_DEV, seg, SSM_W), ys, u)


def _du_to_dz(du_raw, dys, d_skip, dz, ucol_block, tm):
    t = du_raw.shape[0]
    seg = t // N_DEV

    def body(du_ref, dys_ref, d_ref, dz_in, o_ref, tmp):
        del dz_in
        du = du_ref[...] + d_ref[...] * dys_ref[...]
        for n in range(SSM_W // BLK):
            tmp[n] = du[:, n * BLK:(n + 1) * BLK]
            for j in range(N_DEV):
                o_ref[j, :, n * BLK:(n + 1) * BLK] = tmp[n, pl.ds(j, tm // N_DEV, stride=N_DEV), :].astype(BF16)

    row = _bs((tm, SSM_W), lambda i: (i, 0))
    out = pl.pallas_call(
        body, grid=(t // tm,), in_specs=[row, row, _bs((1, SSM_W), lambda i: (0, 0)), pl.BlockSpec(memory_space=pl.ANY)],
        out_specs=_bs((N_DEV, tm // N_DEV, SSM_W), lambda i: (0, i, ucol_block)), out_shape=SDS((N_DEV, seg, dz.shape[1]), BF16),
        scratch_shapes=[pltpu.VMEM((SSM_W // BLK, tm, BLK), F32)], input_output_aliases={3: 0}, compiler_params=_cp(1),
        name="du_to_dz")(du_raw, dys, d_skip, dz.reshape(N_DEV, seg, dz.shape[1]))
    return out.reshape(dz.shape)


def _merge_fwd(z, gcol, attn_d, ya, yb, tm):
    t, d = attn_d.shape

    def body(ga_ref, gs_ref, ad_ref, ya_ref, yb_ref, m_ref):
        m = _sigmoid(ga_ref[...]) * ad_ref[...] + _sigmoid(gs_ref[...]) * (ya_ref[...] * _sigmoid(yb_ref[...]))
        m_ref[...] = m.astype(BF16)

    row = _bs((tm, d), lambda i: (i, 0))
    return pl.pallas_call(
        body, grid=(t // tm,), in_specs=[_bs((tm, d), lambda i: (i, gcol)), _bs((tm, d), lambda i: (i, gcol + 1)), row, row, row],
        out_specs=row, out_shape=SDS((t, d), BF16), compiler_params=_cp(1), name="merge_fwd")(z, z, attn_d, ya, yb)


def _merge_bwd(dm, z, gcol, attn_d, ya, yb, tm):
    t, d = attn_d.shape

    def body(dm_ref, ga_ref, gs_ref, ad_ref, ya_ref, yb_ref, dz_ref, dad_ref, dya_ref, dyb_ref):
        dmv = dm_ref[...]
        ga, gs, sb = _sigmoid(ga_ref[...]), _sigmoid(gs_ref[...]), _sigmoid(yb_ref[...])
        yav = ya_ref[...]
        d_ga = dmv * ad_ref[...] * ga * (1.0 - ga)
        d_gs = dmv * (yav * sb) * gs * (1.0 - gs)
        dz_ref[...] = jnp.where(pl.program_id(1) == 0, d_ga, d_gs).astype(BF16)
        dad_ref[...] = (dmv * ga).astype(BF16)
        dsd = dmv * gs
        dya_ref[...] = (dsd * sb).astype(BF16)
        dyb_ref[...] = (dsd * yav * sb * (1.0 - sb)).astype(BF16)

    row = _bs((tm, d), lambda i, c: (i, 0))
    return pl.pallas_call(
        body, grid=(t // tm, 2),
        in_specs=[row, _bs((tm, d), lambda i, c: (i, gcol)), _bs((tm, d), lambda i, c: (i, gcol + 1)), row, row, row],
        out_specs=[_bs((tm, d), lambda i, c: (i, gcol + c)), row, row, row],
        out_shape=[SDS(z.shape, BF16), SDS((t, d), BF16), SDS((t, d), BF16), SDS((t, d), BF16)],
        compiler_params=_cp(2), name="merge_bwd")(dm, z, z, attn_d, ya, yb)


def _swiglu_fwd(fg, fu, tm):
    nd, t, fs = fg.shape

    def body(g_ref, u_ref, o_ref):
        gv = g_ref[...]
        o_ref[...] = (gv * _sigmoid(gv) * u_ref[...]).astype(BF16)

    blk = _bs((None, tm, fs), lambda j, i: (j, i, 0))
    return pl.pallas_call(body, grid=(nd, t // tm), in_specs=[blk, blk], out_specs=blk, out_shape=SDS(fg.shape, BF16),
                          compiler_params=_cp(2), name="swiglu_fwd")(fg, fu)


def _swiglu_bwd(d_act, fg, fu, tm):
    nd, t, fs = fg.shape

    def body(da_ref, g_ref, u_ref, dg_ref, du_ref):
        gv, dav = g_ref[...], da_ref[...]
        sg = _sigmoid(gv)
        du_ref[...] = (dav * gv * sg).astype(BF16)
        dg_ref[...] = (dav * u_ref[...] * sg * (1.0 + gv * (1.0 - sg))).astype(BF16)

    blk = _bs((None, tm, fs), lambda j, i: (j, i, 0))
    return pl.pallas_call(body, grid=(nd, t // tm), in_specs=[blk, blk, blk], out_specs=[blk, blk],
                          out_shape=[SDS(fg.shape, BF16)] * 2, compiler_params=_cp(2), name="swiglu_bwd")(d_act, fg, fu)


def _head(h2, pg, pp, g_final, target, tm):
    t, d = h2.shape
    nt = t // tm

    def body(h2_ref, pg_ref, pp_ref, g_ref, tg_ref, loss_ref, dg_ref, dh3_ref, dpp_ref, dpg_ref, lacc):
        i = pl.program_id(0)
        sg = _sigmoid(pg_ref[...])
        ppv = pp_ref[...]
        h3 = h2_ref[...] + sg * ppv
        r = lax.rsqrt(jnp.mean(h3 * h3, axis=-1, keepdims=True) + EPS)
        xh = h3 * r
        gv = g_ref[...]
        diff = xh * gv - tg_ref[...]
        dout = diff * (1.0 / d)
        dxh = dout * gv
        dh3 = r * (dxh - xh * jnp.mean(dxh * xh, axis=-1, keepdims=True))
        dh3_ref[...] = dh3
        dpp_ref[...] = (dh3 * sg).astype(BF16)
        dpg_ref[...] = (dh3 * ppv * sg * (1.0 - sg)).astype(BF16)
        part = jnp.sum(dout * xh, axis=0, keepdims=True)
        lpart = jnp.sum(diff * diff, axis=0, keepdims=True)

        @pl.when(i == 0)
        def _():
            dg_ref[...] = part
            lacc[...] = lpart

        @pl.when(i > 0)
        def _():
            dg_ref[...] += part
            lacc[...] += lpart

        @pl.when(i == nt - 1)
        def _():
            loss_ref[...] = (0.5 / d) * jnp.sum(lacc[...], axis=-1, keepdims=True)

    row = _bs((tm, d), lambda i: (i, 0))
    vec = _bs((1, d), lambda i: (0, 0))
    return pl.pallas_call(
        body, grid=(nt,), in_specs=[row, row, row, vec, row],
        out_specs=[_bs((1, 1), lambda i: (0, 0)), vec, row, row, row],
        out_shape=[SDS((1, 1), F32), SDS((1, d), F32), SDS((t, d), F32), SDS((t, d), BF16), SDS((t, d), BF16)],
        scratch_shapes=[pltpu.VMEM((1, d), F32)], compiler_params=_cp(1), name="head")(h2, pg, pp, g_final, target)


def _block_diag(blocks):
    nb, ng, r, c = blocks.shape
    eye = jnp.eye(ng, dtype=blocks.dtype)
    return (blocks[:, :, :, None, :] * eye[None, :, None, :, None]).reshape(nb, ng * r, ng * c)


def _diag_blocks(full, r, c):
    k, nb = full.shape[:2]
    ng = full.shape[2] // r
    x = full.reshape(k, nb, ng, r, ng, c)
    eye = jnp.eye(ng, dtype=full.dtype)
    return jnp.sum(x * eye[None, None, :, None, :, None], axis=4).reshape(k, nb * ng, r, c)


_SMALL = ("g_mix", "a_re", "a_im", "log_dt", "b_re", "b_im", "c_re", "c_im", "d_skip", "g_ffn", "g_final")


def _pack_small(arrs):
    flat = jnp.concatenate([a.reshape(-1) for a in arrs])
    pad = (-flat.shape[0]) % (8 * 128)
    return jnp.pad(flat, (0, pad)).reshape(-1, 128)


def _unpack_small(packed, shapes):
    flat = packed.reshape(-1)
    out, off = [], 0
    for s in shapes:
        n = math.prod(s)
        out.append(flat[off:off + n].reshape(s))
        off += n
    return out


def kernel(x, p, positions, g_mix, w_in, a_re, a_im, log_dt, b_re, b_im, c_re, c_im, d_skip, w_attn_proj, w_glu_a, w_glu_b, w_out, g_ffn, w_ffn_gate, w_ffn_up, w_ffn_down, w_ple_gate, w_ple_proj, g_final, loss_target, m_g_mix, m_w_in, m_a_re, m_a_im, m_log_dt, m_b_re, m_b_im, m_c_re, m_c_im, m_d_skip, m_w_attn_proj, m_w_glu_a, m_w_glu_b, m_w_out, m_g_ffn, m_w_ffn_gate, m_w_ffn_up, m_w_ffn_down, m_w_ple_gate, m_w_ple_proj, m_g_final, v_g_mix, v_w_in, v_a_re, v_a_im, v_log_dt, v_b_re, v_b_im, v_c_re, v_c_im, v_d_skip, v_w_attn_proj, v_w_glu_a, v_w_glu_b, v_w_out, v_g_ffn, v_w_ffn_gate, v_w_ffn_up, v_w_ffn_down, v_w_ple_gate, v_w_ple_proj, v_g_final):
    args = dict(locals())
    t, d = x.shape[1], x.shape[2]
    inw = w_in.shape[2] * N_DEV
    fs = w_ffn_gate.shape[2]
    ff = fs * N_DEV
    ple = w_ple_proj.shape[1]
    seg = t // N_DEV
    assert inw == 3 * QK_W + SSM_W + 2 * d and t % (N_DEV * SCAN_ROWS // 8) == 0 and seg & (seg - 1) == 0
    tm = min(1024, t)
    te = min(512, t)
    tk = min(512, t)
    ucol = (3 * QK_W) // SSM_W
    gcol = (3 * QK_W + SSM_W) // d
    assert (3 * QK_W + SSM_W) % d == 0

    x2, p2, tgt = x[0], p[0, 0], loss_target[0]
    pos = positions.reshape(t, 1)
    inv = ROPE_THETA ** (-jnp.arange(ROPE_HALF, dtype=F32) * 2.0 / ROPE_DIM)
    invf = jnp.concatenate([inv, inv, jnp.zeros((HEAD_DIM - ROPE_DIM,), F32)]).reshape(1, HEAD_DIM)

    wnames = ("w_in", "w_attn_proj", "w_glu_a", "w_glu_b", "w_out", "w_ffn_gate", "w_ffn_up", "w_ffn_down", "w_ple_gate",
              "w_ple_proj")
    kinds = ("cols", "cols", "cols", "cols", "rows", "slot", "slot", "rows", "rows", "cols")
    shards = [args[n][0].astype(BF16) for n in wnames]
    W_in, W_ap, W_ga, W_gb, W_out, W_fg, W_fu, W_fd, W_pg, W_pp = _all_gather(shards, kinds)

    n1 = _rms_fwd("norm_mix", x2, g_mix, tm)
    tn_in = _pick(inw, (1024, 512, 256, 128))
    z = _mm("z_proj", (t // tm, inw // tn_in, 1), "nn", n1, _bs((tm, d), lambda i, j, k: (i, 0)),
            W_in, _bs((d, tn_in), lambda i, j, k: (0, j)), SDS((t, inw), F32), _bs((tm, tn_in), lambda i, j, k: (i, j)))

    qkv = _rope_dilate(z, pos, invf, tm)
    outs, lses = [], []
    for g, dil in enumerate(DILATIONS):
        o_g, l_g = _attn_fwd(qkv[g], dil, min(512, t // dil))
        outs.append(o_g)
        lses.append(l_g)
    merged = _attn_merge(outs, lses, te)
    attn, attn_bf, lts = merged[0], merged[1], merged[2:]
    row_d = _bs((tm, d), lambda i, j, k: (i, 0))
    attn_d = _mm("attn_proj", (t // tm, 1, 1), "nn", attn_bf, _bs((tm, GROUP_W), lambda i, j, k: (i, 0)),
                 W_ap, _bs((GROUP_W, d), lambda i, j, k: (0, 0)), SDS((t, d), F32), row_d)

    nsq = seg.bit_length() - 1
    bar_re, bar_im, z_re, z_im, pw_re, pw_im = _ssm_disc(a_re[0], a_im[0], log_dt.reshape(SSM_GROUPS, 1), nsq)
    gp = SSM_GROUPS * SSM_STATE
    b_re2, b_im2 = b_re.reshape(gp, SSM_GROUP), b_im.reshape(gp, SSM_GROUP)
    bb_re, bb_im = _ssm_scale_b(z_re.reshape(gp, 1), z_im.reshape(gp, 1), b_re2, b_im2)

    def chunks(a, r, c):
        return a.reshape(SSM_NB, SSM_GROUPS // SSM_NB, r, c)

    bbt = lambda a: jnp.swapaxes(a.reshape(SSM_GROUPS, SSM_STATE, SSM_GROUP), 1, 2)
    bd = jnp.concatenate([_block_diag(chunks(bbt(bb_re), SSM_GROUP, SSM_STATE)),
                          _block_diag(chunks(bbt(bb_im), SSM_GROUP, SSM_STATE))]).astype(BF16)
    ct = lambda a: jnp.swapaxes(a[0], 1, 2)
    cd = jnp.concatenate([_block_diag(chunks(ct(c_re), SSM_STATE, SSM_GROUP)),
                          _block_diag(chunks(-ct(c_im), SSM_STATE, SSM_GROUP))]).astype(BF16)
    lam = jnp.concatenate([bar_re.reshape(1, gp), bar_im.reshape(1, gp)], axis=1)
    lamc = jnp.concatenate([bar_re.reshape(1, gp), -bar_im.reshape(1, gp)], axis=1)
    pw = jnp.concatenate([pw_re.reshape(1, gp), pw_im.reshape(1, gp)], axis=1)
    pwc = jnp.concatenate([pw_re.reshape(1, gp), -pw_im.reshape(1, gp)], axis=1)
    lam8, lamc8 = jnp.broadcast_to(lam, (8, 2 * gp)), jnp.broadcast_to(lamc, (8, 2 * gp))

    u_perm, u_bf = _permute_u(z, ucol, te)
    start_f = _ssm_carries("ssm_carries_fwd", u_bf, bd, "nn", lam8, pw, False)
    h_all, y_raw = _ssm_fwd(u_bf, bd, cd, lam8, start_f)
    dsk = d_skip.reshape(1, SSM_W)
    ys, yg_bf = _ssm_out(y_raw, u_perm, dsk, te)
    glu_w = _bs((SSM_W, d), lambda i, j, k: (0, 0))
    row_s = _bs((tm, SSM_W), lambda i, j, k: (i, 0))
    ya = _mm("glu_a", (t // tm, 1, 1), "nn", yg_bf, row_s, W_ga, glu_w, SDS((t, d), F32), row_d)
    yb = _mm("glu_b", (t // tm, 1, 1), "nn", yg_bf, row_s, W_gb, glu_w, SDS((t, d), F32), row_d)

    m_bf = _merge_fwd(z, gcol, attn_d, ya, yb, te)
    sq_w = _bs((d, d), lambda i, j, k: (0, 0))
    h1 = _mm("out_proj", (t // tm, 1, 1), "nn", m_bf, row_d, W_out, sq_w, SDS((t, d), F32), row_d, res=x2, res_spec=row_d)

    n2 = _rms_fwd("norm_ffn", h1, g_ffn, tm)
    hid = _bs((None, tm, fs), lambda i, j, k: (j, i, 0))
    wcol = _bs((None, d, fs), lambda i, j, k: (j, 0, 0))
    fg = _mm("ffn_gate", (t // tm, N_DEV, 1), "nn", n2, row_d, W_fg, wcol, SDS((N_DEV, t, fs), F32), hid)
    fu = _mm("ffn_up", (t // tm, N_DEV, 1), "nn", n2, row_d, W_fu, wcol, SDS((N_DEV, t, fs), F32), hid)
    act = _swiglu_fwd(fg, fu, tm)
    hid_k = _bs((None, tm, fs), lambda i, j, k: (k, i, 0))
    h2, h2_bf = _mm("ffn_down", (t // tm, 1, N_DEV), "nn", act, hid_k, W_fd, _bs((fs, d), lambda i, j, k: (k, 0)),
                    SDS((t, d), F32), row_d, res=h1, res_spec=row_d, out2_dtype=BF16)

    pg = _mm("ple_gate", (t // tm, 1, 1), "nn", h2_bf, row_d, W_pg, sq_w, SDS((t, d), F32), row_d)
    pp = _mm("ple_proj", (t // tm, 1, 1), "nn", p2, _bs((tm, ple), lambda i, j, k: (i, 0)), W_pp,
             _bs((ple, d), lambda i, j, k: (0, 0)), SDS((t, d), F32), row_d)
    loss_part, dg_final, dh3, dpp_bf, dpg_bf = _head(h2, pg, pp, g_final.reshape(1, d), tgt, te)
    loss = lax.psum(loss_part[0, 0], ("x", "y", "c"))

    nkt = t // tk
    tok_a = lambda w: _bs((tk, w), lambda i, j, k: (k, 0))

    def wgrad(name, a, wa, b, wb):
        return _mm(name, (1, 1, nkt), "tn", a, tok_a(wa), b, tok_a(wb), SDS((wa, wb), F32), _bs((wa, wb), lambda i, j, k: (0, 0)))

    dW_pp = wgrad("dw_ple_proj", p2, ple, dpp_bf, d)
    dW_pg = wgrad("dw_ple_gate", h2_bf, d, dpg_bf, d)
    dh2, dh2_bf = _mm("d_ple_gate", (t // tm, 1, 1), "nt", dpg_bf, row_d, W_pg, sq_w, SDS((t, d), F32), row_d,
                      res=dh3, res_spec=row_d, out2_dtype=BF16)

    d_act = _mm("d_ffn_down", (t // tm, N_DEV, 1), "nt", dh2_bf, row_d, W_fd, _bs((fs, d), lambda i, j, k: (j, 0)),
                SDS((N_DEV, t, fs), F32), hid)
    dW_fd = _mm("dw_ffn_down", (N_DEV, 1, nkt), "tn", act, _bs((None, tk, fs), lambda i, j, k: (i, k, 0)), dh2_bf, tok_a(d),
                SDS((ff, d), F32), _bs((fs, d), lambda i, j, k: (i, 0)))
    dfg_bf, dfu_bf = _swiglu_bwd(d_act, fg, fu, tm)
    wcol_k = _bs((None, d, fs), lambda i, j, k: (k, 0, 0))
    dn2 = _mm("d_ffn_gate", (t // tm, 1, N_DEV), "nt", dfg_bf, hid_k, W_fg, wcol_k, SDS((t, d), F32), row_d)
    dn2 = _mm("d_ffn_up", (t // tm, 1, N_DEV), "nt", dfu_bf, hid_k, W_fu, wcol_k, SDS((t, d), F32), row_d, res=dn2, res_spec=row_d)
    hid_g = _bs((None, tk, fs), lambda i, j, k: (j, k, 0))
    slot_o = _bs((None, d, fs), lambda i, j, k: (j, 0, 0))
    dW_fg = _mm("dw_ffn_gate", (1, N_DEV, nkt), "tn", n2, tok_a(d), dfg_bf, hid_g, SDS((N_DEV, d, fs), F32), slot_o)
    dW_fu = _mm("dw_ffn_up", (1, N_DEV, nkt), "tn", n2, tok_a(d), dfu_bf, hid_g, SDS((N_DEV, d, fs), F32), slot_o)
    dh1, dh1_bf, dg_ffn = _rms_bwd("norm_ffn_bwd", dn2, h1, g_ffn, dh2, te, True)

    dm = _mm("d_out_proj", (t // tm, 1, 1), "nt", dh1_bf, row_d, W_out, sq_w, SDS((t, d), F32), row_d)
    dW_out = wgrad("dw_out", m_bf, d, dh1_bf, d)
    dz, dad_bf, dya_bf, dyb_bf = _merge_bwd(dm, z, gcol, attn_d, ya, yb, te)

    d_yg = _mm("d_glu_a", (t // tm, 1, 1), "nt", dya_bf, row_d, W_ga, glu_w, SDS((t, SSM_W), F32), row_s)
    d_yg = _mm("d_glu_b", (t // tm, 1, 1), "nt", dyb_bf, row_d, W_gb, glu_w, SDS((t, SSM_W), F32), row_s, res=d_yg, res_spec=row_s)
    dW_ga = wgrad("dw_glu_a", yg_bf, SSM_W, dya_bf, d)
    dW_gb = wgrad("dw_glu_b", yg_bf, SSM_W, dyb_bf, d)
    dys, dys_bf, dd_skip = _ssm_out_bwd(d_yg, ys, u_perm, te)
    start_b = _ssm_carries("ssm_carries_bwd", dys_bf, cd, "nt", lamc8, pwc, True)
    du_raw, dlam8, dbd, dcd = _ssm_bwd(dys_bf, u_bf, h_all, bd, cd, lamc8, start_b)
    dz = _du_to_dz(du_raw, dys, dsk, dz, ucol, te)
    dlam = jnp.sum(dlam8, axis=0)
    dbb = _diag_blocks(dbd.reshape(2, SSM_NB, BLK, 512), SSM_GROUP, SSM_STATE)
    dbb_re = jnp.swapaxes(dbb[0], 1, 2).reshape(gp, SSM_GROUP)
    dbb_im = jnp.swapaxes(dbb[1], 1, 2).reshape(gp, SSM_GROUP)
    dcc = _diag_blocks(dcd.reshape(2, SSM_NB, 512, BLK), SSM_STATE, SSM_GROUP)
    dc_re, dc_im = jnp.swapaxes(dcc[0], 1, 2), -jnp.swapaxes(dcc[1], 1, 2)
    db_re, db_im, dz_re, dz_im = _ssm_scale_b_bwd(z_re.reshape(gp, 1), z_im.reshape(gp, 1), b_re2, b_im2, dbb_re, dbb_im)
    gshape = (SSM_GROUPS, SSM_STATE)
    da_re, da_im, dlog_dt = _ssm_disc_bwd(a_re[0], a_im[0], log_dt.reshape(SSM_GROUPS, 1), dlam[:gp].reshape(gshape),
                                          dlam[gp:].reshape(gshape), dz_re.reshape(gshape), dz_im.reshape(gshape))

    d_attn = _mm("d_attn_proj", (t // tm, 1, 1), "nt", dad_bf, row_d, W_ap, _bs((GROUP_W, d), lambda i, j, k: (0, 0)),
                 SDS((t, GROUP_W), F32), _bs((tm, GROUP_W), lambda i, j, k: (i, 0)))
    dW_ap = wgrad("dw_attn_proj", attn_bf, GROUP_W, dad_bf, d)
    pre = _attn_bwd_pre(d_attn, attn, te)
    das, deltas = pre[:N_GROUPS], pre[N_GROUPS:]
    dqkvs = [_attn_bwd(qkv[g], das[g], lts[g], deltas[g], dil, min(512, t // dil)) for g, dil in enumerate(DILATIONS)]
    dz = _undilate_rope_bwd(dqkvs, pos, invf, dz, tm)

    tk_in = _pick(inw, (1024, 512, 256, 128))
    dn1 = _mm("d_z_proj", (t // tm, 1, inw // tk_in), "nt", dz, _bs((tm, tk_in), lambda i, j, k: (i, k)),
              W_in, _bs((d, tk_in), lambda i, j, k: (0, k)), SDS((t, d), F32), row_d)
    dW_in = _mm("dw_in", (1, inw // tn_in, nkt), "tn", n1, tok_a(d), dz, _bs((tk, tn_in), lambda i, j, k: (k, j)),
                SDS((d, inw), F32), _bs((d, tn_in), lambda i, j, k: (0, j)))
    dx, dg_mix = _rms_bwd("norm_mix_bwd", dn1, x2, g_mix, dh1, te, False)

    grads = [dW_in, dW_ap, dW_ga, dW_gb, dW_out, dW_fg, dW_fu, dW_fd, dW_pg, dW_pp]
    sizes = [s.shape[0] if k == "rows" else s.shape[-1] for s, k in zip(shards, kinds)]
    small_parts = dict(g_mix=dg_mix, a_re=da_re, a_im=da_im, log_dt=dlog_dt, b_re=db_re, b_im=db_im, c_re=dc_re, c_im=dc_im,
                       d_skip=dd_skip, g_ffn=dg_ffn, g_final=dg_final)
    small = _pack_small([small_parts[n] for n in _SMALL])
    received = _exchange_grads(grads, kinds, sizes, small)

    new = {}
    for n, recv in zip(wnames, received[:-1]):
        new[n] = [o.reshape(args[n].shape) for o in _adamw("adamw_" + n, recv, args[n][0], args["m_" + n][0], args["v_" + n][0])]
    pk = lambda pre: _pack_small([args[pre + n] for n in _SMALL])
    sm = _adamw("adamw_small", received[-1], pk(""), pk("m_"), pk("v_"))
    shapes = [args[n].shape for n in _SMALL]
    for n, vals in zip(_SMALL, zip(*[_unpack_small(o, shapes) for o in sm])):
        new[n] = list(vals)

    order = ("g_mix", "w_in", "a_re", "a_im", "log_dt", "b_re", "b_im", "c_re", "c_im", "d_skip", "w_attn_proj", "w_glu_a",
             "w_glu_b", "w_out", "g_ffn", "w_ffn_gate", "w_ffn_up", "w_ffn_down", "w_ple_gate", "w_ple_proj", "g_final")
    return (loss, dx.reshape(x.shape), *[new[n][0] for n in order], *[new[n][1] for n in order],
            *[new[n][2] for n in order], *[new[n][3] for n in order])
```

```python
import functools
import math

import jax
import jax.numpy as jnp
from jax import lax
from jax.experimental import pallas as pl
from jax.experimental.pallas import tpu as pltpu

F32 = jnp.float32
BF16 = jnp.bfloat16
SDS = jax.ShapeDtypeStruct

N_DEV = 8
HEAD_DIM = 128
HEADS_PER_GROUP = 4
GROUP_W = HEADS_PER_GROUP * HEAD_DIM
DILATIONS = (1, 4, 16)
N_GROUPS = len(DILATIONS)
QK_W = N_GROUPS * GROUP_W
BLK = 128
ROPE_THETA = 500000.0
ROPE_DIM = HEAD_DIM // 4
ROPE_HALF = ROPE_DIM // 2
SSM_W = 512
SSM_GROUP = 16
SSM_GROUPS = SSM_W // SSM_GROUP
SSM_STATE = 64
NSTATE = SSM_GROUPS * SSM_STATE
SSM_NB = 4
EPS = 1e-6
ADAM_LR, ADAM_B1, ADAM_B2, ADAM_EPS, ADAM_WD, ADAM_STEP = 0.001, 0.9, 0.999, 1e-08, 0.01, 10
NEG = -1e30

VMEM_LIMIT = 52 * 1024 * 1024
SCAN_ROWS = 256
SCAN_LANES = 512


def _cp(n):
    return pltpu.CompilerParams(dimension_semantics=("arbitrary",) * n, vmem_limit_bytes=VMEM_LIMIT)


def _sigmoid(x):
    return 1.0 / (1.0 + jnp.exp(-x))


_DNUMS = {"nn": (((1,), (0,)), ((), ())), "nt": (((1,), (1,)), ((), ())), "tn": (((0,), (0,)), ((), ()))}


def _bs(shape, fn):
    return pl.BlockSpec(shape, fn)


def _store_all(prods, extra_refs, out_refs, scratch_refs):
    r = prods[0]
    for p in prods[1:]:
        r = r + p
    for e in extra_refs:
        r = r + e[...]
    for o in out_refs:
        o[...] = r.astype(o.dtype)


def _mm(name, grid, pairs, outs, extras=(), epilogue=_store_all, scratch=(), alias_to_out0=None):
    nk = grid[2]
    npair = len(pairs)
    steps = [p[5] if len(p) > 5 else nk for p in pairs]

    def block(spec):
        return tuple(s for s in spec.block_shape if s is not None)

    acc_shapes = [jax.eval_shape(lambda u, v, dn=_DNUMS[p[0]]: lax.dot_general(u, v, dn, preferred_element_type=F32),
                                 SDS(block(p[2]), BF16), SDS(block(p[4]), BF16)).shape for p in pairs]
    n_in = 2 * npair + len(extras) + (alias_to_out0 is not None)

    def body(*refs):
        extra_refs = refs[2 * npair:2 * npair + len(extras)]
        out_refs = refs[n_in:n_in + len(outs)]
        rest = refs[n_in + len(outs):]
        acc_refs = rest[:npair] if nk > 1 else ()
        scratch_refs = rest[len(acc_refs):]
        k = pl.program_id(2)

        def product(i):
            return lax.dot_general(refs[2 * i][...].astype(BF16), refs[2 * i + 1][...].astype(BF16), _DNUMS[pairs[i][0]],
                                   preferred_element_type=F32)

        if nk == 1:
            epilogue([product(i) for i in range(npair)], extra_refs, out_refs, scratch_refs)
            return
        for i in range(npair):
            @pl.when(k == 0)
            def _(i=i):
                acc_refs[i][...] = product(i)

            @pl.when((k > 0) & (k < steps[i]))
            def _(i=i):
                acc_refs[i][...] += product(i)

        @pl.when(k == nk - 1)
        def _():
            epilogue([a[...] for a in acc_refs], extra_refs, out_refs, scratch_refs)

    ins, in_specs = [], []
    for p in pairs:
        ins += [p[1], p[3]]
        in_specs += [p[2], p[4]]
    ins += [e[0] for e in extras]
    in_specs += [e[1] for e in extras]
    aliases = {}
    if alias_to_out0 is not None:
        aliases = {len(ins): 0}
        ins.append(alias_to_out0)
        in_specs.append(pl.BlockSpec(memory_space=pl.ANY))
    scratch_shapes = ([pltpu.VMEM(s, F32) for s in acc_shapes] if nk > 1 else []) + list(scratch)
    return pl.pallas_call(body, grid=grid, in_specs=in_specs, out_specs=[o[1] for o in outs], out_shape=[o[0] for o in outs],
                          scratch_shapes=scratch_shapes, input_output_aliases=aliases, compiler_params=_cp(3), name=name)(*ins)


def _pick(n, cands):
    for c in cands:
        if n % c == 0:
            return c
    return n


def _my_index():
    return 4 * lax.axis_index("x") + 2 * lax.axis_index("y") + lax.axis_index("c")


def _peer(d):
    mx, my, mc = lax.axis_index("x"), lax.axis_index("y"), lax.axis_index("c")
    return (mx ^ ((d >> 2) & 1), my ^ ((d >> 1) & 1), mc ^ (d & 1))


def _win(ref, kind, j, n):
    if kind == "slot":
        return ref.at[j]
    if kind == "rows":
        return ref.at[pl.ds(pl.multiple_of(j * n, 8), n)]
    return ref.at[:, pl.ds(pl.multiple_of(j * n, 128), n)]


def _win7(ref, kind, n):
    if kind == "slot":
        return ref.at[pl.ds(0, 7)]
    if kind == "rows":
        return ref.at[pl.ds(0, 7 * n)]
    return ref.at[:, pl.ds(0, 7 * n)]


def _full_shape(shard_shape, kind):
    if kind == "slot":
        return (N_DEV,) + tuple(shard_shape)
    if kind == "rows":
        return (N_DEV * shard_shape[0],) + tuple(shard_shape[1:])
    return (shard_shape[0], N_DEV * shard_shape[1])


def _all_gather(shards, kinds):
    n = len(shards)
    sizes = [s.shape[0] if k == "rows" else s.shape[-1] for s, k in zip(shards, kinds)]

    def body(*refs):
        ins, outs = refs[:n], refs[n:2 * n]
        send_sems, recv_sems, local_sems = refs[2 * n:2 * n + 3]
        me = _my_index()
        my_id = (lax.axis_index("x"), lax.axis_index("y"), lax.axis_index("c"))
        local = []
        for a in range(n):
            cp = pltpu.make_async_copy(ins[a], _win(outs[a], kinds[a], me, sizes[a]), local_sems.at[a])
            cp.start()
            local.append(cp)
        for a in range(n):
            for d in range(1, N_DEV):
                pltpu.make_async_remote_copy(
                    src_ref=ins[a], dst_ref=_win(outs[a], kinds[a], me, sizes[a]), send_sem=send_sems.at[a],
                    recv_sem=recv_sems.at[a], device_id=_peer(d), device_id_type=pl.DeviceIdType.MESH).start()
        for a in range(n):
            seven = _win7(outs[a], kinds[a], sizes[a])
            pltpu.make_async_remote_copy(src_ref=seven, dst_ref=seven, send_sem=send_sems.at[a], recv_sem=recv_sems.at[a],
                                         device_id=my_id, device_id_type=pl.DeviceIdType.MESH).wait()
        for cp in local:
            cp.wait()

    any_spec = pl.BlockSpec(memory_space=pl.ANY)
    return pl.pallas_call(
        body, in_specs=[any_spec] * n, out_specs=[any_spec] * n,
        out_shape=[SDS(_full_shape(s.shape, k), s.dtype) for s, k in zip(shards, kinds)],
        scratch_shapes=[pltpu.SemaphoreType.DMA((n,)), pltpu.SemaphoreType.DMA((n,)), pltpu.SemaphoreType.DMA((n,))],
        name="all_gather_weights")(*shards)


def _exchange_grads(grads, kinds, sizes, small):
    n = len(grads)

    def shard_shape(g, kind, sz):
        if kind == "slot":
            return g.shape[1:]
        if kind == "rows":
            return (sz,) + g.shape[1:]
        return (g.shape[0], sz)

    def body(*refs):
        ins, small_in = refs[:n], refs[n]
        outs, small_out = refs[n + 1:2 * n + 1], refs[2 * n + 1]
        send_sems, recv_sems, local_sems = refs[2 * n + 2:2 * n + 5]
        me = _my_index()
        my_id = (lax.axis_index("x"), lax.axis_index("y"), lax.axis_index("c"))
        local = []
        for a in range(n):
            cp = pltpu.make_async_copy(_win(ins[a], kinds[a], me, sizes[a]), outs[a].at[me], local_sems.at[a])
            cp.start()
            local.append(cp)
        cp = pltpu.make_async_copy(small_in, small_out.at[me], local_sems.at[n])
        cp.start()
        local.append(cp)
        for a in range(n):
            for d in range(1, N_DEV):
                px, py, pc = _peer(d)
                pltpu.make_async_remote_copy(
                    src_ref=_win(ins[a], kinds[a], 4 * px + 2 * py + pc, sizes[a]), dst_ref=outs[a].at[me],
                    send_sem=send_sems.at[a], recv_sem=recv_sems.at[a], device_id=(px, py, pc),
                    device_id_type=pl.DeviceIdType.MESH).start()
        for d in range(1, N_DEV):
            pltpu.make_async_remote_copy(src_ref=small_in, dst_ref=small_out.at[me], send_sem=send_sems.at[n],
                                         recv_sem=recv_sems.at[n], device_id=_peer(d),
                                         device_id_type=pl.DeviceIdType.MESH).start()
        for a, out in enumerate(list(outs) + [small_out]):
            seven = out.at[pl.ds(0, 7)]
            pltpu.make_async_remote_copy(src_ref=seven, dst_ref=seven, send_sem=send_sems.at[a], recv_sem=recv_sems.at[a],
                                         device_id=my_id, device_id_type=pl.DeviceIdType.MESH).wait()
        for cp in local:
            cp.wait()

    any_spec = pl.BlockSpec(memory_space=pl.ANY)
    out_shape = [SDS((N_DEV,) + tuple(shard_shape(g, k, s)), F32) for g, k, s in zip(grads, kinds, sizes)]
    out_shape.append(SDS((N_DEV,) + small.shape, F32))
    return pl.pallas_call(
        body, in_specs=[any_spec] * (n + 1), out_specs=[any_spec] * (n + 1), out_shape=out_shape,
        scratch_shapes=[pltpu.SemaphoreType.DMA((n + 1,)), pltpu.SemaphoreType.DMA((n + 1,)),
                        pltpu.SemaphoreType.DMA((n + 1,))],
        name="exchange_grads")(*grads, small)


def _adamw(name, recv, w, m, v):
    rows, cols = w.shape
    tr = max(c for c in range(8, 257, 8) if rows % c == 0) if rows % 8 == 0 else rows

    def body(r_ref, w_ref, m_ref, v_ref, g_ref, d_ref, nm_ref, nv_ref):
        g = r_ref[0]
        for s in range(1, N_DEV):
            g = g + r_ref[s]
        nm = ADAM_B1 * m_ref[...] + (1.0 - ADAM_B1) * g
        nv = ADAM_B2 * v_ref[...] + (1.0 - ADAM_B2) * (g * g)
        m_hat = nm / (1.0 - ADAM_B1 ** ADAM_STEP)
        v_hat = nv / (1.0 - ADAM_B2 ** ADAM_STEP)
        g_ref[...] = g
        d_ref[...] = -ADAM_LR * (m_hat / (jnp.sqrt(v_hat) + ADAM_EPS) + ADAM_WD * w_ref[...])
        nm_ref[...] = nm
        nv_ref[...] = nv

    blk = _bs((tr, cols), lambda i: (i, 0))
    return pl.pallas_call(
        body, grid=(rows // tr,), in_specs=[_bs((N_DEV, tr, cols), lambda i: (0, i, 0)), blk, blk, blk],
        out_specs=[blk] * 4, out_shape=[SDS((rows, cols), F32)] * 4, compiler_params=_cp(1), name=name)(recv, w, m, v)


def _rms_fwd(name, x, g, tm):
    t, d = x.shape

    def body(x_ref, g_ref, n_ref):
        xv = x_ref[...]
        r = lax.rsqrt(jnp.mean(xv * xv, axis=-1, keepdims=True) + EPS)
        n_ref[...] = (xv * r * g_ref[...]).astype(BF16)

    return pl.pallas_call(body, grid=(t // tm,), in_specs=[_bs((tm, d), lambda i: (i, 0)), _bs((1, d), lambda i: (0, 0))],
                          out_specs=_bs((tm, d), lambda i: (i, 0)), out_shape=SDS((t, d), BF16), compiler_params=_cp(1),
                          name=name)(x, g)


def _accumulate_rows(ref, part):
    @pl.when(pl.program_id(0) == 0)
    def _():
        ref[...] = part

    @pl.when(pl.program_id(0) > 0)
    def _():
        ref[...] += part


def _rms_bwd_epilogue(prods, extra_refs, out_refs, scratch_refs):
    dyv = prods[0]
    for p in prods[1:]:
        dyv = dyv + p
    if len(extra_refs) > 3:
        dyv = dyv + extra_refs[3][...]
    xv = extra_refs[0][...]
    r = lax.rsqrt(jnp.mean(xv * xv, axis=-1, keepdims=True) + EPS)
    xh = xv * r
    dxh = dyv * extra_refs[1][...]
    dx = extra_refs[2][...] + r * (dxh - xh * jnp.mean(dxh * xh, axis=-1, keepdims=True))
    for o in out_refs[:-1]:
        o[...] = dx.astype(o.dtype)
    _accumulate_rows(out_refs[-1], jnp.sum(dyv * xh, axis=0, keepdims=True))


def _out_norm_epilogue(prods, extra_refs, out_refs, scratch_refs):
    h = prods[0] + extra_refs[0][...]
    r = lax.rsqrt(jnp.mean(h * h, axis=-1, keepdims=True) + EPS)
    out_refs[0][...] = h
    out_refs[1][...] = (h * r * extra_refs[1][...]).astype(BF16)


def _glu_merge_epilogue(prods, extra_refs, out_refs, scratch_refs):
    ya, yb, ad = prods
    m = _sigmoid(extra_refs[0][...]) * ad + _sigmoid(extra_refs[1][...]) * (ya * _sigmoid(yb))
    out_refs[0][...] = m.astype(BF16)
    out_refs[1][...] = ya
    out_refs[2][...] = yb
    out_refs[3][...] = ad


def _merge_bwd_epilogue(prods, extra_refs, out_refs, scratch_refs):
    dmv = prods[0]
    d = dmv.shape[1]
    ga, gs = _sigmoid(extra_refs[0][...]), _sigmoid(extra_refs[1][...])
    adv, yav = extra_refs[2][...], extra_refs[3][...]
    sb = _sigmoid(extra_refs[4][...])
    out_refs[0][:, 0:d] = (dmv * adv * ga * (1.0 - ga)).astype(BF16)
    out_refs[0][:, d:2 * d] = (dmv * (yav * sb) * gs * (1.0 - gs)).astype(BF16)
    out_refs[1][...] = (dmv * ga).astype(BF16)
    dsd = dmv * gs
    out_refs[2][...] = (dsd * sb).astype(BF16)
    out_refs[3][...] = (dsd * yav * sb * (1.0 - sb)).astype(BF16)


def _swiglu_epilogue(prods, extra_refs, out_refs, scratch_refs):
    gv, uv = prods
    out_refs[0][...] = (gv * _sigmoid(gv) * uv).astype(BF16)
    out_refs[1][...] = gv
    out_refs[2][...] = uv


def _swiglu_bwd_epilogue(prods, extra_refs, out_refs, scratch_refs):
    dav = prods[0]
    gv, uv = extra_refs[0][...], extra_refs[1][...]
    sg = _sigmoid(gv)
    out_refs[0][...] = (dav * uv * sg * (1.0 + gv * (1.0 - sg))).astype(BF16)
    out_refs[1][...] = (dav * gv * sg).astype(BF16)


def _head_epilogue(n_tiles):
    def epilogue(prods, extra_refs, out_refs, scratch_refs):
        pgv, ppv = prods
        d = pgv.shape[1]
        lacc = scratch_refs[0]
        sg = _sigmoid(pgv)
        h3 = extra_refs[0][...] + sg * ppv
        r = lax.rsqrt(jnp.mean(h3 * h3, axis=-1, keepdims=True) + EPS)
        xh = h3 * r
        gv = extra_refs[1][...]
        diff = xh * gv - extra_refs[2][...]
        dout = diff * (1.0 / d)
        dxh = dout * gv
        dh3 = r * (dxh - xh * jnp.mean(dxh * xh, axis=-1, keepdims=True))
        out_refs[2][...] = dh3
        out_refs[3][...] = (dh3 * sg).astype(BF16)
        out_refs[4][...] = (dh3 * ppv * sg * (1.0 - sg)).astype(BF16)
        _accumulate_rows(out_refs[1], jnp.sum(dout * xh, axis=0, keepdims=True))
        _accumulate_rows(lacc, jnp.sum(diff * diff, axis=0, keepdims=True))

        @pl.when(pl.program_id(0) == n_tiles - 1)
        def _():
            out_refs[0][...] = (0.5 / d) * jnp.sum(lacc[...], axis=-1, keepdims=True)

    return epilogue


def _strided(r, n, d):
    return pl.ds(r, n, stride=d) if d > 1 else pl.ds(0, n)


def _rope_tables(pos_ref, invf_ref, c_s, s1_s, s2_s, on):
    ang = pos_ref[...].astype(F32) * invf_ref[...]
    lane = lax.broadcasted_iota(jnp.int32, ang.shape, 1)
    sn = jnp.sin(ang)
    c_s[...] = jnp.where((lane < ROPE_DIM) & on, jnp.cos(ang), 1.0)
    s1_s[...] = jnp.where((lane < ROPE_HALF) & on, -sn, 0.0)
    s2_s[...] = jnp.where((lane >= ROPE_HALF) & (lane < ROPE_DIM) & on, sn, 0.0)


def _rope_dilate(z, pos, invf, tm):
    t = z.shape[0]

    def body(z_ref, pos_ref, invf_ref, o0, o1, o2, c_s, s1_s, s2_s, rot):
        _rope_tables(pos_ref, invf_ref, c_s, s1_s, s2_s, pl.program_id(1) < 2)
        cc, s1, s2 = c_s[...], s1_s[...], s2_s[...]
        for h in range(QK_W // HEAD_DIM):
            xv = z_ref[:, h * HEAD_DIM:(h + 1) * HEAD_DIM]
            rot[h] = xv * cc + pltpu.roll(xv, HEAD_DIM - ROPE_HALF, 1) * s1 + pltpu.roll(xv, ROPE_HALF, 1) * s2
        for g, (d, o_ref) in enumerate(zip(DILATIONS, (o0, o1, o2))):
            n = tm // d
            for r in range(d):
                for hh in range(HEADS_PER_GROUP):
                    oc = r * GROUP_W + hh * HEAD_DIM
                    o_ref[:, oc:oc + HEAD_DIM] = rot[g * HEADS_PER_GROUP + hh, _strided(r, n, d), :].astype(BF16)

    return pl.pallas_call(
        body, grid=(t // tm, 3),
        in_specs=[_bs((tm, QK_W), lambda i, c: (i, c)), _bs((tm, 1), lambda i, c: (i, 0)), _bs((1, HEAD_DIM), lambda i, c: (0, 0))],
        out_specs=[_bs((None, tm // d, d * GROUP_W), lambda i, c: (c, i, 0)) for d in DILATIONS],
        out_shape=[SDS((3, t // d, d * GROUP_W), BF16) for d in DILATIONS],
        scratch_shapes=[pltpu.VMEM((tm, HEAD_DIM), F32)] * 3 + [pltpu.VMEM((QK_W // HEAD_DIM, tm, HEAD_DIM), F32)],
        compiler_params=_cp(2), name="rope_dilate")(z, pos, invf)


def _band_mask(first):
    qi = lax.broadcasted_iota(jnp.int32, (BLK, 2 * BLK), 0)
    kj = lax.broadcasted_iota(jnp.int32, (BLK, 2 * BLK), 1)
    return (kj >= qi) & (kj <= qi + BLK) & ((kj >= BLK) | jnp.logical_not(first))


def _attn_fwd(qkv, d, qt):
    ell = qkv.shape[1]
    nsub = qt // BLK
    scale = 1.0 / math.sqrt(HEAD_DIM)

    def body(q_ref, kc_ref, kp_ref, vc_ref, vp_ref, o_ref, lse_ref, kcat, vcat):
        nb = pl.program_id(1)
        kcat[0:BLK, :] = kp_ref[...]
        kcat[BLK:, :] = kc_ref[...]
        vcat[0:BLK, :] = vp_ref[...]
        vcat[BLK:, :] = vc_ref[...]
        lane = lax.broadcasted_iota(jnp.int32, (BLK, HEAD_DIM), 1)
        for b in range(nsub):
            valid = _band_mask((nb == 0) if b == 0 else False)
            lse_t = jnp.zeros((BLK, HEAD_DIM), F32)
            for hh in range(HEADS_PER_GROUP):
                cs = slice(hh * HEAD_DIM, (hh + 1) * HEAD_DIM)
                qb = q_ref[b * BLK:(b + 1) * BLK, cs]
                kk = kcat[b * BLK:(b + 2) * BLK, cs]
                vv = vcat[b * BLK:(b + 2) * BLK, cs]
                s = lax.dot_general(qb, kk, _DNUMS["nt"], preferred_element_type=F32) * scale
                s = jnp.where(valid, s, NEG)
                mx = jnp.max(s, axis=-1, keepdims=True)
                p = jnp.exp(s - mx)
                den = jnp.sum(p, axis=-1, keepdims=True)
                o = jnp.dot(p.astype(BF16), vv, preferred_element_type=F32) / den
                o_ref[b * BLK:(b + 1) * BLK, cs] = o
                lse_t = jnp.where(lane == hh, mx + jnp.log(den), lse_t)
            lse_ref[b * BLK:(b + 1) * BLK, :] = lse_t

    cur = lambda c: _bs((None, qt, GROUP_W), lambda r, nb: (c, nb, r))
    prev = lambda c: _bs((None, BLK, GROUP_W), lambda r, nb: (c, jnp.maximum(nb * nsub - 1, 0), r))
    return pl.pallas_call(
        body, grid=(d, ell // qt), in_specs=[cur(0), cur(1), prev(1), cur(2), prev(2)],
        out_specs=[_bs((qt, GROUP_W), lambda r, nb: (nb, r)), _bs((None, qt, HEAD_DIM), lambda r, nb: (r, nb, 0))],
        out_shape=[SDS((ell, d * GROUP_W), F32), SDS((d, ell, HEAD_DIM), F32)],
        scratch_shapes=[pltpu.VMEM((qt + BLK, GROUP_W), BF16)] * 2, compiler_params=_cp(2), name=f"attn_fwd_d{d}")(
            qkv, qkv, qkv, qkv, qkv)


def _attn_merge(outs, lses, tm):
    t = outs[0].shape[0]

    def body(o0, o1, o2, l0, l1, l2, attn_ref, attn_bf_ref, t0, t1, t2, so, sl, lt_s):
        for g, (d, o_ref, l_ref) in enumerate(zip(DILATIONS, (o0, o1, o2), (l0, l1, l2))):
            n = tm // d
            for r in range(d):
                rows = _strided(r, n, d)
                for hh in range(HEADS_PER_GROUP):
                    oc = r * GROUP_W + hh * HEAD_DIM
                    so[g * HEADS_PER_GROUP + hh, rows, :] = o_ref[:, oc:oc + HEAD_DIM]
                sl[g, rows, :] = l_ref[r]
        ls = [sl[g] for g in range(N_GROUPS)]
        mx = jnp.maximum(jnp.maximum(ls[0], ls[1]), ls[2])
        es = [jnp.exp(l - mx) for l in ls]
        den = es[0] + es[1] + es[2]
        ws = [e / den for e in es]
        lt_s[...] = mx + jnp.log(den)
        for hh in range(HEADS_PER_GROUP):
            cs = slice(hh * HEAD_DIM, (hh + 1) * HEAD_DIM)
            a = ws[0][:, hh:hh + 1] * so[hh]
            for g in range(1, N_GROUPS):
                a = a + ws[g][:, hh:hh + 1] * so[g * HEADS_PER_GROUP + hh]
            attn_ref[:, cs] = a
            attn_bf_ref[:, cs] = a.astype(BF16)
        for d, t_ref in zip(DILATIONS, (t0, t1, t2)):
            n = tm // d
            for r in range(d):
                t_ref[r] = lt_s[_strided(r, n, d), :]

    dil = lambda d: _bs((tm // d, d * GROUP_W), lambda i: (i, 0))
    lsp = lambda d: _bs((d, tm // d, HEAD_DIM), lambda i: (0, i, 0))
    row = _bs((tm, GROUP_W), lambda i: (i, 0))
    return pl.pallas_call(
        body, grid=(t // tm,),
        in_specs=[dil(d) for d in DILATIONS] + [lsp(d) for d in DILATIONS],
        out_specs=[row, row] + [lsp(d) for d in DILATIONS],
        out_shape=[SDS((t, GROUP_W), F32), SDS((t, GROUP_W), BF16)] + [SDS(l.shape, F32) for l in lses],
        scratch_shapes=[pltpu.VMEM((N_GROUPS * HEADS_PER_GROUP, tm, HEAD_DIM), F32), pltpu.VMEM((N_GROUPS, tm, HEAD_DIM), F32),
                        pltpu.VMEM((tm, HEAD_DIM), F32)],
        compiler_params=_cp(1), name="attn_merge")(*outs, *lses)


def _attn_bwd_pre(d_attn, attn, tm):
    t = attn.shape[0]

    def body(da_ref, a_ref, g0, g1, g2, e0, e1, e2, dl_s, da_s):
        lane = lax.broadcasted_iota(jnp.int32, (tm, HEAD_DIM), 1)
        dl = jnp.zeros((tm, HEAD_DIM), F32)
        for hh in range(HEADS_PER_GROUP):
            cs = slice(hh * HEAD_DIM, (hh + 1) * HEAD_DIM)
            dav = da_ref[:, cs]
            da_s[hh] = dav
            dl = jnp.where(lane == hh, jnp.sum(dav * a_ref[:, cs], axis=-1, keepdims=True), dl)
        dl_s[...] = dl
        for d, g_ref, e_ref in zip(DILATIONS, (g0, g1, g2), (e0, e1, e2)):
            n = tm // d
            for r in range(d):
                rows = _strided(r, n, d)
                for hh in range(HEADS_PER_GROUP):
                    oc = r * GROUP_W + hh * HEAD_DIM
                    g_ref[:, oc:oc + HEAD_DIM] = da_s[hh, rows, :].astype(BF16)
                e_ref[r] = dl_s[rows, :]

    row = _bs((tm, GROUP_W), lambda i: (i, 0))
    return pl.pallas_call(
        body, grid=(t // tm,), in_specs=[row, row],
        out_specs=[_bs((tm // d, d * GROUP_W), lambda i: (i, 0)) for d in DILATIONS]
        + [_bs((d, tm // d, HEAD_DIM), lambda i: (0, i, 0)) for d in DILATIONS],
        out_shape=[SDS((t // d, d * GROUP_W), BF16) for d in DILATIONS]
        + [SDS((d, t // d, HEAD_DIM), F32) for d in DILATIONS],
        scratch_shapes=[pltpu.VMEM((tm, HEAD_DIM), F32), pltpu.VMEM((HEADS_PER_GROUP, tm, HEAD_DIM), F32)],
        compiler_params=_cp(1), name="attn_bwd_pre")(d_attn, attn)


def _attn_bwd(qkv, d_a, lt, delta, d, qt):
    ell = qkv.shape[1]
    nsub = qt // BLK
    ntile = ell // qt
    nblk = ell // BLK
    scale = 1.0 / math.sqrt(HEAD_DIM)

    def body(q_ref, qn_ref, kc_ref, kp_ref, vc_ref, vp_ref, da_ref, dan_ref, lt_ref, ltn_ref, dl_ref, dln_ref, o_ref,
             kcat, vcat, dk_acc, dv_acc):
        nb = pl.program_id(1)
        kcat[0:BLK, :] = kp_ref[...]
        kcat[BLK:, :] = kc_ref[...]
        vcat[0:BLK, :] = vp_ref[...]
        vcat[BLK:, :] = vc_ref[...]
        qi = lax.broadcasted_iota(jnp.int32, (BLK, BLK), 0)
        kj = lax.broadcasted_iota(jnp.int32, (BLK, BLK), 1)
        valid_next = (kj >= qi) & (nb < ntile - 1)
        for hh in range(HEADS_PER_GROUP):
            cs = slice(hh * HEAD_DIM, (hh + 1) * HEAD_DIM)
            dk_acc[...] = jnp.zeros_like(dk_acc)
            dv_acc[...] = jnp.zeros_like(dv_acc)
            for b in range(nsub):
                rs = slice(b * BLK, (b + 1) * BLK)
                ks = slice(b * BLK, (b + 2) * BLK)
                valid = _band_mask((nb == 0) if b == 0 else False)
                qb, kk, vv, dab = q_ref[rs, cs], kcat[ks, cs], vcat[ks, cs], da_ref[rs, cs]
                s = lax.dot_general(qb, kk, _DNUMS["nt"], preferred_element_type=F32) * scale
                p = jnp.where(valid, jnp.exp(s - lt_ref[rs, hh:hh + 1]), 0.0)
                dp = lax.dot_general(dab, vv, _DNUMS["nt"], preferred_element_type=F32)
                ds = (p * (dp - dl_ref[rs, hh:hh + 1])).astype(BF16)
                o_ref[0, rs, cs] = jnp.dot(ds, kk, preferred_element_type=F32) * scale
                dk_acc[ks, :] += lax.dot_general(ds, qb, _DNUMS["tn"], preferred_element_type=F32) * scale
                dv_acc[ks, :] += lax.dot_general(p.astype(BF16), dab, _DNUMS["tn"], preferred_element_type=F32)
            ks = slice(nsub * BLK, (nsub + 1) * BLK)
            qn, kl, vl, dan = qn_ref[:, cs], kcat[ks, cs], vcat[ks, cs], dan_ref[:, cs]
            s = lax.dot_general(qn, kl, _DNUMS["nt"], preferred_element_type=F32) * scale
            p = jnp.where(valid_next, jnp.exp(s - ltn_ref[:, hh:hh + 1]), 0.0)
            dp = lax.dot_general(dan, vl, _DNUMS["nt"], preferred_element_type=F32)
            ds = (p * (dp - dln_ref[:, hh:hh + 1])).astype(BF16)
            dk_acc[ks, :] += lax.dot_general(ds, qn, _DNUMS["tn"], preferred_element_type=F32) * scale
            dv_acc[ks, :] += lax.dot_general(p.astype(BF16), dan, _DNUMS["tn"], preferred_element_type=F32)
            o_ref[1, :, cs] = dk_acc[BLK:, :]
            o_ref[2, :, cs] = dv_acc[BLK:, :]

    nxt = lambda nb: jnp.minimum((nb + 1) * nsub, nblk - 1)
    prv = lambda nb: jnp.maximum(nb * nsub - 1, 0)
    cur3 = lambda c: _bs((None, qt, GROUP_W), lambda r, nb: (c, nb, r))
    in_specs = [
        cur3(0), _bs((None, BLK, GROUP_W), lambda r, nb: (0, nxt(nb), r)),
        cur3(1), _bs((None, BLK, GROUP_W), lambda r, nb: (1, prv(nb), r)),
        cur3(2), _bs((None, BLK, GROUP_W), lambda r, nb: (2, prv(nb), r)),
        _bs((qt, GROUP_W), lambda r, nb: (nb, r)), _bs((BLK, GROUP_W), lambda r, nb: (nxt(nb), r)),
        _bs((None, qt, HEAD_DIM), lambda r, nb: (r, nb, 0)), _bs((None, BLK, HEAD_DIM), lambda r, nb: (r, nxt(nb), 0)),
        _bs((None, qt, HEAD_DIM), lambda r, nb: (r, nb, 0)), _bs((None, BLK, HEAD_DIM), lambda r, nb: (r, nxt(nb), 0)),
    ]
    return pl.pallas_call(
        body, grid=(d, ntile), in_specs=in_specs, out_specs=_bs((3, qt, GROUP_W), lambda r, nb: (0, nb, r)),
        out_shape=SDS((3, ell, d * GROUP_W), F32),
        scratch_shapes=[pltpu.VMEM((qt + BLK, GROUP_W), BF16)] * 2 + [pltpu.VMEM((qt + BLK, HEAD_DIM), F32)] * 2,
        compiler_params=_cp(2), name=f"attn_bwd_d{d}")(qkv, qkv, qkv, qkv, qkv, qkv, d_a, d_a, lt, lt, delta, delta)


def _undilate_rope_bwd(dqkvs, pos, invf, tm):
    t = pos.shape[0]

    def body(g0, g1, g2, pos_ref, invf_ref, o_ref, c_s, s1_s, s2_s, nat):
        _rope_tables(pos_ref, invf_ref, c_s, s1_s, s2_s, pl.program_id(1) < 2)
        for g, (d, g_ref) in enumerate(zip(DILATIONS, (g0, g1, g2))):
            n = tm // d
            for r in range(d):
                for hh in range(HEADS_PER_GROUP):
                    oc = r * GROUP_W + hh * HEAD_DIM
                    nat[g * HEADS_PER_GROUP + hh, _strided(r, n, d), :] = g_ref[:, oc:oc + HEAD_DIM]
        cc, s1, s2 = c_s[...], s1_s[...], s2_s[...]
        for h in range(QK_W // HEAD_DIM):
            xv = nat[h]
            y = xv * cc - pltpu.roll(xv, HEAD_DIM - ROPE_HALF, 1) * s1 - pltpu.roll(xv, ROPE_HALF, 1) * s2
            o_ref[:, h * HEAD_DIM:(h + 1) * HEAD_DIM] = y.astype(BF16)

    return pl.pallas_call(
        body, grid=(t // tm, 3),
        in_specs=[_bs((None, tm // d, d * GROUP_W), lambda i, c: (c, i, 0)) for d in DILATIONS]
        + [_bs((tm, 1), lambda i, c: (i, 0)), _bs((1, HEAD_DIM), lambda i, c: (0, 0))],
        out_specs=_bs((tm, QK_W), lambda i, c: (i, c)), out_shape=SDS((t, 3 * QK_W), BF16),
        scratch_shapes=[pltpu.VMEM((tm, HEAD_DIM), F32)] * 3 + [pltpu.VMEM((QK_W // HEAD_DIM, tm, HEAD_DIM), F32)],
        compiler_params=_cp(2), name="undilate_rope_bwd")(*dqkvs, pos, invf)


def _cmul(ar, ai, br, bi):
    return ar * br - ai * bi, ar * bi + ai * br


def _ssm_disc(a_re, a_im, log_dt, nsq):
    def body(lr_ref, li_ref, ldt_ref, br_ref, bi_ref, zr_ref, zi_ref, pr_ref, pi_ref):
        lr, li = lr_ref[...], li_ref[...]
        dt = jnp.exp(ldt_ref[...])
        mag = jnp.exp(lr * dt)
        bar_re, bar_im = mag * jnp.cos(li * dt), mag * jnp.sin(li * dt)
        nr, ni = bar_re - 1.0, bar_im
        den = lr * lr + li * li
        br_ref[...], bi_ref[...] = bar_re, bar_im
        zr_ref[...] = (nr * lr + ni * li) / den
        zi_ref[...] = (ni * lr - nr * li) / den
        pr, pi = bar_re, bar_im
        for _ in range(nsq):
            pr, pi = _cmul(pr, pi, pr, pi)
        pr_ref[...], pi_ref[...] = pr, pi

    return pl.pallas_call(body, out_shape=[SDS(a_re.shape, F32)] * 6, name="ssm_discretise")(a_re, a_im, log_dt)


def _ssm_scale_b(z_re, z_im, b_re, b_im):
    def body(zr_ref, zi_ref, br_ref, bi_ref, or_ref, oi_ref):
        zr, zi, br, bi = zr_ref[...], zi_ref[...], br_ref[...], bi_ref[...]
        or_ref[...] = zr * br - zi * bi
        oi_ref[...] = zr * bi + zi * br

    return pl.pallas_call(body, out_shape=[SDS(b_re.shape, F32)] * 2, name="ssm_scale_b")(z_re, z_im, b_re, b_im)


def _ssm_scale_b_bwd(z_re, z_im, b_re, b_im, g_re, g_im):
    def body(zr_ref, zi_ref, br_ref, bi_ref, gr_ref, gi_ref, dbr_ref, dbi_ref, dzr_ref, dzi_ref):
        zr, zi, br, bi, gr, gi = zr_ref[...], zi_ref[...], br_ref[...], bi_ref[...], gr_ref[...], gi_ref[...]
        dbr_ref[...] = zr * gr + zi * gi
        dbi_ref[...] = zr * gi - zi * gr
        dzr_ref[...] = jnp.sum(br * gr + bi * gi, axis=-1, keepdims=True)
        dzi_ref[...] = jnp.sum(br * gi - bi * gr, axis=-1, keepdims=True)

    return pl.pallas_call(body, out_shape=[SDS(b_re.shape, F32)] * 2 + [SDS(z_re.shape, F32)] * 2,
                          name="ssm_scale_b_bwd")(z_re, z_im, b_re, b_im, g_re, g_im)


def _ssm_disc_bwd(a_re, a_im, log_dt, gb_re, gb_im, gz_re, gz_im):
    def body(lr_ref, li_ref, ldt_ref, gbr_ref, gbi_ref, gzr_ref, gzi_ref, dar_ref, dai_ref, dldt_ref):
        lr, li = lr_ref[...], li_ref[...]
        dt = jnp.exp(ldt_ref[...])
        mag = jnp.exp(lr * dt)
        bar_re, bar_im = mag * jnp.cos(li * dt), mag * jnp.sin(li * dt)
        nr, ni = bar_re - 1.0, bar_im
        den = lr * lr + li * li
        zr, zi = (nr * lr + ni * li) / den, (ni * lr - nr * li) / den
        gzr, gzi = gzr_ref[...], gzi_ref[...]
        gbr = gbr_ref[...] + (lr * gzr - li * gzi) / den
        gbi = gbi_ref[...] + (lr * gzi + li * gzr) / den
        qr, qi = (zr * lr + zi * li) / den, (zi * lr - zr * li) / den
        dar_ref[...] = dt * (bar_re * gbr + bar_im * gbi) - qr * gzr - qi * gzi
        dai_ref[...] = dt * (bar_re * gbi - bar_im * gbr) - qr * gzi + qi * gzr
        wr, wi = lr * bar_re - li * bar_im, lr * bar_im + li * bar_re
        dldt_ref[...] = dt * jnp.sum(wr * gbr + wi * gbi, axis=-1, keepdims=True)

    return pl.pallas_call(body, out_shape=[SDS(a_re.shape, F32)] * 2 + [SDS(log_dt.shape, F32)],
                          name="ssm_discretise_bwd")(a_re, a_im, log_dt, gb_re, gb_im, gz_re, gz_im)


def _permute_u(z, ucol_block, tm):
    t = z.shape[0]
    seg = t // N_DEV
    z3 = z.reshape(N_DEV, seg, z.shape[1])

    def body(z_ref, u_ref, ub_ref, tmp):
        for n in range(SSM_W // BLK):
            for j in range(N_DEV):
                tmp[n, pl.ds(j, tm // N_DEV, stride=N_DEV), :] = z_ref[j, :, n * BLK:(n + 1) * BLK]
            u_ref[:, n * BLK:(n + 1) * BLK] = tmp[n]
            ub_ref[:, n * BLK:(n + 1) * BLK] = tmp[n].astype(BF16)

    row = _bs((tm, SSM_W), lambda i: (i, 0))
    return pl.pallas_call(
        body, grid=(t // tm,), in_specs=[_bs((N_DEV, tm // N_DEV, SSM_W), lambda i: (0, i, ucol_block))],
        out_specs=[row, row], out_shape=[SDS((t, SSM_W), F32), SDS((t, SSM_W), BF16)],
        scratch_shapes=[pltpu.VMEM((SSM_W // BLK, tm, BLK), F32)], compiler_params=_cp(1), name="permute_u")(z3)


def _drive(src_ref, mat_ref, dst, mode):
    for kn in range(2 * SSM_NB):
        n = kn % SSM_NB
        a = src_ref[:, n * BLK:(n + 1) * BLK]
        dst[:, kn * 512:(kn + 1) * 512] = lax.dot_general(a, mat_ref[kn], _DNUMS[mode], preferred_element_type=F32)


def _scan_chunk(src, lam_ref, carry, *, reverse, store=None, h_ref=None, acc=None):
    steps = src.shape[0] // 8
    for c in range(NSTATE // SCAN_LANES):
        re = slice(c * SCAN_LANES, (c + 1) * SCAN_LANES)
        im = slice(NSTATE + c * SCAN_LANES, NSTATE + (c + 1) * SCAN_LANES)
        ar, ai = lam_ref[:, re], lam_ref[:, im]

        def step(s, val):
            i = (steps - 1 - s) if reverse else s
            rows = pl.ds(pl.multiple_of(i * 8, 8), 8)
            if acc is not None:
                hr, hi, dr, di = val
                pr, pi = h_ref[rows, re], h_ref[rows, im]
                dr = dr + hr * pr + hi * pi
                di = di + hi * pr - hr * pi
            else:
                hr, hi = val
            nr = ar * hr - ai * hi + src[rows, re]
            ni = ar * hi + ai * hr + src[rows, im]
            if store is not None:
                store[rows, re] = nr
                store[rows, im] = ni
            return (nr, ni, dr, di) if acc is not None else (nr, ni)

        init = (carry[:, re], carry[:, im])
        if acc is not None:
            init = init + (acc[:, re], acc[:, im])
        out = lax.fori_loop(0, steps, step, init, unroll=4)
        carry[:, re], carry[:, im] = out[0], out[1]
        if acc is not None:
            acc[:, re], acc[:, im] = out[2], out[3]


def _segment_carries(e_ref, pw_ref, out_ref, reverse):
    pr, pi = pw_ref[:, 0:NSTATE], pw_ref[:, NSTATE:]
    hr = jnp.zeros((1, NSTATE), F32)
    hi = jnp.zeros((1, NSTATE), F32)
    order = range(N_DEV - 1, -1, -1) if reverse else range(N_DEV)
    for j in order:
        out_ref[j:j + 1, 0:NSTATE] = hr
        out_ref[j:j + 1, NSTATE:] = hi
        tr, ti = _cmul(pr, pi, hr, hi)
        hr, hi = e_ref[j:j + 1, 0:NSTATE] + tr, e_ref[j:j + 1, NSTATE:] + ti


def _ssm_carries(name, src, mat, mode, lam8, pw, reverse):
    t = src.shape[0]
    nchunk = t // SCAN_ROWS

    def body(src_ref, mat_ref, lam_ref, pw_ref, out_ref, drive, carry):
        c = pl.program_id(0)

        @pl.when(c == 0)
        def _():
            carry[...] = jnp.zeros_like(carry)

        _drive(src_ref, mat_ref, drive, mode)
        _scan_chunk(drive, lam_ref, carry, reverse=reverse)

        @pl.when(c == nchunk - 1)
        def _():
            _segment_carries(carry, pw_ref, out_ref, reverse)

    blk = (lambda c: (nchunk - 1 - c, 0)) if reverse else (lambda c: (c, 0))
    return pl.pallas_call(
        body, grid=(nchunk,),
        in_specs=[_bs((SCAN_ROWS, SSM_W), blk), _bs(mat.shape, lambda c: (0, 0, 0)), _bs((8, 2 * NSTATE), lambda c: (0, 0)),
                  _bs((1, 2 * NSTATE), lambda c: (0, 0))],
        out_specs=_bs((8, 2 * NSTATE), lambda c: (0, 0)), out_shape=SDS((8, 2 * NSTATE), F32),
        scratch_shapes=[pltpu.VMEM((SCAN_ROWS, 2 * NSTATE), F32), pltpu.VMEM((8, 2 * NSTATE), F32)],
        compiler_params=_cp(1), name=name)(src, mat, lam8, pw)


def _ssm_fwd(u_bf, bd, cd, lam8, start):
    t = u_bf.shape[0]
    nchunk = t // SCAN_ROWS

    def body(u_ref, bd_ref, cd_ref, lam_ref, start_ref, h_ref, y_ref, drive, carry):
        @pl.when(pl.program_id(0) == 0)
        def _():
            carry[...] = start_ref[...]

        _drive(u_ref, bd_ref, drive, "nn")
        _scan_chunk(drive, lam_ref, carry, reverse=False, store=h_ref)
        for n in range(SSM_NB):
            hr = h_ref[:, n * 512:(n + 1) * 512].astype(BF16)
            hi = h_ref[:, NSTATE + n * 512:NSTATE + (n + 1) * 512].astype(BF16)
            y_ref[:, n * BLK:(n + 1) * BLK] = (jnp.dot(hr, cd_ref[n], preferred_element_type=F32)
                                              + jnp.dot(hi, cd_ref[SSM_NB + n], preferred_element_type=F32))

    return pl.pallas_call(
        body, grid=(nchunk,),
        in_specs=[_bs((SCAN_ROWS, SSM_W), lambda c: (c, 0)), _bs(bd.shape, lambda c: (0, 0, 0)), _bs(cd.shape, lambda c: (0, 0, 0)),
                  _bs((8, 2 * NSTATE), lambda c: (0, 0)), _bs((8, 2 * NSTATE), lambda c: (0, 0))],
        out_specs=[_bs((SCAN_ROWS, 2 * NSTATE), lambda c: (c, 0)), _bs((SCAN_ROWS, SSM_W), lambda c: (c, 0))],
        out_shape=[SDS((t, 2 * NSTATE), F32), SDS((t, SSM_W), F32)],
        scratch_shapes=[pltpu.VMEM((SCAN_ROWS, 2 * NSTATE), F32), pltpu.VMEM((8, 2 * NSTATE), F32)],
        compiler_params=_cp(1), name="ssm_scan_fwd")(u_bf, bd, cd, lam8, start)


def _ssm_bwd(dys_bf, u_bf, h, bd, cd, lamc8, start):
    t = u_bf.shape[0]
    nchunk = t // SCAN_ROWS

    def body(dys_ref, u_ref, h_ref, bd_ref, cd_ref, lam_ref, start_ref, du_ref, dlam_ref, dbd_ref, dcd_ref, drive, adj, carry):
        c = pl.program_id(0)

        @pl.when(c == 0)
        def _():
            carry[...] = start_ref[...]
            dlam_ref[...] = jnp.zeros_like(dlam_ref)
            dbd_ref[...] = jnp.zeros_like(dbd_ref)
            dcd_ref[...] = jnp.zeros_like(dcd_ref)

        _drive(dys_ref, cd_ref, drive, "nt")
        _scan_chunk(drive, lam_ref, carry, reverse=True, store=adj, h_ref=h_ref, acc=dlam_ref)
        for n in range(SSM_NB):
            cs = slice(n * BLK, (n + 1) * BLK)
            acc = None
            for k in range(2):
                kn = k * SSM_NB + n
                ss = slice(kn * 512, (kn + 1) * 512)
                lam_b = adj[:, ss].astype(BF16)
                part = lax.dot_general(lam_b, bd_ref[kn], _DNUMS["nt"], preferred_element_type=F32)
                acc = part if acc is None else acc + part
                dbd_ref[kn] += lax.dot_general(u_ref[:, cs], lam_b, _DNUMS["tn"], preferred_element_type=F32)
                dcd_ref[kn] += lax.dot_general(h_ref[:, ss].astype(BF16), dys_ref[:, cs], _DNUMS["tn"],
                                               preferred_element_type=F32)
            du_ref[:, cs] = acc

    rev = lambda c: (nchunk - 1 - c, 0)
    const2 = lambda c: (0, 0)
    const3 = lambda c: (0, 0, 0)
    return pl.pallas_call(
        body, grid=(nchunk,),
        in_specs=[_bs((SCAN_ROWS, SSM_W), rev), _bs((SCAN_ROWS, SSM_W), rev), _bs((SCAN_ROWS, 2 * NSTATE), rev),
                  _bs(bd.shape, const3), _bs(cd.shape, const3), _bs((8, 2 * NSTATE), const2), _bs((8, 2 * NSTATE), const2)],
        out_specs=[_bs((SCAN_ROWS, SSM_W), rev), _bs((8, 2 * NSTATE), const2), _bs(bd.shape, const3), _bs(cd.shape, const3)],
        out_shape=[SDS((t, SSM_W), F32), SDS((8, 2 * NSTATE), F32), SDS(bd.shape, F32), SDS(cd.shape, F32)],
        scratch_shapes=[pltpu.VMEM((SCAN_ROWS, 2 * NSTATE), F32), pltpu.VMEM((SCAN_ROWS, 2 * NSTATE), F32),
                        pltpu.VMEM((8, 2 * NSTATE), F32)],
        compiler_params=_cp(1), name="ssm_scan_bwd")(dys_bf, u_bf, h, bd, cd, lamc8, start)


def _gelu_parts(x):
    c0 = math.sqrt(2.0 / math.pi)
    inner = c0 * (x + 0.044715 * x * x * x)
    th = jnp.tanh(inner)
    val = 0.5 * x * (1.0 + th)
    grad = 0.5 * (1.0 + th) + 0.5 * x * (1.0 - th * th) * c0 * (1.0 + 3.0 * 0.044715 * x * x)
    return val, grad


def _ssm_out(y_raw, u, d_skip, tm):
    t = u.shape[0]
    seg = t // N_DEV

    def body(y_ref, u_ref, d_ref, ys_ref, yg_ref, tmp):
        ys = y_ref[...] + d_ref[...] * u_ref[...]
        ys_ref[...] = ys
        yg = _gelu_parts(ys)[0]
        for n in range(SSM_W // BLK):
            tmp[n] = yg[:, n * BLK:(n + 1) * BLK]
            for j in range(N_DEV):
                yg_ref[j, :, n * BLK:(n + 1) * BLK] = tmp[n, pl.ds(j, tm // N_DEV, stride=N_DEV), :].astype(BF16)

    row = _bs((tm, SSM_W), lambda i: (i, 0))
    ys, yg = pl.pallas_call(
        body, grid=(t // tm,), in_specs=[row, row, _bs((1, SSM_W), lambda i: (0, 0))],
        out_specs=[row, _bs((N_DEV, tm // N_DEV, SSM_W), lambda i: (0, i, 0))],
        out_shape=[SDS((t, SSM_W), F32), SDS((N_DEV, seg, SSM_W), BF16)],
        scratch_shapes=[pltpu.VMEM((SSM_W // BLK, tm, BLK), F32)], compiler_params=_cp(1), name="ssm_out")(y_raw, u, d_skip)
    return ys, yg.reshape(t, SSM_W)


def _ssm_out_bwd(d_yg, ys, u, tm):
    t = u.shape[0]
    seg = t // N_DEV

    def body(dg_ref, ys_ref, u_ref, dys_ref, dysb_ref, dd_ref, tmp):
        for n in range(SSM_W // BLK):
            for j in range(N_DEV):
                tmp[n, pl.ds(j, tm // N_DEV, stride=N_DEV), :] = dg_ref[j, :, n * BLK:(n + 1) * BLK]
        dyg = jnp.concatenate([tmp[n] for n in range(SSM_W // BLK)], axis=1)
        dys = dyg * _gelu_parts(ys_ref[...])[1]
        dys_ref[...] = dys
        dysb_ref[...] = dys.astype(BF16)
        part = jnp.sum(dys * u_ref[...], axis=0, keepdims=True)

        @pl.when(pl.program_id(0) == 0)
        def _():
            dd_ref[...] = part

        @pl.when(pl.program_id(0) > 0)
        def _():
            dd_ref[...] += part

    row = _bs((tm, SSM_W), lambda i: (i, 0))
    return pl.pallas_call(
        body, grid=(t // tm,), in_specs=[_bs((N_DEV, tm // N_DEV, SSM_W), lambda i: (0, i, 0)), row, row],
        out_specs=[row, row, _bs((1, SSM_W), lambda i: (0, 0))],
        out_shape=[SDS((t, SSM_W), F32), SDS((t, SSM_W), BF16), SDS((1, SSM_W), F32)],
        scratch_shapes=[pltpu.VMEM((SSM_W // BLK, tm, BLK), F32)], compiler_params=_cp(1), name="ssm_out_bwd")(
            d_yg.reshape(N_DEV, seg, SSM_W), ys, u)


def _du_to_dz(du_raw, dys, d_skip, tm):
    t = du_raw.shape[0]
    seg = t // N_DEV

    def body(du_ref, dys_ref, d_ref, o_ref, tmp):
        du = du_ref[...] + d_ref[...] * dys_ref[...]
        for n in range(SSM_W // BLK):
            tmp[n] = du[:, n * BLK:(n + 1) * BLK]
            for j in range(N_DEV):
                o_ref[j, :, n * BLK:(n + 1) * BLK] = tmp[n, pl.ds(j, tm // N_DEV, stride=N_DEV), :].astype(BF16)

    row = _bs((tm, SSM_W), lambda i: (i, 0))
    out = pl.pallas_call(
        body, grid=(t // tm,), in_specs=[row, row, _bs((1, SSM_W), lambda i: (0, 0))],
        out_specs=_bs((N_DEV, tm // N_DEV, SSM_W), lambda i: (0, i, 0)), out_shape=SDS((N_DEV, seg, SSM_W), BF16),
        scratch_shapes=[pltpu.VMEM((SSM_W // BLK, tm, BLK), F32)], compiler_params=_cp(1), name="du_to_dz")(du_raw, dys, d_skip)
    return out.reshape(t, SSM_W)


def _block_diag(blocks):
    nb, ng, r, c = blocks.shape
    eye = jnp.eye(ng, dtype=blocks.dtype)
    return (blocks[:, :, :, None, :] * eye[None, :, None, :, None]).reshape(nb, ng * r, ng * c)


def _diag_blocks(full, r, c):
    k, nb = full.shape[:2]
    ng = full.shape[2] // r
    x = full.reshape(k, nb, ng, r, ng, c)
    eye = jnp.eye(ng, dtype=full.dtype)
    return jnp.sum(x * eye[None, None, :, None, :, None], axis=4).reshape(k, nb * ng, r, c)


_SMALL = ("g_mix", "a_re", "a_im", "log_dt", "b_re", "b_im", "c_re", "c_im", "d_skip", "g_ffn", "g_final")


def _pack_small(arrs):
    flat = jnp.concatenate([a.reshape(-1) for a in arrs])
    pad = (-flat.shape[0]) % (8 * 128)
    return jnp.pad(flat, (0, pad)).reshape(-1, 128)


def _unpack_small(packed, shapes):
    flat = packed.reshape(-1)
    out, off = [], 0
    for s in shapes:
        n = math.prod(s)
        out.append(flat[off:off + n].reshape(s))
        off += n
    return out


def kernel(x, p, positions, g_mix, w_in, a_re, a_im, log_dt, b_re, b_im, c_re, c_im, d_skip, w_attn_proj, w_glu_a, w_glu_b, w_out, g_ffn, w_ffn_gate, w_ffn_up, w_ffn_down, w_ple_gate, w_ple_proj, g_final, loss_target, m_g_mix, m_w_in, m_a_re, m_a_im, m_log_dt, m_b_re, m_b_im, m_c_re, m_c_im, m_d_skip, m_w_attn_proj, m_w_glu_a, m_w_glu_b, m_w_out, m_g_ffn, m_w_ffn_gate, m_w_ffn_up, m_w_ffn_down, m_w_ple_gate, m_w_ple_proj, m_g_final, v_g_mix, v_w_in, v_a_re, v_a_im, v_log_dt, v_b_re, v_b_im, v_c_re, v_c_im, v_d_skip, v_w_attn_proj, v_w_glu_a, v_w_glu_b, v_w_out, v_g_ffn, v_w_ffn_gate, v_w_ffn_up, v_w_ffn_down, v_w_ple_gate, v_w_ple_proj, v_g_final):
    args = dict(locals())
    t, d = x.shape[1], x.shape[2]
    inw = w_in.shape[2] * N_DEV
    fs = w_ffn_gate.shape[2]
    ff = fs * N_DEV
    ple = w_ple_proj.shape[1]
    seg = t // N_DEV
    assert inw == 3 * QK_W + SSM_W + 2 * d and t % (N_DEV * SCAN_ROWS // 8) == 0 and seg & (seg - 1) == 0
    tm = min(1024, t)
    te = min(512, t)
    tk = min(512, t)
    ucol = (3 * QK_W) // SSM_W
    gcol = (3 * QK_W + SSM_W) // d
    assert (3 * QK_W + SSM_W) % d == 0

    x2, p2, tgt = x[0], p[0, 0], loss_target[0]
    pos = positions.reshape(t, 1)
    inv = ROPE_THETA ** (-jnp.arange(ROPE_HALF, dtype=F32) * 2.0 / ROPE_DIM)
    invf = jnp.concatenate([inv, inv, jnp.zeros((HEAD_DIM - ROPE_DIM,), F32)]).reshape(1, HEAD_DIM)

    wnames = ("w_in", "w_attn_proj", "w_glu_a", "w_glu_b", "w_out", "w_ffn_gate", "w_ffn_up", "w_ffn_down", "w_ple_gate",
              "w_ple_proj")
    kinds = ("cols", "cols", "cols", "cols", "rows", "slot", "slot", "rows", "rows", "cols")
    shards = [args[n][0].astype(BF16) for n in wnames]
    W_in, W_ap, W_ga, W_gb, W_out, W_fg, W_fu, W_fd, W_pg, W_pp = _all_gather(shards, kinds)
    W_fg = jnp.swapaxes(W_fg, 0, 1).reshape(d, ff)
    W_fu = jnp.swapaxes(W_fu, 0, 1).reshape(d, ff)

    row_d = _bs((tm, d), lambda i, j, k: (i, 0))
    row_e = _bs((te, d), lambda i, j, k: (i, 0))
    vec_d = _bs((1, d), lambda i, j, k: (0, 0))
    sq_w = _bs((d, d), lambda i, j, k: (0, 0))
    n1 = _rms_fwd("norm_mix", x2, g_mix, tm)
    tn_in = _pick(inw, (1024, 512, 256, 128))
    z, = _mm("z_proj", (t // tm, inw // tn_in, 1),
             [("nn", n1, row_d, W_in, _bs((d, tn_in), lambda i, j, k: (0, j)))],
             [(SDS((t, inw), F32), _bs((tm, tn_in), lambda i, j, k: (i, j)))])

    qkv = _rope_dilate(z, pos, invf, tm)
    outs, lses = [], []
    for g, dil in enumerate(DILATIONS):
        o_g, l_g = _attn_fwd(qkv[g], dil, min(512, t // dil))
        outs.append(o_g)
        lses.append(l_g)
    merged = _attn_merge(outs, lses, te)
    attn, attn_bf, lts = merged[0], merged[1], merged[2:]

    nsq = seg.bit_length() - 1
    bar_re, bar_im, z_re, z_im, pw_re, pw_im = _ssm_disc(a_re[0], a_im[0], log_dt.reshape(SSM_GROUPS, 1), nsq)
    gp = SSM_GROUPS * SSM_STATE
    b_re2, b_im2 = b_re.reshape(gp, SSM_GROUP), b_im.reshape(gp, SSM_GROUP)
    bb_re, bb_im = _ssm_scale_b(z_re.reshape(gp, 1), z_im.reshape(gp, 1), b_re2, b_im2)

    def chunks(a, r, c):
        return a.reshape(SSM_NB, SSM_GROUPS // SSM_NB, r, c)

    bbt = lambda a: jnp.swapaxes(a.reshape(SSM_GROUPS, SSM_STATE, SSM_GROUP), 1, 2)
    bd = jnp.concatenate([_block_diag(chunks(bbt(bb_re), SSM_GROUP, SSM_STATE)),
                          _block_diag(chunks(bbt(bb_im), SSM_GROUP, SSM_STATE))]).astype(BF16)
    ct = lambda a: jnp.swapaxes(a[0], 1, 2)
    cd = jnp.concatenate([_block_diag(chunks(ct(c_re), SSM_STATE, SSM_GROUP)),
                          _block_diag(chunks(-ct(c_im), SSM_STATE, SSM_GROUP))]).astype(BF16)
    lam = jnp.concatenate([bar_re.reshape(1, gp), bar_im.reshape(1, gp)], axis=1)
    lamc = jnp.concatenate([bar_re.reshape(1, gp), -bar_im.reshape(1, gp)], axis=1)
    pw = jnp.concatenate([pw_re.reshape(1, gp), pw_im.reshape(1, gp)], axis=1)
    pwc = jnp.concatenate([pw_re.reshape(1, gp), -pw_im.reshape(1, gp)], axis=1)
    lam8, lamc8 = jnp.broadcast_to(lam, (8, 2 * gp)), jnp.broadcast_to(lamc, (8, 2 * gp))

    u_perm, u_bf = _permute_u(z, ucol, te)
    start_f = _ssm_carries("ssm_carries_fwd", u_bf, bd, "nn", lam8, pw, False)
    h_all, y_raw = _ssm_fwd(u_bf, bd, cd, lam8, start_f)
    dsk = d_skip.reshape(1, SSM_W)
    ys, yg_bf = _ssm_out(y_raw, u_perm, dsk, te)
    glu_w = _bs((SSM_W, d), lambda i, j, k: (0, 0))
    row_es = _bs((te, SSM_W), lambda i, j, k: (i, 0))
    gate_a = _bs((te, d), lambda i, j, k: (i, gcol))
    gate_s = _bs((te, d), lambda i, j, k: (i, gcol + 1))
    td_f32, td_bf = SDS((t, d), F32), SDS((t, d), BF16)
    m_bf, ya, yb, attn_d = _mm(
        "glu_merge", (t // te, 1, 1),
        [("nn", yg_bf, row_es, W_ga, glu_w), ("nn", yg_bf, row_es, W_gb, glu_w), ("nn", attn_bf, row_es, W_ap, glu_w)],
        [(td_bf, row_e), (td_f32, row_e), (td_f32, row_e), (td_f32, row_e)],
        extras=[(z, gate_a), (z, gate_s)], epilogue=_glu_merge_epilogue)

    h1, n2 = _mm("out_proj", (t // tm, 1, 1), [("nn", m_bf, row_d, W_out, sq_w)], [(td_f32, row_d), (td_bf, row_d)],
                 extras=[(x2, row_d), (g_ffn, vec_d)], epilogue=_out_norm_epilogue)

    tn_f = ff // 2
    nf = ff // tn_f
    hid_o = _bs((te, tn_f), lambda j, i, k: (i, j))
    tf_f32, tf_bf = SDS((t, ff), F32), SDS((t, ff), BF16)
    a_rows = _bs((te, d), lambda j, i, k: (i, 0))
    w_cols = _bs((d, tn_f), lambda j, i, k: (0, j))
    act, fg, fu = _mm("ffn_gate_up", (nf, t // te, 1), [("nn", n2, a_rows, W_fg, w_cols), ("nn", n2, a_rows, W_fu, w_cols)],
                      [(tf_bf, hid_o), (tf_f32, hid_o), (tf_f32, hid_o)], epilogue=_swiglu_epilogue)
    h2, h2_bf = _mm("ffn_down", (t // tm, 1, nf),
                    [("nn", act, _bs((tm, tn_f), lambda i, j, k: (i, k)), W_fd, _bs((tn_f, d), lambda i, j, k: (k, 0)))],
                    [(td_f32, row_d), (td_bf, row_d)], extras=[(h1, row_d)])

    loss_part, dg_final, dh3, dpp_bf, dpg_bf = _mm(
        "ple_head", (t // te, 1, 1),
        [("nn", h2_bf, row_e, W_pg, sq_w), ("nn", p2, _bs((te, ple), lambda i, j, k: (i, 0)), W_pp, _bs((ple, d), lambda i, j, k: (0, 0)))],
        [(SDS((1, 1), F32), _bs((1, 1), lambda i, j, k: (0, 0))), (SDS((1, d), F32), vec_d), (td_f32, row_e), (td_bf, row_e),
         (td_bf, row_e)],
        extras=[(h2, row_e), (g_final.reshape(1, d), vec_d), (tgt, row_e)], epilogue=_head_epilogue(t // te),
        scratch=[pltpu.VMEM((1, d), F32)])
    loss = lax.psum(loss_part[0, 0], ("x", "y", "c"))

    nkt = t // tk
    tok_a = lambda w: _bs((tk, w), lambda i, j, k: (k, 0))

    def wgrad(name, a, wa, b, wb):
        return _mm(name, (1, 1, nkt), [("tn", a, tok_a(wa), b, tok_a(wb))],
                   [(SDS((wa, wb), F32), _bs((wa, wb), lambda i, j, k: (0, 0)))])[0]

    dW_pp = wgrad("dw_ple_proj", p2, ple, dpp_bf, d)
    dW_pg = wgrad("dw_ple_gate", h2_bf, d, dpg_bf, d)
    dh2, dh2_bf = _mm("d_ple_gate", (t // tm, 1, 1), [("nt", dpg_bf, row_d, W_pg, sq_w)], [(td_f32, row_d), (td_bf, row_d)],
                      extras=[(dh3, row_d)])

    dfg_bf, dfu_bf = _mm("d_ffn_down", (nf, t // te, 1),
                         [("nt", dh2_bf, a_rows, W_fd, _bs((tn_f, d), lambda j, i, k: (j, 0)))],
                         [(tf_bf, hid_o), (tf_bf, hid_o)], extras=[(fg, hid_o), (fu, hid_o)], epilogue=_swiglu_bwd_epilogue)
    dW_fd, = _mm("dw_ffn_down", (nf, 1, nkt), [("tn", act, _bs((tk, tn_f), lambda i, j, k: (k, i)), dh2_bf, tok_a(d))],
                 [(SDS((ff, d), F32), _bs((tn_f, d), lambda i, j, k: (i, 0)))])
    hid_t = _bs((tk, tn_f), lambda i, j, k: (k, j))
    wg_o = [(SDS((d, ff), F32), _bs((d, tn_f), lambda i, j, k: (0, j)))]
    dW_fg, = _mm("dw_ffn_gate", (1, nf, nkt), [("tn", n2, tok_a(d), dfg_bf, hid_t)], wg_o)
    dW_fu, = _mm("dw_ffn_up", (1, nf, nkt), [("tn", n2, tok_a(d), dfu_bf, hid_t)], wg_o)
    dW_fg = jnp.swapaxes(dW_fg.reshape(d, N_DEV, fs), 0, 1)
    dW_fu = jnp.swapaxes(dW_fu.reshape(d, N_DEV, fs), 0, 1)
    hid_k = _bs((te, tn_f), lambda i, j, k: (i, k))
    w_k = _bs((d, tn_f), lambda i, j, k: (0, k))
    dh1, dh1_bf, dg_ffn = _mm("d_ffn_gate_up", (t // te, 1, nf), [("nt", dfg_bf, hid_k, W_fg, w_k), ("nt", dfu_bf, hid_k, W_fu, w_k)],
                              [(td_f32, row_e), (td_bf, row_e), (SDS((1, d), F32), vec_d)],
                              extras=[(h1, row_e), (g_ffn, vec_d), (dh2, row_e)], epilogue=_rms_bwd_epilogue)

    dW_out = wgrad("dw_out", m_bf, d, dh1_bf, d)
    dz_g, dad_bf, dya_bf, dyb_bf = _mm(
        "d_out_proj", (t // te, 1, 1), [("nt", dh1_bf, row_e, W_out, sq_w)],
        [(SDS((t, 2 * d), BF16), _bs((te, 2 * d), lambda i, j, k: (i, 0))), (td_bf, row_e), (td_bf, row_e), (td_bf, row_e)],
        extras=[(z, gate_a), (z, gate_s), (attn_d, row_e), (ya, row_e), (yb, row_e)], epilogue=_merge_bwd_epilogue)

    row_s = _bs((tm, SSM_W), lambda i, j, k: (i, 0))
    d_yg, = _mm("d_glu", (t // tm, 1, 1), [("nt", dya_bf, row_d, W_ga, glu_w), ("nt", dyb_bf, row_d, W_gb, glu_w)],
                [(SDS((t, SSM_W), F32), row_s)])
    dW_ga = wgrad("dw_glu_a", yg_bf, SSM_W, dya_bf, d)
    dW_gb = wgrad("dw_glu_b", yg_bf, SSM_W, dyb_bf, d)
    dys, dys_bf, dd_skip = _ssm_out_bwd(d_yg, ys, u_perm, te)
    start_b = _ssm_carries("ssm_carries_bwd", dys_bf, cd, "nt", lamc8, pwc, True)
    du_raw, dlam8, dbd, dcd = _ssm_bwd(dys_bf, u_bf, h_all, bd, cd, lamc8, start_b)
    dz_u = _du_to_dz(du_raw, dys, dsk, te)
    dlam = jnp.sum(dlam8, axis=0)
    dbb = _diag_blocks(dbd.reshape(2, SSM_NB, BLK, 512), SSM_GROUP, SSM_STATE)
    dbb_re = jnp.swapaxes(dbb[0], 1, 2).reshape(gp, SSM_GROUP)
    dbb_im = jnp.swapaxes(dbb[1], 1, 2).reshape(gp, SSM_GROUP)
    dcc = _diag_blocks(dcd.reshape(2, SSM_NB, 512, BLK), SSM_STATE, SSM_GROUP)
    dc_re, dc_im = jnp.swapaxes(dcc[0], 1, 2), -jnp.swapaxes(dcc[1], 1, 2)
    db_re, db_im, dz_re, dz_im = _ssm_scale_b_bwd(z_re.reshape(gp, 1), z_im.reshape(gp, 1), b_re2, b_im2, dbb_re, dbb_im)
    gshape = (SSM_GROUPS, SSM_STATE)
    da_re, da_im, dlog_dt = _ssm_disc_bwd(a_re[0], a_im[0], log_dt.reshape(SSM_GROUPS, 1), dlam[:gp].reshape(gshape),
                                          dlam[gp:].reshape(gshape), dz_re.reshape(gshape), dz_im.reshape(gshape))

    d_attn, = _mm("d_attn_proj", (t // tm, 1, 1), [("nt", dad_bf, row_d, W_ap, glu_w)], [(SDS((t, GROUP_W), F32), row_s)])
    dW_ap = wgrad("dw_attn_proj", attn_bf, GROUP_W, dad_bf, d)
    pre = _attn_bwd_pre(d_attn, attn, te)
    das, deltas = pre[:N_GROUPS], pre[N_GROUPS:]
    dqkvs = [_attn_bwd(qkv[g], das[g], lts[g], deltas[g], dil, min(512, t // dil)) for g, dil in enumerate(DILATIONS)]
    dz_qkv = _undilate_rope_bwd(dqkvs, pos, invf, tm)

    dW_in, = _mm("dw_in_qkv", (1, 3, nkt), [("tn", n1, tok_a(d), dz_qkv, _bs((tk, QK_W), lambda i, j, k: (k, j)))],
                 [(SDS((d, inw), F32), _bs((d, QK_W), lambda i, j, k: (0, j)))])
    dW_in, = _mm("dw_in_u", (1, 1, nkt), [("tn", n1, tok_a(d), dz_u, tok_a(SSM_W))],
                 [(SDS((d, inw), F32), _bs((d, SSM_W), lambda i, j, k: (0, ucol)))], alias_to_out0=dW_in)
    dW_in, = _mm("dw_in_gates", (1, 2, nkt), [("tn", n1, tok_a(d), dz_g, _bs((tk, d), lambda i, j, k: (k, j)))],
                 [(SDS((d, inw), F32), _bs((d, d), lambda i, j, k: (0, gcol + j)))], alias_to_out0=dW_in)
    dx, dg_mix = _mm(
        "d_z_proj", (t // te, 1, 3),
        [("nt", dz_qkv, _bs((te, QK_W), lambda i, j, k: (i, k)), W_in, _bs((d, QK_W), lambda i, j, k: (0, k)), 3),
         ("nt", dz_u, _bs((te, SSM_W), lambda i, j, k: (i, 0)), W_in, _bs((d, SSM_W), lambda i, j, k: (0, ucol)), 1),
         ("nt", dz_g, _bs((te, d), lambda i, j, k: (i, jnp.minimum(k, 1))), W_in,
          _bs((d, d), lambda i, j, k: (0, gcol + jnp.minimum(k, 1))), 2)],
        [(td_f32, row_e), (SDS((1, d), F32), vec_d)],
        extras=[(x2, row_e), (g_mix, vec_d), (dh1, row_e)], epilogue=_rms_bwd_epilogue)

    grads = [dW_in, dW_ap, dW_ga, dW_gb, dW_out, dW_fg, dW_fu, dW_fd, dW_pg, dW_pp]
    sizes = [s.shape[0] if k == "rows" else s.shape[-1] for s, k in zip(shards, kinds)]
    small_parts = dict(g_mix=dg_mix, a_re=da_re, a_im=da_im, log_dt=dlog_dt, b_re=db_re, b_im=db_im, c_re=dc_re, c_im=dc_im,
                       d_skip=dd_skip, g_ffn=dg_ffn, g_final=dg_final)
    small = _pack_small([small_parts[n] for n in _SMALL])
    received = _exchange_grads(grads, kinds, sizes, small)

    new = {}
    for n, recv in zip(wnames, received[:-1]):
        new[n] = [o.reshape(args[n].shape) for o in _adamw("adamw_" + n, recv, args[n][0], args["m_" + n][0], args["v_" + n][0])]
    pk = lambda pre: _pack_small([args[pre + n] for n in _SMALL])
    sm = _adamw("adamw_small", received[-1], pk(""), pk("m_"), pk("v_"))
    shapes = [args[n].shape for n in _SMALL]
    for n, vals in zip(_SMALL, zip(*[_unpack_small(o, shapes) for o in sm])):
        new[n] = list(vals)

    order = ("g_mix", "w_in", "a_re", "a_im", "log_dt", "b_re", "b_im", "c_re", "c_im", "d_skip", "w_attn_proj", "w_glu_a",
             "w_glu_b", "w_out", "g_ffn", "w_ffn_gate", "w_ffn_up", "w_ffn_down", "w_ple_gate", "w_ple_proj", "g_final")
    return (loss, dx.reshape(x.shape), *[new[n][0] for n in order], *[new[n][1] for n in order],
            *[new[n][2] for n in order], *[new[n][3] for n in order])
```

```python
import functools
import math

import jax
import jax.numpy as jnp
from jax import lax
from jax.experimental import pallas as pl
from jax.experimental.pallas import tpu as pltpu

F32 = jnp.float32
BF16 = jnp.bfloat16
SDS = jax.ShapeDtypeStruct

N_DEV = 8
HEAD_DIM = 128
HEADS_PER_GROUP = 4
GROUP_W = HEADS_PER_GROUP * HEAD_DIM
DILATIONS = (1, 4, 16)
N_GROUPS = len(DILATIONS)
QK_W = N_GROUPS * GROUP_W
BLK = 128
ROPE_THETA = 500000.0
ROPE_DIM = HEAD_DIM // 4
ROPE_HALF = ROPE_DIM // 2
SSM_W = 512
SSM_GROUP = 16
SSM_GROUPS = SSM_W // SSM_GROUP
SSM_STATE = 64
NSTATE = SSM_GROUPS * SSM_STATE
SSM_NB = 4
EPS = 1e-6
ADAM_LR, ADAM_B1, ADAM_B2, ADAM_EPS, ADAM_WD, ADAM_STEP = 0.001, 0.9, 0.999, 1e-08, 0.01, 10
NEG = -1e30

VMEM_LIMIT = 52 * 1024 * 1024
SCAN_ROWS = 256
SCAN_LANES = 512


def _cp(n):
    return pltpu.CompilerParams(dimension_semantics=("arbitrary",) * n, vmem_limit_bytes=VMEM_LIMIT)


def _sigmoid(x):
    return 1.0 / (1.0 + jnp.exp(-x))


_DNUMS = {"nn": (((1,), (0,)), ((), ())), "nt": (((1,), (1,)), ((), ())), "tn": (((0,), (0,)), ((), ()))}


def _bs(shape, fn):
    return pl.BlockSpec(shape, fn)


def _store_all(prods, extra_refs, out_refs, scratch_refs):
    r = prods[0]
    for p in prods[1:]:
        r = r + p
    for e in extra_refs:
        r = r + e[...]
    for o in out_refs:
        o[...] = r.astype(o.dtype)


def _mm(name, grid, pairs, outs, extras=(), epilogue=_store_all, scratch=(), alias_to_out0=None, after=None):
    nk = grid[2]
    npair = len(pairs)
    steps = [p[5] if len(p) > 5 else nk for p in pairs]

    def block(spec):
        return tuple(s for s in spec.block_shape if s is not None)

    acc_shapes = [jax.eval_shape(lambda u, v, dn=_DNUMS[p[0]]: lax.dot_general(u, v, dn, preferred_element_type=F32),
                                 SDS(block(p[2]), BF16), SDS(block(p[4]), BF16)).shape for p in pairs]
    n_in = 2 * npair + len(extras) + (alias_to_out0 is not None) + (after is not None)

    def body(*refs):
        extra_refs = refs[2 * npair:2 * npair + len(extras)]
        out_refs = refs[n_in:n_in + len(outs)]
        rest = refs[n_in + len(outs):]
        acc_refs = rest[:npair] if nk > 1 else ()
        scratch_refs = rest[len(acc_refs):]
        k = pl.program_id(2)

        def product(i):
            return lax.dot_general(refs[2 * i][...].astype(BF16), refs[2 * i + 1][...].astype(BF16), _DNUMS[pairs[i][0]],
                                   preferred_element_type=F32)

        if nk == 1:
            epilogue([product(i) for i in range(npair)], extra_refs, out_refs, scratch_refs)
            return
        for i in range(npair):
            @pl.when(k == 0)
            def _(i=i):
                acc_refs[i][...] = product(i)

            @pl.when((k > 0) & (k < steps[i]))
            def _(i=i):
                acc_refs[i][...] += product(i)

        @pl.when(k == nk - 1)
        def _():
            epilogue([a[...] for a in acc_refs], extra_refs, out_refs, scratch_refs)

    ins, in_specs = [], []
    for p in pairs:
        ins += [p[1], p[3]]
        in_specs += [p[2], p[4]]
    ins += [e[0] for e in extras]
    in_specs += [e[1] for e in extras]
    aliases = {}
    if alias_to_out0 is not None:
        aliases = {len(ins): 0}
        ins.append(alias_to_out0)
        in_specs.append(pl.BlockSpec(memory_space=pl.ANY))
    if after is not None:
        ins.append(after)
        in_specs.append(pl.BlockSpec(memory_space=pl.ANY))
    scratch_shapes =([pltpu.VMEM(s, F32) for s in acc_shapes] if nk > 1 else []) + list(scratch)
    return pl.pallas_call(body, grid=grid, in_specs=in_specs, out_specs=[o[1] for o in outs], out_shape=[o[0] for o in outs],
                          scratch_shapes=scratch_shapes, input_output_aliases=aliases, compiler_params=_cp(3), name=name)(*ins)


def _pick(n, cands):
    for c in cands:
        if n % c == 0:
            return c
    return n


def _my_index():
    return 4 * lax.axis_index("x") + 2 * lax.axis_index("y") + lax.axis_index("c")


def _peer(d):
    mx, my, mc = lax.axis_index("x"), lax.axis_index("y"), lax.axis_index("c")
    return (mx ^ ((d >> 2) & 1), my ^ ((d >> 1) & 1), mc ^ (d & 1))


def _win(ref, kind, j, n):
    if kind == "slot":
        return ref.at[j]
    if kind == "rows":
        return ref.at[pl.ds(pl.multiple_of(j * n, 8), n)]
    return ref.at[:, pl.ds(pl.multiple_of(j * n, 128), n)]


def _win7(ref, kind, n):
    if kind == "slot":
        return ref.at[pl.ds(0, 7)]
    if kind == "rows":
        return ref.at[pl.ds(0, 7 * n)]
    return ref.at[:, pl.ds(0, 7 * n)]


def _full_shape(shard_shape, kind):
    if kind == "slot":
        return (N_DEV,) + tuple(shard_shape)
    if kind == "rows":
        return (N_DEV * shard_shape[0],) + tuple(shard_shape[1:])
    return (shard_shape[0], N_DEV * shard_shape[1])


def _shard_shape(full_shape, kind, n):
    if kind == "slot":
        return tuple(full_shape[1:])
    if kind == "rows":
        return (n,) + tuple(full_shape[1:])
    return (full_shape[0], n)


_HBM = pl.BlockSpec(memory_space=pltpu.HBM)
_SEM = pl.BlockSpec(memory_space=pltpu.SEMAPHORE)
_DATAFLOW = pltpu.SideEffectType.DATAFLOW_SIDE_EFFECTING


def _exchange_start(name, srcs, kinds, sizes, gather):
    n = len(srcs)
    if gather:
        lands = [lax.empty(_full_shape(s.shape, k), s.dtype) for s, k in zip(srcs, kinds)]
    else:
        lands = [lax.empty((N_DEV,) + _shard_shape(s.shape, k, z), s.dtype) for s, k, z in zip(srcs, kinds, sizes)]

    def body(*refs):
        src, land = refs[:n], refs[n:2 * n]
        send_sems, recv_sems = refs[2 * n], refs[2 * n + 1]
        token, local_sems = refs[4 * n + 2], refs[4 * n + 3]
        me = _my_index()
        local = []
        for a in range(n):
            if gather:
                cp = pltpu.make_async_copy(src[a], _win(land[a], kinds[a], me, sizes[a]), local_sems.at[a])
            else:
                cp = pltpu.make_async_copy(_win(src[a], kinds[a], me, sizes[a]), land[a].at[me], local_sems.at[a])
            cp.start()
            local.append(cp)
        for a in range(n):
            for d in range(1, N_DEV):
                px, py, pc = _peer(d)
                if gather:
                    s_ref, d_ref = src[a], _win(land[a], kinds[a], me, sizes[a])
                else:
                    s_ref, d_ref = _win(src[a], kinds[a], 4 * px + 2 * py + pc, sizes[a]), land[a].at[me]
                pltpu.make_async_remote_copy(src_ref=s_ref, dst_ref=d_ref, send_sem=send_sems.at[a], recv_sem=recv_sems.at[a],
                                             device_id=(px, py, pc), device_id_type=pl.DeviceIdType.MESH).start()
        for cp in local:
            cp.wait()
        token[...] = jnp.zeros_like(token)

    hbm = [pltpu.with_memory_space_constraint(a, pltpu.HBM) for a in list(srcs) + lands]
    out = pl.pallas_call(
        body, name=name, in_specs=[_HBM] * (2 * n),
        out_shape=[pltpu.SemaphoreType.DMA((n,)), pltpu.SemaphoreType.DMA((n,))] + [pltpu.HBM(a.shape, a.dtype) for a in hbm]
        + [SDS((8, 128), F32)],
        out_specs=[_SEM, _SEM] + [_HBM] * (2 * n) + [pl.BlockSpec(memory_space=pltpu.VMEM)],
        input_output_aliases={i: 2 + i for i in range(2 * n)}, scratch_shapes=[pltpu.SemaphoreType.DMA((n,))],
        compiler_params=pltpu.CompilerParams(has_side_effects=_DATAFLOW))(*hbm)
    return out[0], out[1], out[2:2 + n], out[2 + n:2 + 2 * n], out[-1]


def _exchange_wait(name, started, kinds, sizes, gather, after):
    send_sems, recv_sems, srcs, lands, _ = started
    n = len(srcs)

    def body(*refs):
        land = refs[n:2 * n]
        send_ref, recv_ref = refs[2 * n], refs[2 * n + 1]
        my_id = (lax.axis_index("x"), lax.axis_index("y"), lax.axis_index("c"))
        for a in range(n):
            seven = _win7(land[a], kinds[a], sizes[a]) if gather else land[a].at[pl.ds(0, 7)]
            pltpu.make_async_remote_copy(src_ref=seven, dst_ref=seven, send_sem=send_ref.at[a], recv_sem=recv_ref.at[a],
                                         device_id=my_id, device_id_type=pl.DeviceIdType.MESH).wait()

    hbm = list(srcs) + list(lands)
    out = pl.pallas_call(
        body, name=name, in_specs=[_HBM] * (2 * n) + [_SEM, _SEM, pl.BlockSpec(memory_space=pl.ANY)],
        out_shape=[pltpu.HBM(a.shape, a.dtype) for a in hbm], out_specs=[_HBM] * (2 * n),
        input_output_aliases={i: i for i in range(2 * n)},
        compiler_params=pltpu.CompilerParams(has_side_effects=_DATAFLOW))(*hbm, send_sems, recv_sems, after)
    return out[n:]


def _gather_small(small):
    def body(in_ref, out_ref, send_sem, recv_sem, local_sem):
        me = _my_index()
        my_id = (lax.axis_index("x"), lax.axis_index("y"), lax.axis_index("c"))
        cp = pltpu.make_async_copy(in_ref, out_ref.at[me], local_sem)
        cp.start()
        for d in range(1, N_DEV):
            pltpu.make_async_remote_copy(src_ref=in_ref, dst_ref=out_ref.at[me], send_sem=send_sem, recv_sem=recv_sem,
                                         device_id=_peer(d), device_id_type=pl.DeviceIdType.MESH).start()
        seven = out_ref.at[pl.ds(0, 7)]
        pltpu.make_async_remote_copy(src_ref=seven, dst_ref=seven, send_sem=send_sem, recv_sem=recv_sem, device_id=my_id,
                                     device_id_type=pl.DeviceIdType.MESH).wait()
        cp.wait()

    any_spec = pl.BlockSpec(memory_space=pl.ANY)
    return pl.pallas_call(body, in_specs=[any_spec], out_specs=any_spec, out_shape=SDS((N_DEV,) + small.shape, F32),
                          scratch_shapes=[pltpu.SemaphoreType.DMA] * 3, name="gather_small")(small)


def _adamw(name, recv, w, m, v):
    rows, cols = w.shape
    tr = max(c for c in range(8, 257, 8) if rows % c == 0) if rows % 8 == 0 else rows

    def body(r_ref, w_ref, m_ref, v_ref, g_ref, d_ref, nm_ref, nv_ref):
        g = r_ref[0]
        for s in range(1, N_DEV):
            g = g + r_ref[s]
        nm = ADAM_B1 * m_ref[...] + (1.0 - ADAM_B1) * g
        nv = ADAM_B2 * v_ref[...] + (1.0 - ADAM_B2) * (g * g)
        m_hat = nm / (1.0 - ADAM_B1 ** ADAM_STEP)
        v_hat = nv / (1.0 - ADAM_B2 ** ADAM_STEP)
        g_ref[...] = g
        d_ref[...] = -ADAM_LR * (m_hat / (jnp.sqrt(v_hat) + ADAM_EPS) + ADAM_WD * w_ref[...])
        nm_ref[...] = nm
        nv_ref[...] = nv

    blk = _bs((tr, cols), lambda i: (i, 0))
    return pl.pallas_call(
        body, grid=(rows // tr,), in_specs=[_bs((N_DEV, tr, cols), lambda i: (0, i, 0)), blk, blk, blk],
        out_specs=[blk] * 4, out_shape=[SDS((rows, cols), F32)] * 4, compiler_params=_cp(1), name=name)(recv, w, m, v)


def _rms_fwd(name, x, g, tm):
    t, d = x.shape

    def body(x_ref, g_ref, n_ref):
        xv = x_ref[...]
        r = lax.rsqrt(jnp.mean(xv * xv, axis=-1, keepdims=True) + EPS)
        n_ref[...] = (xv * r * g_ref[...]).astype(BF16)

    return pl.pallas_call(body, grid=(t // tm,), in_specs=[_bs((tm, d), lambda i: (i, 0)), _bs((1, d), lambda i: (0, 0))],
                          out_specs=_bs((tm, d), lambda i: (i, 0)), out_shape=SDS((t, d), BF16), compiler_params=_cp(1),
                          name=name)(x, g)


def _accumulate_rows(ref, part):
    @pl.when(pl.program_id(0) == 0)
    def _():
        ref[...] = part

    @pl.when(pl.program_id(0) > 0)
    def _():
        ref[...] += part


def _rms_bwd_epilogue(prods, extra_refs, out_refs, scratch_refs):
    dyv = prods[0]
    for p in prods[1:]:
        dyv = dyv + p
    if len(extra_refs) > 3:
        dyv = dyv + extra_refs[3][...]
    xv = extra_refs[0][...]
    r = lax.rsqrt(jnp.mean(xv * xv, axis=-1, keepdims=True) + EPS)
    xh = xv * r
    dxh = dyv * extra_refs[1][...]
    dx = extra_refs[2][...] + r * (dxh - xh * jnp.mean(dxh * xh, axis=-1, keepdims=True))
    for o in out_refs[:-1]:
        o[...] = dx.astype(o.dtype)
    _accumulate_rows(out_refs[-1], jnp.sum(dyv * xh, axis=0, keepdims=True))


def _out_norm_epilogue(prods, extra_refs, out_refs, scratch_refs):
    h = prods[0] + extra_refs[0][...]
    r = lax.rsqrt(jnp.mean(h * h, axis=-1, keepdims=True) + EPS)
    out_refs[0][...] = h
    out_refs[1][...] = (h * r * extra_refs[1][...]).astype(BF16)


def _glu_merge_epilogue(prods, extra_refs, out_refs, scratch_refs):
    ya, yb, ad = prods
    m = _sigmoid(extra_refs[0][...]) * ad + _sigmoid(extra_refs[1][...]) * (ya * _sigmoid(yb))
    out_refs[0][...] = m.astype(BF16)
    out_refs[1][...] = ya
    out_refs[2][...] = yb
    out_refs[3][...] = ad


def _merge_bwd_epilogue(prods, extra_refs, out_refs, scratch_refs):
    dmv = prods[0]
    d = dmv.shape[1]
    ga, gs = _sigmoid(extra_refs[0][...]), _sigmoid(extra_refs[1][...])
    adv, yav = extra_refs[2][...], extra_refs[3][...]
    sb = _sigmoid(extra_refs[4][...])
    out_refs[0][:, 0:d] = (dmv * adv * ga * (1.0 - ga)).astype(BF16)
    out_refs[0][:, d:2 * d] = (dmv * (yav * sb) * gs * (1.0 - gs)).astype(BF16)
    out_refs[1][...] = (dmv * ga).astype(BF16)
    dsd = dmv * gs
    out_refs[2][...] = (dsd * sb).astype(BF16)
    out_refs[3][...] = (dsd * yav * sb * (1.0 - sb)).astype(BF16)


def _swiglu_epilogue(prods, extra_refs, out_refs, scratch_refs):
    gv, uv = prods
    out_refs[0][...] = (gv * _sigmoid(gv) * uv).astype(BF16)
    out_refs[1][...] = gv
    out_refs[2][...] = uv


def _swiglu_bwd_epilogue(prods, extra_refs, out_refs, scratch_refs):
    dav = prods[0]
    gv, uv = extra_refs[0][...], extra_refs[1][...]
    sg = _sigmoid(gv)
    out_refs[0][...] = (dav * uv * sg * (1.0 + gv * (1.0 - sg))).astype(BF16)
    out_refs[1][...] = (dav * gv * sg).astype(BF16)


def _head_epilogue(n_tiles):
    def epilogue(prods, extra_refs, out_refs, scratch_refs):
        pgv, ppv = prods
        d = pgv.shape[1]
        lacc = scratch_refs[0]
        sg = _sigmoid(pgv)
        h3 = extra_refs[0][...] + sg * ppv
        r = lax.rsqrt(jnp.mean(h3 * h3, axis=-1, keepdims=True) + EPS)
        xh = h3 * r
        gv = extra_refs[1][...]
        diff = xh * gv - extra_refs[2][...]
        dout = diff * (1.0 / d)
        dxh = dout * gv
        dh3 = r * (dxh - xh * jnp.mean(dxh * xh, axis=-1, keepdims=True))
        out_refs[2][...] = dh3
        out_refs[3][...] = (dh3 * sg).astype(BF16)
        out_refs[4][...] = (dh3 * ppv * sg * (1.0 - sg)).astype(BF16)
        _accumulate_rows(out_refs[1], jnp.sum(dout * xh, axis=0, keepdims=True))
        _accumulate_rows(lacc, jnp.sum(diff * diff, axis=0, keepdims=True))

        @pl.when(pl.program_id(0) == n_tiles - 1)
        def _():
            out_refs[0][...] = (0.5 / d) * jnp.sum(lacc[...], axis=-1, keepdims=True)

    return epilogue


def _strided(r, n, d):
    return pl.ds(r, n, stride=d) if d > 1 else pl.ds(0, n)


def _rope_tables(pos_ref, invf_ref, c_s, s1_s, s2_s, on):
    ang = pos_ref[...].astype(F32) * invf_ref[...]
    lane = lax.broadcasted_iota(jnp.int32, ang.shape, 1)
    sn = jnp.sin(ang)
    c_s[...] = jnp.where((lane < ROPE_DIM) & on, jnp.cos(ang), 1.0)
    s1_s[...] = jnp.where((lane < ROPE_HALF) & on, -sn, 0.0)
    s2_s[...] = jnp.where((lane >= ROPE_HALF) & (lane < ROPE_DIM) & on, sn, 0.0)


def _rope_dilate(z, pos, invf, tm):
    t = z.shape[0]

    def body(z_ref, pos_ref, invf_ref, o0, o1, o2, c_s, s1_s, s2_s, rot):
        _rope_tables(pos_ref, invf_ref, c_s, s1_s, s2_s, pl.program_id(1) < 2)
        cc, s1, s2 = c_s[...], s1_s[...], s2_s[...]
        for h in range(QK_W // HEAD_DIM):
            xv = z_ref[:, h * HEAD_DIM:(h + 1) * HEAD_DIM]
            rot[h] = xv * cc + pltpu.roll(xv, HEAD_DIM - ROPE_HALF, 1) * s1 + pltpu.roll(xv, ROPE_HALF, 1) * s2
        for g, (d, o_ref) in enumerate(zip(DILATIONS, (o0, o1, o2))):
            n = tm // d
            for r in range(d):
                for hh in range(HEADS_PER_GROUP):
                    oc = r * GROUP_W + hh * HEAD_DIM
                    o_ref[:, oc:oc + HEAD_DIM] = rot[g * HEADS_PER_GROUP + hh, _strided(r, n, d), :].astype(BF16)

    return pl.pallas_call(
        body, grid=(t // tm, 3),
        in_specs=[_bs((tm, QK_W), lambda i, c: (i, c)), _bs((tm, 1), lambda i, c: (i, 0)), _bs((1, HEAD_DIM), lambda i, c: (0, 0))],
        out_specs=[_bs((None, tm // d, d * GROUP_W), lambda i, c: (c, i, 0)) for d in DILATIONS],
        out_shape=[SDS((3, t // d, d * GROUP_W), BF16) for d in DILATIONS],
        scratch_shapes=[pltpu.VMEM((tm, HEAD_DIM), F32)] * 3 + [pltpu.VMEM((QK_W // HEAD_DIM, tm, HEAD_DIM), F32)],
        compiler_params=_cp(2), name="rope_dilate")(z, pos, invf)


def _band_mask(first):
    qi = lax.broadcasted_iota(jnp.int32, (BLK, 2 * BLK), 0)
    kj = lax.broadcasted_iota(jnp.int32, (BLK, 2 * BLK), 1)
    return (kj >= qi) & (kj <= qi + BLK) & ((kj >= BLK) | jnp.logical_not(first))


def _attn_fwd(qkv, d, qt):
    ell = qkv.shape[1]
    nsub = qt // BLK
    scale = 1.0 / math.sqrt(HEAD_DIM)

    def body(q_ref, kc_ref, kp_ref, vc_ref, vp_ref, o_ref, lse_ref, kcat, vcat):
        nb = pl.program_id(1)
        kcat[0:BLK, :] = kp_ref[...]
        kcat[BLK:, :] = kc_ref[...]
        vcat[0:BLK, :] = vp_ref[...]
        vcat[BLK:, :] = vc_ref[...]
        lane = lax.broadcasted_iota(jnp.int32, (BLK, HEAD_DIM), 1)
        for b in range(nsub):
            valid = _band_mask((nb == 0) if b == 0 else False)
            lse_t = jnp.zeros((BLK, HEAD_DIM), F32)
            for hh in range(HEADS_PER_GROUP):
                cs = slice(hh * HEAD_DIM, (hh + 1) * HEAD_DIM)
                qb = q_ref[b * BLK:(b + 1) * BLK, cs]
                kk = kcat[b * BLK:(b + 2) * BLK, cs]
                vv = vcat[b * BLK:(b + 2) * BLK, cs]
                s = lax.dot_general(qb, kk, _DNUMS["nt"], preferred_element_type=F32) * scale
                s = jnp.where(valid, s, NEG)
                mx = jnp.max(s, axis=-1, keepdims=True)
                p = jnp.exp(s - mx)
                den = jnp.sum(p, axis=-1, keepdims=True)
                o = jnp.dot(p.astype(BF16), vv, preferred_element_type=F32) / den
                o_ref[b * BLK:(b + 1) * BLK, cs] = o
                lse_t = jnp.where(lane == hh, mx + jnp.log(den), lse_t)
            lse_ref[b * BLK:(b + 1) * BLK, :] = lse_t

    cur = lambda c: _bs((None, qt, GROUP_W), lambda r, nb: (c, nb, r))
    prev = lambda c: _bs((None, BLK, GROUP_W), lambda r, nb: (c, jnp.maximum(nb * nsub - 1, 0), r))
    return pl.pallas_call(
        body, grid=(d, ell // qt), in_specs=[cur(0), cur(1), prev(1), cur(2), prev(2)],
        out_specs=[_bs((qt, GROUP_W), lambda r, nb: (nb, r)), _bs((None, qt, HEAD_DIM), lambda r, nb: (r, nb, 0))],
        out_shape=[SDS((ell, d * GROUP_W), F32), SDS((d, ell, HEAD_DIM), F32)],
        scratch_shapes=[pltpu.VMEM((qt + BLK, GROUP_W), BF16)] * 2, compiler_params=_cp(2), name=f"attn_fwd_d{d}")(
            qkv, qkv, qkv, qkv, qkv)


def _attn_merge(outs, lses, tm):
    t = outs[0].shape[0]

    def body(o0, o1, o2, l0, l1, l2, attn_ref, attn_bf_ref, t0, t1, t2, so, sl, lt_s):
        for g, (d, o_ref, l_ref) in enumerate(zip(DILATIONS, (o0, o1, o2), (l0, l1, l2))):
            n = tm // d
            for r in range(d):
                rows = _strided(r, n, d)
                for hh in range(HEADS_PER_GROUP):
                    oc = r * GROUP_W + hh * HEAD_DIM
                    so[g * HEADS_PER_GROUP + hh, rows, :] = o_ref[:, oc:oc + HEAD_DIM]
                sl[g, rows, :] = l_ref[r]
        ls = [sl[g] for g in range(N_GROUPS)]
        mx = jnp.maximum(jnp.maximum(ls[0], ls[1]), ls[2])
        es = [jnp.exp(l - mx) for l in ls]
        den = es[0] + es[1] + es[2]
        ws = [e / den for e in es]
        lt_s[...] = mx + jnp.log(den)
        for hh in range(HEADS_PER_GROUP):
            cs = slice(hh * HEAD_DIM, (hh + 1) * HEAD_DIM)
            a = ws[0][:, hh:hh + 1] * so[hh]
            for g in range(1, N_GROUPS):
                a = a + ws[g][:, hh:hh + 1] * so[g * HEADS_PER_GROUP + hh]
            attn_ref[:, cs] = a
            attn_bf_ref[:, cs] = a.astype(BF16)
        for d, t_ref in zip(DILATIONS, (t0, t1, t2)):
            n = tm // d
            for r in range(d):
                t_ref[r] = lt_s[_strided(r, n, d), :]

    dil = lambda d: _bs((tm // d, d * GROUP_W), lambda i: (i, 0))
    lsp = lambda d: _bs((d, tm // d, HEAD_DIM), lambda i: (0, i, 0))
    row = _bs((tm, GROUP_W), lambda i: (i, 0))
    return pl.pallas_call(
        body, grid=(t // tm,),
        in_specs=[dil(d) for d in DILATIONS] + [lsp(d) for d in DILATIONS],
        out_specs=[row, row] + [lsp(d) for d in DILATIONS],
        out_shape=[SDS((t, GROUP_W), F32), SDS((t, GROUP_W), BF16)] + [SDS(l.shape, F32) for l in lses],
        scratch_shapes=[pltpu.VMEM((N_GROUPS * HEADS_PER_GROUP, tm, HEAD_DIM), F32), pltpu.VMEM((N_GROUPS, tm, HEAD_DIM), F32),
                        pltpu.VMEM((tm, HEAD_DIM), F32)],
        compiler_params=_cp(1), name="attn_merge")(*outs, *lses)


def _attn_bwd_pre(d_attn, attn, tm, after):
    t = attn.shape[0]

    def body(da_ref, a_ref, after_ref, g0, g1, g2, e0, e1, e2, dl_s, da_s):
        del after_ref
        lane = lax.broadcasted_iota(jnp.int32, (tm, HEAD_DIM), 1)
        dl = jnp.zeros((tm, HEAD_DIM), F32)
        for hh in range(HEADS_PER_GROUP):
            cs = slice(hh * HEAD_DIM, (hh + 1) * HEAD_DIM)
            dav = da_ref[:, cs]
            da_s[hh] = dav
            dl = jnp.where(lane == hh, jnp.sum(dav * a_ref[:, cs], axis=-1, keepdims=True), dl)
        dl_s[...] = dl
        for d, g_ref, e_ref in zip(DILATIONS, (g0, g1, g2), (e0, e1, e2)):
            n = tm // d
            for r in range(d):
                rows = _strided(r, n, d)
                for hh in range(HEADS_PER_GROUP):
                    oc = r * GROUP_W + hh * HEAD_DIM
                    g_ref[:, oc:oc + HEAD_DIM] = da_s[hh, rows, :].astype(BF16)
                e_ref[r] = dl_s[rows, :]

    row = _bs((tm, GROUP_W), lambda i: (i, 0))
    return pl.pallas_call(
        body, grid=(t // tm,), in_specs=[row, row, pl.BlockSpec(memory_space=pl.ANY)],
        out_specs=[_bs((tm // d, d * GROUP_W), lambda i: (i, 0)) for d in DILATIONS]
        + [_bs((d, tm // d, HEAD_DIM), lambda i: (0, i, 0)) for d in DILATIONS],
        out_shape=[SDS((t // d, d * GROUP_W), BF16) for d in DILATIONS]
        + [SDS((d, t // d, HEAD_DIM), F32) for d in DILATIONS],
        scratch_shapes=[pltpu.VMEM((tm, HEAD_DIM), F32), pltpu.VMEM((HEADS_PER_GROUP, tm, HEAD_DIM), F32)],
        compiler_params=_cp(1), name="attn_bwd_pre")(d_attn, attn, after)


def _attn_bwd(qkv, d_a, lt, delta, d, qt):
    ell = qkv.shape[1]
    nsub = qt // BLK
    ntile = ell // qt
    nblk = ell // BLK
    scale = 1.0 / math.sqrt(HEAD_DIM)

    def body(q_ref, qn_ref, kc_ref, kp_ref, vc_ref, vp_ref, da_ref, dan_ref, lt_ref, ltn_ref, dl_ref, dln_ref, o_ref,
             kcat, vcat, dk_acc, dv_acc):
        nb = pl.program_id(1)
        kcat[0:BLK, :] = kp_ref[...]
        kcat[BLK:, :] = kc_ref[...]
        vcat[0:BLK, :] = vp_ref[...]
        vcat[BLK:, :] = vc_ref[...]
        qi = lax.broadcasted_iota(jnp.int32, (BLK, BLK), 0)
        kj = lax.broadcasted_iota(jnp.int32, (BLK, BLK), 1)
        valid_next = (kj >= qi) & (nb < ntile - 1)
        for hh in range(HEADS_PER_GROUP):
            cs = slice(hh * HEAD_DIM, (hh + 1) * HEAD_DIM)
            dk_acc[...] = jnp.zeros_like(dk_acc)
            dv_acc[...] = jnp.zeros_like(dv_acc)
            for b in range(nsub):
                rs = slice(b * BLK, (b + 1) * BLK)
                ks = slice(b * BLK, (b + 2) * BLK)
                valid = _band_mask((nb == 0) if b == 0 else False)
                qb, kk, vv, dab = q_ref[rs, cs], kcat[ks, cs], vcat[ks, cs], da_ref[rs, cs]
                s = lax.dot_general(qb, kk, _DNUMS["nt"], preferred_element_type=F32) * scale
                p = jnp.where(valid, jnp.exp(s - lt_ref[rs, hh:hh + 1]), 0.0)
                dp = lax.dot_general(dab, vv, _DNUMS["nt"], preferred_element_type=F32)
                ds = (p * (dp - dl_ref[rs, hh:hh + 1])).astype(BF16)
                o_ref[0, rs, cs] = jnp.dot(ds, kk, preferred_element_type=F32) * scale
                dk_acc[ks, :] += lax.dot_general(ds, qb, _DNUMS["tn"], preferred_element_type=F32) * scale
                dv_acc[ks, :] += lax.dot_general(p.astype(BF16), dab, _DNUMS["tn"], preferred_element_type=F32)
            ks = slice(nsub * BLK, (nsub + 1) * BLK)
            qn, kl, vl, dan = qn_ref[:, cs], kcat[ks, cs], vcat[ks, cs], dan_ref[:, cs]
            s = lax.dot_general(qn, kl, _DNUMS["nt"], preferred_element_type=F32) * scale
            p = jnp.where(valid_next, jnp.exp(s - ltn_ref[:, hh:hh + 1]), 0.0)
            dp = lax.dot_general(dan, vl, _DNUMS["nt"], preferred_element_type=F32)
            ds = (p * (dp - dln_ref[:, hh:hh + 1])).astype(BF16)
            dk_acc[ks, :] += lax.dot_general(ds, qn, _DNUMS["tn"], preferred_element_type=F32) * scale
            dv_acc[ks, :] += lax.dot_general(p.astype(BF16), dan, _DNUMS["tn"], preferred_element_type=F32)
            o_ref[1, :, cs] = dk_acc[BLK:, :]
            o_ref[2, :, cs] = dv_acc[BLK:, :]

    nxt = lambda nb: jnp.minimum((nb + 1) * nsub, nblk - 1)
    prv = lambda nb: jnp.maximum(nb * nsub - 1, 0)
    cur3 = lambda c: _bs((None, qt, GROUP_W), lambda r, nb: (c, nb, r))
    in_specs = [
        cur3(0), _bs((None, BLK, GROUP_W), lambda r, nb: (0, nxt(nb), r)),
        cur3(1), _bs((None, BLK, GROUP_W), lambda r, nb: (1, prv(nb), r)),
        cur3(2), _bs((None, BLK, GROUP_W), lambda r, nb: (2, prv(nb), r)),
        _bs((qt, GROUP_W), lambda r, nb: (nb, r)), _bs((BLK, GROUP_W), lambda r, nb: (nxt(nb), r)),
        _bs((None, qt, HEAD_DIM), lambda r, nb: (r, nb, 0)), _bs((None, BLK, HEAD_DIM), lambda r, nb: (r, nxt(nb), 0)),
        _bs((None, qt, HEAD_DIM), lambda r, nb: (r, nb, 0)), _bs((None, BLK, HEAD_DIM), lambda r, nb: (r, nxt(nb), 0)),
    ]
    return pl.pallas_call(
        body, grid=(d, ntile), in_specs=in_specs, out_specs=_bs((3, qt, GROUP_W), lambda r, nb: (0, nb, r)),
        out_shape=SDS((3, ell, d * GROUP_W), F32),
        scratch_shapes=[pltpu.VMEM((qt + BLK, GROUP_W), BF16)] * 2 + [pltpu.VMEM((qt + BLK, HEAD_DIM), F32)] * 2,
        compiler_params=_cp(2), name=f"attn_bwd_d{d}")(qkv, qkv, qkv, qkv, qkv, qkv, d_a, d_a, lt, lt, delta, delta)


def _undilate_rope_bwd(dqkvs, pos, invf, tm):
    t = pos.shape[0]

    def body(g0, g1, g2, pos_ref, invf_ref, o_ref, c_s, s1_s, s2_s, nat):
        _rope_tables(pos_ref, invf_ref, c_s, s1_s, s2_s, pl.program_id(1) < 2)
        for g, (d, g_ref) in enumerate(zip(DILATIONS, (g0, g1, g2))):
            n = tm // d
            for r in range(d):
                for hh in range(HEADS_PER_GROUP):
                    oc = r * GROUP_W + hh * HEAD_DIM
                    nat[g * HEADS_PER_GROUP + hh, _strided(r, n, d), :] = g_ref[:, oc:oc + HEAD_DIM]
        cc, s1, s2 = c_s[...], s1_s[...], s2_s[...]
        for h in range(QK_W // HEAD_DIM):
            xv = nat[h]
            y = xv * cc - pltpu.roll(xv, HEAD_DIM - ROPE_HALF, 1) * s1 - pltpu.roll(xv, ROPE_HALF, 1) * s2
            o_ref[:, h * HEAD_DIM:(h + 1) * HEAD_DIM] = y.astype(BF16)

    return pl.pallas_call(
        body, grid=(t // tm, 3),
        in_specs=[_bs((None, tm // d, d * GROUP_W), lambda i, c: (c, i, 0)) for d in DILATIONS]
        + [_bs((tm, 1), lambda i, c: (i, 0)), _bs((1, HEAD_DIM), lambda i, c: (0, 0))],
        out_specs=_bs((tm, QK_W), lambda i, c: (i, c)), out_shape=SDS((t, 3 * QK_W), BF16),
        scratch_shapes=[pltpu.VMEM((tm, HEAD_DIM), F32)] * 3 + [pltpu.VMEM((QK_W // HEAD_DIM, tm, HEAD_DIM), F32)],
        compiler_params=_cp(2), name="undilate_rope_bwd")(*dqkvs, pos, invf)


def _cmul(ar, ai, br, bi):
    return ar * br - ai * bi, ar * bi + ai * br


def _ssm_disc(a_re, a_im, log_dt, nsq):
    def body(lr_ref, li_ref, ldt_ref, br_ref, bi_ref, zr_ref, zi_ref, pr_ref, pi_ref):
        lr, li = lr_ref[...], li_ref[...]
        dt = jnp.exp(ldt_ref[...])
        mag = jnp.exp(lr * dt)
        bar_re, bar_im = mag * jnp.cos(li * dt), mag * jnp.sin(li * dt)
        nr, ni = bar_re - 1.0, bar_im
        den = lr * lr + li * li
        br_ref[...], bi_ref[...] = bar_re, bar_im
        zr_ref[...] = (nr * lr + ni * li) / den
        zi_ref[...] = (ni * lr - nr * li) / den
        pr, pi = bar_re, bar_im
        for _ in range(nsq):
            pr, pi = _cmul(pr, pi, pr, pi)
        pr_ref[...], pi_ref[...] = pr, pi

    return pl.pallas_call(body, out_shape=[SDS(a_re.shape, F32)] * 6, name="ssm_discretise")(a_re, a_im, log_dt)


def _ssm_scale_b(z_re, z_im, b_re, b_im):
    def body(zr_ref, zi_ref, br_ref, bi_ref, or_ref, oi_ref):
        zr, zi, br, bi = zr_ref[...], zi_ref[...], br_ref[...], bi_ref[...]
        or_ref[...] = zr * br - zi * bi
        oi_ref[...] = zr * bi + zi * br

    return pl.pallas_call(body, out_shape=[SDS(b_re.shape, F32)] * 2, name="ssm_scale_b")(z_re, z_im, b_re, b_im)


def _ssm_scale_b_bwd(z_re, z_im, b_re, b_im, g_re, g_im):
    def body(zr_ref, zi_ref, br_ref, bi_ref, gr_ref, gi_ref, dbr_ref, dbi_ref, dzr_ref, dzi_ref):
        zr, zi, br, bi, gr, gi = zr_ref[...], zi_ref[...], br_ref[...], bi_ref[...], gr_ref[...], gi_ref[...]
        dbr_ref[...] = zr * gr + zi * gi
        dbi_ref[...] = zr * gi - zi * gr
        dzr_ref[...] = jnp.sum(br * gr + bi * gi, axis=-1, keepdims=True)
        dzi_ref[...] = jnp.sum(br * gi - bi * gr, axis=-1, keepdims=True)

    return pl.pallas_call(body, out_shape=[SDS(b_re.shape, F32)] * 2 + [SDS(z_re.shape, F32)] * 2,
                          name="ssm_scale_b_bwd")(z_re, z_im, b_re, b_im, g_re, g_im)


def _ssm_disc_bwd(a_re, a_im, log_dt, gb_re, gb_im, gz_re, gz_im):
    def body(lr_ref, li_ref, ldt_ref, gbr_ref, gbi_ref, gzr_ref, gzi_ref, dar_ref, dai_ref, dldt_ref):
        lr, li = lr_ref[...], li_ref[...]
        dt = jnp.exp(ldt_ref[...])
        mag = jnp.exp(lr * dt)
        bar_re, bar_im = mag * jnp.cos(li * dt), mag * jnp.sin(li * dt)
        nr, ni = bar_re - 1.0, bar_im
        den = lr * lr + li * li
        zr, zi = (nr * lr + ni * li) / den, (ni * lr - nr * li) / den
        gzr, gzi = gzr_ref[...], gzi_ref[...]
        gbr = gbr_ref[...] + (lr * gzr - li * gzi) / den
        gbi = gbi_ref[...] + (lr * gzi + li * gzr) / den
        qr, qi = (zr * lr + zi * li) / den, (zi * lr - zr * li) / den
        dar_ref[...] = dt * (bar_re * gbr + bar_im * gbi) - qr * gzr - qi * gzi
        dai_ref[...] = dt * (bar_re * gbi - bar_im * gbr) - qr * gzi + qi * gzr
        wr, wi = lr * bar_re - li * bar_im, lr * bar_im + li * bar_re
        dldt_ref[...] = dt * jnp.sum(wr * gbr + wi * gbi, axis=-1, keepdims=True)

    return pl.pallas_call(body, out_shape=[SDS(a_re.shape, F32)] * 2 + [SDS(log_dt.shape, F32)],
                          name="ssm_discretise_bwd")(a_re, a_im, log_dt, gb_re, gb_im, gz_re, gz_im)


def _permute_u(z, ucol_block, tm):
    t = z.shape[0]
    seg = t // N_DEV
    z3 = z.reshape(N_DEV, seg, z.shape[1])

    def body(z_ref, u_ref, ub_ref, tmp):
        for n in range(SSM_W // BLK):
            for j in range(N_DEV):
                tmp[n, pl.ds(j, tm // N_DEV, stride=N_DEV), :] = z_ref[j, :, n * BLK:(n + 1) * BLK]
            u_ref[:, n * BLK:(n + 1) * BLK] = tmp[n]
            ub_ref[:, n * BLK:(n + 1) * BLK] = tmp[n].astype(BF16)

    row = _bs((tm, SSM_W), lambda i: (i, 0))
    return pl.pallas_call(
        body, grid=(t // tm,), in_specs=[_bs((N_DEV, tm // N_DEV, SSM_W), lambda i: (0, i, ucol_block))],
        out_specs=[row, row], out_shape=[SDS((t, SSM_W), F32), SDS((t, SSM_W), BF16)],
        scratch_shapes=[pltpu.VMEM((SSM_W // BLK, tm, BLK), F32)], compiler_params=_cp(1), name="permute_u")(z3)


def _drive(src_ref, mat_ref, dst, mode):
    for kn in range(2 * SSM_NB):
        n = kn % SSM_NB
        a = src_ref[:, n * BLK:(n + 1) * BLK]
        dst[:, kn * 512:(kn + 1) * 512] = lax.dot_general(a, mat_ref[kn], _DNUMS[mode], preferred_element_type=F32)


def _scan_chunk(src, lam_ref, carry, *, reverse, store=None, h_ref=None, acc=None):
    steps = src.shape[0] // 8
    for c in range(NSTATE // SCAN_LANES):
        re = slice(c * SCAN_LANES, (c + 1) * SCAN_LANES)
        im = slice(NSTATE + c * SCAN_LANES, NSTATE + (c + 1) * SCAN_LANES)
        ar, ai = lam_ref[:, re], lam_ref[:, im]

        def step(s, val):
            i = (steps - 1 - s) if reverse else s
            rows = pl.ds(pl.multiple_of(i * 8, 8), 8)
            if acc is not None:
                hr, hi, dr, di = val
                pr, pi = h_ref[rows, re], h_ref[rows, im]
                dr = dr + hr * pr + hi * pi
                di = di + hi * pr - hr * pi
            else:
                hr, hi = val
            nr = ar * hr - ai * hi + src[rows, re]
            ni = ar * hi + ai * hr + src[rows, im]
            if store is not None:
                store[rows, re] = nr
                store[rows, im] = ni
            return (nr, ni, dr, di) if acc is not None else (nr, ni)

        init = (carry[:, re], carry[:, im])
        if acc is not None:
            init = init + (acc[:, re], acc[:, im])
        out = lax.fori_loop(0, steps, step, init, unroll=4)
        carry[:, re], carry[:, im] = out[0], out[1]
        if acc is not None:
            acc[:, re], acc[:, im] = out[2], out[3]


def _segment_carries(e_ref, pw_ref, out_ref, reverse):
    pr, pi = pw_ref[:, 0:NSTATE], pw_ref[:, NSTATE:]
    hr = jnp.zeros((1, NSTATE), F32)
    hi = jnp.zeros((1, NSTATE), F32)
    order = range(N_DEV - 1, -1, -1) if reverse else range(N_DEV)
    for j in order:
        out_ref[j:j + 1, 0:NSTATE] = hr
        out_ref[j:j + 1, NSTATE:] = hi
        tr, ti = _cmul(pr, pi, hr, hi)
        hr, hi = e_ref[j:j + 1, 0:NSTATE] + tr, e_ref[j:j + 1, NSTATE:] + ti


def _ssm_carries(name, src, mat, mode, lam8, pw, reverse):
    t = src.shape[0]
    nchunk = t // SCAN_ROWS

    def body(src_ref, mat_ref, lam_ref, pw_ref, out_ref, drive, carry):
        c = pl.program_id(0)

        @pl.when(c == 0)
        def _():
            carry[...] = jnp.zeros_like(carry)

        _drive(src_ref, mat_ref, drive, mode)
        _scan_chunk(drive, lam_ref, carry, reverse=reverse)

        @pl.when(c == nchunk - 1)
        def _():
            _segment_carries(carry, pw_ref, out_ref, reverse)

    blk = (lambda c: (nchunk - 1 - c, 0)) if reverse else (lambda c: (c, 0))
    return pl.pallas_call(
        body, grid=(nchunk,),
        in_specs=[_bs((SCAN_ROWS, SSM_W), blk), _bs(mat.shape, lambda c: (0, 0, 0)), _bs((8, 2 * NSTATE), lambda c: (0, 0)),
                  _bs((1, 2 * NSTATE), lambda c: (0, 0))],
        out_specs=_bs((8, 2 * NSTATE), lambda c: (0, 0)), out_shape=SDS((8, 2 * NSTATE), F32),
        scratch_shapes=[pltpu.VMEM((SCAN_ROWS, 2 * NSTATE), F32), pltpu.VMEM((8, 2 * NSTATE), F32)],
        compiler_params=_cp(1), name=name)(src, mat, lam8, pw)


def _ssm_fwd(u_bf, bd, cd, lam8, start):
    t = u_bf.shape[0]
    nchunk = t // SCAN_ROWS

    def body(u_ref, bd_ref, cd_ref, lam_ref, start_ref, h_ref, y_ref, drive, carry):
        @pl.when(pl.program_id(0) == 0)
        def _():
            carry[...] = start_ref[...]

        _drive(u_ref, bd_ref, drive, "nn")
        _scan_chunk(drive, lam_ref, carry, reverse=False, store=h_ref)
        for n in range(SSM_NB):
            hr = h_ref[:, n * 512:(n + 1) * 512].astype(BF16)
            hi = h_ref[:, NSTATE + n * 512:NSTATE + (n + 1) * 512].astype(BF16)
            y_ref[:, n * BLK:(n + 1) * BLK] = (jnp.dot(hr, cd_ref[n], preferred_element_type=F32)
                                              + jnp.dot(hi, cd_ref[SSM_NB + n], preferred_element_type=F32))

    return pl.pallas_call(
        body, grid=(nchunk,),
        in_specs=[_bs((SCAN_ROWS, SSM_W), lambda c: (c, 0)), _bs(bd.shape, lambda c: (0, 0, 0)), _bs(cd.shape, lambda c: (0, 0, 0)),
                  _bs((8, 2 * NSTATE), lambda c: (0, 0)), _bs((8, 2 * NSTATE), lambda c: (0, 0))],
        out_specs=[_bs((SCAN_ROWS, 2 * NSTATE), lambda c: (c, 0)), _bs((SCAN_ROWS, SSM_W), lambda c: (c, 0))],
        out_shape=[SDS((t, 2 * NSTATE), F32), SDS((t, SSM_W), F32)],
        scratch_shapes=[pltpu.VMEM((SCAN_ROWS, 2 * NSTATE), F32), pltpu.VMEM((8, 2 * NSTATE), F32)],
        compiler_params=_cp(1), name="ssm_scan_fwd")(u_bf, bd, cd, lam8, start)


def _ssm_bwd(dys_bf, u_bf, h, bd, cd, lamc8, start):
    t = u_bf.shape[0]
    nchunk = t // SCAN_ROWS

    def body(dys_ref, u_ref, h_ref, bd_ref, cd_ref, lam_ref, start_ref, du_ref, dlam_ref, dbd_ref, dcd_ref, drive, adj, carry):
        c = pl.program_id(0)

        @pl.when(c == 0)
        def _():
            carry[...] = start_ref[...]
            dlam_ref[...] = jnp.zeros_like(dlam_ref)
            dbd_ref[...] = jnp.zeros_like(dbd_ref)
            dcd_ref[...] = jnp.zeros_like(dcd_ref)

        _drive(dys_ref, cd_ref, drive, "nt")
        _scan_chunk(drive, lam_ref, carry, reverse=True, store=adj, h_ref=h_ref, acc=dlam_ref)
        for n in range(SSM_NB):
            cs = slice(n * BLK, (n + 1) * BLK)
            acc = None
            for k in range(2):
                kn = k * SSM_NB + n
                ss = slice(kn * 512, (kn + 1) * 512)
                lam_b = adj[:, ss].astype(BF16)
                part = lax.dot_general(lam_b, bd_ref[kn], _DNUMS["nt"], preferred_element_type=F32)
                acc = part if acc is None else acc + part
                dbd_ref[kn] += lax.dot_general(u_ref[:, cs], lam_b, _DNUMS["tn"], preferred_element_type=F32)
                dcd_ref[kn] += lax.dot_general(h_ref[:, ss].astype(BF16), dys_ref[:, cs], _DNUMS["tn"],
                                               preferred_element_type=F32)
            du_ref[:, cs] = acc

    rev = lambda c: (nchunk - 1 - c, 0)
    const2 = lambda c: (0, 0)
    const3 = lambda c: (0, 0, 0)
    return pl.pallas_call(
        body, grid=(nchunk,),
        in_specs=[_bs((SCAN_ROWS, SSM_W), rev), _bs((SCAN_ROWS, SSM_W), rev), _bs((SCAN_ROWS, 2 * NSTATE), rev),
                  _bs(bd.shape, const3), _bs(cd.shape, const3), _bs((8, 2 * NSTATE), const2), _bs((8, 2 * NSTATE), const2)],
        out_specs=[_bs((SCAN_ROWS, SSM_W), rev), _bs((8, 2 * NSTATE), const2), _bs(bd.shape, const3), _bs(cd.shape, const3)],
        out_shape=[SDS((t, SSM_W), F32), SDS((8, 2 * NSTATE), F32), SDS(bd.shape, F32), SDS(cd.shape, F32)],
        scratch_shapes=[pltpu.VMEM((SCAN_ROWS, 2 * NSTATE), F32), pltpu.VMEM((SCAN_ROWS, 2 * NSTATE), F32),
                        pltpu.VMEM((8, 2 * NSTATE), F32)],
        compiler_params=_cp(1), name="ssm_scan_bwd")(dys_bf, u_bf, h, bd, cd, lamc8, start)


def _gelu_parts(x):
    c0 = math.sqrt(2.0 / math.pi)
    inner = c0 * (x + 0.044715 * x * x * x)
    th = jnp.tanh(inner)
    val = 0.5 * x * (1.0 + th)
    grad = 0.5 * (1.0 + th) + 0.5 * x * (1.0 - th * th) * c0 * (1.0 + 3.0 * 0.044715 * x * x)
    return val, grad


def _ssm_out(y_raw, u, d_skip, tm):
    t = u.shape[0]
    seg = t // N_DEV

    def body(y_ref, u_ref, d_ref, ys_ref, yg_ref, tmp):
        ys = y_ref[...] + d_ref[...] * u_ref[...]
        ys_ref[...] = ys
        yg = _gelu_parts(ys)[0]
        for n in range(SSM_W // BLK):
            tmp[n] = yg[:, n * BLK:(n + 1) * BLK]
            for j in range(N_DEV):
                yg_ref[j, :, n * BLK:(n + 1) * BLK] = tmp[n, pl.ds(j, tm // N_DEV, stride=N_DEV), :].astype(BF16)

    row = _bs((tm, SSM_W), lambda i: (i, 0))
    ys, yg = pl.pallas_call(
        body, grid=(t // tm,), in_specs=[row, row, _bs((1, SSM_W), lambda i: (0, 0))],
        out_specs=[row, _bs((N_DEV, tm // N_DEV, SSM_W), lambda i: (0, i, 0))],
        out_shape=[SDS((t, SSM_W), F32), SDS((N_DEV, seg, SSM_W), BF16)],
        scratch_shapes=[pltpu.VMEM((SSM_W // BLK, tm, BLK), F32)], compiler_params=_cp(1), name="ssm_out")(y_raw, u, d_skip)
    return ys, yg.reshape(t, SSM_W)


def _ssm_out_bwd(d_yg, ys, u, tm):
    t = u.shape[0]
    seg = t // N_DEV

    def body(dg_ref, ys_ref, u_ref, dys_ref, dysb_ref, dd_ref, tmp):
        for n in range(SSM_W // BLK):
            for j in range(N_DEV):
                tmp[n, pl.ds(j, tm // N_DEV, stride=N_DEV), :] = dg_ref[j, :, n * BLK:(n + 1) * BLK]
        dyg = jnp.concatenate([tmp[n] for n in range(SSM_W // BLK)], axis=1)
        dys = dyg * _gelu_parts(ys_ref[...])[1]
        dys_ref[...] = dys
        dysb_ref[...] = dys.astype(BF16)
        part = jnp.sum(dys * u_ref[...], axis=0, keepdims=True)

        @pl.when(pl.program_id(0) == 0)
        def _():
            dd_ref[...] = part

        @pl.when(pl.program_id(0) > 0)
        def _():
            dd_ref[...] += part

    row = _bs((tm, SSM_W), lambda i: (i, 0))
    return pl.pallas_call(
        body, grid=(t // tm,), in_specs=[_bs((N_DEV, tm // N_DEV, SSM_W), lambda i: (0, i, 0)), row, row],
        out_specs=[row, row, _bs((1, SSM_W), lambda i: (0, 0))],
        out_shape=[SDS((t, SSM_W), F32), SDS((t, SSM_W), BF16), SDS((1, SSM_W), F32)],
        scratch_shapes=[pltpu.VMEM((SSM_W // BLK, tm, BLK), F32)], compiler_params=_cp(1), name="ssm_out_bwd")(
            d_yg.reshape(N_DEV, seg, SSM_W), ys, u)


def _du_to_dz(du_raw, dys, d_skip, tm):
    t = du_raw.shape[0]
    seg = t // N_DEV

    def body(du_ref, dys_ref, d_ref, o_ref, tmp):
        du = du_ref[...] + d_ref[...] * dys_ref[...]
        for n in range(SSM_W // BLK):
            tmp[n] = du[:, n * BLK:(n + 1) * BLK]
            for j in range(N_DEV):
                o_ref[j, :, n * BLK:(n + 1) * BLK] = tmp[n, pl.ds(j, tm // N_DEV, stride=N_DEV), :].astype(BF16)

    row = _bs((tm, SSM_W), lambda i: (i, 0))
    out = pl.pallas_call(
        body, grid=(t // tm,), in_specs=[row, row, _bs((1, SSM_W), lambda i: (0, 0))],
        out_specs=_bs((N_DEV, tm // N_DEV, SSM_W), lambda i: (0, i, 0)), out_shape=SDS((N_DEV, seg, SSM_W), BF16),
        scratch_shapes=[pltpu.VMEM((SSM_W // BLK, tm, BLK), F32)], compiler_params=_cp(1), name="du_to_dz")(du_raw, dys, d_skip)
    return out.reshape(t, SSM_W)


def _block_diag(blocks):
    nb, ng, r, c = blocks.shape
    eye = jnp.eye(ng, dtype=blocks.dtype)
    return (blocks[:, :, :, None, :] * eye[None, :, None, :, None]).reshape(nb, ng * r, ng * c)


def _diag_blocks(full, r, c):
    k, nb = full.shape[:2]
    ng = full.shape[2] // r
    x = full.reshape(k, nb, ng, r, ng, c)
    eye = jnp.eye(ng, dtype=full.dtype)
    return jnp.sum(x * eye[None, None, :, None, :, None], axis=4).reshape(k, nb * ng, r, c)


_SMALL = ("g_mix", "a_re", "a_im", "log_dt", "b_re", "b_im", "c_re", "c_im", "d_skip", "g_ffn", "g_final")


def _pack_small(arrs):
    flat = jnp.concatenate([a.reshape(-1) for a in arrs])
    pad = (-flat.shape[0]) % (8 * 128)
    return jnp.pad(flat, (0, pad)).reshape(-1, 128)


def _unpack_small(packed, shapes):
    flat = packed.reshape(-1)
    out, off = [], 0
    for s in shapes:
        n = math.prod(s)
        out.append(flat[off:off + n].reshape(s))
        off += n
    return out


def kernel(x, p, positions, g_mix, w_in, a_re, a_im, log_dt, b_re, b_im, c_re, c_im, d_skip, w_attn_proj, w_glu_a, w_glu_b, w_out, g_ffn, w_ffn_gate, w_ffn_up, w_ffn_down, w_ple_gate, w_ple_proj, g_final, loss_target, m_g_mix, m_w_in, m_a_re, m_a_im, m_log_dt, m_b_re, m_b_im, m_c_re, m_c_im, m_d_skip, m_w_attn_proj, m_w_glu_a, m_w_glu_b, m_w_out, m_g_ffn, m_w_ffn_gate, m_w_ffn_up, m_w_ffn_down, m_w_ple_gate, m_w_ple_proj, m_g_final, v_g_mix, v_w_in, v_a_re, v_a_im, v_log_dt, v_b_re, v_b_im, v_c_re, v_c_im, v_d_skip, v_w_attn_proj, v_w_glu_a, v_w_glu_b, v_w_out, v_g_ffn, v_w_ffn_gate, v_w_ffn_up, v_w_ffn_down, v_w_ple_gate, v_w_ple_proj, v_g_final):
    args = dict(locals())
    t, d = x.shape[1], x.shape[2]
    inw = w_in.shape[2] * N_DEV
    fs = w_ffn_gate.shape[2]
    ff = fs * N_DEV
    ple = w_ple_proj.shape[1]
    seg = t // N_DEV
    assert inw == 3 * QK_W + SSM_W + 2 * d and t % (N_DEV * SCAN_ROWS // 8) == 0 and seg & (seg - 1) == 0
    tm = min(1024, t)
    te = min(512, t)
    tk = min(512, t)
    ucol = (3 * QK_W) // SSM_W
    gcol = (3 * QK_W + SSM_W) // d
    assert (3 * QK_W + SSM_W) % d == 0

    x2, p2, tgt = x[0], p[0, 0], loss_target[0]
    pos = positions.reshape(t, 1)
    inv = ROPE_THETA ** (-jnp.arange(ROPE_HALF, dtype=F32) * 2.0 / ROPE_DIM)
    invf = jnp.concatenate([inv, inv, jnp.zeros((HEAD_DIM - ROPE_DIM,), F32)]).reshape(1, HEAD_DIM)

    wnames = ("w_in", "w_attn_proj", "w_glu_a", "w_glu_b", "w_out", "w_ffn_gate", "w_ffn_up", "w_ffn_down", "w_ple_gate",
              "w_ple_proj")
    kinds = ("cols", "cols", "cols", "cols", "rows", "slot", "slot", "rows", "rows", "cols")
    shards = [args[n][0].astype(BF16) for n in wnames]
    sizes = [s.shape[0] if k == "rows" else s.shape[-1] for s, k in zip(shards, kinds)]
    ag_in = _exchange_start("gather_w_in_start", shards[:1], kinds[:1], sizes[:1], True)
    rest = shards[1:-1] + [shards[-1] + ag_in[4][0, 0].astype(BF16)]
    ag_rest = _exchange_start("gather_rest_start", rest, kinds[1:], sizes[1:], True)

    row_d = _bs((tm, d), lambda i, j, k: (i, 0))
    row_e = _bs((te, d), lambda i, j, k: (i, 0))
    vec_d = _bs((1, d), lambda i, j, k: (0, 0))
    sq_w = _bs((d, d), lambda i, j, k: (0, 0))
    n1 = _rms_fwd("norm_mix", x2, g_mix + ag_rest[4][0:1, 0:1], tm)
    W_in, = _exchange_wait("gather_w_in_wait", ag_in, kinds[:1], sizes[:1], True, n1)
    tn_in = _pick(inw, (1024, 512, 256, 128))
    z, = _mm("z_proj", (t // tm, inw // tn_in, 1),
             [("nn", n1, row_d, W_in, _bs((d, tn_in), lambda i, j, k: (0, j)))],
             [(SDS((t, inw), F32), _bs((tm, tn_in), lambda i, j, k: (i, j)))])

    qkv = _rope_dilate(z, pos, invf, tm)
    outs, lses = [], []
    for g, dil in enumerate(DILATIONS):
        o_g, l_g = _attn_fwd(qkv[g], dil, min(512, t // dil))
        outs.append(o_g)
        lses.append(l_g)
    merged = _attn_merge(outs, lses, te)
    attn, attn_bf, lts = merged[0], merged[1], merged[2:]

    nsq = seg.bit_length() - 1
    bar_re, bar_im, z_re, z_im, pw_re, pw_im = _ssm_disc(a_re[0], a_im[0], log_dt.reshape(SSM_GROUPS, 1), nsq)
    gp = SSM_GROUPS * SSM_STATE
    b_re2, b_im2 = b_re.reshape(gp, SSM_GROUP), b_im.reshape(gp, SSM_GROUP)
    bb_re, bb_im = _ssm_scale_b(z_re.reshape(gp, 1), z_im.reshape(gp, 1), b_re2, b_im2)

    def chunks(a, r, c):
        return a.reshape(SSM_NB, SSM_GROUPS // SSM_NB, r, c)

    bbt = lambda a: jnp.swapaxes(a.reshape(SSM_GROUPS, SSM_STATE, SSM_GROUP), 1, 2)
    bd = jnp.concatenate([_block_diag(chunks(bbt(bb_re), SSM_GROUP, SSM_STATE)),
                          _block_diag(chunks(bbt(bb_im), SSM_GROUP, SSM_STATE))]).astype(BF16)
    ct = lambda a: jnp.swapaxes(a[0], 1, 2)
    cd = jnp.concatenate([_block_diag(chunks(ct(c_re), SSM_STATE, SSM_GROUP)),
                          _block_diag(chunks(-ct(c_im), SSM_STATE, SSM_GROUP))]).astype(BF16)
    lam = jnp.concatenate([bar_re.reshape(1, gp), bar_im.reshape(1, gp)], axis=1)
    lamc = jnp.concatenate([bar_re.reshape(1, gp), -bar_im.reshape(1, gp)], axis=1)
    pw = jnp.concatenate([pw_re.reshape(1, gp), pw_im.reshape(1, gp)], axis=1)
    pwc = jnp.concatenate([pw_re.reshape(1, gp), -pw_im.reshape(1, gp)], axis=1)
    lam8, lamc8 = jnp.broadcast_to(lam, (8, 2 * gp)), jnp.broadcast_to(lamc, (8, 2 * gp))

    u_perm, u_bf = _permute_u(z, ucol, te)
    start_f = _ssm_carries("ssm_carries_fwd", u_bf, bd, "nn", lam8, pw, False)
    h_all, y_raw = _ssm_fwd(u_bf, bd, cd, lam8, start_f)
    dsk = d_skip.reshape(1, SSM_W)
    ys, yg_bf = _ssm_out(y_raw, u_perm, dsk, te)
    W_ap, W_ga, W_gb, W_out, W_fg, W_fu, W_fd, W_pg, W_pp = _exchange_wait("gather_rest_wait", ag_rest, kinds[1:], sizes[1:], True, yg_bf)
    W_fg = jnp.swapaxes(W_fg, 0, 1).reshape(d, ff)
    W_fu = jnp.swapaxes(W_fu, 0, 1).reshape(d, ff)

    glu_w = _bs((SSM_W, d), lambda i, j, k: (0, 0))
    row_es = _bs((te, SSM_W), lambda i, j, k: (i, 0))
    gate_a = _bs((te, d), lambda i, j, k: (i, gcol))
    gate_s = _bs((te, d), lambda i, j, k: (i, gcol + 1))
    td_f32, td_bf = SDS((t, d), F32), SDS((t, d), BF16)
    m_bf, ya, yb, attn_d = _mm(
        "glu_merge", (t // te, 1, 1),
        [("nn", yg_bf, row_es, W_ga, glu_w), ("nn", yg_bf, row_es, W_gb, glu_w), ("nn", attn_bf, row_es, W_ap, glu_w)],
        [(td_bf, row_e), (td_f32, row_e), (td_f32, row_e), (td_f32, row_e)],
        extras=[(z, gate_a), (z, gate_s)], epilogue=_glu_merge_epilogue)

    h1, n2 = _mm("out_proj", (t // tm, 1, 1), [("nn", m_bf, row_d, W_out, sq_w)], [(td_f32, row_d), (td_bf, row_d)],
                 extras=[(x2, row_d), (g_ffn, vec_d)], epilogue=_out_norm_epilogue)

    tn_f = ff // 2
    nf = ff // tn_f
    hid_o = _bs((te, tn_f), lambda j, i, k: (i, j))
    tf_f32, tf_bf = SDS((t, ff), F32), SDS((t, ff), BF16)
    a_rows = _bs((te, d), lambda j, i, k: (i, 0))
    w_cols = _bs((d, tn_f), lambda j, i, k: (0, j))
    act, fg, fu = _mm("ffn_gate_up", (nf, t // te, 1), [("nn", n2, a_rows, W_fg, w_cols), ("nn", n2, a_rows, W_fu, w_cols)],
                      [(tf_bf, hid_o), (tf_f32, hid_o), (tf_f32, hid_o)], epilogue=_swiglu_epilogue)
    h2, h2_bf = _mm("ffn_down", (t // tm, 1, nf),
                    [("nn", act, _bs((tm, tn_f), lambda i, j, k: (i, k)), W_fd, _bs((tn_f, d), lambda i, j, k: (k, 0)))],
                    [(td_f32, row_d), (td_bf, row_d)], extras=[(h1, row_d)])

    loss_part, dg_final, dh3, dpp_bf, dpg_bf = _mm(
        "ple_head", (t // te, 1, 1),
        [("nn", h2_bf, row_e, W_pg, sq_w), ("nn", p2, _bs((te, ple), lambda i, j, k: (i, 0)), W_pp, _bs((ple, d), lambda i, j, k: (0, 0)))],
        [(SDS((1, 1), F32), _bs((1, 1), lambda i, j, k: (0, 0))), (SDS((1, d), F32), vec_d), (td_f32, row_e), (td_bf, row_e),
         (td_bf, row_e)],
        extras=[(h2, row_e), (g_final.reshape(1, d), vec_d), (tgt, row_e)], epilogue=_head_epilogue(t // te),
        scratch=[pltpu.VMEM((1, d), F32)])
    loss = lax.psum(loss_part[0, 0], ("x", "y", "c"))

    nkt = t // tk
    tok_a = lambda w: _bs((tk, w), lambda i, j, k: (k, 0))

    def wgrad(name, a, wa, b, wb):
        return _mm(name, (1, 1, nkt), [("tn", a, tok_a(wa), b, tok_a(wb))],
                   [(SDS((wa, wb), F32), _bs((wa, wb), lambda i, j, k: (0, 0)))])[0]

    dW_pp = wgrad("dw_ple_proj", p2, ple, dpp_bf, d)
    dW_pg = wgrad("dw_ple_gate", h2_bf, d, dpg_bf, d)
    dh2, dh2_bf = _mm("d_ple_gate", (t // tm, 1, 1), [("nt", dpg_bf, row_d, W_pg, sq_w)], [(td_f32, row_d), (td_bf, row_d)],
                      extras=[(dh3, row_d)])

    dfg_bf, dfu_bf = _mm("d_ffn_down", (nf, t // te, 1),
                         [("nt", dh2_bf, a_rows, W_fd, _bs((tn_f, d), lambda j, i, k: (j, 0)))],
                         [(tf_bf, hid_o), (tf_bf, hid_o)], extras=[(fg, hid_o), (fu, hid_o)], epilogue=_swiglu_bwd_epilogue)
    dW_fd, = _mm("dw_ffn_down", (nf, 1, nkt), [("tn", act, _bs((tk, tn_f), lambda i, j, k: (k, i)), dh2_bf, tok_a(d))],
                 [(SDS((ff, d), F32), _bs((tn_f, d), lambda i, j, k: (i, 0)))])
    hid_t = _bs((tk, tn_f), lambda i, j, k: (k, j))
    wg_o = [(SDS((d, ff), F32), _bs((d, tn_f), lambda i, j, k: (0, j)))]
    dW_fg, = _mm("dw_ffn_gate", (1, nf, nkt), [("tn", n2, tok_a(d), dfg_bf, hid_t)], wg_o)
    dW_fu, = _mm("dw_ffn_up", (1, nf, nkt), [("tn", n2, tok_a(d), dfu_bf, hid_t)], wg_o)
    dW_fg = jnp.swapaxes(dW_fg.reshape(d, N_DEV, fs), 0, 1)
    dW_fu = jnp.swapaxes(dW_fu.reshape(d, N_DEV, fs), 0, 1)
    group = lambda names: ([kinds[wnames.index(n)] for n in names], [sizes[wnames.index(n)] for n in names])
    ffn_names = ("w_ffn_gate", "w_ffn_up", "w_ffn_down", "w_ple_gate", "w_ple_proj")
    rs_ffn = _exchange_start("scatter_ffn_start", [dW_fg, dW_fu, dW_fd, dW_pg, dW_pp], *group(ffn_names), False)
    hid_k = _bs((te, tn_f), lambda i, j, k: (i, k))
    w_k = _bs((d, tn_f), lambda i, j, k: (0, k))
    dh1, dh1_bf, dg_ffn = _mm("d_ffn_gate_up", (t // te, 1, nf), [("nt", dfg_bf, hid_k, W_fg, w_k), ("nt", dfu_bf, hid_k, W_fu, w_k)],
                              [(td_f32, row_e), (td_bf, row_e), (SDS((1, d), F32), vec_d)],
                              extras=[(h1, row_e), (g_ffn, vec_d), (dh2, row_e)], epilogue=_rms_bwd_epilogue, after=rs_ffn[4])

    dW_out = wgrad("dw_out", m_bf, d, dh1_bf, d)
    dz_g, dad_bf, dya_bf, dyb_bf = _mm(
        "d_out_proj", (t // te, 1, 1), [("nt", dh1_bf, row_e, W_out, sq_w)],
        [(SDS((t, 2 * d), BF16), _bs((te, 2 * d), lambda i, j, k: (i, 0))), (td_bf, row_e), (td_bf, row_e), (td_bf, row_e)],
        extras=[(z, gate_a), (z, gate_s), (attn_d, row_e), (ya, row_e), (yb, row_e)], epilogue=_merge_bwd_epilogue)

    row_s = _bs((tm, SSM_W), lambda i, j, k: (i, 0))
    d_yg, = _mm("d_glu", (t // tm, 1, 1), [("nt", dya_bf, row_d, W_ga, glu_w), ("nt", dyb_bf, row_d, W_gb, glu_w)],
                [(SDS((t, SSM_W), F32), row_s)])
    dW_ga = wgrad("dw_glu_a", yg_bf, SSM_W, dya_bf, d)
    dW_gb = wgrad("dw_glu_b", yg_bf, SSM_W, dyb_bf, d)
    dys, dys_bf, dd_skip = _ssm_out_bwd(d_yg, ys, u_perm, te)
    start_b = _ssm_carries("ssm_carries_bwd", dys_bf, cd, "nt", lamc8, pwc, True)
    du_raw, dlam8, dbd, dcd = _ssm_bwd(dys_bf, u_bf, h_all, bd, cd, lamc8, start_b)
    dz_u = _du_to_dz(du_raw, dys, dsk, te)
    dlam = jnp.sum(dlam8, axis=0)
    dbb = _diag_blocks(dbd.reshape(2, SSM_NB, BLK, 512), SSM_GROUP, SSM_STATE)
    dbb_re = jnp.swapaxes(dbb[0], 1, 2).reshape(gp, SSM_GROUP)
    dbb_im = jnp.swapaxes(dbb[1], 1, 2).reshape(gp, SSM_GROUP)
    dcc = _diag_blocks(dcd.reshape(2, SSM_NB, 512, BLK), SSM_STATE, SSM_GROUP)
    dc_re, dc_im = jnp.swapaxes(dcc[0], 1, 2), -jnp.swapaxes(dcc[1], 1, 2)
    db_re, db_im, dz_re, dz_im = _ssm_scale_b_bwd(z_re.reshape(gp, 1), z_im.reshape(gp, 1), b_re2, b_im2, dbb_re, dbb_im)
    gshape = (SSM_GROUPS, SSM_STATE)
    da_re, da_im, dlog_dt = _ssm_disc_bwd(a_re[0], a_im[0], log_dt.reshape(SSM_GROUPS, 1), dlam[:gp].reshape(gshape),
                                          dlam[gp:].reshape(gshape), dz_re.reshape(gshape), dz_im.reshape(gshape))

    d_attn, = _mm("d_attn_proj", (t // tm, 1, 1), [("nt", dad_bf, row_d, W_ap, glu_w)], [(SDS((t, GROUP_W), F32), row_s)])
    dW_ap = wgrad("dw_attn_proj", attn_bf, GROUP_W, dad_bf, d)
    mix_names = ("w_attn_proj", "w_glu_a", "w_glu_b", "w_out")
    rs_mix = _exchange_start("scatter_mix_start", [dW_ap, dW_ga, dW_gb, dW_out], *group(mix_names), False)
    pre = _attn_bwd_pre(d_attn, attn, te, rs_mix[4])
    das, deltas = pre[:N_GROUPS], pre[N_GROUPS:]
    dqkvs = [_attn_bwd(qkv[g], das[g], lts[g], deltas[g], dil, min(512, t // dil)) for g, dil in enumerate(DILATIONS)]
    dz_qkv = _undilate_rope_bwd(dqkvs, pos, invf, tm)

    dW_in, = _mm("dw_in_qkv", (1, 3, nkt), [("tn", n1, tok_a(d), dz_qkv, _bs((tk, QK_W), lambda i, j, k: (k, j)))],
                 [(SDS((d, inw), F32), _bs((d, QK_W), lambda i, j, k: (0, j)))])
    dW_in, = _mm("dw_in_u", (1, 1, nkt), [("tn", n1, tok_a(d), dz_u, tok_a(SSM_W))],
                 [(SDS((d, inw), F32), _bs((d, SSM_W), lambda i, j, k: (0, ucol)))], alias_to_out0=dW_in)
    dW_in, = _mm("dw_in_gates", (1, 2, nkt), [("tn", n1, tok_a(d), dz_g, _bs((tk, d), lambda i, j, k: (k, j)))],
                 [(SDS((d, inw), F32), _bs((d, d), lambda i, j, k: (0, gcol + j)))], alias_to_out0=dW_in)
    rs_in = _exchange_start("scatter_w_in_start", [dW_in], *group(("w_in",)), False)
    dx, dg_mix = _mm(
        "d_z_proj", (t // te, 1, 3),
        [("nt", dz_qkv, _bs((te, QK_W), lambda i, j, k: (i, k)), W_in, _bs((d, QK_W), lambda i, j, k: (0, k)), 3),
         ("nt", dz_u, _bs((te, SSM_W), lambda i, j, k: (i, 0)), W_in, _bs((d, SSM_W), lambda i, j, k: (0, ucol)), 1),
         ("nt", dz_g, _bs((te, d), lambda i, j, k: (i, jnp.minimum(k, 1))), W_in,
          _bs((d, d), lambda i, j, k: (0, gcol + jnp.minimum(k, 1))), 2)],
        [(td_f32, row_e), (SDS((1, d), F32), vec_d)],
        extras=[(x2, row_e), (g_mix, vec_d), (dh1, row_e)], epilogue=_rms_bwd_epilogue, after=rs_in[4])

    small_parts = dict(g_mix=dg_mix, a_re=da_re, a_im=da_im, log_dt=dlog_dt, b_re=db_re, b_im=db_im, c_re=dc_re, c_im=dc_im,
                       d_skip=dd_skip, g_ffn=dg_ffn, g_final=dg_final)
    small = _pack_small([small_parts[n] for n in _SMALL])
    received = {}
    for names, started, label in ((ffn_names, rs_ffn, "ffn"), (mix_names, rs_mix, "mix"), (("w_in",), rs_in, "w_in")):
        landed = _exchange_wait(f"scatter_{label}_wait", started, *group(names), False, dx)
        received.update(zip(names, landed))

    new = {}
    for n in wnames:
        new[n] = [o.reshape(args[n].shape)
                  for o in _adamw("adamw_" + n, received[n], args[n][0], args["m_" + n][0], args["v_" + n][0])]
    pk = lambda pre: _pack_small([args[pre + n] for n in _SMALL])
    sm = _adamw("adamw_small", _gather_small(small), pk(""), pk("m_"), pk("v_"))
    shapes = [args[n].shape for n in _SMALL]
    for n, vals in zip(_SMALL, zip(*[_unpack_small(o, shapes) for o in sm])):
        new[n] = list(vals)

    order = ("g_mix", "w_in", "a_re", "a_im", "log_dt", "b_re", "b_im", "c_re", "c_im", "d_skip", "w_attn_proj", "w_glu_a",
             "w_glu_b", "w_out", "g_ffn", "w_ffn_gate", "w_ffn_up", "w_ffn_down", "w_ple_gate", "w_ple_proj", "g_final")
    return (loss, dx.reshape(x.shape), *[new[n][0] for n in order], *[new[n][1] for n in order],
            *[new[n][2] for n in order], *[new[n][3] for n in order])
```

```python
import functools
import math

import jax
import jax.numpy as jnp
from jax import lax
from jax.experimental import pallas as pl
from jax.experimental.pallas import tpu as pltpu

F32 = jnp.float32
BF16 = jnp.bfloat16
SDS = jax.ShapeDtypeStruct

N_DEV = 8
HEAD_DIM = 128
HEADS_PER_GROUP = 4
GROUP_W = HEADS_PER_GROUP * HEAD_DIM
DILATIONS = (1, 4, 16)
N_GROUPS = len(DILATIONS)
QK_W = N_GROUPS * GROUP_W
BLK = 128
ROPE_THETA = 500000.0
ROPE_DIM = HEAD_DIM // 4
ROPE_HALF = ROPE_DIM // 2
SSM_W = 512
SSM_GROUP = 16
SSM_GROUPS = SSM_W // SSM_GROUP
SSM_STATE = 64
NSTATE = SSM_GROUPS * SSM_STATE
SSM_NB = 4
EPS = 1e-6
ADAM_LR, ADAM_B1, ADAM_B2, ADAM_EPS, ADAM_WD, ADAM_STEP = 0.001, 0.9, 0.999, 1e-08, 0.01, 10
NEG = -1e30

VMEM_LIMIT = 52 * 1024 * 1024
SCAN_ROWS = 256
SCAN_LANES = 512


def _cp(n):
    return pltpu.CompilerParams(dimension_semantics=("arbitrary",) * n, vmem_limit_bytes=VMEM_LIMIT)


def _sigmoid(x):
    return 1.0 / (1.0 + jnp.exp(-x))


_DNUMS = {"nn": (((1,), (0,)), ((), ())), "nt": (((1,), (1,)), ((), ())), "tn": (((0,), (0,)), ((), ()))}


def _bs(shape, fn):
    return pl.BlockSpec(shape, fn)


def _store_all(prods, extra_refs, out_refs, scratch_refs):
    r = prods[0]
    for p in prods[1:]:
        r = r + p
    for e in extra_refs:
        r = r + e[...]
    for o in out_refs:
        o[...] = r.astype(o.dtype)


def _mm(name, grid, pairs, outs, extras=(), epilogue=_store_all, scratch=(), alias_to_out0=None, after=None):
    nk = grid[2]
    npair = len(pairs)
    steps = [p[5] if len(p) > 5 else nk for p in pairs]

    def block(spec):
        return tuple(s for s in spec.block_shape if s is not None)

    acc_shapes = [jax.eval_shape(lambda u, v, dn=_DNUMS[p[0]]: lax.dot_general(u, v, dn, preferred_element_type=F32),
                                 SDS(block(p[2]), BF16), SDS(block(p[4]), BF16)).shape for p in pairs]
    n_in = 2 * npair + len(extras) + (alias_to_out0 is not None) + (after is not None)

    def body(*refs):
        extra_refs = refs[2 * npair:2 * npair + len(extras)]
        out_refs = refs[n_in:n_in + len(outs)]
        rest = refs[n_in + len(outs):]
        acc_refs = rest[:npair] if nk > 1 else ()
        scratch_refs = rest[len(acc_refs):]
        k = pl.program_id(2)

        def product(i):
            return lax.dot_general(refs[2 * i][...].astype(BF16), refs[2 * i + 1][...].astype(BF16), _DNUMS[pairs[i][0]],
                                   preferred_element_type=F32)

        if nk == 1:
            epilogue([product(i) for i in range(npair)], extra_refs, out_refs, scratch_refs)
            return
        for i in range(npair):
            @pl.when(k == 0)
            def _(i=i):
                acc_refs[i][...] = product(i)

            @pl.when((k > 0) & (k < steps[i]))
            def _(i=i):
                acc_refs[i][...] += product(i)

        @pl.when(k == nk - 1)
        def _():
            epilogue([a[...] for a in acc_refs], extra_refs, out_refs, scratch_refs)

    ins, in_specs = [], []
    for p in pairs:
        ins += [p[1], p[3]]
        in_specs += [p[2], p[4]]
    ins += [e[0] for e in extras]
    in_specs += [e[1] for e in extras]
    aliases = {}
    if alias_to_out0 is not None:
        aliases = {len(ins): 0}
        ins.append(alias_to_out0)
        in_specs.append(pl.BlockSpec(memory_space=pl.ANY))
    if after is not None:
        ins.append(after)
        in_specs.append(pl.BlockSpec(memory_space=pl.ANY))
    scratch_shapes =([pltpu.VMEM(s, F32) for s in acc_shapes] if nk > 1 else []) + list(scratch)
    return pl.pallas_call(body, grid=grid, in_specs=in_specs, out_specs=[o[1] for o in outs], out_shape=[o[0] for o in outs],
                          scratch_shapes=scratch_shapes, input_output_aliases=aliases, compiler_params=_cp(3), name=name)(*ins)


def _pick(n, cands):
    for c in cands:
        if n % c == 0:
            return c
    return n


def _my_index():
    return 4 * lax.axis_index("x") + 2 * lax.axis_index("y") + lax.axis_index("c")


def _peer(d):
    mx, my, mc = lax.axis_index("x"), lax.axis_index("y"), lax.axis_index("c")
    return (mx ^ ((d >> 2) & 1), my ^ ((d >> 1) & 1), mc ^ (d & 1))


def _win(ref, kind, j, n):
    if kind == "all":
        return ref
    if kind == "slot":
        return ref.at[j]
    if kind == "rows":
        return ref.at[pl.ds(pl.multiple_of(j * n, 8), n)]
    return ref.at[:, pl.ds(pl.multiple_of(j * n, 128), n)]


def _win7(ref, kind, n):
    if kind == "slot":
        return ref.at[pl.ds(0, 7)]
    if kind == "rows":
        return ref.at[pl.ds(0, 7 * n)]
    return ref.at[:, pl.ds(0, 7 * n)]


def _full_shape(shard_shape, kind):
    if kind == "slot":
        return (N_DEV,) + tuple(shard_shape)
    if kind == "rows":
        return (N_DEV * shard_shape[0],) + tuple(shard_shape[1:])
    return (shard_shape[0], N_DEV * shard_shape[1])


def _shard_shape(full_shape, kind, n):
    if kind == "all":
        return tuple(full_shape)
    if kind == "slot":
        return tuple(full_shape[1:])
    if kind == "rows":
        return (n,) + tuple(full_shape[1:])
    return (full_shape[0], n)


_HBM = pl.BlockSpec(memory_space=pltpu.HBM)
_SEM = pl.BlockSpec(memory_space=pltpu.SEMAPHORE)
_DATAFLOW = pltpu.SideEffectType.DATAFLOW_SIDE_EFFECTING


def _exchange_start(name, srcs, kinds, sizes, gather):
    n = len(srcs)
    if gather:
        lands = [lax.empty(_full_shape(s.shape, k), s.dtype) for s, k in zip(srcs, kinds)]
    else:
        lands = [lax.empty((N_DEV,) + _shard_shape(s.shape, k, z), s.dtype) for s, k, z in zip(srcs, kinds, sizes)]

    def body(*refs):
        src, land = refs[:n], refs[n:2 * n]
        send_sems, recv_sems, local_sems = refs[2 * n], refs[2 * n + 1], refs[2 * n + 2]
        token = refs[4 * n + 3]
        me = _my_index()
        for a in range(n):
            _local_copy(src[a], land[a], kinds[a], sizes[a], gather, me, local_sems.at[a]).start()
        for a in range(n):
            for d in range(1, N_DEV):
                px, py, pc = _peer(d)
                if gather:
                    s_ref, d_ref = src[a], _win(land[a], kinds[a], me, sizes[a])
                else:
                    s_ref, d_ref = _win(src[a], kinds[a], 4 * px + 2 * py + pc, sizes[a]), land[a].at[me]
                pltpu.make_async_remote_copy(src_ref=s_ref, dst_ref=d_ref, send_sem=send_sems.at[a], recv_sem=recv_sems.at[a],
                                             device_id=(px, py, pc), device_id_type=pl.DeviceIdType.MESH).start()
        token[...] = jnp.zeros_like(token)

    hbm = [pltpu.with_memory_space_constraint(a, pltpu.HBM) for a in list(srcs) + lands]
    out = pl.pallas_call(
        body, name=name, in_specs=[_HBM] * (2 * n),
        out_shape=[pltpu.SemaphoreType.DMA((n,))] * 3 + [pltpu.HBM(a.shape, a.dtype) for a in hbm] + [SDS((8, 128), F32)],
        out_specs=[_SEM] * 3 + [_HBM] * (2 * n) + [pl.BlockSpec(memory_space=pltpu.VMEM)],
        input_output_aliases={i: 3 + i for i in range(2 * n)},
        compiler_params=pltpu.CompilerParams(has_side_effects=_DATAFLOW))(*hbm)
    return out[0:3], out[3:3 + n], out[3 + n:3 + 2 * n], out[-1]


def _local_copy(src, land, kind, size, gather, me, sem):
    if gather:
        return pltpu.make_async_copy(src, _win(land, kind, me, size), sem)
    return pltpu.make_async_copy(_win(src, kind, me, size), land.at[me], sem)


def _exchange_wait(name, started, which, kinds, sizes, gather, after):
    sems, srcs, lands, _ = started
    n = len(which)

    def body(*refs):
        src, land = refs[:n], refs[n:2 * n]
        send_ref, recv_ref, local_ref = refs[2 * n:2 * n + 3]
        me = _my_index()
        my_id = (lax.axis_index("x"), lax.axis_index("y"), lax.axis_index("c"))
        for i, a in enumerate(which):
            seven = _win7(land[i], kinds[a], sizes[a]) if gather else land[i].at[pl.ds(0, 7)]
            pltpu.make_async_remote_copy(src_ref=seven, dst_ref=seven, send_sem=send_ref.at[a], recv_sem=recv_ref.at[a],
                                         device_id=my_id, device_id_type=pl.DeviceIdType.MESH).wait()
            _local_copy(src[i], land[i], kinds[a], sizes[a], gather, me, local_ref.at[a]).wait()

    hbm = [srcs[a] for a in which] + [lands[a] for a in which]
    out = pl.pallas_call(
        body, name=name, in_specs=[_HBM] * (2 * n) + [_SEM] * 3 + [pl.BlockSpec(memory_space=pl.ANY)],
        out_shape=[pltpu.HBM(a.shape, a.dtype) for a in hbm], out_specs=[_HBM] * (2 * n),
        input_output_aliases={i: i for i in range(2 * n)},
        compiler_params=pltpu.CompilerParams(has_side_effects=_DATAFLOW))(*hbm, *sems, after)
    return out[n:]


def _gather_small(small):
    def body(in_ref, out_ref, send_sem, recv_sem, local_sem):
        me = _my_index()
        my_id = (lax.axis_index("x"), lax.axis_index("y"), lax.axis_index("c"))
        cp = pltpu.make_async_copy(in_ref, out_ref.at[me], local_sem)
        cp.start()
        for d in range(1, N_DEV):
            pltpu.make_async_remote_copy(src_ref=in_ref, dst_ref=out_ref.at[me], send_sem=send_sem, recv_sem=recv_sem,
                                         device_id=_peer(d), device_id_type=pl.DeviceIdType.MESH).start()
        seven = out_ref.at[pl.ds(0, 7)]
        pltpu.make_async_remote_copy(src_ref=seven, dst_ref=seven, send_sem=send_sem, recv_sem=recv_sem, device_id=my_id,
                                     device_id_type=pl.DeviceIdType.MESH).wait()
        cp.wait()

    any_spec = pl.BlockSpec(memory_space=pl.ANY)
    return pl.pallas_call(body, in_specs=[any_spec], out_specs=any_spec, out_shape=SDS((N_DEV,) + small.shape, F32),
                          scratch_shapes=[pltpu.SemaphoreType.DMA] * 3, name="gather_small")(small)


def _adamw(name, recv, w, m, v):
    rows, cols = w.shape
    tr = max(c for c in range(16, 257, 16) if rows % c == 0) if rows % 16 == 0 else rows

    def body(r_ref, w_ref, m_ref, v_ref, g_ref, d_ref, nm_ref, nv_ref):
        g = r_ref[0].astype(F32)
        for s in range(1, N_DEV):
            g = g + r_ref[s].astype(F32)
        nm = ADAM_B1 * m_ref[...] + (1.0 - ADAM_B1) * g
        nv = ADAM_B2 * v_ref[...] + (1.0 - ADAM_B2) * (g * g)
        m_hat = nm / (1.0 - ADAM_B1 ** ADAM_STEP)
        v_hat = nv / (1.0 - ADAM_B2 ** ADAM_STEP)
        g_ref[...] = g
        d_ref[...] = -ADAM_LR * (m_hat / (jnp.sqrt(v_hat) + ADAM_EPS) + ADAM_WD * w_ref[...])
        nm_ref[...] = nm
        nv_ref[...] = nv

    blk = _bs((tr, cols), lambda i: (i, 0))
    return pl.pallas_call(
        body, grid=(rows // tr,), in_specs=[_bs((N_DEV, tr, cols), lambda i: (0, i, 0)), blk, blk, blk],
        out_specs=[blk] * 4, out_shape=[SDS((rows, cols), F32)] * 4, compiler_params=_cp(1), name=name)(recv, w, m, v)


def _rms_fwd(name, x, g, tm):
    t, d = x.shape

    def body(x_ref, g_ref, n_ref):
        xv = x_ref[...]
        r = lax.rsqrt(jnp.mean(xv * xv, axis=-1, keepdims=True) + EPS)
        n_ref[...] = (xv * r * g_ref[...]).astype(BF16)

    return pl.pallas_call(body, grid=(t // tm,), in_specs=[_bs((tm, d), lambda i: (i, 0)), _bs((1, d), lambda i: (0, 0))],
                          out_specs=_bs((tm, d), lambda i: (i, 0)), out_shape=SDS((t, d), BF16), compiler_params=_cp(1),
                          name=name)(x, g)


def _accumulate_rows(ref, part):
    @pl.when(pl.program_id(0) == 0)
    def _():
        ref[...] = part

    @pl.when(pl.program_id(0) > 0)
    def _():
        ref[...] += part


def _rms_bwd_epilogue(prods, extra_refs, out_refs, scratch_refs):
    dyv = prods[0]
    for p in prods[1:]:
        dyv = dyv + p
    if len(extra_refs) > 3:
        dyv = dyv + extra_refs[3][...]
    xv = extra_refs[0][...]
    r = lax.rsqrt(jnp.mean(xv * xv, axis=-1, keepdims=True) + EPS)
    xh = xv * r
    dxh = dyv * extra_refs[1][...]
    dx = extra_refs[2][...] + r * (dxh - xh * jnp.mean(dxh * xh, axis=-1, keepdims=True))
    for o in out_refs[:-1]:
        o[...] = dx.astype(o.dtype)
    _accumulate_rows(out_refs[-1], jnp.sum(dyv * xh, axis=0, keepdims=True))


def _out_norm_epilogue(prods, extra_refs, out_refs, scratch_refs):
    h = prods[0] + extra_refs[0][...]
    r = lax.rsqrt(jnp.mean(h * h, axis=-1, keepdims=True) + EPS)
    out_refs[0][...] = h
    out_refs[1][...] = (h * r * extra_refs[1][...]).astype(BF16)


def _glu_merge_epilogue(prods, extra_refs, out_refs, scratch_refs):
    ya, yb, ad = prods
    m = _sigmoid(extra_refs[0][...]) * ad + _sigmoid(extra_refs[1][...]) * (ya * _sigmoid(yb))
    out_refs[0][...] = m.astype(BF16)
    out_refs[1][...] = ya
    out_refs[2][...] = yb
    out_refs[3][...] = ad


def _merge_bwd_epilogue(prods, extra_refs, out_refs, scratch_refs):
    dmv = prods[0]
    d = dmv.shape[1]
    ga, gs = _sigmoid(extra_refs[0][...]), _sigmoid(extra_refs[1][...])
    adv, yav = extra_refs[2][...], extra_refs[3][...]
    sb = _sigmoid(extra_refs[4][...])
    out_refs[0][:, 0:d] = (dmv * adv * ga * (1.0 - ga)).astype(BF16)
    out_refs[0][:, d:2 * d] = (dmv * (yav * sb) * gs * (1.0 - gs)).astype(BF16)
    out_refs[1][...] = (dmv * ga).astype(BF16)
    dsd = dmv * gs
    out_refs[2][...] = (dsd * sb).astype(BF16)
    out_refs[3][...] = (dsd * yav * sb * (1.0 - sb)).astype(BF16)


def _swiglu_epilogue(prods, extra_refs, out_refs, scratch_refs):
    gv, uv = prods
    out_refs[0][...] = (gv * _sigmoid(gv) * uv).astype(BF16)
    out_refs[1][...] = gv
    out_refs[2][...] = uv


def _swiglu_bwd_epilogue(prods, extra_refs, out_refs, scratch_refs):
    dav = prods[0]
    gv, uv = extra_refs[0][...], extra_refs[1][...]
    sg = _sigmoid(gv)
    out_refs[0][...] = (dav * uv * sg * (1.0 + gv * (1.0 - sg))).astype(BF16)
    out_refs[1][...] = (dav * gv * sg).astype(BF16)


def _head_epilogue(n_tiles):
    def epilogue(prods, extra_refs, out_refs, scratch_refs):
        pgv, ppv = prods
        d = pgv.shape[1]
        lacc = scratch_refs[0]
        sg = _sigmoid(pgv)
        h3 = extra_refs[0][...] + sg * ppv
        r = lax.rsqrt(jnp.mean(h3 * h3, axis=-1, keepdims=True) + EPS)
        xh = h3 * r
        gv = extra_refs[1][...]
        diff = xh * gv - extra_refs[2][...]
        dout = diff * (1.0 / d)
        dxh = dout * gv
        dh3 = r * (dxh - xh * jnp.mean(dxh * xh, axis=-1, keepdims=True))
        out_refs[2][...] = dh3
        out_refs[3][...] = (dh3 * sg).astype(BF16)
        out_refs[4][...] = (dh3 * ppv * sg * (1.0 - sg)).astype(BF16)
        _accumulate_rows(out_refs[1], jnp.sum(dout * xh, axis=0, keepdims=True))
        _accumulate_rows(lacc, jnp.sum(diff * diff, axis=0, keepdims=True))

        @pl.when(pl.program_id(0) == n_tiles - 1)
        def _():
            out_refs[0][...] = (0.5 / d) * jnp.sum(lacc[...], axis=-1, keepdims=True)

    return epilogue


def _strided(r, n, d):
    return pl.ds(r, n, stride=d) if d > 1 else pl.ds(0, n)


def _rope_tables(pos_ref, invf_ref, c_s, s1_s, s2_s, on):
    ang = pos_ref[...].astype(F32) * invf_ref[...]
    lane = lax.broadcasted_iota(jnp.int32, ang.shape, 1)
    sn = jnp.sin(ang)
    c_s[...] = jnp.where((lane < ROPE_DIM) & on, jnp.cos(ang), 1.0)
    s1_s[...] = jnp.where((lane < ROPE_HALF) & on, -sn, 0.0)
    s2_s[...] = jnp.where((lane >= ROPE_HALF) & (lane < ROPE_DIM) & on, sn, 0.0)


def _rope_dilate(z, pos, invf, tm):
    t = z.shape[0]

    def body(z_ref, pos_ref, invf_ref, o0, o1, o2, c_s, s1_s, s2_s, rot):
        _rope_tables(pos_ref, invf_ref, c_s, s1_s, s2_s, pl.program_id(1) < 2)
        cc, s1, s2 = c_s[...], s1_s[...], s2_s[...]
        for h in range(QK_W // HEAD_DIM):
            xv = z_ref[:, h * HEAD_DIM:(h + 1) * HEAD_DIM]
            rot[h] = xv * cc + pltpu.roll(xv, HEAD_DIM - ROPE_HALF, 1) * s1 + pltpu.roll(xv, ROPE_HALF, 1) * s2
        for g, (d, o_ref) in enumerate(zip(DILATIONS, (o0, o1, o2))):
            n = tm // d
            for r in range(d):
                for hh in range(HEADS_PER_GROUP):
                    oc = r * GROUP_W + hh * HEAD_DIM
                    o_ref[:, oc:oc + HEAD_DIM] = rot[g * HEADS_PER_GROUP + hh, _strided(r, n, d), :].astype(BF16)

    return pl.pallas_call(
        body, grid=(t // tm, 3),
        in_specs=[_bs((tm, QK_W), lambda i, c: (i, c)), _bs((tm, 1), lambda i, c: (i, 0)), _bs((1, HEAD_DIM), lambda i, c: (0, 0))],
        out_specs=[_bs((None, tm // d, d * GROUP_W), lambda i, c: (c, i, 0)) for d in DILATIONS],
        out_shape=[SDS((3, t // d, d * GROUP_W), BF16) for d in DILATIONS],
        scratch_shapes=[pltpu.VMEM((tm, HEAD_DIM), F32)] * 3 + [pltpu.VMEM((QK_W // HEAD_DIM, tm, HEAD_DIM), F32)],
        compiler_params=_cp(2), name="rope_dilate")(z, pos, invf)


def _band_mask(first):
    qi = lax.broadcasted_iota(jnp.int32, (BLK, 2 * BLK), 0)
    kj = lax.broadcasted_iota(jnp.int32, (BLK, 2 * BLK), 1)
    return (kj >= qi) & (kj <= qi + BLK) & ((kj >= BLK) | jnp.logical_not(first))


def _attn_fwd(qkv, d, qt):
    ell = qkv.shape[1]
    nsub = qt // BLK
    scale = 1.0 / math.sqrt(HEAD_DIM)

    def body(q_ref, kc_ref, kp_ref, vc_ref, vp_ref, o_ref, lse_ref, kcat, vcat):
        nb = pl.program_id(1)
        kcat[0:BLK, :] = kp_ref[...]
        kcat[BLK:, :] = kc_ref[...]
        vcat[0:BLK, :] = vp_ref[...]
        vcat[BLK:, :] = vc_ref[...]
        lane = lax.broadcasted_iota(jnp.int32, (BLK, HEAD_DIM), 1)
        for b in range(nsub):
            valid = _band_mask((nb == 0) if b == 0 else False)
            lse_t = jnp.zeros((BLK, HEAD_DIM), F32)
            for hh in range(HEADS_PER_GROUP):
                cs = slice(hh * HEAD_DIM, (hh + 1) * HEAD_DIM)
                qb = q_ref[b * BLK:(b + 1) * BLK, cs]
                kk = kcat[b * BLK:(b + 2) * BLK, cs]
                vv = vcat[b * BLK:(b + 2) * BLK, cs]
                s = lax.dot_general(qb, kk, _DNUMS["nt"], preferred_element_type=F32) * scale
                s = jnp.where(valid, s, NEG)
                mx = jnp.max(s, axis=-1, keepdims=True)
                p = jnp.exp(s - mx)
                den = jnp.sum(p, axis=-1, keepdims=True)
                o = jnp.dot(p.astype(BF16), vv, preferred_element_type=F32) / den
                o_ref[b * BLK:(b + 1) * BLK, cs] = o
                lse_t = jnp.where(lane == hh, mx + jnp.log(den), lse_t)
            lse_ref[b * BLK:(b + 1) * BLK, :] = lse_t

    cur = lambda c: _bs((None, qt, GROUP_W), lambda r, nb: (c, nb, r))
    prev = lambda c: _bs((None, BLK, GROUP_W), lambda r, nb: (c, jnp.maximum(nb * nsub - 1, 0), r))
    return pl.pallas_call(
        body, grid=(d, ell // qt), in_specs=[cur(0), cur(1), prev(1), cur(2), prev(2)],
        out_specs=[_bs((qt, GROUP_W), lambda r, nb: (nb, r)), _bs((None, qt, HEAD_DIM), lambda r, nb: (r, nb, 0))],
        out_shape=[SDS((ell, d * GROUP_W), F32), SDS((d, ell, HEAD_DIM), F32)],
        scratch_shapes=[pltpu.VMEM((qt + BLK, GROUP_W), BF16)] * 2, compiler_params=_cp(2), name=f"attn_fwd_d{d}")(
            qkv, qkv, qkv, qkv, qkv)


def _attn_merge(outs, lses, tm):
    t = outs[0].shape[0]

    def body(o0, o1, o2, l0, l1, l2, attn_ref, attn_bf_ref, t0, t1, t2, so, sl, lt_s):
        for g, (d, o_ref, l_ref) in enumerate(zip(DILATIONS, (o0, o1, o2), (l0, l1, l2))):
            n = tm // d
            for r in range(d):
                rows = _strided(r, n, d)
                for hh in range(HEADS_PER_GROUP):
                    oc = r * GROUP_W + hh * HEAD_DIM
                    so[g * HEADS_PER_GROUP + hh, rows, :] = o_ref[:, oc:oc + HEAD_DIM]
                sl[g, rows, :] = l_ref[r]
        ls = [sl[g] for g in range(N_GROUPS)]
        mx = jnp.maximum(jnp.maximum(ls[0], ls[1]), ls[2])
        es = [jnp.exp(l - mx) for l in ls]
        den = es[0] + es[1] + es[2]
        ws = [e / den for e in es]
        lt_s[...] = mx + jnp.log(den)
        for hh in range(HEADS_PER_GROUP):
            cs = slice(hh * HEAD_DIM, (hh + 1) * HEAD_DIM)
            a = ws[0][:, hh:hh + 1] * so[hh]
            for g in range(1, N_GROUPS):
                a = a + ws[g][:, hh:hh + 1] * so[g * HEADS_PER_GROUP + hh]
            attn_ref[:, cs] = a
            attn_bf_ref[:, cs] = a.astype(BF16)
        for d, t_ref in zip(DILATIONS, (t0, t1, t2)):
            n = tm // d
            for r in range(d):
                t_ref[r] = lt_s[_strided(r, n, d), :]

    dil = lambda d: _bs((tm // d, d * GROUP_W), lambda i: (i, 0))
    lsp = lambda d: _bs((d, tm // d, HEAD_DIM), lambda i: (0, i, 0))
    row = _bs((tm, GROUP_W), lambda i: (i, 0))
    return pl.pallas_call(
        body, grid=(t // tm,),
        in_specs=[dil(d) for d in DILATIONS] + [lsp(d) for d in DILATIONS],
        out_specs=[row, row] + [lsp(d) for d in DILATIONS],
        out_shape=[SDS((t, GROUP_W), F32), SDS((t, GROUP_W), BF16)] + [SDS(l.shape, F32) for l in lses],
        scratch_shapes=[pltpu.VMEM((N_GROUPS * HEADS_PER_GROUP, tm, HEAD_DIM), F32), pltpu.VMEM((N_GROUPS, tm, HEAD_DIM), F32),
                        pltpu.VMEM((tm, HEAD_DIM), F32)],
        compiler_params=_cp(1), name="attn_merge")(*outs, *lses)


def _attn_bwd_pre(d_attn, attn, tm):
    t = attn.shape[0]

    def body(da_ref, a_ref, g0, g1, g2, e0, e1, e2, dl_s, da_s):
        lane = lax.broadcasted_iota(jnp.int32, (tm, HEAD_DIM), 1)
        dl = jnp.zeros((tm, HEAD_DIM), F32)
        for hh in range(HEADS_PER_GROUP):
            cs = slice(hh * HEAD_DIM, (hh + 1) * HEAD_DIM)
            dav = da_ref[:, cs]
            da_s[hh] = dav
            dl = jnp.where(lane == hh, jnp.sum(dav * a_ref[:, cs], axis=-1, keepdims=True), dl)
        dl_s[...] = dl
        for d, g_ref, e_ref in zip(DILATIONS, (g0, g1, g2), (e0, e1, e2)):
            n = tm // d
            for r in range(d):
                rows = _strided(r, n, d)
                for hh in range(HEADS_PER_GROUP):
                    oc = r * GROUP_W + hh * HEAD_DIM
                    g_ref[:, oc:oc + HEAD_DIM] = da_s[hh, rows, :].astype(BF16)
                e_ref[r] = dl_s[rows, :]

    row = _bs((tm, GROUP_W), lambda i: (i, 0))
    return pl.pallas_call(
        body, grid=(t // tm,), in_specs=[row, row],
        out_specs=[_bs((tm // d, d * GROUP_W), lambda i: (i, 0)) for d in DILATIONS]
        + [_bs((d, tm // d, HEAD_DIM), lambda i: (0, i, 0)) for d in DILATIONS],
        out_shape=[SDS((t // d, d * GROUP_W), BF16) for d in DILATIONS]
        + [SDS((d, t // d, HEAD_DIM), F32) for d in DILATIONS],
        scratch_shapes=[pltpu.VMEM((tm, HEAD_DIM), F32), pltpu.VMEM((HEADS_PER_GROUP, tm, HEAD_DIM), F32)],
        compiler_params=_cp(1), name="attn_bwd_pre")(d_attn, attn)


def _attn_bwd(qkv, d_a, lt, delta, d, qt):
    ell = qkv.shape[1]
    nsub = qt // BLK
    ntile = ell // qt
    nblk = ell // BLK
    scale = 1.0 / math.sqrt(HEAD_DIM)

    def body(q_ref, qn_ref, kc_ref, kp_ref, vc_ref, vp_ref, da_ref, dan_ref, lt_ref, ltn_ref, dl_ref, dln_ref, o_ref,
             kcat, vcat, dk_acc, dv_acc):
        nb = pl.program_id(1)
        kcat[0:BLK, :] = kp_ref[...]
        kcat[BLK:, :] = kc_ref[...]
        vcat[0:BLK, :] = vp_ref[...]
        vcat[BLK:, :] = vc_ref[...]
        qi = lax.broadcasted_iota(jnp.int32, (BLK, BLK), 0)
        kj = lax.broadcasted_iota(jnp.int32, (BLK, BLK), 1)
        valid_next = (kj >= qi) & (nb < ntile - 1)
        for hh in range(HEADS_PER_GROUP):
            cs = slice(hh * HEAD_DIM, (hh + 1) * HEAD_DIM)
            dk_acc[...] = jnp.zeros_like(dk_acc)
            dv_acc[...] = jnp.zeros_like(dv_acc)
            for b in range(nsub):
                rs = slice(b * BLK, (b + 1) * BLK)
                ks = slice(b * BLK, (b + 2) * BLK)
                valid = _band_mask((nb == 0) if b == 0 else False)
                qb, kk, vv, dab = q_ref[rs, cs], kcat[ks, cs], vcat[ks, cs], da_ref[rs, cs]
                s = lax.dot_general(qb, kk, _DNUMS["nt"], preferred_element_type=F32) * scale
                p = jnp.where(valid, jnp.exp(s - lt_ref[rs, hh:hh + 1]), 0.0)
                dp = lax.dot_general(dab, vv, _DNUMS["nt"], preferred_element_type=F32)
                ds = (p * (dp - dl_ref[rs, hh:hh + 1])).astype(BF16)
                o_ref[0, rs, cs] = jnp.dot(ds, kk, preferred_element_type=F32) * scale
                dk_acc[ks, :] += lax.dot_general(ds, qb, _DNUMS["tn"], preferred_element_type=F32) * scale
                dv_acc[ks, :] += lax.dot_general(p.astype(BF16), dab, _DNUMS["tn"], preferred_element_type=F32)
            ks = slice(nsub * BLK, (nsub + 1) * BLK)
            qn, kl, vl, dan = qn_ref[:, cs], kcat[ks, cs], vcat[ks, cs], dan_ref[:, cs]
            s = lax.dot_general(qn, kl, _DNUMS["nt"], preferred_element_type=F32) * scale
            p = jnp.where(valid_next, jnp.exp(s - ltn_ref[:, hh:hh + 1]), 0.0)
            dp = lax.dot_general(dan, vl, _DNUMS["nt"], preferred_element_type=F32)
            ds = (p * (dp - dln_ref[:, hh:hh + 1])).astype(BF16)
            dk_acc[ks, :] += lax.dot_general(ds, qn, _DNUMS["tn"], preferred_element_type=F32) * scale
            dv_acc[ks, :] += lax.dot_general(p.astype(BF16), dan, _DNUMS["tn"], preferred_element_type=F32)
            o_ref[1, :, cs] = dk_acc[BLK:, :]
            o_ref[2, :, cs] = dv_acc[BLK:, :]

    nxt = lambda nb: jnp.minimum((nb + 1) * nsub, nblk - 1)
    prv = lambda nb: jnp.maximum(nb * nsub - 1, 0)
    cur3 = lambda c: _bs((None, qt, GROUP_W), lambda r, nb: (c, nb, r))
    in_specs = [
        cur3(0), _bs((None, BLK, GROUP_W), lambda r, nb: (0, nxt(nb), r)),
        cur3(1), _bs((None, BLK, GROUP_W), lambda r, nb: (1, prv(nb), r)),
        cur3(2), _bs((None, BLK, GROUP_W), lambda r, nb: (2, prv(nb), r)),
        _bs((qt, GROUP_W), lambda r, nb: (nb, r)), _bs((BLK, GROUP_W), lambda r, nb: (nxt(nb), r)),
        _bs((None, qt, HEAD_DIM), lambda r, nb: (r, nb, 0)), _bs((None, BLK, HEAD_DIM), lambda r, nb: (r, nxt(nb), 0)),
        _bs((None, qt, HEAD_DIM), lambda r, nb: (r, nb, 0)), _bs((None, BLK, HEAD_DIM), lambda r, nb: (r, nxt(nb), 0)),
    ]
    return pl.pallas_call(
        body, grid=(d, ntile), in_specs=in_specs, out_specs=_bs((3, qt, GROUP_W), lambda r, nb: (0, nb, r)),
        out_shape=SDS((3, ell, d * GROUP_W), F32),
        scratch_shapes=[pltpu.VMEM((qt + BLK, GROUP_W), BF16)] * 2 + [pltpu.VMEM((qt + BLK, HEAD_DIM), F32)] * 2,
        compiler_params=_cp(2), name=f"attn_bwd_d{d}")(qkv, qkv, qkv, qkv, qkv, qkv, d_a, d_a, lt, lt, delta, delta)


def _undilate_rope_bwd(dqkvs, pos, invf, tm):
    t = pos.shape[0]

    def body(g0, g1, g2, pos_ref, invf_ref, o_ref, c_s, s1_s, s2_s, nat):
        _rope_tables(pos_ref, invf_ref, c_s, s1_s, s2_s, pl.program_id(1) < 2)
        for g, (d, g_ref) in enumerate(zip(DILATIONS, (g0, g1, g2))):
            n = tm // d
            for r in range(d):
                for hh in range(HEADS_PER_GROUP):
                    oc = r * GROUP_W + hh * HEAD_DIM
                    nat[g * HEADS_PER_GROUP + hh, _strided(r, n, d), :] = g_ref[:, oc:oc + HEAD_DIM]
        cc, s1, s2 = c_s[...], s1_s[...], s2_s[...]
        for h in range(QK_W // HEAD_DIM):
            xv = nat[h]
            y = xv * cc - pltpu.roll(xv, HEAD_DIM - ROPE_HALF, 1) * s1 - pltpu.roll(xv, ROPE_HALF, 1) * s2
            o_ref[:, h * HEAD_DIM:(h + 1) * HEAD_DIM] = y.astype(BF16)

    return pl.pallas_call(
        body, grid=(t // tm, 3),
        in_specs=[_bs((None, tm // d, d * GROUP_W), lambda i, c: (c, i, 0)) for d in DILATIONS]
        + [_bs((tm, 1), lambda i, c: (i, 0)), _bs((1, HEAD_DIM), lambda i, c: (0, 0))],
        out_specs=_bs((tm, QK_W), lambda i, c: (i, c)), out_shape=SDS((t, 3 * QK_W), BF16),
        scratch_shapes=[pltpu.VMEM((tm, HEAD_DIM), F32)] * 3 + [pltpu.VMEM((QK_W // HEAD_DIM, tm, HEAD_DIM), F32)],
        compiler_params=_cp(2), name="undilate_rope_bwd")(*dqkvs, pos, invf)


def _cmul(ar, ai, br, bi):
    return ar * br - ai * bi, ar * bi + ai * br


def _ssm_disc(a_re, a_im, log_dt, nsq):
    def body(lr_ref, li_ref, ldt_ref, br_ref, bi_ref, zr_ref, zi_ref, pr_ref, pi_ref):
        lr, li = lr_ref[...], li_ref[...]
        dt = jnp.exp(ldt_ref[...])
        mag = jnp.exp(lr * dt)
        bar_re, bar_im = mag * jnp.cos(li * dt), mag * jnp.sin(li * dt)
        nr, ni = bar_re - 1.0, bar_im
        den = lr * lr + li * li
        br_ref[...], bi_ref[...] = bar_re, bar_im
        zr_ref[...] = (nr * lr + ni * li) / den
        zi_ref[...] = (ni * lr - nr * li) / den
        pr, pi = bar_re, bar_im
        for _ in range(nsq):
            pr, pi = _cmul(pr, pi, pr, pi)
        pr_ref[...], pi_ref[...] = pr, pi

    return pl.pallas_call(body, out_shape=[SDS(a_re.shape, F32)] * 6, name="ssm_discretise")(a_re, a_im, log_dt)


def _ssm_scale_b(z_re, z_im, b_re, b_im):
    def body(zr_ref, zi_ref, br_ref, bi_ref, or_ref, oi_ref):
        zr, zi, br, bi = zr_ref[...], zi_ref[...], br_ref[...], bi_ref[...]
        or_ref[...] = zr * br - zi * bi
        oi_ref[...] = zr * bi + zi * br

    return pl.pallas_call(body, out_shape=[SDS(b_re.shape, F32)] * 2, name="ssm_scale_b")(z_re, z_im, b_re, b_im)


def _ssm_scale_b_bwd(z_re, z_im, b_re, b_im, g_re, g_im):
    def body(zr_ref, zi_ref, br_ref, bi_ref, gr_ref, gi_ref, dbr_ref, dbi_ref, dzr_ref, dzi_ref):
        zr, zi, br, bi, gr, gi = zr_ref[...], zi_ref[...], br_ref[...], bi_ref[...], gr_ref[...], gi_ref[...]
        dbr_ref[...] = zr * gr + zi * gi
        dbi_ref[...] = zr * gi - zi * gr
        dzr_ref[...] = jnp.sum(br * gr + bi * gi, axis=-1, keepdims=True)
        dzi_ref[...] = jnp.sum(br * gi - bi * gr, axis=-1, keepdims=True)

    return pl.pallas_call(body, out_shape=[SDS(b_re.shape, F32)] * 2 + [SDS(z_re.shape, F32)] * 2,
                          name="ssm_scale_b_bwd")(z_re, z_im, b_re, b_im, g_re, g_im)


def _ssm_disc_bwd(a_re, a_im, log_dt, gb_re, gb_im, gz_re, gz_im):
    def body(lr_ref, li_ref, ldt_ref, gbr_ref, gbi_ref, gzr_ref, gzi_ref, dar_ref, dai_ref, dldt_ref):
        lr, li = lr_ref[...], li_ref[...]
        dt = jnp.exp(ldt_ref[...])
        mag = jnp.exp(lr * dt)
        bar_re, bar_im = mag * jnp.cos(li * dt), mag * jnp.sin(li * dt)
        nr, ni = bar_re - 1.0, bar_im
        den = lr * lr + li * li
        zr, zi = (nr * lr + ni * li) / den, (ni * lr - nr * li) / den
        gzr, gzi = gzr_ref[...], gzi_ref[...]
        gbr = gbr_ref[...] + (lr * gzr - li * gzi) / den
        gbi = gbi_ref[...] + (lr * gzi + li * gzr) / den
        qr, qi = (zr * lr + zi * li) / den, (zi * lr - zr * li) / den
        dar_ref[...] = dt * (bar_re * gbr + bar_im * gbi) - qr * gzr - qi * gzi
        dai_ref[...] = dt * (bar_re * gbi - bar_im * gbr) - qr * gzi + qi * gzr
        wr, wi = lr * bar_re - li * bar_im, lr * bar_im + li * bar_re
        dldt_ref[...] = dt * jnp.sum(wr * gbr + wi * gbi, axis=-1, keepdims=True)

    return pl.pallas_call(body, out_shape=[SDS(a_re.shape, F32)] * 2 + [SDS(log_dt.shape, F32)],
                          name="ssm_discretise_bwd")(a_re, a_im, log_dt, gb_re, gb_im, gz_re, gz_im)


def _permute_u(z, ucol_block, tm):
    t = z.shape[0]
    seg = t // N_DEV
    z3 = z.reshape(N_DEV, seg, z.shape[1])

    def body(z_ref, u_ref, ub_ref, tmp):
        for n in range(SSM_W // BLK):
            for j in range(N_DEV):
                tmp[n, pl.ds(j, tm // N_DEV, stride=N_DEV), :] = z_ref[j, :, n * BLK:(n + 1) * BLK]
            u_ref[:, n * BLK:(n + 1) * BLK] = tmp[n]
            ub_ref[:, n * BLK:(n + 1) * BLK] = tmp[n].astype(BF16)

    row = _bs((tm, SSM_W), lambda i: (i, 0))
    return pl.pallas_call(
        body, grid=(t // tm,), in_specs=[_bs((N_DEV, tm // N_DEV, SSM_W), lambda i: (0, i, ucol_block))],
        out_specs=[row, row], out_shape=[SDS((t, SSM_W), F32), SDS((t, SSM_W), BF16)],
        scratch_shapes=[pltpu.VMEM((SSM_W // BLK, tm, BLK), F32)], compiler_params=_cp(1), name="permute_u")(z3)


def _drive(src_ref, mat_ref, dst, mode):
    for kn in range(2 * SSM_NB):
        n = kn % SSM_NB
        a = src_ref[:, n * BLK:(n + 1) * BLK]
        dst[:, kn * 512:(kn + 1) * 512] = lax.dot_general(a, mat_ref[kn], _DNUMS[mode], preferred_element_type=F32)


def _scan_chunk(src, lam_ref, carry, *, reverse, store=None, h_ref=None, acc=None):
    steps = src.shape[0] // 8
    for c in range(NSTATE // SCAN_LANES):
        re = slice(c * SCAN_LANES, (c + 1) * SCAN_LANES)
        im = slice(NSTATE + c * SCAN_LANES, NSTATE + (c + 1) * SCAN_LANES)
        ar, ai = lam_ref[:, re], lam_ref[:, im]

        def step(s, val):
            i = (steps - 1 - s) if reverse else s
            rows = pl.ds(pl.multiple_of(i * 8, 8), 8)
            if acc is not None:
                hr, hi, dr, di = val
                pr, pi = h_ref[rows, re], h_ref[rows, im]
                dr = dr + hr * pr + hi * pi
                di = di + hi * pr - hr * pi
            else:
                hr, hi = val
            nr = ar * hr - ai * hi + src[rows, re]
            ni = ar * hi + ai * hr + src[rows, im]
            if store is not None:
                store[rows, re] = nr
                store[rows, im] = ni
            return (nr, ni, dr, di) if acc is not None else (nr, ni)

        init = (carry[:, re], carry[:, im])
        if acc is not None:
            init = init + (acc[:, re], acc[:, im])
        out = lax.fori_loop(0, steps, step, init, unroll=4)
        carry[:, re], carry[:, im] = out[0], out[1]
        if acc is not None:
            acc[:, re], acc[:, im] = out[2], out[3]


def _segment_carries(e_ref, pw_ref, out_ref, reverse):
    pr, pi = pw_ref[:, 0:NSTATE], pw_ref[:, NSTATE:]
    hr = jnp.zeros((1, NSTATE), F32)
    hi = jnp.zeros((1, NSTATE), F32)
    order = range(N_DEV - 1, -1, -1) if reverse else range(N_DEV)
    for j in order:
        out_ref[j:j + 1, 0:NSTATE] = hr
        out_ref[j:j + 1, NSTATE:] = hi
        tr, ti = _cmul(pr, pi, hr, hi)
        hr, hi = e_ref[j:j + 1, 0:NSTATE] + tr, e_ref[j:j + 1, NSTATE:] + ti


def _ssm_carries(name, src, mat, mode, lam8, pw, reverse):
    t = src.shape[0]
    nchunk = t // SCAN_ROWS

    def body(src_ref, mat_ref, lam_ref, pw_ref, out_ref, drive, carry):
        c = pl.program_id(0)

        @pl.when(c == 0)
        def _():
            carry[...] = jnp.zeros_like(carry)

        _drive(src_ref, mat_ref, drive, mode)
        _scan_chunk(drive, lam_ref, carry, reverse=reverse)

        @pl.when(c == nchunk - 1)
        def _():
            _segment_carries(carry, pw_ref, out_ref, reverse)

    blk = (lambda c: (nchunk - 1 - c, 0)) if reverse else (lambda c: (c, 0))
    return pl.pallas_call(
        body, grid=(nchunk,),
        in_specs=[_bs((SCAN_ROWS, SSM_W), blk), _bs(mat.shape, lambda c: (0, 0, 0)), _bs((8, 2 * NSTATE), lambda c: (0, 0)),
                  _bs((1, 2 * NSTATE), lambda c: (0, 0))],
        out_specs=_bs((8, 2 * NSTATE), lambda c: (0, 0)), out_shape=SDS((8, 2 * NSTATE), F32),
        scratch_shapes=[pltpu.VMEM((SCAN_ROWS, 2 * NSTATE), F32), pltpu.VMEM((8, 2 * NSTATE), F32)],
        compiler_params=_cp(1), name=name)(src, mat, lam8, pw)


def _ssm_fwd(u_bf, bd, cd, lam8, start):
    t = u_bf.shape[0]
    nchunk = t // SCAN_ROWS

    def body(u_ref, bd_ref, cd_ref, lam_ref, start_ref, h_ref, y_ref, drive, carry):
        @pl.when(pl.program_id(0) == 0)
        def _():
            carry[...] = start_ref[...]

        _drive(u_ref, bd_ref, drive, "nn")
        _scan_chunk(drive, lam_ref, carry, reverse=False, store=h_ref)
        for n in range(SSM_NB):
            hr = h_ref[:, n * 512:(n + 1) * 512].astype(BF16)
            hi = h_ref[:, NSTATE + n * 512:NSTATE + (n + 1) * 512].astype(BF16)
            y_ref[:, n * BLK:(n + 1) * BLK] = (jnp.dot(hr, cd_ref[n], preferred_element_type=F32)
                                              + jnp.dot(hi, cd_ref[SSM_NB + n], preferred_element_type=F32))

    return pl.pallas_call(
        body, grid=(nchunk,),
        in_specs=[_bs((SCAN_ROWS, SSM_W), lambda c: (c, 0)), _bs(bd.shape, lambda c: (0, 0, 0)), _bs(cd.shape, lambda c: (0, 0, 0)),
                  _bs((8, 2 * NSTATE), lambda c: (0, 0)), _bs((8, 2 * NSTATE), lambda c: (0, 0))],
        out_specs=[_bs((SCAN_ROWS, 2 * NSTATE), lambda c: (c, 0)), _bs((SCAN_ROWS, SSM_W), lambda c: (c, 0))],
        out_shape=[SDS((t, 2 * NSTATE), F32), SDS((t, SSM_W), F32)],
        scratch_shapes=[pltpu.VMEM((SCAN_ROWS, 2 * NSTATE), F32), pltpu.VMEM((8, 2 * NSTATE), F32)],
        compiler_params=_cp(1), name="ssm_scan_fwd")(u_bf, bd, cd, lam8, start)


def _ssm_bwd(dys_bf, u_bf, h, bd, cd, lamc8, start):
    t = u_bf.shape[0]
    nchunk = t // SCAN_ROWS

    def body(dys_ref, u_ref, h_ref, bd_ref, cd_ref, lam_ref, start_ref, du_ref, dlam_ref, dbd_ref, dcd_ref, drive, adj, carry):
        c = pl.program_id(0)

        @pl.when(c == 0)
        def _():
            carry[...] = start_ref[...]
            dlam_ref[...] = jnp.zeros_like(dlam_ref)
            dbd_ref[...] = jnp.zeros_like(dbd_ref)
            dcd_ref[...] = jnp.zeros_like(dcd_ref)

        _drive(dys_ref, cd_ref, drive, "nt")
        _scan_chunk(drive, lam_ref, carry, reverse=True, store=adj, h_ref=h_ref, acc=dlam_ref)
        for n in range(SSM_NB):
            cs = slice(n * BLK, (n + 1) * BLK)
            acc = None
            for k in range(2):
                kn = k * SSM_NB + n
                ss = slice(kn * 512, (kn + 1) * 512)
                lam_b = adj[:, ss].astype(BF16)
                part = lax.dot_general(lam_b, bd_ref[kn], _DNUMS["nt"], preferred_element_type=F32)
                acc = part if acc is None else acc + part
                dbd_ref[kn] += lax.dot_general(u_ref[:, cs], lam_b, _DNUMS["tn"], preferred_element_type=F32)
                dcd_ref[kn] += lax.dot_general(h_ref[:, ss].astype(BF16), dys_ref[:, cs], _DNUMS["tn"],
                                               preferred_element_type=F32)
            du_ref[:, cs] = acc

    rev = lambda c: (nchunk - 1 - c, 0)
    const2 = lambda c: (0, 0)
    const3 = lambda c: (0, 0, 0)
    return pl.pallas_call(
        body, grid=(nchunk,),
        in_specs=[_bs((SCAN_ROWS, SSM_W), rev), _bs((SCAN_ROWS, SSM_W), rev), _bs((SCAN_ROWS, 2 * NSTATE), rev),
                  _bs(bd.shape, const3), _bs(cd.shape, const3), _bs((8, 2 * NSTATE), const2), _bs((8, 2 * NSTATE), const2)],
        out_specs=[_bs((SCAN_ROWS, SSM_W), rev), _bs((8, 2 * NSTATE), const2), _bs(bd.shape, const3), _bs(cd.shape, const3)],
        out_shape=[SDS((t, SSM_W), F32), SDS((8, 2 * NSTATE), F32), SDS(bd.shape, F32), SDS(cd.shape, F32)],
        scratch_shapes=[pltpu.VMEM((SCAN_ROWS, 2 * NSTATE), F32), pltpu.VMEM((SCAN_ROWS, 2 * NSTATE), F32),
                        pltpu.VMEM((8, 2 * NSTATE), F32)],
        compiler_params=_cp(1), name="ssm_scan_bwd")(dys_bf, u_bf, h, bd, cd, lamc8, start)


def _gelu_parts(x):
    c0 = math.sqrt(2.0 / math.pi)
    inner = c0 * (x + 0.044715 * x * x * x)
    th = jnp.tanh(inner)
    val = 0.5 * x * (1.0 + th)
    grad = 0.5 * (1.0 + th) + 0.5 * x * (1.0 - th * th) * c0 * (1.0 + 3.0 * 0.044715 * x * x)
    return val, grad


def _ssm_out(y_raw, u, d_skip, tm):
    t = u.shape[0]
    seg = t // N_DEV

    def body(y_ref, u_ref, d_ref, ys_ref, yg_ref, tmp):
        ys = y_ref[...] + d_ref[...] * u_ref[...]
        ys_ref[...] = ys
        yg = _gelu_parts(ys)[0]
        for n in range(SSM_W // BLK):
            tmp[n] = yg[:, n * BLK:(n + 1) * BLK]
            for j in range(N_DEV):
                yg_ref[j, :, n * BLK:(n + 1) * BLK] = tmp[n, pl.ds(j, tm // N_DEV, stride=N_DEV), :].astype(BF16)

    row = _bs((tm, SSM_W), lambda i: (i, 0))
    ys, yg = pl.pallas_call(
        body, grid=(t // tm,), in_specs=[row, row, _bs((1, SSM_W), lambda i: (0, 0))],
        out_specs=[row, _bs((N_DEV, tm // N_DEV, SSM_W), lambda i: (0, i, 0))],
        out_shape=[SDS((t, SSM_W), F32), SDS((N_DEV, seg, SSM_W), BF16)],
        scratch_shapes=[pltpu.VMEM((SSM_W // BLK, tm, BLK), F32)], compiler_params=_cp(1), name="ssm_out")(y_raw, u, d_skip)
    return ys, yg.reshape(t, SSM_W)


def _ssm_out_bwd(d_yg, ys, u, tm):
    t = u.shape[0]
    seg = t // N_DEV

    def body(dg_ref, ys_ref, u_ref, dys_ref, dysb_ref, dd_ref, tmp):
        for n in range(SSM_W // BLK):
            for j in range(N_DEV):
                tmp[n, pl.ds(j, tm // N_DEV, stride=N_DEV), :] = dg_ref[j, :, n * BLK:(n + 1) * BLK]
        dyg = jnp.concatenate([tmp[n] for n in range(SSM_W // BLK)], axis=1)
        dys = dyg * _gelu_parts(ys_ref[...])[1]
        dys_ref[...] = dys
        dysb_ref[...] = dys.astype(BF16)
        part = jnp.sum(dys * u_ref[...], axis=0, keepdims=True)

        @pl.when(pl.program_id(0) == 0)
        def _():
            dd_ref[...] = part

        @pl.when(pl.program_id(0) > 0)
        def _():
            dd_ref[...] += part

    row = _bs((tm, SSM_W), lambda i: (i, 0))
    return pl.pallas_call(
        body, grid=(t // tm,), in_specs=[_bs((N_DEV, tm // N_DEV, SSM_W), lambda i: (0, i, 0)), row, row],
        out_specs=[row, row, _bs((1, SSM_W), lambda i: (0, 0))],
        out_shape=[SDS((t, SSM_W), F32), SDS((t, SSM_W), BF16), SDS((1, SSM_W), F32)],
        scratch_shapes=[pltpu.VMEM((SSM_W // BLK, tm, BLK), F32)], compiler_params=_cp(1), name="ssm_out_bwd")(
            d_yg.reshape(N_DEV, seg, SSM_W), ys, u)


def _du_to_dz(du_raw, dys, d_skip, tm):
    t = du_raw.shape[0]
    seg = t // N_DEV

    def body(du_ref, dys_ref, d_ref, o_ref, tmp):
        du = du_ref[...] + d_ref[...] * dys_ref[...]
        for n in range(SSM_W // BLK):
            tmp[n] = du[:, n * BLK:(n + 1) * BLK]
            for j in range(N_DEV):
                o_ref[j, :, n * BLK:(n + 1) * BLK] = tmp[n, pl.ds(j, tm // N_DEV, stride=N_DEV), :].astype(BF16)

    row = _bs((tm, SSM_W), lambda i: (i, 0))
    out = pl.pallas_call(
        body, grid=(t // tm,), in_specs=[row, row, _bs((1, SSM_W), lambda i: (0, 0))],
        out_specs=_bs((N_DEV, tm // N_DEV, SSM_W), lambda i: (0, i, 0)), out_shape=SDS((N_DEV, seg, SSM_W), BF16),
        scratch_shapes=[pltpu.VMEM((SSM_W // BLK, tm, BLK), F32)], compiler_params=_cp(1), name="du_to_dz")(du_raw, dys, d_skip)
    return out.reshape(t, SSM_W)


def _block_diag(blocks):
    nb, ng, r, c = blocks.shape
    eye = jnp.eye(ng, dtype=blocks.dtype)
    return (blocks[:, :, :, None, :] * eye[None, :, None, :, None]).reshape(nb, ng * r, ng * c)


def _diag_blocks(full, r, c):
    k, nb = full.shape[:2]
    ng = full.shape[2] // r
    x = full.reshape(k, nb, ng, r, ng, c)
    eye = jnp.eye(ng, dtype=full.dtype)
    return jnp.sum(x * eye[None, None, :, None, :, None], axis=4).reshape(k, nb * ng, r, c)


_SMALL = ("g_mix", "a_re", "a_im", "log_dt", "b_re", "b_im", "c_re", "c_im", "d_skip", "g_ffn", "g_final")


def _pack_small(arrs):
    flat = jnp.concatenate([a.reshape(-1) for a in arrs])
    pad = (-flat.shape[0]) % (8 * 128)
    return jnp.pad(flat, (0, pad)).reshape(-1, 128)


def _unpack_small(packed, shapes):
    flat = packed.reshape(-1)
    out, off = [], 0
    for s in shapes:
        n = math.prod(s)
        out.append(flat[off:off + n].reshape(s))
        off += n
    return out


def kernel(x, p, positions, g_mix, w_in, a_re, a_im, log_dt, b_re, b_im, c_re, c_im, d_skip, w_attn_proj, w_glu_a, w_glu_b, w_out, g_ffn, w_ffn_gate, w_ffn_up, w_ffn_down, w_ple_gate, w_ple_proj, g_final, loss_target, m_g_mix, m_w_in, m_a_re, m_a_im, m_log_dt, m_b_re, m_b_im, m_c_re, m_c_im, m_d_skip, m_w_attn_proj, m_w_glu_a, m_w_glu_b, m_w_out, m_g_ffn, m_w_ffn_gate, m_w_ffn_up, m_w_ffn_down, m_w_ple_gate, m_w_ple_proj, m_g_final, v_g_mix, v_w_in, v_a_re, v_a_im, v_log_dt, v_b_re, v_b_im, v_c_re, v_c_im, v_d_skip, v_w_attn_proj, v_w_glu_a, v_w_glu_b, v_w_out, v_g_ffn, v_w_ffn_gate, v_w_ffn_up, v_w_ffn_down, v_w_ple_gate, v_w_ple_proj, v_g_final):
    args = dict(locals())
    t, d = x.shape[1], x.shape[2]
    inw = w_in.shape[2] * N_DEV
    fs = w_ffn_gate.shape[2]
    ff = fs * N_DEV
    ple = w_ple_proj.shape[1]
    seg = t // N_DEV
    assert inw == 3 * QK_W + SSM_W + 2 * d and t % (N_DEV * SCAN_ROWS // 8) == 0 and seg & (seg - 1) == 0
    tm = min(1024, t)
    te = min(512, t)
    tk = min(512, t)
    ucol = (3 * QK_W) // SSM_W
    gcol = (3 * QK_W + SSM_W) // d
    assert (3 * QK_W + SSM_W) % d == 0

    x2, p2, tgt = x[0], p[0, 0], loss_target[0]
    pos = positions.reshape(t, 1)
    inv = ROPE_THETA ** (-jnp.arange(ROPE_HALF, dtype=F32) * 2.0 / ROPE_DIM)
    invf = jnp.concatenate([inv, inv, jnp.zeros((HEAD_DIM - ROPE_DIM,), F32)]).reshape(1, HEAD_DIM)

    wnames = ("w_in", "w_attn_proj", "w_glu_a", "w_glu_b", "w_out", "w_ffn_gate", "w_ffn_up", "w_ffn_down", "w_ple_gate",
              "w_ple_proj")
    kinds = ("cols", "cols", "cols", "cols", "rows", "slot", "slot", "rows", "rows", "cols")
    shards = [args[n][0].astype(BF16) for n in wnames]
    sizes = [s.shape[0] if k == "rows" else s.shape[-1] for s, k in zip(shards, kinds)]
    ag = _exchange_start("gather_weights_start", shards, kinds, sizes, True)

    row_d = _bs((tm, d), lambda i, j, k: (i, 0))
    row_e = _bs((te, d), lambda i, j, k: (i, 0))
    vec_d = _bs((1, d), lambda i, j, k: (0, 0))
    sq_w = _bs((d, d), lambda i, j, k: (0, 0))
    n1 = _rms_fwd("norm_mix", x2, g_mix + ag[3][0:1, 0:1], tm)
    W_in, = _exchange_wait("gather_w_in_wait", ag, [0], kinds, sizes, True, n1)
    tn_in = _pick(inw, (1024, 512, 256, 128))
    z, = _mm("z_proj", (t // tm, inw // tn_in, 1),
             [("nn", n1, row_d, W_in, _bs((d, tn_in), lambda i, j, k: (0, j)))],
             [(SDS((t, inw), F32), _bs((tm, tn_in), lambda i, j, k: (i, j)))])

    qkv = _rope_dilate(z, pos, invf, tm)
    outs, lses = [], []
    for g, dil in enumerate(DILATIONS):
        o_g, l_g = _attn_fwd(qkv[g], dil, min(512, t // dil))
        outs.append(o_g)
        lses.append(l_g)
    merged = _attn_merge(outs, lses, te)
    attn, attn_bf, lts = merged[0], merged[1], merged[2:]

    nsq = seg.bit_length() - 1
    bar_re, bar_im, z_re, z_im, pw_re, pw_im = _ssm_disc(a_re[0], a_im[0], log_dt.reshape(SSM_GROUPS, 1), nsq)
    gp = SSM_GROUPS * SSM_STATE
    b_re2, b_im2 = b_re.reshape(gp, SSM_GROUP), b_im.reshape(gp, SSM_GROUP)
    bb_re, bb_im = _ssm_scale_b(z_re.reshape(gp, 1), z_im.reshape(gp, 1), b_re2, b_im2)

    def chunks(a, r, c):
        return a.reshape(SSM_NB, SSM_GROUPS // SSM_NB, r, c)

    bbt = lambda a: jnp.swapaxes(a.reshape(SSM_GROUPS, SSM_STATE, SSM_GROUP), 1, 2)
    bd = jnp.concatenate([_block_diag(chunks(bbt(bb_re), SSM_GROUP, SSM_STATE)),
                          _block_diag(chunks(bbt(bb_im), SSM_GROUP, SSM_STATE))]).astype(BF16)
    ct = lambda a: jnp.swapaxes(a[0], 1, 2)
    cd = jnp.concatenate([_block_diag(chunks(ct(c_re), SSM_STATE, SSM_GROUP)),
                          _block_diag(chunks(-ct(c_im), SSM_STATE, SSM_GROUP))]).astype(BF16)
    lam = jnp.concatenate([bar_re.reshape(1, gp), bar_im.reshape(1, gp)], axis=1)
    lamc = jnp.concatenate([bar_re.reshape(1, gp), -bar_im.reshape(1, gp)], axis=1)
    pw = jnp.concatenate([pw_re.reshape(1, gp), pw_im.reshape(1, gp)], axis=1)
    pwc = jnp.concatenate([pw_re.reshape(1, gp), -pw_im.reshape(1, gp)], axis=1)
    lam8, lamc8 = jnp.broadcast_to(lam, (8, 2 * gp)), jnp.broadcast_to(lamc, (8, 2 * gp))

    u_perm, u_bf = _permute_u(z, ucol, te)
    start_f = _ssm_carries("ssm_carries_fwd", u_bf, bd, "nn", lam8, pw, False)
    h_all, y_raw = _ssm_fwd(u_bf, bd, cd, lam8, start_f)
    dsk = d_skip.reshape(1, SSM_W)
    ys, yg_bf = _ssm_out(y_raw, u_perm, dsk, te)
    W_ap, W_ga, W_gb, W_out, W_fg, W_fu, W_fd, W_pg, W_pp = _exchange_wait(
        "gather_rest_wait", ag, list(range(1, len(wnames))), kinds, sizes, True, yg_bf)
    W_fg = jnp.swapaxes(W_fg, 0, 1).reshape(d, ff)
    W_fu = jnp.swapaxes(W_fu, 0, 1).reshape(d, ff)

    glu_w = _bs((SSM_W, d), lambda i, j, k: (0, 0))
    row_es = _bs((te, SSM_W), lambda i, j, k: (i, 0))
    gate_a = _bs((te, d), lambda i, j, k: (i, gcol))
    gate_s = _bs((te, d), lambda i, j, k: (i, gcol + 1))
    td_f32, td_bf = SDS((t, d), F32), SDS((t, d), BF16)
    m_bf, ya, yb, attn_d = _mm(
        "glu_merge", (t // te, 1, 1),
        [("nn", yg_bf, row_es, W_ga, glu_w), ("nn", yg_bf, row_es, W_gb, glu_w), ("nn", attn_bf, row_es, W_ap, glu_w)],
        [(td_bf, row_e), (td_f32, row_e), (td_f32, row_e), (td_f32, row_e)],
        extras=[(z, gate_a), (z, gate_s)], epilogue=_glu_merge_epilogue)

    h1, n2 = _mm("out_proj", (t // tm, 1, 1), [("nn", m_bf, row_d, W_out, sq_w)], [(td_f32, row_d), (td_bf, row_d)],
                 extras=[(x2, row_d), (g_ffn, vec_d)], epilogue=_out_norm_epilogue)

    tn_f = ff // 2
    nf = ff // tn_f
    hid_o = _bs((te, tn_f), lambda j, i, k: (i, j))
    tf_f32, tf_bf = SDS((t, ff), F32), SDS((t, ff), BF16)
    a_rows = _bs((te, d), lambda j, i, k: (i, 0))
    w_cols = _bs((d, tn_f), lambda j, i, k: (0, j))
    act, fg, fu = _mm("ffn_gate_up", (nf, t // te, 1), [("nn", n2, a_rows, W_fg, w_cols), ("nn", n2, a_rows, W_fu, w_cols)],
                      [(tf_bf, hid_o), (tf_f32, hid_o), (tf_f32, hid_o)], epilogue=_swiglu_epilogue)
    h2, h2_bf = _mm("ffn_down", (t // tm, 1, nf),
                    [("nn", act, _bs((tm, tn_f), lambda i, j, k: (i, k)), W_fd, _bs((tn_f, d), lambda i, j, k: (k, 0)))],
                    [(td_f32, row_d), (td_bf, row_d)], extras=[(h1, row_d)])

    loss_part, dg_final, dh3, dpp_bf, dpg_bf = _mm(
        "ple_head", (t // te, 1, 1),
        [("nn", h2_bf, row_e, W_pg, sq_w), ("nn", p2, _bs((te, ple), lambda i, j, k: (i, 0)), W_pp, _bs((ple, d), lambda i, j, k: (0, 0)))],
        [(SDS((1, 1), F32), _bs((1, 1), lambda i, j, k: (0, 0))), (SDS((1, d), F32), vec_d), (td_f32, row_e), (td_bf, row_e),
         (td_bf, row_e)],
        extras=[(h2, row_e), (g_final.reshape(1, d), vec_d), (tgt, row_e)], epilogue=_head_epilogue(t // te),
        scratch=[pltpu.VMEM((1, d), F32)])
    loss = lax.psum(loss_part[0, 0], ("x", "y", "c"))

    nkt = t // tk
    tok_a = lambda w: _bs((tk, w), lambda i, j, k: (k, 0))

    def wgrad(name, a, wa, b, wb):
        return _mm(name, (1, 1, nkt), [("tn", a, tok_a(wa), b, tok_a(wb))],
                   [(SDS((wa, wb), BF16), _bs((wa, wb), lambda i, j, k: (0, 0)))])[0]

    dW_pp = wgrad("dw_ple_proj", p2, ple, dpp_bf, d)
    dW_pg = wgrad("dw_ple_gate", h2_bf, d, dpg_bf, d)
    dh2, dh2_bf = _mm("d_ple_gate", (t // tm, 1, 1), [("nt", dpg_bf, row_d, W_pg, sq_w)], [(td_f32, row_d), (td_bf, row_d)],
                      extras=[(dh3, row_d)])

    dfg_bf, dfu_bf = _mm("d_ffn_down", (nf, t // te, 1),
                         [("nt", dh2_bf, a_rows, W_fd, _bs((tn_f, d), lambda j, i, k: (j, 0)))],
                         [(tf_bf, hid_o), (tf_bf, hid_o)], extras=[(fg, hid_o), (fu, hid_o)], epilogue=_swiglu_bwd_epilogue)
    dW_fd, = _mm("dw_ffn_down", (nf, 1, nkt), [("tn", act, _bs((tk, tn_f), lambda i, j, k: (k, i)), dh2_bf, tok_a(d))],
                 [(SDS((ff, d), BF16), _bs((tn_f, d), lambda i, j, k: (i, 0)))])
    hid_t = _bs((tk, tn_f), lambda i, j, k: (k, j))
    wg_o = [(SDS((d, ff), BF16), _bs((d, tn_f), lambda i, j, k: (0, j)))]
    dW_fg, = _mm("dw_ffn_gate", (1, nf, nkt), [("tn", n2, tok_a(d), dfg_bf, hid_t)], wg_o)
    dW_fu, = _mm("dw_ffn_up", (1, nf, nkt), [("tn", n2, tok_a(d), dfu_bf, hid_t)], wg_o)
    dW_fg = jnp.swapaxes(dW_fg.reshape(d, N_DEV, fs), 0, 1)
    dW_fu = jnp.swapaxes(dW_fu.reshape(d, N_DEV, fs), 0, 1)
    group = lambda names: ([kinds[wnames.index(n)] for n in names], [sizes[wnames.index(n)] for n in names])
    ffn_names = ("w_ffn_gate", "w_ffn_up", "w_ffn_down", "w_ple_gate", "w_ple_proj")
    rs_ffn = _exchange_start("scatter_ffn_start", [dW_fg, dW_fu, dW_fd, dW_pg, dW_pp], *group(ffn_names), False)
    hid_k = _bs((te, tn_f), lambda i, j, k: (i, k))
    w_k = _bs((d, tn_f), lambda i, j, k: (0, k))
    dh1, dh1_bf, dg_ffn = _mm("d_ffn_gate_up", (t // te, 1, nf), [("nt", dfg_bf, hid_k, W_fg, w_k), ("nt", dfu_bf, hid_k, W_fu, w_k)],
                              [(td_f32, row_e), (td_bf, row_e), (SDS((1, d), F32), vec_d)],
                              extras=[(h1, row_e), (g_ffn, vec_d), (dh2, row_e)], epilogue=_rms_bwd_epilogue, after=rs_ffn[3])

    dW_out = wgrad("dw_out", m_bf, d, dh1_bf, d)
    dz_g, dad_bf, dya_bf, dyb_bf = _mm(
        "d_out_proj", (t // te, 1, 1), [("nt", dh1_bf, row_e, W_out, sq_w)],
        [(SDS((t, 2 * d), BF16), _bs((te, 2 * d), lambda i, j, k: (i, 0))), (td_bf, row_e), (td_bf, row_e), (td_bf, row_e)],
        extras=[(z, gate_a), (z, gate_s), (attn_d, row_e), (ya, row_e), (yb, row_e)], epilogue=_merge_bwd_epilogue)

    row_s = _bs((tm, SSM_W), lambda i, j, k: (i, 0))
    d_yg, = _mm("d_glu", (t // tm, 1, 1), [("nt", dya_bf, row_d, W_ga, glu_w), ("nt", dyb_bf, row_d, W_gb, glu_w)],
                [(SDS((t, SSM_W), F32), row_s)])
    dW_ga = wgrad("dw_glu_a", yg_bf, SSM_W, dya_bf, d)
    dW_gb = wgrad("dw_glu_b", yg_bf, SSM_W, dyb_bf, d)
    dys, dys_bf, dd_skip = _ssm_out_bwd(d_yg, ys, u_perm, te)
    start_b = _ssm_carries("ssm_carries_bwd", dys_bf, cd, "nt", lamc8, pwc, True)
    du_raw, dlam8, dbd, dcd = _ssm_bwd(dys_bf, u_bf, h_all, bd, cd, lamc8, start_b)
    dz_u = _du_to_dz(du_raw, dys, dsk, te)
    dlam = jnp.sum(dlam8, axis=0)
    dbb = _diag_blocks(dbd.reshape(2, SSM_NB, BLK, 512), SSM_GROUP, SSM_STATE)
    dbb_re = jnp.swapaxes(dbb[0], 1, 2).reshape(gp, SSM_GROUP)
    dbb_im = jnp.swapaxes(dbb[1], 1, 2).reshape(gp, SSM_GROUP)
    dcc = _diag_blocks(dcd.reshape(2, SSM_NB, 512, BLK), SSM_STATE, SSM_GROUP)
    dc_re, dc_im = jnp.swapaxes(dcc[0], 1, 2), -jnp.swapaxes(dcc[1], 1, 2)
    db_re, db_im, dz_re, dz_im = _ssm_scale_b_bwd(z_re.reshape(gp, 1), z_im.reshape(gp, 1), b_re2, b_im2, dbb_re, dbb_im)
    gshape = (SSM_GROUPS, SSM_STATE)
    da_re, da_im, dlog_dt = _ssm_disc_bwd(a_re[0], a_im[0], log_dt.reshape(SSM_GROUPS, 1), dlam[:gp].reshape(gshape),
                                          dlam[gp:].reshape(gshape), dz_re.reshape(gshape), dz_im.reshape(gshape))

    d_attn, = _mm("d_attn_proj", (t // tm, 1, 1), [("nt", dad_bf, row_d, W_ap, glu_w)], [(SDS((t, GROUP_W), F32), row_s)])
    dW_ap = wgrad("dw_attn_proj", attn_bf, GROUP_W, dad_bf, d)
    pre = _attn_bwd_pre(d_attn, attn, te)
    das, deltas = pre[:N_GROUPS], pre[N_GROUPS:]
    dqkvs = [_attn_bwd(qkv[g], das[g], lts[g], deltas[g], dil, min(512, t // dil)) for g, dil in enumerate(DILATIONS)]
    dz_qkv = _undilate_rope_bwd(dqkvs, pos, invf, tm)

    dW_in, = _mm("dw_in_qkv", (1, 3, nkt), [("tn", n1, tok_a(d), dz_qkv, _bs((tk, QK_W), lambda i, j, k: (k, j)))],
                 [(SDS((d, inw), BF16), _bs((d, QK_W), lambda i, j, k: (0, j)))])
    dW_in, = _mm("dw_in_u", (1, 1, nkt), [("tn", n1, tok_a(d), dz_u, tok_a(SSM_W))],
                 [(SDS((d, inw), BF16), _bs((d, SSM_W), lambda i, j, k: (0, ucol)))], alias_to_out0=dW_in)
    dW_in, = _mm("dw_in_gates", (1, 2, nkt), [("tn", n1, tok_a(d), dz_g, _bs((tk, d), lambda i, j, k: (k, j)))],
                 [(SDS((d, inw), BF16), _bs((d, d), lambda i, j, k: (0, gcol + j)))], alias_to_out0=dW_in)
    rest_names = ("w_in", "w_attn_proj", "w_glu_a", "w_glu_b", "w_out")
    rs_in = _exchange_start("scatter_rest_start", [dW_in, dW_ap, dW_ga, dW_gb, dW_out], *group(rest_names), False)
    dx, dg_mix = _mm(
        "d_z_proj", (t // te, 1, 3),
        [("nt", dz_qkv, _bs((te, QK_W), lambda i, j, k: (i, k)), W_in, _bs((d, QK_W), lambda i, j, k: (0, k)), 3),
         ("nt", dz_u, _bs((te, SSM_W), lambda i, j, k: (i, 0)), W_in, _bs((d, SSM_W), lambda i, j, k: (0, ucol)), 1),
         ("nt", dz_g, _bs((te, d), lambda i, j, k: (i, jnp.minimum(k, 1))), W_in,
          _bs((d, d), lambda i, j, k: (0, gcol + jnp.minimum(k, 1))), 2)],
        [(td_f32, row_e), (SDS((1, d), F32), vec_d)],
        extras=[(x2, row_e), (g_mix, vec_d), (dh1, row_e)], epilogue=_rms_bwd_epilogue, after=rs_in[3])

    small_parts = dict(g_mix=dg_mix, a_re=da_re, a_im=da_im, log_dt=dlog_dt, b_re=db_re, b_im=db_im, c_re=dc_re, c_im=dc_im,
                       d_skip=dd_skip, g_ffn=dg_ffn, g_final=dg_final)
    small = _pack_small([small_parts[n] for n in _SMALL])
    received = {}
    for names, started, label in ((ffn_names, rs_ffn, "ffn"), (rest_names, rs_in, "rest")):
        landed = _exchange_wait(f"scatter_{label}_wait", started, list(range(len(names))), *group(names), False, dx)
        received.update(zip(names, landed))

    new = {}
    for n in wnames:
        new[n] = [o.reshape(args[n].shape)
                  for o in _adamw("adamw_" + n, received[n], args[n][0], args["m_" + n][0], args["v_" + n][0])]
    pk = lambda pre: _pack_small([args[pre + n] for n in _SMALL])
    sm = _adamw("adamw_small", _gather_small(small), pk(""), pk("m_"), pk("v_"))
    shapes = [args[n].shape for n in _SMALL]
    for n, vals in zip(_SMALL, zip(*[_unpack_small(o, shapes) for o in sm])):
        new[n] = list(vals)

    order = ("g_mix", "w_in", "a_re", "a_im", "log_dt", "b_re", "b_im", "c_re", "c_im", "d_skip", "w_attn_proj", "w_glu_a",
             "w_glu_b", "w_out", "g_ffn", "w_ffn_gate", "w_ffn_up", "w_ffn_down", "w_ple_gate", "w_ple_proj", "g_final")
    return (loss, dx.reshape(x.shape), *[new[n][0] for n in order], *[new[n][1] for n in order],
            *[new[n][2] for n in order], *[new[n][3] for n in order])
```

```python
import functools
import math

import jax
import jax.numpy as jnp
from jax import lax
from jax.experimental import pallas as pl
from jax.experimental.pallas import tpu as pltpu

F32 = jnp.float32
BF16 = jnp.bfloat16
SDS = jax.ShapeDtypeStruct

N_DEV = 8
HEAD_DIM = 128
HEADS_PER_GROUP = 4
GROUP_W = HEADS_PER_GROUP * HEAD_DIM
DILATIONS = (1, 4, 16)
N_GROUPS = len(DILATIONS)
QK_W = N_GROUPS * GROUP_W
BLK = 128
ROPE_THETA = 500000.0
ROPE_DIM = HEAD_DIM // 4
ROPE_HALF = ROPE_DIM // 2
SSM_W = 512
SSM_GROUP = 16
SSM_GROUPS = SSM_W // SSM_GROUP
SSM_STATE = 64
NSTATE = SSM_GROUPS * SSM_STATE
SSM_NB = 4
EPS = 1e-6
ADAM_LR, ADAM_B1, ADAM_B2, ADAM_EPS, ADAM_WD, ADAM_STEP = 0.001, 0.9, 0.999, 1e-08, 0.01, 10
NEG = -1e30

VMEM_LIMIT = 52 * 1024 * 1024
SCAN_ROWS = 256
SCAN_LANES = 512


def _cp(n):
    return pltpu.CompilerParams(dimension_semantics=("arbitrary",) * n, vmem_limit_bytes=VMEM_LIMIT)


def _sigmoid(x):
    return 1.0 / (1.0 + jnp.exp(-x))


_DNUMS = {"nn": (((1,), (0,)), ((), ())), "nt": (((1,), (1,)), ((), ())), "tn": (((0,), (0,)), ((), ()))}


def _bs(shape, fn):
    return pl.BlockSpec(shape, fn)


def _store_all(prods, extra_refs, out_refs, scratch_refs):
    r = prods[0]
    for p in prods[1:]:
        r = r + p
    for e in extra_refs:
        r = r + e[...]
    for o in out_refs:
        o[...] = r.astype(o.dtype)


def _mm(name, grid, pairs, outs, extras=(), epilogue=_store_all, scratch=(), alias_to_out0=None, after=None):
    nk = grid[2]
    npair = len(pairs)
    steps = [p[5] if len(p) > 5 else nk for p in pairs]

    def block(spec):
        return tuple(s for s in spec.block_shape if s is not None)

    acc_shapes = [jax.eval_shape(lambda u, v, dn=_DNUMS[p[0]]: lax.dot_general(u, v, dn, preferred_element_type=F32),
                                 SDS(block(p[2]), BF16), SDS(block(p[4]), BF16)).shape for p in pairs]
    if nk == 1:
        acc_shapes = []
    n_in = 2 * npair + len(extras) + (alias_to_out0 is not None) + (after is not None)

    def body(*refs):
        extra_refs = refs[2 * npair:2 * npair + len(extras)]
        out_refs = refs[n_in:n_in + len(outs)]
        rest = refs[n_in + len(outs):]
        acc_refs = rest[:len(acc_shapes)]
        scratch_refs = rest[len(acc_refs):]
        k = pl.program_id(2)

        def product(i):
            return lax.dot_general(refs[2 * i][...].astype(BF16), refs[2 * i + 1][...].astype(BF16), _DNUMS[pairs[i][0]],
                                   preferred_element_type=F32)

        if nk == 1:
            epilogue([product(i) for i in range(npair)], extra_refs, out_refs, scratch_refs)
            return
        for i in range(npair):
            @pl.when(k == 0)
            def _(i=i):
                acc_refs[i][...] = product(i)

            @pl.when((k > 0) & (k < steps[i]))
            def _(i=i):
                acc_refs[i][...] += product(i)

        @pl.when(k == nk - 1)
        def _():
            epilogue([a[...] for a in acc_refs], extra_refs, out_refs, scratch_refs)

    ins, in_specs = [], []
    for p in pairs:
        ins += [p[1], p[3]]
        in_specs += [p[2], p[4]]
    ins += [e[0] for e in extras]
    in_specs += [e[1] for e in extras]
    aliases = {}
    if alias_to_out0 is not None:
        aliases = {len(ins): 0}
        ins.append(alias_to_out0)
        in_specs.append(pl.BlockSpec(memory_space=pl.ANY))
    if after is not None:
        ins.append(after)
        in_specs.append(pl.BlockSpec(memory_space=pl.ANY))
    scratch_shapes = [pltpu.VMEM(s, F32) for s in acc_shapes] + list(scratch)
    return pl.pallas_call(body, grid=grid, in_specs=in_specs, out_specs=[o[1] for o in outs], out_shape=[o[0] for o in outs],
                          scratch_shapes=scratch_shapes, input_output_aliases=aliases, compiler_params=_cp(3), name=name)(*ins)


def _my_index():
    return 4 * lax.axis_index("x") + 2 * lax.axis_index("y") + lax.axis_index("c")


def _peer(d):
    mx, my, mc = lax.axis_index("x"), lax.axis_index("y"), lax.axis_index("c")
    return (mx ^ ((d >> 2) & 1), my ^ ((d >> 1) & 1), mc ^ (d & 1))


def _win(ref, kind, j, n):
    if kind == "all":
        return ref
    if kind == "slot":
        return ref.at[j]
    if kind == "rows":
        return ref.at[pl.ds(pl.multiple_of(j * n, 8), n)]
    return ref.at[:, pl.ds(pl.multiple_of(j * n, 128), n)]


def _win7(ref, kind, n):
    if kind == "slot":
        return ref.at[pl.ds(0, 7)]
    if kind == "rows":
        return ref.at[pl.ds(0, 7 * n)]
    return ref.at[:, pl.ds(0, 7 * n)]


def _full_shape(shard_shape, kind):
    if kind == "slot":
        return (N_DEV,) + tuple(shard_shape)
    if kind == "rows":
        return (N_DEV * shard_shape[0],) + tuple(shard_shape[1:])
    return (shard_shape[0], N_DEV * shard_shape[1])


def _shard_shape(full_shape, kind, n):
    if kind == "all":
        return tuple(full_shape)
    if kind == "slot":
        return tuple(full_shape[1:])
    if kind == "rows":
        return (n,) + tuple(full_shape[1:])
    return (full_shape[0], n)


_HBM = pl.BlockSpec(memory_space=pltpu.HBM)
_SEM = pl.BlockSpec(memory_space=pltpu.SEMAPHORE)
_DATAFLOW = pltpu.SideEffectType.DATAFLOW_SIDE_EFFECTING


def _exchange_start(name, srcs, kinds, sizes, gather):
    n = len(srcs)
    if gather:
        lands = [lax.empty(_full_shape(s.shape, k), s.dtype) for s, k in zip(srcs, kinds)]
    else:
        lands = [lax.empty((N_DEV,) + _shard_shape(s.shape, k, z), s.dtype) for s, k, z in zip(srcs, kinds, sizes)]

    def body(*refs):
        src, land = refs[:n], refs[n:2 * n]
        send_sems, recv_sems, local_sems = refs[2 * n], refs[2 * n + 1], refs[2 * n + 2]
        token = refs[4 * n + 3]
        me = _my_index()
        for a in range(n):
            _local_copy(src[a], land[a], kinds[a], sizes[a], gather, me, local_sems.at[a]).start()
        for a in range(n):
            for d in range(1, N_DEV):
                px, py, pc = _peer(d)
                if gather:
                    s_ref, d_ref = src[a], _win(land[a], kinds[a], me, sizes[a])
                else:
                    s_ref, d_ref = _win(src[a], kinds[a], 4 * px + 2 * py + pc, sizes[a]), land[a].at[me]
                pltpu.make_async_remote_copy(src_ref=s_ref, dst_ref=d_ref, send_sem=send_sems.at[a], recv_sem=recv_sems.at[a],
                                             device_id=(px, py, pc), device_id_type=pl.DeviceIdType.MESH).start()
        token[...] = jnp.zeros_like(token)

    hbm = [pltpu.with_memory_space_constraint(a, pltpu.HBM) for a in list(srcs) + lands]
    out = pl.pallas_call(
        body, name=name, in_specs=[_HBM] * (2 * n),
        out_shape=[pltpu.SemaphoreType.DMA((n,))] * 3 + [pltpu.HBM(a.shape, a.dtype) for a in hbm] + [SDS((8, 128), F32)],
        out_specs=[_SEM] * 3 + [_HBM] * (2 * n) + [pl.BlockSpec(memory_space=pltpu.VMEM)],
        input_output_aliases={i: 3 + i for i in range(2 * n)},
        compiler_params=pltpu.CompilerParams(has_side_effects=_DATAFLOW))(*hbm)
    return out[0:3], out[3:3 + n], out[3 + n:3 + 2 * n], out[-1]


def _local_copy(src, land, kind, size, gather, me, sem):
    if gather:
        return pltpu.make_async_copy(src, _win(land, kind, me, size), sem)
    return pltpu.make_async_copy(_win(src, kind, me, size), land.at[me], sem)


def _exchange_wait(name, started, which, kinds, sizes, gather, after):
    sems, srcs, lands, _ = started
    n = len(which)

    def body(*refs):
        src, land = refs[:n], refs[n:2 * n]
        send_ref, recv_ref, local_ref = refs[2 * n:2 * n + 3]
        me = _my_index()
        my_id = (lax.axis_index("x"), lax.axis_index("y"), lax.axis_index("c"))
        for i, a in enumerate(which):
            seven = _win7(land[i], kinds[a], sizes[a]) if gather else land[i].at[pl.ds(0, 7)]
            pltpu.make_async_remote_copy(src_ref=seven, dst_ref=seven, send_sem=send_ref.at[a], recv_sem=recv_ref.at[a],
                                         device_id=my_id, device_id_type=pl.DeviceIdType.MESH).wait()
            _local_copy(src[i], land[i], kinds[a], sizes[a], gather, me, local_ref.at[a]).wait()

    hbm = [srcs[a] for a in which] + [lands[a] for a in which]
    out = pl.pallas_call(
        body, name=name, in_specs=[_HBM] * (2 * n) + [_SEM] * 3 + [pl.BlockSpec(memory_space=pl.ANY)],
        out_shape=[pltpu.HBM(a.shape, a.dtype) for a in hbm], out_specs=[_HBM] * (2 * n),
        input_output_aliases={i: i for i in range(2 * n)},
        compiler_params=pltpu.CompilerParams(has_side_effects=_DATAFLOW))(*hbm, *sems, after)
    return out[n:]


def _gather_small(small):
    def body(in_ref, out_ref, send_sem, recv_sem, local_sem):
        me = _my_index()
        my_id = (lax.axis_index("x"), lax.axis_index("y"), lax.axis_index("c"))
        cp = pltpu.make_async_copy(in_ref, out_ref.at[me], local_sem)
        cp.start()
        for d in range(1, N_DEV):
            pltpu.make_async_remote_copy(src_ref=in_ref, dst_ref=out_ref.at[me], send_sem=send_sem, recv_sem=recv_sem,
                                         device_id=_peer(d), device_id_type=pl.DeviceIdType.MESH).start()
        seven = out_ref.at[pl.ds(0, 7)]
        pltpu.make_async_remote_copy(src_ref=seven, dst_ref=seven, send_sem=send_sem, recv_sem=recv_sem, device_id=my_id,
                                     device_id_type=pl.DeviceIdType.MESH).wait()
        cp.wait()

    any_spec = pl.BlockSpec(memory_space=pl.ANY)
    return pl.pallas_call(body, in_specs=[any_spec], out_specs=any_spec, out_shape=SDS((N_DEV,) + small.shape, F32),
                          scratch_shapes=[pltpu.SemaphoreType.DMA] * 3, name="gather_small")(small)


def _adamw(name, recv, w, m, v):
    rows, cols = w.shape
    tr = max(c for c in range(16, 257, 16) if rows % c == 0) if rows % 16 == 0 else rows

    def body(r_ref, w_ref, m_ref, v_ref, g_ref, d_ref, nm_ref, nv_ref):
        g = r_ref[0].astype(F32)
        for s in range(1, N_DEV):
            g = g + r_ref[s].astype(F32)
        nm = ADAM_B1 * m_ref[...] + (1.0 - ADAM_B1) * g
        nv = ADAM_B2 * v_ref[...] + (1.0 - ADAM_B2) * (g * g)
        m_hat = nm / (1.0 - ADAM_B1 ** ADAM_STEP)
        v_hat = nv / (1.0 - ADAM_B2 ** ADAM_STEP)
        g_ref[...] = g
        d_ref[...] = -ADAM_LR * (m_hat / (jnp.sqrt(v_hat) + ADAM_EPS) + ADAM_WD * w_ref[...])
        nm_ref[...] = nm
        nv_ref[...] = nv

    blk = _bs((tr, cols), lambda i: (i, 0))
    return pl.pallas_call(
        body, grid=(rows // tr,), in_specs=[_bs((N_DEV, tr, cols), lambda i: (0, i, 0)), blk, blk, blk],
        out_specs=[blk] * 4, out_shape=[SDS((rows, cols), F32)] * 4, compiler_params=_cp(1), name=name)(recv, w, m, v)


def _rms_fwd(name, x, g, tm):
    t, d = x.shape

    def body(x_ref, g_ref, n_ref):
        xv = x_ref[...]
        r = lax.rsqrt(jnp.mean(xv * xv, axis=-1, keepdims=True) + EPS)
        n_ref[...] = (xv * r * g_ref[...]).astype(BF16)

    return pl.pallas_call(body, grid=(t // tm,), in_specs=[_bs((tm, d), lambda i: (i, 0)), _bs((1, d), lambda i: (0, 0))],
                          out_specs=_bs((tm, d), lambda i: (i, 0)), out_shape=SDS((t, d), BF16), compiler_params=_cp(1),
                          name=name)(x, g)


def _accumulate_rows(ref, part):
    @pl.when(pl.program_id(0) == 0)
    def _():
        ref[...] = part

    @pl.when(pl.program_id(0) > 0)
    def _():
        ref[...] += part


def _rms_bwd_epilogue(prods, extra_refs, out_refs, scratch_refs):
    dyv = prods[0]
    for p in prods[1:]:
        dyv = dyv + p
    if len(extra_refs) > 3:
        dyv = dyv + extra_refs[3][...]
    xv = extra_refs[0][...]
    r = lax.rsqrt(jnp.mean(xv * xv, axis=-1, keepdims=True) + EPS)
    xh = xv * r
    dxh = dyv * extra_refs[1][...]
    dx = extra_refs[2][...] + r * (dxh - xh * jnp.mean(dxh * xh, axis=-1, keepdims=True))
    for o in out_refs[:-1]:
        o[...] = dx.astype(o.dtype)
    _accumulate_rows(out_refs[-1], jnp.sum(dyv * xh, axis=0, keepdims=True))


def _out_norm_epilogue(prods, extra_refs, out_refs, scratch_refs):
    h = prods[0] + extra_refs[0][...]
    r = lax.rsqrt(jnp.mean(h * h, axis=-1, keepdims=True) + EPS)
    out_refs[0][...] = h
    out_refs[1][...] = (h * r * extra_refs[1][...]).astype(BF16)


def _glu_merge_epilogue(prods, extra_refs, out_refs, scratch_refs):
    ya, yb, ad = prods
    ga, gs = extra_refs[0][...].astype(F32), extra_refs[1][...].astype(F32)
    m = _sigmoid(ga) * ad + _sigmoid(gs) * (ya * _sigmoid(yb))
    out_refs[0][...] = m.astype(BF16)
    for o, val in zip(out_refs[1:], (ya, yb, ad)):
        o[...] = val.astype(o.dtype)


def _merge_bwd_epilogue(prods, extra_refs, out_refs, scratch_refs):
    dmv = prods[0]
    d = dmv.shape[1]
    ga, gs = _sigmoid(extra_refs[0][...].astype(F32)), _sigmoid(extra_refs[1][...].astype(F32))
    adv, yav = extra_refs[2][...].astype(F32), extra_refs[3][...].astype(F32)
    sb = _sigmoid(extra_refs[4][...].astype(F32))
    out_refs[0][:, 0:d] = (dmv * adv * ga * (1.0 - ga)).astype(BF16)
    out_refs[0][:, d:2 * d] = (dmv * (yav * sb) * gs * (1.0 - gs)).astype(BF16)
    out_refs[1][...] = (dmv * ga).astype(BF16)
    dsd = dmv * gs
    out_refs[2][...] = (dsd * sb).astype(BF16)
    out_refs[3][...] = (dsd * yav * sb * (1.0 - sb)).astype(BF16)


def _swiglu_epilogue(prods, extra_refs, out_refs, scratch_refs):
    gv, uv = prods
    out_refs[0][...] = (gv * _sigmoid(gv) * uv).astype(BF16)
    out_refs[1][...] = gv.astype(out_refs[1].dtype)
    out_refs[2][...] = uv.astype(out_refs[2].dtype)


def _swiglu_bwd_epilogue(prods, extra_refs, out_refs, scratch_refs):
    dav = prods[0]
    gv, uv = extra_refs[0][...].astype(F32), extra_refs[1][...].astype(F32)
    sg = _sigmoid(gv)
    out_refs[0][...] = (dav * uv * sg * (1.0 + gv * (1.0 - sg))).astype(BF16)
    out_refs[1][...] = (dav * gv * sg).astype(BF16)


def _head_epilogue(n_tiles):
    def epilogue(prods, extra_refs, out_refs, scratch_refs):
        pgv, ppv = prods
        d = pgv.shape[1]
        lacc = scratch_refs[0]
        sg = _sigmoid(pgv)
        h3 = extra_refs[0][...] + sg * ppv
        r = lax.rsqrt(jnp.mean(h3 * h3, axis=-1, keepdims=True) + EPS)
        xh = h3 * r
        gv = extra_refs[1][...]
        diff = xh * gv - extra_refs[2][...]
        dout = diff * (1.0 / d)
        dxh = dout * gv
        dh3 = r * (dxh - xh * jnp.mean(dxh * xh, axis=-1, keepdims=True))
        out_refs[2][...] = dh3
        out_refs[3][...] = (dh3 * sg).astype(BF16)
        out_refs[4][...] = (dh3 * ppv * sg * (1.0 - sg)).astype(BF16)
        _accumulate_rows(out_refs[1], jnp.sum(dout * xh, axis=0, keepdims=True))
        _accumulate_rows(lacc, jnp.sum(diff * diff, axis=0, keepdims=True))

        @pl.when(pl.program_id(0) == n_tiles - 1)
        def _():
            out_refs[0][...] = (0.5 / d) * jnp.sum(lacc[...], axis=-1, keepdims=True)

    return epilogue


def _strided(r, n, d):
    return pl.ds(r, n, stride=d) if d > 1 else pl.ds(0, n)


def _rope_tables(pos_ref, invf_ref, c_s, s1_s, s2_s):
    ang = pos_ref[...].astype(F32) * invf_ref[...]
    lane = lax.broadcasted_iota(jnp.int32, ang.shape, 1)
    sn = jnp.sin(ang)
    c_s[...] = jnp.where(lane < ROPE_DIM, jnp.cos(ang), 1.0)
    s1_s[...] = jnp.where(lane < ROPE_HALF, -sn, 0.0)
    s2_s[...] = jnp.where((lane >= ROPE_HALF) & (lane < ROPE_DIM), sn, 0.0)


def _rope_dilate_epilogue(tm):
    def epilogue(prods, extra_refs, out_refs, scratch_refs):
        zv = prods[0]
        pos_ref, invf_ref = extra_refs
        c_s, s1_s, s2_s, rot = scratch_refs
        c = pl.program_id(1)

        @pl.when(c == 0)
        def _():
            _rope_tables(pos_ref, invf_ref, c_s, s1_s, s2_s)

        @pl.when(c < 2)
        def _():
            cc, s1, s2 = c_s[...], s1_s[...], s2_s[...]
            for h in range(QK_W // HEAD_DIM):
                xv = zv[:, h * HEAD_DIM:(h + 1) * HEAD_DIM]
                rot[h] = xv * cc + pltpu.roll(xv, HEAD_DIM - ROPE_HALF, 1) * s1 + pltpu.roll(xv, ROPE_HALF, 1) * s2

        @pl.when(c == 2)
        def _():
            for h in range(QK_W // HEAD_DIM):
                rot[h] = zv[:, h * HEAD_DIM:(h + 1) * HEAD_DIM]

        for g, (d, o_ref) in enumerate(zip(DILATIONS, out_refs)):
            n = tm // d
            for r in range(d):
                for hh in range(HEADS_PER_GROUP):
                    oc = r * GROUP_W + hh * HEAD_DIM
                    o_ref[:, oc:oc + HEAD_DIM] = rot[g * HEADS_PER_GROUP + hh, _strided(r, n, d), :].astype(BF16)

    return epilogue


def _band_mask(first):
    qi = lax.broadcasted_iota(jnp.int32, (BLK, 2 * BLK), 0)
    kj = lax.broadcasted_iota(jnp.int32, (BLK, 2 * BLK), 1)
    return (kj >= qi) & (kj <= qi + BLK) & ((kj >= BLK) | jnp.logical_not(first))


def _attn_fwd(qkv, d, qt):
    ell = qkv.shape[1]
    nsub = qt // BLK
    scale = 1.0 / math.sqrt(HEAD_DIM)

    def body(q_ref, kc_ref, kp_ref, vc_ref, vp_ref, o_ref, lse_ref, kcat, vcat):
        nb = pl.program_id(1)
        kcat[0:BLK, :] = kp_ref[...]
        kcat[BLK:, :] = kc_ref[...]
        vcat[0:BLK, :] = vp_ref[...]
        vcat[BLK:, :] = vc_ref[...]
        lane = lax.broadcasted_iota(jnp.int32, (BLK, HEAD_DIM), 1)
        for b in range(nsub):
            valid = _band_mask((nb == 0) if b == 0 else False)
            lse_t = jnp.zeros((BLK, HEAD_DIM), F32)
            for hh in range(HEADS_PER_GROUP):
                cs = slice(hh * HEAD_DIM, (hh + 1) * HEAD_DIM)
                qb = q_ref[b * BLK:(b + 1) * BLK, cs]
                kk = kcat[b * BLK:(b + 2) * BLK, cs]
                vv = vcat[b * BLK:(b + 2) * BLK, cs]
                s = lax.dot_general(qb, kk, _DNUMS["nt"], preferred_element_type=F32) * scale
                s = jnp.where(valid, s, NEG)
                mx = jnp.max(s, axis=-1, keepdims=True)
                p = jnp.exp(s - mx)
                den = jnp.sum(p, axis=-1, keepdims=True)
                o = jnp.dot(p.astype(BF16), vv, preferred_element_type=F32) / den
                o_ref[b * BLK:(b + 1) * BLK, cs] = o
                lse_t = jnp.where(lane == hh, mx + jnp.log(den), lse_t)
            lse_ref[b * BLK:(b + 1) * BLK, :] = lse_t

    cur = lambda c: _bs((None, qt, GROUP_W), lambda r, nb: (c, nb, r))
    prev = lambda c: _bs((None, BLK, GROUP_W), lambda r, nb: (c, jnp.maximum(nb * nsub - 1, 0), r))
    return pl.pallas_call(
        body, grid=(d, ell // qt), in_specs=[cur(0), cur(1), prev(1), cur(2), prev(2)],
        out_specs=[_bs((qt, GROUP_W), lambda r, nb: (nb, r)), _bs((None, qt, HEAD_DIM), lambda r, nb: (r, nb, 0))],
        out_shape=[SDS((ell, d * GROUP_W), F32), SDS((d, ell, HEAD_DIM), F32)],
        scratch_shapes=[pltpu.VMEM((qt + BLK, GROUP_W), BF16)] * 2, compiler_params=_cp(2), name=f"attn_fwd_d{d}")(
            qkv, qkv, qkv, qkv, qkv)


def _attn_merge(outs, lses, tm):
    t = outs[0].shape[0]

    def body(o0, o1, o2, l0, l1, l2, attn_ref, attn_bf_ref, t0, t1, t2, so, sl, lt_s):
        for g, (d, o_ref, l_ref) in enumerate(zip(DILATIONS, (o0, o1, o2), (l0, l1, l2))):
            n = tm // d
            for r in range(d):
                rows = _strided(r, n, d)
                for hh in range(HEADS_PER_GROUP):
                    oc = r * GROUP_W + hh * HEAD_DIM
                    so[g * HEADS_PER_GROUP + hh, rows, :] = o_ref[:, oc:oc + HEAD_DIM]
                sl[g, rows, :] = l_ref[r]
        ls = [sl[g] for g in range(N_GROUPS)]
        mx = jnp.maximum(jnp.maximum(ls[0], ls[1]), ls[2])
        es = [jnp.exp(l - mx) for l in ls]
        den = es[0] + es[1] + es[2]
        ws = [e / den for e in es]
        lt_s[...] = mx + jnp.log(den)
        for hh in range(HEADS_PER_GROUP):
            cs = slice(hh * HEAD_DIM, (hh + 1) * HEAD_DIM)
            a = ws[0][:, hh:hh + 1] * so[hh]
            for g in range(1, N_GROUPS):
                a = a + ws[g][:, hh:hh + 1] * so[g * HEADS_PER_GROUP + hh]
            attn_ref[:, cs] = a
            attn_bf_ref[:, cs] = a.astype(BF16)
        for d, t_ref in zip(DILATIONS, (t0, t1, t2)):
            n = tm // d
            for r in range(d):
                t_ref[r] = lt_s[_strided(r, n, d), :]

    dil = lambda d: _bs((tm // d, d * GROUP_W), lambda i: (i, 0))
    lsp = lambda d: _bs((d, tm // d, HEAD_DIM), lambda i: (0, i, 0))
    row = _bs((tm, GROUP_W), lambda i: (i, 0))
    return pl.pallas_call(
        body, grid=(t // tm,),
        in_specs=[dil(d) for d in DILATIONS] + [lsp(d) for d in DILATIONS],
        out_specs=[row, row] + [lsp(d) for d in DILATIONS],
        out_shape=[SDS((t, GROUP_W), F32), SDS((t, GROUP_W), BF16)] + [SDS(l.shape, F32) for l in lses],
        scratch_shapes=[pltpu.VMEM((N_GROUPS * HEADS_PER_GROUP, tm, HEAD_DIM), F32), pltpu.VMEM((N_GROUPS, tm, HEAD_DIM), F32),
                        pltpu.VMEM((tm, HEAD_DIM), F32)],
        compiler_params=_cp(1), name="attn_merge")(*outs, *lses)


def _attn_bwd_pre(d_attn, attn, tm):
    t = attn.shape[0]

    def body(da_ref, a_ref, g0, g1, g2, e0, e1, e2, dl_s, da_s):
        lane = lax.broadcasted_iota(jnp.int32, (tm, HEAD_DIM), 1)
        dl = jnp.zeros((tm, HEAD_DIM), F32)
        for hh in range(HEADS_PER_GROUP):
            cs = slice(hh * HEAD_DIM, (hh + 1) * HEAD_DIM)
            dav = da_ref[:, cs]
            da_s[hh] = dav
            dl = jnp.where(lane == hh, jnp.sum(dav * a_ref[:, cs], axis=-1, keepdims=True), dl)
        dl_s[...] = dl
        for d, g_ref, e_ref in zip(DILATIONS, (g0, g1, g2), (e0, e1, e2)):
            n = tm // d
            for r in range(d):
                rows = _strided(r, n, d)
                for hh in range(HEADS_PER_GROUP):
                    oc = r * GROUP_W + hh * HEAD_DIM
                    g_ref[:, oc:oc + HEAD_DIM] = da_s[hh, rows, :].astype(BF16)
                e_ref[r] = dl_s[rows, :]

    row = _bs((tm, GROUP_W), lambda i: (i, 0))
    return pl.pallas_call(
        body, grid=(t // tm,), in_specs=[row, row],
        out_specs=[_bs((tm // d, d * GROUP_W), lambda i: (i, 0)) for d in DILATIONS]
        + [_bs((d, tm // d, HEAD_DIM), lambda i: (0, i, 0)) for d in DILATIONS],
        out_shape=[SDS((t // d, d * GROUP_W), BF16) for d in DILATIONS]
        + [SDS((d, t // d, HEAD_DIM), F32) for d in DILATIONS],
        scratch_shapes=[pltpu.VMEM((tm, HEAD_DIM), F32), pltpu.VMEM((HEADS_PER_GROUP, tm, HEAD_DIM), F32)],
        compiler_params=_cp(1), name="attn_bwd_pre")(d_attn, attn)


def _attn_bwd(qkv, d_a, lt, delta, d, qt):
    ell = qkv.shape[1]
    nsub = qt // BLK
    ntile = ell // qt
    nblk = ell // BLK
    scale = 1.0 / math.sqrt(HEAD_DIM)

    def body(q_ref, qn_ref, kc_ref, kp_ref, vc_ref, vp_ref, da_ref, dan_ref, lt_ref, ltn_ref, dl_ref, dln_ref, o_ref,
             kcat, vcat, dk_acc, dv_acc):
        nb = pl.program_id(1)
        kcat[0:BLK, :] = kp_ref[...]
        kcat[BLK:, :] = kc_ref[...]
        vcat[0:BLK, :] = vp_ref[...]
        vcat[BLK:, :] = vc_ref[...]
        qi = lax.broadcasted_iota(jnp.int32, (BLK, BLK), 0)
        kj = lax.broadcasted_iota(jnp.int32, (BLK, BLK), 1)
        valid_next = (kj >= qi) & (nb < ntile - 1)
        for hh in range(HEADS_PER_GROUP):
            cs = slice(hh * HEAD_DIM, (hh + 1) * HEAD_DIM)
            dk_acc[...] = jnp.zeros_like(dk_acc)
            dv_acc[...] = jnp.zeros_like(dv_acc)
            for b in range(nsub):
                rs = slice(b * BLK, (b + 1) * BLK)
                ks = slice(b * BLK, (b + 2) * BLK)
                valid = _band_mask((nb == 0) if b == 0 else False)
                qb, kk, vv, dab = q_ref[rs, cs], kcat[ks, cs], vcat[ks, cs], da_ref[rs, cs]
                s = lax.dot_general(qb, kk, _DNUMS["nt"], preferred_element_type=F32) * scale
                p = jnp.where(valid, jnp.exp(s - lt_ref[rs, hh:hh + 1]), 0.0)
                dp = lax.dot_general(dab, vv, _DNUMS["nt"], preferred_element_type=F32)
                ds = (p * (dp - dl_ref[rs, hh:hh + 1])).astype(BF16)
                o_ref[0, rs, cs] = jnp.dot(ds, kk, preferred_element_type=F32) * scale
                dk_acc[ks, :] += lax.dot_general(ds, qb, _DNUMS["tn"], preferred_element_type=F32) * scale
                dv_acc[ks, :] += lax.dot_general(p.astype(BF16), dab, _DNUMS["tn"], preferred_element_type=F32)
            ks = slice(nsub * BLK, (nsub + 1) * BLK)
            qn, kl, vl, dan = qn_ref[:, cs], kcat[ks, cs], vcat[ks, cs], dan_ref[:, cs]
            s = lax.dot_general(qn, kl, _DNUMS["nt"], preferred_element_type=F32) * scale
            p = jnp.where(valid_next, jnp.exp(s - ltn_ref[:, hh:hh + 1]), 0.0)
            dp = lax.dot_general(dan, vl, _DNUMS["nt"], preferred_element_type=F32)
            ds = (p * (dp - dln_ref[:, hh:hh + 1])).astype(BF16)
            dk_acc[ks, :] += lax.dot_general(ds, qn, _DNUMS["tn"], preferred_element_type=F32) * scale
            dv_acc[ks, :] += lax.dot_general(p.astype(BF16), dan, _DNUMS["tn"], preferred_element_type=F32)
            o_ref[1, :, cs] = dk_acc[BLK:, :]
            o_ref[2, :, cs] = dv_acc[BLK:, :]

    nxt = lambda nb: jnp.minimum((nb + 1) * nsub, nblk - 1)
    prv = lambda nb: jnp.maximum(nb * nsub - 1, 0)
    cur3 = lambda c: _bs((None, qt, GROUP_W), lambda r, nb: (c, nb, r))
    in_specs = [
        cur3(0), _bs((None, BLK, GROUP_W), lambda r, nb: (0, nxt(nb), r)),
        cur3(1), _bs((None, BLK, GROUP_W), lambda r, nb: (1, prv(nb), r)),
        cur3(2), _bs((None, BLK, GROUP_W), lambda r, nb: (2, prv(nb), r)),
        _bs((qt, GROUP_W), lambda r, nb: (nb, r)), _bs((BLK, GROUP_W), lambda r, nb: (nxt(nb), r)),
        _bs((None, qt, HEAD_DIM), lambda r, nb: (r, nb, 0)), _bs((None, BLK, HEAD_DIM), lambda r, nb: (r, nxt(nb), 0)),
        _bs((None, qt, HEAD_DIM), lambda r, nb: (r, nb, 0)), _bs((None, BLK, HEAD_DIM), lambda r, nb: (r, nxt(nb), 0)),
    ]
    return pl.pallas_call(
        body, grid=(d, ntile), in_specs=in_specs, out_specs=_bs((3, qt, GROUP_W), lambda r, nb: (0, nb, r)),
        out_shape=SDS((3, ell, d * GROUP_W), F32),
        scratch_shapes=[pltpu.VMEM((qt + BLK, GROUP_W), BF16)] * 2 + [pltpu.VMEM((qt + BLK, HEAD_DIM), F32)] * 2,
        compiler_params=_cp(2), name=f"attn_bwd_d{d}")(qkv, qkv, qkv, qkv, qkv, qkv, d_a, d_a, lt, lt, delta, delta)


def _undilate_rope_bwd(dqkvs, pos, invf, tm):
    t = pos.shape[0]

    def body(g0, g1, g2, pos_ref, invf_ref, o_ref, c_s, s1_s, s2_s, nat):
        c = pl.program_id(1)

        @pl.when(c == 0)
        def _():
            _rope_tables(pos_ref, invf_ref, c_s, s1_s, s2_s)

        for g, (d, g_ref) in enumerate(zip(DILATIONS, (g0, g1, g2))):
            n = tm // d
            for r in range(d):
                for hh in range(HEADS_PER_GROUP):
                    oc = r * GROUP_W + hh * HEAD_DIM
                    nat[g * HEADS_PER_GROUP + hh, _strided(r, n, d), :] = g_ref[:, oc:oc + HEAD_DIM]

        @pl.when(c < 2)
        def _():
            cc, s1, s2 = c_s[...], s1_s[...], s2_s[...]
            for h in range(QK_W // HEAD_DIM):
                xv = nat[h]
                y = xv * cc - pltpu.roll(xv, HEAD_DIM - ROPE_HALF, 1) * s1 - pltpu.roll(xv, ROPE_HALF, 1) * s2
                o_ref[:, h * HEAD_DIM:(h + 1) * HEAD_DIM] = y.astype(BF16)

        @pl.when(c == 2)
        def _():
            for h in range(QK_W // HEAD_DIM):
                o_ref[:, h * HEAD_DIM:(h + 1) * HEAD_DIM] = nat[h].astype(BF16)

    return pl.pallas_call(
        body, grid=(t // tm, 3),
        in_specs=[_bs((None, tm // d, d * GROUP_W), lambda i, c: (c, i, 0)) for d in DILATIONS]
        + [_bs((tm, 1), lambda i, c: (i, 0)), _bs((1, HEAD_DIM), lambda i, c: (0, 0))],
        out_specs=_bs((tm, QK_W), lambda i, c: (i, c)), out_shape=SDS((t, 3 * QK_W), BF16),
        scratch_shapes=[pltpu.VMEM((tm, HEAD_DIM), F32)] * 3 + [pltpu.VMEM((QK_W // HEAD_DIM, tm, HEAD_DIM), F32)],
        compiler_params=_cp(2), name="undilate_rope_bwd")(*dqkvs, pos, invf)


def _cmul(ar, ai, br, bi):
    return ar * br - ai * bi, ar * bi + ai * br


def _ssm_disc(a_re, a_im, log_dt, nsq):
    def body(lr_ref, li_ref, ldt_ref, br_ref, bi_ref, zr_ref, zi_ref, pr_ref, pi_ref):
        lr, li = lr_ref[...], li_ref[...]
        dt = jnp.exp(ldt_ref[...])
        mag = jnp.exp(lr * dt)
        bar_re, bar_im = mag * jnp.cos(li * dt), mag * jnp.sin(li * dt)
        nr, ni = bar_re - 1.0, bar_im
        den = lr * lr + li * li
        br_ref[...], bi_ref[...] = bar_re, bar_im
        zr_ref[...] = (nr * lr + ni * li) / den
        zi_ref[...] = (ni * lr - nr * li) / den
        pr, pi = bar_re, bar_im
        for _ in range(nsq):
            pr, pi = _cmul(pr, pi, pr, pi)
        pr_ref[...], pi_ref[...] = pr, pi

    return pl.pallas_call(body, out_shape=[SDS(a_re.shape, F32)] * 6, name="ssm_discretise")(a_re, a_im, log_dt)


def _ssm_scale_b(z_re, z_im, b_re, b_im):
    def body(zr_ref, zi_ref, br_ref, bi_ref, or_ref, oi_ref):
        zr, zi, br, bi = zr_ref[...], zi_ref[...], br_ref[...], bi_ref[...]
        or_ref[...] = zr * br - zi * bi
        oi_ref[...] = zr * bi + zi * br

    return pl.pallas_call(body, out_shape=[SDS(b_re.shape, F32)] * 2, name="ssm_scale_b")(z_re, z_im, b_re, b_im)


def _ssm_scale_b_bwd(z_re, z_im, b_re, b_im, g_re, g_im):
    def body(zr_ref, zi_ref, br_ref, bi_ref, gr_ref, gi_ref, dbr_ref, dbi_ref, dzr_ref, dzi_ref):
        zr, zi, br, bi, gr, gi = zr_ref[...], zi_ref[...], br_ref[...], bi_ref[...], gr_ref[...], gi_ref[...]
        dbr_ref[...] = zr * gr + zi * gi
        dbi_ref[...] = zr * gi - zi * gr
        dzr_ref[...] = jnp.sum(br * gr + bi * gi, axis=-1, keepdims=True)
        dzi_ref[...] = jnp.sum(br * gi - bi * gr, axis=-1, keepdims=True)

    return pl.pallas_call(body, out_shape=[SDS(b_re.shape, F32)] * 2 + [SDS(z_re.shape, F32)] * 2,
                          name="ssm_scale_b_bwd")(z_re, z_im, b_re, b_im, g_re, g_im)


def _ssm_disc_bwd(a_re, a_im, log_dt, gb_re, gb_im, gz_re, gz_im):
    def body(lr_ref, li_ref, ldt_ref, gbr_ref, gbi_ref, gzr_ref, gzi_ref, dar_ref, dai_ref, dldt_ref):
        lr, li = lr_ref[...], li_ref[...]
        dt = jnp.exp(ldt_ref[...])
        mag = jnp.exp(lr * dt)
        bar_re, bar_im = mag * jnp.cos(li * dt), mag * jnp.sin(li * dt)
        nr, ni = bar_re - 1.0, bar_im
        den = lr * lr + li * li
        zr, zi = (nr * lr + ni * li) / den, (ni * lr - nr * li) / den
        gzr, gzi = gzr_ref[...], gzi_ref[...]
        gbr = gbr_ref[...] + (lr * gzr - li * gzi) / den
        gbi = gbi_ref[...] + (lr * gzi + li * gzr) / den
        qr, qi = (zr * lr + zi * li) / den, (zi * lr - zr * li) / den
        dar_ref[...] = dt * (bar_re * gbr + bar_im * gbi) - qr * gzr - qi * gzi
        dai_ref[...] = dt * (bar_re * gbi - bar_im * gbr) - qr * gzi + qi * gzr
        wr, wi = lr * bar_re - li * bar_im, lr * bar_im + li * bar_re
        dldt_ref[...] = dt * jnp.sum(wr * gbr + wi * gbi, axis=-1, keepdims=True)

    return pl.pallas_call(body, out_shape=[SDS(a_re.shape, F32)] * 2 + [SDS(log_dt.shape, F32)],
                          name="ssm_discretise_bwd")(a_re, a_im, log_dt, gb_re, gb_im, gz_re, gz_im)


def _permute_u(z, ucol_block, tm):
    t = z.shape[0]
    seg = t // N_DEV
    z3 = z.reshape(N_DEV, seg, z.shape[1])

    def body(z_ref, u_ref, ub_ref, tmp):
        for n in range(SSM_W // BLK):
            for j in range(N_DEV):
                tmp[n, pl.ds(j, tm // N_DEV, stride=N_DEV), :] = z_ref[j, :, n * BLK:(n + 1) * BLK]
            u_ref[:, n * BLK:(n + 1) * BLK] = tmp[n]
            ub_ref[:, n * BLK:(n + 1) * BLK] = tmp[n].astype(BF16)

    row = _bs((tm, SSM_W), lambda i: (i, 0))
    return pl.pallas_call(
        body, grid=(t // tm,), in_specs=[_bs((N_DEV, tm // N_DEV, SSM_W), lambda i: (0, i, ucol_block))],
        out_specs=[row, row], out_shape=[SDS((t, SSM_W), F32), SDS((t, SSM_W), BF16)],
        scratch_shapes=[pltpu.VMEM((SSM_W // BLK, tm, BLK), F32)], compiler_params=_cp(1), name="permute_u")(z3)


def _drive(src_ref, mat_ref, dst, mode):
    for kn in range(2 * SSM_NB):
        n = kn % SSM_NB
        a = src_ref[:, n * BLK:(n + 1) * BLK]
        dst[:, kn * 512:(kn + 1) * 512] = lax.dot_general(a, mat_ref[kn], _DNUMS[mode], preferred_element_type=F32)


def _scan_chunk(src, lam_ref, carry, *, reverse, store=None, h_ref=None, acc=None):
    steps = src.shape[0] // 8
    for c in range(NSTATE // SCAN_LANES):
        re = slice(c * SCAN_LANES, (c + 1) * SCAN_LANES)
        im = slice(NSTATE + c * SCAN_LANES, NSTATE + (c + 1) * SCAN_LANES)
        ar, ai = lam_ref[:, re], lam_ref[:, im]

        def step(s, val):
            i = (steps - 1 - s) if reverse else s
            rows = pl.ds(pl.multiple_of(i * 8, 8), 8)
            if acc is not None:
                hr, hi, dr, di = val
                pr, pi = h_ref[rows, re], h_ref[rows, im]
                dr = dr + hr * pr + hi * pi
                di = di + hi * pr - hr * pi
            else:
                hr, hi = val
            nr = ar * hr - ai * hi + src[rows, re]
            ni = ar * hi + ai * hr + src[rows, im]
            if store is not None:
                store[rows, re] = nr
                store[rows, im] = ni
            return (nr, ni, dr, di) if acc is not None else (nr, ni)

        init = (carry[:, re], carry[:, im])
        if acc is not None:
            init = init + (acc[:, re], acc[:, im])
        out = lax.fori_loop(0, steps, step, init, unroll=4)
        carry[:, re], carry[:, im] = out[0], out[1]
        if acc is not None:
            acc[:, re], acc[:, im] = out[2], out[3]


def _segment_carries(e_ref, pw_ref, out_ref, reverse):
    pr, pi = pw_ref[:, 0:NSTATE], pw_ref[:, NSTATE:]
    hr = jnp.zeros((1, NSTATE), F32)
    hi = jnp.zeros((1, NSTATE), F32)
    order = range(N_DEV - 1, -1, -1) if reverse else range(N_DEV)
    for j in order:
        out_ref[j:j + 1, 0:NSTATE] = hr
        out_ref[j:j + 1, NSTATE:] = hi
        tr, ti = _cmul(pr, pi, hr, hi)
        hr, hi = e_ref[j:j + 1, 0:NSTATE] + tr, e_ref[j:j + 1, NSTATE:] + ti


def _ssm_carries(name, src, mat, mode, lam8, pw, reverse):
    t = src.shape[0]
    nchunk = t // SCAN_ROWS

    def body(src_ref, mat_ref, lam_ref, pw_ref, out_ref, drive, carry):
        c = pl.program_id(0)

        @pl.when(c == 0)
        def _():
            carry[...] = jnp.zeros_like(carry)

        _drive(src_ref, mat_ref, drive, mode)
        _scan_chunk(drive, lam_ref, carry, reverse=reverse)

        @pl.when(c == nchunk - 1)
        def _():
            _segment_carries(carry, pw_ref, out_ref, reverse)

    blk = (lambda c: (nchunk - 1 - c, 0)) if reverse else (lambda c: (c, 0))
    return pl.pallas_call(
        body, grid=(nchunk,),
        in_specs=[_bs((SCAN_ROWS, SSM_W), blk), _bs(mat.shape, lambda c: (0, 0, 0)), _bs((8, 2 * NSTATE), lambda c: (0, 0)),
                  _bs((1, 2 * NSTATE), lambda c: (0, 0))],
        out_specs=_bs((8, 2 * NSTATE), lambda c: (0, 0)), out_shape=SDS((8, 2 * NSTATE), F32),
        scratch_shapes=[pltpu.VMEM((SCAN_ROWS, 2 * NSTATE), F32), pltpu.VMEM((8, 2 * NSTATE), F32)],
        compiler_params=_cp(1), name=name)(src, mat, lam8, pw)


def _ssm_fwd(u_bf, bd, cd, lam8, start):
    t = u_bf.shape[0]
    nchunk = t // SCAN_ROWS

    def body(u_ref, bd_ref, cd_ref, lam_ref, start_ref, h_ref, y_ref, drive, carry):
        @pl.when(pl.program_id(0) == 0)
        def _():
            carry[...] = start_ref[...]

        _drive(u_ref, bd_ref, drive, "nn")
        _scan_chunk(drive, lam_ref, carry, reverse=False, store=h_ref)
        for n in range(SSM_NB):
            hr = h_ref[:, n * 512:(n + 1) * 512].astype(BF16)
            hi = h_ref[:, NSTATE + n * 512:NSTATE + (n + 1) * 512].astype(BF16)
            y_ref[:, n * BLK:(n + 1) * BLK] = (jnp.dot(hr, cd_ref[n], preferred_element_type=F32)
                                              + jnp.dot(hi, cd_ref[SSM_NB + n], preferred_element_type=F32))

    return pl.pallas_call(
        body, grid=(nchunk,),
        in_specs=[_bs((SCAN_ROWS, SSM_W), lambda c: (c, 0)), _bs(bd.shape, lambda c: (0, 0, 0)), _bs(cd.shape, lambda c: (0, 0, 0)),
                  _bs((8, 2 * NSTATE), lambda c: (0, 0)), _bs((8, 2 * NSTATE), lambda c: (0, 0))],
        out_specs=[_bs((SCAN_ROWS, 2 * NSTATE), lambda c: (c, 0)), _bs((SCAN_ROWS, SSM_W), lambda c: (c, 0))],
        out_shape=[SDS((t, 2 * NSTATE), F32), SDS((t, SSM_W), F32)],
        scratch_shapes=[pltpu.VMEM((SCAN_ROWS, 2 * NSTATE), F32), pltpu.VMEM((8, 2 * NSTATE), F32)],
        compiler_params=_cp(1), name="ssm_scan_fwd")(u_bf, bd, cd, lam8, start)


def _ssm_bwd(dys_bf, u_bf, h, bd, cd, lamc8, start):
    t = u_bf.shape[0]
    nchunk = t // SCAN_ROWS

    def body(dys_ref, u_ref, h_ref, bd_ref, cd_ref, lam_ref, start_ref, du_ref, dlam_ref, dbd_ref, dcd_ref, drive, adj, carry):
        c = pl.program_id(0)

        @pl.when(c == 0)
        def _():
            carry[...] = start_ref[...]
            dlam_ref[...] = jnp.zeros_like(dlam_ref)
            dbd_ref[...] = jnp.zeros_like(dbd_ref)
            dcd_ref[...] = jnp.zeros_like(dcd_ref)

        _drive(dys_ref, cd_ref, drive, "nt")
        _scan_chunk(drive, lam_ref, carry, reverse=True, store=adj, h_ref=h_ref, acc=dlam_ref)
        for n in range(SSM_NB):
            cs = slice(n * BLK, (n + 1) * BLK)
            acc = None
            for k in range(2):
                kn = k * SSM_NB + n
                ss = slice(kn * 512, (kn + 1) * 512)
                lam_b = adj[:, ss].astype(BF16)
                part = lax.dot_general(lam_b, bd_ref[kn], _DNUMS["nt"], preferred_element_type=F32)
                acc = part if acc is None else acc + part
                dbd_ref[kn] += lax.dot_general(u_ref[:, cs], lam_b, _DNUMS["tn"], preferred_element_type=F32)
                dcd_ref[kn] += lax.dot_general(h_ref[:, ss].astype(BF16), dys_ref[:, cs], _DNUMS["tn"],
                                               preferred_element_type=F32)
            du_ref[:, cs] = acc

    rev = lambda c: (nchunk - 1 - c, 0)
    const2 = lambda c: (0, 0)
    const3 = lambda c: (0, 0, 0)
    return pl.pallas_call(
        body, grid=(nchunk,),
        in_specs=[_bs((SCAN_ROWS, SSM_W), rev), _bs((SCAN_ROWS, SSM_W), rev), _bs((SCAN_ROWS, 2 * NSTATE), rev),
                  _bs(bd.shape, const3), _bs(cd.shape, const3), _bs((8, 2 * NSTATE), const2), _bs((8, 2 * NSTATE), const2)],
        out_specs=[_bs((SCAN_ROWS, SSM_W), rev), _bs((8, 2 * NSTATE), const2), _bs(bd.shape, const3), _bs(cd.shape, const3)],
        out_shape=[SDS((t, SSM_W), F32), SDS((8, 2 * NSTATE), F32), SDS(bd.shape, F32), SDS(cd.shape, F32)],
        scratch_shapes=[pltpu.VMEM((SCAN_ROWS, 2 * NSTATE), F32), pltpu.VMEM((SCAN_ROWS, 2 * NSTATE), F32),
                        pltpu.VMEM((8, 2 * NSTATE), F32)],
        compiler_params=_cp(1), name="ssm_scan_bwd")(dys_bf, u_bf, h, bd, cd, lamc8, start)


def _gelu_parts(x):
    c0 = math.sqrt(2.0 / math.pi)
    inner = c0 * (x + 0.044715 * x * x * x)
    th = jnp.tanh(inner)
    val = 0.5 * x * (1.0 + th)
    grad = 0.5 * (1.0 + th) + 0.5 * x * (1.0 - th * th) * c0 * (1.0 + 3.0 * 0.044715 * x * x)
    return val, grad


def _ssm_out(y_raw, u, d_skip, tm):
    t = u.shape[0]
    seg = t // N_DEV

    def body(y_ref, u_ref, d_ref, ys_ref, yg_ref, tmp):
        ys = y_ref[...] + d_ref[...] * u_ref[...]
        ys_ref[...] = ys
        yg = _gelu_parts(ys)[0]
        for n in range(SSM_W // BLK):
            tmp[n] = yg[:, n * BLK:(n + 1) * BLK]
            for j in range(N_DEV):
                yg_ref[j, :, n * BLK:(n + 1) * BLK] = tmp[n, pl.ds(j, tm // N_DEV, stride=N_DEV), :].astype(BF16)

    row = _bs((tm, SSM_W), lambda i: (i, 0))
    ys, yg = pl.pallas_call(
        body, grid=(t // tm,), in_specs=[row, row, _bs((1, SSM_W), lambda i: (0, 0))],
        out_specs=[row, _bs((N_DEV, tm // N_DEV, SSM_W), lambda i: (0, i, 0))],
        out_shape=[SDS((t, SSM_W), F32), SDS((N_DEV, seg, SSM_W), BF16)],
        scratch_shapes=[pltpu.VMEM((SSM_W // BLK, tm, BLK), F32)], compiler_params=_cp(1), name="ssm_out")(y_raw, u, d_skip)
    return ys, yg.reshape(t, SSM_W)


def _ssm_out_bwd(d_yg, ys, u, tm):
    t = u.shape[0]
    seg = t // N_DEV

    def body(dg_ref, ys_ref, u_ref, dys_ref, dysb_ref, dd_ref, tmp):
        for n in range(SSM_W // BLK):
            for j in range(N_DEV):
                tmp[n, pl.ds(j, tm // N_DEV, stride=N_DEV), :] = dg_ref[j, :, n * BLK:(n + 1) * BLK]
        dyg = jnp.concatenate([tmp[n] for n in range(SSM_W // BLK)], axis=1)
        dys = dyg * _gelu_parts(ys_ref[...])[1]
        dys_ref[...] = dys
        dysb_ref[...] = dys.astype(BF16)
        part = jnp.sum(dys * u_ref[...], axis=0, keepdims=True)

        @pl.when(pl.program_id(0) == 0)
        def _():
            dd_ref[...] = part

        @pl.when(pl.program_id(0) > 0)
        def _():
            dd_ref[...] += part

    row = _bs((tm, SSM_W), lambda i: (i, 0))
    return pl.pallas_call(
        body, grid=(t // tm,), in_specs=[_bs((N_DEV, tm // N_DEV, SSM_W), lambda i: (0, i, 0)), row, row],
        out_specs=[row, row, _bs((1, SSM_W), lambda i: (0, 0))],
        out_shape=[SDS((t, SSM_W), F32), SDS((t, SSM_W), BF16), SDS((1, SSM_W), F32)],
        scratch_shapes=[pltpu.VMEM((SSM_W // BLK, tm, BLK), F32)], compiler_params=_cp(1), name="ssm_out_bwd")(
            d_yg.reshape(N_DEV, seg, SSM_W), ys, u)


def _du_to_dz(du_raw, dys, d_skip, tm):
    t = du_raw.shape[0]
    seg = t // N_DEV

    def body(du_ref, dys_ref, d_ref, o_ref, tmp):
        du = du_ref[...] + d_ref[...] * dys_ref[...]
        for n in range(SSM_W // BLK):
            tmp[n] = du[:, n * BLK:(n + 1) * BLK]
            for j in range(N_DEV):
                o_ref[j, :, n * BLK:(n + 1) * BLK] = tmp[n, pl.ds(j, tm // N_DEV, stride=N_DEV), :].astype(BF16)

    row = _bs((tm, SSM_W), lambda i: (i, 0))
    out = pl.pallas_call(
        body, grid=(t // tm,), in_specs=[row, row, _bs((1, SSM_W), lambda i: (0, 0))],
        out_specs=_bs((N_DEV, tm // N_DEV, SSM_W), lambda i: (0, i, 0)), out_shape=SDS((N_DEV, seg, SSM_W), BF16),
        scratch_shapes=[pltpu.VMEM((SSM_W // BLK, tm, BLK), F32)], compiler_params=_cp(1), name="du_to_dz")(du_raw, dys, d_skip)
    return out.reshape(t, SSM_W)


def _block_diag(blocks):
    nb, ng, r, c = blocks.shape
    eye = jnp.eye(ng, dtype=blocks.dtype)
    return (blocks[:, :, :, None, :] * eye[None, :, None, :, None]).reshape(nb, ng * r, ng * c)


def _diag_blocks(full, r, c):
    k, nb = full.shape[:2]
    ng = full.shape[2] // r
    x = full.reshape(k, nb, ng, r, ng, c)
    eye = jnp.eye(ng, dtype=full.dtype)
    return jnp.sum(x * eye[None, None, :, None, :, None], axis=4).reshape(k, nb * ng, r, c)


_SMALL = ("g_mix", "a_re", "a_im", "log_dt", "b_re", "b_im", "c_re", "c_im", "d_skip", "g_ffn", "g_final")


def _pack_small(arrs):
    flat = jnp.concatenate([a.reshape(-1) for a in arrs])
    pad = (-flat.shape[0]) % (8 * 128)
    return jnp.pad(flat, (0, pad)).reshape(-1, 128)


def _unpack_small(packed, shapes):
    flat = packed.reshape(-1)
    out, off = [], 0
    for s in shapes:
        n = math.prod(s)
        out.append(flat[off:off + n].reshape(s))
        off += n
    return out


def kernel(x, p, positions, g_mix, w_in, a_re, a_im, log_dt, b_re, b_im, c_re, c_im, d_skip, w_attn_proj, w_glu_a, w_glu_b, w_out, g_ffn, w_ffn_gate, w_ffn_up, w_ffn_down, w_ple_gate, w_ple_proj, g_final, loss_target, m_g_mix, m_w_in, m_a_re, m_a_im, m_log_dt, m_b_re, m_b_im, m_c_re, m_c_im, m_d_skip, m_w_attn_proj, m_w_glu_a, m_w_glu_b, m_w_out, m_g_ffn, m_w_ffn_gate, m_w_ffn_up, m_w_ffn_down, m_w_ple_gate, m_w_ple_proj, m_g_final, v_g_mix, v_w_in, v_a_re, v_a_im, v_log_dt, v_b_re, v_b_im, v_c_re, v_c_im, v_d_skip, v_w_attn_proj, v_w_glu_a, v_w_glu_b, v_w_out, v_g_ffn, v_w_ffn_gate, v_w_ffn_up, v_w_ffn_down, v_w_ple_gate, v_w_ple_proj, v_g_final):
    args = dict(locals())
    t, d = x.shape[1], x.shape[2]
    inw = w_in.shape[2] * N_DEV
    fs = w_ffn_gate.shape[2]
    ff = fs * N_DEV
    ple = w_ple_proj.shape[1]
    seg = t // N_DEV
    assert inw == 3 * QK_W + SSM_W + 2 * d and t % (N_DEV * SCAN_ROWS // 8) == 0 and seg & (seg - 1) == 0
    tm = min(1024, t)
    te = min(512, t)
    tk = min(512, t)
    ucol = (3 * QK_W) // SSM_W
    gcol = (3 * QK_W + SSM_W) // d
    assert (3 * QK_W + SSM_W) % d == 0

    x2, p2, tgt = x[0], p[0, 0], loss_target[0]
    pos = positions.reshape(t, 1)
    inv = ROPE_THETA ** (-jnp.arange(ROPE_HALF, dtype=F32) * 2.0 / ROPE_DIM)
    invf = jnp.concatenate([inv, inv, jnp.zeros((HEAD_DIM - ROPE_DIM,), F32)]).reshape(1, HEAD_DIM)

    wnames = ("w_in", "w_attn_proj", "w_glu_a", "w_glu_b", "w_out", "w_ffn_gate", "w_ffn_up", "w_ffn_down", "w_ple_gate",
              "w_ple_proj")
    kinds = ("cols", "cols", "cols", "cols", "rows", "slot", "slot", "rows", "rows", "cols")
    shards = [args[n][0].astype(BF16) for n in wnames]
    sizes = [s.shape[0] if k == "rows" else s.shape[-1] for s, k in zip(shards, kinds)]
    ag = _exchange_start("gather_weights_start", shards, kinds, sizes, True)

    row_d = _bs((tm, d), lambda i, j, k: (i, 0))
    row_e = _bs((te, d), lambda i, j, k: (i, 0))
    vec_d = _bs((1, d), lambda i, j, k: (0, 0))
    sq_w = _bs((d, d), lambda i, j, k: (0, 0))
    n1 = _rms_fwd("norm_mix", x2, g_mix + ag[3][0:1, 0:1], tm)
    W_in, = _exchange_wait("gather_w_in_wait", ag, [0], kinds, sizes, True, n1)
    qkv = _mm("qkv_proj", (t // tm, 3, 1), [("nn", n1, row_d, W_in, _bs((d, QK_W), lambda i, j, k: (0, j)))],
              [(SDS((3, t // dil, dil * GROUP_W), BF16), _bs((None, tm // dil, dil * GROUP_W), lambda i, j, k: (j, i, 0)))
               for dil in DILATIONS],
              extras=[(pos, _bs((tm, 1), lambda i, j, k: (i, 0))), (invf, _bs((1, HEAD_DIM), lambda i, j, k: (0, 0)))],
              epilogue=_rope_dilate_epilogue(tm),
              scratch=[pltpu.VMEM((tm, HEAD_DIM), F32)] * 3 + [pltpu.VMEM((QK_W // HEAD_DIM, tm, HEAD_DIM), F32)])
    z_u, = _mm("u_proj", (t // tm, 1, 1), [("nn", n1, row_d, W_in, _bs((d, SSM_W), lambda i, j, k: (0, ucol)))],
               [(SDS((t, SSM_W), F32), _bs((tm, SSM_W), lambda i, j, k: (i, 0)))])
    zg, = _mm("z_gates", (t // tm, 2, 1),
              [("nn", n1, row_d, W_in, _bs((d, d), lambda i, j, k: (0, gcol + j)))],
              [(SDS((t, 2 * d), BF16), _bs((tm, d), lambda i, j, k: (i, j)))])

    outs, lses = [], []
    for g, dil in enumerate(DILATIONS):
        o_g, l_g = _attn_fwd(qkv[g], dil, min(512, t // dil))
        outs.append(o_g)
        lses.append(l_g)
    merged = _attn_merge(outs, lses, te)
    attn, attn_bf, lts = merged[0], merged[1], merged[2:]

    nsq = seg.bit_length() - 1
    bar_re, bar_im, z_re, z_im, pw_re, pw_im = _ssm_disc(a_re[0], a_im[0], log_dt.reshape(SSM_GROUPS, 1), nsq)
    gp = SSM_GROUPS * SSM_STATE
    b_re2, b_im2 = b_re.reshape(gp, SSM_GROUP), b_im.reshape(gp, SSM_GROUP)
    bb_re, bb_im = _ssm_scale_b(z_re.reshape(gp, 1), z_im.reshape(gp, 1), b_re2, b_im2)

    def chunks(a, r, c):
        return a.reshape(SSM_NB, SSM_GROUPS // SSM_NB, r, c)

    bbt = lambda a: jnp.swapaxes(a.reshape(SSM_GROUPS, SSM_STATE, SSM_GROUP), 1, 2)
    bd = jnp.concatenate([_block_diag(chunks(bbt(bb_re), SSM_GROUP, SSM_STATE)),
                          _block_diag(chunks(bbt(bb_im), SSM_GROUP, SSM_STATE))]).astype(BF16)
    ct = lambda a: jnp.swapaxes(a[0], 1, 2)
    cd = jnp.concatenate([_block_diag(chunks(ct(c_re), SSM_STATE, SSM_GROUP)),
                          _block_diag(chunks(-ct(c_im), SSM_STATE, SSM_GROUP))]).astype(BF16)
    lam = jnp.concatenate([bar_re.reshape(1, gp), bar_im.reshape(1, gp)], axis=1)
    lamc = jnp.concatenate([bar_re.reshape(1, gp), -bar_im.reshape(1, gp)], axis=1)
    pw = jnp.concatenate([pw_re.reshape(1, gp), pw_im.reshape(1, gp)], axis=1)
    pwc = jnp.concatenate([pw_re.reshape(1, gp), -pw_im.reshape(1, gp)], axis=1)
    lam8, lamc8 = jnp.broadcast_to(lam, (8, 2 * gp)), jnp.broadcast_to(lamc, (8, 2 * gp))

    u_perm, u_bf = _permute_u(z_u, 0, te)
    start_f = _ssm_carries("ssm_carries_fwd", u_bf, bd, "nn", lam8, pw, False)
    h_all, y_raw = _ssm_fwd(u_bf, bd, cd, lam8, start_f)
    dsk = d_skip.reshape(1, SSM_W)
    ys, yg_bf = _ssm_out(y_raw, u_perm, dsk, te)
    W_ap, W_ga, W_gb, W_out, W_fg, W_fu, W_fd, W_pg, W_pp = _exchange_wait(
        "gather_rest_wait", ag, list(range(1, len(wnames))), kinds, sizes, True, yg_bf)
    W_fg = jnp.swapaxes(W_fg, 0, 1).reshape(d, ff)
    W_fu = jnp.swapaxes(W_fu, 0, 1).reshape(d, ff)

    glu_w = _bs((SSM_W, d), lambda i, j, k: (0, 0))
    row_es = _bs((te, SSM_W), lambda i, j, k: (i, 0))
    gate_a = _bs((te, d), lambda i, j, k: (i, 0))
    gate_s = _bs((te, d), lambda i, j, k: (i, 1))
    td_f32, td_bf = SDS((t, d), F32), SDS((t, d), BF16)
    m_bf, ya, yb, attn_d = _mm(
        "glu_merge", (t // te, 1, 1),
        [("nn", yg_bf, row_es, W_ga, glu_w), ("nn", yg_bf, row_es, W_gb, glu_w), ("nn", attn_bf, row_es, W_ap, glu_w)],
        [(td_bf, row_e)] * 4, extras=[(zg, gate_a), (zg, gate_s)], epilogue=_glu_merge_epilogue)

    h1, n2 = _mm("out_proj", (t // tm, 1, 1), [("nn", m_bf, row_d, W_out, sq_w)], [(td_f32, row_d), (td_bf, row_d)],
                 extras=[(x2, row_d), (g_ffn, vec_d)], epilogue=_out_norm_epilogue)

    tn_f = ff // 2
    nf = ff // tn_f
    hid_o = _bs((te, tn_f), lambda j, i, k: (i, j))
    tf_bf = SDS((t, ff), BF16)
    a_rows = _bs((te, d), lambda j, i, k: (i, 0))
    w_cols = _bs((d, tn_f), lambda j, i, k: (0, j))
    act, fg, fu = _mm("ffn_gate_up", (nf, t // te, 1), [("nn", n2, a_rows, W_fg, w_cols), ("nn", n2, a_rows, W_fu, w_cols)],
                      [(tf_bf, hid_o)] * 3, epilogue=_swiglu_epilogue)
    h2, h2_bf = _mm("ffn_down", (t // tm, 1, nf),
                    [("nn", act, _bs((tm, tn_f), lambda i, j, k: (i, k)), W_fd, _bs((tn_f, d), lambda i, j, k: (k, 0)))],
                    [(td_f32, row_d), (td_bf, row_d)], extras=[(h1, row_d)])

    loss_part, dg_final, dh3, dpp_bf, dpg_bf = _mm(
        "ple_head", (t // te, 1, 1),
        [("nn", h2_bf, row_e, W_pg, sq_w), ("nn", p2, _bs((te, ple), lambda i, j, k: (i, 0)), W_pp, _bs((ple, d), lambda i, j, k: (0, 0)))],
        [(SDS((1, 1), F32), _bs((1, 1), lambda i, j, k: (0, 0))), (SDS((1, d), F32), vec_d), (td_f32, row_e), (td_bf, row_e),
         (td_bf, row_e)],
        extras=[(h2, row_e), (g_final.reshape(1, d), vec_d), (tgt, row_e)], epilogue=_head_epilogue(t // te),
        scratch=[pltpu.VMEM((1, d), F32)])
    loss = lax.psum(loss_part[0, 0], ("x", "y", "c"))

    nkt = t // tk
    tok_a = lambda w: _bs((tk, w), lambda i, j, k: (k, 0))

    def wgrad(name, a, wa, b, wb):
        return _mm(name, (1, 1, nkt), [("tn", a, tok_a(wa), b, tok_a(wb))],
                   [(SDS((wa, wb), BF16), _bs((wa, wb), lambda i, j, k: (0, 0)))])[0]

    dW_pp = wgrad("dw_ple_proj", p2, ple, dpp_bf, d)
    dW_pg = wgrad("dw_ple_gate", h2_bf, d, dpg_bf, d)
    dh2, dh2_bf = _mm("d_ple_gate", (t // tm, 1, 1), [("nt", dpg_bf, row_d, W_pg, sq_w)], [(td_f32, row_d), (td_bf, row_d)],
                      extras=[(dh3, row_d)])

    dfg_bf, dfu_bf = _mm("d_ffn_down", (nf, t // te, 1),
                         [("nt", dh2_bf, a_rows, W_fd, _bs((tn_f, d), lambda j, i, k: (j, 0)))],
                         [(tf_bf, hid_o), (tf_bf, hid_o)], extras=[(fg, hid_o), (fu, hid_o)], epilogue=_swiglu_bwd_epilogue)
    dW_fd, = _mm("dw_ffn_down", (nf, 1, nkt), [("tn", act, _bs((tk, tn_f), lambda i, j, k: (k, i)), dh2_bf, tok_a(d))],
                 [(SDS((ff, d), BF16), _bs((tn_f, d), lambda i, j, k: (i, 0)))])
    hid_t = _bs((tk, tn_f), lambda i, j, k: (k, j))
    wg_o = [(SDS((d, ff), BF16), _bs((d, tn_f), lambda i, j, k: (0, j)))]
    dW_fg, = _mm("dw_ffn_gate", (1, nf, nkt), [("tn", n2, tok_a(d), dfg_bf, hid_t)], wg_o)
    dW_fu, = _mm("dw_ffn_up", (1, nf, nkt), [("tn", n2, tok_a(d), dfu_bf, hid_t)], wg_o)
    dW_fg = jnp.swapaxes(dW_fg.reshape(d, N_DEV, fs), 0, 1)
    dW_fu = jnp.swapaxes(dW_fu.reshape(d, N_DEV, fs), 0, 1)
    group = lambda names: ([kinds[wnames.index(n)] for n in names], [sizes[wnames.index(n)] for n in names])
    ffn_names = ("w_ffn_gate", "w_ffn_up", "w_ffn_down", "w_ple_gate", "w_ple_proj")
    rs_ffn = _exchange_start("scatter_ffn_start", [dW_fg, dW_fu, dW_fd, dW_pg, dW_pp], *group(ffn_names), False)
    hid_all = _bs((te, ff), lambda i, j, k: (i, 0))
    w_all = pl.BlockSpec((d, ff), lambda i, j, k: (0, 0), pipeline_mode=pl.Buffered(1))
    dh1, dh1_bf, dg_ffn = _mm("d_ffn_gate_up", (t // te, 1, 1),
                              [("nt", dfg_bf, hid_all, W_fg, w_all), ("nt", dfu_bf, hid_all, W_fu, w_all)],
                              [(td_f32, row_e), (td_bf, row_e), (SDS((1, d), F32), vec_d)],
                              extras=[(h1, row_e), (g_ffn, vec_d), (dh2, row_e)], epilogue=_rms_bwd_epilogue, after=rs_ffn[3])

    dW_out = wgrad("dw_out", m_bf, d, dh1_bf, d)
    dz_g, dad_bf, dya_bf, dyb_bf = _mm(
        "d_out_proj", (t // te, 1, 1), [("nt", dh1_bf, row_e, W_out, sq_w)],
        [(SDS((t, 2 * d), BF16), _bs((te, 2 * d), lambda i, j, k: (i, 0))), (td_bf, row_e), (td_bf, row_e), (td_bf, row_e)],
        extras=[(zg, gate_a), (zg, gate_s), (attn_d, row_e), (ya, row_e), (yb, row_e)], epilogue=_merge_bwd_epilogue)

    row_s = _bs((tm, SSM_W), lambda i, j, k: (i, 0))
    d_yg, = _mm("d_glu", (t // tm, 1, 1), [("nt", dya_bf, row_d, W_ga, glu_w), ("nt", dyb_bf, row_d, W_gb, glu_w)],
                [(SDS((t, SSM_W), F32), row_s)])
    dW_ga = wgrad("dw_glu_a", yg_bf, SSM_W, dya_bf, d)
    dW_gb = wgrad("dw_glu_b", yg_bf, SSM_W, dyb_bf, d)
    dys, dys_bf, dd_skip = _ssm_out_bwd(d_yg, ys, u_perm, te)
    start_b = _ssm_carries("ssm_carries_bwd", dys_bf, cd, "nt", lamc8, pwc, True)
    du_raw, dlam8, dbd, dcd = _ssm_bwd(dys_bf, u_bf, h_all, bd, cd, lamc8, start_b)
    dz_u = _du_to_dz(du_raw, dys, dsk, te)
    dlam = jnp.sum(dlam8, axis=0)
    dbb = _diag_blocks(dbd.reshape(2, SSM_NB, BLK, 512), SSM_GROUP, SSM_STATE)
    dbb_re = jnp.swapaxes(dbb[0], 1, 2).reshape(gp, SSM_GROUP)
    dbb_im = jnp.swapaxes(dbb[1], 1, 2).reshape(gp, SSM_GROUP)
    dcc = _diag_blocks(dcd.reshape(2, SSM_NB, 512, BLK), SSM_STATE, SSM_GROUP)
    dc_re, dc_im = jnp.swapaxes(dcc[0], 1, 2), -jnp.swapaxes(dcc[1], 1, 2)
    db_re, db_im, dz_re, dz_im = _ssm_scale_b_bwd(z_re.reshape(gp, 1), z_im.reshape(gp, 1), b_re2, b_im2, dbb_re, dbb_im)
    gshape = (SSM_GROUPS, SSM_STATE)
    da_re, da_im, dlog_dt = _ssm_disc_bwd(a_re[0], a_im[0], log_dt.reshape(SSM_GROUPS, 1), dlam[:gp].reshape(gshape),
                                          dlam[gp:].reshape(gshape), dz_re.reshape(gshape), dz_im.reshape(gshape))

    d_attn, = _mm("d_attn_proj", (t // tm, 1, 1), [("nt", dad_bf, row_d, W_ap, glu_w)], [(SDS((t, GROUP_W), F32), row_s)])
    dW_ap = wgrad("dw_attn_proj", attn_bf, GROUP_W, dad_bf, d)
    pre = _attn_bwd_pre(d_attn, attn, te)
    das, deltas = pre[:N_GROUPS], pre[N_GROUPS:]
    dqkvs = [_attn_bwd(qkv[g], das[g], lts[g], deltas[g], dil, min(512, t // dil)) for g, dil in enumerate(DILATIONS)]
    dz_qkv = _undilate_rope_bwd(dqkvs, pos, invf, tm)

    dW_in, = _mm("dw_in_qkv", (1, 3, nkt), [("tn", n1, tok_a(d), dz_qkv, _bs((tk, QK_W), lambda i, j, k: (k, j)))],
                 [(SDS((d, inw), BF16), _bs((d, QK_W), lambda i, j, k: (0, j)))])
    dW_in, = _mm("dw_in_u", (1, 1, nkt), [("tn", n1, tok_a(d), dz_u, tok_a(SSM_W))],
                 [(SDS((d, inw), BF16), _bs((d, SSM_W), lambda i, j, k: (0, ucol)))], alias_to_out0=dW_in)
    dW_in, = _mm("dw_in_gates", (1, 2, nkt), [("tn", n1, tok_a(d), dz_g, _bs((tk, d), lambda i, j, k: (k, j)))],
                 [(SDS((d, inw), BF16), _bs((d, d), lambda i, j, k: (0, gcol + j)))], alias_to_out0=dW_in)
    rest_names = ("w_in", "w_attn_proj", "w_glu_a", "w_glu_b", "w_out")
    rs_in = _exchange_start("scatter_rest_start", [dW_in, dW_ap, dW_ga, dW_gb, dW_out], *group(rest_names), False)
    w_piece = lambda w, cb: pl.BlockSpec((d, w), lambda i, j, k: (0, cb), pipeline_mode=pl.Buffered(1))
    dx, dg_mix = _mm(
        "d_z_proj", (t // te, 1, 1),
        [("nt", dz_qkv, _bs((te, 3 * QK_W), lambda i, j, k: (i, 0)), W_in, w_piece(3 * QK_W, 0)),
         ("nt", dz_u, _bs((te, SSM_W), lambda i, j, k: (i, 0)), W_in, w_piece(SSM_W, ucol)),
         ("nt", dz_g, _bs((te, d), lambda i, j, k: (i, 0)), W_in, w_piece(d, gcol)),
         ("nt", dz_g, _bs((te, d), lambda i, j, k: (i, 1)), W_in, w_piece(d, gcol + 1))],
        [(td_f32, row_e), (SDS((1, d), F32), vec_d)],
        extras=[(x2, row_e), (g_mix, vec_d), (dh1, row_e)], epilogue=_rms_bwd_epilogue, after=rs_in[3])

    small_parts = dict(g_mix=dg_mix, a_re=da_re, a_im=da_im, log_dt=dlog_dt, b_re=db_re, b_im=db_im, c_re=dc_re, c_im=dc_im,
                       d_skip=dd_skip, g_ffn=dg_ffn, g_final=dg_final)
    small = _pack_small([small_parts[n] for n in _SMALL])
    received = {}
    for names, started, label in ((ffn_names, rs_ffn, "ffn"), (rest_names, rs_in, "rest")):
        landed = _exchange_wait(f"scatter_{label}_wait", started, list(range(len(names))), *group(names), False, dx)
        received.update(zip(names, landed))

    new = {}
    for n in wnames:
        new[n] = [o.reshape(args[n].shape)
                  for o in _adamw("adamw_" + n, received[n], args[n][0], args["m_" + n][0], args["v_" + n][0])]
    pk = lambda pre: _pack_small([args[pre + n] for n in _SMALL])
    sm = _adamw("adamw_small", _gather_small(small), pk(""), pk("m_"), pk("v_"))
    shapes = [args[n].shape for n in _SMALL]
    for n, vals in zip(_SMALL, zip(*[_unpack_small(o, shapes) for o in sm])):
        new[n] = list(vals)

    order = ("g_mix", "w_in", "a_re", "a_im", "log_dt", "b_re", "b_im", "c_re", "c_im", "d_skip", "w_attn_proj", "w_glu_a",
             "w_glu_b", "w_out", "g_ffn", "w_ffn_gate", "w_ffn_up", "w_ffn_down", "w_ple_gate", "w_ple_proj", "g_final")
    return (loss, dx.reshape(x.shape), *[new[n][0] for n in order], *[new[n][1] for n in order],
            *[new[n][2] for n in order], *[new[n][3] for n in order])
```

```python
import functools
import math

import jax
import jax.numpy as jnp
from jax import lax
from jax.experimental import pallas as pl
from jax.experimental.pallas import tpu as pltpu

F32 = jnp.float32
BF16 = jnp.bfloat16
SDS = jax.ShapeDtypeStruct

N_DEV = 8
HEAD_DIM = 128
HEADS_PER_GROUP = 4
GROUP_W = HEADS_PER_GROUP * HEAD_DIM
DILATIONS = (1, 4, 16)
N_GROUPS = len(DILATIONS)
QK_W = N_GROUPS * GROUP_W
BLK = 128
ROPE_THETA = 500000.0
ROPE_DIM = HEAD_DIM // 4
ROPE_HALF = ROPE_DIM // 2
SSM_W = 512
SSM_GROUP = 16
SSM_GROUPS = SSM_W // SSM_GROUP
SSM_STATE = 64
NSTATE = SSM_GROUPS * SSM_STATE
SSM_NB = 4
EPS = 1e-6
ADAM_LR, ADAM_B1, ADAM_B2, ADAM_EPS, ADAM_WD, ADAM_STEP = 0.001, 0.9, 0.999, 1e-08, 0.01, 10
NEG = -1e30

VMEM_LIMIT = 52 * 1024 * 1024
SCAN_ROWS = 512
SCAN_LANES = 512


def _cp(n):
    return pltpu.CompilerParams(dimension_semantics=("arbitrary",) * n, vmem_limit_bytes=VMEM_LIMIT)


def _sigmoid(x):
    return 0.5 * jnp.tanh(0.5 * x) + 0.5


_DNUMS = {"nn": (((1,), (0,)), ((), ())), "nt": (((1,), (1,)), ((), ())), "tn": (((0,), (0,)), ((), ()))}


def _bs(shape, fn):
    return pl.BlockSpec(shape, fn)


def _store_all(prods, extra_refs, out_refs, scratch_refs):
    r = prods[0]
    for p in prods[1:]:
        r = r + p
    for e in extra_refs:
        r = r + e[...]
    for o in out_refs:
        o[...] = r.astype(o.dtype)


def _mm(name, grid, pairs, outs, extras=(), epilogue=_store_all, scratch=(), alias_to_out0=None, after=None):
    nk = grid[2]
    npair = len(pairs)
    steps = [p[5] if len(p) > 5 else nk for p in pairs]

    def block(spec):
        return tuple(s for s in spec.block_shape if s is not None)

    acc_shapes = [jax.eval_shape(lambda u, v, dn=_DNUMS[p[0]]: lax.dot_general(u, v, dn, preferred_element_type=F32),
                                 SDS(block(p[2]), BF16), SDS(block(p[4]), BF16)).shape for p in pairs]
    if nk == 1:
        acc_shapes = []
    n_in = 2 * npair + len(extras) + (alias_to_out0 is not None) + (after is not None)

    def body(*refs):
        extra_refs = refs[2 * npair:2 * npair + len(extras)]
        out_refs = refs[n_in:n_in + len(outs)]
        rest = refs[n_in + len(outs):]
        acc_refs = rest[:len(acc_shapes)]
        scratch_refs = rest[len(acc_refs):]
        k = pl.program_id(2)

        def product(i):
            return lax.dot_general(refs[2 * i][...].astype(BF16), refs[2 * i + 1][...].astype(BF16), _DNUMS[pairs[i][0]],
                                   preferred_element_type=F32)

        if nk == 1:
            epilogue([product(i) for i in range(npair)], extra_refs, out_refs, scratch_refs)
            return
        for i in range(npair):
            @pl.when(k == 0)
            def _(i=i):
                acc_refs[i][...] = product(i)

            @pl.when((k > 0) & (k < steps[i]))
            def _(i=i):
                acc_refs[i][...] += product(i)

        @pl.when(k == nk - 1)
        def _():
            epilogue([a[...] for a in acc_refs], extra_refs, out_refs, scratch_refs)

    ins, in_specs = [], []
    for p in pairs:
        ins += [p[1], p[3]]
        in_specs += [p[2], p[4]]
    ins += [e[0] for e in extras]
    in_specs += [e[1] for e in extras]
    aliases = {}
    if alias_to_out0 is not None:
        aliases = {len(ins): 0}
        ins.append(alias_to_out0)
        in_specs.append(pl.BlockSpec(memory_space=pl.ANY))
    if after is not None:
        ins.append(after)
        in_specs.append(pl.BlockSpec(memory_space=pl.ANY))
    scratch_shapes = [pltpu.VMEM(s, F32) for s in acc_shapes] + list(scratch)
    return pl.pallas_call(body, grid=grid, in_specs=in_specs, out_specs=[o[1] for o in outs], out_shape=[o[0] for o in outs],
                          scratch_shapes=scratch_shapes, input_output_aliases=aliases, compiler_params=_cp(3), name=name)(*ins)


def _my_index():
    return 4 * lax.axis_index("x") + 2 * lax.axis_index("y") + lax.axis_index("c")


def _peer(d):
    mx, my, mc = lax.axis_index("x"), lax.axis_index("y"), lax.axis_index("c")
    return (mx ^ ((d >> 2) & 1), my ^ ((d >> 1) & 1), mc ^ (d & 1))


def _win(ref, kind, j, n):
    if kind == "all":
        return ref
    if kind == "slot":
        return ref.at[j]
    if kind == "rows":
        return ref.at[pl.ds(pl.multiple_of(j * n, 8), n)]
    return ref.at[:, pl.ds(pl.multiple_of(j * n, 128), n)]


def _win7(ref, kind, n):
    if kind == "slot":
        return ref.at[pl.ds(0, 7)]
    if kind == "rows":
        return ref.at[pl.ds(0, 7 * n)]
    return ref.at[:, pl.ds(0, 7 * n)]


def _full_shape(shard_shape, kind):
    if kind == "slot":
        return (N_DEV,) + tuple(shard_shape)
    if kind == "rows":
        return (N_DEV * shard_shape[0],) + tuple(shard_shape[1:])
    return (shard_shape[0], N_DEV * shard_shape[1])


def _shard_shape(full_shape, kind, n):
    if kind == "all":
        return tuple(full_shape)
    if kind == "slot":
        return tuple(full_shape[1:])
    if kind == "rows":
        return (n,) + tuple(full_shape[1:])
    return (full_shape[0], n)


_HBM = pl.BlockSpec(memory_space=pltpu.HBM)
_SEM = pl.BlockSpec(memory_space=pltpu.SEMAPHORE)
_DATAFLOW = pltpu.SideEffectType.DATAFLOW_SIDE_EFFECTING


def _exchange_start(name, srcs, kinds, sizes, gather):
    n = len(srcs)
    if gather:
        lands = [lax.empty(_full_shape(s.shape, k), s.dtype) for s, k in zip(srcs, kinds)]
    else:
        lands = [lax.empty((N_DEV,) + _shard_shape(s.shape, k, z), s.dtype) for s, k, z in zip(srcs, kinds, sizes)]

    def body(*refs):
        src, land = refs[:n], refs[n:2 * n]
        send_sems, recv_sems, local_sems = refs[2 * n], refs[2 * n + 1], refs[2 * n + 2]
        token = refs[4 * n + 3]
        me = _my_index()
        for a in range(n):
            _local_copy(src[a], land[a], kinds[a], sizes[a], gather, me, local_sems.at[a]).start()
        for a in range(n):
            for d in range(1, N_DEV):
                px, py, pc = _peer(d)
                if gather:
                    s_ref, d_ref = src[a], _win(land[a], kinds[a], me, sizes[a])
                else:
                    s_ref, d_ref = _win(src[a], kinds[a], 4 * px + 2 * py + pc, sizes[a]), land[a].at[me]
                pltpu.make_async_remote_copy(src_ref=s_ref, dst_ref=d_ref, send_sem=send_sems.at[a], recv_sem=recv_sems.at[a],
                                             device_id=(px, py, pc), device_id_type=pl.DeviceIdType.MESH).start()
        token[...] = jnp.zeros_like(token)

    hbm = [pltpu.with_memory_space_constraint(a, pltpu.HBM) for a in list(srcs) + lands]
    out = pl.pallas_call(
        body, name=name, in_specs=[_HBM] * (2 * n),
        out_shape=[pltpu.SemaphoreType.DMA((n,))] * 3 + [pltpu.HBM(a.shape, a.dtype) for a in hbm] + [SDS((8, 128), F32)],
        out_specs=[_SEM] * 3 + [_HBM] * (2 * n) + [pl.BlockSpec(memory_space=pltpu.VMEM)],
        input_output_aliases={i: 3 + i for i in range(2 * n)},
        compiler_params=pltpu.CompilerParams(has_side_effects=_DATAFLOW))(*hbm)
    return out[0:3], out[3:3 + n], out[3 + n:3 + 2 * n], out[-1]


def _local_copy(src, land, kind, size, gather, me, sem):
    if gather:
        return pltpu.make_async_copy(src, _win(land, kind, me, size), sem)
    return pltpu.make_async_copy(_win(src, kind, me, size), land.at[me], sem)


def _exchange_wait(name, started, which, kinds, sizes, gather, after):
    sems, srcs, lands, _ = started
    n = len(which)

    def body(*refs):
        src, land = refs[:n], refs[n:2 * n]
        send_ref, recv_ref, local_ref = refs[2 * n:2 * n + 3]
        me = _my_index()
        my_id = (lax.axis_index("x"), lax.axis_index("y"), lax.axis_index("c"))
        for i, a in enumerate(which):
            seven = _win7(land[i], kinds[a], sizes[a]) if gather else land[i].at[pl.ds(0, 7)]
            pltpu.make_async_remote_copy(src_ref=seven, dst_ref=seven, send_sem=send_ref.at[a], recv_sem=recv_ref.at[a],
                                         device_id=my_id, device_id_type=pl.DeviceIdType.MESH).wait()
            _local_copy(src[i], land[i], kinds[a], sizes[a], gather, me, local_ref.at[a]).wait()

    hbm = [srcs[a] for a in which] + [lands[a] for a in which]
    out = pl.pallas_call(
        body, name=name, in_specs=[_HBM] * (2 * n) + [_SEM] * 3 + [pl.BlockSpec(memory_space=pl.ANY)],
        out_shape=[pltpu.HBM(a.shape, a.dtype) for a in hbm], out_specs=[_HBM] * (2 * n),
        input_output_aliases={i: i for i in range(2 * n)},
        compiler_params=pltpu.CompilerParams(has_side_effects=_DATAFLOW))(*hbm, *sems, after)
    return out[n:]


def _gather_small(small):
    def body(in_ref, out_ref, send_sem, recv_sem, local_sem):
        me = _my_index()
        my_id = (lax.axis_index("x"), lax.axis_index("y"), lax.axis_index("c"))
        cp = pltpu.make_async_copy(in_ref, out_ref.at[me], local_sem)
        cp.start()
        for d in range(1, N_DEV):
            pltpu.make_async_remote_copy(src_ref=in_ref, dst_ref=out_ref.at[me], send_sem=send_sem, recv_sem=recv_sem,
                                         device_id=_peer(d), device_id_type=pl.DeviceIdType.MESH).start()
        seven = out_ref.at[pl.ds(0, 7)]
        pltpu.make_async_remote_copy(src_ref=seven, dst_ref=seven, send_sem=send_sem, recv_sem=recv_sem, device_id=my_id,
                                     device_id_type=pl.DeviceIdType.MESH).wait()
        cp.wait()

    any_spec = pl.BlockSpec(memory_space=pl.ANY)
    return pl.pallas_call(body, in_specs=[any_spec], out_specs=any_spec, out_shape=SDS((N_DEV,) + small.shape, F32),
                          scratch_shapes=[pltpu.SemaphoreType.DMA] * 3, name="gather_small")(small)


def _adamw(name, recv, w, m, v):
    rows, cols = w.shape
    tr = max(c for c in range(16, 257, 16) if rows % c == 0) if rows % 16 == 0 else rows

    def body(r_ref, w_ref, m_ref, v_ref, g_ref, d_ref, nm_ref, nv_ref):
        g = r_ref[0].astype(F32)
        for s in range(1, N_DEV):
            g = g + r_ref[s].astype(F32)
        nm = ADAM_B1 * m_ref[...] + (1.0 - ADAM_B1) * g
        nv = ADAM_B2 * v_ref[...] + (1.0 - ADAM_B2) * (g * g)
        m_hat = nm / (1.0 - ADAM_B1 ** ADAM_STEP)
        v_hat = nv / (1.0 - ADAM_B2 ** ADAM_STEP)
        g_ref[...] = g
        d_ref[...] = -ADAM_LR * (m_hat / (jnp.sqrt(v_hat) + ADAM_EPS) + ADAM_WD * w_ref[...])
        nm_ref[...] = nm
        nv_ref[...] = nv

    blk = _bs((tr, cols), lambda i: (i, 0))
    return pl.pallas_call(
        body, grid=(rows // tr,), in_specs=[_bs((N_DEV, tr, cols), lambda i: (0, i, 0)), blk, blk, blk],
        out_specs=[blk] * 4, out_shape=[SDS((rows, cols), F32)] * 4, compiler_params=_cp(1), name=name)(recv, w, m, v)


def _rms_fwd(name, x, g, tm):
    t, d = x.shape

    def body(x_ref, g_ref, n_ref):
        xv = x_ref[...]
        r = lax.rsqrt(jnp.mean(xv * xv, axis=-1, keepdims=True) + EPS)
        n_ref[...] = (xv * r * g_ref[...]).astype(BF16)

    return pl.pallas_call(body, grid=(t // tm,), in_specs=[_bs((tm, d), lambda i: (i, 0)), _bs((1, d), lambda i: (0, 0))],
                          out_specs=_bs((tm, d), lambda i: (i, 0)), out_shape=SDS((t, d), BF16), compiler_params=_cp(1),
                          name=name)(x, g)


def _accumulate_rows(ref, part):
    @pl.when(pl.program_id(0) == 0)
    def _():
        ref[...] = part

    @pl.when(pl.program_id(0) > 0)
    def _():
        ref[...] += part


def _rms_bwd_epilogue(prods, extra_refs, out_refs, scratch_refs):
    dyv = prods[0]
    for p in prods[1:]:
        dyv = dyv + p
    if len(extra_refs) > 3:
        dyv = dyv + extra_refs[3][...]
    xv = extra_refs[0][...]
    r = lax.rsqrt(jnp.mean(xv * xv, axis=-1, keepdims=True) + EPS)
    xh = xv * r
    dxh = dyv * extra_refs[1][...]
    dx = extra_refs[2][...] + r * (dxh - xh * jnp.mean(dxh * xh, axis=-1, keepdims=True))
    for o in out_refs[:-1]:
        o[...] = dx.astype(o.dtype)
    _accumulate_rows(out_refs[-1], jnp.sum(dyv * xh, axis=0, keepdims=True))


def _out_norm_epilogue(prods, extra_refs, out_refs, scratch_refs):
    h = prods[0] + extra_refs[0][...]
    r = lax.rsqrt(jnp.mean(h * h, axis=-1, keepdims=True) + EPS)
    out_refs[0][...] = h
    out_refs[1][...] = (h * r * extra_refs[1][...]).astype(BF16)


def _glu_merge_epilogue(prods, extra_refs, out_refs, scratch_refs):
    ya, yb, ad = prods
    ga, gs = extra_refs[0][...].astype(F32), extra_refs[1][...].astype(F32)
    m = _sigmoid(ga) * ad + _sigmoid(gs) * (ya * _sigmoid(yb))
    out_refs[0][...] = m.astype(BF16)
    for o, val in zip(out_refs[1:], (ya, yb, ad)):
        o[...] = val.astype(o.dtype)


def _merge_bwd_epilogue(prods, extra_refs, out_refs, scratch_refs):
    dmv = prods[0]
    d = dmv.shape[1]
    ga, gs = _sigmoid(extra_refs[0][...].astype(F32)), _sigmoid(extra_refs[1][...].astype(F32))
    adv, yav = extra_refs[2][...].astype(F32), extra_refs[3][...].astype(F32)
    sb = _sigmoid(extra_refs[4][...].astype(F32))
    out_refs[0][:, 0:d] = (dmv * adv * ga * (1.0 - ga)).astype(BF16)
    out_refs[0][:, d:2 * d] = (dmv * (yav * sb) * gs * (1.0 - gs)).astype(BF16)
    out_refs[1][...] = (dmv * ga).astype(BF16)
    dsd = dmv * gs
    out_refs[2][...] = (dsd * sb).astype(BF16)
    out_refs[3][...] = (dsd * yav * sb * (1.0 - sb)).astype(BF16)


def _swiglu_epilogue(prods, extra_refs, out_refs, scratch_refs):
    gv, uv = prods
    out_refs[0][...] = (gv * _sigmoid(gv) * uv).astype(BF16)
    out_refs[1][...] = gv.astype(out_refs[1].dtype)
    out_refs[2][...] = uv.astype(out_refs[2].dtype)


def _swiglu_bwd_epilogue(prods, extra_refs, out_refs, scratch_refs):
    dav = prods[0]
    gv, uv = extra_refs[0][...].astype(F32), extra_refs[1][...].astype(F32)
    sg = _sigmoid(gv)
    out_refs[0][...] = (dav * uv * sg * (1.0 + gv * (1.0 - sg))).astype(BF16)
    out_refs[1][...] = (dav * gv * sg).astype(BF16)


def _head_epilogue(n_tiles):
    def epilogue(prods, extra_refs, out_refs, scratch_refs):
        pgv, ppv = prods
        d = pgv.shape[1]
        lacc = scratch_refs[0]
        sg = _sigmoid(pgv)
        h3 = extra_refs[0][...] + sg * ppv
        r = lax.rsqrt(jnp.mean(h3 * h3, axis=-1, keepdims=True) + EPS)
        xh = h3 * r
        gv = extra_refs[1][...]
        diff = xh * gv - extra_refs[2][...]
        dout = diff * (1.0 / d)
        dxh = dout * gv
        dh3 = r * (dxh - xh * jnp.mean(dxh * xh, axis=-1, keepdims=True))
        out_refs[2][...] = dh3
        out_refs[3][...] = (dh3 * sg).astype(BF16)
        out_refs[4][...] = (dh3 * ppv * sg * (1.0 - sg)).astype(BF16)
        _accumulate_rows(out_refs[1], jnp.sum(dout * xh, axis=0, keepdims=True))
        _accumulate_rows(lacc, jnp.sum(diff * diff, axis=0, keepdims=True))

        @pl.when(pl.program_id(0) == n_tiles - 1)
        def _():
            out_refs[0][...] = (0.5 / d) * jnp.sum(lacc[...], axis=-1, keepdims=True)

    return epilogue


def _strided(r, n, d):
    return pl.ds(r, n, stride=d) if d > 1 else pl.ds(0, n)


def _rope_tables(pos_ref, invf_ref, c_s, s1_s, s2_s):
    ang = pos_ref[...].astype(F32) * invf_ref[...]
    lane = lax.broadcasted_iota(jnp.int32, ang.shape, 1)
    sn = jnp.sin(ang)
    c_s[...] = jnp.where(lane < ROPE_DIM, jnp.cos(ang), 1.0)
    s1_s[...] = jnp.where(lane < ROPE_HALF, -sn, 0.0)
    s2_s[...] = jnp.where((lane >= ROPE_HALF) & (lane < ROPE_DIM), sn, 0.0)


def _rope_dilate_epilogue(tm):
    def epilogue(prods, extra_refs, out_refs, scratch_refs):
        zv = prods[0]
        pos_ref, invf_ref = extra_refs
        c_s, s1_s, s2_s, rot = scratch_refs
        c = pl.program_id(1)

        @pl.when(c == 0)
        def _():
            _rope_tables(pos_ref, invf_ref, c_s, s1_s, s2_s)

        @pl.when(c < 2)
        def _():
            cc, s1, s2 = c_s[...], s1_s[...], s2_s[...]
            for h in range(QK_W // HEAD_DIM):
                xv = zv[:, h * HEAD_DIM:(h + 1) * HEAD_DIM]
                rot[h] = xv * cc + pltpu.roll(xv, HEAD_DIM - ROPE_HALF, 1) * s1 + pltpu.roll(xv, ROPE_HALF, 1) * s2

        @pl.when(c == 2)
        def _():
            for h in range(QK_W // HEAD_DIM):
                rot[h] = zv[:, h * HEAD_DIM:(h + 1) * HEAD_DIM]

        for g, (d, o_ref) in enumerate(zip(DILATIONS, out_refs)):
            n = tm // d
            for r in range(d):
                for hh in range(HEADS_PER_GROUP):
                    oc = r * GROUP_W + hh * HEAD_DIM
                    o_ref[:, oc:oc + HEAD_DIM] = rot[g * HEADS_PER_GROUP + hh, _strided(r, n, d), :].astype(BF16)

    return epilogue


def _band_mask(first):
    qi = lax.broadcasted_iota(jnp.int32, (BLK, 2 * BLK), 0)
    kj = lax.broadcasted_iota(jnp.int32, (BLK, 2 * BLK), 1)
    return (kj >= qi) & (kj <= qi + BLK) & ((kj >= BLK) | jnp.logical_not(first))


def _attn_fwd(qkv, d, qt):
    ell = qkv.shape[1]
    nsub = qt // BLK
    scale = 1.0 / math.sqrt(HEAD_DIM)

    def body(q_ref, kc_ref, kp_ref, vc_ref, vp_ref, o_ref, lse_ref, kcat, vcat):
        nb = pl.program_id(1)
        kcat[0:BLK, :] = kp_ref[...]
        kcat[BLK:, :] = kc_ref[...]
        vcat[0:BLK, :] = vp_ref[...]
        vcat[BLK:, :] = vc_ref[...]
        lane = lax.broadcasted_iota(jnp.int32, (BLK, HEAD_DIM), 1)
        for b in range(nsub):
            valid = _band_mask((nb == 0) if b == 0 else False)
            lse_t = jnp.zeros((BLK, HEAD_DIM), F32)
            for hh in range(HEADS_PER_GROUP):
                cs = slice(hh * HEAD_DIM, (hh + 1) * HEAD_DIM)
                qb = q_ref[b * BLK:(b + 1) * BLK, cs]
                kk = kcat[b * BLK:(b + 2) * BLK, cs]
                vv = vcat[b * BLK:(b + 2) * BLK, cs]
                s = lax.dot_general(qb, kk, _DNUMS["nt"], preferred_element_type=F32) * scale
                s = jnp.where(valid, s, NEG)
                mx = jnp.max(s, axis=-1, keepdims=True)
                p = jnp.exp(s - mx)
                den = jnp.sum(p, axis=-1, keepdims=True)
                o = jnp.dot(p.astype(BF16), vv, preferred_element_type=F32) / den
                o_ref[b * BLK:(b + 1) * BLK, cs] = o
                lse_t = jnp.where(lane == hh, mx + jnp.log(den), lse_t)
            lse_ref[b * BLK:(b + 1) * BLK, :] = lse_t

    cur = lambda c: _bs((None, qt, GROUP_W), lambda r, nb: (c, nb, r))
    prev = lambda c: _bs((None, BLK, GROUP_W), lambda r, nb: (c, jnp.maximum(nb * nsub - 1, 0), r))
    return pl.pallas_call(
        body, grid=(d, ell // qt), in_specs=[cur(0), cur(1), prev(1), cur(2), prev(2)],
        out_specs=[_bs((qt, GROUP_W), lambda r, nb: (nb, r)), _bs((None, qt, HEAD_DIM), lambda r, nb: (r, nb, 0))],
        out_shape=[SDS((ell, d * GROUP_W), F32), SDS((d, ell, HEAD_DIM), F32)],
        scratch_shapes=[pltpu.VMEM((qt + BLK, GROUP_W), BF16)] * 2, compiler_params=_cp(2), name=f"attn_fwd_d{d}")(
            qkv, qkv, qkv, qkv, qkv)


def _attn_merge(outs, lses, tm):
    t = outs[0].shape[0]

    def body(o0, o1, o2, l0, l1, l2, attn_ref, attn_bf_ref, t0, t1, t2, so, sl, lt_s):
        for g, (d, o_ref, l_ref) in enumerate(zip(DILATIONS, (o0, o1, o2), (l0, l1, l2))):
            n = tm // d
            for r in range(d):
                rows = _strided(r, n, d)
                for hh in range(HEADS_PER_GROUP):
                    oc = r * GROUP_W + hh * HEAD_DIM
                    so[g * HEADS_PER_GROUP + hh, rows, :] = o_ref[:, oc:oc + HEAD_DIM]
                sl[g, rows, :] = l_ref[r]
        ls = [sl[g] for g in range(N_GROUPS)]
        mx = jnp.maximum(jnp.maximum(ls[0], ls[1]), ls[2])
        es = [jnp.exp(l - mx) for l in ls]
        den = es[0] + es[1] + es[2]
        ws = [e / den for e in es]
        lt_s[...] = mx + jnp.log(den)
        for hh in range(HEADS_PER_GROUP):
            cs = slice(hh * HEAD_DIM, (hh + 1) * HEAD_DIM)
            a = ws[0][:, hh:hh + 1] * so[hh]
            for g in range(1, N_GROUPS):
                a = a + ws[g][:, hh:hh + 1] * so[g * HEADS_PER_GROUP + hh]
            attn_ref[:, cs] = a
            attn_bf_ref[:, cs] = a.astype(BF16)
        for d, t_ref in zip(DILATIONS, (t0, t1, t2)):
            n = tm // d
            for r in range(d):
                t_ref[r] = lt_s[_strided(r, n, d), :]

    dil = lambda d: _bs((tm // d, d * GROUP_W), lambda i: (i, 0))
    lsp = lambda d: _bs((d, tm // d, HEAD_DIM), lambda i: (0, i, 0))
    row = _bs((tm, GROUP_W), lambda i: (i, 0))
    return pl.pallas_call(
        body, grid=(t // tm,),
        in_specs=[dil(d) for d in DILATIONS] + [lsp(d) for d in DILATIONS],
        out_specs=[row, row] + [lsp(d) for d in DILATIONS],
        out_shape=[SDS((t, GROUP_W), F32), SDS((t, GROUP_W), BF16)] + [SDS(l.shape, F32) for l in lses],
        scratch_shapes=[pltpu.VMEM((N_GROUPS * HEADS_PER_GROUP, tm, HEAD_DIM), F32), pltpu.VMEM((N_GROUPS, tm, HEAD_DIM), F32),
                        pltpu.VMEM((tm, HEAD_DIM), F32)],
        compiler_params=_cp(1), name="attn_merge")(*outs, *lses)


def _attn_bwd_pre(d_attn, attn, tm):
    t = attn.shape[0]

    def body(da_ref, a_ref, g0, g1, g2, e0, e1, e2, dl_s, da_s):
        lane = lax.broadcasted_iota(jnp.int32, (tm, HEAD_DIM), 1)
        dl = jnp.zeros((tm, HEAD_DIM), F32)
        for hh in range(HEADS_PER_GROUP):
            cs = slice(hh * HEAD_DIM, (hh + 1) * HEAD_DIM)
            dav = da_ref[:, cs]
            da_s[hh] = dav
            dl = jnp.where(lane == hh, jnp.sum(dav * a_ref[:, cs], axis=-1, keepdims=True), dl)
        dl_s[...] = dl
        for d, g_ref, e_ref in zip(DILATIONS, (g0, g1, g2), (e0, e1, e2)):
            n = tm // d
            for r in range(d):
                rows = _strided(r, n, d)
                for hh in range(HEADS_PER_GROUP):
                    oc = r * GROUP_W + hh * HEAD_DIM
                    g_ref[:, oc:oc + HEAD_DIM] = da_s[hh, rows, :].astype(BF16)
                e_ref[r] = dl_s[rows, :]

    row = _bs((tm, GROUP_W), lambda i: (i, 0))
    return pl.pallas_call(
        body, grid=(t // tm,), in_specs=[row, row],
        out_specs=[_bs((tm // d, d * GROUP_W), lambda i: (i, 0)) for d in DILATIONS]
        + [_bs((d, tm // d, HEAD_DIM), lambda i: (0, i, 0)) for d in DILATIONS],
        out_shape=[SDS((t // d, d * GROUP_W), BF16) for d in DILATIONS]
        + [SDS((d, t // d, HEAD_DIM), F32) for d in DILATIONS],
        scratch_shapes=[pltpu.VMEM((tm, HEAD_DIM), F32), pltpu.VMEM((HEADS_PER_GROUP, tm, HEAD_DIM), F32)],
        compiler_params=_cp(1), name="attn_bwd_pre")(d_attn, attn)


def _attn_bwd(qkv, d_a, lt, delta, d, qt):
    ell = qkv.shape[1]
    nsub = qt // BLK
    ntile = ell // qt
    nblk = ell // BLK
    scale = 1.0 / math.sqrt(HEAD_DIM)

    def body(q_ref, qn_ref, kc_ref, kp_ref, vc_ref, vp_ref, da_ref, dan_ref, lt_ref, ltn_ref, dl_ref, dln_ref, o_ref,
             kcat, vcat, dk_acc, dv_acc):
        nb = pl.program_id(1)
        kcat[0:BLK, :] = kp_ref[...]
        kcat[BLK:, :] = kc_ref[...]
        vcat[0:BLK, :] = vp_ref[...]
        vcat[BLK:, :] = vc_ref[...]
        qi = lax.broadcasted_iota(jnp.int32, (BLK, BLK), 0)
        kj = lax.broadcasted_iota(jnp.int32, (BLK, BLK), 1)
        valid_next = (kj >= qi) & (nb < ntile - 1)
        for hh in range(HEADS_PER_GROUP):
            cs = slice(hh * HEAD_DIM, (hh + 1) * HEAD_DIM)
            dk_acc[...] = jnp.zeros_like(dk_acc)
            dv_acc[...] = jnp.zeros_like(dv_acc)
            for b in range(nsub):
                rs = slice(b * BLK, (b + 1) * BLK)
                ks = slice(b * BLK, (b + 2) * BLK)
                valid = _band_mask((nb == 0) if b == 0 else False)
                qb, kk, vv, dab = q_ref[rs, cs], kcat[ks, cs], vcat[ks, cs], da_ref[rs, cs]
                s = lax.dot_general(qb, kk, _DNUMS["nt"], preferred_element_type=F32) * scale
                p = jnp.where(valid, jnp.exp(s - lt_ref[rs, hh:hh + 1]), 0.0)
                dp = lax.dot_general(dab, vv, _DNUMS["nt"], preferred_element_type=F32)
                ds = (p * (dp - dl_ref[rs, hh:hh + 1])).astype(BF16)
                o_ref[0, rs, cs] = jnp.dot(ds, kk, preferred_element_type=F32) * scale
                dk_acc[ks, :] += lax.dot_general(ds, qb, _DNUMS["tn"], preferred_element_type=F32) * scale
                dv_acc[ks, :] += lax.dot_general(p.astype(BF16), dab, _DNUMS["tn"], preferred_element_type=F32)
            ks = slice(nsub * BLK, (nsub + 1) * BLK)
            qn, kl, vl, dan = qn_ref[:, cs], kcat[ks, cs], vcat[ks, cs], dan_ref[:, cs]
            s = lax.dot_general(qn, kl, _DNUMS["nt"], preferred_element_type=F32) * scale
            p = jnp.where(valid_next, jnp.exp(s - ltn_ref[:, hh:hh + 1]), 0.0)
            dp = lax.dot_general(dan, vl, _DNUMS["nt"], preferred_element_type=F32)
            ds = (p * (dp - dln_ref[:, hh:hh + 1])).astype(BF16)
            dk_acc[ks, :] += lax.dot_general(ds, qn, _DNUMS["tn"], preferred_element_type=F32) * scale
            dv_acc[ks, :] += lax.dot_general(p.astype(BF16), dan, _DNUMS["tn"], preferred_element_type=F32)
            o_ref[1, :, cs] = dk_acc[BLK:, :]
            o_ref[2, :, cs] = dv_acc[BLK:, :]

    nxt = lambda nb: jnp.minimum((nb + 1) * nsub, nblk - 1)
    prv = lambda nb: jnp.maximum(nb * nsub - 1, 0)
    cur3 = lambda c: _bs((None, qt, GROUP_W), lambda r, nb: (c, nb, r))
    in_specs = [
        cur3(0), _bs((None, BLK, GROUP_W), lambda r, nb: (0, nxt(nb), r)),
        cur3(1), _bs((None, BLK, GROUP_W), lambda r, nb: (1, prv(nb), r)),
        cur3(2), _bs((None, BLK, GROUP_W), lambda r, nb: (2, prv(nb), r)),
        _bs((qt, GROUP_W), lambda r, nb: (nb, r)), _bs((BLK, GROUP_W), lambda r, nb: (nxt(nb), r)),
        _bs((None, qt, HEAD_DIM), lambda r, nb: (r, nb, 0)), _bs((None, BLK, HEAD_DIM), lambda r, nb: (r, nxt(nb), 0)),
        _bs((None, qt, HEAD_DIM), lambda r, nb: (r, nb, 0)), _bs((None, BLK, HEAD_DIM), lambda r, nb: (r, nxt(nb), 0)),
    ]
    return pl.pallas_call(
        body, grid=(d, ntile), in_specs=in_specs, out_specs=_bs((3, qt, GROUP_W), lambda r, nb: (0, nb, r)),
        out_shape=SDS((3, ell, d * GROUP_W), F32),
        scratch_shapes=[pltpu.VMEM((qt + BLK, GROUP_W), BF16)] * 2 + [pltpu.VMEM((qt + BLK, HEAD_DIM), F32)] * 2,
        compiler_params=_cp(2), name=f"attn_bwd_d{d}")(qkv, qkv, qkv, qkv, qkv, qkv, d_a, d_a, lt, lt, delta, delta)


def _undilate_rope_bwd(dqkvs, pos, invf, tm):
    t = pos.shape[0]

    def body(g0, g1, g2, pos_ref, invf_ref, o_ref, c_s, s1_s, s2_s, nat):
        c = pl.program_id(1)

        @pl.when(c == 0)
        def _():
            _rope_tables(pos_ref, invf_ref, c_s, s1_s, s2_s)

        for g, (d, g_ref) in enumerate(zip(DILATIONS, (g0, g1, g2))):
            n = tm // d
            for r in range(d):
                for hh in range(HEADS_PER_GROUP):
                    oc = r * GROUP_W + hh * HEAD_DIM
                    nat[g * HEADS_PER_GROUP + hh, _strided(r, n, d), :] = g_ref[:, oc:oc + HEAD_DIM]

        @pl.when(c < 2)
        def _():
            cc, s1, s2 = c_s[...], s1_s[...], s2_s[...]
            for h in range(QK_W // HEAD_DIM):
                xv = nat[h]
                y = xv * cc - pltpu.roll(xv, HEAD_DIM - ROPE_HALF, 1) * s1 - pltpu.roll(xv, ROPE_HALF, 1) * s2
                o_ref[:, h * HEAD_DIM:(h + 1) * HEAD_DIM] = y.astype(BF16)

        @pl.when(c == 2)
        def _():
            for h in range(QK_W // HEAD_DIM):
                o_ref[:, h * HEAD_DIM:(h + 1) * HEAD_DIM] = nat[h].astype(BF16)

    return pl.pallas_call(
        body, grid=(t // tm, 3),
        in_specs=[_bs((None, tm // d, d * GROUP_W), lambda i, c: (c, i, 0)) for d in DILATIONS]
        + [_bs((tm, 1), lambda i, c: (i, 0)), _bs((1, HEAD_DIM), lambda i, c: (0, 0))],
        out_specs=_bs((tm, QK_W), lambda i, c: (i, c)), out_shape=SDS((t, 3 * QK_W), BF16),
        scratch_shapes=[pltpu.VMEM((tm, HEAD_DIM), F32)] * 3 + [pltpu.VMEM((QK_W // HEAD_DIM, tm, HEAD_DIM), F32)],
        compiler_params=_cp(2), name="undilate_rope_bwd")(*dqkvs, pos, invf)


def _cmul(ar, ai, br, bi):
    return ar * br - ai * bi, ar * bi + ai * br


def _ssm_disc(a_re, a_im, log_dt, nsq):
    def body(lr_ref, li_ref, ldt_ref, br_ref, bi_ref, zr_ref, zi_ref, pr_ref, pi_ref):
        lr, li = lr_ref[...], li_ref[...]
        dt = jnp.exp(ldt_ref[...])
        mag = jnp.exp(lr * dt)
        bar_re, bar_im = mag * jnp.cos(li * dt), mag * jnp.sin(li * dt)
        nr, ni = bar_re - 1.0, bar_im
        den = lr * lr + li * li
        br_ref[...], bi_ref[...] = bar_re, bar_im
        zr_ref[...] = (nr * lr + ni * li) / den
        zi_ref[...] = (ni * lr - nr * li) / den
        pr, pi = bar_re, bar_im
        for _ in range(nsq):
            pr, pi = _cmul(pr, pi, pr, pi)
        pr_ref[...], pi_ref[...] = pr, pi

    return pl.pallas_call(body, out_shape=[SDS(a_re.shape, F32)] * 6, name="ssm_discretise")(a_re, a_im, log_dt)


def _ssm_scale_b(z_re, z_im, b_re, b_im):
    def body(zr_ref, zi_ref, br_ref, bi_ref, or_ref, oi_ref):
        zr, zi, br, bi = zr_ref[...], zi_ref[...], br_ref[...], bi_ref[...]
        or_ref[...] = zr * br - zi * bi
        oi_ref[...] = zr * bi + zi * br

    return pl.pallas_call(body, out_shape=[SDS(b_re.shape, F32)] * 2, name="ssm_scale_b")(z_re, z_im, b_re, b_im)


def _ssm_scale_b_bwd(z_re, z_im, b_re, b_im, g_re, g_im):
    def body(zr_ref, zi_ref, br_ref, bi_ref, gr_ref, gi_ref, dbr_ref, dbi_ref, dzr_ref, dzi_ref):
        zr, zi, br, bi, gr, gi = zr_ref[...], zi_ref[...], br_ref[...], bi_ref[...], gr_ref[...], gi_ref[...]
        dbr_ref[...] = zr * gr + zi * gi
        dbi_ref[...] = zr * gi - zi * gr
        dzr_ref[...] = jnp.sum(br * gr + bi * gi, axis=-1, keepdims=True)
        dzi_ref[...] = jnp.sum(br * gi - bi * gr, axis=-1, keepdims=True)

    return pl.pallas_call(body, out_shape=[SDS(b_re.shape, F32)] * 2 + [SDS(z_re.shape, F32)] * 2,
                          name="ssm_scale_b_bwd")(z_re, z_im, b_re, b_im, g_re, g_im)


def _ssm_disc_bwd(a_re, a_im, log_dt, gb_re, gb_im, gz_re, gz_im):
    def body(lr_ref, li_ref, ldt_ref, gbr_ref, gbi_ref, gzr_ref, gzi_ref, dar_ref, dai_ref, dldt_ref):
        lr, li = lr_ref[...], li_ref[...]
        dt = jnp.exp(ldt_ref[...])
        mag = jnp.exp(lr * dt)
        bar_re, bar_im = mag * jnp.cos(li * dt), mag * jnp.sin(li * dt)
        nr, ni = bar_re - 1.0, bar_im
        den = lr * lr + li * li
        zr, zi = (nr * lr + ni * li) / den, (ni * lr - nr * li) / den
        gzr, gzi = gzr_ref[...], gzi_ref[...]
        gbr = gbr_ref[...] + (lr * gzr - li * gzi) / den
        gbi = gbi_ref[...] + (lr * gzi + li * gzr) / den
        qr, qi = (zr * lr + zi * li) / den, (zi * lr - zr * li) / den
        dar_ref[...] = dt * (bar_re * gbr + bar_im * gbi) - qr * gzr - qi * gzi
        dai_ref[...] = dt * (bar_re * gbi - bar_im * gbr) - qr * gzi + qi * gzr
        wr, wi = lr * bar_re - li * bar_im, lr * bar_im + li * bar_re
        dldt_ref[...] = dt * jnp.sum(wr * gbr + wi * gbi, axis=-1, keepdims=True)

    return pl.pallas_call(body, out_shape=[SDS(a_re.shape, F32)] * 2 + [SDS(log_dt.shape, F32)],
                          name="ssm_discretise_bwd")(a_re, a_im, log_dt, gb_re, gb_im, gz_re, gz_im)


def _permute_u(z, ucol_block, tm):
    t = z.shape[0]
    seg = t // N_DEV
    z3 = z.reshape(N_DEV, seg, z.shape[1])

    def body(z_ref, u_ref, ub_ref, tmp):
        for n in range(SSM_W // BLK):
            for j in range(N_DEV):
                tmp[n, pl.ds(j, tm // N_DEV, stride=N_DEV), :] = z_ref[j, :, n * BLK:(n + 1) * BLK]
            u_ref[:, n * BLK:(n + 1) * BLK] = tmp[n]
            ub_ref[:, n * BLK:(n + 1) * BLK] = tmp[n].astype(BF16)

    row = _bs((tm, SSM_W), lambda i: (i, 0))
    return pl.pallas_call(
        body, grid=(t // tm,), in_specs=[_bs((N_DEV, tm // N_DEV, SSM_W), lambda i: (0, i, ucol_block))],
        out_specs=[row, row], out_shape=[SDS((t, SSM_W), F32), SDS((t, SSM_W), BF16)],
        scratch_shapes=[pltpu.VMEM((SSM_W // BLK, tm, BLK), F32)], compiler_params=_cp(1), name="permute_u")(z3)


def _drive(src_ref, mat_ref, dst, mode):
    for kn in range(2 * SSM_NB):
        n = kn % SSM_NB
        a = src_ref[:, n * BLK:(n + 1) * BLK]
        dst[:, kn * 512:(kn + 1) * 512] = lax.dot_general(a, mat_ref[kn], _DNUMS[mode], preferred_element_type=F32)


def _scan_chunk(src, lam_ref, carry, *, reverse, store=None, h_ref=None, acc=None):
    steps = src.shape[0] // 8
    for c in range(NSTATE // SCAN_LANES):
        re = slice(c * SCAN_LANES, (c + 1) * SCAN_LANES)
        im = slice(NSTATE + c * SCAN_LANES, NSTATE + (c + 1) * SCAN_LANES)
        ar, ai = lam_ref[:, re], lam_ref[:, im]

        def step(s, val):
            i = (steps - 1 - s) if reverse else s
            rows = pl.ds(pl.multiple_of(i * 8, 8), 8)
            if acc is not None:
                hr, hi, dr, di = val
                pr, pi = h_ref[rows, re], h_ref[rows, im]
                dr = dr + hr * pr + hi * pi
                di = di + hi * pr - hr * pi
            else:
                hr, hi = val
            nr = ar * hr - ai * hi + src[rows, re]
            ni = ar * hi + ai * hr + src[rows, im]
            if store is not None:
                store[rows, re] = nr
                store[rows, im] = ni
            return (nr, ni, dr, di) if acc is not None else (nr, ni)

        init = (carry[:, re], carry[:, im])
        if acc is not None:
            init = init + (acc[:, re], acc[:, im])
        out = lax.fori_loop(0, steps, step, init, unroll=4)
        carry[:, re], carry[:, im] = out[0], out[1]
        if acc is not None:
            acc[:, re], acc[:, im] = out[2], out[3]


def _segment_carries(e_ref, pw_ref, out_ref, reverse):
    pr, pi = pw_ref[:, 0:NSTATE], pw_ref[:, NSTATE:]
    hr = jnp.zeros((1, NSTATE), F32)
    hi = jnp.zeros((1, NSTATE), F32)
    order = range(N_DEV - 1, -1, -1) if reverse else range(N_DEV)
    for j in order:
        out_ref[j:j + 1, 0:NSTATE] = hr
        out_ref[j:j + 1, NSTATE:] = hi
        tr, ti = _cmul(pr, pi, hr, hi)
        hr, hi = e_ref[j:j + 1, 0:NSTATE] + tr, e_ref[j:j + 1, NSTATE:] + ti


def _ssm_carries(name, src, mat, mode, lam8, pw, reverse):
    t = src.shape[0]
    nchunk = t // SCAN_ROWS

    def body(src_ref, mat_ref, lam_ref, pw_ref, out_ref, drive, carry):
        c = pl.program_id(0)

        @pl.when(c == 0)
        def _():
            carry[...] = jnp.zeros_like(carry)

        _drive(src_ref, mat_ref, drive, mode)
        _scan_chunk(drive, lam_ref, carry, reverse=reverse)

        @pl.when(c == nchunk - 1)
        def _():
            _segment_carries(carry, pw_ref, out_ref, reverse)

    blk = (lambda c: (nchunk - 1 - c, 0)) if reverse else (lambda c: (c, 0))
    return pl.pallas_call(
        body, grid=(nchunk,),
        in_specs=[_bs((SCAN_ROWS, SSM_W), blk), _bs(mat.shape, lambda c: (0, 0, 0)), _bs((8, 2 * NSTATE), lambda c: (0, 0)),
                  _bs((1, 2 * NSTATE), lambda c: (0, 0))],
        out_specs=_bs((8, 2 * NSTATE), lambda c: (0, 0)), out_shape=SDS((8, 2 * NSTATE), F32),
        scratch_shapes=[pltpu.VMEM((SCAN_ROWS, 2 * NSTATE), F32), pltpu.VMEM((8, 2 * NSTATE), F32)],
        compiler_params=_cp(1), name=name)(src, mat, lam8, pw)


def _ssm_fwd(u_bf, bd, cd, lam8, start):
    t = u_bf.shape[0]
    nchunk = t // SCAN_ROWS

    def body(u_ref, bd_ref, cd_ref, lam_ref, start_ref, h_ref, y_ref, drive, carry):
        @pl.when(pl.program_id(0) == 0)
        def _():
            carry[...] = start_ref[...]

        _drive(u_ref, bd_ref, drive, "nn")
        _scan_chunk(drive, lam_ref, carry, reverse=False, store=h_ref)
        for n in range(SSM_NB):
            hr = h_ref[:, n * 512:(n + 1) * 512].astype(BF16)
            hi = h_ref[:, NSTATE + n * 512:NSTATE + (n + 1) * 512].astype(BF16)
            y_ref[:, n * BLK:(n + 1) * BLK] = (jnp.dot(hr, cd_ref[n], preferred_element_type=F32)
                                              + jnp.dot(hi, cd_ref[SSM_NB + n], preferred_element_type=F32))

    return pl.pallas_call(
        body, grid=(nchunk,),
        in_specs=[_bs((SCAN_ROWS, SSM_W), lambda c: (c, 0)), _bs(bd.shape, lambda c: (0, 0, 0)), _bs(cd.shape, lambda c: (0, 0, 0)),
                  _bs((8, 2 * NSTATE), lambda c: (0, 0)), _bs((8, 2 * NSTATE), lambda c: (0, 0))],
        out_specs=[_bs((SCAN_ROWS, 2 * NSTATE), lambda c: (c, 0)), _bs((SCAN_ROWS, SSM_W), lambda c: (c, 0))],
        out_shape=[SDS((t, 2 * NSTATE), F32), SDS((t, SSM_W), F32)],
        scratch_shapes=[pltpu.VMEM((SCAN_ROWS, 2 * NSTATE), F32), pltpu.VMEM((8, 2 * NSTATE), F32)],
        compiler_params=_cp(1), name="ssm_scan_fwd")(u_bf, bd, cd, lam8, start)


def _ssm_bwd(dys_bf, u_bf, h, bd, cd, lamc8, start):
    t = u_bf.shape[0]
    nchunk = t // SCAN_ROWS

    def body(dys_ref, u_ref, h_ref, bd_ref, cd_ref, lam_ref, start_ref, du_ref, dlam_ref, dbd_ref, dcd_ref, drive, adj, carry):
        c = pl.program_id(0)

        @pl.when(c == 0)
        def _():
            carry[...] = start_ref[...]
            dlam_ref[...] = jnp.zeros_like(dlam_ref)
            dbd_ref[...] = jnp.zeros_like(dbd_ref)
            dcd_ref[...] = jnp.zeros_like(dcd_ref)

        _drive(dys_ref, cd_ref, drive, "nt")
        _scan_chunk(drive, lam_ref, carry, reverse=True, store=adj, h_ref=h_ref, acc=dlam_ref)
        for n in range(SSM_NB):
            cs = slice(n * BLK, (n + 1) * BLK)
            acc = None
            for k in range(2):
                kn = k * SSM_NB + n
                ss = slice(kn * 512, (kn + 1) * 512)
                lam_b = adj[:, ss].astype(BF16)
                part = lax.dot_general(lam_b, bd_ref[kn], _DNUMS["nt"], preferred_element_type=F32)
                acc = part if acc is None else acc + part
                dbd_ref[kn] += lax.dot_general(u_ref[:, cs], lam_b, _DNUMS["tn"], preferred_element_type=F32)
                dcd_ref[kn] += lax.dot_general(h_ref[:, ss].astype(BF16), dys_ref[:, cs], _DNUMS["tn"],
                                               preferred_element_type=F32)
            du_ref[:, cs] = acc

    rev = lambda c: (nchunk - 1 - c, 0)
    const2 = lambda c: (0, 0)
    const3 = lambda c: (0, 0, 0)
    return pl.pallas_call(
        body, grid=(nchunk,),
        in_specs=[_bs((SCAN_ROWS, SSM_W), rev), _bs((SCAN_ROWS, SSM_W), rev), _bs((SCAN_ROWS, 2 * NSTATE), rev),
                  _bs(bd.shape, const3), _bs(cd.shape, const3), _bs((8, 2 * NSTATE), const2), _bs((8, 2 * NSTATE), const2)],
        out_specs=[_bs((SCAN_ROWS, SSM_W), rev), _bs((8, 2 * NSTATE), const2), _bs(bd.shape, const3), _bs(cd.shape, const3)],
        out_shape=[SDS((t, SSM_W), F32), SDS((8, 2 * NSTATE), F32), SDS(bd.shape, F32), SDS(cd.shape, F32)],
        scratch_shapes=[pltpu.VMEM((SCAN_ROWS, 2 * NSTATE), F32), pltpu.VMEM((SCAN_ROWS, 2 * NSTATE), F32),
                        pltpu.VMEM((8, 2 * NSTATE), F32)],
        compiler_params=_cp(1), name="ssm_scan_bwd")(dys_bf, u_bf, h, bd, cd, lamc8, start)


def _gelu_parts(x):
    c0 = math.sqrt(2.0 / math.pi)
    inner = c0 * (x + 0.044715 * x * x * x)
    th = jnp.tanh(inner)
    val = 0.5 * x * (1.0 + th)
    grad = 0.5 * (1.0 + th) + 0.5 * x * (1.0 - th * th) * c0 * (1.0 + 3.0 * 0.044715 * x * x)
    return val, grad


def _ssm_out(y_raw, u, d_skip, tm):
    t = u.shape[0]
    seg = t // N_DEV

    def body(y_ref, u_ref, d_ref, ys_ref, yg_ref, tmp):
        ys = y_ref[...] + d_ref[...] * u_ref[...]
        ys_ref[...] = ys
        yg = _gelu_parts(ys)[0]
        for n in range(SSM_W // BLK):
            tmp[n] = yg[:, n * BLK:(n + 1) * BLK]
            for j in range(N_DEV):
                yg_ref[j, :, n * BLK:(n + 1) * BLK] = tmp[n, pl.ds(j, tm // N_DEV, stride=N_DEV), :].astype(BF16)

    row = _bs((tm, SSM_W), lambda i: (i, 0))
    ys, yg = pl.pallas_call(
        body, grid=(t // tm,), in_specs=[row, row, _bs((1, SSM_W), lambda i: (0, 0))],
        out_specs=[row, _bs((N_DEV, tm // N_DEV, SSM_W), lambda i: (0, i, 0))],
        out_shape=[SDS((t, SSM_W), F32), SDS((N_DEV, seg, SSM_W), BF16)],
        scratch_shapes=[pltpu.VMEM((SSM_W // BLK, tm, BLK), F32)], compiler_params=_cp(1), name="ssm_out")(y_raw, u, d_skip)
    return ys, yg.reshape(t, SSM_W)


def _ssm_out_bwd(d_yg, ys, u, tm):
    t = u.shape[0]
    seg = t // N_DEV

    def body(dg_ref, ys_ref, u_ref, dys_ref, dysb_ref, dd_ref, tmp):
        for n in range(SSM_W // BLK):
            for j in range(N_DEV):
                tmp[n, pl.ds(j, tm // N_DEV, stride=N_DEV), :] = dg_ref[j, :, n * BLK:(n + 1) * BLK]
        dyg = jnp.concatenate([tmp[n] for n in range(SSM_W // BLK)], axis=1)
        dys = dyg * _gelu_parts(ys_ref[...])[1]
        dys_ref[...] = dys
        dysb_ref[...] = dys.astype(BF16)
        part = jnp.sum(dys * u_ref[...], axis=0, keepdims=True)

        @pl.when(pl.program_id(0) == 0)
        def _():
            dd_ref[...] = part

        @pl.when(pl.program_id(0) > 0)
        def _():
            dd_ref[...] += part

    row = _bs((tm, SSM_W), lambda i: (i, 0))
    return pl.pallas_call(
        body, grid=(t // tm,), in_specs=[_bs((N_DEV, tm // N_DEV, SSM_W), lambda i: (0, i, 0)), row, row],
        out_specs=[row, row, _bs((1, SSM_W), lambda i: (0, 0))],
        out_shape=[SDS((t, SSM_W), F32), SDS((t, SSM_W), BF16), SDS((1, SSM_W), F32)],
        scratch_shapes=[pltpu.VMEM((SSM_W // BLK, tm, BLK), F32)], compiler_params=_cp(1), name="ssm_out_bwd")(
            d_yg.reshape(N_DEV, seg, SSM_W), ys, u)


def _du_to_dz(du_raw, dys, d_skip, tm):
    t = du_raw.shape[0]
    seg = t // N_DEV

    def body(du_ref, dys_ref, d_ref, o_ref, tmp):
        du = du_ref[...] + d_ref[...] * dys_ref[...]
        for n in range(SSM_W // BLK):
            tmp[n] = du[:, n * BLK:(n + 1) * BLK]
            for j in range(N_DEV):
                o_ref[j, :, n * BLK:(n + 1) * BLK] = tmp[n, pl.ds(j, tm // N_DEV, stride=N_DEV), :].astype(BF16)

    row = _bs((tm, SSM_W), lambda i: (i, 0))
    out = pl.pallas_call(
        body, grid=(t // tm,), in_specs=[row, row, _bs((1, SSM_W), lambda i: (0, 0))],
        out_specs=_bs((N_DEV, tm // N_DEV, SSM_W), lambda i: (0, i, 0)), out_shape=SDS((N_DEV, seg, SSM_W), BF16),
        scratch_shapes=[pltpu.VMEM((SSM_W // BLK, tm, BLK), F32)], compiler_params=_cp(1), name="du_to_dz")(du_raw, dys, d_skip)
    return out.reshape(t, SSM_W)


def _block_diag(blocks):
    nb, ng, r, c = blocks.shape
    eye = jnp.eye(ng, dtype=blocks.dtype)
    return (blocks[:, :, :, None, :] * eye[None, :, None, :, None]).reshape(nb, ng * r, ng * c)


def _diag_blocks(full, r, c):
    k, nb = full.shape[:2]
    ng = full.shape[2] // r
    x = full.reshape(k, nb, ng, r, ng, c)
    eye = jnp.eye(ng, dtype=full.dtype)
    return jnp.sum(x * eye[None, None, :, None, :, None], axis=4).reshape(k, nb * ng, r, c)


_SMALL = ("g_mix", "a_re", "a_im", "log_dt", "b_re", "b_im", "c_re", "c_im", "d_skip", "g_ffn", "g_final")


def _pack_small(arrs):
    flat = jnp.concatenate([a.reshape(-1) for a in arrs])
    pad = (-flat.shape[0]) % (8 * 128)
    return jnp.pad(flat, (0, pad)).reshape(-1, 128)


def _unpack_small(packed, shapes):
    flat = packed.reshape(-1)
    out, off = [], 0
    for s in shapes:
        n = math.prod(s)
        out.append(flat[off:off + n].reshape(s))
        off += n
    return out


def kernel(x, p, positions, g_mix, w_in, a_re, a_im, log_dt, b_re, b_im, c_re, c_im, d_skip, w_attn_proj, w_glu_a, w_glu_b, w_out, g_ffn, w_ffn_gate, w_ffn_up, w_ffn_down, w_ple_gate, w_ple_proj, g_final, loss_target, m_g_mix, m_w_in, m_a_re, m_a_im, m_log_dt, m_b_re, m_b_im, m_c_re, m_c_im, m_d_skip, m_w_attn_proj, m_w_glu_a, m_w_glu_b, m_w_out, m_g_ffn, m_w_ffn_gate, m_w_ffn_up, m_w_ffn_down, m_w_ple_gate, m_w_ple_proj, m_g_final, v_g_mix, v_w_in, v_a_re, v_a_im, v_log_dt, v_b_re, v_b_im, v_c_re, v_c_im, v_d_skip, v_w_attn_proj, v_w_glu_a, v_w_glu_b, v_w_out, v_g_ffn, v_w_ffn_gate, v_w_ffn_up, v_w_ffn_down, v_w_ple_gate, v_w_ple_proj, v_g_final):
    args = dict(locals())
    t, d = x.shape[1], x.shape[2]
    inw = w_in.shape[2] * N_DEV
    fs = w_ffn_gate.shape[2]
    ff = fs * N_DEV
    ple = w_ple_proj.shape[1]
    seg = t // N_DEV
    assert inw == 3 * QK_W + SSM_W + 2 * d and t % (N_DEV * SCAN_ROWS // 8) == 0 and seg & (seg - 1) == 0
    tm = min(1024, t)
    te = min(512, t)
    tk = min(1024, t)
    ucol = (3 * QK_W) // SSM_W
    gcol = (3 * QK_W + SSM_W) // d
    assert (3 * QK_W + SSM_W) % d == 0

    x2, p2, tgt = x[0], p[0, 0], loss_target[0]
    pos = positions.reshape(t, 1)
    inv = ROPE_THETA ** (-jnp.arange(ROPE_HALF, dtype=F32) * 2.0 / ROPE_DIM)
    invf = jnp.concatenate([inv, inv, jnp.zeros((HEAD_DIM - ROPE_DIM,), F32)]).reshape(1, HEAD_DIM)

    wnames = ("w_in", "w_attn_proj", "w_glu_a", "w_glu_b", "w_out", "w_ffn_gate", "w_ffn_up", "w_ffn_down", "w_ple_gate",
              "w_ple_proj")
    kinds = ("cols", "cols", "cols", "cols", "rows", "slot", "slot", "rows", "rows", "cols")
    shards = [args[n][0].astype(BF16) for n in wnames]
    sizes = [s.shape[0] if k == "rows" else s.shape[-1] for s, k in zip(shards, kinds)]
    ag = _exchange_start("gather_weights_start", shards, kinds, sizes, True)

    row_d = _bs((tm, d), lambda i, j, k: (i, 0))
    row_e = _bs((te, d), lambda i, j, k: (i, 0))
    vec_d = _bs((1, d), lambda i, j, k: (0, 0))
    sq_w = _bs((d, d), lambda i, j, k: (0, 0))
    n1 = _rms_fwd("norm_mix", x2, g_mix + ag[3][0:1, 0:1], tm)
    W_in, = _exchange_wait("gather_w_in_wait", ag, [0], kinds, sizes, True, n1)
    qkv = _mm("qkv_proj", (t // tm, 3, 1), [("nn", n1, row_d, W_in, _bs((d, QK_W), lambda i, j, k: (0, j)))],
              [(SDS((3, t // dil, dil * GROUP_W), BF16), _bs((None, tm // dil, dil * GROUP_W), lambda i, j, k: (j, i, 0)))
               for dil in DILATIONS],
              extras=[(pos, _bs((tm, 1), lambda i, j, k: (i, 0))), (invf, _bs((1, HEAD_DIM), lambda i, j, k: (0, 0)))],
              epilogue=_rope_dilate_epilogue(tm),
              scratch=[pltpu.VMEM((tm, HEAD_DIM), F32)] * 3 + [pltpu.VMEM((QK_W // HEAD_DIM, tm, HEAD_DIM), F32)])
    z_u, = _mm("u_proj", (t // tm, 1, 1), [("nn", n1, row_d, W_in, _bs((d, SSM_W), lambda i, j, k: (0, ucol)))],
               [(SDS((t, SSM_W), F32), _bs((tm, SSM_W), lambda i, j, k: (i, 0)))])
    zg, = _mm("z_gates", (t // tm, 2, 1),
              [("nn", n1, row_d, W_in, _bs((d, d), lambda i, j, k: (0, gcol + j)))],
              [(SDS((t, 2 * d), BF16), _bs((tm, d), lambda i, j, k: (i, j)))])

    outs, lses = [], []
    for g, dil in enumerate(DILATIONS):
        o_g, l_g = _attn_fwd(qkv[g], dil, min(512, t // dil))
        outs.append(o_g)
        lses.append(l_g)
    merged = _attn_merge(outs, lses, te)
    attn, attn_bf, lts = merged[0], merged[1], merged[2:]

    nsq = seg.bit_length() - 1
    bar_re, bar_im, z_re, z_im, pw_re, pw_im = _ssm_disc(a_re[0], a_im[0], log_dt.reshape(SSM_GROUPS, 1), nsq)
    gp = SSM_GROUPS * SSM_STATE
    b_re2, b_im2 = b_re.reshape(gp, SSM_GROUP), b_im.reshape(gp, SSM_GROUP)
    bb_re, bb_im = _ssm_scale_b(z_re.reshape(gp, 1), z_im.reshape(gp, 1), b_re2, b_im2)

    def chunks(a, r, c):
        return a.reshape(SSM_NB, SSM_GROUPS // SSM_NB, r, c)

    bbt = lambda a: jnp.swapaxes(a.reshape(SSM_GROUPS, SSM_STATE, SSM_GROUP), 1, 2)
    bd = jnp.concatenate([_block_diag(chunks(bbt(bb_re), SSM_GROUP, SSM_STATE)),
                          _block_diag(chunks(bbt(bb_im), SSM_GROUP, SSM_STATE))]).astype(BF16)
    ct = lambda a: jnp.swapaxes(a[0], 1, 2)
    cd = jnp.concatenate([_block_diag(chunks(ct(c_re), SSM_STATE, SSM_GROUP)),
                          _block_diag(chunks(-ct(c_im), SSM_STATE, SSM_GROUP))]).astype(BF16)
    lam = jnp.concatenate([bar_re.reshape(1, gp), bar_im.reshape(1, gp)], axis=1)
    lamc = jnp.concatenate([bar_re.reshape(1, gp), -bar_im.reshape(1, gp)], axis=1)
    pw = jnp.concatenate([pw_re.reshape(1, gp), pw_im.reshape(1, gp)], axis=1)
    pwc = jnp.concatenate([pw_re.reshape(1, gp), -pw_im.reshape(1, gp)], axis=1)
    lam8, lamc8 = jnp.broadcast_to(lam, (8, 2 * gp)), jnp.broadcast_to(lamc, (8, 2 * gp))

    u_perm, u_bf = _permute_u(z_u, 0, te)
    start_f = _ssm_carries("ssm_carries_fwd", u_bf, bd, "nn", lam8, pw, False)
    h_all, y_raw = _ssm_fwd(u_bf, bd, cd, lam8, start_f)
    dsk = d_skip.reshape(1, SSM_W)
    ys, yg_bf = _ssm_out(y_raw, u_perm, dsk, te)
    W_ap, W_ga, W_gb, W_out, W_fg, W_fu, W_fd, W_pg, W_pp = _exchange_wait(
        "gather_rest_wait", ag, list(range(1, len(wnames))), kinds, sizes, True, yg_bf)
    W_fg = jnp.swapaxes(W_fg, 0, 1).reshape(d, ff)
    W_fu = jnp.swapaxes(W_fu, 0, 1).reshape(d, ff)

    glu_w = _bs((SSM_W, d), lambda i, j, k: (0, 0))
    row_s = _bs((tm, SSM_W), lambda i, j, k: (i, 0))
    gate_a = _bs((te, d), lambda i, j, k: (i, 0))
    gate_s = _bs((te, d), lambda i, j, k: (i, 1))
    td_f32, td_bf = SDS((t, d), F32), SDS((t, d), BF16)
    m_bf, ya, yb, attn_d = _mm(
        "glu_merge", (t // tm, 1, 1),
        [("nn", yg_bf, row_s, W_ga, glu_w), ("nn", yg_bf, row_s, W_gb, glu_w), ("nn", attn_bf, row_s, W_ap, glu_w)],
        [(td_bf, row_d)] * 4, extras=[(zg, row_d), (zg, _bs((tm, d), lambda i, j, k: (i, 1)))], epilogue=_glu_merge_epilogue)

    h1, n2 = _mm("out_proj", (t // tm, 1, 1), [("nn", m_bf, row_d, W_out, sq_w)], [(td_f32, row_d), (td_bf, row_d)],
                 extras=[(x2, row_d), (g_ffn, vec_d)], epilogue=_out_norm_epilogue)

    tn_f = ff // 2
    nf = ff // tn_f
    hid_o = _bs((tm, tn_f), lambda j, i, k: (i, j))
    tf_bf = SDS((t, ff), BF16)
    a_rows = _bs((tm, d), lambda j, i, k: (i, 0))
    w_cols = _bs((d, tn_f), lambda j, i, k: (0, j))
    act, fg, fu = _mm("ffn_gate_up", (nf, t // tm, 1), [("nn", n2, a_rows, W_fg, w_cols), ("nn", n2, a_rows, W_fu, w_cols)],
                      [(tf_bf, hid_o)] * 3, epilogue=_swiglu_epilogue)
    h2, h2_bf = _mm("ffn_down", (t // tm, 1, nf),
                    [("nn", act, _bs((tm, tn_f), lambda i, j, k: (i, k)), W_fd, _bs((tn_f, d), lambda i, j, k: (k, 0)))],
                    [(td_f32, row_d), (td_bf, row_d)], extras=[(h1, row_d)])

    loss_part, dg_final, dh3, dpp_bf, dpg_bf = _mm(
        "ple_head", (t // te, 1, 1),
        [("nn", h2_bf, row_e, W_pg, sq_w), ("nn", p2, _bs((te, ple), lambda i, j, k: (i, 0)), W_pp, _bs((ple, d), lambda i, j, k: (0, 0)))],
        [(SDS((1, 1), F32), _bs((1, 1), lambda i, j, k: (0, 0))), (SDS((1, d), F32), vec_d), (td_f32, row_e), (td_bf, row_e),
         (td_bf, row_e)],
        extras=[(h2, row_e), (g_final.reshape(1, d), vec_d), (tgt, row_e)], epilogue=_head_epilogue(t // te),
        scratch=[pltpu.VMEM((1, d), F32)])
    loss = lax.psum(loss_part[0, 0], ("x", "y", "c"))

    nkt = t // tk
    tok_a = lambda w: _bs((tk, w), lambda i, j, k: (k, 0))

    def wgrad(name, a, wa, b, wb):
        return _mm(name, (1, 1, nkt), [("tn", a, tok_a(wa), b, tok_a(wb))],
                   [(SDS((wa, wb), BF16), _bs((wa, wb), lambda i, j, k: (0, 0)))])[0]

    dW_pp = wgrad("dw_ple_proj", p2, ple, dpp_bf, d)
    dW_pg = wgrad("dw_ple_gate", h2_bf, d, dpg_bf, d)
    dh2, dh2_bf = _mm("d_ple_gate", (t // tm, 1, 1), [("nt", dpg_bf, row_d, W_pg, sq_w)], [(td_f32, row_d), (td_bf, row_d)],
                      extras=[(dh3, row_d)])

    dfg_bf, dfu_bf = _mm("d_ffn_down", (nf, t // tm, 1),
                         [("nt", dh2_bf, a_rows, W_fd, _bs((tn_f, d), lambda j, i, k: (j, 0)))],
                         [(tf_bf, hid_o), (tf_bf, hid_o)], extras=[(fg, hid_o), (fu, hid_o)], epilogue=_swiglu_bwd_epilogue)
    dW_fd, = _mm("dw_ffn_down", (nf, 1, nkt), [("tn", act, _bs((tk, tn_f), lambda i, j, k: (k, i)), dh2_bf, tok_a(d))],
                 [(SDS((ff, d), BF16), _bs((tn_f, d), lambda i, j, k: (i, 0)))])
    hid_t = _bs((tk, tn_f), lambda i, j, k: (k, j))
    wg_o = [(SDS((d, ff), BF16), _bs((d, tn_f), lambda i, j, k: (0, j)))]
    dW_fg, = _mm("dw_ffn_gate", (1, nf, nkt), [("tn", n2, tok_a(d), dfg_bf, hid_t)], wg_o)
    dW_fu, = _mm("dw_ffn_up", (1, nf, nkt), [("tn", n2, tok_a(d), dfu_bf, hid_t)], wg_o)
    dW_fg = jnp.swapaxes(dW_fg.reshape(d, N_DEV, fs), 0, 1)
    dW_fu = jnp.swapaxes(dW_fu.reshape(d, N_DEV, fs), 0, 1)
    group = lambda names: ([kinds[wnames.index(n)] for n in names], [sizes[wnames.index(n)] for n in names])
    ffn_names = ("w_ffn_gate", "w_ffn_up", "w_ffn_down", "w_ple_gate", "w_ple_proj")
    rs_ffn = _exchange_start("scatter_ffn_start", [dW_fg, dW_fu, dW_fd, dW_pg, dW_pp], *group(ffn_names), False)
    hid_all = _bs((te, ff), lambda i, j, k: (i, 0))
    w_all = pl.BlockSpec((d, ff), lambda i, j, k: (0, 0), pipeline_mode=pl.Buffered(1))
    dh1, dh1_bf, dg_ffn = _mm("d_ffn_gate_up", (t // te, 1, 1),
                              [("nt", dfg_bf, hid_all, W_fg, w_all), ("nt", dfu_bf, hid_all, W_fu, w_all)],
                              [(td_f32, row_e), (td_bf, row_e), (SDS((1, d), F32), vec_d)],
                              extras=[(h1, row_e), (g_ffn, vec_d), (dh2, row_e)], epilogue=_rms_bwd_epilogue, after=rs_ffn[3])

    dW_out = wgrad("dw_out", m_bf, d, dh1_bf, d)
    dz_g, dad_bf, dya_bf, dyb_bf = _mm(
        "d_out_proj", (t // te, 1, 1), [("nt", dh1_bf, row_e, W_out, sq_w)],
        [(SDS((t, 2 * d), BF16), _bs((te, 2 * d), lambda i, j, k: (i, 0))), (td_bf, row_e), (td_bf, row_e), (td_bf, row_e)],
        extras=[(zg, gate_a), (zg, gate_s), (attn_d, row_e), (ya, row_e), (yb, row_e)], epilogue=_merge_bwd_epilogue)

    d_yg, = _mm("d_glu", (t // tm, 1, 1), [("nt", dya_bf, row_d, W_ga, glu_w), ("nt", dyb_bf, row_d, W_gb, glu_w)],
                [(SDS((t, SSM_W), F32), row_s)])
    dW_ga = wgrad("dw_glu_a", yg_bf, SSM_W, dya_bf, d)
    dW_gb = wgrad("dw_glu_b", yg_bf, SSM_W, dyb_bf, d)
    dys, dys_bf, dd_skip = _ssm_out_bwd(d_yg, ys, u_perm, te)
    start_b = _ssm_carries("ssm_carries_bwd", dys_bf, cd, "nt", lamc8, pwc, True)
    du_raw, dlam8, dbd, dcd = _ssm_bwd(dys_bf, u_bf, h_all, bd, cd, lamc8, start_b)
    dz_u = _du_to_dz(du_raw, dys, dsk, te)
    dlam = jnp.sum(dlam8, axis=0)
    dbb = _diag_blocks(dbd.reshape(2, SSM_NB, BLK, 512), SSM_GROUP, SSM_STATE)
    dbb_re = jnp.swapaxes(dbb[0], 1, 2).reshape(gp, SSM_GROUP)
    dbb_im = jnp.swapaxes(dbb[1], 1, 2).reshape(gp, SSM_GROUP)
    dcc = _diag_blocks(dcd.reshape(2, SSM_NB, 512, BLK), SSM_STATE, SSM_GROUP)
    dc_re, dc_im = jnp.swapaxes(dcc[0], 1, 2), -jnp.swapaxes(dcc[1], 1, 2)
    db_re, db_im, dz_re, dz_im = _ssm_scale_b_bwd(z_re.reshape(gp, 1), z_im.reshape(gp, 1), b_re2, b_im2, dbb_re, dbb_im)
    gshape = (SSM_GROUPS, SSM_STATE)
    da_re, da_im, dlog_dt = _ssm_disc_bwd(a_re[0], a_im[0], log_dt.reshape(SSM_GROUPS, 1), dlam[:gp].reshape(gshape),
                                          dlam[gp:].reshape(gshape), dz_re.reshape(gshape), dz_im.reshape(gshape))

    d_attn, = _mm("d_attn_proj", (t // tm, 1, 1), [("nt", dad_bf, row_d, W_ap, glu_w)], [(SDS((t, GROUP_W), F32), row_s)])
    dW_ap = wgrad("dw_attn_proj", attn_bf, GROUP_W, dad_bf, d)
    pre = _attn_bwd_pre(d_attn, attn, te)
    das, deltas = pre[:N_GROUPS], pre[N_GROUPS:]
    dqkvs = [_attn_bwd(qkv[g], das[g], lts[g], deltas[g], dil, min(512, t // dil)) for g, dil in enumerate(DILATIONS)]
    dz_qkv = _undilate_rope_bwd(dqkvs, pos, invf, tm)

    dW_in, = _mm("dw_in_qkv", (1, 3, nkt), [("tn", n1, tok_a(d), dz_qkv, _bs((tk, QK_W), lambda i, j, k: (k, j)))],
                 [(SDS((d, inw), BF16), _bs((d, QK_W), lambda i, j, k: (0, j)))])
    dW_in, = _mm("dw_in_u", (1, 1, nkt), [("tn", n1, tok_a(d), dz_u, tok_a(SSM_W))],
                 [(SDS((d, inw), BF16), _bs((d, SSM_W), lambda i, j, k: (0, ucol)))], alias_to_out0=dW_in)
    dW_in, = _mm("dw_in_gates", (1, 2, nkt), [("tn", n1, tok_a(d), dz_g, _bs((tk, d), lambda i, j, k: (k, j)))],
                 [(SDS((d, inw), BF16), _bs((d, d), lambda i, j, k: (0, gcol + j)))], alias_to_out0=dW_in)
    rest_names = ("w_in", "w_attn_proj", "w_glu_a", "w_glu_b", "w_out")
    rs_in = _exchange_start("scatter_rest_start", [dW_in, dW_ap, dW_ga, dW_gb, dW_out], *group(rest_names), False)
    w_piece = lambda w, cb: pl.BlockSpec((d, w), lambda i, j, k: (0, cb), pipeline_mode=pl.Buffered(1))
    dx, dg_mix = _mm(
        "d_z_proj", (t // te, 1, 1),
        [("nt", dz_qkv, _bs((te, 3 * QK_W), lambda i, j, k: (i, 0)), W_in, w_piece(3 * QK_W, 0)),
         ("nt", dz_u, _bs((te, SSM_W), lambda i, j, k: (i, 0)), W_in, w_piece(SSM_W, ucol)),
         ("nt", dz_g, _bs((te, d), lambda i, j, k: (i, 0)), W_in, w_piece(d, gcol)),
         ("nt", dz_g, _bs((te, d), lambda i, j, k: (i, 1)), W_in, w_piece(d, gcol + 1))],
        [(td_f32, row_e), (SDS((1, d), F32), vec_d)],
        extras=[(x2, row_e), (g_mix, vec_d), (dh1, row_e)], epilogue=_rms_bwd_epilogue, after=rs_in[3])

    small_parts = dict(g_mix=dg_mix, a_re=da_re, a_im=da_im, log_dt=dlog_dt, b_re=db_re, b_im=db_im, c_re=dc_re, c_im=dc_im,
                       d_skip=dd_skip, g_ffn=dg_ffn, g_final=dg_final)
    small = _pack_small([small_parts[n] for n in _SMALL])
    received = {}
    for names, started, label in ((ffn_names, rs_ffn, "ffn"), (rest_names, rs_in, "rest")):
        landed = _exchange_wait(f"scatter_{label}_wait", started, list(range(len(names))), *group(names), False, dx)
        received.update(zip(names, landed))

    new = {}
    for n in wnames:
        new[n] = [o.reshape(args[n].shape)
                  for o in _adamw("adamw_" + n, received[n], args[n][0], args["m_" + n][0], args["v_" + n][0])]
    pk = lambda pre: _pack_small([args[pre + n] for n in _SMALL])
    sm = _adamw("adamw_small", _gather_small(small), pk(""), pk("m_"), pk("v_"))
    shapes = [args[n].shape for n in _SMALL]
    for n, vals in zip(_SMALL, zip(*[_unpack_small(o, shapes) for o in sm])):
        new[n] = list(vals)

    order = ("g_mix", "w_in", "a_re", "a_im", "log_dt", "b_re", "b_im", "c_re", "c_im", "d_skip", "w_attn_proj", "w_glu_a",
             "w_glu_b", "w_out", "g_ffn", "w_ffn_gate", "w_ffn_up", "w_ffn_down", "w_ple_gate", "w_ple_proj", "g_final")
    return (loss, dx.reshape(x.shape), *[new[n][0] for n in order], *[new[n][1] for n in order],
            *[new[n][2] for n in order], *[new[n][3] for n in order])
```

```python
import functools
import math

import jax
import jax.numpy as jnp
from jax import lax
from jax.experimental import pallas as pl
from jax.experimental.pallas import tpu as pltpu

F32 = jnp.float32
BF16 = jnp.bfloat16
SDS = jax.ShapeDtypeStruct

N_DEV = 8
HEAD_DIM = 128
HEADS_PER_GROUP = 4
GROUP_W = HEADS_PER_GROUP * HEAD_DIM
DILATIONS = (1, 4, 16)
N_GROUPS = len(DILATIONS)
QK_W = N_GROUPS * GROUP_W
BLK = 128
ROPE_THETA = 500000.0
ROPE_DIM = HEAD_DIM // 4
ROPE_HALF = ROPE_DIM // 2
SSM_W = 512
SSM_GROUP = 16
SSM_GROUPS = SSM_W // SSM_GROUP
SSM_STATE = 64
NSTATE = SSM_GROUPS * SSM_STATE
SSM_NB = 4
EPS = 1e-6
ADAM_LR, ADAM_B1, ADAM_B2, ADAM_EPS, ADAM_WD, ADAM_STEP = 0.001, 0.9, 0.999, 1e-08, 0.01, 10
NEG = -1e30

VMEM_LIMIT = 52 * 1024 * 1024
SCAN_ROWS = 512
SCAN_LANES = 512


def _cp(n):
    return pltpu.CompilerParams(dimension_semantics=("arbitrary",) * n, vmem_limit_bytes=VMEM_LIMIT)


def _sigmoid(x):
    return 0.5 * jnp.tanh(0.5 * x) + 0.5


_DNUMS = {"nn": (((1,), (0,)), ((), ())), "nt": (((1,), (1,)), ((), ())), "tn": (((0,), (0,)), ((), ()))}


def _bs(shape, fn):
    return pl.BlockSpec(shape, fn)


def _store_all(prods, extra_refs, out_refs, scratch_refs):
    r = prods[0]
    for p in prods[1:]:
        r = r + p
    for e in extra_refs:
        r = r + e[...]
    for o in out_refs:
        o[...] = r.astype(o.dtype)


def _mm(name, grid, pairs, outs, extras=(), epilogue=_store_all, scratch=(), alias_to_out0=None, after=None):
    nk = grid[2]
    npair = len(pairs)
    steps = [p[5] if len(p) > 5 else nk for p in pairs]

    def block(spec):
        return tuple(s for s in spec.block_shape if s is not None)

    acc_shapes = [jax.eval_shape(lambda u, v, dn=_DNUMS[p[0]]: lax.dot_general(u, v, dn, preferred_element_type=F32),
                                 SDS(block(p[2]), BF16), SDS(block(p[4]), BF16)).shape for p in pairs]
    if nk == 1:
        acc_shapes = []
    n_in = 2 * npair + len(extras) + (alias_to_out0 is not None) + (after is not None)

    def body(*refs):
        extra_refs = refs[2 * npair:2 * npair + len(extras)]
        out_refs = refs[n_in:n_in + len(outs)]
        rest = refs[n_in + len(outs):]
        acc_refs = rest[:len(acc_shapes)]
        scratch_refs = rest[len(acc_refs):]
        k = pl.program_id(2)

        def product(i):
            return lax.dot_general(refs[2 * i][...].astype(BF16), refs[2 * i + 1][...].astype(BF16), _DNUMS[pairs[i][0]],
                                   preferred_element_type=F32)

        if nk == 1:
            epilogue([product(i) for i in range(npair)], extra_refs, out_refs, scratch_refs)
            return
        for i in range(npair):
            @pl.when(k == 0)
            def _(i=i):
                acc_refs[i][...] = product(i)

            @pl.when((k > 0) & (k < steps[i]))
            def _(i=i):
                acc_refs[i][...] += product(i)

        @pl.when(k == nk - 1)
        def _():
            epilogue([a[...] for a in acc_refs], extra_refs, out_refs, scratch_refs)

    ins, in_specs = [], []
    for p in pairs:
        ins += [p[1], p[3]]
        in_specs += [p[2], p[4]]
    ins += [e[0] for e in extras]
    in_specs += [e[1] for e in extras]
    aliases = {}
    if alias_to_out0 is not None:
        aliases = {len(ins): 0}
        ins.append(alias_to_out0)
        in_specs.append(pl.BlockSpec(memory_space=pl.ANY))
    if after is not None:
        ins.append(after)
        in_specs.append(pl.BlockSpec(memory_space=pl.ANY))
    scratch_shapes = [pltpu.VMEM(s, F32) for s in acc_shapes] + list(scratch)
    return pl.pallas_call(body, grid=grid, in_specs=in_specs, out_specs=[o[1] for o in outs], out_shape=[o[0] for o in outs],
                          scratch_shapes=scratch_shapes, input_output_aliases=aliases, compiler_params=_cp(3), name=name)(*ins)


def _my_index():
    return 4 * lax.axis_index("x") + 2 * lax.axis_index("y") + lax.axis_index("c")


def _peer(d):
    mx, my, mc = lax.axis_index("x"), lax.axis_index("y"), lax.axis_index("c")
    return (mx ^ ((d >> 2) & 1), my ^ ((d >> 1) & 1), mc ^ (d & 1))


def _win(ref, kind, j, n):
    if kind == "all":
        return ref
    if kind == "slot":
        return ref.at[j]
    if kind == "rows":
        return ref.at[pl.ds(pl.multiple_of(j * n, 8), n)]
    return ref.at[:, pl.ds(pl.multiple_of(j * n, 128), n)]


def _win7(ref, kind, n):
    if kind == "slot":
        return ref.at[pl.ds(0, 7)]
    if kind == "rows":
        return ref.at[pl.ds(0, 7 * n)]
    return ref.at[:, pl.ds(0, 7 * n)]


def _full_shape(shard_shape, kind):
    if kind == "slot":
        return (N_DEV,) + tuple(shard_shape)
    if kind == "rows":
        return (N_DEV * shard_shape[0],) + tuple(shard_shape[1:])
    return (shard_shape[0], N_DEV * shard_shape[1])


def _shard_shape(full_shape, kind, n):
    if kind == "all":
        return tuple(full_shape)
    if kind == "slot":
        return tuple(full_shape[1:])
    if kind == "rows":
        return (n,) + tuple(full_shape[1:])
    return (full_shape[0], n)


_HBM = pl.BlockSpec(memory_space=pltpu.HBM)
_SEM = pl.BlockSpec(memory_space=pltpu.SEMAPHORE)
_DATAFLOW = pltpu.SideEffectType.DATAFLOW_SIDE_EFFECTING


def _exchange_start(name, srcs, kinds, sizes, gather):
    n = len(srcs)
    if gather:
        lands = [lax.empty(_full_shape(s.shape, k), s.dtype) for s, k in zip(srcs, kinds)]
    else:
        lands = [lax.empty((N_DEV,) + _shard_shape(s.shape, k, z), s.dtype) for s, k, z in zip(srcs, kinds, sizes)]

    def body(*refs):
        src, land = refs[:n], refs[n:2 * n]
        send_sems, recv_sems, local_sems = refs[2 * n], refs[2 * n + 1], refs[2 * n + 2]
        token = refs[4 * n + 3]
        me = _my_index()
        for a in range(n):
            _local_copy(src[a], land[a], kinds[a], sizes[a], gather, me, local_sems.at[a]).start()
        for a in range(n):
            for d in range(1, N_DEV):
                px, py, pc = _peer(d)
                if gather:
                    s_ref, d_ref = src[a], _win(land[a], kinds[a], me, sizes[a])
                else:
                    s_ref, d_ref = _win(src[a], kinds[a], 4 * px + 2 * py + pc, sizes[a]), land[a].at[me]
                pltpu.make_async_remote_copy(src_ref=s_ref, dst_ref=d_ref, send_sem=send_sems.at[a], recv_sem=recv_sems.at[a],
                                             device_id=(px, py, pc), device_id_type=pl.DeviceIdType.MESH).start()
        token[...] = jnp.zeros_like(token)

    hbm = [pltpu.with_memory_space_constraint(a, pltpu.HBM) for a in list(srcs) + lands]
    out = pl.pallas_call(
        body, name=name, in_specs=[_HBM] * (2 * n),
        out_shape=[pltpu.SemaphoreType.DMA((n,))] * 3 + [pltpu.HBM(a.shape, a.dtype) for a in hbm] + [SDS((8, 128), F32)],
        out_specs=[_SEM] * 3 + [_HBM] * (2 * n) + [pl.BlockSpec(memory_space=pltpu.VMEM)],
        input_output_aliases={i: 3 + i for i in range(2 * n)},
        compiler_params=pltpu.CompilerParams(has_side_effects=_DATAFLOW))(*hbm)
    return out[0:3], out[3:3 + n], out[3 + n:3 + 2 * n], out[-1]


def _local_copy(src, land, kind, size, gather, me, sem):
    if gather:
        return pltpu.make_async_copy(src, _win(land, kind, me, size), sem)
    return pltpu.make_async_copy(_win(src, kind, me, size), land.at[me], sem)


def _exchange_wait(name, started, which, kinds, sizes, gather, after):
    sems, srcs, lands, _ = started
    n = len(which)

    def body(*refs):
        src, land = refs[:n], refs[n:2 * n]
        send_ref, recv_ref, local_ref = refs[2 * n:2 * n + 3]
        me = _my_index()
        my_id = (lax.axis_index("x"), lax.axis_index("y"), lax.axis_index("c"))
        for i, a in enumerate(which):
            seven = _win7(land[i], kinds[a], sizes[a]) if gather else land[i].at[pl.ds(0, 7)]
            pltpu.make_async_remote_copy(src_ref=seven, dst_ref=seven, send_sem=send_ref.at[a], recv_sem=recv_ref.at[a],
                                         device_id=my_id, device_id_type=pl.DeviceIdType.MESH).wait()
            _local_copy(src[i], land[i], kinds[a], sizes[a], gather, me, local_ref.at[a]).wait()

    hbm = [srcs[a] for a in which] + [lands[a] for a in which]
    out = pl.pallas_call(
        body, name=name, in_specs=[_HBM] * (2 * n) + [_SEM] * 3 + [pl.BlockSpec(memory_space=pl.ANY)],
        out_shape=[pltpu.HBM(a.shape, a.dtype) for a in hbm], out_specs=[_HBM] * (2 * n),
        input_output_aliases={i: i for i in range(2 * n)},
        compiler_params=pltpu.CompilerParams(has_side_effects=_DATAFLOW))(*hbm, *sems, after)
    return out[n:]


def _gather_small(small):
    def body(in_ref, out_ref, send_sem, recv_sem, local_sem):
        me = _my_index()
        my_id = (lax.axis_index("x"), lax.axis_index("y"), lax.axis_index("c"))
        cp = pltpu.make_async_copy(in_ref, out_ref.at[me], local_sem)
        cp.start()
        for d in range(1, N_DEV):
            pltpu.make_async_remote_copy(src_ref=in_ref, dst_ref=out_ref.at[me], send_sem=send_sem, recv_sem=recv_sem,
                                         device_id=_peer(d), device_id_type=pl.DeviceIdType.MESH).start()
        seven = out_ref.at[pl.ds(0, 7)]
        pltpu.make_async_remote_copy(src_ref=seven, dst_ref=seven, send_sem=send_sem, recv_sem=recv_sem, device_id=my_id,
                                     device_id_type=pl.DeviceIdType.MESH).wait()
        cp.wait()

    any_spec = pl.BlockSpec(memory_space=pl.ANY)
    return pl.pallas_call(body, in_specs=[any_spec], out_specs=any_spec, out_shape=SDS((N_DEV,) + small.shape, F32),
                          scratch_shapes=[pltpu.SemaphoreType.DMA] * 3, name="gather_small")(small)


def _adamw(name, recv, w, m, v):
    rows, cols = w.shape
    tr = max(c for c in range(16, 257, 16) if rows % c == 0) if rows % 16 == 0 else rows

    def body(r_ref, w_ref, m_ref, v_ref, g_ref, d_ref, nm_ref, nv_ref):
        g = r_ref[0].astype(F32)
        for s in range(1, N_DEV):
            g = g + r_ref[s].astype(F32)
        nm = ADAM_B1 * m_ref[...] + (1.0 - ADAM_B1) * g
        nv = ADAM_B2 * v_ref[...] + (1.0 - ADAM_B2) * (g * g)
        m_hat = nm / (1.0 - ADAM_B1 ** ADAM_STEP)
        v_hat = nv / (1.0 - ADAM_B2 ** ADAM_STEP)
        g_ref[...] = g
        d_ref[...] = -ADAM_LR * (m_hat / (jnp.sqrt(v_hat) + ADAM_EPS) + ADAM_WD * w_ref[...])
        nm_ref[...] = nm
        nv_ref[...] = nv

    blk = _bs((tr, cols), lambda i: (i, 0))
    return pl.pallas_call(
        body, grid=(rows // tr,), in_specs=[_bs((N_DEV, tr, cols), lambda i: (0, i, 0)), blk, blk, blk],
        out_specs=[blk] * 4, out_shape=[SDS((rows, cols), F32)] * 4, compiler_params=_cp(1), name=name)(recv, w, m, v)


def _rms_fwd(name, x, g, tm):
    t, d = x.shape

    def body(x_ref, g_ref, n_ref):
        xv = x_ref[...]
        r = lax.rsqrt(jnp.mean(xv * xv, axis=-1, keepdims=True) + EPS)
        n_ref[...] = (xv * r * g_ref[...]).astype(BF16)

    return pl.pallas_call(body, grid=(t // tm,), in_specs=[_bs((tm, d), lambda i: (i, 0)), _bs((1, d), lambda i: (0, 0))],
                          out_specs=_bs((tm, d), lambda i: (i, 0)), out_shape=SDS((t, d), BF16), compiler_params=_cp(1),
                          name=name)(x, g)


def _accumulate_rows(ref, part):
    @pl.when(pl.program_id(0) == 0)
    def _():
        ref[...] = part

    @pl.when(pl.program_id(0) > 0)
    def _():
        ref[...] += part


def _rms_bwd_epilogue(prods, extra_refs, out_refs, scratch_refs):
    dyv = prods[0]
    for p in prods[1:]:
        dyv = dyv + p
    if len(extra_refs) > 3:
        dyv = dyv + extra_refs[3][...]
    xv = extra_refs[0][...]
    r = lax.rsqrt(jnp.mean(xv * xv, axis=-1, keepdims=True) + EPS)
    xh = xv * r
    dxh = dyv * extra_refs[1][...]
    dx = extra_refs[2][...] + r * (dxh - xh * jnp.mean(dxh * xh, axis=-1, keepdims=True))
    for o in out_refs[:-1]:
        o[...] = dx.astype(o.dtype)
    _accumulate_rows(out_refs[-1], jnp.sum(dyv * xh, axis=0, keepdims=True))


def _out_norm_epilogue(prods, extra_refs, out_refs, scratch_refs):
    h = prods[0] + extra_refs[0][...]
    r = lax.rsqrt(jnp.mean(h * h, axis=-1, keepdims=True) + EPS)
    out_refs[0][...] = h
    out_refs[1][...] = (h * r * extra_refs[1][...]).astype(BF16)


def _glu_merge_epilogue(prods, extra_refs, out_refs, scratch_refs):
    ya, yb, ad = prods
    ga, gs = extra_refs[0][...].astype(F32), extra_refs[1][...].astype(F32)
    m = _sigmoid(ga) * ad + _sigmoid(gs) * (ya * _sigmoid(yb))
    out_refs[0][...] = m.astype(BF16)
    for o, val in zip(out_refs[1:], (ya, yb, ad)):
        o[...] = val.astype(o.dtype)


def _merge_bwd_epilogue(prods, extra_refs, out_refs, scratch_refs):
    dmv = prods[0]
    d = dmv.shape[1]
    ga, gs = _sigmoid(extra_refs[0][...].astype(F32)), _sigmoid(extra_refs[1][...].astype(F32))
    adv, yav = extra_refs[2][...].astype(F32), extra_refs[3][...].astype(F32)
    sb = _sigmoid(extra_refs[4][...].astype(F32))
    out_refs[0][:, 0:d] = (dmv * adv * ga * (1.0 - ga)).astype(BF16)
    out_refs[0][:, d:2 * d] = (dmv * (yav * sb) * gs * (1.0 - gs)).astype(BF16)
    dad = (dmv * ga).astype(BF16)
    dsd = dmv * gs
    dya = (dsd * sb).astype(BF16)
    dyb = (dsd * yav * sb * (1.0 - sb)).astype(BF16)
    out_refs[1][...], out_refs[2][...], out_refs[3][...] = dad, dya, dyb
    nt = _DNUMS["nt"]
    out_refs[4][...] = (lax.dot_general(dya, extra_refs[5][...], nt, preferred_element_type=F32)
                        + lax.dot_general(dyb, extra_refs[6][...], nt, preferred_element_type=F32))
    out_refs[5][...] = lax.dot_general(dad, extra_refs[7][...], nt, preferred_element_type=F32)


def _swiglu_epilogue(prods, extra_refs, out_refs, scratch_refs):
    gv, uv = prods
    out_refs[0][...] = (gv * _sigmoid(gv) * uv).astype(BF16)
    out_refs[1][...] = gv.astype(out_refs[1].dtype)
    out_refs[2][...] = uv.astype(out_refs[2].dtype)


def _swiglu_bwd_epilogue(prods, extra_refs, out_refs, scratch_refs):
    dav = prods[0]
    gv, uv = extra_refs[0][...].astype(F32), extra_refs[1][...].astype(F32)
    sg = _sigmoid(gv)
    out_refs[0][...] = (dav * uv * sg * (1.0 + gv * (1.0 - sg))).astype(BF16)
    out_refs[1][...] = (dav * gv * sg).astype(BF16)


def _head_epilogue(n_tiles):
    def epilogue(prods, extra_refs, out_refs, scratch_refs):
        pgv, ppv = prods
        d = pgv.shape[1]
        lacc = scratch_refs[0]
        sg = _sigmoid(pgv)
        h3 = extra_refs[0][...] + sg * ppv
        r = lax.rsqrt(jnp.mean(h3 * h3, axis=-1, keepdims=True) + EPS)
        xh = h3 * r
        gv = extra_refs[1][...]
        diff = xh * gv - extra_refs[2][...]
        dout = diff * (1.0 / d)
        dxh = dout * gv
        dh3 = r * (dxh - xh * jnp.mean(dxh * xh, axis=-1, keepdims=True))
        dpg = (dh3 * ppv * sg * (1.0 - sg)).astype(BF16)
        dh2 = dh3 + lax.dot_general(dpg, extra_refs[3][...], _DNUMS["nt"], preferred_element_type=F32)
        out_refs[2][...] = dh2
        out_refs[3][...] = dh2.astype(BF16)
        out_refs[4][...] = (dh3 * sg).astype(BF16)
        out_refs[5][...] = dpg
        _accumulate_rows(out_refs[1], jnp.sum(dout * xh, axis=0, keepdims=True))
        _accumulate_rows(lacc, jnp.sum(diff * diff, axis=0, keepdims=True))

        @pl.when(pl.program_id(0) == n_tiles - 1)
        def _():
            out_refs[0][...] = (0.5 / d) * jnp.sum(lacc[...], axis=-1, keepdims=True)

    return epilogue


def _strided(r, n, d):
    return pl.ds(r, n, stride=d) if d > 1 else pl.ds(0, n)


def _rope_tables(pos_ref, invf_ref, c_s, s1_s, s2_s):
    ang = pos_ref[...].astype(F32) * invf_ref[...]
    lane = lax.broadcasted_iota(jnp.int32, ang.shape, 1)
    sn = jnp.sin(ang)
    c_s[...] = jnp.where(lane < ROPE_DIM, jnp.cos(ang), 1.0)
    s1_s[...] = jnp.where(lane < ROPE_HALF, -sn, 0.0)
    s2_s[...] = jnp.where((lane >= ROPE_HALF) & (lane < ROPE_DIM), sn, 0.0)


def _rope_dilate_epilogue(tm):
    def epilogue(prods, extra_refs, out_refs, scratch_refs):
        zv = prods[0]
        pos_ref, invf_ref = extra_refs
        c_s, s1_s, s2_s, rot = scratch_refs
        c = pl.program_id(1)

        @pl.when(c == 0)
        def _():
            _rope_tables(pos_ref, invf_ref, c_s, s1_s, s2_s)

        @pl.when(c < 2)
        def _():
            cc, s1, s2 = c_s[...], s1_s[...], s2_s[...]
            for h in range(QK_W // HEAD_DIM):
                xv = zv[:, h * HEAD_DIM:(h + 1) * HEAD_DIM]
                rot[h] = xv * cc + pltpu.roll(xv, HEAD_DIM - ROPE_HALF, 1) * s1 + pltpu.roll(xv, ROPE_HALF, 1) * s2

        @pl.when(c == 2)
        def _():
            for h in range(QK_W // HEAD_DIM):
                rot[h] = zv[:, h * HEAD_DIM:(h + 1) * HEAD_DIM]

        for g, (d, o_ref) in enumerate(zip(DILATIONS, out_refs)):
            n = tm // d
            for r in range(d):
                for hh in range(HEADS_PER_GROUP):
                    oc = r * GROUP_W + hh * HEAD_DIM
                    o_ref[:, oc:oc + HEAD_DIM] = rot[g * HEADS_PER_GROUP + hh, _strided(r, n, d), :].astype(BF16)

    return epilogue


def _band_masks(first_tile):
    qi = lax.broadcasted_iota(jnp.int32, (BLK, 2 * BLK), 0)
    kj = lax.broadcasted_iota(jnp.int32, (BLK, 2 * BLK), 1)
    band = (kj >= qi) & (kj <= qi + BLK)
    return band, band & ((kj >= BLK) | jnp.logical_not(first_tile))


def _attn_fwd(qkv, d, qt):
    ell = qkv.shape[1]
    nsub = qt // BLK
    scale = 1.0 / math.sqrt(HEAD_DIM)

    def body(q_ref, kc_ref, kp_ref, vc_ref, vp_ref, o_ref, lse_ref, kcat, vcat):
        nb = pl.program_id(1)
        kcat[0:BLK, :] = kp_ref[...]
        kcat[BLK:, :] = kc_ref[...]
        vcat[0:BLK, :] = vp_ref[...]
        vcat[BLK:, :] = vc_ref[...]
        lane = lax.broadcasted_iota(jnp.int32, (BLK, HEAD_DIM), 1)
        band, band_first = _band_masks(nb == 0)
        for b in range(nsub):
            valid = band_first if b == 0 else band
            lse_t = jnp.zeros((BLK, HEAD_DIM), F32)
            for hh in range(HEADS_PER_GROUP):
                cs = slice(hh * HEAD_DIM, (hh + 1) * HEAD_DIM)
                qb = q_ref[b * BLK:(b + 1) * BLK, cs]
                kk = kcat[b * BLK:(b + 2) * BLK, cs]
                vv = vcat[b * BLK:(b + 2) * BLK, cs]
                s = lax.dot_general(qb, kk, _DNUMS["nt"], preferred_element_type=F32) * scale
                s = jnp.where(valid, s, NEG)
                mx = jnp.max(s, axis=-1, keepdims=True)
                p = jnp.exp(s - mx)
                den = jnp.sum(p, axis=-1, keepdims=True)
                o = jnp.dot(p.astype(BF16), vv, preferred_element_type=F32) / den
                o_ref[b * BLK:(b + 1) * BLK, cs] = o
                lse_t = jnp.where(lane == hh, mx + jnp.log(den), lse_t)
            lse_ref[b * BLK:(b + 1) * BLK, :] = lse_t

    cur = lambda c: _bs((None, qt, GROUP_W), lambda r, nb: (c, nb, r))
    prev = lambda c: _bs((None, BLK, GROUP_W), lambda r, nb: (c, jnp.maximum(nb * nsub - 1, 0), r))
    return pl.pallas_call(
        body, grid=(d, ell // qt), in_specs=[cur(0), cur(1), prev(1), cur(2), prev(2)],
        out_specs=[_bs((qt, GROUP_W), lambda r, nb: (nb, r)), _bs((None, qt, HEAD_DIM), lambda r, nb: (r, nb, 0))],
        out_shape=[SDS((ell, d * GROUP_W), F32), SDS((d, ell, HEAD_DIM), F32)],
        scratch_shapes=[pltpu.VMEM((qt + BLK, GROUP_W), BF16)] * 2, compiler_params=_cp(2), name=f"attn_fwd_d{d}")(
            qkv, qkv, qkv, qkv, qkv)


def _attn_merge(outs, lses, tm):
    t = outs[0].shape[0]

    def body(o0, o1, o2, l0, l1, l2, attn_ref, attn_bf_ref, t0, t1, t2, so, sl, lt_s):
        for g, (d, o_ref, l_ref) in enumerate(zip(DILATIONS, (o0, o1, o2), (l0, l1, l2))):
            n = tm // d
            for r in range(d):
                rows = _strided(r, n, d)
                for hh in range(HEADS_PER_GROUP):
                    oc = r * GROUP_W + hh * HEAD_DIM
                    so[g * HEADS_PER_GROUP + hh, rows, :] = o_ref[:, oc:oc + HEAD_DIM]
                sl[g, rows, :] = l_ref[r]
        ls = [sl[g] for g in range(N_GROUPS)]
        mx = jnp.maximum(jnp.maximum(ls[0], ls[1]), ls[2])
        es = [jnp.exp(l - mx) for l in ls]
        den = es[0] + es[1] + es[2]
        ws = [e / den for e in es]
        lt_s[...] = mx + jnp.log(den)
        for hh in range(HEADS_PER_GROUP):
            cs = slice(hh * HEAD_DIM, (hh + 1) * HEAD_DIM)
            a = ws[0][:, hh:hh + 1] * so[hh]
            for g in range(1, N_GROUPS):
                a = a + ws[g][:, hh:hh + 1] * so[g * HEADS_PER_GROUP + hh]
            attn_ref[:, cs] = a
            attn_bf_ref[:, cs] = a.astype(BF16)
        for d, t_ref in zip(DILATIONS, (t0, t1, t2)):
            n = tm // d
            for r in range(d):
                t_ref[r] = lt_s[_strided(r, n, d), :]

    dil = lambda d: _bs((tm // d, d * GROUP_W), lambda i: (i, 0))
    lsp = lambda d: _bs((d, tm // d, HEAD_DIM), lambda i: (0, i, 0))
    row = _bs((tm, GROUP_W), lambda i: (i, 0))
    return pl.pallas_call(
        body, grid=(t // tm,),
        in_specs=[dil(d) for d in DILATIONS] + [lsp(d) for d in DILATIONS],
        out_specs=[row, row] + [lsp(d) for d in DILATIONS],
        out_shape=[SDS((t, GROUP_W), F32), SDS((t, GROUP_W), BF16)] + [SDS(l.shape, F32) for l in lses],
        scratch_shapes=[pltpu.VMEM((N_GROUPS * HEADS_PER_GROUP, tm, HEAD_DIM), F32), pltpu.VMEM((N_GROUPS, tm, HEAD_DIM), F32),
                        pltpu.VMEM((tm, HEAD_DIM), F32)],
        compiler_params=_cp(1), name="attn_merge")(*outs, *lses)


def _attn_bwd_pre(d_attn, attn, tm):
    t = attn.shape[0]

    def body(da_ref, a_ref, g0, g1, g2, e0, e1, e2, dl_s, da_s):
        lane = lax.broadcasted_iota(jnp.int32, (tm, HEAD_DIM), 1)
        dl = jnp.zeros((tm, HEAD_DIM), F32)
        for hh in range(HEADS_PER_GROUP):
            cs = slice(hh * HEAD_DIM, (hh + 1) * HEAD_DIM)
            dav = da_ref[:, cs]
            da_s[hh] = dav
            dl = jnp.where(lane == hh, jnp.sum(dav * a_ref[:, cs], axis=-1, keepdims=True), dl)
        dl_s[...] = dl
        for d, g_ref, e_ref in zip(DILATIONS, (g0, g1, g2), (e0, e1, e2)):
            n = tm // d
            for r in range(d):
                rows = _strided(r, n, d)
                for hh in range(HEADS_PER_GROUP):
                    oc = r * GROUP_W + hh * HEAD_DIM
                    g_ref[:, oc:oc + HEAD_DIM] = da_s[hh, rows, :].astype(BF16)
                e_ref[r] = dl_s[rows, :]

    row = _bs((tm, GROUP_W), lambda i: (i, 0))
    return pl.pallas_call(
        body, grid=(t // tm,), in_specs=[row, row],
        out_specs=[_bs((tm // d, d * GROUP_W), lambda i: (i, 0)) for d in DILATIONS]
        + [_bs((d, tm // d, HEAD_DIM), lambda i: (0, i, 0)) for d in DILATIONS],
        out_shape=[SDS((t // d, d * GROUP_W), BF16) for d in DILATIONS]
        + [SDS((d, t // d, HEAD_DIM), F32) for d in DILATIONS],
        scratch_shapes=[pltpu.VMEM((tm, HEAD_DIM), F32), pltpu.VMEM((HEADS_PER_GROUP, tm, HEAD_DIM), F32)],
        compiler_params=_cp(1), name="attn_bwd_pre")(d_attn, attn)


def _attn_bwd(qkv, d_a, lt, delta, d, qt):
    ell = qkv.shape[1]
    nsub = qt // BLK
    ntile = ell // qt
    nblk = ell // BLK
    scale = 1.0 / math.sqrt(HEAD_DIM)

    def body(q_ref, qn_ref, kc_ref, kp_ref, vc_ref, vp_ref, da_ref, dan_ref, lt_ref, ltn_ref, dl_ref, dln_ref, o_ref,
             kcat, vcat, dk_acc, dv_acc):
        nb = pl.program_id(1)
        kcat[0:BLK, :] = kp_ref[...]
        kcat[BLK:, :] = kc_ref[...]
        vcat[0:BLK, :] = vp_ref[...]
        vcat[BLK:, :] = vc_ref[...]
        qi = lax.broadcasted_iota(jnp.int32, (BLK, BLK), 0)
        kj = lax.broadcasted_iota(jnp.int32, (BLK, BLK), 1)
        valid_next = (kj >= qi) & (nb < ntile - 1)
        band, band_first = _band_masks(nb == 0)
        for hh in range(HEADS_PER_GROUP):
            cs = slice(hh * HEAD_DIM, (hh + 1) * HEAD_DIM)
            dk_acc[...] = jnp.zeros_like(dk_acc)
            dv_acc[...] = jnp.zeros_like(dv_acc)
            for b in range(nsub):
                rs = slice(b * BLK, (b + 1) * BLK)
                ks = slice(b * BLK, (b + 2) * BLK)
                valid = band_first if b == 0 else band
                qb, kk, vv, dab = q_ref[rs, cs], kcat[ks, cs], vcat[ks, cs], da_ref[rs, cs]
                s = lax.dot_general(qb, kk, _DNUMS["nt"], preferred_element_type=F32) * scale
                p = jnp.where(valid, jnp.exp(s - lt_ref[rs, hh:hh + 1]), 0.0)
                dp = lax.dot_general(dab, vv, _DNUMS["nt"], preferred_element_type=F32)
                ds = (p * (dp - dl_ref[rs, hh:hh + 1])).astype(BF16)
                o_ref[0, rs, cs] = jnp.dot(ds, kk, preferred_element_type=F32) * scale
                dk_acc[ks, :] += lax.dot_general(ds, qb, _DNUMS["tn"], preferred_element_type=F32) * scale
                dv_acc[ks, :] += lax.dot_general(p.astype(BF16), dab, _DNUMS["tn"], preferred_element_type=F32)
            ks = slice(nsub * BLK, (nsub + 1) * BLK)
            qn, kl, vl, dan = qn_ref[:, cs], kcat[ks, cs], vcat[ks, cs], dan_ref[:, cs]
            s = lax.dot_general(qn, kl, _DNUMS["nt"], preferred_element_type=F32) * scale
            p = jnp.where(valid_next, jnp.exp(s - ltn_ref[:, hh:hh + 1]), 0.0)
            dp = lax.dot_general(dan, vl, _DNUMS["nt"], preferred_element_type=F32)
            ds = (p * (dp - dln_ref[:, hh:hh + 1])).astype(BF16)
            dk_acc[ks, :] += lax.dot_general(ds, qn, _DNUMS["tn"], preferred_element_type=F32) * scale
            dv_acc[ks, :] += lax.dot_general(p.astype(BF16), dan, _DNUMS["tn"], preferred_element_type=F32)
            o_ref[1, :, cs] = dk_acc[BLK:, :]
            o_ref[2, :, cs] = dv_acc[BLK:, :]

    nxt = lambda nb: jnp.minimum((nb + 1) * nsub, nblk - 1)
    prv = lambda nb: jnp.maximum(nb * nsub - 1, 0)
    cur3 = lambda c: _bs((None, qt, GROUP_W), lambda r, nb: (c, nb, r))
    in_specs = [
        cur3(0), _bs((None, BLK, GROUP_W), lambda r, nb: (0, nxt(nb), r)),
        cur3(1), _bs((None, BLK, GROUP_W), lambda r, nb: (1, prv(nb), r)),
        cur3(2), _bs((None, BLK, GROUP_W), lambda r, nb: (2, prv(nb), r)),
        _bs((qt, GROUP_W), lambda r, nb: (nb, r)), _bs((BLK, GROUP_W), lambda r, nb: (nxt(nb), r)),
        _bs((None, qt, HEAD_DIM), lambda r, nb: (r, nb, 0)), _bs((None, BLK, HEAD_DIM), lambda r, nb: (r, nxt(nb), 0)),
        _bs((None, qt, HEAD_DIM), lambda r, nb: (r, nb, 0)), _bs((None, BLK, HEAD_DIM), lambda r, nb: (r, nxt(nb), 0)),
    ]
    return pl.pallas_call(
        body, grid=(d, ntile), in_specs=in_specs, out_specs=_bs((3, qt, GROUP_W), lambda r, nb: (0, nb, r)),
        out_shape=SDS((3, ell, d * GROUP_W), F32),
        scratch_shapes=[pltpu.VMEM((qt + BLK, GROUP_W), BF16)] * 2 + [pltpu.VMEM((qt + BLK, HEAD_DIM), F32)] * 2,
        compiler_params=_cp(2), name=f"attn_bwd_d{d}")(qkv, qkv, qkv, qkv, qkv, qkv, d_a, d_a, lt, lt, delta, delta)


def _undilate_rope_bwd(dqkvs, pos, invf, tm):
    t = pos.shape[0]

    def body(g0, g1, g2, pos_ref, invf_ref, o_ref, c_s, s1_s, s2_s, nat):
        c = pl.program_id(1)

        @pl.when(c == 0)
        def _():
            _rope_tables(pos_ref, invf_ref, c_s, s1_s, s2_s)

        for g, (d, g_ref) in enumerate(zip(DILATIONS, (g0, g1, g2))):
            n = tm // d
            for r in range(d):
                for hh in range(HEADS_PER_GROUP):
                    oc = r * GROUP_W + hh * HEAD_DIM
                    nat[g * HEADS_PER_GROUP + hh, _strided(r, n, d), :] = g_ref[:, oc:oc + HEAD_DIM]

        @pl.when(c < 2)
        def _():
            cc, s1, s2 = c_s[...], s1_s[...], s2_s[...]
            for h in range(QK_W // HEAD_DIM):
                xv = nat[h]
                y = xv * cc - pltpu.roll(xv, HEAD_DIM - ROPE_HALF, 1) * s1 - pltpu.roll(xv, ROPE_HALF, 1) * s2
                o_ref[:, h * HEAD_DIM:(h + 1) * HEAD_DIM] = y.astype(BF16)

        @pl.when(c == 2)
        def _():
            for h in range(QK_W // HEAD_DIM):
                o_ref[:, h * HEAD_DIM:(h + 1) * HEAD_DIM] = nat[h].astype(BF16)

    return pl.pallas_call(
        body, grid=(t // tm, 3),
        in_specs=[_bs((None, tm // d, d * GROUP_W), lambda i, c: (c, i, 0)) for d in DILATIONS]
        + [_bs((tm, 1), lambda i, c: (i, 0)), _bs((1, HEAD_DIM), lambda i, c: (0, 0))],
        out_specs=_bs((tm, QK_W), lambda i, c: (i, c)), out_shape=SDS((t, 3 * QK_W), BF16),
        scratch_shapes=[pltpu.VMEM((tm, HEAD_DIM), F32)] * 3 + [pltpu.VMEM((QK_W // HEAD_DIM, tm, HEAD_DIM), F32)],
        compiler_params=_cp(2), name="undilate_rope_bwd")(*dqkvs, pos, invf)


def _cmul(ar, ai, br, bi):
    return ar * br - ai * bi, ar * bi + ai * br


def _ssm_disc(a_re, a_im, log_dt, nsq):
    def body(lr_ref, li_ref, ldt_ref, br_ref, bi_ref, zr_ref, zi_ref, pr_ref, pi_ref):
        lr, li = lr_ref[...], li_ref[...]
        dt = jnp.exp(ldt_ref[...])
        mag = jnp.exp(lr * dt)
        bar_re, bar_im = mag * jnp.cos(li * dt), mag * jnp.sin(li * dt)
        nr, ni = bar_re - 1.0, bar_im
        den = lr * lr + li * li
        br_ref[...], bi_ref[...] = bar_re, bar_im
        zr_ref[...] = (nr * lr + ni * li) / den
        zi_ref[...] = (ni * lr - nr * li) / den
        pr, pi = bar_re, bar_im
        for _ in range(nsq):
            pr, pi = _cmul(pr, pi, pr, pi)
        pr_ref[...], pi_ref[...] = pr, pi

    return pl.pallas_call(body, out_shape=[SDS(a_re.shape, F32)] * 6, name="ssm_discretise")(a_re, a_im, log_dt)


def _ssm_scale_b(z_re, z_im, b_re, b_im):
    def body(zr_ref, zi_ref, br_ref, bi_ref, or_ref, oi_ref):
        zr, zi, br, bi = zr_ref[...], zi_ref[...], br_ref[...], bi_ref[...]
        or_ref[...] = zr * br - zi * bi
        oi_ref[...] = zr * bi + zi * br

    return pl.pallas_call(body, out_shape=[SDS(b_re.shape, F32)] * 2, name="ssm_scale_b")(z_re, z_im, b_re, b_im)


def _ssm_scale_b_bwd(z_re, z_im, b_re, b_im, g_re, g_im):
    def body(zr_ref, zi_ref, br_ref, bi_ref, gr_ref, gi_ref, dbr_ref, dbi_ref, dzr_ref, dzi_ref):
        zr, zi, br, bi, gr, gi = zr_ref[...], zi_ref[...], br_ref[...], bi_ref[...], gr_ref[...], gi_ref[...]
        dbr_ref[...] = zr * gr + zi * gi
        dbi_ref[...] = zr * gi - zi * gr
        dzr_ref[...] = jnp.sum(br * gr + bi * gi, axis=-1, keepdims=True)
        dzi_ref[...] = jnp.sum(br * gi - bi * gr, axis=-1, keepdims=True)

    return pl.pallas_call(body, out_shape=[SDS(b_re.shape, F32)] * 2 + [SDS(z_re.shape, F32)] * 2,
                          name="ssm_scale_b_bwd")(z_re, z_im, b_re, b_im, g_re, g_im)


def _ssm_disc_bwd(a_re, a_im, log_dt, gb_re, gb_im, gz_re, gz_im):
    def body(lr_ref, li_ref, ldt_ref, gbr_ref, gbi_ref, gzr_ref, gzi_ref, dar_ref, dai_ref, dldt_ref):
        lr, li = lr_ref[...], li_ref[...]
        dt = jnp.exp(ldt_ref[...])
        mag = jnp.exp(lr * dt)
        bar_re, bar_im = mag * jnp.cos(li * dt), mag * jnp.sin(li * dt)
        nr, ni = bar_re - 1.0, bar_im
        den = lr * lr + li * li
        zr, zi = (nr * lr + ni * li) / den, (ni * lr - nr * li) / den
        gzr, gzi = gzr_ref[...], gzi_ref[...]
        gbr = gbr_ref[...] + (lr * gzr - li * gzi) / den
        gbi = gbi_ref[...] + (lr * gzi + li * gzr) / den
        qr, qi = (zr * lr + zi * li) / den, (zi * lr - zr * li) / den
        dar_ref[...] = dt * (bar_re * gbr + bar_im * gbi) - qr * gzr - qi * gzi
        dai_ref[...] = dt * (bar_re * gbi - bar_im * gbr) - qr * gzi + qi * gzr
        wr, wi = lr * bar_re - li * bar_im, lr * bar_im + li * bar_re
        dldt_ref[...] = dt * jnp.sum(wr * gbr + wi * gbi, axis=-1, keepdims=True)

    return pl.pallas_call(body, out_shape=[SDS(a_re.shape, F32)] * 2 + [SDS(log_dt.shape, F32)],
                          name="ssm_discretise_bwd")(a_re, a_im, log_dt, gb_re, gb_im, gz_re, gz_im)


def _permute_u(z, ucol_block, tm):
    t = z.shape[0]
    seg = t // N_DEV
    z3 = z.reshape(N_DEV, seg, z.shape[1])

    def body(z_ref, u_ref, ub_ref, tmp):
        for n in range(SSM_W // BLK):
            for j in range(N_DEV):
                tmp[n, pl.ds(j, tm // N_DEV, stride=N_DEV), :] = z_ref[j, :, n * BLK:(n + 1) * BLK]
            u_ref[:, n * BLK:(n + 1) * BLK] = tmp[n]
            ub_ref[:, n * BLK:(n + 1) * BLK] = tmp[n].astype(BF16)

    row = _bs((tm, SSM_W), lambda i: (i, 0))
    return pl.pallas_call(
        body, grid=(t // tm,), in_specs=[_bs((N_DEV, tm // N_DEV, SSM_W), lambda i: (0, i, ucol_block))],
        out_specs=[row, row], out_shape=[SDS((t, SSM_W), F32), SDS((t, SSM_W), BF16)],
        scratch_shapes=[pltpu.VMEM((SSM_W // BLK, tm, BLK), F32)], compiler_params=_cp(1), name="permute_u")(z3)


def _drive(src_ref, mat_ref, dst, mode):
    for kn in range(2 * SSM_NB):
        n = kn % SSM_NB
        a = src_ref[:, n * BLK:(n + 1) * BLK]
        dst[:, kn * 512:(kn + 1) * 512] = lax.dot_general(a, mat_ref[kn], _DNUMS[mode], preferred_element_type=F32)


def _scan_chunk(src, lam_ref, carry, *, reverse, store=None, h_ref=None, acc=None):
    steps = src.shape[0] // 8
    for c in range(NSTATE // SCAN_LANES):
        re = slice(c * SCAN_LANES, (c + 1) * SCAN_LANES)
        im = slice(NSTATE + c * SCAN_LANES, NSTATE + (c + 1) * SCAN_LANES)
        ar, ai = lam_ref[:, re], lam_ref[:, im]

        def step(s, val):
            i = (steps - 1 - s) if reverse else s
            rows = pl.ds(pl.multiple_of(i * 8, 8), 8)
            if acc is not None:
                hr, hi, dr, di = val
                pr, pi = h_ref[rows, re], h_ref[rows, im]
                dr = dr + hr * pr + hi * pi
                di = di + hi * pr - hr * pi
            else:
                hr, hi = val
            nr = ar * hr - ai * hi + src[rows, re]
            ni = ar * hi + ai * hr + src[rows, im]
            if store is not None:
                store[rows, re] = nr
                store[rows, im] = ni
            return (nr, ni, dr, di) if acc is not None else (nr, ni)

        init = (carry[:, re], carry[:, im])
        if acc is not None:
            init = init + (acc[:, re], acc[:, im])
        out = lax.fori_loop(0, steps, step, init, unroll=4)
        carry[:, re], carry[:, im] = out[0], out[1]
        if acc is not None:
            acc[:, re], acc[:, im] = out[2], out[3]


def _segment_carries(e_ref, pw_ref, out_ref, reverse):
    pr, pi = pw_ref[:, 0:NSTATE], pw_ref[:, NSTATE:]
    hr = jnp.zeros((1, NSTATE), F32)
    hi = jnp.zeros((1, NSTATE), F32)
    order = range(N_DEV - 1, -1, -1) if reverse else range(N_DEV)
    for j in order:
        out_ref[j:j + 1, 0:NSTATE] = hr
        out_ref[j:j + 1, NSTATE:] = hi
        tr, ti = _cmul(pr, pi, hr, hi)
        hr, hi = e_ref[j:j + 1, 0:NSTATE] + tr, e_ref[j:j + 1, NSTATE:] + ti


def _ssm_carries(name, src, mat, mode, lam8, pw, reverse):
    t = src.shape[0]
    nchunk = t // SCAN_ROWS

    def body(src_ref, mat_ref, lam_ref, pw_ref, out_ref, drive, carry):
        c = pl.program_id(0)

        @pl.when(c == 0)
        def _():
            carry[...] = jnp.zeros_like(carry)

        _drive(src_ref, mat_ref, drive, mode)
        _scan_chunk(drive, lam_ref, carry, reverse=reverse)

        @pl.when(c == nchunk - 1)
        def _():
            _segment_carries(carry, pw_ref, out_ref, reverse)

    blk = (lambda c: (nchunk - 1 - c, 0)) if reverse else (lambda c: (c, 0))
    return pl.pallas_call(
        body, grid=(nchunk,),
        in_specs=[_bs((SCAN_ROWS, SSM_W), blk), _bs(mat.shape, lambda c: (0, 0, 0)), _bs((8, 2 * NSTATE), lambda c: (0, 0)),
                  _bs((1, 2 * NSTATE), lambda c: (0, 0))],
        out_specs=_bs((8, 2 * NSTATE), lambda c: (0, 0)), out_shape=SDS((8, 2 * NSTATE), F32),
        scratch_shapes=[pltpu.VMEM((SCAN_ROWS, 2 * NSTATE), F32), pltpu.VMEM((8, 2 * NSTATE), F32)],
        compiler_params=_cp(1), name=name)(src, mat, lam8, pw)


def _ssm_fwd(u_bf, bd, cd, lam8, start):
    t = u_bf.shape[0]
    nchunk = t // SCAN_ROWS

    def body(u_ref, bd_ref, cd_ref, lam_ref, start_ref, h_ref, y_ref, drive, carry):
        @pl.when(pl.program_id(0) == 0)
        def _():
            carry[...] = start_ref[...]

        _drive(u_ref, bd_ref, drive, "nn")
        _scan_chunk(drive, lam_ref, carry, reverse=False, store=h_ref)
        for n in range(SSM_NB):
            hr = h_ref[:, n * 512:(n + 1) * 512].astype(BF16)
            hi = h_ref[:, NSTATE + n * 512:NSTATE + (n + 1) * 512].astype(BF16)
            y_ref[:, n * BLK:(n + 1) * BLK] = (jnp.dot(hr, cd_ref[n], preferred_element_type=F32)
                                              + jnp.dot(hi, cd_ref[SSM_NB + n], preferred_element_type=F32))

    return pl.pallas_call(
        body, grid=(nchunk,),
        in_specs=[_bs((SCAN_ROWS, SSM_W), lambda c: (c, 0)), _bs(bd.shape, lambda c: (0, 0, 0)), _bs(cd.shape, lambda c: (0, 0, 0)),
                  _bs((8, 2 * NSTATE), lambda c: (0, 0)), _bs((8, 2 * NSTATE), lambda c: (0, 0))],
        out_specs=[_bs((SCAN_ROWS, 2 * NSTATE), lambda c: (c, 0)), _bs((SCAN_ROWS, SSM_W), lambda c: (c, 0))],
        out_shape=[SDS((t, 2 * NSTATE), F32), SDS((t, SSM_W), F32)],
        scratch_shapes=[pltpu.VMEM((SCAN_ROWS, 2 * NSTATE), F32), pltpu.VMEM((8, 2 * NSTATE), F32)],
        compiler_params=_cp(1), name="ssm_scan_fwd")(u_bf, bd, cd, lam8, start)


def _ssm_bwd(dys_bf, u_bf, h, bd, cd, lamc8, start):
    t = u_bf.shape[0]
    nchunk = t // SCAN_ROWS

    def body(dys_ref, u_ref, h_ref, bd_ref, cd_ref, lam_ref, start_ref, du_ref, dlam_ref, dbd_ref, dcd_ref, drive, adj, carry):
        c = pl.program_id(0)

        @pl.when(c == 0)
        def _():
            carry[...] = start_ref[...]
            dlam_ref[...] = jnp.zeros_like(dlam_ref)
            dbd_ref[...] = jnp.zeros_like(dbd_ref)
            dcd_ref[...] = jnp.zeros_like(dcd_ref)

        _drive(dys_ref, cd_ref, drive, "nt")
        _scan_chunk(drive, lam_ref, carry, reverse=True, store=adj, h_ref=h_ref, acc=dlam_ref)
        for n in range(SSM_NB):
            cs = slice(n * BLK, (n + 1) * BLK)
            acc = None
            for k in range(2):
                kn = k * SSM_NB + n
                ss = slice(kn * 512, (kn + 1) * 512)
                lam_b = adj[:, ss].astype(BF16)
                part = lax.dot_general(lam_b, bd_ref[kn], _DNUMS["nt"], preferred_element_type=F32)
                acc = part if acc is None else acc + part
                dbd_ref[kn] += lax.dot_general(u_ref[:, cs], lam_b, _DNUMS["tn"], preferred_element_type=F32)
                dcd_ref[kn] += lax.dot_general(h_ref[:, ss].astype(BF16), dys_ref[:, cs], _DNUMS["tn"],
                                               preferred_element_type=F32)
            du_ref[:, cs] = acc

    rev = lambda c: (nchunk - 1 - c, 0)
    const2 = lambda c: (0, 0)
    const3 = lambda c: (0, 0, 0)
    return pl.pallas_call(
        body, grid=(nchunk,),
        in_specs=[_bs((SCAN_ROWS, SSM_W), rev), _bs((SCAN_ROWS, SSM_W), rev), _bs((SCAN_ROWS, 2 * NSTATE), rev),
                  _bs(bd.shape, const3), _bs(cd.shape, const3), _bs((8, 2 * NSTATE), const2), _bs((8, 2 * NSTATE), const2)],
        out_specs=[_bs((SCAN_ROWS, SSM_W), rev), _bs((8, 2 * NSTATE), const2), _bs(bd.shape, const3), _bs(cd.shape, const3)],
        out_shape=[SDS((t, SSM_W), F32), SDS((8, 2 * NSTATE), F32), SDS(bd.shape, F32), SDS(cd.shape, F32)],
        scratch_shapes=[pltpu.VMEM((SCAN_ROWS, 2 * NSTATE), F32), pltpu.VMEM((SCAN_ROWS, 2 * NSTATE), F32),
                        pltpu.VMEM((8, 2 * NSTATE), F32)],
        compiler_params=_cp(1), name="ssm_scan_bwd")(dys_bf, u_bf, h, bd, cd, lamc8, start)


def _gelu_parts(x):
    c0 = math.sqrt(2.0 / math.pi)
    inner = c0 * (x + 0.044715 * x * x * x)
    th = jnp.tanh(inner)
    val = 0.5 * x * (1.0 + th)
    grad = 0.5 * (1.0 + th) + 0.5 * x * (1.0 - th * th) * c0 * (1.0 + 3.0 * 0.044715 * x * x)
    return val, grad


def _ssm_out(y_raw, u, d_skip, tm):
    t = u.shape[0]
    seg = t // N_DEV

    def body(y_ref, u_ref, d_ref, ys_ref, yg_ref, tmp):
        ys = y_ref[...] + d_ref[...] * u_ref[...]
        ys_ref[...] = ys
        yg = _gelu_parts(ys)[0]
        for n in range(SSM_W // BLK):
            tmp[n] = yg[:, n * BLK:(n + 1) * BLK]
            for j in range(N_DEV):
                yg_ref[j, :, n * BLK:(n + 1) * BLK] = tmp[n, pl.ds(j, tm // N_DEV, stride=N_DEV), :].astype(BF16)

    row = _bs((tm, SSM_W), lambda i: (i, 0))
    ys, yg = pl.pallas_call(
        body, grid=(t // tm,), in_specs=[row, row, _bs((1, SSM_W), lambda i: (0, 0))],
        out_specs=[row, _bs((N_DEV, tm // N_DEV, SSM_W), lambda i: (0, i, 0))],
        out_shape=[SDS((t, SSM_W), F32), SDS((N_DEV, seg, SSM_W), BF16)],
        scratch_shapes=[pltpu.VMEM((SSM_W // BLK, tm, BLK), F32)], compiler_params=_cp(1), name="ssm_out")(y_raw, u, d_skip)
    return ys, yg.reshape(t, SSM_W)


def _ssm_out_bwd(d_yg, ys, u, tm):
    t = u.shape[0]
    seg = t // N_DEV

    def body(dg_ref, ys_ref, u_ref, dys_ref, dysb_ref, dd_ref, tmp):
        for n in range(SSM_W // BLK):
            for j in range(N_DEV):
                tmp[n, pl.ds(j, tm // N_DEV, stride=N_DEV), :] = dg_ref[j, :, n * BLK:(n + 1) * BLK]
        dyg = jnp.concatenate([tmp[n] for n in range(SSM_W // BLK)], axis=1)
        dys = dyg * _gelu_parts(ys_ref[...])[1]
        dys_ref[...] = dys
        dysb_ref[...] = dys.astype(BF16)
        part = jnp.sum(dys * u_ref[...], axis=0, keepdims=True)

        @pl.when(pl.program_id(0) == 0)
        def _():
            dd_ref[...] = part

        @pl.when(pl.program_id(0) > 0)
        def _():
            dd_ref[...] += part

    row = _bs((tm, SSM_W), lambda i: (i, 0))
    return pl.pallas_call(
        body, grid=(t // tm,), in_specs=[_bs((N_DEV, tm // N_DEV, SSM_W), lambda i: (0, i, 0)), row, row],
        out_specs=[row, row, _bs((1, SSM_W), lambda i: (0, 0))],
        out_shape=[SDS((t, SSM_W), F32), SDS((t, SSM_W), BF16), SDS((1, SSM_W), F32)],
        scratch_shapes=[pltpu.VMEM((SSM_W // BLK, tm, BLK), F32)], compiler_params=_cp(1), name="ssm_out_bwd")(
            d_yg.reshape(N_DEV, seg, SSM_W), ys, u)


def _du_to_dz(du_raw, dys, d_skip, tm):
    t = du_raw.shape[0]
    seg = t // N_DEV

    def body(du_ref, dys_ref, d_ref, o_ref, tmp):
        du = du_ref[...] + d_ref[...] * dys_ref[...]
        for n in range(SSM_W // BLK):
            tmp[n] = du[:, n * BLK:(n + 1) * BLK]
            for j in range(N_DEV):
                o_ref[j, :, n * BLK:(n + 1) * BLK] = tmp[n, pl.ds(j, tm // N_DEV, stride=N_DEV), :].astype(BF16)

    row = _bs((tm, SSM_W), lambda i: (i, 0))
    out = pl.pallas_call(
        body, grid=(t // tm,), in_specs=[row, row, _bs((1, SSM_W), lambda i: (0, 0))],
        out_specs=_bs((N_DEV, tm // N_DEV, SSM_W), lambda i: (0, i, 0)), out_shape=SDS((N_DEV, seg, SSM_W), BF16),
        scratch_shapes=[pltpu.VMEM((SSM_W // BLK, tm, BLK), F32)], compiler_params=_cp(1), name="du_to_dz")(du_raw, dys, d_skip)
    return out.reshape(t, SSM_W)


def _block_diag(blocks):
    nb, ng, r, c = blocks.shape
    eye = jnp.eye(ng, dtype=blocks.dtype)
    return (blocks[:, :, :, None, :] * eye[None, :, None, :, None]).reshape(nb, ng * r, ng * c)


def _diag_blocks(full, r, c):
    k, nb = full.shape[:2]
    ng = full.shape[2] // r
    x = full.reshape(k, nb, ng, r, ng, c)
    eye = jnp.eye(ng, dtype=full.dtype)
    return jnp.sum(x * eye[None, None, :, None, :, None], axis=4).reshape(k, nb * ng, r, c)


_SMALL = ("g_mix", "a_re", "a_im", "log_dt", "b_re", "b_im", "c_re", "c_im", "d_skip", "g_ffn", "g_final")


def _pack_small(arrs):
    flat = jnp.concatenate([a.reshape(-1) for a in arrs])
    pad = (-flat.shape[0]) % (8 * 128)
    return jnp.pad(flat, (0, pad)).reshape(-1, 128)


def _unpack_small(packed, shapes):
    flat = packed.reshape(-1)
    out, off = [], 0
    for s in shapes:
        n = math.prod(s)
        out.append(flat[off:off + n].reshape(s))
        off += n
    return out


def kernel(x, p, positions, g_mix, w_in, a_re, a_im, log_dt, b_re, b_im, c_re, c_im, d_skip, w_attn_proj, w_glu_a, w_glu_b, w_out, g_ffn, w_ffn_gate, w_ffn_up, w_ffn_down, w_ple_gate, w_ple_proj, g_final, loss_target, m_g_mix, m_w_in, m_a_re, m_a_im, m_log_dt, m_b_re, m_b_im, m_c_re, m_c_im, m_d_skip, m_w_attn_proj, m_w_glu_a, m_w_glu_b, m_w_out, m_g_ffn, m_w_ffn_gate, m_w_ffn_up, m_w_ffn_down, m_w_ple_gate, m_w_ple_proj, m_g_final, v_g_mix, v_w_in, v_a_re, v_a_im, v_log_dt, v_b_re, v_b_im, v_c_re, v_c_im, v_d_skip, v_w_attn_proj, v_w_glu_a, v_w_glu_b, v_w_out, v_g_ffn, v_w_ffn_gate, v_w_ffn_up, v_w_ffn_down, v_w_ple_gate, v_w_ple_proj, v_g_final):
    args = dict(locals())
    t, d = x.shape[1], x.shape[2]
    inw = w_in.shape[2] * N_DEV
    fs = w_ffn_gate.shape[2]
    ff = fs * N_DEV
    ple = w_ple_proj.shape[1]
    seg = t // N_DEV
    assert inw == 3 * QK_W + SSM_W + 2 * d and t % (N_DEV * SCAN_ROWS // 8) == 0 and seg & (seg - 1) == 0
    tm = min(1024, t)
    te = min(512, t)
    tk = min(1024, t)
    ucol = (3 * QK_W) // SSM_W
    gcol = (3 * QK_W + SSM_W) // d
    assert (3 * QK_W + SSM_W) % d == 0

    x2, p2, tgt = x[0], p[0, 0], loss_target[0]
    pos = positions.reshape(t, 1)
    inv = ROPE_THETA ** (-jnp.arange(ROPE_HALF, dtype=F32) * 2.0 / ROPE_DIM)
    invf = jnp.concatenate([inv, inv, jnp.zeros((HEAD_DIM - ROPE_DIM,), F32)]).reshape(1, HEAD_DIM)

    wnames = ("w_in", "w_attn_proj", "w_glu_a", "w_glu_b", "w_out", "w_ffn_gate", "w_ffn_up", "w_ffn_down", "w_ple_gate",
              "w_ple_proj")
    kinds = ("cols", "cols", "cols", "cols", "rows", "slot", "slot", "rows", "rows", "cols")
    shards = [args[n][0].astype(BF16) for n in wnames]
    sizes = [s.shape[0] if k == "rows" else s.shape[-1] for s, k in zip(shards, kinds)]
    ag = _exchange_start("gather_weights_start", shards, kinds, sizes, True)

    row_d = _bs((tm, d), lambda i, j, k: (i, 0))
    row_e = _bs((te, d), lambda i, j, k: (i, 0))
    vec_d = _bs((1, d), lambda i, j, k: (0, 0))
    sq_w = _bs((d, d), lambda i, j, k: (0, 0))
    n1 = _rms_fwd("norm_mix", x2, g_mix + ag[3][0:1, 0:1], tm)
    W_in, = _exchange_wait("gather_w_in_wait", ag, [0], kinds, sizes, True, n1)
    qkv = _mm("qkv_proj", (t // tm, 3, 1), [("nn", n1, row_d, W_in, _bs((d, QK_W), lambda i, j, k: (0, j)))],
              [(SDS((3, t // dil, dil * GROUP_W), BF16), _bs((None, tm // dil, dil * GROUP_W), lambda i, j, k: (j, i, 0)))
               for dil in DILATIONS],
              extras=[(pos, _bs((tm, 1), lambda i, j, k: (i, 0))), (invf, _bs((1, HEAD_DIM), lambda i, j, k: (0, 0)))],
              epilogue=_rope_dilate_epilogue(tm),
              scratch=[pltpu.VMEM((tm, HEAD_DIM), F32)] * 3 + [pltpu.VMEM((QK_W // HEAD_DIM, tm, HEAD_DIM), F32)])
    z_u, = _mm("u_proj", (t // tm, 1, 1), [("nn", n1, row_d, W_in, _bs((d, SSM_W), lambda i, j, k: (0, ucol)))],
               [(SDS((t, SSM_W), F32), _bs((tm, SSM_W), lambda i, j, k: (i, 0)))])
    zg, = _mm("z_gates", (t // tm, 2, 1),
              [("nn", n1, row_d, W_in, _bs((d, d), lambda i, j, k: (0, gcol + j)))],
              [(SDS((t, 2 * d), BF16), _bs((tm, d), lambda i, j, k: (i, j)))])

    outs, lses = [], []
    for g, dil in enumerate(DILATIONS):
        o_g, l_g = _attn_fwd(qkv[g], dil, min(512, t // dil))
        outs.append(o_g)
        lses.append(l_g)
    merged = _attn_merge(outs, lses, te)
    attn, attn_bf, lts = merged[0], merged[1], merged[2:]

    nsq = seg.bit_length() - 1
    bar_re, bar_im, z_re, z_im, pw_re, pw_im = _ssm_disc(a_re[0], a_im[0], log_dt.reshape(SSM_GROUPS, 1), nsq)
    gp = SSM_GROUPS * SSM_STATE
    b_re2, b_im2 = b_re.reshape(gp, SSM_GROUP), b_im.reshape(gp, SSM_GROUP)
    bb_re, bb_im = _ssm_scale_b(z_re.reshape(gp, 1), z_im.reshape(gp, 1), b_re2, b_im2)

    def chunks(a, r, c):
        return a.reshape(SSM_NB, SSM_GROUPS // SSM_NB, r, c)

    bbt = lambda a: jnp.swapaxes(a.reshape(SSM_GROUPS, SSM_STATE, SSM_GROUP), 1, 2)
    bd = jnp.concatenate([_block_diag(chunks(bbt(bb_re), SSM_GROUP, SSM_STATE)),
                          _block_diag(chunks(bbt(bb_im), SSM_GROUP, SSM_STATE))]).astype(BF16)
    ct = lambda a: jnp.swapaxes(a[0], 1, 2)
    cd = jnp.concatenate([_block_diag(chunks(ct(c_re), SSM_STATE, SSM_GROUP)),
                          _block_diag(chunks(-ct(c_im), SSM_STATE, SSM_GROUP))]).astype(BF16)
    lam = jnp.concatenate([bar_re.reshape(1, gp), bar_im.reshape(1, gp)], axis=1)
    lamc = jnp.concatenate([bar_re.reshape(1, gp), -bar_im.reshape(1, gp)], axis=1)
    pw = jnp.concatenate([pw_re.reshape(1, gp), pw_im.reshape(1, gp)], axis=1)
    pwc = jnp.concatenate([pw_re.reshape(1, gp), -pw_im.reshape(1, gp)], axis=1)
    lam8, lamc8 = jnp.broadcast_to(lam, (8, 2 * gp)), jnp.broadcast_to(lamc, (8, 2 * gp))

    u_perm, u_bf = _permute_u(z_u, 0, te)
    start_f = _ssm_carries("ssm_carries_fwd", u_bf, bd, "nn", lam8, pw, False)
    h_all, y_raw = _ssm_fwd(u_bf, bd, cd, lam8, start_f)
    dsk = d_skip.reshape(1, SSM_W)
    ys, yg_bf = _ssm_out(y_raw, u_perm, dsk, te)
    W_ap, W_ga, W_gb, W_out, W_fg, W_fu, W_fd, W_pg, W_pp = _exchange_wait(
        "gather_rest_wait", ag, list(range(1, len(wnames))), kinds, sizes, True, yg_bf)
    W_fg = jnp.swapaxes(W_fg, 0, 1).reshape(d, ff)
    W_fu = jnp.swapaxes(W_fu, 0, 1).reshape(d, ff)

    glu_w = _bs((SSM_W, d), lambda i, j, k: (0, 0))
    row_s = _bs((tm, SSM_W), lambda i, j, k: (i, 0))
    gate_a = _bs((te, d), lambda i, j, k: (i, 0))
    gate_s = _bs((te, d), lambda i, j, k: (i, 1))
    td_f32, td_bf = SDS((t, d), F32), SDS((t, d), BF16)
    m_bf, ya, yb, attn_d = _mm(
        "glu_merge", (t // tm, 1, 1),
        [("nn", yg_bf, row_s, W_ga, glu_w), ("nn", yg_bf, row_s, W_gb, glu_w), ("nn", attn_bf, row_s, W_ap, glu_w)],
        [(td_bf, row_d)] * 4, extras=[(zg, row_d), (zg, _bs((tm, d), lambda i, j, k: (i, 1)))], epilogue=_glu_merge_epilogue)

    h1, n2 = _mm("out_proj", (t // tm, 1, 1), [("nn", m_bf, row_d, W_out, sq_w)], [(td_f32, row_d), (td_bf, row_d)],
                 extras=[(x2, row_d), (g_ffn, vec_d)], epilogue=_out_norm_epilogue)

    tn_f = ff // 2
    nf = ff // tn_f
    hid_o = _bs((tm, tn_f), lambda j, i, k: (i, j))
    tf_bf = SDS((t, ff), BF16)
    a_rows = _bs((tm, d), lambda j, i, k: (i, 0))
    w_cols = _bs((d, tn_f), lambda j, i, k: (0, j))
    act, fg, fu = _mm("ffn_gate_up", (nf, t // tm, 1), [("nn", n2, a_rows, W_fg, w_cols), ("nn", n2, a_rows, W_fu, w_cols)],
                      [(tf_bf, hid_o)] * 3, epilogue=_swiglu_epilogue)
    h2, h2_bf = _mm("ffn_down", (t // tm, 1, nf),
                    [("nn", act, _bs((tm, tn_f), lambda i, j, k: (i, k)), W_fd, _bs((tn_f, d), lambda i, j, k: (k, 0)))],
                    [(td_f32, row_d), (td_bf, row_d)], extras=[(h1, row_d)])

    w_once = pl.BlockSpec((d, d), lambda i, j, k: (0, 0), pipeline_mode=pl.Buffered(1))
    loss_part, dg_final, dh2, dh2_bf, dpp_bf, dpg_bf = _mm(
        "ple_head", (t // te, 1, 1),
        [("nn", h2_bf, row_e, W_pg, w_once), ("nn", p2, _bs((te, ple), lambda i, j, k: (i, 0)), W_pp, _bs((ple, d), lambda i, j, k: (0, 0)))],
        [(SDS((1, 1), F32), _bs((1, 1), lambda i, j, k: (0, 0))), (SDS((1, d), F32), vec_d), (td_f32, row_e), (td_bf, row_e),
         (td_bf, row_e), (td_bf, row_e)],
        extras=[(h2, row_e), (g_final.reshape(1, d), vec_d), (tgt, row_e), (W_pg, w_once)], epilogue=_head_epilogue(t // te),
        scratch=[pltpu.VMEM((1, d), F32)])
    loss = lax.psum(loss_part[0, 0], ("x", "y", "c"))

    nkt = t // tk
    tok_a = lambda w: _bs((tk, w), lambda i, j, k: (k, 0))

    def wgrad(name, a, wa, b, wb):
        return _mm(name, (1, 1, nkt), [("tn", a, tok_a(wa), b, tok_a(wb))],
                   [(SDS((wa, wb), BF16), _bs((wa, wb), lambda i, j, k: (0, 0)))])[0]

    dW_pp = wgrad("dw_ple_proj", p2, ple, dpp_bf, d)
    dW_pg = wgrad("dw_ple_gate", h2_bf, d, dpg_bf, d)
    dfg_bf, dfu_bf = _mm("d_ffn_down", (nf, t // tm, 1),
                         [("nt", dh2_bf, a_rows, W_fd, _bs((tn_f, d), lambda j, i, k: (j, 0)))],
                         [(tf_bf, hid_o), (tf_bf, hid_o)], extras=[(fg, hid_o), (fu, hid_o)], epilogue=_swiglu_bwd_epilogue)
    dW_fd, = _mm("dw_ffn_down", (nf, 1, nkt), [("tn", act, _bs((tk, tn_f), lambda i, j, k: (k, i)), dh2_bf, tok_a(d))],
                 [(SDS((ff, d), BF16), _bs((tn_f, d), lambda i, j, k: (i, 0)))])
    hid_t = _bs((tk, tn_f), lambda i, j, k: (k, j))
    wg_o = [(SDS((d, ff), BF16), _bs((d, tn_f), lambda i, j, k: (0, j)))]
    dW_fg, = _mm("dw_ffn_gate", (1, nf, nkt), [("tn", n2, tok_a(d), dfg_bf, hid_t)], wg_o)
    dW_fu, = _mm("dw_ffn_up", (1, nf, nkt), [("tn", n2, tok_a(d), dfu_bf, hid_t)], wg_o)
    dW_fg = jnp.swapaxes(dW_fg.reshape(d, N_DEV, fs), 0, 1)
    dW_fu = jnp.swapaxes(dW_fu.reshape(d, N_DEV, fs), 0, 1)
    group = lambda names: ([kinds[wnames.index(n)] for n in names], [sizes[wnames.index(n)] for n in names])
    ffn_names = ("w_ffn_gate", "w_ffn_up", "w_ffn_down", "w_ple_gate", "w_ple_proj")
    rs_ffn = _exchange_start("scatter_ffn_start", [dW_fg, dW_fu, dW_fd, dW_pg, dW_pp], *group(ffn_names), False)
    hid_all = _bs((te, ff), lambda i, j, k: (i, 0))
    w_all = pl.BlockSpec((d, ff), lambda i, j, k: (0, 0), pipeline_mode=pl.Buffered(1))
    dh1, dh1_bf, dg_ffn = _mm("d_ffn_gate_up", (t // te, 1, 1),
                              [("nt", dfg_bf, hid_all, W_fg, w_all), ("nt", dfu_bf, hid_all, W_fu, w_all)],
                              [(td_f32, row_e), (td_bf, row_e), (SDS((1, d), F32), vec_d)],
                              extras=[(h1, row_e), (g_ffn, vec_d), (dh2, row_e)], epilogue=_rms_bwd_epilogue, after=rs_ffn[3])

    dW_out = wgrad("dw_out", m_bf, d, dh1_bf, d)
    glu_once = pl.BlockSpec((SSM_W, d), lambda i, j, k: (0, 0), pipeline_mode=pl.Buffered(1))
    row_es = _bs((te, SSM_W), lambda i, j, k: (i, 0))
    ts_f32 = SDS((t, SSM_W), F32)
    dz_g, dad_bf, dya_bf, dyb_bf, d_yg, d_attn = _mm(
        "d_out_proj", (t // te, 1, 1), [("nt", dh1_bf, row_e, W_out, w_once)],
        [(SDS((t, 2 * d), BF16), _bs((te, 2 * d), lambda i, j, k: (i, 0))), (td_bf, row_e), (td_bf, row_e), (td_bf, row_e),
         (ts_f32, row_es), (ts_f32, row_es)],
        extras=[(zg, gate_a), (zg, gate_s), (attn_d, row_e), (ya, row_e), (yb, row_e), (W_ga, glu_once), (W_gb, glu_once),
                (W_ap, glu_once)], epilogue=_merge_bwd_epilogue)

    dW_ga = wgrad("dw_glu_a", yg_bf, SSM_W, dya_bf, d)
    dW_gb = wgrad("dw_glu_b", yg_bf, SSM_W, dyb_bf, d)
    dys, dys_bf, dd_skip = _ssm_out_bwd(d_yg, ys, u_perm, te)
    start_b = _ssm_carries("ssm_carries_bwd", dys_bf, cd, "nt", lamc8, pwc, True)
    du_raw, dlam8, dbd, dcd = _ssm_bwd(dys_bf, u_bf, h_all, bd, cd, lamc8, start_b)
    dz_u = _du_to_dz(du_raw, dys, dsk, te)
    dlam = jnp.sum(dlam8, axis=0)
    dbb = _diag_blocks(dbd.reshape(2, SSM_NB, BLK, 512), SSM_GROUP, SSM_STATE)
    dbb_re = jnp.swapaxes(dbb[0], 1, 2).reshape(gp, SSM_GROUP)
    dbb_im = jnp.swapaxes(dbb[1], 1, 2).reshape(gp, SSM_GROUP)
    dcc = _diag_blocks(dcd.reshape(2, SSM_NB, 512, BLK), SSM_STATE, SSM_GROUP)
    dc_re, dc_im = jnp.swapaxes(dcc[0], 1, 2), -jnp.swapaxes(dcc[1], 1, 2)
    db_re, db_im, dz_re, dz_im = _ssm_scale_b_bwd(z_re.reshape(gp, 1), z_im.reshape(gp, 1), b_re2, b_im2, dbb_re, dbb_im)
    gshape = (SSM_GROUPS, SSM_STATE)
    da_re, da_im, dlog_dt = _ssm_disc_bwd(a_re[0], a_im[0], log_dt.reshape(SSM_GROUPS, 1), dlam[:gp].reshape(gshape),
                                          dlam[gp:].reshape(gshape), dz_re.reshape(gshape), dz_im.reshape(gshape))

    dW_ap = wgrad("dw_attn_proj", attn_bf, GROUP_W, dad_bf, d)
    pre = _attn_bwd_pre(d_attn, attn, te)
    das, deltas = pre[:N_GROUPS], pre[N_GROUPS:]
    dqkvs = [_attn_bwd(qkv[g], das[g], lts[g], deltas[g], dil, min(512, t // dil)) for g, dil in enumerate(DILATIONS)]
    dz_qkv = _undilate_rope_bwd(dqkvs, pos, invf, tm)

    dW_in, = _mm("dw_in_qkv", (1, 3, nkt), [("tn", n1, tok_a(d), dz_qkv, _bs((tk, QK_W), lambda i, j, k: (k, j)))],
                 [(SDS((d, inw), BF16), _bs((d, QK_W), lambda i, j, k: (0, j)))])
    dW_in, = _mm("dw_in_u", (1, 1, nkt), [("tn", n1, tok_a(d), dz_u, tok_a(SSM_W))],
                 [(SDS((d, inw), BF16), _bs((d, SSM_W), lambda i, j, k: (0, ucol)))], alias_to_out0=dW_in)
    dW_in, = _mm("dw_in_gates", (1, 2, nkt), [("tn", n1, tok_a(d), dz_g, _bs((tk, d), lambda i, j, k: (k, j)))],
                 [(SDS((d, inw), BF16), _bs((d, d), lambda i, j, k: (0, gcol + j)))], alias_to_out0=dW_in)
    rest_names = ("w_in", "w_attn_proj", "w_glu_a", "w_glu_b", "w_out")
    rs_in = _exchange_start("scatter_rest_start", [dW_in, dW_ap, dW_ga, dW_gb, dW_out], *group(rest_names), False)
    w_piece = lambda w, cb: pl.BlockSpec((d, w), lambda i, j, k: (0, cb), pipeline_mode=pl.Buffered(1))
    dx, dg_mix = _mm(
        "d_z_proj", (t // te, 1, 1),
        [("nt", dz_qkv, _bs((te, 3 * QK_W), lambda i, j, k: (i, 0)), W_in, w_piece(3 * QK_W, 0)),
         ("nt", dz_u, _bs((te, SSM_W), lambda i, j, k: (i, 0)), W_in, w_piece(SSM_W, ucol)),
         ("nt", dz_g, _bs((te, d), lambda i, j, k: (i, 0)), W_in, w_piece(d, gcol)),
         ("nt", dz_g, _bs((te, d), lambda i, j, k: (i, 1)), W_in, w_piece(d, gcol + 1))],
        [(td_f32, row_e), (SDS((1, d), F32), vec_d)],
        extras=[(x2, row_e), (g_mix, vec_d), (dh1, row_e)], epilogue=_rms_bwd_epilogue, after=rs_in[3])

    small_parts = dict(g_mix=dg_mix, a_re=da_re, a_im=da_im, log_dt=dlog_dt, b_re=db_re, b_im=db_im, c_re=dc_re, c_im=dc_im,
                       d_skip=dd_skip, g_ffn=dg_ffn, g_final=dg_final)
    small = _pack_small([small_parts[n] for n in _SMALL])
    received = {}
    for names, started, label in ((ffn_names, rs_ffn, "ffn"), (rest_names, rs_in, "rest")):
        landed = _exchange_wait(f"scatter_{label}_wait", started, list(range(len(names))), *group(names), False, dx)
        received.update(zip(names, landed))

    new = {}
    for n in wnames:
        new[n] = [o.reshape(args[n].shape)
                  for o in _adamw("adamw_" + n, received[n], args[n][0], args["m_" + n][0], args["v_" + n][0])]
    pk = lambda pre: _pack_small([args[pre + n] for n in _SMALL])
    sm = _adamw("adamw_small", _gather_small(small), pk(""), pk("m_"), pk("v_"))
    shapes = [args[n].shape for n in _SMALL]
    for n, vals in zip(_SMALL, zip(*[_unpack_small(o, shapes) for o in sm])):
        new[n] = list(vals)

    order = ("g_mix", "w_in", "a_re", "a_im", "log_dt", "b_re", "b_im", "c_re", "c_im", "d_skip", "w_attn_proj", "w_glu_a",
             "w_glu_b", "w_out", "g_ffn", "w_ffn_gate", "w_ffn_up", "w_ffn_down", "w_ple_gate", "w_ple_proj", "g_final")
    return (loss, dx.reshape(x.shape), *[new[n][0] for n in order], *[new[n][1] for n in order],
            *[new[n][2] for n in order], *[new[n][3] for n in order])
```

```python
import functools
import math

import jax
import jax.numpy as jnp
from jax import lax
from jax.experimental import pallas as pl
from jax.experimental.pallas import tpu as pltpu

F32 = jnp.float32
BF16 = jnp.bfloat16
SDS = jax.ShapeDtypeStruct

N_DEV = 8
HEAD_DIM = 128
HEADS_PER_GROUP = 4
GROUP_W = HEADS_PER_GROUP * HEAD_DIM
DILATIONS = (1, 4, 16)
N_GROUPS = len(DILATIONS)
QK_W = N_GROUPS * GROUP_W
BLK = 128
ROPE_THETA = 500000.0
ROPE_DIM = HEAD_DIM // 4
ROPE_HALF = ROPE_DIM // 2
SSM_W = 512
SSM_GROUP = 16
SSM_GROUPS = SSM_W // SSM_GROUP
SSM_STATE = 64
NSTATE = SSM_GROUPS * SSM_STATE
SSM_NB = 4
EPS = 1e-6
ADAM_LR, ADAM_B1, ADAM_B2, ADAM_EPS, ADAM_WD, ADAM_STEP = 0.001, 0.9, 0.999, 1e-08, 0.01, 10
NEG = -1e30

VMEM_LIMIT = 52 * 1024 * 1024
SCAN_ROWS = 512
SCAN_LANES = 512


def _cp(n):
    return pltpu.CompilerParams(dimension_semantics=("arbitrary",) * n, vmem_limit_bytes=VMEM_LIMIT)


def _sigmoid(x):
    return 0.5 * jnp.tanh(0.5 * x) + 0.5


_DNUMS = {"nn": (((1,), (0,)), ((), ())), "nt": (((1,), (1,)), ((), ())), "tn": (((0,), (0,)), ((), ()))}


def _bs(shape, fn):
    return pl.BlockSpec(shape, fn)


def _store_all(prods, extra_refs, out_refs, scratch_refs):
    r = prods[0]
    for p in prods[1:]:
        r = r + p
    for e in extra_refs:
        r = r + e[...]
    for o in out_refs:
        o[...] = r.astype(o.dtype)


def _mm(name, grid, pairs, outs, extras=(), epilogue=_store_all, scratch=(), alias_to_out0=None, after=None):
    nk = grid[2]
    npair = len(pairs)
    steps = [p[5] if len(p) > 5 else nk for p in pairs]

    def block(spec):
        return tuple(s for s in spec.block_shape if s is not None)

    acc_shapes = [jax.eval_shape(lambda u, v, dn=_DNUMS[p[0]]: lax.dot_general(u, v, dn, preferred_element_type=F32),
                                 SDS(block(p[2]), BF16), SDS(block(p[4]), BF16)).shape for p in pairs]
    if nk == 1:
        acc_shapes = []
    n_in = 2 * npair + len(extras) + (alias_to_out0 is not None) + (after is not None)

    def body(*refs):
        extra_refs = refs[2 * npair:2 * npair + len(extras)]
        out_refs = refs[n_in:n_in + len(outs)]
        rest = refs[n_in + len(outs):]
        acc_refs = rest[:len(acc_shapes)]
        scratch_refs = rest[len(acc_refs):]
        k = pl.program_id(2)

        def product(i):
            return lax.dot_general(refs[2 * i][...].astype(BF16), refs[2 * i + 1][...].astype(BF16), _DNUMS[pairs[i][0]],
                                   preferred_element_type=F32)

        if nk == 1:
            epilogue([product(i) for i in range(npair)], extra_refs, out_refs, scratch_refs)
            return
        for i in range(npair):
            @pl.when(k == 0)
            def _(i=i):
                acc_refs[i][...] = product(i)

            @pl.when((k > 0) & (k < steps[i]))
            def _(i=i):
                acc_refs[i][...] += product(i)

        @pl.when(k == nk - 1)
        def _():
            epilogue([a[...] for a in acc_refs], extra_refs, out_refs, scratch_refs)

    ins, in_specs = [], []
    for p in pairs:
        ins += [p[1], p[3]]
        in_specs += [p[2], p[4]]
    ins += [e[0] for e in extras]
    in_specs += [e[1] for e in extras]
    aliases = {}
    if alias_to_out0 is not None:
        aliases = {len(ins): 0}
        ins.append(alias_to_out0)
        in_specs.append(pl.BlockSpec(memory_space=pl.ANY))
    if after is not None:
        ins.append(after)
        in_specs.append(pl.BlockSpec(memory_space=pl.ANY))
    scratch_shapes = [pltpu.VMEM(s, F32) for s in acc_shapes] + list(scratch)
    return pl.pallas_call(body, grid=grid, in_specs=in_specs, out_specs=[o[1] for o in outs], out_shape=[o[0] for o in outs],
                          scratch_shapes=scratch_shapes, input_output_aliases=aliases, compiler_params=_cp(3), name=name)(*ins)


def _my_index():
    return 4 * lax.axis_index("x") + 2 * lax.axis_index("y") + lax.axis_index("c")


def _peer(d):
    mx, my, mc = lax.axis_index("x"), lax.axis_index("y"), lax.axis_index("c")
    return (mx ^ ((d >> 2) & 1), my ^ ((d >> 1) & 1), mc ^ (d & 1))


def _win(ref, kind, j, n):
    if kind == "all":
        return ref
    if kind == "slot":
        return ref.at[j]
    if kind == "rows":
        return ref.at[pl.ds(pl.multiple_of(j * n, 8), n)]
    return ref.at[:, pl.ds(pl.multiple_of(j * n, 128), n)]


def _win7(ref, kind, n):
    if kind == "slot":
        return ref.at[pl.ds(0, 7)]
    if kind == "rows":
        return ref.at[pl.ds(0, 7 * n)]
    return ref.at[:, pl.ds(0, 7 * n)]


def _full_shape(shard_shape, kind):
    if kind == "slot":
        return (N_DEV,) + tuple(shard_shape)
    if kind == "rows":
        return (N_DEV * shard_shape[0],) + tuple(shard_shape[1:])
    return (shard_shape[0], N_DEV * shard_shape[1])


def _shard_shape(full_shape, kind, n):
    if kind == "all":
        return tuple(full_shape)
    if kind == "slot":
        return tuple(full_shape[1:])
    if kind == "rows":
        return (n,) + tuple(full_shape[1:])
    return (full_shape[0], n)


_HBM = pl.BlockSpec(memory_space=pltpu.HBM)
_SEM = pl.BlockSpec(memory_space=pltpu.SEMAPHORE)
_DATAFLOW = pltpu.SideEffectType.DATAFLOW_SIDE_EFFECTING


def _exchange_start(name, srcs, kinds, sizes, gather):
    n = len(srcs)
    if gather:
        lands = [lax.empty(_full_shape(s.shape, k), s.dtype) for s, k in zip(srcs, kinds)]
    else:
        lands = [lax.empty((N_DEV,) + _shard_shape(s.shape, k, z), s.dtype) for s, k, z in zip(srcs, kinds, sizes)]

    def body(*refs):
        src, land = refs[:n], refs[n:2 * n]
        send_sems, recv_sems, local_sems = refs[2 * n], refs[2 * n + 1], refs[2 * n + 2]
        token = refs[4 * n + 3]
        me = _my_index()
        for a in range(n):
            _local_copy(src[a], land[a], kinds[a], sizes[a], gather, me, local_sems.at[a]).start()
        for a in range(n):
            for d in range(1, N_DEV):
                px, py, pc = _peer(d)
                if gather:
                    s_ref, d_ref = src[a], _win(land[a], kinds[a], me, sizes[a])
                else:
                    s_ref, d_ref = _win(src[a], kinds[a], 4 * px + 2 * py + pc, sizes[a]), land[a].at[me]
                pltpu.make_async_remote_copy(src_ref=s_ref, dst_ref=d_ref, send_sem=send_sems.at[a], recv_sem=recv_sems.at[a],
                                             device_id=(px, py, pc), device_id_type=pl.DeviceIdType.MESH).start()
        token[...] = jnp.zeros_like(token)

    hbm = [pltpu.with_memory_space_constraint(a, pltpu.HBM) for a in list(srcs) + lands]
    out = pl.pallas_call(
        body, name=name, in_specs=[_HBM] * (2 * n),
        out_shape=[pltpu.SemaphoreType.DMA((n,))] * 3 + [pltpu.HBM(a.shape, a.dtype) for a in hbm] + [SDS((8, 128), F32)],
        out_specs=[_SEM] * 3 + [_HBM] * (2 * n) + [pl.BlockSpec(memory_space=pltpu.VMEM)],
        input_output_aliases={i: 3 + i for i in range(2 * n)},
        compiler_params=pltpu.CompilerParams(has_side_effects=_DATAFLOW))(*hbm)
    return out[0:3], out[3:3 + n], out[3 + n:3 + 2 * n], out[-1]


def _local_copy(src, land, kind, size, gather, me, sem):
    if gather:
        return pltpu.make_async_copy(src, _win(land, kind, me, size), sem)
    return pltpu.make_async_copy(_win(src, kind, me, size), land.at[me], sem)


def _exchange_wait(name, started, which, kinds, sizes, gather, after):
    sems, srcs, lands, _ = started
    n = len(which)

    def body(*refs):
        src, land = refs[:n], refs[n:2 * n]
        send_ref, recv_ref, local_ref = refs[2 * n:2 * n + 3]
        me = _my_index()
        my_id = (lax.axis_index("x"), lax.axis_index("y"), lax.axis_index("c"))
        for i, a in enumerate(which):
            seven = _win7(land[i], kinds[a], sizes[a]) if gather else land[i].at[pl.ds(0, 7)]
            pltpu.make_async_remote_copy(src_ref=seven, dst_ref=seven, send_sem=send_ref.at[a], recv_sem=recv_ref.at[a],
                                         device_id=my_id, device_id_type=pl.DeviceIdType.MESH).wait()
            _local_copy(src[i], land[i], kinds[a], sizes[a], gather, me, local_ref.at[a]).wait()

    hbm = [srcs[a] for a in which] + [lands[a] for a in which]
    out = pl.pallas_call(
        body, name=name, in_specs=[_HBM] * (2 * n) + [_SEM] * 3 + [pl.BlockSpec(memory_space=pl.ANY)],
        out_shape=[pltpu.HBM(a.shape, a.dtype) for a in hbm], out_specs=[_HBM] * (2 * n),
        input_output_aliases={i: i for i in range(2 * n)},
        compiler_params=pltpu.CompilerParams(has_side_effects=_DATAFLOW))(*hbm, *sems, after)
    return out[n:]


def _gather_small(small):
    def body(in_ref, out_ref, send_sem, recv_sem, local_sem):
        me = _my_index()
        my_id = (lax.axis_index("x"), lax.axis_index("y"), lax.axis_index("c"))
        cp = pltpu.make_async_copy(in_ref, out_ref.at[me], local_sem)
        cp.start()
        for d in range(1, N_DEV):
            pltpu.make_async_remote_copy(src_ref=in_ref, dst_ref=out_ref.at[me], send_sem=send_sem, recv_sem=recv_sem,
                                         device_id=_peer(d), device_id_type=pl.DeviceIdType.MESH).start()
        seven = out_ref.at[pl.ds(0, 7)]
        pltpu.make_async_remote_copy(src_ref=seven, dst_ref=seven, send_sem=send_sem, recv_sem=recv_sem, device_id=my_id,
                                     device_id_type=pl.DeviceIdType.MESH).wait()
        cp.wait()

    any_spec = pl.BlockSpec(memory_space=pl.ANY)
    return pl.pallas_call(body, in_specs=[any_spec], out_specs=any_spec, out_shape=SDS((N_DEV,) + small.shape, F32),
                          scratch_shapes=[pltpu.SemaphoreType.DMA] * 3, name="gather_small")(small)


def _adamw(name, recv, w, m, v):
    rows, cols = w.shape
    tr = max(c for c in range(16, 257, 16) if rows % c == 0) if rows % 16 == 0 else rows

    def body(r_ref, w_ref, m_ref, v_ref, g_ref, d_ref, nm_ref, nv_ref):
        g = r_ref[0].astype(F32)
        for s in range(1, N_DEV):
            g = g + r_ref[s].astype(F32)
        nm = ADAM_B1 * m_ref[...] + (1.0 - ADAM_B1) * g
        nv = ADAM_B2 * v_ref[...] + (1.0 - ADAM_B2) * (g * g)
        m_hat = nm / (1.0 - ADAM_B1 ** ADAM_STEP)
        v_hat = nv / (1.0 - ADAM_B2 ** ADAM_STEP)
        g_ref[...] = g
        d_ref[...] = -ADAM_LR * (m_hat / (jnp.sqrt(v_hat) + ADAM_EPS) + ADAM_WD * w_ref[...])
        nm_ref[...] = nm
        nv_ref[...] = nv

    blk = _bs((tr, cols), lambda i: (i, 0))
    return pl.pallas_call(
        body, grid=(rows // tr,), in_specs=[_bs((N_DEV, tr, cols), lambda i: (0, i, 0)), blk, blk, blk],
        out_specs=[blk] * 4, out_shape=[SDS((rows, cols), F32)] * 4, compiler_params=_cp(1), name=name)(recv, w, m, v)


def _rms_fwd(name, x, g, tm):
    t, d = x.shape

    def body(x_ref, g_ref, n_ref):
        xv = x_ref[...]
        r = lax.rsqrt(jnp.mean(xv * xv, axis=-1, keepdims=True) + EPS)
        n_ref[...] = (xv * r * g_ref[...]).astype(BF16)

    return pl.pallas_call(body, grid=(t // tm,), in_specs=[_bs((tm, d), lambda i: (i, 0)), _bs((1, d), lambda i: (0, 0))],
                          out_specs=_bs((tm, d), lambda i: (i, 0)), out_shape=SDS((t, d), BF16), compiler_params=_cp(1),
                          name=name)(x, g)


def _accumulate_rows(ref, part):
    @pl.when(pl.program_id(0) == 0)
    def _():
        ref[...] = part

    @pl.when(pl.program_id(0) > 0)
    def _():
        ref[...] += part


def _rms_bwd_epilogue(prods, extra_refs, out_refs, scratch_refs):
    dyv = prods[0]
    for p in prods[1:]:
        dyv = dyv + p
    if len(extra_refs) > 3:
        dyv = dyv + extra_refs[3][...]
    xv = extra_refs[0][...]
    r = lax.rsqrt(jnp.mean(xv * xv, axis=-1, keepdims=True) + EPS)
    xh = xv * r
    dxh = dyv * extra_refs[1][...]
    dx = extra_refs[2][...] + r * (dxh - xh * jnp.mean(dxh * xh, axis=-1, keepdims=True))
    for o in out_refs[:-1]:
        o[...] = dx.astype(o.dtype)
    _accumulate_rows(out_refs[-1], jnp.sum(dyv * xh, axis=0, keepdims=True))


def _out_norm_epilogue(prods, extra_refs, out_refs, scratch_refs):
    h = prods[0] + extra_refs[0][...]
    r = lax.rsqrt(jnp.mean(h * h, axis=-1, keepdims=True) + EPS)
    out_refs[0][...] = h
    out_refs[1][...] = (h * r * extra_refs[1][...]).astype(BF16)


def _glu_merge_epilogue(prods, extra_refs, out_refs, scratch_refs):
    ya, yb, ad = prods
    ga, gs = extra_refs[0][...].astype(F32), extra_refs[1][...].astype(F32)
    m = _sigmoid(ga) * ad + _sigmoid(gs) * (ya * _sigmoid(yb))
    out_refs[0][...] = m.astype(BF16)
    for o, val in zip(out_refs[1:], (ya, yb, ad)):
        o[...] = val.astype(o.dtype)


def _merge_bwd_epilogue(prods, extra_refs, out_refs, scratch_refs):
    dmv = prods[0]
    d = dmv.shape[1]
    ga, gs = _sigmoid(extra_refs[0][...].astype(F32)), _sigmoid(extra_refs[1][...].astype(F32))
    adv, yav = extra_refs[2][...].astype(F32), extra_refs[3][...].astype(F32)
    sb = _sigmoid(extra_refs[4][...].astype(F32))
    out_refs[0][:, 0:d] = (dmv * adv * ga * (1.0 - ga)).astype(BF16)
    out_refs[0][:, d:2 * d] = (dmv * (yav * sb) * gs * (1.0 - gs)).astype(BF16)
    dad = (dmv * ga).astype(BF16)
    dsd = dmv * gs
    dya = (dsd * sb).astype(BF16)
    dyb = (dsd * yav * sb * (1.0 - sb)).astype(BF16)
    out_refs[1][...], out_refs[2][...], out_refs[3][...] = dad, dya, dyb
    nt = _DNUMS["nt"]
    out_refs[4][...] = (lax.dot_general(dya, extra_refs[5][...], nt, preferred_element_type=F32)
                        + lax.dot_general(dyb, extra_refs[6][...], nt, preferred_element_type=F32))
    out_refs[5][...] = lax.dot_general(dad, extra_refs[7][...], nt, preferred_element_type=F32)


def _swiglu_epilogue(prods, extra_refs, out_refs, scratch_refs):
    gv, uv = prods
    out_refs[0][...] = (gv * _sigmoid(gv) * uv).astype(BF16)
    out_refs[1][...] = gv.astype(out_refs[1].dtype)
    out_refs[2][...] = uv.astype(out_refs[2].dtype)


def _swiglu_bwd_epilogue(prods, extra_refs, out_refs, scratch_refs):
    dav = prods[0]
    gv, uv = extra_refs[0][...].astype(F32), extra_refs[1][...].astype(F32)
    sg = _sigmoid(gv)
    out_refs[0][...] = (dav * uv * sg * (1.0 + gv * (1.0 - sg))).astype(BF16)
    out_refs[1][...] = (dav * gv * sg).astype(BF16)


def _head_epilogue(n_tiles):
    def epilogue(prods, extra_refs, out_refs, scratch_refs):
        h2 = prods[0] + extra_refs[0][...]
        h2_bf = h2.astype(BF16)
        out_refs[6][...] = h2_bf
        pgv = jnp.dot(h2_bf, extra_refs[3][...], preferred_element_type=F32)
        ppv = prods[1]
        d = pgv.shape[1]
        lacc = scratch_refs[0]
        sg = _sigmoid(pgv)
        h3 = h2 + sg * ppv
        r = lax.rsqrt(jnp.mean(h3 * h3, axis=-1, keepdims=True) + EPS)
        xh = h3 * r
        gv = extra_refs[1][...]
        diff = xh * gv - extra_refs[2][...]
        dout = diff * (1.0 / d)
        dxh = dout * gv
        dh3 = r * (dxh - xh * jnp.mean(dxh * xh, axis=-1, keepdims=True))
        dpg = (dh3 * ppv * sg * (1.0 - sg)).astype(BF16)
        dh2 = dh3 + lax.dot_general(dpg, extra_refs[3][...], _DNUMS["nt"], preferred_element_type=F32)
        out_refs[2][...] = dh2
        out_refs[3][...] = dh2.astype(BF16)
        out_refs[4][...] = (dh3 * sg).astype(BF16)
        out_refs[5][...] = dpg
        _accumulate_rows(out_refs[1], jnp.sum(dout * xh, axis=0, keepdims=True))
        _accumulate_rows(lacc, jnp.sum(diff * diff, axis=0, keepdims=True))

        @pl.when(pl.program_id(0) == n_tiles - 1)
        def _():
            out_refs[0][...] = (0.5 / d) * jnp.sum(lacc[...], axis=-1, keepdims=True)

    return epilogue


def _strided(r, n, d):
    return pl.ds(r, n, stride=d) if d > 1 else pl.ds(0, n)


def _rope_tables(pos_ref, invf_ref, c_s, s1_s, s2_s):
    ang = pos_ref[...].astype(F32) * invf_ref[...]
    lane = lax.broadcasted_iota(jnp.int32, ang.shape, 1)
    sn = jnp.sin(ang)
    c_s[...] = jnp.where(lane < ROPE_DIM, jnp.cos(ang), 1.0)
    s1_s[...] = jnp.where(lane < ROPE_HALF, -sn, 0.0)
    s2_s[...] = jnp.where((lane >= ROPE_HALF) & (lane < ROPE_DIM), sn, 0.0)


def _rope_dilate_epilogue(tm):
    def epilogue(prods, extra_refs, out_refs, scratch_refs):
        zv = prods[0]
        pos_ref, invf_ref = extra_refs
        c_s, s1_s, s2_s, rot = scratch_refs
        c = pl.program_id(1)

        @pl.when(c == 0)
        def _():
            _rope_tables(pos_ref, invf_ref, c_s, s1_s, s2_s)

        @pl.when(c < 2)
        def _():
            cc, s1, s2 = c_s[...], s1_s[...], s2_s[...]
            for h in range(QK_W // HEAD_DIM):
                xv = zv[:, h * HEAD_DIM:(h + 1) * HEAD_DIM]
                rot[h] = xv * cc + pltpu.roll(xv, HEAD_DIM - ROPE_HALF, 1) * s1 + pltpu.roll(xv, ROPE_HALF, 1) * s2

        @pl.when(c == 2)
        def _():
            for h in range(QK_W // HEAD_DIM):
                rot[h] = zv[:, h * HEAD_DIM:(h + 1) * HEAD_DIM]

        for g, (d, o_ref) in enumerate(zip(DILATIONS, out_refs)):
            n = tm // d
            for r in range(d):
                for hh in range(HEADS_PER_GROUP):
                    oc = r * GROUP_W + hh * HEAD_DIM
                    o_ref[:, oc:oc + HEAD_DIM] = rot[g * HEADS_PER_GROUP + hh, _strided(r, n, d), :].astype(BF16)

    return epilogue


def _band_masks(first_tile):
    qi = lax.broadcasted_iota(jnp.int32, (BLK, 2 * BLK), 0)
    kj = lax.broadcasted_iota(jnp.int32, (BLK, 2 * BLK), 1)
    band = (kj >= qi) & (kj <= qi + BLK)
    return band, band & ((kj >= BLK) | jnp.logical_not(first_tile))


def _attn_fwd(qkv, d, qt):
    ell = qkv.shape[1]
    nsub = qt // BLK
    scale = 1.0 / math.sqrt(HEAD_DIM)

    def body(q_ref, kc_ref, kp_ref, vc_ref, vp_ref, o_ref, lse_ref, kcat, vcat):
        nb = pl.program_id(1)
        kcat[0:BLK, :] = kp_ref[...]
        kcat[BLK:, :] = kc_ref[...]
        vcat[0:BLK, :] = vp_ref[...]
        vcat[BLK:, :] = vc_ref[...]
        lane = lax.broadcasted_iota(jnp.int32, (BLK, HEAD_DIM), 1)
        band, band_first = _band_masks(nb == 0)
        for b in range(nsub):
            valid = band_first if b == 0 else band
            lse_t = jnp.zeros((BLK, HEAD_DIM), F32)
            for hh in range(HEADS_PER_GROUP):
                cs = slice(hh * HEAD_DIM, (hh + 1) * HEAD_DIM)
                qb = q_ref[b * BLK:(b + 1) * BLK, cs]
                kk = kcat[b * BLK:(b + 2) * BLK, cs]
                vv = vcat[b * BLK:(b + 2) * BLK, cs]
                s = lax.dot_general(qb, kk, _DNUMS["nt"], preferred_element_type=F32) * scale
                s = jnp.where(valid, s, NEG)
                mx = jnp.max(s, axis=-1, keepdims=True)
                p = jnp.exp(s - mx)
                den = jnp.sum(p, axis=-1, keepdims=True)
                o = jnp.dot(p.astype(BF16), vv, preferred_element_type=F32) / den
                o_ref[b * BLK:(b + 1) * BLK, cs] = o
                lse_t = jnp.where(lane == hh, mx + jnp.log(den), lse_t)
            lse_ref[b * BLK:(b + 1) * BLK, :] = lse_t

    cur = lambda c: _bs((None, qt, GROUP_W), lambda r, nb: (c, nb, r))
    prev = lambda c: _bs((None, BLK, GROUP_W), lambda r, nb: (c, jnp.maximum(nb * nsub - 1, 0), r))
    return pl.pallas_call(
        body, grid=(d, ell // qt), in_specs=[cur(0), cur(1), prev(1), cur(2), prev(2)],
        out_specs=[_bs((qt, GROUP_W), lambda r, nb: (nb, r)), _bs((None, qt, HEAD_DIM), lambda r, nb: (r, nb, 0))],
        out_shape=[SDS((ell, d * GROUP_W), F32), SDS((d, ell, HEAD_DIM), F32)],
        scratch_shapes=[pltpu.VMEM((qt + BLK, GROUP_W), BF16)] * 2, compiler_params=_cp(2), name=f"attn_fwd_d{d}")(
            qkv, qkv, qkv, qkv, qkv)


def _attn_merge(outs, lses, tm):
    t = outs[0].shape[0]

    def body(o0, o1, o2, l0, l1, l2, attn_ref, attn_bf_ref, t0, t1, t2, so, sl, lt_s):
        for g, (d, o_ref, l_ref) in enumerate(zip(DILATIONS, (o0, o1, o2), (l0, l1, l2))):
            n = tm // d
            for r in range(d):
                rows = _strided(r, n, d)
                for hh in range(HEADS_PER_GROUP):
                    oc = r * GROUP_W + hh * HEAD_DIM
                    so[g * HEADS_PER_GROUP + hh, rows, :] = o_ref[:, oc:oc + HEAD_DIM]
                sl[g, rows, :] = l_ref[r]
        ls = [sl[g] for g in range(N_GROUPS)]
        mx = jnp.maximum(jnp.maximum(ls[0], ls[1]), ls[2])
        es = [jnp.exp(l - mx) for l in ls]
        den = es[0] + es[1] + es[2]
        ws = [e / den for e in es]
        lt_s[...] = mx + jnp.log(den)
        for hh in range(HEADS_PER_GROUP):
            cs = slice(hh * HEAD_DIM, (hh + 1) * HEAD_DIM)
            a = ws[0][:, hh:hh + 1] * so[hh]
            for g in range(1, N_GROUPS):
                a = a + ws[g][:, hh:hh + 1] * so[g * HEADS_PER_GROUP + hh]
            attn_ref[:, cs] = a
            attn_bf_ref[:, cs] = a.astype(BF16)
        for d, t_ref in zip(DILATIONS, (t0, t1, t2)):
            n = tm // d
            for r in range(d):
                t_ref[r] = lt_s[_strided(r, n, d), :]

    dil = lambda d: _bs((tm // d, d * GROUP_W), lambda i: (i, 0))
    lsp = lambda d: _bs((d, tm // d, HEAD_DIM), lambda i: (0, i, 0))
    row = _bs((tm, GROUP_W), lambda i: (i, 0))
    return pl.pallas_call(
        body, grid=(t // tm,),
        in_specs=[dil(d) for d in DILATIONS] + [lsp(d) for d in DILATIONS],
        out_specs=[row, row] + [lsp(d) for d in DILATIONS],
        out_shape=[SDS((t, GROUP_W), F32), SDS((t, GROUP_W), BF16)] + [SDS(l.shape, F32) for l in lses],
        scratch_shapes=[pltpu.VMEM((N_GROUPS * HEADS_PER_GROUP, tm, HEAD_DIM), F32), pltpu.VMEM((N_GROUPS, tm, HEAD_DIM), F32),
                        pltpu.VMEM((tm, HEAD_DIM), F32)],
        compiler_params=_cp(1), name="attn_merge")(*outs, *lses)


def _attn_bwd_pre(d_attn, attn, tm):
    t = attn.shape[0]

    def body(da_ref, a_ref, g0, g1, g2, e0, e1, e2, dl_s, da_s):
        lane = lax.broadcasted_iota(jnp.int32, (tm, HEAD_DIM), 1)
        dl = jnp.zeros((tm, HEAD_DIM), F32)
        for hh in range(HEADS_PER_GROUP):
            cs = slice(hh * HEAD_DIM, (hh + 1) * HEAD_DIM)
            dav = da_ref[:, cs]
            da_s[hh] = dav
            dl = jnp.where(lane == hh, jnp.sum(dav * a_ref[:, cs], axis=-1, keepdims=True), dl)
        dl_s[...] = dl
        for d, g_ref, e_ref in zip(DILATIONS, (g0, g1, g2), (e0, e1, e2)):
            n = tm // d
            for r in range(d):
                rows = _strided(r, n, d)
                for hh in range(HEADS_PER_GROUP):
                    oc = r * GROUP_W + hh * HEAD_DIM
                    g_ref[:, oc:oc + HEAD_DIM] = da_s[hh, rows, :].astype(BF16)
                e_ref[r] = dl_s[rows, :]

    row = _bs((tm, GROUP_W), lambda i: (i, 0))
    return pl.pallas_call(
        body, grid=(t // tm,), in_specs=[row, row],
        out_specs=[_bs((tm // d, d * GROUP_W), lambda i: (i, 0)) for d in DILATIONS]
        + [_bs((d, tm // d, HEAD_DIM), lambda i: (0, i, 0)) for d in DILATIONS],
        out_shape=[SDS((t // d, d * GROUP_W), BF16) for d in DILATIONS]
        + [SDS((d, t // d, HEAD_DIM), F32) for d in DILATIONS],
        scratch_shapes=[pltpu.VMEM((tm, HEAD_DIM), F32), pltpu.VMEM((HEADS_PER_GROUP, tm, HEAD_DIM), F32)],
        compiler_params=_cp(1), name="attn_bwd_pre")(d_attn, attn)


def _attn_bwd(qkv, d_a, lt, delta, d, qt):
    ell = qkv.shape[1]
    nsub = qt // BLK
    ntile = ell // qt
    nblk = ell // BLK
    scale = 1.0 / math.sqrt(HEAD_DIM)

    def body(q_ref, qn_ref, kc_ref, kp_ref, vc_ref, vp_ref, da_ref, dan_ref, lt_ref, ltn_ref, dl_ref, dln_ref, o_ref,
             kcat, vcat):
        nb = pl.program_id(1)
        kcat[0:BLK, :] = kp_ref[...]
        kcat[BLK:, :] = kc_ref[...]
        vcat[0:BLK, :] = vp_ref[...]
        vcat[BLK:, :] = vc_ref[...]
        qi = lax.broadcasted_iota(jnp.int32, (BLK, BLK), 0)
        kj = lax.broadcasted_iota(jnp.int32, (BLK, BLK), 1)
        valid_next = (kj >= qi) & (nb < ntile - 1)
        band, band_first = _band_masks(nb == 0)
        for hh in range(HEADS_PER_GROUP):
            cs = slice(hh * HEAD_DIM, (hh + 1) * HEAD_DIM)
            dks, dvs = [], []
            for b in range(nsub):
                rs = slice(b * BLK, (b + 1) * BLK)
                ks = slice(b * BLK, (b + 2) * BLK)
                valid = band_first if b == 0 else band
                qb, kk, vv, dab = q_ref[rs, cs], kcat[ks, cs], vcat[ks, cs], da_ref[rs, cs]
                s = lax.dot_general(qb, kk, _DNUMS["nt"], preferred_element_type=F32) * scale
                p = jnp.where(valid, jnp.exp(s - lt_ref[rs, hh:hh + 1]), 0.0)
                dp = lax.dot_general(dab, vv, _DNUMS["nt"], preferred_element_type=F32)
                ds = (p * (dp - dl_ref[rs, hh:hh + 1])).astype(BF16)
                o_ref[0, rs, cs] = jnp.dot(ds, kk, preferred_element_type=F32) * scale
                dks.append(lax.dot_general(ds, qb, _DNUMS["tn"], preferred_element_type=F32))
                dvs.append(lax.dot_general(p.astype(BF16), dab, _DNUMS["tn"], preferred_element_type=F32))
            ks = slice(nsub * BLK, (nsub + 1) * BLK)
            qn, kl, vl, dan = qn_ref[:, cs], kcat[ks, cs], vcat[ks, cs], dan_ref[:, cs]
            s = lax.dot_general(qn, kl, _DNUMS["nt"], preferred_element_type=F32) * scale
            p = jnp.where(valid_next, jnp.exp(s - ltn_ref[:, hh:hh + 1]), 0.0)
            dp = lax.dot_general(dan, vl, _DNUMS["nt"], preferred_element_type=F32)
            ds = (p * (dp - dln_ref[:, hh:hh + 1])).astype(BF16)
            dk_next = lax.dot_general(ds, qn, _DNUMS["tn"], preferred_element_type=F32)
            dv_next = lax.dot_general(p.astype(BF16), dan, _DNUMS["tn"], preferred_element_type=F32)
            for b in range(nsub):
                rs = slice(b * BLK, (b + 1) * BLK)
                after_k = dks[b + 1][:BLK] if b + 1 < nsub else dk_next
                after_v = dvs[b + 1][:BLK] if b + 1 < nsub else dv_next
                o_ref[1, rs, cs] = (dks[b][BLK:] + after_k) * scale
                o_ref[2, rs, cs] = dvs[b][BLK:] + after_v

    nxt = lambda nb: jnp.minimum((nb + 1) * nsub, nblk - 1)
    prv = lambda nb: jnp.maximum(nb * nsub - 1, 0)
    cur3 = lambda c: _bs((None, qt, GROUP_W), lambda r, nb: (c, nb, r))
    in_specs = [
        cur3(0), _bs((None, BLK, GROUP_W), lambda r, nb: (0, nxt(nb), r)),
        cur3(1), _bs((None, BLK, GROUP_W), lambda r, nb: (1, prv(nb), r)),
        cur3(2), _bs((None, BLK, GROUP_W), lambda r, nb: (2, prv(nb), r)),
        _bs((qt, GROUP_W), lambda r, nb: (nb, r)), _bs((BLK, GROUP_W), lambda r, nb: (nxt(nb), r)),
        _bs((None, qt, HEAD_DIM), lambda r, nb: (r, nb, 0)), _bs((None, BLK, HEAD_DIM), lambda r, nb: (r, nxt(nb), 0)),
        _bs((None, qt, HEAD_DIM), lambda r, nb: (r, nb, 0)), _bs((None, BLK, HEAD_DIM), lambda r, nb: (r, nxt(nb), 0)),
    ]
    return pl.pallas_call(
        body, grid=(d, ntile), in_specs=in_specs, out_specs=_bs((3, qt, GROUP_W), lambda r, nb: (0, nb, r)),
        out_shape=SDS((3, ell, d * GROUP_W), F32),
        scratch_shapes=[pltpu.VMEM((qt + BLK, GROUP_W), BF16)] * 2,
        compiler_params=_cp(2), name=f"attn_bwd_d{d}")(qkv, qkv, qkv, qkv, qkv, qkv, d_a, d_a, lt, lt, delta, delta)


def _undilate_rope_bwd(dqkvs, pos, invf, tm):
    t = pos.shape[0]

    def body(g0, g1, g2, pos_ref, invf_ref, o_ref, c_s, s1_s, s2_s, nat):
        c = pl.program_id(1)

        @pl.when(c == 0)
        def _():
            _rope_tables(pos_ref, invf_ref, c_s, s1_s, s2_s)

        for g, (d, g_ref) in enumerate(zip(DILATIONS, (g0, g1, g2))):
            n = tm // d
            for r in range(d):
                for hh in range(HEADS_PER_GROUP):
                    oc = r * GROUP_W + hh * HEAD_DIM
                    nat[g * HEADS_PER_GROUP + hh, _strided(r, n, d), :] = g_ref[:, oc:oc + HEAD_DIM]

        @pl.when(c < 2)
        def _():
            cc, s1, s2 = c_s[...], s1_s[...], s2_s[...]
            for h in range(QK_W // HEAD_DIM):
                xv = nat[h]
                y = xv * cc - pltpu.roll(xv, HEAD_DIM - ROPE_HALF, 1) * s1 - pltpu.roll(xv, ROPE_HALF, 1) * s2
                o_ref[:, h * HEAD_DIM:(h + 1) * HEAD_DIM] = y.astype(BF16)

        @pl.when(c == 2)
        def _():
            for h in range(QK_W // HEAD_DIM):
                o_ref[:, h * HEAD_DIM:(h + 1) * HEAD_DIM] = nat[h].astype(BF16)

    return pl.pallas_call(
        body, grid=(t // tm, 3),
        in_specs=[_bs((None, tm // d, d * GROUP_W), lambda i, c: (c, i, 0)) for d in DILATIONS]
        + [_bs((tm, 1), lambda i, c: (i, 0)), _bs((1, HEAD_DIM), lambda i, c: (0, 0))],
        out_specs=_bs((tm, QK_W), lambda i, c: (i, c)), out_shape=SDS((t, 3 * QK_W), BF16),
        scratch_shapes=[pltpu.VMEM((tm, HEAD_DIM), F32)] * 3 + [pltpu.VMEM((QK_W // HEAD_DIM, tm, HEAD_DIM), F32)],
        compiler_params=_cp(2), name="undilate_rope_bwd")(*dqkvs, pos, invf)


def _cmul(ar, ai, br, bi):
    return ar * br - ai * bi, ar * bi + ai * br


def _ssm_disc(a_re, a_im, log_dt, nsq):
    def body(lr_ref, li_ref, ldt_ref, br_ref, bi_ref, zr_ref, zi_ref, pr_ref, pi_ref):
        lr, li = lr_ref[...], li_ref[...]
        dt = jnp.exp(ldt_ref[...])
        mag = jnp.exp(lr * dt)
        bar_re, bar_im = mag * jnp.cos(li * dt), mag * jnp.sin(li * dt)
        nr, ni = bar_re - 1.0, bar_im
        den = lr * lr + li * li
        br_ref[...], bi_ref[...] = bar_re, bar_im
        zr_ref[...] = (nr * lr + ni * li) / den
        zi_ref[...] = (ni * lr - nr * li) / den
        pr, pi = bar_re, bar_im
        for _ in range(nsq):
            pr, pi = _cmul(pr, pi, pr, pi)
        pr_ref[...], pi_ref[...] = pr, pi

    return pl.pallas_call(body, out_shape=[SDS(a_re.shape, F32)] * 6, name="ssm_discretise")(a_re, a_im, log_dt)


def _ssm_scale_b(z_re, z_im, b_re, b_im):
    def body(zr_ref, zi_ref, br_ref, bi_ref, or_ref, oi_ref):
        zr, zi, br, bi = zr_ref[...], zi_ref[...], br_ref[...], bi_ref[...]
        or_ref[...] = zr * br - zi * bi
        oi_ref[...] = zr * bi + zi * br

    return pl.pallas_call(body, out_shape=[SDS(b_re.shape, F32)] * 2, name="ssm_scale_b")(z_re, z_im, b_re, b_im)


def _ssm_scale_b_bwd(z_re, z_im, b_re, b_im, g_re, g_im):
    def body(zr_ref, zi_ref, br_ref, bi_ref, gr_ref, gi_ref, dbr_ref, dbi_ref, dzr_ref, dzi_ref):
        zr, zi, br, bi, gr, gi = zr_ref[...], zi_ref[...], br_ref[...], bi_ref[...], gr_ref[...], gi_ref[...]
        dbr_ref[...] = zr * gr + zi * gi
        dbi_ref[...] = zr * gi - zi * gr
        dzr_ref[...] = jnp.sum(br * gr + bi * gi, axis=-1, keepdims=True)
        dzi_ref[...] = jnp.sum(br * gi - bi * gr, axis=-1, keepdims=True)

    return pl.pallas_call(body, out_shape=[SDS(b_re.shape, F32)] * 2 + [SDS(z_re.shape, F32)] * 2,
                          name="ssm_scale_b_bwd")(z_re, z_im, b_re, b_im, g_re, g_im)


def _ssm_disc_bwd(a_re, a_im, log_dt, gb_re, gb_im, gz_re, gz_im):
    def body(lr_ref, li_ref, ldt_ref, gbr_ref, gbi_ref, gzr_ref, gzi_ref, dar_ref, dai_ref, dldt_ref):
        lr, li = lr_ref[...], li_ref[...]
        dt = jnp.exp(ldt_ref[...])
        mag = jnp.exp(lr * dt)
        bar_re, bar_im = mag * jnp.cos(li * dt), mag * jnp.sin(li * dt)
        nr, ni = bar_re - 1.0, bar_im
        den = lr * lr + li * li
        zr, zi = (nr * lr + ni * li) / den, (ni * lr - nr * li) / den
        gzr, gzi = gzr_ref[...], gzi_ref[...]
        gbr = gbr_ref[...] + (lr * gzr - li * gzi) / den
        gbi = gbi_ref[...] + (lr * gzi + li * gzr) / den
        qr, qi = (zr * lr + zi * li) / den, (zi * lr - zr * li) / den
        dar_ref[...] = dt * (bar_re * gbr + bar_im * gbi) - qr * gzr - qi * gzi
        dai_ref[...] = dt * (bar_re * gbi - bar_im * gbr) - qr * gzi + qi * gzr
        wr, wi = lr * bar_re - li * bar_im, lr * bar_im + li * bar_re
        dldt_ref[...] = dt * jnp.sum(wr * gbr + wi * gbi, axis=-1, keepdims=True)

    return pl.pallas_call(body, out_shape=[SDS(a_re.shape, F32)] * 2 + [SDS(log_dt.shape, F32)],
                          name="ssm_discretise_bwd")(a_re, a_im, log_dt, gb_re, gb_im, gz_re, gz_im)


def _permute_u(z, ucol_block, tm):
    t = z.shape[0]
    seg = t // N_DEV
    z3 = z.reshape(N_DEV, seg, z.shape[1])

    def body(z_ref, u_ref, ub_ref, tmp):
        for n in range(SSM_W // BLK):
            for j in range(N_DEV):
                tmp[n, pl.ds(j, tm // N_DEV, stride=N_DEV), :] = z_ref[j, :, n * BLK:(n + 1) * BLK]
            u_ref[:, n * BLK:(n + 1) * BLK] = tmp[n]
            ub_ref[:, n * BLK:(n + 1) * BLK] = tmp[n].astype(BF16)

    row = _bs((tm, SSM_W), lambda i: (i, 0))
    return pl.pallas_call(
        body, grid=(t // tm,), in_specs=[_bs((N_DEV, tm // N_DEV, SSM_W), lambda i: (0, i, ucol_block))],
        out_specs=[row, row], out_shape=[SDS((t, SSM_W), F32), SDS((t, SSM_W), BF16)],
        scratch_shapes=[pltpu.VMEM((SSM_W // BLK, tm, BLK), F32)], compiler_params=_cp(1), name="permute_u")(z3)


def _drive(src_ref, mat_ref, dst, mode):
    for kn in range(2 * SSM_NB):
        n = kn % SSM_NB
        a = src_ref[:, n * BLK:(n + 1) * BLK]
        dst[:, kn * 512:(kn + 1) * 512] = lax.dot_general(a, mat_ref[kn], _DNUMS[mode], preferred_element_type=F32)


def _scan_chunk(src, lam_ref, carry, *, reverse, store=None, h_ref=None, acc=None):
    steps = src.shape[0] // 8
    for c in range(NSTATE // SCAN_LANES):
        re = slice(c * SCAN_LANES, (c + 1) * SCAN_LANES)
        im = slice(NSTATE + c * SCAN_LANES, NSTATE + (c + 1) * SCAN_LANES)
        ar, ai = lam_ref[:, re], lam_ref[:, im]

        def step(s, val):
            i = (steps - 1 - s) if reverse else s
            rows = pl.ds(pl.multiple_of(i * 8, 8), 8)
            if acc is not None:
                hr, hi, dr, di = val
                pr, pi = h_ref[rows, re], h_ref[rows, im]
                dr = dr + hr * pr + hi * pi
                di = di + hi * pr - hr * pi
            else:
                hr, hi = val
            nr = ar * hr - ai * hi + src[rows, re]
            ni = ar * hi + ai * hr + src[rows, im]
            if store is not None:
                store[rows, re] = nr
                store[rows, im] = ni
            return (nr, ni, dr, di) if acc is not None else (nr, ni)

        init = (carry[:, re], carry[:, im])
        if acc is not None:
            init = init + (acc[:, re], acc[:, im])
        out = lax.fori_loop(0, steps, step, init, unroll=4)
        carry[:, re], carry[:, im] = out[0], out[1]
        if acc is not None:
            acc[:, re], acc[:, im] = out[2], out[3]


def _segment_carries(e_ref, pw_ref, out_ref, reverse):
    pr, pi = pw_ref[:, 0:NSTATE], pw_ref[:, NSTATE:]
    hr = jnp.zeros((1, NSTATE), F32)
    hi = jnp.zeros((1, NSTATE), F32)
    order = range(N_DEV - 1, -1, -1) if reverse else range(N_DEV)
    for j in order:
        out_ref[j:j + 1, 0:NSTATE] = hr
        out_ref[j:j + 1, NSTATE:] = hi
        tr, ti = _cmul(pr, pi, hr, hi)
        hr, hi = e_ref[j:j + 1, 0:NSTATE] + tr, e_ref[j:j + 1, NSTATE:] + ti


def _ssm_carries(name, src, mat, mode, lam8, pw, reverse):
    t = src.shape[0]
    nchunk = t // SCAN_ROWS

    def body(src_ref, mat_ref, lam_ref, pw_ref, out_ref, drive, carry):
        c = pl.program_id(0)

        @pl.when(c == 0)
        def _():
            carry[...] = jnp.zeros_like(carry)

        _drive(src_ref, mat_ref, drive, mode)
        _scan_chunk(drive, lam_ref, carry, reverse=reverse)

        @pl.when(c == nchunk - 1)
        def _():
            _segment_carries(carry, pw_ref, out_ref, reverse)

    blk = (lambda c: (nchunk - 1 - c, 0)) if reverse else (lambda c: (c, 0))
    return pl.pallas_call(
        body, grid=(nchunk,),
        in_specs=[_bs((SCAN_ROWS, SSM_W), blk), _bs(mat.shape, lambda c: (0, 0, 0)), _bs((8, 2 * NSTATE), lambda c: (0, 0)),
                  _bs((1, 2 * NSTATE), lambda c: (0, 0))],
        out_specs=_bs((8, 2 * NSTATE), lambda c: (0, 0)), out_shape=SDS((8, 2 * NSTATE), F32),
        scratch_shapes=[pltpu.VMEM((SCAN_ROWS, 2 * NSTATE), F32), pltpu.VMEM((8, 2 * NSTATE), F32)],
        compiler_params=_cp(1), name=name)(src, mat, lam8, pw)


def _ssm_fwd(u_bf, u, d_skip, bd, cd, lam8, start):
    t = u_bf.shape[0]
    nchunk = t // SCAN_ROWS
    per_seg = SCAN_ROWS // N_DEV

    def body(ub_ref, u_ref, d_ref, bd_ref, cd_ref, lam_ref, start_ref, h_ref, ys_ref, yg_ref, drive, carry, tmp):
        @pl.when(pl.program_id(0) == 0)
        def _():
            carry[...] = start_ref[...]

        _drive(ub_ref, bd_ref, drive, "nn")
        _scan_chunk(drive, lam_ref, carry, reverse=False, store=h_ref)
        for n in range(SSM_NB):
            cs = slice(n * BLK, (n + 1) * BLK)
            hr = h_ref[:, n * 512:(n + 1) * 512].astype(BF16)
            hi = h_ref[:, NSTATE + n * 512:NSTATE + (n + 1) * 512].astype(BF16)
            ys = (jnp.dot(hr, cd_ref[n], preferred_element_type=F32) + jnp.dot(hi, cd_ref[SSM_NB + n], preferred_element_type=F32)
                  + d_ref[:, cs] * u_ref[:, cs])
            ys_ref[:, cs] = ys
            tmp[n] = _gelu_parts(ys)[0]
            for j in range(N_DEV):
                yg_ref[j, :, cs] = tmp[n, pl.ds(j, per_seg, stride=N_DEV), :].astype(BF16)

    row = _bs((SCAN_ROWS, SSM_W), lambda c: (c, 0))
    h, ys, yg = pl.pallas_call(
        body, grid=(nchunk,),
        in_specs=[row, row, _bs((1, SSM_W), lambda c: (0, 0)), _bs(bd.shape, lambda c: (0, 0, 0)), _bs(cd.shape, lambda c: (0, 0, 0)),
                  _bs((8, 2 * NSTATE), lambda c: (0, 0)), _bs((8, 2 * NSTATE), lambda c: (0, 0))],
        out_specs=[_bs((SCAN_ROWS, 2 * NSTATE), lambda c: (c, 0)), row, _bs((N_DEV, per_seg, SSM_W), lambda c: (0, c, 0))],
        out_shape=[SDS((t, 2 * NSTATE), F32), SDS((t, SSM_W), F32), SDS((N_DEV, t // N_DEV, SSM_W), BF16)],
        scratch_shapes=[pltpu.VMEM((SCAN_ROWS, 2 * NSTATE), F32), pltpu.VMEM((8, 2 * NSTATE), F32),
                        pltpu.VMEM((SSM_NB, SCAN_ROWS, BLK), F32)],
        compiler_params=_cp(1), name="ssm_scan_fwd")(u_bf, u, d_skip, bd, cd, lam8, start)
    return h, ys, yg.reshape(t, SSM_W)


def _ssm_bwd(dys_bf, dys, d_skip, u_bf, h, bd, cd, lamc8, start):
    t = u_bf.shape[0]
    nchunk = t // SCAN_ROWS
    per_seg = SCAN_ROWS // N_DEV

    def body(dys_ref, dysf_ref, d_ref, u_ref, h_ref, bd_ref, cd_ref, lam_ref, start_ref, du_ref, dlam_ref, dbd_ref, dcd_ref,
             drive, adj, carry, tmp):
        c = pl.program_id(0)

        @pl.when(c == 0)
        def _():
            carry[...] = start_ref[...]
            dlam_ref[...] = jnp.zeros_like(dlam_ref)
            dbd_ref[...] = jnp.zeros_like(dbd_ref)
            dcd_ref[...] = jnp.zeros_like(dcd_ref)

        _drive(dys_ref, cd_ref, drive, "nt")
        _scan_chunk(drive, lam_ref, carry, reverse=True, store=adj, h_ref=h_ref, acc=dlam_ref)
        for n in range(SSM_NB):
            cs = slice(n * BLK, (n + 1) * BLK)
            acc = None
            for k in range(2):
                kn = k * SSM_NB + n
                ss = slice(kn * 512, (kn + 1) * 512)
                lam_b = adj[:, ss].astype(BF16)
                part = lax.dot_general(lam_b, bd_ref[kn], _DNUMS["nt"], preferred_element_type=F32)
                acc = part if acc is None else acc + part
                dbd_ref[kn] += lax.dot_general(u_ref[:, cs], lam_b, _DNUMS["tn"], preferred_element_type=F32)
                dcd_ref[kn] += lax.dot_general(h_ref[:, ss].astype(BF16), dys_ref[:, cs], _DNUMS["tn"],
                                               preferred_element_type=F32)
            tmp[n] = acc + d_ref[:, cs] * dysf_ref[:, cs]
            for j in range(N_DEV):
                du_ref[j, :, cs] = tmp[n, pl.ds(j, per_seg, stride=N_DEV), :].astype(BF16)

    rev = lambda c: (nchunk - 1 - c, 0)
    const2 = lambda c: (0, 0)
    const3 = lambda c: (0, 0, 0)
    row = _bs((SCAN_ROWS, SSM_W), rev)
    du, dlam, dbd, dcd = pl.pallas_call(
        body, grid=(nchunk,),
        in_specs=[row, row, _bs((1, SSM_W), const2), row, _bs((SCAN_ROWS, 2 * NSTATE), rev),
                  _bs(bd.shape, const3), _bs(cd.shape, const3), _bs((8, 2 * NSTATE), const2), _bs((8, 2 * NSTATE), const2)],
        out_specs=[_bs((N_DEV, per_seg, SSM_W), lambda c: (0, nchunk - 1 - c, 0)), _bs((8, 2 * NSTATE), const2),
                   _bs(bd.shape, const3), _bs(cd.shape, const3)],
        out_shape=[SDS((N_DEV, t // N_DEV, SSM_W), BF16), SDS((8, 2 * NSTATE), F32), SDS(bd.shape, F32), SDS(cd.shape, F32)],
        scratch_shapes=[pltpu.VMEM((SCAN_ROWS, 2 * NSTATE), F32), pltpu.VMEM((SCAN_ROWS, 2 * NSTATE), F32),
                        pltpu.VMEM((8, 2 * NSTATE), F32), pltpu.VMEM((SSM_NB, SCAN_ROWS, BLK), F32)],
        compiler_params=_cp(1), name="ssm_scan_bwd")(dys_bf, dys, d_skip, u_bf, h, bd, cd, lamc8, start)
    return du.reshape(t, SSM_W), dlam, dbd, dcd


def _gelu_parts(x):
    c0 = math.sqrt(2.0 / math.pi)
    inner = c0 * (x + 0.044715 * x * x * x)
    th = jnp.tanh(inner)
    val = 0.5 * x * (1.0 + th)
    grad = 0.5 * (1.0 + th) + 0.5 * x * (1.0 - th * th) * c0 * (1.0 + 3.0 * 0.044715 * x * x)
    return val, grad


def _ssm_out_bwd(d_yg, ys, u, tm):
    t = u.shape[0]
    seg = t // N_DEV

    def body(dg_ref, ys_ref, u_ref, dys_ref, dysb_ref, dd_ref, tmp):
        for n in range(SSM_W // BLK):
            for j in range(N_DEV):
                tmp[n, pl.ds(j, tm // N_DEV, stride=N_DEV), :] = dg_ref[j, :, n * BLK:(n + 1) * BLK]
        dyg = jnp.concatenate([tmp[n] for n in range(SSM_W // BLK)], axis=1)
        dys = dyg * _gelu_parts(ys_ref[...])[1]
        dys_ref[...] = dys
        dysb_ref[...] = dys.astype(BF16)
        part = jnp.sum(dys * u_ref[...], axis=0, keepdims=True)

        @pl.when(pl.program_id(0) == 0)
        def _():
            dd_ref[...] = part

        @pl.when(pl.program_id(0) > 0)
        def _():
            dd_ref[...] += part

    row = _bs((tm, SSM_W), lambda i: (i, 0))
    return pl.pallas_call(
        body, grid=(t // tm,), in_specs=[_bs((N_DEV, tm // N_DEV, SSM_W), lambda i: (0, i, 0)), row, row],
        out_specs=[row, row, _bs((1, SSM_W), lambda i: (0, 0))],
        out_shape=[SDS((t, SSM_W), F32), SDS((t, SSM_W), BF16), SDS((1, SSM_W), F32)],
        scratch_shapes=[pltpu.VMEM((SSM_W // BLK, tm, BLK), F32)], compiler_params=_cp(1), name="ssm_out_bwd")(
            d_yg.reshape(N_DEV, seg, SSM_W), ys, u)


def _block_diag(blocks):
    nb, ng, r, c = blocks.shape
    eye = jnp.eye(ng, dtype=blocks.dtype)
    return (blocks[:, :, :, None, :] * eye[None, :, None, :, None]).reshape(nb, ng * r, ng * c)


def _diag_blocks(full, r, c):
    k, nb = full.shape[:2]
    ng = full.shape[2] // r
    x = full.reshape(k, nb, ng, r, ng, c)
    eye = jnp.eye(ng, dtype=full.dtype)
    return jnp.sum(x * eye[None, None, :, None, :, None], axis=4).reshape(k, nb * ng, r, c)


_SMALL = ("g_mix", "a_re", "a_im", "log_dt", "b_re", "b_im", "c_re", "c_im", "d_skip", "g_ffn", "g_final")


def _pack_small(arrs):
    flat = jnp.concatenate([a.reshape(-1) for a in arrs])
    pad = (-flat.shape[0]) % (8 * 128)
    return jnp.pad(flat, (0, pad)).reshape(-1, 128)


def _unpack_small(packed, shapes):
    flat = packed.reshape(-1)
    out, off = [], 0
    for s in shapes:
        n = math.prod(s)
        out.append(flat[off:off + n].reshape(s))
        off += n
    return out


def kernel(x, p, positions, g_mix, w_in, a_re, a_im, log_dt, b_re, b_im, c_re, c_im, d_skip, w_attn_proj, w_glu_a, w_glu_b, w_out, g_ffn, w_ffn_gate, w_ffn_up, w_ffn_down, w_ple_gate, w_ple_proj, g_final, loss_target, m_g_mix, m_w_in, m_a_re, m_a_im, m_log_dt, m_b_re, m_b_im, m_c_re, m_c_im, m_d_skip, m_w_attn_proj, m_w_glu_a, m_w_glu_b, m_w_out, m_g_ffn, m_w_ffn_gate, m_w_ffn_up, m_w_ffn_down, m_w_ple_gate, m_w_ple_proj, m_g_final, v_g_mix, v_w_in, v_a_re, v_a_im, v_log_dt, v_b_re, v_b_im, v_c_re, v_c_im, v_d_skip, v_w_attn_proj, v_w_glu_a, v_w_glu_b, v_w_out, v_g_ffn, v_w_ffn_gate, v_w_ffn_up, v_w_ffn_down, v_w_ple_gate, v_w_ple_proj, v_g_final):
    args = dict(locals())
    t, d = x.shape[1], x.shape[2]
    inw = w_in.shape[2] * N_DEV
    fs = w_ffn_gate.shape[2]
    ff = fs * N_DEV
    ple = w_ple_proj.shape[1]
    seg = t // N_DEV
    assert inw == 3 * QK_W + SSM_W + 2 * d and t % (N_DEV * SCAN_ROWS // 8) == 0 and seg & (seg - 1) == 0
    tm = min(1024, t)
    te = min(512, t)
    tk = min(1024, t)
    ucol = (3 * QK_W) // SSM_W
    gcol = (3 * QK_W + SSM_W) // d
    assert (3 * QK_W + SSM_W) % d == 0

    x2, p2, tgt = x[0], p[0, 0], loss_target[0]
    pos = positions.reshape(t, 1)
    inv = ROPE_THETA ** (-jnp.arange(ROPE_HALF, dtype=F32) * 2.0 / ROPE_DIM)
    invf = jnp.concatenate([inv, inv, jnp.zeros((HEAD_DIM - ROPE_DIM,), F32)]).reshape(1, HEAD_DIM)

    wnames = ("w_in", "w_attn_proj", "w_glu_a", "w_glu_b", "w_out", "w_ffn_gate", "w_ffn_up", "w_ffn_down", "w_ple_gate",
              "w_ple_proj")
    kinds = ("cols", "cols", "cols", "cols", "rows", "slot", "slot", "rows", "rows", "cols")
    shards = [args[n][0].astype(BF16) for n in wnames]
    sizes = [s.shape[0] if k == "rows" else s.shape[-1] for s, k in zip(shards, kinds)]
    ag = _exchange_start("gather_weights_start", shards, kinds, sizes, True)

    row_d = _bs((tm, d), lambda i, j, k: (i, 0))
    row_e = _bs((te, d), lambda i, j, k: (i, 0))
    vec_d = _bs((1, d), lambda i, j, k: (0, 0))
    sq_w = _bs((d, d), lambda i, j, k: (0, 0))
    n1 = _rms_fwd("norm_mix", x2, g_mix + ag[3][0:1, 0:1], tm)
    W_in, = _exchange_wait("gather_w_in_wait", ag, [0], kinds, sizes, True, n1)
    qkv = _mm("qkv_proj", (t // tm, 3, 1), [("nn", n1, row_d, W_in, _bs((d, QK_W), lambda i, j, k: (0, j)))],
              [(SDS((3, t // dil, dil * GROUP_W), BF16), _bs((None, tm // dil, dil * GROUP_W), lambda i, j, k: (j, i, 0)))
               for dil in DILATIONS],
              extras=[(pos, _bs((tm, 1), lambda i, j, k: (i, 0))), (invf, _bs((1, HEAD_DIM), lambda i, j, k: (0, 0)))],
              epilogue=_rope_dilate_epilogue(tm),
              scratch=[pltpu.VMEM((tm, HEAD_DIM), F32)] * 3 + [pltpu.VMEM((QK_W // HEAD_DIM, tm, HEAD_DIM), F32)])
    z_u, = _mm("u_proj", (t // tm, 1, 1), [("nn", n1, row_d, W_in, _bs((d, SSM_W), lambda i, j, k: (0, ucol)))],
               [(SDS((t, SSM_W), F32), _bs((tm, SSM_W), lambda i, j, k: (i, 0)))])
    zg, = _mm("z_gates", (t // tm, 2, 1),
              [("nn", n1, row_d, W_in, _bs((d, d), lambda i, j, k: (0, gcol + j)))],
              [(SDS((t, 2 * d), BF16), _bs((tm, d), lambda i, j, k: (i, j)))])

    outs, lses = [], []
    for g, dil in enumerate(DILATIONS):
        o_g, l_g = _attn_fwd(qkv[g], dil, min(512, t // dil))
        outs.append(o_g)
        lses.append(l_g)
    merged = _attn_merge(outs, lses, te)
    attn, attn_bf, lts = merged[0], merged[1], merged[2:]

    nsq = seg.bit_length() - 1
    bar_re, bar_im, z_re, z_im, pw_re, pw_im = _ssm_disc(a_re[0], a_im[0], log_dt.reshape(SSM_GROUPS, 1), nsq)
    gp = SSM_GROUPS * SSM_STATE
    b_re2, b_im2 = b_re.reshape(gp, SSM_GROUP), b_im.reshape(gp, SSM_GROUP)
    bb_re, bb_im = _ssm_scale_b(z_re.reshape(gp, 1), z_im.reshape(gp, 1), b_re2, b_im2)

    def chunks(a, r, c):
        return a.reshape(SSM_NB, SSM_GROUPS // SSM_NB, r, c)

    bbt = lambda a: jnp.swapaxes(a.reshape(SSM_GROUPS, SSM_STATE, SSM_GROUP), 1, 2)
    bd = jnp.concatenate([_block_diag(chunks(bbt(bb_re), SSM_GROUP, SSM_STATE)),
                          _block_diag(chunks(bbt(bb_im), SSM_GROUP, SSM_STATE))]).astype(BF16)
    ct = lambda a: jnp.swapaxes(a[0], 1, 2)
    cd = jnp.concatenate([_block_diag(chunks(ct(c_re), SSM_STATE, SSM_GROUP)),
                          _block_diag(chunks(-ct(c_im), SSM_STATE, SSM_GROUP))]).astype(BF16)
    lam = jnp.concatenate([bar_re.reshape(1, gp), bar_im.reshape(1, gp)], axis=1)
    lamc = jnp.concatenate([bar_re.reshape(1, gp), -bar_im.reshape(1, gp)], axis=1)
    pw = jnp.concatenate([pw_re.reshape(1, gp), pw_im.reshape(1, gp)], axis=1)
    pwc = jnp.concatenate([pw_re.reshape(1, gp), -pw_im.reshape(1, gp)], axis=1)
    lam8, lamc8 = jnp.broadcast_to(lam, (8, 2 * gp)), jnp.broadcast_to(lamc, (8, 2 * gp))

    u_perm, u_bf = _permute_u(z_u, 0, te)
    start_f = _ssm_carries("ssm_carries_fwd", u_bf, bd, "nn", lam8, pw, False)
    dsk = d_skip.reshape(1, SSM_W)
    h_all, ys, yg_bf = _ssm_fwd(u_bf, u_perm, dsk, bd, cd, lam8, start_f)
    W_ap, W_ga, W_gb, W_out, W_fg, W_fu, W_fd, W_pg, W_pp = _exchange_wait(
        "gather_rest_wait", ag, list(range(1, len(wnames))), kinds, sizes, True, yg_bf)
    W_fg = jnp.swapaxes(W_fg, 0, 1).reshape(d, ff)
    W_fu = jnp.swapaxes(W_fu, 0, 1).reshape(d, ff)

    glu_w = _bs((SSM_W, d), lambda i, j, k: (0, 0))
    row_s = _bs((tm, SSM_W), lambda i, j, k: (i, 0))
    gate_a = _bs((te, d), lambda i, j, k: (i, 0))
    gate_s = _bs((te, d), lambda i, j, k: (i, 1))
    td_f32, td_bf = SDS((t, d), F32), SDS((t, d), BF16)
    m_bf, ya, yb, attn_d = _mm(
        "glu_merge", (t // tm, 1, 1),
        [("nn", yg_bf, row_s, W_ga, glu_w), ("nn", yg_bf, row_s, W_gb, glu_w), ("nn", attn_bf, row_s, W_ap, glu_w)],
        [(td_bf, row_d)] * 4, extras=[(zg, row_d), (zg, _bs((tm, d), lambda i, j, k: (i, 1)))], epilogue=_glu_merge_epilogue)

    h1, n2 = _mm("out_proj", (t // tm, 1, 1), [("nn", m_bf, row_d, W_out, sq_w)], [(td_f32, row_d), (td_bf, row_d)],
                 extras=[(x2, row_d), (g_ffn, vec_d)], epilogue=_out_norm_epilogue)

    tn_f = ff // 2
    nf = ff // tn_f
    hid_o = _bs((tm, tn_f), lambda j, i, k: (i, j))
    tf_bf = SDS((t, ff), BF16)
    a_rows = _bs((tm, d), lambda j, i, k: (i, 0))
    w_cols = _bs((d, tn_f), lambda j, i, k: (0, j))
    act, fg, fu = _mm("ffn_gate_up", (nf, t // tm, 1), [("nn", n2, a_rows, W_fg, w_cols), ("nn", n2, a_rows, W_fu, w_cols)],
                      [(tf_bf, hid_o)] * 3, epilogue=_swiglu_epilogue)
    w_once = pl.BlockSpec((d, d), lambda i, j, k: (0, 0), pipeline_mode=pl.Buffered(1))
    loss_part, dg_final, dh2, dh2_bf, dpp_bf, dpg_bf, h2_bf = _mm(
        "ffn_down_head", (t // te, 1, 1),
        [("nn", act, _bs((te, ff), lambda i, j, k: (i, 0)), W_fd,
          pl.BlockSpec((ff, d), lambda i, j, k: (0, 0), pipeline_mode=pl.Buffered(1))),
         ("nn", p2, _bs((te, ple), lambda i, j, k: (i, 0)), W_pp, _bs((ple, d), lambda i, j, k: (0, 0)))],
        [(SDS((1, 1), F32), _bs((1, 1), lambda i, j, k: (0, 0))), (SDS((1, d), F32), vec_d), (td_f32, row_e), (td_bf, row_e),
         (td_bf, row_e), (td_bf, row_e), (td_bf, row_e)],
        extras=[(h1, row_e), (g_final.reshape(1, d), vec_d), (tgt, row_e), (W_pg, w_once)], epilogue=_head_epilogue(t // te),
        scratch=[pltpu.VMEM((1, d), F32)])
    loss = lax.psum(loss_part[0, 0], ("x", "y", "c"))

    nkt = t // tk
    tok_a = lambda w: _bs((tk, w), lambda i, j, k: (k, 0))

    def wgrad(name, a, wa, b, wb):
        return _mm(name, (1, 1, nkt), [("tn", a, tok_a(wa), b, tok_a(wb))],
                   [(SDS((wa, wb), BF16), _bs((wa, wb), lambda i, j, k: (0, 0)))])[0]

    dW_pp = wgrad("dw_ple_proj", p2, ple, dpp_bf, d)
    dW_pg = wgrad("dw_ple_gate", h2_bf, d, dpg_bf, d)
    dfg_bf, dfu_bf = _mm("d_ffn_down", (nf, t // tm, 1),
                         [("nt", dh2_bf, a_rows, W_fd, _bs((tn_f, d), lambda j, i, k: (j, 0)))],
                         [(tf_bf, hid_o), (tf_bf, hid_o)], extras=[(fg, hid_o), (fu, hid_o)], epilogue=_swiglu_bwd_epilogue)
    dW_fd, = _mm("dw_ffn_down", (nf, 1, nkt), [("tn", act, _bs((tk, tn_f), lambda i, j, k: (k, i)), dh2_bf, tok_a(d))],
                 [(SDS((ff, d), BF16), _bs((tn_f, d), lambda i, j, k: (i, 0)))])
    hid_t = _bs((tk, tn_f), lambda i, j, k: (k, j))
    wg_o = [(SDS((d, ff), BF16), _bs((d, tn_f), lambda i, j, k: (0, j)))]
    dW_fg, = _mm("dw_ffn_gate", (1, nf, nkt), [("tn", n2, tok_a(d), dfg_bf, hid_t)], wg_o)
    dW_fu, = _mm("dw_ffn_up", (1, nf, nkt), [("tn", n2, tok_a(d), dfu_bf, hid_t)], wg_o)
    dW_fg = jnp.swapaxes(dW_fg.reshape(d, N_DEV, fs), 0, 1)
    dW_fu = jnp.swapaxes(dW_fu.reshape(d, N_DEV, fs), 0, 1)
    group = lambda names: ([kinds[wnames.index(n)] for n in names], [sizes[wnames.index(n)] for n in names])
    ffn_names = ("w_ffn_gate", "w_ffn_up", "w_ffn_down", "w_ple_gate", "w_ple_proj")
    rs_ffn = _exchange_start("scatter_ffn_start", [dW_fg, dW_fu, dW_fd, dW_pg, dW_pp], *group(ffn_names), False)
    hid_all = _bs((te, ff), lambda i, j, k: (i, 0))
    w_all = pl.BlockSpec((d, ff), lambda i, j, k: (0, 0), pipeline_mode=pl.Buffered(1))
    dh1, dh1_bf, dg_ffn = _mm("d_ffn_gate_up", (t // te, 1, 1),
                              [("nt", dfg_bf, hid_all, W_fg, w_all), ("nt", dfu_bf, hid_all, W_fu, w_all)],
                              [(td_f32, row_e), (td_bf, row_e), (SDS((1, d), F32), vec_d)],
                              extras=[(h1, row_e), (g_ffn, vec_d), (dh2, row_e)], epilogue=_rms_bwd_epilogue, after=rs_ffn[3])

    dW_out = wgrad("dw_out", m_bf, d, dh1_bf, d)
    glu_once = pl.BlockSpec((SSM_W, d), lambda i, j, k: (0, 0), pipeline_mode=pl.Buffered(1))
    row_es = _bs((te, SSM_W), lambda i, j, k: (i, 0))
    ts_f32 = SDS((t, SSM_W), F32)
    dz_g, dad_bf, dya_bf, dyb_bf, d_yg, d_attn = _mm(
        "d_out_proj", (t // te, 1, 1), [("nt", dh1_bf, row_e, W_out, w_once)],
        [(SDS((t, 2 * d), BF16), _bs((te, 2 * d), lambda i, j, k: (i, 0))), (td_bf, row_e), (td_bf, row_e), (td_bf, row_e),
         (ts_f32, row_es), (ts_f32, row_es)],
        extras=[(zg, gate_a), (zg, gate_s), (attn_d, row_e), (ya, row_e), (yb, row_e), (W_ga, glu_once), (W_gb, glu_once),
                (W_ap, glu_once)], epilogue=_merge_bwd_epilogue)

    dW_ga = wgrad("dw_glu_a", yg_bf, SSM_W, dya_bf, d)
    dW_gb = wgrad("dw_glu_b", yg_bf, SSM_W, dyb_bf, d)
    dys, dys_bf, dd_skip = _ssm_out_bwd(d_yg, ys, u_perm, te)
    start_b = _ssm_carries("ssm_carries_bwd", dys_bf, cd, "nt", lamc8, pwc, True)
    dz_u, dlam8, dbd, dcd = _ssm_bwd(dys_bf, dys, dsk, u_bf, h_all, bd, cd, lamc8, start_b)
    dlam = jnp.sum(dlam8, axis=0)
    dbb = _diag_blocks(dbd.reshape(2, SSM_NB, BLK, 512), SSM_GROUP, SSM_STATE)
    dbb_re = jnp.swapaxes(dbb[0], 1, 2).reshape(gp, SSM_GROUP)
    dbb_im = jnp.swapaxes(dbb[1], 1, 2).reshape(gp, SSM_GROUP)
    dcc = _diag_blocks(dcd.reshape(2, SSM_NB, 512, BLK), SSM_STATE, SSM_GROUP)
    dc_re, dc_im = jnp.swapaxes(dcc[0], 1, 2), -jnp.swapaxes(dcc[1], 1, 2)
    db_re, db_im, dz_re, dz_im = _ssm_scale_b_bwd(z_re.reshape(gp, 1), z_im.reshape(gp, 1), b_re2, b_im2, dbb_re, dbb_im)
    gshape = (SSM_GROUPS, SSM_STATE)
    da_re, da_im, dlog_dt = _ssm_disc_bwd(a_re[0], a_im[0], log_dt.reshape(SSM_GROUPS, 1), dlam[:gp].reshape(gshape),
                                          dlam[gp:].reshape(gshape), dz_re.reshape(gshape), dz_im.reshape(gshape))

    dW_ap = wgrad("dw_attn_proj", attn_bf, GROUP_W, dad_bf, d)
    pre = _attn_bwd_pre(d_attn, attn, te)
    das, deltas = pre[:N_GROUPS], pre[N_GROUPS:]
    dqkvs = [_attn_bwd(qkv[g], das[g], lts[g], deltas[g], dil, min(512, t // dil)) for g, dil in enumerate(DILATIONS)]
    dz_qkv = _undilate_rope_bwd(dqkvs, pos, invf, tm)

    dW_in, = _mm("dw_in_qkv", (1, 3, nkt), [("tn", n1, tok_a(d), dz_qkv, _bs((tk, QK_W), lambda i, j, k: (k, j)))],
                 [(SDS((d, inw), BF16), _bs((d, QK_W), lambda i, j, k: (0, j)))])
    dW_in, = _mm("dw_in_u", (1, 1, nkt), [("tn", n1, tok_a(d), dz_u, tok_a(SSM_W))],
                 [(SDS((d, inw), BF16), _bs((d, SSM_W), lambda i, j, k: (0, ucol)))], alias_to_out0=dW_in)
    dW_in, = _mm("dw_in_gates", (1, 2, nkt), [("tn", n1, tok_a(d), dz_g, _bs((tk, d), lambda i, j, k: (k, j)))],
                 [(SDS((d, inw), BF16), _bs((d, d), lambda i, j, k: (0, gcol + j)))], alias_to_out0=dW_in)
    rest_names = ("w_in", "w_attn_proj", "w_glu_a", "w_glu_b", "w_out")
    rs_in = _exchange_start("scatter_rest_start", [dW_in, dW_ap, dW_ga, dW_gb, dW_out], *group(rest_names), False)
    w_piece = lambda w, cb: pl.BlockSpec((d, w), lambda i, j, k: (0, cb), pipeline_mode=pl.Buffered(1))
    dx, dg_mix = _mm(
        "d_z_proj", (t // te, 1, 1),
        [("nt", dz_qkv, _bs((te, 3 * QK_W), lambda i, j, k: (i, 0)), W_in, w_piece(3 * QK_W, 0)),
         ("nt", dz_u, _bs((te, SSM_W), lambda i, j, k: (i, 0)), W_in, w_piece(SSM_W, ucol)),
         ("nt", dz_g, _bs((te, d), lambda i, j, k: (i, 0)), W_in, w_piece(d, gcol)),
         ("nt", dz_g, _bs((te, d), lambda i, j, k: (i, 1)), W_in, w_piece(d, gcol + 1))],
        [(td_f32, row_e), (SDS((1, d), F32), vec_d)],
        extras=[(x2, row_e), (g_mix, vec_d), (dh1, row_e)], epilogue=_rms_bwd_epilogue, after=rs_in[3])

    small_parts = dict(g_mix=dg_mix, a_re=da_re, a_im=da_im, log_dt=dlog_dt, b_re=db_re, b_im=db_im, c_re=dc_re, c_im=dc_im,
                       d_skip=dd_skip, g_ffn=dg_ffn, g_final=dg_final)
    small = _pack_small([small_parts[n] for n in _SMALL])
    received = {}
    for names, started, label in ((ffn_names, rs_ffn, "ffn"), (rest_names, rs_in, "rest")):
        landed = _exchange_wait(f"scatter_{label}_wait", started, list(range(len(names))), *group(names), False, dx)
        received.update(zip(names, landed))

    new = {}
    for n in wnames:
        new[n] = [o.reshape(args[n].shape)
                  for o in _adamw("adamw_" + n, received[n], args[n][0], args["m_" + n][0], args["v_" + n][0])]
    pk = lambda pre: _pack_small([args[pre + n] for n in _SMALL])
    sm = _adamw("adamw_small", _gather_small(small), pk(""), pk("m_"), pk("v_"))
    shapes = [args[n].shape for n in _SMALL]
    for n, vals in zip(_SMALL, zip(*[_unpack_small(o, shapes) for o in sm])):
        new[n] = list(vals)

    order = ("g_mix", "w_in", "a_re", "a_im", "log_dt", "b_re", "b_im", "c_re", "c_im", "d_skip", "w_attn_proj", "w_glu_a",
             "w_glu_b", "w_out", "g_ffn", "w_ffn_gate", "w_ffn_up", "w_ffn_down", "w_ple_gate", "w_ple_proj", "g_final")
    return (loss, dx.reshape(x.shape), *[new[n][0] for n in order], *[new[n][1] for n in order],
            *[new[n][2] for n in order], *[new[n][3] for n in order])
```

```python
import functools
import math

import jax
import jax.numpy as jnp
from jax import lax
from jax.experimental import pallas as pl
from jax.experimental.pallas import tpu as pltpu

F32 = jnp.float32
BF16 = jnp.bfloat16
SDS = jax.ShapeDtypeStruct

N_DEV = 8
HEAD_DIM = 128
HEADS_PER_GROUP = 4
GROUP_W = HEADS_PER_GROUP * HEAD_DIM
DILATIONS = (1, 4, 16)
N_GROUPS = len(DILATIONS)
QK_W = N_GROUPS * GROUP_W
BLK = 128
ROPE_THETA = 500000.0
ROPE_DIM = HEAD_DIM // 4
ROPE_HALF = ROPE_DIM // 2
SSM_W = 512
SSM_GROUP = 16
SSM_GROUPS = SSM_W // SSM_GROUP
SSM_STATE = 64
NSTATE = SSM_GROUPS * SSM_STATE
SSM_NB = 4
EPS = 1e-6
ADAM_LR, ADAM_B1, ADAM_B2, ADAM_EPS, ADAM_WD, ADAM_STEP = 0.001, 0.9, 0.999, 1e-08, 0.01, 10
NEG = -1e30

VMEM_LIMIT = 52 * 1024 * 1024
SCAN_ROWS = 512
SCAN_LANES = 512


def _cp(n):
    return pltpu.CompilerParams(dimension_semantics=("arbitrary",) * n, vmem_limit_bytes=VMEM_LIMIT)


def _sigmoid(x):
    return 0.5 * jnp.tanh(0.5 * x) + 0.5


_DNUMS = {"nn": (((1,), (0,)), ((), ())), "nt": (((1,), (1,)), ((), ())), "tn": (((0,), (0,)), ((), ()))}


def _bs(shape, fn):
    return pl.BlockSpec(shape, fn)


def _store_all(prods, extra_refs, out_refs, scratch_refs):
    r = prods[0]
    for p in prods[1:]:
        r = r + p
    for e in extra_refs:
        r = r + e[...]
    for o in out_refs:
        o[...] = r.astype(o.dtype)


def _mm(name, grid, pairs, outs, extras=(), epilogue=_store_all, scratch=(), alias_to_out0=None, after=None):
    nk = grid[2]
    npair = len(pairs)
    steps = [p[5] if len(p) > 5 else nk for p in pairs]

    def block(spec):
        return tuple(s for s in spec.block_shape if s is not None)

    def rows2d(shape):
        return (math.prod(shape[:-1]), shape[-1]) if len(shape) == 3 else shape

    acc_shapes = [jax.eval_shape(lambda u, v, dn=_DNUMS[p[0]]: lax.dot_general(u, v, dn, preferred_element_type=F32),
                                 SDS(rows2d(block(p[2])), BF16), SDS(block(p[4]), BF16)).shape for p in pairs]
    if nk == 1:
        acc_shapes = []
    n_in = 2 * npair + len(extras) + (alias_to_out0 is not None) + (after is not None)

    def body(*refs):
        extra_refs = refs[2 * npair:2 * npair + len(extras)]
        out_refs = refs[n_in:n_in + len(outs)]
        rest = refs[n_in + len(outs):]
        acc_refs = rest[:len(acc_shapes)]
        scratch_refs = rest[len(acc_refs):]
        k = pl.program_id(2)

        def product(i):
            a = refs[2 * i][...]
            if a.ndim == 3:
                a = a.reshape(-1, a.shape[-1])
            return lax.dot_general(a.astype(BF16), refs[2 * i + 1][...].astype(BF16), _DNUMS[pairs[i][0]],
                                   preferred_element_type=F32)

        if nk == 1:
            epilogue([product(i) for i in range(npair)], extra_refs, out_refs, scratch_refs)
            return
        for i in range(npair):
            @pl.when(k == 0)
            def _(i=i):
                acc_refs[i][...] = product(i)

            @pl.when((k > 0) & (k < steps[i]))
            def _(i=i):
                acc_refs[i][...] += product(i)

        @pl.when(k == nk - 1)
        def _():
            epilogue([a[...] for a in acc_refs], extra_refs, out_refs, scratch_refs)

    ins, in_specs = [], []
    for p in pairs:
        ins += [p[1], p[3]]
        in_specs += [p[2], p[4]]
    ins += [e[0] for e in extras]
    in_specs += [e[1] for e in extras]
    aliases = {}
    if alias_to_out0 is not None:
        aliases = {len(ins): 0}
        ins.append(alias_to_out0)
        in_specs.append(pl.BlockSpec(memory_space=pl.ANY))
    if after is not None:
        ins.append(after)
        in_specs.append(pl.BlockSpec(memory_space=pl.ANY))
    scratch_shapes = [pltpu.VMEM(s, F32) for s in acc_shapes] + list(scratch)
    return pl.pallas_call(body, grid=grid, in_specs=in_specs, out_specs=[o[1] for o in outs], out_shape=[o[0] for o in outs],
                          scratch_shapes=scratch_shapes, input_output_aliases=aliases, compiler_params=_cp(3), name=name)(*ins)


def _my_index():
    return 4 * lax.axis_index("x") + 2 * lax.axis_index("y") + lax.axis_index("c")


def _peer(d):
    mx, my, mc = lax.axis_index("x"), lax.axis_index("y"), lax.axis_index("c")
    return (mx ^ ((d >> 2) & 1), my ^ ((d >> 1) & 1), mc ^ (d & 1))


def _win(ref, kind, j, n):
    if kind == "all":
        return ref
    if kind == "slot":
        return ref.at[j]
    if kind == "rows":
        return ref.at[pl.ds(pl.multiple_of(j * n, 8), n)]
    return ref.at[:, pl.ds(pl.multiple_of(j * n, 128), n)]


def _win7(ref, kind, n):
    if kind == "slot":
        return ref.at[pl.ds(0, 7)]
    if kind == "rows":
        return ref.at[pl.ds(0, 7 * n)]
    return ref.at[:, pl.ds(0, 7 * n)]


def _full_shape(shard_shape, kind):
    if kind == "slot":
        return (N_DEV,) + tuple(shard_shape)
    if kind == "rows":
        return (N_DEV * shard_shape[0],) + tuple(shard_shape[1:])
    return (shard_shape[0], N_DEV * shard_shape[1])


def _shard_shape(full_shape, kind, n):
    if kind == "all":
        return tuple(full_shape)
    if kind == "slot":
        return tuple(full_shape[1:])
    if kind == "rows":
        return (n,) + tuple(full_shape[1:])
    return (full_shape[0], n)


_HBM = pl.BlockSpec(memory_space=pltpu.HBM)
_SEM = pl.BlockSpec(memory_space=pltpu.SEMAPHORE)
_DATAFLOW = pltpu.SideEffectType.DATAFLOW_SIDE_EFFECTING


def _exchange_start(name, srcs, kinds, sizes, gather):
    n = len(srcs)
    if gather:
        lands = [lax.empty(_full_shape(s.shape, k), s.dtype) for s, k in zip(srcs, kinds)]
    else:
        lands = [lax.empty((N_DEV,) + _shard_shape(s.shape, k, z), s.dtype) for s, k, z in zip(srcs, kinds, sizes)]

    def body(*refs):
        src, land = refs[:n], refs[n:2 * n]
        send_sems, recv_sems, local_sems = refs[2 * n], refs[2 * n + 1], refs[2 * n + 2]
        token = refs[4 * n + 3]
        me = _my_index()
        for a in range(n):
            _local_copy(src[a], land[a], kinds[a], sizes[a], gather, me, local_sems.at[a]).start()
        for a in range(n):
            for d in range(1, N_DEV):
                px, py, pc = _peer(d)
                if gather:
                    s_ref, d_ref = src[a], _win(land[a], kinds[a], me, sizes[a])
                else:
                    s_ref, d_ref = _win(src[a], kinds[a], 4 * px + 2 * py + pc, sizes[a]), land[a].at[me]
                pltpu.make_async_remote_copy(src_ref=s_ref, dst_ref=d_ref, send_sem=send_sems.at[a], recv_sem=recv_sems.at[a],
                                             device_id=(px, py, pc), device_id_type=pl.DeviceIdType.MESH).start()
        token[...] = jnp.zeros_like(token)

    hbm = [pltpu.with_memory_space_constraint(a, pltpu.HBM) for a in list(srcs) + lands]
    out = pl.pallas_call(
        body, name=name, in_specs=[_HBM] * (2 * n),
        out_shape=[pltpu.SemaphoreType.DMA((n,))] * 3 + [pltpu.HBM(a.shape, a.dtype) for a in hbm] + [SDS((8, 128), F32)],
        out_specs=[_SEM] * 3 + [_HBM] * (2 * n) + [pl.BlockSpec(memory_space=pltpu.VMEM)],
        input_output_aliases={i: 3 + i for i in range(2 * n)},
        compiler_params=pltpu.CompilerParams(has_side_effects=_DATAFLOW))(*hbm)
    return out[0:3], out[3:3 + n], out[3 + n:3 + 2 * n], out[-1]


def _local_copy(src, land, kind, size, gather, me, sem):
    if gather:
        return pltpu.make_async_copy(src, _win(land, kind, me, size), sem)
    return pltpu.make_async_copy(_win(src, kind, me, size), land.at[me], sem)


def _exchange_wait(name, started, which, kinds, sizes, gather, after):
    sems, srcs, lands, _ = started
    n = len(which)

    def body(*refs):
        src, land = refs[:n], refs[n:2 * n]
        send_ref, recv_ref, local_ref = refs[2 * n:2 * n + 3]
        me = _my_index()
        my_id = (lax.axis_index("x"), lax.axis_index("y"), lax.axis_index("c"))
        for i, a in enumerate(which):
            seven = _win7(land[i], kinds[a], sizes[a]) if gather else land[i].at[pl.ds(0, 7)]
            pltpu.make_async_remote_copy(src_ref=seven, dst_ref=seven, send_sem=send_ref.at[a], recv_sem=recv_ref.at[a],
                                         device_id=my_id, device_id_type=pl.DeviceIdType.MESH).wait()
            _local_copy(src[i], land[i], kinds[a], sizes[a], gather, me, local_ref.at[a]).wait()

    hbm = [srcs[a] for a in which] + [lands[a] for a in which]
    out = pl.pallas_call(
        body, name=name, in_specs=[_HBM] * (2 * n) + [_SEM] * 3 + [pl.BlockSpec(memory_space=pl.ANY)],
        out_shape=[pltpu.HBM(a.shape, a.dtype) for a in hbm], out_specs=[_HBM] * (2 * n),
        input_output_aliases={i: i for i in range(2 * n)},
        compiler_params=pltpu.CompilerParams(has_side_effects=_DATAFLOW))(*hbm, *sems, after)
    return out[n:]


def _gather_small(small):
    def body(in_ref, out_ref, send_sem, recv_sem, local_sem):
        me = _my_index()
        my_id = (lax.axis_index("x"), lax.axis_index("y"), lax.axis_index("c"))
        cp = pltpu.make_async_copy(in_ref, out_ref.at[me], local_sem)
        cp.start()
        for d in range(1, N_DEV):
            pltpu.make_async_remote_copy(src_ref=in_ref, dst_ref=out_ref.at[me], send_sem=send_sem, recv_sem=recv_sem,
                                         device_id=_peer(d), device_id_type=pl.DeviceIdType.MESH).start()
        seven = out_ref.at[pl.ds(0, 7)]
        pltpu.make_async_remote_copy(src_ref=seven, dst_ref=seven, send_sem=send_sem, recv_sem=recv_sem, device_id=my_id,
                                     device_id_type=pl.DeviceIdType.MESH).wait()
        cp.wait()

    any_spec = pl.BlockSpec(memory_space=pl.ANY)
    return pl.pallas_call(body, in_specs=[any_spec], out_specs=any_spec, out_shape=SDS((N_DEV,) + small.shape, F32),
                          scratch_shapes=[pltpu.SemaphoreType.DMA] * 3, name="gather_small")(small)


def _adamw(name, recv, w, m, v):
    rows, cols = w.shape
    tr = max(c for c in range(16, 257, 16) if rows % c == 0) if rows % 16 == 0 else rows

    def body(r_ref, w_ref, m_ref, v_ref, g_ref, d_ref, nm_ref, nv_ref):
        g = r_ref[0].astype(F32)
        for s in range(1, N_DEV):
            g = g + r_ref[s].astype(F32)
        nm = ADAM_B1 * m_ref[...] + (1.0 - ADAM_B1) * g
        nv = ADAM_B2 * v_ref[...] + (1.0 - ADAM_B2) * (g * g)
        m_hat = nm / (1.0 - ADAM_B1 ** ADAM_STEP)
        v_hat = nv / (1.0 - ADAM_B2 ** ADAM_STEP)
        g_ref[...] = g
        d_ref[...] = -ADAM_LR * (m_hat / (jnp.sqrt(v_hat) + ADAM_EPS) + ADAM_WD * w_ref[...])
        nm_ref[...] = nm
        nv_ref[...] = nv

    blk = _bs((tr, cols), lambda i: (i, 0))
    return pl.pallas_call(
        body, grid=(rows // tr,), in_specs=[_bs((N_DEV, tr, cols), lambda i: (0, i, 0)), blk, blk, blk],
        out_specs=[blk] * 4, out_shape=[SDS((rows, cols), F32)] * 4, compiler_params=_cp(1), name=name)(recv, w, m, v)


def _rms_fwd(name, x, g, tm):
    t, d = x.shape

    def body(x_ref, g_ref, n_ref):
        xv = x_ref[...]
        r = lax.rsqrt(jnp.mean(xv * xv, axis=-1, keepdims=True) + EPS)
        n_ref[...] = (xv * r * g_ref[...]).astype(BF16)

    return pl.pallas_call(body, grid=(t // tm,), in_specs=[_bs((tm, d), lambda i: (i, 0)), _bs((1, d), lambda i: (0, 0))],
                          out_specs=_bs((tm, d), lambda i: (i, 0)), out_shape=SDS((t, d), BF16), compiler_params=_cp(1),
                          name=name)(x, g)


def _accumulate_rows(ref, part):
    @pl.when(pl.program_id(0) == 0)
    def _():
        ref[...] = part

    @pl.when(pl.program_id(0) > 0)
    def _():
        ref[...] += part


def _rms_bwd_epilogue(prods, extra_refs, out_refs, scratch_refs):
    dyv = prods[0]
    for p in prods[1:]:
        dyv = dyv + p
    if len(extra_refs) > 3:
        dyv = dyv + extra_refs[3][...]
    xv = extra_refs[0][...]
    r = lax.rsqrt(jnp.mean(xv * xv, axis=-1, keepdims=True) + EPS)
    xh = xv * r
    dxh = dyv * extra_refs[1][...]
    dx = extra_refs[2][...] + r * (dxh - xh * jnp.mean(dxh * xh, axis=-1, keepdims=True))
    for o in out_refs[:-1]:
        o[...] = dx.astype(o.dtype)
    _accumulate_rows(out_refs[-1], jnp.sum(dyv * xh, axis=0, keepdims=True))


def _out_norm_epilogue(prods, extra_refs, out_refs, scratch_refs):
    h = prods[0] + extra_refs[0][...]
    r = lax.rsqrt(jnp.mean(h * h, axis=-1, keepdims=True) + EPS)
    out_refs[0][...] = h
    out_refs[1][...] = (h * r * extra_refs[1][...]).astype(BF16)


def _glu_merge_epilogue(prods, extra_refs, out_refs, scratch_refs):
    ya, yb, ad = prods
    ga, gs = extra_refs[0][...].astype(F32), extra_refs[1][...].astype(F32)
    m = _sigmoid(ga) * ad + _sigmoid(gs) * (ya * _sigmoid(yb))
    out_refs[0][...] = m.astype(BF16)
    for o, val in zip(out_refs[1:], (ya, yb, ad)):
        o[...] = val.astype(o.dtype)


def _merge_bwd_epilogue(prods, extra_refs, out_refs, scratch_refs):
    dmv = prods[0]
    d = dmv.shape[1]
    ga, gs = _sigmoid(extra_refs[0][...].astype(F32)), _sigmoid(extra_refs[1][...].astype(F32))
    adv, yav = extra_refs[2][...].astype(F32), extra_refs[3][...].astype(F32)
    sb = _sigmoid(extra_refs[4][...].astype(F32))
    out_refs[0][:, 0:d] = (dmv * adv * ga * (1.0 - ga)).astype(BF16)
    out_refs[0][:, d:2 * d] = (dmv * (yav * sb) * gs * (1.0 - gs)).astype(BF16)
    dad = (dmv * ga).astype(BF16)
    dsd = dmv * gs
    dya = (dsd * sb).astype(BF16)
    dyb = (dsd * yav * sb * (1.0 - sb)).astype(BF16)
    out_refs[1][...], out_refs[2][...], out_refs[3][...] = dad, dya, dyb
    nt = _DNUMS["nt"]
    out_refs[4][...] = (lax.dot_general(dya, extra_refs[5][...], nt, preferred_element_type=F32)
                        + lax.dot_general(dyb, extra_refs[6][...], nt, preferred_element_type=F32))
    out_refs[5][...] = lax.dot_general(dad, extra_refs[7][...], nt, preferred_element_type=F32)


def _swiglu_epilogue(prods, extra_refs, out_refs, scratch_refs):
    gv, uv = prods
    out_refs[0][...] = (gv * _sigmoid(gv) * uv).astype(BF16)
    out_refs[1][...] = gv.astype(out_refs[1].dtype)
    out_refs[2][...] = uv.astype(out_refs[2].dtype)


def _swiglu_bwd_epilogue(n_row_tiles):
    def epilogue(prods, extra_refs, out_refs, scratch_refs):
        dav = prods[0]
        gv, uv = extra_refs[0][...].astype(F32), extra_refs[1][...].astype(F32)
        sg = _sigmoid(gv)
        dfg = (dav * uv * sg * (1.0 + gv * (1.0 - sg))).astype(BF16)
        dfu = (dav * gv * sg).astype(BF16)
        out_refs[0][...], out_refs[1][...] = dfg, dfu
        n2 = extra_refs[2][...]
        i = pl.program_id(1)
        for acc, dw_ref, dh in zip(scratch_refs, out_refs[2:], (dfg, dfu)):
            part = lax.dot_general(n2, dh, _DNUMS["tn"], preferred_element_type=F32)

            @pl.when(i == 0)
            def _(acc=acc, part=part):
                acc[...] = part

            @pl.when(i > 0)
            def _(acc=acc, part=part):
                acc[...] += part

            @pl.when(i == n_row_tiles - 1)
            def _(acc=acc, dw_ref=dw_ref):
                dw_ref[...] = acc[...].astype(BF16)

    return epilogue


def _head_epilogue(n_tiles):
    def epilogue(prods, extra_refs, out_refs, scratch_refs):
        h2 = prods[0] + extra_refs[0][...]
        h2_bf = h2.astype(BF16)
        out_refs[6][...] = h2_bf
        pgv = jnp.dot(h2_bf, extra_refs[3][...], preferred_element_type=F32)
        ppv = prods[1]
        d = pgv.shape[1]
        lacc = scratch_refs[0]
        sg = _sigmoid(pgv)
        h3 = h2 + sg * ppv
        r = lax.rsqrt(jnp.mean(h3 * h3, axis=-1, keepdims=True) + EPS)
        xh = h3 * r
        gv = extra_refs[1][...]
        diff = xh * gv - extra_refs[2][...]
        dout = diff * (1.0 / d)
        dxh = dout * gv
        dh3 = r * (dxh - xh * jnp.mean(dxh * xh, axis=-1, keepdims=True))
        dpg = (dh3 * ppv * sg * (1.0 - sg)).astype(BF16)
        dh2 = dh3 + lax.dot_general(dpg, extra_refs[3][...], _DNUMS["nt"], preferred_element_type=F32)
        out_refs[2][...] = dh2
        out_refs[3][...] = dh2.astype(BF16)
        out_refs[4][...] = (dh3 * sg).astype(BF16)
        out_refs[5][...] = dpg
        _accumulate_rows(out_refs[1], jnp.sum(dout * xh, axis=0, keepdims=True))
        _accumulate_rows(lacc, jnp.sum(diff * diff, axis=0, keepdims=True))

        @pl.when(pl.program_id(0) == n_tiles - 1)
        def _():
            out_refs[0][...] = (0.5 / d) * jnp.sum(lacc[...], axis=-1, keepdims=True)

    return epilogue


def _strided(r, n, d):
    return pl.ds(r, n, stride=d) if d > 1 else pl.ds(0, n)


def _rope_tables(pos_ref, invf_ref, c_s, s1_s, s2_s):
    ang = pos_ref[...].astype(F32) * invf_ref[...]
    lane = lax.broadcasted_iota(jnp.int32, ang.shape, 1)
    sn = jnp.sin(ang)
    c_s[...] = jnp.where(lane < ROPE_DIM, jnp.cos(ang), 1.0)
    s1_s[...] = jnp.where(lane < ROPE_HALF, -sn, 0.0)
    s2_s[...] = jnp.where((lane >= ROPE_HALF) & (lane < ROPE_DIM), sn, 0.0)


def _rope_dilate_epilogue(tm):
    def epilogue(prods, extra_refs, out_refs, scratch_refs):
        zv = prods[0]
        pos_ref, invf_ref = extra_refs
        c_s, s1_s, s2_s, rot = scratch_refs
        c = pl.program_id(1)

        @pl.when(c == 0)
        def _():
            _rope_tables(pos_ref, invf_ref, c_s, s1_s, s2_s)

        @pl.when(c < 2)
        def _():
            cc, s1, s2 = c_s[...], s1_s[...], s2_s[...]
            for h in range(QK_W // HEAD_DIM):
                xv = zv[:, h * HEAD_DIM:(h + 1) * HEAD_DIM]
                rot[h] = xv * cc + pltpu.roll(xv, HEAD_DIM - ROPE_HALF, 1) * s1 + pltpu.roll(xv, ROPE_HALF, 1) * s2

        @pl.when(c == 2)
        def _():
            for h in range(QK_W // HEAD_DIM):
                rot[h] = zv[:, h * HEAD_DIM:(h + 1) * HEAD_DIM]

        for g, (d, o_ref) in enumerate(zip(DILATIONS, out_refs)):
            n = tm // d
            for r in range(d):
                for hh in range(HEADS_PER_GROUP):
                    oc = r * GROUP_W + hh * HEAD_DIM
                    o_ref[:, oc:oc + HEAD_DIM] = rot[g * HEADS_PER_GROUP + hh, _strided(r, n, d), :].astype(BF16)

    return epilogue


def _band_masks(first_tile):
    qi = lax.broadcasted_iota(jnp.int32, (BLK, 2 * BLK), 0)
    kj = lax.broadcasted_iota(jnp.int32, (BLK, 2 * BLK), 1)
    band = (kj >= qi) & (kj <= qi + BLK)
    return band, band & ((kj >= BLK) | jnp.logical_not(first_tile))


def _attn_fwd(qkv, d, qt):
    ell = qkv.shape[1]
    nsub = qt // BLK
    scale = 1.0 / math.sqrt(HEAD_DIM)

    def body(q_ref, kc_ref, kp_ref, vc_ref, vp_ref, o_ref, lse_ref, kcat, vcat):
        nb = pl.program_id(1)
        kcat[0:BLK, :] = kp_ref[...]
        kcat[BLK:, :] = kc_ref[...]
        vcat[0:BLK, :] = vp_ref[...]
        vcat[BLK:, :] = vc_ref[...]
        lane = lax.broadcasted_iota(jnp.int32, (BLK, HEAD_DIM), 1)
        band, band_first = _band_masks(nb == 0)
        for b in range(nsub):
            valid = band_first if b == 0 else band
            lse_t = jnp.zeros((BLK, HEAD_DIM), F32)
            for hh in range(HEADS_PER_GROUP):
                cs = slice(hh * HEAD_DIM, (hh + 1) * HEAD_DIM)
                qb = q_ref[b * BLK:(b + 1) * BLK, cs]
                kk = kcat[b * BLK:(b + 2) * BLK, cs]
                vv = vcat[b * BLK:(b + 2) * BLK, cs]
                s = lax.dot_general(qb, kk, _DNUMS["nt"], preferred_element_type=F32) * scale
                s = jnp.where(valid, s, NEG)
                mx = jnp.max(s, axis=-1, keepdims=True)
                p = jnp.exp(s - mx)
                den = jnp.sum(p, axis=-1, keepdims=True)
                o = jnp.dot(p.astype(BF16), vv, preferred_element_type=F32) / den
                o_ref[b * BLK:(b + 1) * BLK, cs] = o
                lse_t = jnp.where(lane == hh, mx + jnp.log(den), lse_t)
            lse_ref[b * BLK:(b + 1) * BLK, :] = lse_t

    cur = lambda c: _bs((None, qt, GROUP_W), lambda r, nb: (c, nb, r))
    prev = lambda c: _bs((None, BLK, GROUP_W), lambda r, nb: (c, jnp.maximum(nb * nsub - 1, 0), r))
    return pl.pallas_call(
        body, grid=(d, ell // qt), in_specs=[cur(0), cur(1), prev(1), cur(2), prev(2)],
        out_specs=[_bs((qt, GROUP_W), lambda r, nb: (nb, r)), _bs((None, qt, HEAD_DIM), lambda r, nb: (r, nb, 0))],
        out_shape=[SDS((ell, d * GROUP_W), F32), SDS((d, ell, HEAD_DIM), F32)],
        scratch_shapes=[pltpu.VMEM((qt + BLK, GROUP_W), BF16)] * 2, compiler_params=_cp(2), name=f"attn_fwd_d{d}")(
            qkv, qkv, qkv, qkv, qkv)


def _attn_merge(outs, lses, tm):
    t = outs[0].shape[0]

    def body(o0, o1, o2, l0, l1, l2, attn_ref, attn_bf_ref, t0, t1, t2, so, sl, lt_s):
        for g, (d, o_ref, l_ref) in enumerate(zip(DILATIONS, (o0, o1, o2), (l0, l1, l2))):
            n = tm // d
            for r in range(d):
                rows = _strided(r, n, d)
                for hh in range(HEADS_PER_GROUP):
                    oc = r * GROUP_W + hh * HEAD_DIM
                    so[g * HEADS_PER_GROUP + hh, rows, :] = o_ref[:, oc:oc + HEAD_DIM]
                sl[g, rows, :] = l_ref[r]
        ls = [sl[g] for g in range(N_GROUPS)]
        mx = jnp.maximum(jnp.maximum(ls[0], ls[1]), ls[2])
        es = [jnp.exp(l - mx) for l in ls]
        den = es[0] + es[1] + es[2]
        ws = [e / den for e in es]
        lt_s[...] = mx + jnp.log(den)
        for hh in range(HEADS_PER_GROUP):
            cs = slice(hh * HEAD_DIM, (hh + 1) * HEAD_DIM)
            a = ws[0][:, hh:hh + 1] * so[hh]
            for g in range(1, N_GROUPS):
                a = a + ws[g][:, hh:hh + 1] * so[g * HEADS_PER_GROUP + hh]
            attn_ref[:, cs] = a
            attn_bf_ref[:, cs] = a.astype(BF16)
        for d, t_ref in zip(DILATIONS, (t0, t1, t2)):
            n = tm // d
            for r in range(d):
                t_ref[r] = lt_s[_strided(r, n, d), :]

    dil = lambda d: _bs((tm // d, d * GROUP_W), lambda i: (i, 0))
    lsp = lambda d: _bs((d, tm // d, HEAD_DIM), lambda i: (0, i, 0))
    row = _bs((tm, GROUP_W), lambda i: (i, 0))
    return pl.pallas_call(
        body, grid=(t // tm,),
        in_specs=[dil(d) for d in DILATIONS] + [lsp(d) for d in DILATIONS],
        out_specs=[row, row] + [lsp(d) for d in DILATIONS],
        out_shape=[SDS((t, GROUP_W), F32), SDS((t, GROUP_W), BF16)] + [SDS(l.shape, F32) for l in lses],
        scratch_shapes=[pltpu.VMEM((N_GROUPS * HEADS_PER_GROUP, tm, HEAD_DIM), F32), pltpu.VMEM((N_GROUPS, tm, HEAD_DIM), F32),
                        pltpu.VMEM((tm, HEAD_DIM), F32)],
        compiler_params=_cp(1), name="attn_merge")(*outs, *lses)


def _attn_bwd_pre(d_attn, attn, tm):
    t = attn.shape[0]

    def body(da_ref, a_ref, g0, g1, g2, e0, e1, e2, dl_s, da_s):
        lane = lax.broadcasted_iota(jnp.int32, (tm, HEAD_DIM), 1)
        dl = jnp.zeros((tm, HEAD_DIM), F32)
        for hh in range(HEADS_PER_GROUP):
            cs = slice(hh * HEAD_DIM, (hh + 1) * HEAD_DIM)
            dav = da_ref[:, cs]
            da_s[hh] = dav
            dl = jnp.where(lane == hh, jnp.sum(dav * a_ref[:, cs], axis=-1, keepdims=True), dl)
        dl_s[...] = dl
        for d, g_ref, e_ref in zip(DILATIONS, (g0, g1, g2), (e0, e1, e2)):
            n = tm // d
            for r in range(d):
                rows = _strided(r, n, d)
                for hh in range(HEADS_PER_GROUP):
                    oc = r * GROUP_W + hh * HEAD_DIM
                    g_ref[:, oc:oc + HEAD_DIM] = da_s[hh, rows, :].astype(BF16)
                e_ref[r] = dl_s[rows, :]

    row = _bs((tm, GROUP_W), lambda i: (i, 0))
    return pl.pallas_call(
        body, grid=(t // tm,), in_specs=[row, row],
        out_specs=[_bs((tm // d, d * GROUP_W), lambda i: (i, 0)) for d in DILATIONS]
        + [_bs((d, tm // d, HEAD_DIM), lambda i: (0, i, 0)) for d in DILATIONS],
        out_shape=[SDS((t // d, d * GROUP_W), BF16) for d in DILATIONS]
        + [SDS((d, t // d, HEAD_DIM), F32) for d in DILATIONS],
        scratch_shapes=[pltpu.VMEM((tm, HEAD_DIM), F32), pltpu.VMEM((HEADS_PER_GROUP, tm, HEAD_DIM), F32)],
        compiler_params=_cp(1), name="attn_bwd_pre")(d_attn, attn)


def _attn_bwd(qkv, d_a, lt, delta, d, qt):
    ell = qkv.shape[1]
    nsub = qt // BLK
    ntile = ell // qt
    nblk = ell // BLK
    scale = 1.0 / math.sqrt(HEAD_DIM)

    def body(q_ref, qn_ref, kc_ref, kp_ref, vc_ref, vp_ref, da_ref, dan_ref, lt_ref, ltn_ref, dl_ref, dln_ref, o_ref,
             kcat, vcat, dk_acc, dv_acc):
        nb = pl.program_id(1)
        kcat[0:BLK, :] = kp_ref[...]
        kcat[BLK:, :] = kc_ref[...]
        vcat[0:BLK, :] = vp_ref[...]
        vcat[BLK:, :] = vc_ref[...]
        qi = lax.broadcasted_iota(jnp.int32, (BLK, BLK), 0)
        kj = lax.broadcasted_iota(jnp.int32, (BLK, BLK), 1)
        valid_next = (kj >= qi) & (nb < ntile - 1)
        band, band_first = _band_masks(nb == 0)
        for hh in range(HEADS_PER_GROUP):
            cs = slice(hh * HEAD_DIM, (hh + 1) * HEAD_DIM)
            dk_acc[...] = jnp.zeros_like(dk_acc)
            dv_acc[...] = jnp.zeros_like(dv_acc)
            for b in range(nsub):
                rs = slice(b * BLK, (b + 1) * BLK)
                ks = slice(b * BLK, (b + 2) * BLK)
                valid = band_first if b == 0 else band
                qb, kk, vv, dab = q_ref[rs, cs], kcat[ks, cs], vcat[ks, cs], da_ref[rs, cs]
                s = lax.dot_general(qb, kk, _DNUMS["nt"], preferred_element_type=F32) * scale
                p = jnp.where(valid, jnp.exp(s - lt_ref[rs, hh:hh + 1]), 0.0)
                dp = lax.dot_general(dab, vv, _DNUMS["nt"], preferred_element_type=F32)
                ds = (p * (dp - dl_ref[rs, hh:hh + 1])).astype(BF16)
                o_ref[0, rs, cs] = jnp.dot(ds, kk, preferred_element_type=F32) * scale
                dk_acc[ks, :] += lax.dot_general(ds, qb, _DNUMS["tn"], preferred_element_type=F32) * scale
                dv_acc[ks, :] += lax.dot_general(p.astype(BF16), dab, _DNUMS["tn"], preferred_element_type=F32)
            ks = slice(nsub * BLK, (nsub + 1) * BLK)
            qn, kl, vl, dan = qn_ref[:, cs], kcat[ks, cs], vcat[ks, cs], dan_ref[:, cs]
            s = lax.dot_general(qn, kl, _DNUMS["nt"], preferred_element_type=F32) * scale
            p = jnp.where(valid_next, jnp.exp(s - ltn_ref[:, hh:hh + 1]), 0.0)
            dp = lax.dot_general(dan, vl, _DNUMS["nt"], preferred_element_type=F32)
            ds = (p * (dp - dln_ref[:, hh:hh + 1])).astype(BF16)
            dk_acc[ks, :] += lax.dot_general(ds, qn, _DNUMS["tn"], preferred_element_type=F32) * scale
            dv_acc[ks, :] += lax.dot_general(p.astype(BF16), dan, _DNUMS["tn"], preferred_element_type=F32)
            o_ref[1, :, cs] = dk_acc[BLK:, :]
            o_ref[2, :, cs] = dv_acc[BLK:, :]

    nxt = lambda nb: jnp.minimum((nb + 1) * nsub, nblk - 1)
    prv = lambda nb: jnp.maximum(nb * nsub - 1, 0)
    cur3 = lambda c: _bs((None, qt, GROUP_W), lambda r, nb: (c, nb, r))
    in_specs = [
        cur3(0), _bs((None, BLK, GROUP_W), lambda r, nb: (0, nxt(nb), r)),
        cur3(1), _bs((None, BLK, GROUP_W), lambda r, nb: (1, prv(nb), r)),
        cur3(2), _bs((None, BLK, GROUP_W), lambda r, nb: (2, prv(nb), r)),
        _bs((qt, GROUP_W), lambda r, nb: (nb, r)), _bs((BLK, GROUP_W), lambda r, nb: (nxt(nb), r)),
        _bs((None, qt, HEAD_DIM), lambda r, nb: (r, nb, 0)), _bs((None, BLK, HEAD_DIM), lambda r, nb: (r, nxt(nb), 0)),
        _bs((None, qt, HEAD_DIM), lambda r, nb: (r, nb, 0)), _bs((None, BLK, HEAD_DIM), lambda r, nb: (r, nxt(nb), 0)),
    ]
    return pl.pallas_call(
        body, grid=(d, ntile), in_specs=in_specs, out_specs=_bs((3, qt, GROUP_W), lambda r, nb: (0, nb, r)),
        out_shape=SDS((3, ell, d * GROUP_W), F32),
        scratch_shapes=[pltpu.VMEM((qt + BLK, GROUP_W), BF16)] * 2 + [pltpu.VMEM((qt + BLK, HEAD_DIM), F32)] * 2,
        compiler_params=_cp(2), name=f"attn_bwd_d{d}")(qkv, qkv, qkv, qkv, qkv, qkv, d_a, d_a, lt, lt, delta, delta)


def _undilate_rope_bwd(dqkvs, pos, invf, tm):
    t = pos.shape[0]

    def body(g0, g1, g2, pos_ref, invf_ref, o_ref, c_s, s1_s, s2_s, nat):
        c = pl.program_id(1)

        @pl.when(c == 0)
        def _():
            _rope_tables(pos_ref, invf_ref, c_s, s1_s, s2_s)

        for g, (d, g_ref) in enumerate(zip(DILATIONS, (g0, g1, g2))):
            n = tm // d
            for r in range(d):
                for hh in range(HEADS_PER_GROUP):
                    oc = r * GROUP_W + hh * HEAD_DIM
                    nat[g * HEADS_PER_GROUP + hh, _strided(r, n, d), :] = g_ref[:, oc:oc + HEAD_DIM]

        @pl.when(c < 2)
        def _():
            cc, s1, s2 = c_s[...], s1_s[...], s2_s[...]
            for h in range(QK_W // HEAD_DIM):
                xv = nat[h]
                y = xv * cc - pltpu.roll(xv, HEAD_DIM - ROPE_HALF, 1) * s1 - pltpu.roll(xv, ROPE_HALF, 1) * s2
                o_ref[:, h * HEAD_DIM:(h + 1) * HEAD_DIM] = y.astype(BF16)

        @pl.when(c == 2)
        def _():
            for h in range(QK_W // HEAD_DIM):
                o_ref[:, h * HEAD_DIM:(h + 1) * HEAD_DIM] = nat[h].astype(BF16)

    return pl.pallas_call(
        body, grid=(t // tm, 3),
        in_specs=[_bs((None, tm // d, d * GROUP_W), lambda i, c: (c, i, 0)) for d in DILATIONS]
        + [_bs((tm, 1), lambda i, c: (i, 0)), _bs((1, HEAD_DIM), lambda i, c: (0, 0))],
        out_specs=_bs((tm, QK_W), lambda i, c: (i, c)), out_shape=SDS((t, 3 * QK_W), BF16),
        scratch_shapes=[pltpu.VMEM((tm, HEAD_DIM), F32)] * 3 + [pltpu.VMEM((QK_W // HEAD_DIM, tm, HEAD_DIM), F32)],
        compiler_params=_cp(2), name="undilate_rope_bwd")(*dqkvs, pos, invf)


def _cmul(ar, ai, br, bi):
    return ar * br - ai * bi, ar * bi + ai * br


def _ssm_disc(a_re, a_im, log_dt, nsq):
    def body(lr_ref, li_ref, ldt_ref, br_ref, bi_ref, zr_ref, zi_ref, pr_ref, pi_ref):
        lr, li = lr_ref[...], li_ref[...]
        dt = jnp.exp(ldt_ref[...])
        mag = jnp.exp(lr * dt)
        bar_re, bar_im = mag * jnp.cos(li * dt), mag * jnp.sin(li * dt)
        nr, ni = bar_re - 1.0, bar_im
        den = lr * lr + li * li
        br_ref[...], bi_ref[...] = bar_re, bar_im
        zr_ref[...] = (nr * lr + ni * li) / den
        zi_ref[...] = (ni * lr - nr * li) / den
        pr, pi = bar_re, bar_im
        for _ in range(nsq):
            pr, pi = _cmul(pr, pi, pr, pi)
        pr_ref[...], pi_ref[...] = pr, pi

    return pl.pallas_call(body, out_shape=[SDS(a_re.shape, F32)] * 6, name="ssm_discretise")(a_re, a_im, log_dt)


def _ssm_scale_b(z_re, z_im, b_re, b_im):
    def body(zr_ref, zi_ref, br_ref, bi_ref, or_ref, oi_ref):
        zr, zi, br, bi = zr_ref[...], zi_ref[...], br_ref[...], bi_ref[...]
        or_ref[...] = zr * br - zi * bi
        oi_ref[...] = zr * bi + zi * br

    return pl.pallas_call(body, out_shape=[SDS(b_re.shape, F32)] * 2, name="ssm_scale_b")(z_re, z_im, b_re, b_im)


def _ssm_scale_b_bwd(z_re, z_im, b_re, b_im, g_re, g_im):
    def body(zr_ref, zi_ref, br_ref, bi_ref, gr_ref, gi_ref, dbr_ref, dbi_ref, dzr_ref, dzi_ref):
        zr, zi, br, bi, gr, gi = zr_ref[...], zi_ref[...], br_ref[...], bi_ref[...], gr_ref[...], gi_ref[...]
        dbr_ref[...] = zr * gr + zi * gi
        dbi_ref[...] = zr * gi - zi * gr
        dzr_ref[...] = jnp.sum(br * gr + bi * gi, axis=-1, keepdims=True)
        dzi_ref[...] = jnp.sum(br * gi - bi * gr, axis=-1, keepdims=True)

    return pl.pallas_call(body, out_shape=[SDS(b_re.shape, F32)] * 2 + [SDS(z_re.shape, F32)] * 2,
                          name="ssm_scale_b_bwd")(z_re, z_im, b_re, b_im, g_re, g_im)


def _ssm_disc_bwd(a_re, a_im, log_dt, gb_re, gb_im, gz_re, gz_im):
    def body(lr_ref, li_ref, ldt_ref, gbr_ref, gbi_ref, gzr_ref, gzi_ref, dar_ref, dai_ref, dldt_ref):
        lr, li = lr_ref[...], li_ref[...]
        dt = jnp.exp(ldt_ref[...])
        mag = jnp.exp(lr * dt)
        bar_re, bar_im = mag * jnp.cos(li * dt), mag * jnp.sin(li * dt)
        nr, ni = bar_re - 1.0, bar_im
        den = lr * lr + li * li
        zr, zi = (nr * lr + ni * li) / den, (ni * lr - nr * li) / den
        gzr, gzi = gzr_ref[...], gzi_ref[...]
        gbr = gbr_ref[...] + (lr * gzr - li * gzi) / den
        gbi = gbi_ref[...] + (lr * gzi + li * gzr) / den
        qr, qi = (zr * lr + zi * li) / den, (zi * lr - zr * li) / den
        dar_ref[...] = dt * (bar_re * gbr + bar_im * gbi) - qr * gzr - qi * gzi
        dai_ref[...] = dt * (bar_re * gbi - bar_im * gbr) - qr * gzi + qi * gzr
        wr, wi = lr * bar_re - li * bar_im, lr * bar_im + li * bar_re
        dldt_ref[...] = dt * jnp.sum(wr * gbr + wi * gbi, axis=-1, keepdims=True)

    return pl.pallas_call(body, out_shape=[SDS(a_re.shape, F32)] * 2 + [SDS(log_dt.shape, F32)],
                          name="ssm_discretise_bwd")(a_re, a_im, log_dt, gb_re, gb_im, gz_re, gz_im)


def _interleave_epilogue(prods, extra_refs, out_refs, scratch_refs):
    uv = prods[0]
    tmp = scratch_refs[0]
    n = uv.shape[0] // N_DEV
    for b in range(SSM_W // BLK):
        cs = slice(b * BLK, (b + 1) * BLK)
        for j in range(N_DEV):
            tmp[b, pl.ds(j, n, stride=N_DEV), :] = uv[j * n:(j + 1) * n, cs]
        out_refs[0][:, cs] = tmp[b]
        out_refs[1][:, cs] = tmp[b].astype(BF16)


def _drive(src_ref, mat_ref, dst, mode):
    for kn in range(2 * SSM_NB):
        n = kn % SSM_NB
        a = src_ref[:, n * BLK:(n + 1) * BLK]
        dst[:, kn * 512:(kn + 1) * 512] = lax.dot_general(a, mat_ref[kn], _DNUMS[mode], preferred_element_type=F32)


def _scan_chunk(src, lam_ref, carry, *, reverse, store=None, h_ref=None, acc=None):
    steps = src.shape[0] // 8
    for c in range(NSTATE // SCAN_LANES):
        re = slice(c * SCAN_LANES, (c + 1) * SCAN_LANES)
        im = slice(NSTATE + c * SCAN_LANES, NSTATE + (c + 1) * SCAN_LANES)
        ar, ai = lam_ref[:, re], lam_ref[:, im]

        def step(s, val):
            i = (steps - 1 - s) if reverse else s
            rows = pl.ds(pl.multiple_of(i * 8, 8), 8)
            if acc is not None:
                hr, hi, dr, di = val
                pr, pi = h_ref[rows, re], h_ref[rows, im]
                dr = dr + hr * pr + hi * pi
                di = di + hi * pr - hr * pi
            else:
                hr, hi = val
            nr = ar * hr - ai * hi + src[rows, re]
            ni = ar * hi + ai * hr + src[rows, im]
            if store is not None:
                store[rows, re] = nr
                store[rows, im] = ni
            return (nr, ni, dr, di) if acc is not None else (nr, ni)

        init = (carry[:, re], carry[:, im])
        if acc is not None:
            init = init + (acc[:, re], acc[:, im])
        out = lax.fori_loop(0, steps, step, init, unroll=4)
        carry[:, re], carry[:, im] = out[0], out[1]
        if acc is not None:
            acc[:, re], acc[:, im] = out[2], out[3]


def _segment_carries(e_ref, pw_ref, out_ref, reverse):
    pr, pi = pw_ref[:, 0:NSTATE], pw_ref[:, NSTATE:]
    hr = jnp.zeros((1, NSTATE), F32)
    hi = jnp.zeros((1, NSTATE), F32)
    order = range(N_DEV - 1, -1, -1) if reverse else range(N_DEV)
    for j in order:
        out_ref[j:j + 1, 0:NSTATE] = hr
        out_ref[j:j + 1, NSTATE:] = hi
        tr, ti = _cmul(pr, pi, hr, hi)
        hr, hi = e_ref[j:j + 1, 0:NSTATE] + tr, e_ref[j:j + 1, NSTATE:] + ti


def _ssm_carries(name, src, mat, mode, lam8, pw, reverse):
    t = src.shape[0]
    nchunk = t // SCAN_ROWS

    def body(src_ref, mat_ref, lam_ref, pw_ref, out_ref, drive, carry):
        c = pl.program_id(0)

        @pl.when(c == 0)
        def _():
            carry[...] = jnp.zeros_like(carry)

        _drive(src_ref, mat_ref, drive, mode)
        _scan_chunk(drive, lam_ref, carry, reverse=reverse)

        @pl.when(c == nchunk - 1)
        def _():
            _segment_carries(carry, pw_ref, out_ref, reverse)

    blk = (lambda c: (nchunk - 1 - c, 0)) if reverse else (lambda c: (c, 0))
    return pl.pallas_call(
        body, grid=(nchunk,),
        in_specs=[_bs((SCAN_ROWS, SSM_W), blk), _bs(mat.shape, lambda c: (0, 0, 0)), _bs((8, 2 * NSTATE), lambda c: (0, 0)),
                  _bs((1, 2 * NSTATE), lambda c: (0, 0))],
        out_specs=_bs((8, 2 * NSTATE), lambda c: (0, 0)), out_shape=SDS((8, 2 * NSTATE), F32),
        scratch_shapes=[pltpu.VMEM((SCAN_ROWS, 2 * NSTATE), F32), pltpu.VMEM((8, 2 * NSTATE), F32)],
        compiler_params=_cp(1), name=name)(src, mat, lam8, pw)


def _ssm_fwd(u_bf, u, d_skip, bd, cd, lam8, start):
    t = u_bf.shape[0]
    nchunk = t // SCAN_ROWS
    per_seg = SCAN_ROWS // N_DEV

    def body(ub_ref, u_ref, d_ref, bd_ref, cd_ref, lam_ref, start_ref, h_ref, ys_ref, yg_ref, drive, carry, tmp):
        @pl.when(pl.program_id(0) == 0)
        def _():
            carry[...] = start_ref[...]

        _drive(ub_ref, bd_ref, drive, "nn")
        _scan_chunk(drive, lam_ref, carry, reverse=False, store=h_ref)
        for n in range(SSM_NB):
            cs = slice(n * BLK, (n + 1) * BLK)
            hr = h_ref[:, n * 512:(n + 1) * 512].astype(BF16)
            hi = h_ref[:, NSTATE + n * 512:NSTATE + (n + 1) * 512].astype(BF16)
            ys = (jnp.dot(hr, cd_ref[n], preferred_element_type=F32) + jnp.dot(hi, cd_ref[SSM_NB + n], preferred_element_type=F32)
                  + d_ref[:, cs] * u_ref[:, cs])
            ys_ref[:, cs] = ys
            tmp[n] = _gelu_parts(ys)[0]
            for j in range(N_DEV):
                yg_ref[j, :, cs] = tmp[n, pl.ds(j, per_seg, stride=N_DEV), :].astype(BF16)

    row = _bs((SCAN_ROWS, SSM_W), lambda c: (c, 0))
    h, ys, yg = pl.pallas_call(
        body, grid=(nchunk,),
        in_specs=[row, row, _bs((1, SSM_W), lambda c: (0, 0)), _bs(bd.shape, lambda c: (0, 0, 0)), _bs(cd.shape, lambda c: (0, 0, 0)),
                  _bs((8, 2 * NSTATE), lambda c: (0, 0)), _bs((8, 2 * NSTATE), lambda c: (0, 0))],
        out_specs=[_bs((SCAN_ROWS, 2 * NSTATE), lambda c: (c, 0)), row, _bs((N_DEV, per_seg, SSM_W), lambda c: (0, c, 0))],
        out_shape=[SDS((t, 2 * NSTATE), F32), SDS((t, SSM_W), F32), SDS((N_DEV, t // N_DEV, SSM_W), BF16)],
        scratch_shapes=[pltpu.VMEM((SCAN_ROWS, 2 * NSTATE), F32), pltpu.VMEM((8, 2 * NSTATE), F32),
                        pltpu.VMEM((SSM_NB, SCAN_ROWS, BLK), F32)],
        compiler_params=_cp(1), name="ssm_scan_fwd")(u_bf, u, d_skip, bd, cd, lam8, start)
    return h, ys, yg.reshape(t, SSM_W)


def _ssm_bwd(dys_bf, dys, d_skip, u_bf, h, bd, cd, lamc8, start):
    t = u_bf.shape[0]
    nchunk = t // SCAN_ROWS
    per_seg = SCAN_ROWS // N_DEV

    def body(dys_ref, dysf_ref, d_ref, u_ref, h_ref, bd_ref, cd_ref, lam_ref, start_ref, du_ref, dlam_ref, dbd_ref, dcd_ref,
             drive, adj, carry, tmp):
        c = pl.program_id(0)

        @pl.when(c == 0)
        def _():
            carry[...] = start_ref[...]
            dlam_ref[...] = jnp.zeros_like(dlam_ref)
            dbd_ref[...] = jnp.zeros_like(dbd_ref)
            dcd_ref[...] = jnp.zeros_like(dcd_ref)

        _drive(dys_ref, cd_ref, drive, "nt")
        _scan_chunk(drive, lam_ref, carry, reverse=True, store=adj, h_ref=h_ref, acc=dlam_ref)
        for n in range(SSM_NB):
            cs = slice(n * BLK, (n + 1) * BLK)
            acc = None
            for k in range(2):
                kn = k * SSM_NB + n
                ss = slice(kn * 512, (kn + 1) * 512)
                lam_b = adj[:, ss].astype(BF16)
                part = lax.dot_general(lam_b, bd_ref[kn], _DNUMS["nt"], preferred_element_type=F32)
                acc = part if acc is None else acc + part
                dbd_ref[kn] += lax.dot_general(u_ref[:, cs], lam_b, _DNUMS["tn"], preferred_element_type=F32)
                dcd_ref[kn] += lax.dot_general(h_ref[:, ss].astype(BF16), dys_ref[:, cs], _DNUMS["tn"],
                                               preferred_element_type=F32)
            tmp[n] = acc + d_ref[:, cs] * dysf_ref[:, cs]
            for j in range(N_DEV):
                du_ref[j, :, cs] = tmp[n, pl.ds(j, per_seg, stride=N_DEV), :].astype(BF16)

    rev = lambda c: (nchunk - 1 - c, 0)
    const2 = lambda c: (0, 0)
    const3 = lambda c: (0, 0, 0)
    row = _bs((SCAN_ROWS, SSM_W), rev)
    du, dlam, dbd, dcd = pl.pallas_call(
        body, grid=(nchunk,),
        in_specs=[row, row, _bs((1, SSM_W), const2), row, _bs((SCAN_ROWS, 2 * NSTATE), rev),
                  _bs(bd.shape, const3), _bs(cd.shape, const3), _bs((8, 2 * NSTATE), const2), _bs((8, 2 * NSTATE), const2)],
        out_specs=[_bs((N_DEV, per_seg, SSM_W), lambda c: (0, nchunk - 1 - c, 0)), _bs((8, 2 * NSTATE), const2),
                   _bs(bd.shape, const3), _bs(cd.shape, const3)],
        out_shape=[SDS((N_DEV, t // N_DEV, SSM_W), BF16), SDS((8, 2 * NSTATE), F32), SDS(bd.shape, F32), SDS(cd.shape, F32)],
        scratch_shapes=[pltpu.VMEM((SCAN_ROWS, 2 * NSTATE), F32), pltpu.VMEM((SCAN_ROWS, 2 * NSTATE), F32),
                        pltpu.VMEM((8, 2 * NSTATE), F32), pltpu.VMEM((SSM_NB, SCAN_ROWS, BLK), F32)],
        compiler_params=_cp(1), name="ssm_scan_bwd")(dys_bf, dys, d_skip, u_bf, h, bd, cd, lamc8, start)
    return du.reshape(t, SSM_W), dlam, dbd, dcd


def _gelu_parts(x):
    c0 = math.sqrt(2.0 / math.pi)
    inner = c0 * (x + 0.044715 * x * x * x)
    th = jnp.tanh(inner)
    val = 0.5 * x * (1.0 + th)
    grad = 0.5 * (1.0 + th) + 0.5 * x * (1.0 - th * th) * c0 * (1.0 + 3.0 * 0.044715 * x * x)
    return val, grad


def _ssm_out_bwd(d_yg, ys, u, tm):
    t = u.shape[0]
    seg = t // N_DEV

    def body(dg_ref, ys_ref, u_ref, dys_ref, dysb_ref, dd_ref, tmp):
        for n in range(SSM_W // BLK):
            for j in range(N_DEV):
                tmp[n, pl.ds(j, tm // N_DEV, stride=N_DEV), :] = dg_ref[j, :, n * BLK:(n + 1) * BLK]
        dyg = jnp.concatenate([tmp[n] for n in range(SSM_W // BLK)], axis=1)
        dys = dyg * _gelu_parts(ys_ref[...])[1]
        dys_ref[...] = dys
        dysb_ref[...] = dys.astype(BF16)
        part = jnp.sum(dys * u_ref[...], axis=0, keepdims=True)

        @pl.when(pl.program_id(0) == 0)
        def _():
            dd_ref[...] = part

        @pl.when(pl.program_id(0) > 0)
        def _():
            dd_ref[...] += part

    row = _bs((tm, SSM_W), lambda i: (i, 0))
    return pl.pallas_call(
        body, grid=(t // tm,), in_specs=[_bs((N_DEV, tm // N_DEV, SSM_W), lambda i: (0, i, 0)), row, row],
        out_specs=[row, row, _bs((1, SSM_W), lambda i: (0, 0))],
        out_shape=[SDS((t, SSM_W), F32), SDS((t, SSM_W), BF16), SDS((1, SSM_W), F32)],
        scratch_shapes=[pltpu.VMEM((SSM_W // BLK, tm, BLK), F32)], compiler_params=_cp(1), name="ssm_out_bwd")(
            d_yg.reshape(N_DEV, seg, SSM_W), ys, u)


def _block_diag(blocks):
    nb, ng, r, c = blocks.shape
    eye = jnp.eye(ng, dtype=blocks.dtype)
    return (blocks[:, :, :, None, :] * eye[None, :, None, :, None]).reshape(nb, ng * r, ng * c)


def _diag_blocks(full, r, c):
    k, nb = full.shape[:2]
    ng = full.shape[2] // r
    x = full.reshape(k, nb, ng, r, ng, c)
    eye = jnp.eye(ng, dtype=full.dtype)
    return jnp.sum(x * eye[None, None, :, None, :, None], axis=4).reshape(k, nb * ng, r, c)


_SMALL = ("g_mix", "a_re", "a_im", "log_dt", "b_re", "b_im", "c_re", "c_im", "d_skip", "g_ffn", "g_final")


def _pack_small(arrs):
    flat = jnp.concatenate([a.reshape(-1) for a in arrs])
    pad = (-flat.shape[0]) % (8 * 128)
    return jnp.pad(flat, (0, pad)).reshape(-1, 128)


def _unpack_small(packed, shapes):
    flat = packed.reshape(-1)
    out, off = [], 0
    for s in shapes:
        n = math.prod(s)
        out.append(flat[off:off + n].reshape(s))
        off += n
    return out


def kernel(x, p, positions, g_mix, w_in, a_re, a_im, log_dt, b_re, b_im, c_re, c_im, d_skip, w_attn_proj, w_glu_a, w_glu_b, w_out, g_ffn, w_ffn_gate, w_ffn_up, w_ffn_down, w_ple_gate, w_ple_proj, g_final, loss_target, m_g_mix, m_w_in, m_a_re, m_a_im, m_log_dt, m_b_re, m_b_im, m_c_re, m_c_im, m_d_skip, m_w_attn_proj, m_w_glu_a, m_w_glu_b, m_w_out, m_g_ffn, m_w_ffn_gate, m_w_ffn_up, m_w_ffn_down, m_w_ple_gate, m_w_ple_proj, m_g_final, v_g_mix, v_w_in, v_a_re, v_a_im, v_log_dt, v_b_re, v_b_im, v_c_re, v_c_im, v_d_skip, v_w_attn_proj, v_w_glu_a, v_w_glu_b, v_w_out, v_g_ffn, v_w_ffn_gate, v_w_ffn_up, v_w_ffn_down, v_w_ple_gate, v_w_ple_proj, v_g_final):
    args = dict(locals())
    t, d = x.shape[1], x.shape[2]
    inw = w_in.shape[2] * N_DEV
    fs = w_ffn_gate.shape[2]
    ff = fs * N_DEV
    ple = w_ple_proj.shape[1]
    seg = t // N_DEV
    assert inw == 3 * QK_W + SSM_W + 2 * d and t % (N_DEV * SCAN_ROWS // 8) == 0 and seg & (seg - 1) == 0
    tm = min(1024, t)
    te = min(512, t)
    tk = min(1024, t)
    ucol = (3 * QK_W) // SSM_W
    gcol = (3 * QK_W + SSM_W) // d
    assert (3 * QK_W + SSM_W) % d == 0

    x2, p2, tgt = x[0], p[0, 0], loss_target[0]
    pos = positions.reshape(t, 1)
    inv = ROPE_THETA ** (-jnp.arange(ROPE_HALF, dtype=F32) * 2.0 / ROPE_DIM)
    invf = jnp.concatenate([inv, inv, jnp.zeros((HEAD_DIM - ROPE_DIM,), F32)]).reshape(1, HEAD_DIM)

    wnames = ("w_in", "w_attn_proj", "w_glu_a", "w_glu_b", "w_out", "w_ffn_gate", "w_ffn_up", "w_ffn_down", "w_ple_gate",
              "w_ple_proj")
    kinds = ("cols", "cols", "cols", "cols", "rows", "slot", "slot", "rows", "rows", "cols")
    shards = [args[n][0].astype(BF16) for n in wnames]
    sizes = [s.shape[0] if k == "rows" else s.shape[-1] for s, k in zip(shards, kinds)]
    ag = _exchange_start("gather_weights_start", shards, kinds, sizes, True)

    row_d = _bs((tm, d), lambda i, j, k: (i, 0))
    row_e = _bs((te, d), lambda i, j, k: (i, 0))
    vec_d = _bs((1, d), lambda i, j, k: (0, 0))
    sq_w = _bs((d, d), lambda i, j, k: (0, 0))
    n1 = _rms_fwd("norm_mix", x2, g_mix + ag[3][0:1, 0:1], tm)
    W_in, = _exchange_wait("gather_w_in_wait", ag, [0], kinds, sizes, True, n1)
    qkv = _mm("qkv_proj", (t // tm, 3, 1), [("nn", n1, row_d, W_in, _bs((d, QK_W), lambda i, j, k: (0, j)))],
              [(SDS((3, t // dil, dil * GROUP_W), BF16), _bs((None, tm // dil, dil * GROUP_W), lambda i, j, k: (j, i, 0)))
               for dil in DILATIONS],
              extras=[(pos, _bs((tm, 1), lambda i, j, k: (i, 0))), (invf, _bs((1, HEAD_DIM), lambda i, j, k: (0, 0)))],
              epilogue=_rope_dilate_epilogue(tm),
              scratch=[pltpu.VMEM((tm, HEAD_DIM), F32)] * 3 + [pltpu.VMEM((QK_W // HEAD_DIM, tm, HEAD_DIM), F32)])
    row_s = _bs((tm, SSM_W), lambda i, j, k: (i, 0))
    u_perm, u_bf = _mm("u_proj", (t // tm, 1, 1),
                       [("nn", n1.reshape(N_DEV, seg, d), _bs((N_DEV, tm // N_DEV, d), lambda i, j, k: (0, i, 0)), W_in,
                         _bs((d, SSM_W), lambda i, j, k: (0, ucol)))],
                       [(SDS((t, SSM_W), F32), row_s), (SDS((t, SSM_W), BF16), row_s)], epilogue=_interleave_epilogue,
                       scratch=[pltpu.VMEM((SSM_W // BLK, tm, BLK), F32)])
    zg, = _mm("z_gates", (t // tm, 2, 1),
              [("nn", n1, row_d, W_in, _bs((d, d), lambda i, j, k: (0, gcol + j)))],
              [(SDS((t, 2 * d), BF16), _bs((tm, d), lambda i, j, k: (i, j)))])

    outs, lses = [], []
    for g, dil in enumerate(DILATIONS):
        o_g, l_g = _attn_fwd(qkv[g], dil, min(512, t // dil))
        outs.append(o_g)
        lses.append(l_g)
    merged = _attn_merge(outs, lses, te)
    attn, attn_bf, lts = merged[0], merged[1], merged[2:]

    nsq = seg.bit_length() - 1
    bar_re, bar_im, z_re, z_im, pw_re, pw_im = _ssm_disc(a_re[0], a_im[0], log_dt.reshape(SSM_GROUPS, 1), nsq)
    gp = SSM_GROUPS * SSM_STATE
    b_re2, b_im2 = b_re.reshape(gp, SSM_GROUP), b_im.reshape(gp, SSM_GROUP)
    bb_re, bb_im = _ssm_scale_b(z_re.reshape(gp, 1), z_im.reshape(gp, 1), b_re2, b_im2)

    def chunks(a, r, c):
        return a.reshape(SSM_NB, SSM_GROUPS // SSM_NB, r, c)

    bbt = lambda a: jnp.swapaxes(a.reshape(SSM_GROUPS, SSM_STATE, SSM_GROUP), 1, 2)
    bd = jnp.concatenate([_block_diag(chunks(bbt(bb_re), SSM_GROUP, SSM_STATE)),
                          _block_diag(chunks(bbt(bb_im), SSM_GROUP, SSM_STATE))]).astype(BF16)
    ct = lambda a: jnp.swapaxes(a[0], 1, 2)
    cd = jnp.concatenate([_block_diag(chunks(ct(c_re), SSM_STATE, SSM_GROUP)),
                          _block_diag(chunks(-ct(c_im), SSM_STATE, SSM_GROUP))]).astype(BF16)
    lam = jnp.concatenate([bar_re.reshape(1, gp), bar_im.reshape(1, gp)], axis=1)
    lamc = jnp.concatenate([bar_re.reshape(1, gp), -bar_im.reshape(1, gp)], axis=1)
    pw = jnp.concatenate([pw_re.reshape(1, gp), pw_im.reshape(1, gp)], axis=1)
    pwc = jnp.concatenate([pw_re.reshape(1, gp), -pw_im.reshape(1, gp)], axis=1)
    lam8, lamc8 = jnp.broadcast_to(lam, (8, 2 * gp)), jnp.broadcast_to(lamc, (8, 2 * gp))

    start_f = _ssm_carries("ssm_carries_fwd", u_bf, bd, "nn", lam8, pw, False)
    dsk = d_skip.reshape(1, SSM_W)
    h_all, ys, yg_bf = _ssm_fwd(u_bf, u_perm, dsk, bd, cd, lam8, start_f)
    W_ap, W_ga, W_gb, W_out, W_fg, W_fu, W_fd, W_pg, W_pp = _exchange_wait(
        "gather_rest_wait", ag, list(range(1, len(wnames))), kinds, sizes, True, yg_bf)
    W_fg = jnp.swapaxes(W_fg, 0, 1).reshape(d, ff)
    W_fu = jnp.swapaxes(W_fu, 0, 1).reshape(d, ff)

    glu_w = _bs((SSM_W, d), lambda i, j, k: (0, 0))
    row_s = _bs((tm, SSM_W), lambda i, j, k: (i, 0))
    gate_a = _bs((te, d), lambda i, j, k: (i, 0))
    gate_s = _bs((te, d), lambda i, j, k: (i, 1))
    td_f32, td_bf = SDS((t, d), F32), SDS((t, d), BF16)
    m_bf, ya, yb, attn_d = _mm(
        "glu_merge", (t // tm, 1, 1),
        [("nn", yg_bf, row_s, W_ga, glu_w), ("nn", yg_bf, row_s, W_gb, glu_w), ("nn", attn_bf, row_s, W_ap, glu_w)],
        [(td_bf, row_d)] * 4, extras=[(zg, row_d), (zg, _bs((tm, d), lambda i, j, k: (i, 1)))], epilogue=_glu_merge_epilogue)

    h1, n2 = _mm("out_proj", (t // tm, 1, 1), [("nn", m_bf, row_d, W_out, sq_w)], [(td_f32, row_d), (td_bf, row_d)],
                 extras=[(x2, row_d), (g_ffn, vec_d)], epilogue=_out_norm_epilogue)

    tn_f = ff // 2
    nf = ff // tn_f
    hid_o = _bs((tm, tn_f), lambda j, i, k: (i, j))
    tf_bf = SDS((t, ff), BF16)
    a_rows = _bs((tm, d), lambda j, i, k: (i, 0))
    w_cols = _bs((d, tn_f), lambda j, i, k: (0, j))
    act, fg, fu = _mm("ffn_gate_up", (nf, t // tm, 1), [("nn", n2, a_rows, W_fg, w_cols), ("nn", n2, a_rows, W_fu, w_cols)],
                      [(tf_bf, hid_o)] * 3, epilogue=_swiglu_epilogue)
    w_once = pl.BlockSpec((d, d), lambda i, j, k: (0, 0), pipeline_mode=pl.Buffered(1))
    loss_part, dg_final, dh2, dh2_bf, dpp_bf, dpg_bf, h2_bf = _mm(
        "ffn_down_head", (t // te, 1, 1),
        [("nn", act, _bs((te, ff), lambda i, j, k: (i, 0)), W_fd,
          pl.BlockSpec((ff, d), lambda i, j, k: (0, 0), pipeline_mode=pl.Buffered(1))),
         ("nn", p2, _bs((te, ple), lambda i, j, k: (i, 0)), W_pp, _bs((ple, d), lambda i, j, k: (0, 0)))],
        [(SDS((1, 1), F32), _bs((1, 1), lambda i, j, k: (0, 0))), (SDS((1, d), F32), vec_d), (td_f32, row_e), (td_bf, row_e),
         (td_bf, row_e), (td_bf, row_e), (td_bf, row_e)],
        extras=[(h1, row_e), (g_final.reshape(1, d), vec_d), (tgt, row_e), (W_pg, w_once)], epilogue=_head_epilogue(t // te),
        scratch=[pltpu.VMEM((1, d), F32)])
    loss = lax.psum(loss_part[0, 0], ("x", "y", "c"))

    nkt = t // tk
    tok_a = lambda w: _bs((tk, w), lambda i, j, k: (k, 0))

    def wgrad(name, a, wa, b, wb):
        return _mm(name, (1, 1, nkt), [("tn", a, tok_a(wa), b, tok_a(wb))],
                   [(SDS((wa, wb), BF16), _bs((wa, wb), lambda i, j, k: (0, 0)))])[0]

    dW_pp = wgrad("dw_ple_proj", p2, ple, dpp_bf, d)
    dW_pg = wgrad("dw_ple_gate", h2_bf, d, dpg_bf, d)
    hid_e = _bs((te, tn_f), lambda j, i, k: (i, j))
    rows_e = _bs((te, d), lambda j, i, k: (i, 0))
    wg_o = (SDS((d, ff), BF16), _bs((d, tn_f), lambda j, i, k: (0, j)))
    dfg_bf, dfu_bf, dW_fg, dW_fu = _mm(
        "d_ffn_down", (nf, t // te, 1), [("nt", dh2_bf, rows_e, W_fd, _bs((tn_f, d), lambda j, i, k: (j, 0)))],
        [(tf_bf, hid_e), (tf_bf, hid_e), wg_o, wg_o], extras=[(fg, hid_e), (fu, hid_e), (n2, rows_e)],
        epilogue=_swiglu_bwd_epilogue(t // te), scratch=[pltpu.VMEM((d, tn_f), F32)] * 2)
    dW_fd, = _mm("dw_ffn_down", (nf, 1, nkt), [("tn", act, _bs((tk, tn_f), lambda i, j, k: (k, i)), dh2_bf, tok_a(d))],
                 [(SDS((ff, d), BF16), _bs((tn_f, d), lambda i, j, k: (i, 0)))])
    dW_fg = jnp.swapaxes(dW_fg.reshape(d, N_DEV, fs), 0, 1)
    dW_fu = jnp.swapaxes(dW_fu.reshape(d, N_DEV, fs), 0, 1)
    group = lambda names: ([kinds[wnames.index(n)] for n in names], [sizes[wnames.index(n)] for n in names])
    ffn_names = ("w_ffn_gate", "w_ffn_up", "w_ffn_down", "w_ple_gate", "w_ple_proj")
    rs_ffn = _exchange_start("scatter_ffn_start", [dW_fg, dW_fu, dW_fd, dW_pg, dW_pp], *group(ffn_names), False)
    hid_all = _bs((te, ff), lambda i, j, k: (i, 0))
    w_all = pl.BlockSpec((d, ff), lambda i, j, k: (0, 0), pipeline_mode=pl.Buffered(1))
    dh1, dh1_bf, dg_ffn = _mm("d_ffn_gate_up", (t // te, 1, 1),
                              [("nt", dfg_bf, hid_all, W_fg, w_all), ("nt", dfu_bf, hid_all, W_fu, w_all)],
                              [(td_f32, row_e), (td_bf, row_e), (SDS((1, d), F32), vec_d)],
                              extras=[(h1, row_e), (g_ffn, vec_d), (dh2, row_e)], epilogue=_rms_bwd_epilogue, after=rs_ffn[3])

    dW_out = wgrad("dw_out", m_bf, d, dh1_bf, d)
    glu_once = pl.BlockSpec((SSM_W, d), lambda i, j, k: (0, 0), pipeline_mode=pl.Buffered(1))
    row_es = _bs((te, SSM_W), lambda i, j, k: (i, 0))
    ts_f32 = SDS((t, SSM_W), F32)
    dz_g, dad_bf, dya_bf, dyb_bf, d_yg, d_attn = _mm(
        "d_out_proj", (t // te, 1, 1), [("nt", dh1_bf, row_e, W_out, w_once)],
        [(SDS((t, 2 * d), BF16), _bs((te, 2 * d), lambda i, j, k: (i, 0))), (td_bf, row_e), (td_bf, row_e), (td_bf, row_e),
         (ts_f32, row_es), (ts_f32, row_es)],
        extras=[(zg, gate_a), (zg, gate_s), (attn_d, row_e), (ya, row_e), (yb, row_e), (W_ga, glu_once), (W_gb, glu_once),
                (W_ap, glu_once)], epilogue=_merge_bwd_epilogue)

    dW_ga = wgrad("dw_glu_a", yg_bf, SSM_W, dya_bf, d)
    dW_gb = wgrad("dw_glu_b", yg_bf, SSM_W, dyb_bf, d)
    dys, dys_bf, dd_skip = _ssm_out_bwd(d_yg, ys, u_perm, te)
    start_b = _ssm_carries("ssm_carries_bwd", dys_bf, cd, "nt", lamc8, pwc, True)
    dz_u, dlam8, dbd, dcd = _ssm_bwd(dys_bf, dys, dsk, u_bf, h_all, bd, cd, lamc8, start_b)
    dlam = jnp.sum(dlam8, axis=0)
    dbb = _diag_blocks(dbd.reshape(2, SSM_NB, BLK, 512), SSM_GROUP, SSM_STATE)
    dbb_re = jnp.swapaxes(dbb[0], 1, 2).reshape(gp, SSM_GROUP)
    dbb_im = jnp.swapaxes(dbb[1], 1, 2).reshape(gp, SSM_GROUP)
    dcc = _diag_blocks(dcd.reshape(2, SSM_NB, 512, BLK), SSM_STATE, SSM_GROUP)
    dc_re, dc_im = jnp.swapaxes(dcc[0], 1, 2), -jnp.swapaxes(dcc[1], 1, 2)
    db_re, db_im, dz_re, dz_im = _ssm_scale_b_bwd(z_re.reshape(gp, 1), z_im.reshape(gp, 1), b_re2, b_im2, dbb_re, dbb_im)
    gshape = (SSM_GROUPS, SSM_STATE)
    da_re, da_im, dlog_dt = _ssm_disc_bwd(a_re[0], a_im[0], log_dt.reshape(SSM_GROUPS, 1), dlam[:gp].reshape(gshape),
                                          dlam[gp:].reshape(gshape), dz_re.reshape(gshape), dz_im.reshape(gshape))

    dW_ap = wgrad("dw_attn_proj", attn_bf, GROUP_W, dad_bf, d)
    pre = _attn_bwd_pre(d_attn, attn, te)
    das, deltas = pre[:N_GROUPS], pre[N_GROUPS:]
    dqkvs = [_attn_bwd(qkv[g], das[g], lts[g], deltas[g], dil, min(512, t // dil)) for g, dil in enumerate(DILATIONS)]
    dz_qkv = _undilate_rope_bwd(dqkvs, pos, invf, tm)

    dW_in, = _mm("dw_in_qkv", (1, 3, nkt), [("tn", n1, tok_a(d), dz_qkv, _bs((tk, QK_W), lambda i, j, k: (k, j)))],
                 [(SDS((d, inw), BF16), _bs((d, QK_W), lambda i, j, k: (0, j)))])
    dW_in, = _mm("dw_in_u", (1, 1, nkt), [("tn", n1, tok_a(d), dz_u, tok_a(SSM_W))],
                 [(SDS((d, inw), BF16), _bs((d, SSM_W), lambda i, j, k: (0, ucol)))], alias_to_out0=dW_in)
    dW_in, = _mm("dw_in_gates", (1, 2, nkt), [("tn", n1, tok_a(d), dz_g, _bs((tk, d), lambda i, j, k: (k, j)))],
                 [(SDS((d, inw), BF16), _bs((d, d), lambda i, j, k: (0, gcol + j)))], alias_to_out0=dW_in)
    rest_names = ("w_in", "w_attn_proj", "w_glu_a", "w_glu_b", "w_out")
    rs_in = _exchange_start("scatter_rest_start", [dW_in, dW_ap, dW_ga, dW_gb, dW_out], *group(rest_names), False)
    w_piece = lambda w, cb: pl.BlockSpec((d, w), lambda i, j, k: (0, cb), pipeline_mode=pl.Buffered(1))
    dx, dg_mix = _mm(
        "d_z_proj", (t // te, 1, 1),
        [("nt", dz_qkv, _bs((te, 3 * QK_W), lambda i, j, k: (i, 0)), W_in, w_piece(3 * QK_W, 0)),
         ("nt", dz_u, _bs((te, SSM_W), lambda i, j, k: (i, 0)), W_in, w_piece(SSM_W, ucol)),
         ("nt", dz_g, _bs((te, d), lambda i, j, k: (i, 0)), W_in, w_piece(d, gcol)),
         ("nt", dz_g, _bs((te, d), lambda i, j, k: (i, 1)), W_in, w_piece(d, gcol + 1))],
        [(td_f32, row_e), (SDS((1, d), F32), vec_d)],
        extras=[(x2, row_e), (g_mix, vec_d), (dh1, row_e)], epilogue=_rms_bwd_epilogue, after=rs_in[3])

    small_parts = dict(g_mix=dg_mix, a_re=da_re, a_im=da_im, log_dt=dlog_dt, b_re=db_re, b_im=db_im, c_re=dc_re, c_im=dc_im,
                       d_skip=dd_skip, g_ffn=dg_ffn, g_final=dg_final)
    small = _pack_small([small_parts[n] for n in _SMALL])
    received = {}
    for names, started, label in ((ffn_names, rs_ffn, "ffn"), (rest_names, rs_in, "rest")):
        landed = _exchange_wait(f"scatter_{label}_wait", started, list(range(len(names))), *group(names), False, dx)
        received.update(zip(names, landed))

    new = {}
    for n in wnames:
        new[n] = [o.reshape(args[n].shape)
                  for o in _adamw("adamw_" + n, received[n], args[n][0], args["m_" + n][0], args["v_" + n][0])]
    pk = lambda pre: _pack_small([args[pre + n] for n in _SMALL])
    sm = _adamw("adamw_small", _gather_small(small), pk(""), pk("m_"), pk("v_"))
    shapes = [args[n].shape for n in _SMALL]
    for n, vals in zip(_SMALL, zip(*[_unpack_small(o, shapes) for o in sm])):
        new[n] = list(vals)

    order = ("g_mix", "w_in", "a_re", "a_im", "log_dt", "b_re", "b_im", "c_re", "c_im", "d_skip", "w_attn_proj", "w_glu_a",
             "w_glu_b", "w_out", "g_ffn", "w_ffn_gate", "w_ffn_up", "w_ffn_down", "w_ple_gate", "w_ple_proj", "g_final")
    return (loss, dx.reshape(x.shape), *[new[n][0] for n in order], *[new[n][1] for n in order],
            *[new[n][2] for n in order], *[new[n][3] for n in order])
```

```python
import functools
import math

import jax
import jax.numpy as jnp
from jax import lax
from jax.experimental import pallas as pl
from jax.experimental.pallas import tpu as pltpu

F32 = jnp.float32
BF16 = jnp.bfloat16
SDS = jax.ShapeDtypeStruct

N_DEV = 8
HEAD_DIM = 128
HEADS_PER_GROUP = 4
GROUP_W = HEADS_PER_GROUP * HEAD_DIM
DILATIONS = (1, 4, 16)
N_GROUPS = len(DILATIONS)
QK_W = N_GROUPS * GROUP_W
BLK = 128
ROPE_THETA = 500000.0
ROPE_DIM = HEAD_DIM // 4
ROPE_HALF = ROPE_DIM // 2
SSM_W = 512
SSM_GROUP = 16
SSM_GROUPS = SSM_W // SSM_GROUP
SSM_STATE = 64
NSTATE = SSM_GROUPS * SSM_STATE
SSM_NB = 4
EPS = 1e-6
ADAM_LR, ADAM_B1, ADAM_B2, ADAM_EPS, ADAM_WD, ADAM_STEP = 0.001, 0.9, 0.999, 1e-08, 0.01, 10
NEG = -1e30

VMEM_LIMIT = 52 * 1024 * 1024
SCAN_ROWS = 512
SCAN_LANES = 512


def _cp(n):
    return pltpu.CompilerParams(dimension_semantics=("arbitrary",) * n, vmem_limit_bytes=VMEM_LIMIT)


def _sigmoid(x):
    return 0.5 * jnp.tanh(0.5 * x) + 0.5


_DNUMS = {"nn": (((1,), (0,)), ((), ())), "nt": (((1,), (1,)), ((), ())), "tn": (((0,), (0,)), ((), ()))}


def _bs(shape, fn):
    return pl.BlockSpec(shape, fn)


def _store_all(prods, extra_refs, out_refs, scratch_refs):
    r = prods[0]
    for p in prods[1:]:
        r = r + p
    for e in extra_refs:
        r = r + e[...]
    for o in out_refs:
        o[...] = r.astype(o.dtype)


def _mm(name, grid, pairs, outs, extras=(), epilogue=_store_all, scratch=(), alias_to_out0=None, after=None):
    nk = grid[2]
    npair = len(pairs)
    steps = [p[5] if len(p) > 5 else nk for p in pairs]

    def block(spec):
        return tuple(s for s in spec.block_shape if s is not None)

    def rows2d(shape):
        return (math.prod(shape[:-1]), shape[-1]) if len(shape) == 3 else shape

    acc_shapes = [jax.eval_shape(lambda u, v, dn=_DNUMS[p[0]]: lax.dot_general(u, v, dn, preferred_element_type=F32),
                                 SDS(rows2d(block(p[2])), BF16), SDS(block(p[4]), BF16)).shape for p in pairs]
    if nk == 1:
        acc_shapes = []
    n_in = 2 * npair + len(extras) + (alias_to_out0 is not None) + (after is not None)

    def body(*refs):
        extra_refs = refs[2 * npair:2 * npair + len(extras)]
        out_refs = refs[n_in:n_in + len(outs)]
        rest = refs[n_in + len(outs):]
        acc_refs = rest[:len(acc_shapes)]
        scratch_refs = rest[len(acc_refs):]
        k = pl.program_id(2)

        def product(i):
            a = refs[2 * i][...]
            if a.ndim == 3:
                a = a.reshape(-1, a.shape[-1])
            return lax.dot_general(a.astype(BF16), refs[2 * i + 1][...].astype(BF16), _DNUMS[pairs[i][0]],
                                   preferred_element_type=F32)

        if nk == 1:
            epilogue([product(i) for i in range(npair)], extra_refs, out_refs, scratch_refs)
            return
        for i in range(npair):
            @pl.when(k == 0)
            def _(i=i):
                acc_refs[i][...] = product(i)

            @pl.when((k > 0) & (k < steps[i]))
            def _(i=i):
                acc_refs[i][...] += product(i)

        @pl.when(k == nk - 1)
        def _():
            epilogue([a[...] for a in acc_refs], extra_refs, out_refs, scratch_refs)

    ins, in_specs = [], []
    for p in pairs:
        ins += [p[1], p[3]]
        in_specs += [p[2], p[4]]
    ins += [e[0] for e in extras]
    in_specs += [e[1] for e in extras]
    aliases = {}
    if alias_to_out0 is not None:
        aliases = {len(ins): 0}
        ins.append(alias_to_out0)
        in_specs.append(pl.BlockSpec(memory_space=pl.ANY))
    if after is not None:
        ins.append(after)
        in_specs.append(pl.BlockSpec(memory_space=pl.ANY))
    scratch_shapes = [pltpu.VMEM(s, F32) for s in acc_shapes] + list(scratch)
    return pl.pallas_call(body, grid=grid, in_specs=in_specs, out_specs=[o[1] for o in outs], out_shape=[o[0] for o in outs],
                          scratch_shapes=scratch_shapes, input_output_aliases=aliases, compiler_params=_cp(3), name=name)(*ins)


def _my_index():
    return 4 * lax.axis_index("x") + 2 * lax.axis_index("y") + lax.axis_index("c")


def _peer(d):
    mx, my, mc = lax.axis_index("x"), lax.axis_index("y"), lax.axis_index("c")
    return (mx ^ ((d >> 2) & 1), my ^ ((d >> 1) & 1), mc ^ (d & 1))


def _win(ref, kind, j, n):
    if kind == "all":
        return ref
    if kind == "slot":
        return ref.at[j]
    if kind == "rows":
        return ref.at[pl.ds(pl.multiple_of(j * n, 8), n)]
    return ref.at[:, pl.ds(pl.multiple_of(j * n, 128), n)]


def _win7(ref, kind, n):
    if kind == "slot":
        return ref.at[pl.ds(0, 7)]
    if kind == "rows":
        return ref.at[pl.ds(0, 7 * n)]
    return ref.at[:, pl.ds(0, 7 * n)]


def _full_shape(shard_shape, kind):
    if kind == "slot":
        return (N_DEV,) + tuple(shard_shape)
    if kind == "rows":
        return (N_DEV * shard_shape[0],) + tuple(shard_shape[1:])
    return (shard_shape[0], N_DEV * shard_shape[1])


def _shard_shape(full_shape, kind, n):
    if kind == "all":
        return tuple(full_shape)
    if kind == "slot":
        return tuple(full_shape[1:])
    if kind == "rows":
        return (n,) + tuple(full_shape[1:])
    return (full_shape[0], n)


_HBM = pl.BlockSpec(memory_space=pltpu.HBM)
_SEM = pl.BlockSpec(memory_space=pltpu.SEMAPHORE)
_DATAFLOW = pltpu.SideEffectType.DATAFLOW_SIDE_EFFECTING


def _exchange_start(name, srcs, kinds, sizes, gather):
    n = len(srcs)
    if gather:
        lands = [lax.empty(_full_shape(s.shape, k), s.dtype) for s, k in zip(srcs, kinds)]
    else:
        lands = [lax.empty((N_DEV,) + _shard_shape(s.shape, k, z), s.dtype) for s, k, z in zip(srcs, kinds, sizes)]

    def body(*refs):
        src, land = refs[:n], refs[n:2 * n]
        send_sems, recv_sems, local_sems = refs[2 * n], refs[2 * n + 1], refs[2 * n + 2]
        token = refs[4 * n + 3]
        me = _my_index()
        for a in range(n):
            _local_copy(src[a], land[a], kinds[a], sizes[a], gather, me, local_sems.at[a]).start()
        for a in range(n):
            for d in range(1, N_DEV):
                px, py, pc = _peer(d)
                if gather:
                    s_ref, d_ref = src[a], _win(land[a], kinds[a], me, sizes[a])
                else:
                    s_ref, d_ref = _win(src[a], kinds[a], 4 * px + 2 * py + pc, sizes[a]), land[a].at[me]
                pltpu.make_async_remote_copy(src_ref=s_ref, dst_ref=d_ref, send_sem=send_sems.at[a], recv_sem=recv_sems.at[a],
                                             device_id=(px, py, pc), device_id_type=pl.DeviceIdType.MESH).start()
        token[...] = jnp.zeros_like(token)

    hbm = [pltpu.with_memory_space_constraint(a, pltpu.HBM) for a in list(srcs) + lands]
    out = pl.pallas_call(
        body, name=name, in_specs=[_HBM] * (2 * n),
        out_shape=[pltpu.SemaphoreType.DMA((n,))] * 3 + [pltpu.HBM(a.shape, a.dtype) for a in hbm] + [SDS((8, 128), F32)],
        out_specs=[_SEM] * 3 + [_HBM] * (2 * n) + [pl.BlockSpec(memory_space=pltpu.VMEM)],
        input_output_aliases={i: 3 + i for i in range(2 * n)},
        compiler_params=pltpu.CompilerParams(has_side_effects=_DATAFLOW))(*hbm)
    return out[0:3], out[3:3 + n], out[3 + n:3 + 2 * n], out[-1]


def _local_copy(src, land, kind, size, gather, me, sem):
    if gather:
        return pltpu.make_async_copy(src, _win(land, kind, me, size), sem)
    return pltpu.make_async_copy(_win(src, kind, me, size), land.at[me], sem)


def _exchange_wait(name, started, which, kinds, sizes, gather, after):
    sems, srcs, lands, _ = started
    n = len(which)

    def body(*refs):
        src, land = refs[:n], refs[n:2 * n]
        send_ref, recv_ref, local_ref = refs[2 * n:2 * n + 3]
        me = _my_index()
        my_id = (lax.axis_index("x"), lax.axis_index("y"), lax.axis_index("c"))
        for i, a in enumerate(which):
            seven = _win7(land[i], kinds[a], sizes[a]) if gather else land[i].at[pl.ds(0, 7)]
            pltpu.make_async_remote_copy(src_ref=seven, dst_ref=seven, send_sem=send_ref.at[a], recv_sem=recv_ref.at[a],
                                         device_id=my_id, device_id_type=pl.DeviceIdType.MESH).wait()
            _local_copy(src[i], land[i], kinds[a], sizes[a], gather, me, local_ref.at[a]).wait()

    hbm = [srcs[a] for a in which] + [lands[a] for a in which]
    out = pl.pallas_call(
        body, name=name, in_specs=[_HBM] * (2 * n) + [_SEM] * 3 + [pl.BlockSpec(memory_space=pl.ANY)],
        out_shape=[pltpu.HBM(a.shape, a.dtype) for a in hbm], out_specs=[_HBM] * (2 * n),
        input_output_aliases={i: i for i in range(2 * n)},
        compiler_params=pltpu.CompilerParams(has_side_effects=_DATAFLOW))(*hbm, *sems, after)
    return out[n:]


def _gather_small(small):
    def body(in_ref, out_ref, send_sem, recv_sem, local_sem):
        me = _my_index()
        my_id = (lax.axis_index("x"), lax.axis_index("y"), lax.axis_index("c"))
        cp = pltpu.make_async_copy(in_ref, out_ref.at[me], local_sem)
        cp.start()
        for d in range(1, N_DEV):
            pltpu.make_async_remote_copy(src_ref=in_ref, dst_ref=out_ref.at[me], send_sem=send_sem, recv_sem=recv_sem,
                                         device_id=_peer(d), device_id_type=pl.DeviceIdType.MESH).start()
        seven = out_ref.at[pl.ds(0, 7)]
        pltpu.make_async_remote_copy(src_ref=seven, dst_ref=seven, send_sem=send_sem, recv_sem=recv_sem, device_id=my_id,
                                     device_id_type=pl.DeviceIdType.MESH).wait()
        cp.wait()

    any_spec = pl.BlockSpec(memory_space=pl.ANY)
    return pl.pallas_call(body, in_specs=[any_spec], out_specs=any_spec, out_shape=SDS((N_DEV,) + small.shape, F32),
                          scratch_shapes=[pltpu.SemaphoreType.DMA] * 3, name="gather_small")(small)


def _adamw(name, recv, w, m, v):
    rows, cols = w.shape
    tr = max(c for c in range(16, 257, 16) if rows % c == 0) if rows % 16 == 0 else rows

    def body(r_ref, w_ref, m_ref, v_ref, g_ref, d_ref, nm_ref, nv_ref):
        g = r_ref[0].astype(F32)
        for s in range(1, N_DEV):
            g = g + r_ref[s].astype(F32)
        nm = ADAM_B1 * m_ref[...] + (1.0 - ADAM_B1) * g
        nv = ADAM_B2 * v_ref[...] + (1.0 - ADAM_B2) * (g * g)
        m_hat = nm / (1.0 - ADAM_B1 ** ADAM_STEP)
        v_hat = nv / (1.0 - ADAM_B2 ** ADAM_STEP)
        g_ref[...] = g
        d_ref[...] = -ADAM_LR * (m_hat / (jnp.sqrt(v_hat) + ADAM_EPS) + ADAM_WD * w_ref[...])
        nm_ref[...] = nm
        nv_ref[...] = nv

    blk = _bs((tr, cols), lambda i: (i, 0))
    return pl.pallas_call(
        body, grid=(rows // tr,), in_specs=[_bs((N_DEV, tr, cols), lambda i: (0, i, 0)), blk, blk, blk],
        out_specs=[blk] * 4, out_shape=[SDS((rows, cols), F32)] * 4, compiler_params=_cp(1), name=name)(recv, w, m, v)


def _rms_fwd(name, x, g, tm):
    t, d = x.shape

    def body(x_ref, g_ref, n_ref):
        xv = x_ref[...]
        r = lax.rsqrt(jnp.mean(xv * xv, axis=-1, keepdims=True) + EPS)
        n_ref[...] = (xv * r * g_ref[...]).astype(BF16)

    return pl.pallas_call(body, grid=(t // tm,), in_specs=[_bs((tm, d), lambda i: (i, 0)), _bs((1, d), lambda i: (0, 0))],
                          out_specs=_bs((tm, d), lambda i: (i, 0)), out_shape=SDS((t, d), BF16), compiler_params=_cp(1),
                          name=name)(x, g)


def _accumulate_rows(ref, part):
    @pl.when(pl.program_id(0) == 0)
    def _():
        ref[...] = part

    @pl.when(pl.program_id(0) > 0)
    def _():
        ref[...] += part


def _rms_bwd_epilogue(prods, extra_refs, out_refs, scratch_refs):
    dyv = prods[0]
    for p in prods[1:]:
        dyv = dyv + p
    if len(extra_refs) > 3:
        dyv = dyv + extra_refs[3][...]
    xv = extra_refs[0][...]
    r = lax.rsqrt(jnp.mean(xv * xv, axis=-1, keepdims=True) + EPS)
    xh = xv * r
    dxh = dyv * extra_refs[1][...]
    dx = extra_refs[2][...] + r * (dxh - xh * jnp.mean(dxh * xh, axis=-1, keepdims=True))
    for o in out_refs[:-1]:
        o[...] = dx.astype(o.dtype)
    _accumulate_rows(out_refs[-1], jnp.sum(dyv * xh, axis=0, keepdims=True))


def _out_norm_epilogue(prods, extra_refs, out_refs, scratch_refs):
    h = prods[0] + extra_refs[0][...]
    r = lax.rsqrt(jnp.mean(h * h, axis=-1, keepdims=True) + EPS)
    out_refs[0][...] = h
    out_refs[1][...] = (h * r * extra_refs[1][...]).astype(BF16)


def _glu_merge_epilogue(prods, extra_refs, out_refs, scratch_refs):
    ya, yb, ad = prods
    ga, gs = extra_refs[0][...].astype(F32), extra_refs[1][...].astype(F32)
    m = _sigmoid(ga) * ad + _sigmoid(gs) * (ya * _sigmoid(yb))
    out_refs[0][...] = m.astype(BF16)
    for o, val in zip(out_refs[1:], (ya, yb, ad)):
        o[...] = val.astype(o.dtype)


def _merge_bwd_epilogue(prods, extra_refs, out_refs, scratch_refs):
    dmv = prods[0]
    d = dmv.shape[1]
    ga, gs = _sigmoid(extra_refs[0][...].astype(F32)), _sigmoid(extra_refs[1][...].astype(F32))
    adv, yav = extra_refs[2][...].astype(F32), extra_refs[3][...].astype(F32)
    sb = _sigmoid(extra_refs[4][...].astype(F32))
    out_refs[0][:, 0:d] = (dmv * adv * ga * (1.0 - ga)).astype(BF16)
    out_refs[0][:, d:2 * d] = (dmv * (yav * sb) * gs * (1.0 - gs)).astype(BF16)
    dad = (dmv * ga).astype(BF16)
    dsd = dmv * gs
    dya = (dsd * sb).astype(BF16)
    dyb = (dsd * yav * sb * (1.0 - sb)).astype(BF16)
    out_refs[1][...], out_refs[2][...], out_refs[3][...] = dad, dya, dyb
    nt = _DNUMS["nt"]
    out_refs[4][...] = (lax.dot_general(dya, extra_refs[5][...], nt, preferred_element_type=F32)
                        + lax.dot_general(dyb, extra_refs[6][...], nt, preferred_element_type=F32))
    out_refs[5][...] = lax.dot_general(dad, extra_refs[7][...], nt, preferred_element_type=F32)


def _swiglu_epilogue(prods, extra_refs, out_refs, scratch_refs):
    gv, uv = prods
    out_refs[0][...] = (gv * _sigmoid(gv) * uv).astype(BF16)
    out_refs[1][...] = gv.astype(out_refs[1].dtype)
    out_refs[2][...] = uv.astype(out_refs[2].dtype)


def _swiglu_bwd_epilogue(prods, extra_refs, out_refs, scratch_refs):
    dav = prods[0]
    gv, uv = extra_refs[0][...].astype(F32), extra_refs[1][...].astype(F32)
    sg = _sigmoid(gv)
    out_refs[0][...] = (dav * uv * sg * (1.0 + gv * (1.0 - sg))).astype(BF16)
    out_refs[1][...] = (dav * gv * sg).astype(BF16)


def _head_epilogue(n_tiles):
    def epilogue(prods, extra_refs, out_refs, scratch_refs):
        h2 = prods[0] + extra_refs[0][...]
        h2_bf = h2.astype(BF16)
        out_refs[6][...] = h2_bf
        pgv = jnp.dot(h2_bf, extra_refs[3][...], preferred_element_type=F32)
        ppv = prods[1]
        d = pgv.shape[1]
        lacc = scratch_refs[0]
        sg = _sigmoid(pgv)
        h3 = h2 + sg * ppv
        r = lax.rsqrt(jnp.mean(h3 * h3, axis=-1, keepdims=True) + EPS)
        xh = h3 * r
        gv = extra_refs[1][...]
        diff = xh * gv - extra_refs[2][...]
        dout = diff * (1.0 / d)
        dxh = dout * gv
        dh3 = r * (dxh - xh * jnp.mean(dxh * xh, axis=-1, keepdims=True))
        dpg = (dh3 * ppv * sg * (1.0 - sg)).astype(BF16)
        dh2 = dh3 + lax.dot_general(dpg, extra_refs[3][...], _DNUMS["nt"], preferred_element_type=F32)
        out_refs[2][...] = dh2
        out_refs[3][...] = dh2.astype(BF16)
        out_refs[4][...] = (dh3 * sg).astype(BF16)
        out_refs[5][...] = dpg
        _accumulate_rows(out_refs[1], jnp.sum(dout * xh, axis=0, keepdims=True))
        _accumulate_rows(lacc, jnp.sum(diff * diff, axis=0, keepdims=True))

        @pl.when(pl.program_id(0) == n_tiles - 1)
        def _():
            out_refs[0][...] = (0.5 / d) * jnp.sum(lacc[...], axis=-1, keepdims=True)

    return epilogue


def _strided(r, n, d):
    return pl.ds(r, n, stride=d) if d > 1 else pl.ds(0, n)


def _rope_tables(pos_ref, invf_ref, c_s, s1_s, s2_s):
    ang = pos_ref[...].astype(F32) * invf_ref[...]
    lane = lax.broadcasted_iota(jnp.int32, ang.shape, 1)
    sn = jnp.sin(ang)
    c_s[...] = jnp.where(lane < ROPE_DIM, jnp.cos(ang), 1.0)
    s1_s[...] = jnp.where(lane < ROPE_HALF, -sn, 0.0)
    s2_s[...] = jnp.where((lane >= ROPE_HALF) & (lane < ROPE_DIM), sn, 0.0)


def _rope_dilate_epilogue(tm):
    def epilogue(prods, extra_refs, out_refs, scratch_refs):
        zv = prods[0]
        pos_ref, invf_ref = extra_refs
        c_s, s1_s, s2_s, rot = scratch_refs
        c = pl.program_id(1)

        @pl.when(c == 0)
        def _():
            _rope_tables(pos_ref, invf_ref, c_s, s1_s, s2_s)

        @pl.when(c < 2)
        def _():
            cc, s1, s2 = c_s[...], s1_s[...], s2_s[...]
            for h in range(QK_W // HEAD_DIM):
                xv = zv[:, h * HEAD_DIM:(h + 1) * HEAD_DIM]
                rot[h] = xv * cc + pltpu.roll(xv, HEAD_DIM - ROPE_HALF, 1) * s1 + pltpu.roll(xv, ROPE_HALF, 1) * s2

        @pl.when(c == 2)
        def _():
            for h in range(QK_W // HEAD_DIM):
                rot[h] = zv[:, h * HEAD_DIM:(h + 1) * HEAD_DIM]

        for g, (d, o_ref) in enumerate(zip(DILATIONS, out_refs)):
            n = tm // d
            for r in range(d):
                for hh in range(HEADS_PER_GROUP):
                    oc = r * GROUP_W + hh * HEAD_DIM
                    o_ref[:, oc:oc + HEAD_DIM] = rot[g * HEADS_PER_GROUP + hh, _strided(r, n, d), :].astype(BF16)

    return epilogue


def _band_masks(first_tile):
    qi = lax.broadcasted_iota(jnp.int32, (BLK, 2 * BLK), 0)
    kj = lax.broadcasted_iota(jnp.int32, (BLK, 2 * BLK), 1)
    band = (kj >= qi) & (kj <= qi + BLK)
    return band, band & ((kj >= BLK) | jnp.logical_not(first_tile))


def _attn_fwd(qkv, d, qt):
    ell = qkv.shape[1]
    nsub = qt // BLK
    scale = 1.0 / math.sqrt(HEAD_DIM)

    def body(q_ref, kc_ref, kp_ref, vc_ref, vp_ref, o_ref, lse_ref, kcat, vcat):
        nb = pl.program_id(1)
        kcat[0:BLK, :] = kp_ref[...]
        kcat[BLK:, :] = kc_ref[...]
        vcat[0:BLK, :] = vp_ref[...]
        vcat[BLK:, :] = vc_ref[...]
        lane = lax.broadcasted_iota(jnp.int32, (BLK, HEAD_DIM), 1)
        band, band_first = _band_masks(nb == 0)
        for b in range(nsub):
            valid = band_first if b == 0 else band
            lse_t = jnp.zeros((BLK, HEAD_DIM), F32)
            for hh in range(HEADS_PER_GROUP):
                cs = slice(hh * HEAD_DIM, (hh + 1) * HEAD_DIM)
                qb = q_ref[b * BLK:(b + 1) * BLK, cs]
                kk = kcat[b * BLK:(b + 2) * BLK, cs]
                vv = vcat[b * BLK:(b + 2) * BLK, cs]
                s = lax.dot_general(qb, kk, _DNUMS["nt"], preferred_element_type=F32) * scale
                s = jnp.where(valid, s, NEG)
                mx = jnp.max(s, axis=-1, keepdims=True)
                p = jnp.exp(s - mx)
                den = jnp.sum(p, axis=-1, keepdims=True)
                o = jnp.dot(p.astype(BF16), vv, preferred_element_type=F32) / den
                o_ref[b * BLK:(b + 1) * BLK, cs] = o
                lse_t = jnp.where(lane == hh, mx + jnp.log(den), lse_t)
            lse_ref[b * BLK:(b + 1) * BLK, :] = lse_t

    cur = lambda c: _bs((None, qt, GROUP_W), lambda r, nb: (c, nb, r))
    prev = lambda c: _bs((None, BLK, GROUP_W), lambda r, nb: (c, jnp.maximum(nb * nsub - 1, 0), r))
    return pl.pallas_call(
        body, grid=(d, ell // qt), in_specs=[cur(0), cur(1), prev(1), cur(2), prev(2)],
        out_specs=[_bs((qt, GROUP_W), lambda r, nb: (nb, r)), _bs((None, qt, HEAD_DIM), lambda r, nb: (r, nb, 0))],
        out_shape=[SDS((ell, d * GROUP_W), F32), SDS((d, ell, HEAD_DIM), F32)],
        scratch_shapes=[pltpu.VMEM((qt + BLK, GROUP_W), BF16)] * 2, compiler_params=_cp(2), name=f"attn_fwd_d{d}")(
            qkv, qkv, qkv, qkv, qkv)


def _attn_merge(outs, lses, tm):
    t = outs[0].shape[0]

    def body(o0, o1, o2, l0, l1, l2, attn_ref, attn_bf_ref, t0, t1, t2, so, sl, lt_s):
        for g, (d, o_ref, l_ref) in enumerate(zip(DILATIONS, (o0, o1, o2), (l0, l1, l2))):
            n = tm // d
            for r in range(d):
                rows = _strided(r, n, d)
                for hh in range(HEADS_PER_GROUP):
                    oc = r * GROUP_W + hh * HEAD_DIM
                    so[g * HEADS_PER_GROUP + hh, rows, :] = o_ref[:, oc:oc + HEAD_DIM]
                sl[g, rows, :] = l_ref[r]
        ls = [sl[g] for g in range(N_GROUPS)]
        mx = jnp.maximum(jnp.maximum(ls[0], ls[1]), ls[2])
        es = [jnp.exp(l - mx) for l in ls]
        den = es[0] + es[1] + es[2]
        ws = [e / den for e in es]
        lt_s[...] = mx + jnp.log(den)
        for hh in range(HEADS_PER_GROUP):
            cs = slice(hh * HEAD_DIM, (hh + 1) * HEAD_DIM)
            a = ws[0][:, hh:hh + 1] * so[hh]
            for g in range(1, N_GROUPS):
                a = a + ws[g][:, hh:hh + 1] * so[g * HEADS_PER_GROUP + hh]
            attn_ref[:, cs] = a
            attn_bf_ref[:, cs] = a.astype(BF16)
        for d, t_ref in zip(DILATIONS, (t0, t1, t2)):
            n = tm // d
            for r in range(d):
                t_ref[r] = lt_s[_strided(r, n, d), :]

    dil = lambda d: _bs((tm // d, d * GROUP_W), lambda i: (i, 0))
    lsp = lambda d: _bs((d, tm // d, HEAD_DIM), lambda i: (0, i, 0))
    row = _bs((tm, GROUP_W), lambda i: (i, 0))
    return pl.pallas_call(
        body, grid=(t // tm,),
        in_specs=[dil(d) for d in DILATIONS] + [lsp(d) for d in DILATIONS],
        out_specs=[row, row] + [lsp(d) for d in DILATIONS],
        out_shape=[SDS((t, GROUP_W), F32), SDS((t, GROUP_W), BF16)] + [SDS(l.shape, F32) for l in lses],
        scratch_shapes=[pltpu.VMEM((N_GROUPS * HEADS_PER_GROUP, tm, HEAD_DIM), F32), pltpu.VMEM((N_GROUPS, tm, HEAD_DIM), F32),
                        pltpu.VMEM((tm, HEAD_DIM), F32)],
        compiler_params=_cp(1), name="attn_merge")(*outs, *lses)


def _attn_bwd_pre(d_attn, attn, tm):
    t = attn.shape[0]

    def body(da_ref, a_ref, g0, g1, g2, e0, e1, e2, dl_s, da_s):
        lane = lax.broadcasted_iota(jnp.int32, (tm, HEAD_DIM), 1)
        dl = jnp.zeros((tm, HEAD_DIM), F32)
        for hh in range(HEADS_PER_GROUP):
            cs = slice(hh * HEAD_DIM, (hh + 1) * HEAD_DIM)
            dav = da_ref[:, cs]
            da_s[hh] = dav
            dl = jnp.where(lane == hh, jnp.sum(dav * a_ref[:, cs], axis=-1, keepdims=True), dl)
        dl_s[...] = dl
        for d, g_ref, e_ref in zip(DILATIONS, (g0, g1, g2), (e0, e1, e2)):
            n = tm // d
            for r in range(d):
                rows = _strided(r, n, d)
                for hh in range(HEADS_PER_GROUP):
                    oc = r * GROUP_W + hh * HEAD_DIM
                    g_ref[:, oc:oc + HEAD_DIM] = da_s[hh, rows, :].astype(BF16)
                e_ref[r] = dl_s[rows, :]

    row = _bs((tm, GROUP_W), lambda i: (i, 0))
    return pl.pallas_call(
        body, grid=(t // tm,), in_specs=[row, row],
        out_specs=[_bs((tm // d, d * GROUP_W), lambda i: (i, 0)) for d in DILATIONS]
        + [_bs((d, tm // d, HEAD_DIM), lambda i: (0, i, 0)) for d in DILATIONS],
        out_shape=[SDS((t // d, d * GROUP_W), BF16) for d in DILATIONS]
        + [SDS((d, t // d, HEAD_DIM), F32) for d in DILATIONS],
        scratch_shapes=[pltpu.VMEM((tm, HEAD_DIM), F32), pltpu.VMEM((HEADS_PER_GROUP, tm, HEAD_DIM), F32)],
        compiler_params=_cp(1), name="attn_bwd_pre")(d_attn, attn)


def _attn_bwd(qkv, d_a, lt, delta, d, qt):
    ell = qkv.shape[1]
    nsub = qt // BLK
    ntile = ell // qt
    nblk = ell // BLK
    scale = 1.0 / math.sqrt(HEAD_DIM)

    def body(q_ref, qn_ref, kc_ref, kp_ref, vc_ref, vp_ref, da_ref, dan_ref, lt_ref, ltn_ref, dl_ref, dln_ref, o_ref,
             kcat, vcat, dk_acc, dv_acc):
        nb = pl.program_id(1)
        kcat[0:BLK, :] = kp_ref[...]
        kcat[BLK:, :] = kc_ref[...]
        vcat[0:BLK, :] = vp_ref[...]
        vcat[BLK:, :] = vc_ref[...]
        qi = lax.broadcasted_iota(jnp.int32, (BLK, BLK), 0)
        kj = lax.broadcasted_iota(jnp.int32, (BLK, BLK), 1)
        valid_next = (kj >= qi) & (nb < ntile - 1)
        band, band_first = _band_masks(nb == 0)
        for hh in range(HEADS_PER_GROUP):
            cs = slice(hh * HEAD_DIM, (hh + 1) * HEAD_DIM)
            dk_acc[...] = jnp.zeros_like(dk_acc)
            dv_acc[...] = jnp.zeros_like(dv_acc)
            for b in range(nsub):
                rs = slice(b * BLK, (b + 1) * BLK)
                ks = slice(b * BLK, (b + 2) * BLK)
                valid = band_first if b == 0 else band
                qb, kk, vv, dab = q_ref[rs, cs], kcat[ks, cs], vcat[ks, cs], da_ref[rs, cs]
                s = lax.dot_general(qb, kk, _DNUMS["nt"], preferred_element_type=F32) * scale
                p = jnp.where(valid, jnp.exp(s - lt_ref[rs, hh:hh + 1]), 0.0)
                dp = lax.dot_general(dab, vv, _DNUMS["nt"], preferred_element_type=F32)
                ds = (p * (dp - dl_ref[rs, hh:hh + 1])).astype(BF16)
                o_ref[0, rs, cs] = jnp.dot(ds, kk, preferred_element_type=F32) * scale
                dk_acc[ks, :] += lax.dot_general(ds, qb, _DNUMS["tn"], preferred_element_type=F32) * scale
                dv_acc[ks, :] += lax.dot_general(p.astype(BF16), dab, _DNUMS["tn"], preferred_element_type=F32)
            ks = slice(nsub * BLK, (nsub + 1) * BLK)
            qn, kl, vl, dan = qn_ref[:, cs], kcat[ks, cs], vcat[ks, cs], dan_ref[:, cs]
            s = lax.dot_general(qn, kl, _DNUMS["nt"], preferred_element_type=F32) * scale
            p = jnp.where(valid_next, jnp.exp(s - ltn_ref[:, hh:hh + 1]), 0.0)
            dp = lax.dot_general(dan, vl, _DNUMS["nt"], preferred_element_type=F32)
            ds = (p * (dp - dln_ref[:, hh:hh + 1])).astype(BF16)
            dk_acc[ks, :] += lax.dot_general(ds, qn, _DNUMS["tn"], preferred_element_type=F32) * scale
            dv_acc[ks, :] += lax.dot_general(p.astype(BF16), dan, _DNUMS["tn"], preferred_element_type=F32)
            o_ref[1, :, cs] = dk_acc[BLK:, :]
            o_ref[2, :, cs] = dv_acc[BLK:, :]

    nxt = lambda nb: jnp.minimum((nb + 1) * nsub, nblk - 1)
    prv = lambda nb: jnp.maximum(nb * nsub - 1, 0)
    cur3 = lambda c: _bs((None, qt, GROUP_W), lambda r, nb: (c, nb, r))
    in_specs = [
        cur3(0), _bs((None, BLK, GROUP_W), lambda r, nb: (0, nxt(nb), r)),
        cur3(1), _bs((None, BLK, GROUP_W), lambda r, nb: (1, prv(nb), r)),
        cur3(2), _bs((None, BLK, GROUP_W), lambda r, nb: (2, prv(nb), r)),
        _bs((qt, GROUP_W), lambda r, nb: (nb, r)), _bs((BLK, GROUP_W), lambda r, nb: (nxt(nb), r)),
        _bs((None, qt, HEAD_DIM), lambda r, nb: (r, nb, 0)), _bs((None, BLK, HEAD_DIM), lambda r, nb: (r, nxt(nb), 0)),
        _bs((None, qt, HEAD_DIM), lambda r, nb: (r, nb, 0)), _bs((None, BLK, HEAD_DIM), lambda r, nb: (r, nxt(nb), 0)),
    ]
    return pl.pallas_call(
        body, grid=(d, ntile), in_specs=in_specs, out_specs=_bs((3, qt, GROUP_W), lambda r, nb: (0, nb, r)),
        out_shape=SDS((3, ell, d * GROUP_W), F32),
        scratch_shapes=[pltpu.VMEM((qt + BLK, GROUP_W), BF16)] * 2 + [pltpu.VMEM((qt + BLK, HEAD_DIM), F32)] * 2,
        compiler_params=_cp(2), name=f"attn_bwd_d{d}")(qkv, qkv, qkv, qkv, qkv, qkv, d_a, d_a, lt, lt, delta, delta)


def _undilate_rope_bwd(dqkvs, pos, invf, tm):
    t = pos.shape[0]

    def body(g0, g1, g2, pos_ref, invf_ref, o_ref, c_s, s1_s, s2_s, nat):
        c = pl.program_id(1)

        @pl.when(c == 0)
        def _():
            _rope_tables(pos_ref, invf_ref, c_s, s1_s, s2_s)

        for g, (d, g_ref) in enumerate(zip(DILATIONS, (g0, g1, g2))):
            n = tm // d
            for r in range(d):
                for hh in range(HEADS_PER_GROUP):
                    oc = r * GROUP_W + hh * HEAD_DIM
                    nat[g * HEADS_PER_GROUP + hh, _strided(r, n, d), :] = g_ref[:, oc:oc + HEAD_DIM]

        @pl.when(c < 2)
        def _():
            cc, s1, s2 = c_s[...], s1_s[...], s2_s[...]
            for h in range(QK_W // HEAD_DIM):
                xv = nat[h]
                y = xv * cc - pltpu.roll(xv, HEAD_DIM - ROPE_HALF, 1) * s1 - pltpu.roll(xv, ROPE_HALF, 1) * s2
                o_ref[:, h * HEAD_DIM:(h + 1) * HEAD_DIM] = y.astype(BF16)

        @pl.when(c == 2)
        def _():
            for h in range(QK_W // HEAD_DIM):
                o_ref[:, h * HEAD_DIM:(h + 1) * HEAD_DIM] = nat[h].astype(BF16)

    return pl.pallas_call(
        body, grid=(t // tm, 3),
        in_specs=[_bs((None, tm // d, d * GROUP_W), lambda i, c: (c, i, 0)) for d in DILATIONS]
        + [_bs((tm, 1), lambda i, c: (i, 0)), _bs((1, HEAD_DIM), lambda i, c: (0, 0))],
        out_specs=_bs((tm, QK_W), lambda i, c: (i, c)), out_shape=SDS((t, 3 * QK_W), BF16),
        scratch_shapes=[pltpu.VMEM((tm, HEAD_DIM), F32)] * 3 + [pltpu.VMEM((QK_W // HEAD_DIM, tm, HEAD_DIM), F32)],
        compiler_params=_cp(2), name="undilate_rope_bwd")(*dqkvs, pos, invf)


def _cmul(ar, ai, br, bi):
    return ar * br - ai * bi, ar * bi + ai * br


def _ssm_disc(a_re, a_im, log_dt, nsq):
    def body(lr_ref, li_ref, ldt_ref, br_ref, bi_ref, zr_ref, zi_ref, pr_ref, pi_ref):
        lr, li = lr_ref[...], li_ref[...]
        dt = jnp.exp(ldt_ref[...])
        mag = jnp.exp(lr * dt)
        bar_re, bar_im = mag * jnp.cos(li * dt), mag * jnp.sin(li * dt)
        nr, ni = bar_re - 1.0, bar_im
        den = lr * lr + li * li
        br_ref[...], bi_ref[...] = bar_re, bar_im
        zr_ref[...] = (nr * lr + ni * li) / den
        zi_ref[...] = (ni * lr - nr * li) / den
        pr, pi = bar_re, bar_im
        for _ in range(nsq):
            pr, pi = _cmul(pr, pi, pr, pi)
        pr_ref[...], pi_ref[...] = pr, pi

    return pl.pallas_call(body, out_shape=[SDS(a_re.shape, F32)] * 6, name="ssm_discretise")(a_re, a_im, log_dt)


def _ssm_scale_b(z_re, z_im, b_re, b_im):
    def body(zr_ref, zi_ref, br_ref, bi_ref, or_ref, oi_ref):
        zr, zi, br, bi = zr_ref[...], zi_ref[...], br_ref[...], bi_ref[...]
        or_ref[...] = zr * br - zi * bi
        oi_ref[...] = zr * bi + zi * br

    return pl.pallas_call(body, out_shape=[SDS(b_re.shape, F32)] * 2, name="ssm_scale_b")(z_re, z_im, b_re, b_im)


def _ssm_scale_b_bwd(z_re, z_im, b_re, b_im, g_re, g_im):
    def body(zr_ref, zi_ref, br_ref, bi_ref, gr_ref, gi_ref, dbr_ref, dbi_ref, dzr_ref, dzi_ref):
        zr, zi, br, bi, gr, gi = zr_ref[...], zi_ref[...], br_ref[...], bi_ref[...], gr_ref[...], gi_ref[...]
        dbr_ref[...] = zr * gr + zi * gi
        dbi_ref[...] = zr * gi - zi * gr
        dzr_ref[...] = jnp.sum(br * gr + bi * gi, axis=-1, keepdims=True)
        dzi_ref[...] = jnp.sum(br * gi - bi * gr, axis=-1, keepdims=True)

    return pl.pallas_call(body, out_shape=[SDS(b_re.shape, F32)] * 2 + [SDS(z_re.shape, F32)] * 2,
                          name="ssm_scale_b_bwd")(z_re, z_im, b_re, b_im, g_re, g_im)


def _ssm_disc_bwd(a_re, a_im, log_dt, gb_re, gb_im, gz_re, gz_im):
    def body(lr_ref, li_ref, ldt_ref, gbr_ref, gbi_ref, gzr_ref, gzi_ref, dar_ref, dai_ref, dldt_ref):
        lr, li = lr_ref[...], li_ref[...]
        dt = jnp.exp(ldt_ref[...])
        mag = jnp.exp(lr * dt)
        bar_re, bar_im = mag * jnp.cos(li * dt), mag * jnp.sin(li * dt)
        nr, ni = bar_re - 1.0, bar_im
        den = lr * lr + li * li
        zr, zi = (nr * lr + ni * li) / den, (ni * lr - nr * li) / den
        gzr, gzi = gzr_ref[...], gzi_ref[...]
        gbr = gbr_ref[...] + (lr * gzr - li * gzi) / den
        gbi = gbi_ref[...] + (lr * gzi + li * gzr) / den
        qr, qi = (zr * lr + zi * li) / den, (zi * lr - zr * li) / den
        dar_ref[...] = dt * (bar_re * gbr + bar_im * gbi) - qr * gzr - qi * gzi
        dai_ref[...] = dt * (bar_re * gbi - bar_im * gbr) - qr * gzi + qi * gzr
        wr, wi = lr * bar_re - li * bar_im, lr * bar_im + li * bar_re
        dldt_ref[...] = dt * jnp.sum(wr * gbr + wi * gbi, axis=-1, keepdims=True)

    return pl.pallas_call(body, out_shape=[SDS(a_re.shape, F32)] * 2 + [SDS(log_dt.shape, F32)],
                          name="ssm_discretise_bwd")(a_re, a_im, log_dt, gb_re, gb_im, gz_re, gz_im)


def _interleave_epilogue(prods, extra_refs, out_refs, scratch_refs):
    uv = prods[0]
    tmp = scratch_refs[0]
    n = uv.shape[0] // N_DEV
    for b in range(SSM_W // BLK):
        cs = slice(b * BLK, (b + 1) * BLK)
        for j in range(N_DEV):
            tmp[b, pl.ds(j, n, stride=N_DEV), :] = uv[j * n:(j + 1) * n, cs]
        out_refs[0][:, cs] = tmp[b]
        out_refs[1][:, cs] = tmp[b].astype(BF16)


def _drive(src_ref, mat_ref, dst, mode):
    for kn in range(2 * SSM_NB):
        n = kn % SSM_NB
        a = src_ref[:, n * BLK:(n + 1) * BLK]
        dst[:, kn * 512:(kn + 1) * 512] = lax.dot_general(a, mat_ref[kn], _DNUMS[mode], preferred_element_type=F32)


def _scan_chunk(src, lam_ref, carry, *, reverse, store=None, h_ref=None, acc=None):
    steps = src.shape[0] // 8
    for c in range(NSTATE // SCAN_LANES):
        re = slice(c * SCAN_LANES, (c + 1) * SCAN_LANES)
        im = slice(NSTATE + c * SCAN_LANES, NSTATE + (c + 1) * SCAN_LANES)
        ar, ai = lam_ref[:, re], lam_ref[:, im]

        def step(s, val):
            i = (steps - 1 - s) if reverse else s
            rows = pl.ds(pl.multiple_of(i * 8, 8), 8)
            if acc is not None:
                hr, hi, dr, di = val
                pr, pi = h_ref[rows, re], h_ref[rows, im]
                dr = dr + hr * pr + hi * pi
                di = di + hi * pr - hr * pi
            else:
                hr, hi = val
            nr = ar * hr - ai * hi + src[rows, re]
            ni = ar * hi + ai * hr + src[rows, im]
            if store is not None:
                store[rows, re] = nr
                store[rows, im] = ni
            return (nr, ni, dr, di) if acc is not None else (nr, ni)

        init = (carry[:, re], carry[:, im])
        if acc is not None:
            init = init + (acc[:, re], acc[:, im])
        out = lax.fori_loop(0, steps, step, init, unroll=4)
        carry[:, re], carry[:, im] = out[0], out[1]
        if acc is not None:
            acc[:, re], acc[:, im] = out[2], out[3]


def _segment_carries(e_ref, pw_ref, out_ref, reverse):
    pr, pi = pw_ref[:, 0:NSTATE], pw_ref[:, NSTATE:]
    hr = jnp.zeros((1, NSTATE), F32)
    hi = jnp.zeros((1, NSTATE), F32)
    order = range(N_DEV - 1, -1, -1) if reverse else range(N_DEV)
    for j in order:
        out_ref[j:j + 1, 0:NSTATE] = hr
        out_ref[j:j + 1, NSTATE:] = hi
        tr, ti = _cmul(pr, pi, hr, hi)
        hr, hi = e_ref[j:j + 1, 0:NSTATE] + tr, e_ref[j:j + 1, NSTATE:] + ti


def _ssm_carries(name, src, mat, mode, lam8, pw, reverse):
    t = src.shape[0]
    nchunk = t // SCAN_ROWS

    def body(src_ref, mat_ref, lam_ref, pw_ref, out_ref, drive, carry):
        c = pl.program_id(0)

        @pl.when(c == 0)
        def _():
            carry[...] = jnp.zeros_like(carry)

        _drive(src_ref, mat_ref, drive, mode)
        _scan_chunk(drive, lam_ref, carry, reverse=reverse)

        @pl.when(c == nchunk - 1)
        def _():
            _segment_carries(carry, pw_ref, out_ref, reverse)

    blk = (lambda c: (nchunk - 1 - c, 0)) if reverse else (lambda c: (c, 0))
    return pl.pallas_call(
        body, grid=(nchunk,),
        in_specs=[_bs((SCAN_ROWS, SSM_W), blk), _bs(mat.shape, lambda c: (0, 0, 0)), _bs((8, 2 * NSTATE), lambda c: (0, 0)),
                  _bs((1, 2 * NSTATE), lambda c: (0, 0))],
        out_specs=_bs((8, 2 * NSTATE), lambda c: (0, 0)), out_shape=SDS((8, 2 * NSTATE), F32),
        scratch_shapes=[pltpu.VMEM((SCAN_ROWS, 2 * NSTATE), F32), pltpu.VMEM((8, 2 * NSTATE), F32)],
        compiler_params=_cp(1), name=name)(src, mat, lam8, pw)


def _ssm_fwd(u_bf, u, d_skip, bd, cd, lam8, start):
    t = u_bf.shape[0]
    nchunk = t // SCAN_ROWS
    per_seg = SCAN_ROWS // N_DEV

    def body(ub_ref, u_ref, d_ref, bd_ref, cd_ref, lam_ref, start_ref, h_ref, ys_ref, yg_ref, drive, carry, tmp):
        @pl.when(pl.program_id(0) == 0)
        def _():
            carry[...] = start_ref[...]

        _drive(ub_ref, bd_ref, drive, "nn")
        _scan_chunk(drive, lam_ref, carry, reverse=False, store=h_ref)
        for n in range(SSM_NB):
            cs = slice(n * BLK, (n + 1) * BLK)
            hr = h_ref[:, n * 512:(n + 1) * 512].astype(BF16)
            hi = h_ref[:, NSTATE + n * 512:NSTATE + (n + 1) * 512].astype(BF16)
            ys = (jnp.dot(hr, cd_ref[n], preferred_element_type=F32) + jnp.dot(hi, cd_ref[SSM_NB + n], preferred_element_type=F32)
                  + d_ref[:, cs] * u_ref[:, cs])
            ys_ref[:, cs] = ys
            tmp[n] = _gelu_parts(ys)[0]
            for j in range(N_DEV):
                yg_ref[j, :, cs] = tmp[n, pl.ds(j, per_seg, stride=N_DEV), :].astype(BF16)

    row = _bs((SCAN_ROWS, SSM_W), lambda c: (c, 0))
    h, ys, yg = pl.pallas_call(
        body, grid=(nchunk,),
        in_specs=[row, row, _bs((1, SSM_W), lambda c: (0, 0)), _bs(bd.shape, lambda c: (0, 0, 0)), _bs(cd.shape, lambda c: (0, 0, 0)),
                  _bs((8, 2 * NSTATE), lambda c: (0, 0)), _bs((8, 2 * NSTATE), lambda c: (0, 0))],
        out_specs=[_bs((SCAN_ROWS, 2 * NSTATE), lambda c: (c, 0)), row, _bs((N_DEV, per_seg, SSM_W), lambda c: (0, c, 0))],
        out_shape=[SDS((t, 2 * NSTATE), F32), SDS((t, SSM_W), F32), SDS((N_DEV, t // N_DEV, SSM_W), BF16)],
        scratch_shapes=[pltpu.VMEM((SCAN_ROWS, 2 * NSTATE), F32), pltpu.VMEM((8, 2 * NSTATE), F32),
                        pltpu.VMEM((SSM_NB, SCAN_ROWS, BLK), F32)],
        compiler_params=_cp(1), name="ssm_scan_fwd")(u_bf, u, d_skip, bd, cd, lam8, start)
    return h, ys, yg.reshape(t, SSM_W)


def _ssm_bwd(dys_bf, dys, d_skip, u_bf, h, bd, cd, lamc8, start):
    t = u_bf.shape[0]
    nchunk = t // SCAN_ROWS
    per_seg = SCAN_ROWS // N_DEV

    def body(dys_ref, dysf_ref, d_ref, u_ref, h_ref, bd_ref, cd_ref, lam_ref, start_ref, du_ref, dlam_ref, dbd_ref, dcd_ref,
             drive, adj, carry, tmp):
        c = pl.program_id(0)

        @pl.when(c == 0)
        def _():
            carry[...] = start_ref[...]
            dlam_ref[...] = jnp.zeros_like(dlam_ref)
            dbd_ref[...] = jnp.zeros_like(dbd_ref)
            dcd_ref[...] = jnp.zeros_like(dcd_ref)

        _drive(dys_ref, cd_ref, drive, "nt")
        _scan_chunk(drive, lam_ref, carry, reverse=True, store=adj, h_ref=h_ref, acc=dlam_ref)
        for n in range(SSM_NB):
            cs = slice(n * BLK, (n + 1) * BLK)
            acc = None
            for k in range(2):
                kn = k * SSM_NB + n
                ss = slice(kn * 512, (kn + 1) * 512)
                lam_b = adj[:, ss].astype(BF16)
                part = lax.dot_general(lam_b, bd_ref[kn], _DNUMS["nt"], preferred_element_type=F32)
                acc = part if acc is None else acc + part
                dbd_ref[kn] += lax.dot_general(u_ref[:, cs], lam_b, _DNUMS["tn"], preferred_element_type=F32)
                dcd_ref[kn] += lax.dot_general(h_ref[:, ss].astype(BF16), dys_ref[:, cs], _DNUMS["tn"],
                                               preferred_element_type=F32)
            tmp[n] = acc + d_ref[:, cs] * dysf_ref[:, cs]
            for j in range(N_DEV):
                du_ref[j, :, cs] = tmp[n, pl.ds(j, per_seg, stride=N_DEV), :].astype(BF16)

    rev = lambda c: (nchunk - 1 - c, 0)
    const2 = lambda c: (0, 0)
    const3 = lambda c: (0, 0, 0)
    row = _bs((SCAN_ROWS, SSM_W), rev)
    du, dlam, dbd, dcd = pl.pallas_call(
        body, grid=(nchunk,),
        in_specs=[row, row, _bs((1, SSM_W), const2), row, _bs((SCAN_ROWS, 2 * NSTATE), rev),
                  _bs(bd.shape, const3), _bs(cd.shape, const3), _bs((8, 2 * NSTATE), const2), _bs((8, 2 * NSTATE), const2)],
        out_specs=[_bs((N_DEV, per_seg, SSM_W), lambda c: (0, nchunk - 1 - c, 0)), _bs((8, 2 * NSTATE), const2),
                   _bs(bd.shape, const3), _bs(cd.shape, const3)],
        out_shape=[SDS((N_DEV, t // N_DEV, SSM_W), BF16), SDS((8, 2 * NSTATE), F32), SDS(bd.shape, F32), SDS(cd.shape, F32)],
        scratch_shapes=[pltpu.VMEM((SCAN_ROWS, 2 * NSTATE), F32), pltpu.VMEM((SCAN_ROWS, 2 * NSTATE), F32),
                        pltpu.VMEM((8, 2 * NSTATE), F32), pltpu.VMEM((SSM_NB, SCAN_ROWS, BLK), F32)],
        compiler_params=_cp(1), name="ssm_scan_bwd")(dys_bf, dys, d_skip, u_bf, h, bd, cd, lamc8, start)
    return du.reshape(t, SSM_W), dlam, dbd, dcd


def _gelu_parts(x):
    c0 = math.sqrt(2.0 / math.pi)
    inner = c0 * (x + 0.044715 * x * x * x)
    th = jnp.tanh(inner)
    val = 0.5 * x * (1.0 + th)
    grad = 0.5 * (1.0 + th) + 0.5 * x * (1.0 - th * th) * c0 * (1.0 + 3.0 * 0.044715 * x * x)
    return val, grad


def _ssm_out_bwd(d_yg, ys, u, tm):
    t = u.shape[0]
    seg = t // N_DEV

    def body(dg_ref, ys_ref, u_ref, dys_ref, dysb_ref, dd_ref, tmp):
        for n in range(SSM_W // BLK):
            for j in range(N_DEV):
                tmp[n, pl.ds(j, tm // N_DEV, stride=N_DEV), :] = dg_ref[j, :, n * BLK:(n + 1) * BLK]
        dyg = jnp.concatenate([tmp[n] for n in range(SSM_W // BLK)], axis=1)
        dys = dyg * _gelu_parts(ys_ref[...])[1]
        dys_ref[...] = dys
        dysb_ref[...] = dys.astype(BF16)
        part = jnp.sum(dys * u_ref[...], axis=0, keepdims=True)

        @pl.when(pl.program_id(0) == 0)
        def _():
            dd_ref[...] = part

        @pl.when(pl.program_id(0) > 0)
        def _():
            dd_ref[...] += part

    row = _bs((tm, SSM_W), lambda i: (i, 0))
    return pl.pallas_call(
        body, grid=(t // tm,), in_specs=[_bs((N_DEV, tm // N_DEV, SSM_W), lambda i: (0, i, 0)), row, row],
        out_specs=[row, row, _bs((1, SSM_W), lambda i: (0, 0))],
        out_shape=[SDS((t, SSM_W), F32), SDS((t, SSM_W), BF16), SDS((1, SSM_W), F32)],
        scratch_shapes=[pltpu.VMEM((SSM_W // BLK, tm, BLK), F32)], compiler_params=_cp(1), name="ssm_out_bwd")(
            d_yg.reshape(N_DEV, seg, SSM_W), ys, u)


def _block_diag(blocks):
    nb, ng, r, c = blocks.shape
    eye = jnp.eye(ng, dtype=blocks.dtype)
    return (blocks[:, :, :, None, :] * eye[None, :, None, :, None]).reshape(nb, ng * r, ng * c)


def _diag_blocks(full, r, c):
    k, nb = full.shape[:2]
    ng = full.shape[2] // r
    x = full.reshape(k, nb, ng, r, ng, c)
    eye = jnp.eye(ng, dtype=full.dtype)
    return jnp.sum(x * eye[None, None, :, None, :, None], axis=4).reshape(k, nb * ng, r, c)


_SMALL = ("a_re", "a_im", "log_dt", "b_re", "b_im", "c_re", "c_im", "d_skip", "g_ffn", "g_final")


def _pack_small(arrs):
    flat = jnp.concatenate([a.reshape(-1) for a in arrs])
    pad = (-flat.shape[0]) % (8 * 128)
    return jnp.pad(flat, (0, pad)).reshape(-1, 128)


def _unpack_small(packed, shapes):
    flat = packed.reshape(-1)
    out, off = [], 0
    for s in shapes:
        n = math.prod(s)
        out.append(flat[off:off + n].reshape(s))
        off += n
    return out


def kernel(x, p, positions, g_mix, w_in, a_re, a_im, log_dt, b_re, b_im, c_re, c_im, d_skip, w_attn_proj, w_glu_a, w_glu_b, w_out, g_ffn, w_ffn_gate, w_ffn_up, w_ffn_down, w_ple_gate, w_ple_proj, g_final, loss_target, m_g_mix, m_w_in, m_a_re, m_a_im, m_log_dt, m_b_re, m_b_im, m_c_re, m_c_im, m_d_skip, m_w_attn_proj, m_w_glu_a, m_w_glu_b, m_w_out, m_g_ffn, m_w_ffn_gate, m_w_ffn_up, m_w_ffn_down, m_w_ple_gate, m_w_ple_proj, m_g_final, v_g_mix, v_w_in, v_a_re, v_a_im, v_log_dt, v_b_re, v_b_im, v_c_re, v_c_im, v_d_skip, v_w_attn_proj, v_w_glu_a, v_w_glu_b, v_w_out, v_g_ffn, v_w_ffn_gate, v_w_ffn_up, v_w_ffn_down, v_w_ple_gate, v_w_ple_proj, v_g_final):
    args = dict(locals())
    t, d = x.shape[1], x.shape[2]
    inw = w_in.shape[2] * N_DEV
    fs = w_ffn_gate.shape[2]
    ff = fs * N_DEV
    ple = w_ple_proj.shape[1]
    seg = t // N_DEV
    assert inw == 3 * QK_W + SSM_W + 2 * d and t % (N_DEV * SCAN_ROWS // 8) == 0 and seg & (seg - 1) == 0
    tm = min(1024, t)
    te = min(512, t)
    tk = min(1024, t)
    ucol = (3 * QK_W) // SSM_W
    gcol = (3 * QK_W + SSM_W) // d
    assert (3 * QK_W + SSM_W) % d == 0

    x2, p2, tgt = x[0], p[0, 0], loss_target[0]
    pos = positions.reshape(t, 1)
    inv = ROPE_THETA ** (-jnp.arange(ROPE_HALF, dtype=F32) * 2.0 / ROPE_DIM)
    invf = jnp.concatenate([inv, inv, jnp.zeros((HEAD_DIM - ROPE_DIM,), F32)]).reshape(1, HEAD_DIM)

    wnames = ("w_in", "w_attn_proj", "w_glu_a", "w_glu_b", "w_out", "w_ffn_gate", "w_ffn_up", "w_ffn_down", "w_ple_gate",
              "w_ple_proj")
    kinds = ("cols", "cols", "cols", "cols", "rows", "slot", "slot", "rows", "rows", "cols")
    shards = [args[n][0].astype(BF16) for n in wnames]
    sizes = [s.shape[0] if k == "rows" else s.shape[-1] for s, k in zip(shards, kinds)]
    ag = _exchange_start("gather_weights_start", shards, kinds, sizes, True)

    row_d = _bs((tm, d), lambda i, j, k: (i, 0))
    row_e = _bs((te, d), lambda i, j, k: (i, 0))
    vec_d = _bs((1, d), lambda i, j, k: (0, 0))
    sq_w = _bs((d, d), lambda i, j, k: (0, 0))
    n1 = _rms_fwd("norm_mix", x2, g_mix + ag[3][0:1, 0:1], tm)
    W_in, = _exchange_wait("gather_w_in_wait", ag, [0], kinds, sizes, True, n1)
    qkv = _mm("qkv_proj", (t // tm, 3, 1), [("nn", n1, row_d, W_in, _bs((d, QK_W), lambda i, j, k: (0, j)))],
              [(SDS((3, t // dil, dil * GROUP_W), BF16), _bs((None, tm // dil, dil * GROUP_W), lambda i, j, k: (j, i, 0)))
               for dil in DILATIONS],
              extras=[(pos, _bs((tm, 1), lambda i, j, k: (i, 0))), (invf, _bs((1, HEAD_DIM), lambda i, j, k: (0, 0)))],
              epilogue=_rope_dilate_epilogue(tm),
              scratch=[pltpu.VMEM((tm, HEAD_DIM), F32)] * 3 + [pltpu.VMEM((QK_W // HEAD_DIM, tm, HEAD_DIM), F32)])
    row_s = _bs((tm, SSM_W), lambda i, j, k: (i, 0))
    u_perm, u_bf = _mm("u_proj", (t // tm, 1, 1),
                       [("nn", n1.reshape(N_DEV, seg, d), _bs((N_DEV, tm // N_DEV, d), lambda i, j, k: (0, i, 0)), W_in,
                         _bs((d, SSM_W), lambda i, j, k: (0, ucol)))],
                       [(SDS((t, SSM_W), F32), row_s), (SDS((t, SSM_W), BF16), row_s)], epilogue=_interleave_epilogue,
                       scratch=[pltpu.VMEM((SSM_W // BLK, tm, BLK), F32)])
    zg, = _mm("z_gates", (t // tm, 2, 1),
              [("nn", n1, row_d, W_in, _bs((d, d), lambda i, j, k: (0, gcol + j)))],
              [(SDS((t, 2 * d), BF16), _bs((tm, d), lambda i, j, k: (i, j)))])

    outs, lses = [], []
    for g, dil in enumerate(DILATIONS):
        o_g, l_g = _attn_fwd(qkv[g], dil, min(512, t // dil))
        outs.append(o_g)
        lses.append(l_g)
    merged = _attn_merge(outs, lses, te)
    attn, attn_bf, lts = merged[0], merged[1], merged[2:]

    nsq = seg.bit_length() - 1
    bar_re, bar_im, z_re, z_im, pw_re, pw_im = _ssm_disc(a_re[0], a_im[0], log_dt.reshape(SSM_GROUPS, 1), nsq)
    gp = SSM_GROUPS * SSM_STATE
    b_re2, b_im2 = b_re.reshape(gp, SSM_GROUP), b_im.reshape(gp, SSM_GROUP)
    bb_re, bb_im = _ssm_scale_b(z_re.reshape(gp, 1), z_im.reshape(gp, 1), b_re2, b_im2)

    def chunks(a, r, c):
        return a.reshape(SSM_NB, SSM_GROUPS // SSM_NB, r, c)

    bbt = lambda a: jnp.swapaxes(a.reshape(SSM_GROUPS, SSM_STATE, SSM_GROUP), 1, 2)
    bd = jnp.concatenate([_block_diag(chunks(bbt(bb_re), SSM_GROUP, SSM_STATE)),
                          _block_diag(chunks(bbt(bb_im), SSM_GROUP, SSM_STATE))]).astype(BF16)
    ct = lambda a: jnp.swapaxes(a[0], 1, 2)
    cd = jnp.concatenate([_block_diag(chunks(ct(c_re), SSM_STATE, SSM_GROUP)),
                          _block_diag(chunks(-ct(c_im), SSM_STATE, SSM_GROUP))]).astype(BF16)
    lam = jnp.concatenate([bar_re.reshape(1, gp), bar_im.reshape(1, gp)], axis=1)
    lamc = jnp.concatenate([bar_re.reshape(1, gp), -bar_im.reshape(1, gp)], axis=1)
    pw = jnp.concatenate([pw_re.reshape(1, gp), pw_im.reshape(1, gp)], axis=1)
    pwc = jnp.concatenate([pw_re.reshape(1, gp), -pw_im.reshape(1, gp)], axis=1)
    lam8, lamc8 = jnp.broadcast_to(lam, (8, 2 * gp)), jnp.broadcast_to(lamc, (8, 2 * gp))

    start_f = _ssm_carries("ssm_carries_fwd", u_bf, bd, "nn", lam8, pw, False)
    dsk = d_skip.reshape(1, SSM_W)
    h_all, ys, yg_bf = _ssm_fwd(u_bf, u_perm, dsk, bd, cd, lam8, start_f)
    W_ap, W_ga, W_gb, W_out, W_fg, W_fu, W_fd, W_pg, W_pp = _exchange_wait(
        "gather_rest_wait", ag, list(range(1, len(wnames))), kinds, sizes, True, yg_bf)
    W_fg = jnp.swapaxes(W_fg, 0, 1).reshape(d, ff)
    W_fu = jnp.swapaxes(W_fu, 0, 1).reshape(d, ff)

    glu_w = _bs((SSM_W, d), lambda i, j, k: (0, 0))
    row_s = _bs((tm, SSM_W), lambda i, j, k: (i, 0))
    gate_a = _bs((te, d), lambda i, j, k: (i, 0))
    gate_s = _bs((te, d), lambda i, j, k: (i, 1))
    td_f32, td_bf = SDS((t, d), F32), SDS((t, d), BF16)
    m_bf, ya, yb, attn_d = _mm(
        "glu_merge", (t // tm, 1, 1),
        [("nn", yg_bf, row_s, W_ga, glu_w), ("nn", yg_bf, row_s, W_gb, glu_w), ("nn", attn_bf, row_s, W_ap, glu_w)],
        [(td_bf, row_d)] * 4, extras=[(zg, row_d), (zg, _bs((tm, d), lambda i, j, k: (i, 1)))], epilogue=_glu_merge_epilogue)

    h1, n2 = _mm("out_proj", (t // tm, 1, 1), [("nn", m_bf, row_d, W_out, sq_w)], [(td_f32, row_d), (td_bf, row_d)],
                 extras=[(x2, row_d), (g_ffn, vec_d)], epilogue=_out_norm_epilogue)

    tn_f = ff // 2
    nf = ff // tn_f
    hid_o = _bs((tm, tn_f), lambda j, i, k: (i, j))
    tf_bf = SDS((t, ff), BF16)
    a_rows = _bs((tm, d), lambda j, i, k: (i, 0))
    w_cols = _bs((d, tn_f), lambda j, i, k: (0, j))
    act, fg, fu = _mm("ffn_gate_up", (nf, t // tm, 1), [("nn", n2, a_rows, W_fg, w_cols), ("nn", n2, a_rows, W_fu, w_cols)],
                      [(tf_bf, hid_o)] * 3, epilogue=_swiglu_epilogue)
    w_once = pl.BlockSpec((d, d), lambda i, j, k: (0, 0), pipeline_mode=pl.Buffered(1))
    loss_part, dg_final, dh2, dh2_bf, dpp_bf, dpg_bf, h2_bf = _mm(
        "ffn_down_head", (t // te, 1, 1),
        [("nn", act, _bs((te, ff), lambda i, j, k: (i, 0)), W_fd,
          pl.BlockSpec((ff, d), lambda i, j, k: (0, 0), pipeline_mode=pl.Buffered(1))),
         ("nn", p2, _bs((te, ple), lambda i, j, k: (i, 0)), W_pp, _bs((ple, d), lambda i, j, k: (0, 0)))],
        [(SDS((1, 1), F32), _bs((1, 1), lambda i, j, k: (0, 0))), (SDS((1, d), F32), vec_d), (td_f32, row_e), (td_bf, row_e),
         (td_bf, row_e), (td_bf, row_e), (td_bf, row_e)],
        extras=[(h1, row_e), (g_final.reshape(1, d), vec_d), (tgt, row_e), (W_pg, w_once)], epilogue=_head_epilogue(t // te),
        scratch=[pltpu.VMEM((1, d), F32)])
    loss = lax.psum(loss_part[0, 0], ("x", "y", "c"))

    nkt = t // tk
    tok_a = lambda w: _bs((tk, w), lambda i, j, k: (k, 0))

    def wgrad(name, a, wa, b, wb):
        return _mm(name, (1, 1, nkt), [("tn", a, tok_a(wa), b, tok_a(wb))],
                   [(SDS((wa, wb), BF16), _bs((wa, wb), lambda i, j, k: (0, 0)))])[0]

    dW_pp = wgrad("dw_ple_proj", p2, ple, dpp_bf, d)
    dW_pg = wgrad("dw_ple_gate", h2_bf, d, dpg_bf, d)
    dfg_bf, dfu_bf = _mm("d_ffn_down", (nf, t // tm, 1),
                         [("nt", dh2_bf, a_rows, W_fd, _bs((tn_f, d), lambda j, i, k: (j, 0)))],
                         [(tf_bf, hid_o), (tf_bf, hid_o)], extras=[(fg, hid_o), (fu, hid_o)], epilogue=_swiglu_bwd_epilogue)
    dW_fd, = _mm("dw_ffn_down", (nf, 1, nkt), [("tn", act, _bs((tk, tn_f), lambda i, j, k: (k, i)), dh2_bf, tok_a(d))],
                 [(SDS((ff, d), BF16), _bs((tn_f, d), lambda i, j, k: (i, 0)))])
    hid_t = _bs((tk, tn_f), lambda i, j, k: (k, j))
    wg_o = [(SDS((d, ff), BF16), _bs((d, tn_f), lambda i, j, k: (0, j)))]
    dW_fg, = _mm("dw_ffn_gate", (1, nf, nkt), [("tn", n2, tok_a(d), dfg_bf, hid_t)], wg_o)
    dW_fu, = _mm("dw_ffn_up", (1, nf, nkt), [("tn", n2, tok_a(d), dfu_bf, hid_t)], wg_o)
    dW_fg = jnp.swapaxes(dW_fg.reshape(d, N_DEV, fs), 0, 1)
    dW_fu = jnp.swapaxes(dW_fu.reshape(d, N_DEV, fs), 0, 1)
    group = lambda names: ([kinds[wnames.index(n)] for n in names], [sizes[wnames.index(n)] for n in names])
    ffn_names = ("w_ffn_gate", "w_ffn_up", "w_ffn_down", "w_ple_gate", "w_ple_proj")
    rs_ffn = _exchange_start("scatter_ffn_start", [dW_fg, dW_fu, dW_fd, dW_pg, dW_pp], *group(ffn_names), False)
    hid_all = _bs((te, ff), lambda i, j, k: (i, 0))
    w_all = pl.BlockSpec((d, ff), lambda i, j, k: (0, 0), pipeline_mode=pl.Buffered(1))
    dh1, dh1_bf, dg_ffn = _mm("d_ffn_gate_up", (t // te, 1, 1),
                              [("nt", dfg_bf, hid_all, W_fg, w_all), ("nt", dfu_bf, hid_all, W_fu, w_all)],
                              [(td_f32, row_e), (td_bf, row_e), (SDS((1, d), F32), vec_d)],
                              extras=[(h1, row_e), (g_ffn, vec_d), (dh2, row_e)], epilogue=_rms_bwd_epilogue, after=rs_ffn[3])

    dW_out = wgrad("dw_out", m_bf, d, dh1_bf, d)
    glu_once = pl.BlockSpec((SSM_W, d), lambda i, j, k: (0, 0), pipeline_mode=pl.Buffered(1))
    row_es = _bs((te, SSM_W), lambda i, j, k: (i, 0))
    ts_f32 = SDS((t, SSM_W), F32)
    dz_g, dad_bf, dya_bf, dyb_bf, d_yg, d_attn = _mm(
        "d_out_proj", (t // te, 1, 1), [("nt", dh1_bf, row_e, W_out, w_once)],
        [(SDS((t, 2 * d), BF16), _bs((te, 2 * d), lambda i, j, k: (i, 0))), (td_bf, row_e), (td_bf, row_e), (td_bf, row_e),
         (ts_f32, row_es), (ts_f32, row_es)],
        extras=[(zg, gate_a), (zg, gate_s), (attn_d, row_e), (ya, row_e), (yb, row_e), (W_ga, glu_once), (W_gb, glu_once),
                (W_ap, glu_once)], epilogue=_merge_bwd_epilogue)

    dW_ga = wgrad("dw_glu_a", yg_bf, SSM_W, dya_bf, d)
    dW_gb = wgrad("dw_glu_b", yg_bf, SSM_W, dyb_bf, d)
    dys, dys_bf, dd_skip = _ssm_out_bwd(d_yg, ys, u_perm, te)
    start_b = _ssm_carries("ssm_carries_bwd", dys_bf, cd, "nt", lamc8, pwc, True)
    dz_u, dlam8, dbd, dcd = _ssm_bwd(dys_bf, dys, dsk, u_bf, h_all, bd, cd, lamc8, start_b)
    dlam = jnp.sum(dlam8, axis=0)
    dbb = _diag_blocks(dbd.reshape(2, SSM_NB, BLK, 512), SSM_GROUP, SSM_STATE)
    dbb_re = jnp.swapaxes(dbb[0], 1, 2).reshape(gp, SSM_GROUP)
    dbb_im = jnp.swapaxes(dbb[1], 1, 2).reshape(gp, SSM_GROUP)
    dcc = _diag_blocks(dcd.reshape(2, SSM_NB, 512, BLK), SSM_STATE, SSM_GROUP)
    dc_re, dc_im = jnp.swapaxes(dcc[0], 1, 2), -jnp.swapaxes(dcc[1], 1, 2)
    db_re, db_im, dz_re, dz_im = _ssm_scale_b_bwd(z_re.reshape(gp, 1), z_im.reshape(gp, 1), b_re2, b_im2, dbb_re, dbb_im)
    gshape = (SSM_GROUPS, SSM_STATE)
    da_re, da_im, dlog_dt = _ssm_disc_bwd(a_re[0], a_im[0], log_dt.reshape(SSM_GROUPS, 1), dlam[:gp].reshape(gshape),
                                          dlam[gp:].reshape(gshape), dz_re.reshape(gshape), dz_im.reshape(gshape))

    dW_ap = wgrad("dw_attn_proj", attn_bf, GROUP_W, dad_bf, d)
    pre = _attn_bwd_pre(d_attn, attn, te)
    das, deltas = pre[:N_GROUPS], pre[N_GROUPS:]
    dqkvs = [_attn_bwd(qkv[g], das[g], lts[g], deltas[g], dil, min(512, t // dil)) for g, dil in enumerate(DILATIONS)]
    dz_qkv = _undilate_rope_bwd(dqkvs, pos, invf, tm)

    dW_in, = _mm("dw_in_qkv", (1, 3, nkt), [("tn", n1, tok_a(d), dz_qkv, _bs((tk, QK_W), lambda i, j, k: (k, j)))],
                 [(SDS((d, inw), BF16), _bs((d, QK_W), lambda i, j, k: (0, j)))])
    dW_in, = _mm("dw_in_u", (1, 1, nkt), [("tn", n1, tok_a(d), dz_u, tok_a(SSM_W))],
                 [(SDS((d, inw), BF16), _bs((d, SSM_W), lambda i, j, k: (0, ucol)))], alias_to_out0=dW_in)
    dW_in, = _mm("dw_in_gates", (1, 2, nkt), [("tn", n1, tok_a(d), dz_g, _bs((tk, d), lambda i, j, k: (k, j)))],
                 [(SDS((d, inw), BF16), _bs((d, d), lambda i, j, k: (0, gcol + j)))], alias_to_out0=dW_in)
    small_parts = dict(a_re=da_re, a_im=da_im, log_dt=dlog_dt, b_re=db_re, b_im=db_im, c_re=dc_re, c_im=dc_im,
                       d_skip=dd_skip, g_ffn=dg_ffn, g_final=dg_final)
    small = _pack_small([small_parts[n] for n in _SMALL])
    rest_names = ("w_in", "w_attn_proj", "w_glu_a", "w_glu_b", "w_out")
    rest_kinds, rest_sizes = group(rest_names)
    rs_in = _exchange_start("scatter_rest_start", [dW_in, dW_ap, dW_ga, dW_gb, dW_out, small], rest_kinds + ["all"],
                            rest_sizes + [0], False)
    w_piece = lambda w, cb: pl.BlockSpec((d, w), lambda i, j, k: (0, cb), pipeline_mode=pl.Buffered(1))
    dx, dg_mix = _mm(
        "d_z_proj", (t // te, 1, 1),
        [("nt", dz_qkv, _bs((te, 3 * QK_W), lambda i, j, k: (i, 0)), W_in, w_piece(3 * QK_W, 0)),
         ("nt", dz_u, _bs((te, SSM_W), lambda i, j, k: (i, 0)), W_in, w_piece(SSM_W, ucol)),
         ("nt", dz_g, _bs((te, d), lambda i, j, k: (i, 0)), W_in, w_piece(d, gcol)),
         ("nt", dz_g, _bs((te, d), lambda i, j, k: (i, 1)), W_in, w_piece(d, gcol + 1))],
        [(td_f32, row_e), (SDS((1, d), F32), vec_d)],
        extras=[(x2, row_e), (g_mix, vec_d), (dh1, row_e)], epilogue=_rms_bwd_epilogue, after=rs_in[3])

    received = dict(zip(ffn_names, _exchange_wait("scatter_ffn_wait", rs_ffn, list(range(len(ffn_names))), *group(ffn_names),
                                                  False, dx)))
    *landed, small_all = _exchange_wait("scatter_rest_wait", rs_in, list(range(len(rest_names) + 1)), rest_kinds + ["all"],
                                        rest_sizes + [0], False, dx)
    received.update(zip(rest_names, landed))

    new = {}
    for n in wnames:
        new[n] = [o.reshape(args[n].shape)
                  for o in _adamw("adamw_" + n, received[n], args[n][0], args["m_" + n][0], args["v_" + n][0])]
    g_mix_all = _gather_small(_pack_small([dg_mix]))
    pk = lambda pre: jnp.concatenate([_pack_small([args[pre + n] for n in _SMALL]), _pack_small([args[pre + "g_mix"]])])
    sm = _adamw("adamw_small", jnp.concatenate([small_all, g_mix_all], axis=1), pk(""), pk("m_"), pk("v_"))
    rows_a = small.shape[0]
    shapes = [args[n].shape for n in _SMALL]
    for n, vals in zip(_SMALL, zip(*[_unpack_small(o[:rows_a], shapes) for o in sm])):
        new[n] = list(vals)
    new["g_mix"] = [_unpack_small(o[rows_a:], [g_mix.shape])[0] for o in sm]

    order = ("g_mix", "w_in", "a_re", "a_im", "log_dt", "b_re", "b_im", "c_re", "c_im", "d_skip", "w_attn_proj", "w_glu_a",
             "w_glu_b", "w_out", "g_ffn", "w_ffn_gate", "w_ffn_up", "w_ffn_down", "w_ple_gate", "w_ple_proj", "g_final")
    return (loss, dx.reshape(x.shape), *[new[n][0] for n in order], *[new[n][1] for n in order],
            *[new[n][2] for n in order], *[new[n][3] for n in order])
```

```python
import functools
import math

import jax
import jax.numpy as jnp
from jax import lax
from jax.experimental import pallas as pl
from jax.experimental.pallas import tpu as pltpu

F32 = jnp.float32
BF16 = jnp.bfloat16
SDS = jax.ShapeDtypeStruct

N_DEV = 8
HEAD_DIM = 128
HEADS_PER_GROUP = 4
GROUP_W = HEADS_PER_GROUP * HEAD_DIM
DILATIONS = (1, 4, 16)
N_GROUPS = len(DILATIONS)
QK_W = N_GROUPS * GROUP_W
BLK = 128
ROPE_THETA = 500000.0
ROPE_DIM = HEAD_DIM // 4
ROPE_HALF = ROPE_DIM // 2
SSM_W = 512
SSM_GROUP = 16
SSM_GROUPS = SSM_W // SSM_GROUP
SSM_STATE = 64
NSTATE = SSM_GROUPS * SSM_STATE
SSM_NB = 4
EPS = 1e-6
ADAM_LR, ADAM_B1, ADAM_B2, ADAM_EPS, ADAM_WD, ADAM_STEP = 0.001, 0.9, 0.999, 1e-08, 0.01, 10
NEG = -1e30

VMEM_LIMIT = 52 * 1024 * 1024
SCAN_ROWS = 512
SCAN_LANES = 512


def _cp(n):
    return pltpu.CompilerParams(dimension_semantics=("arbitrary",) * n, vmem_limit_bytes=VMEM_LIMIT)


def _sigmoid(x):
    return 0.5 * jnp.tanh(0.5 * x) + 0.5


_DNUMS = {"nn": (((1,), (0,)), ((), ())), "nt": (((1,), (1,)), ((), ())), "tn": (((0,), (0,)), ((), ()))}


def _bs(shape, fn):
    return pl.BlockSpec(shape, fn)


def _store_all(prods, extra_refs, out_refs, scratch_refs):
    r = prods[0]
    for p in prods[1:]:
        r = r + p
    for e in extra_refs:
        r = r + e[...]
    for o in out_refs:
        o[...] = r.astype(o.dtype)


def _mm(name, grid, pairs, outs, extras=(), epilogue=_store_all, scratch=(), alias_to_out0=None, after=None):
    nk = grid[2]
    npair = len(pairs)
    steps = [p[5] if len(p) > 5 else nk for p in pairs]

    def block(spec):
        return tuple(s for s in spec.block_shape if s is not None)

    def rows2d(shape):
        return (math.prod(shape[:-1]), shape[-1]) if len(shape) == 3 else shape

    acc_shapes = [jax.eval_shape(lambda u, v, dn=_DNUMS[p[0]]: lax.dot_general(u, v, dn, preferred_element_type=F32),
                                 SDS(rows2d(block(p[2])), BF16), SDS(block(p[4]), BF16)).shape for p in pairs]
    if nk == 1:
        acc_shapes = []
    n_in = 2 * npair + len(extras) + (alias_to_out0 is not None) + (after is not None)

    def body(*refs):
        extra_refs = refs[2 * npair:2 * npair + len(extras)]
        out_refs = refs[n_in:n_in + len(outs)]
        rest = refs[n_in + len(outs):]
        acc_refs = rest[:len(acc_shapes)]
        scratch_refs = rest[len(acc_refs):]
        k = pl.program_id(2)

        def product(i):
            a = refs[2 * i][...]
            if a.ndim == 3:
                a = a.reshape(-1, a.shape[-1])
            return lax.dot_general(a.astype(BF16), refs[2 * i + 1][...].astype(BF16), _DNUMS[pairs[i][0]],
                                   preferred_element_type=F32)

        if nk == 1:
            epilogue([product(i) for i in range(npair)], extra_refs, out_refs, scratch_refs)
            return
        for i in range(npair):
            @pl.when(k == 0)
            def _(i=i):
                acc_refs[i][...] = product(i)

            @pl.when((k > 0) & (k < steps[i]))
            def _(i=i):
                acc_refs[i][...] += product(i)

        @pl.when(k == nk - 1)
        def _():
            epilogue([a[...] for a in acc_refs], extra_refs, out_refs, scratch_refs)

    ins, in_specs = [], []
    for p in pairs:
        ins += [p[1], p[3]]
        in_specs += [p[2], p[4]]
    ins += [e[0] for e in extras]
    in_specs += [e[1] for e in extras]
    aliases = {}
    if alias_to_out0 is not None:
        aliases = {len(ins): 0}
        ins.append(alias_to_out0)
        in_specs.append(pl.BlockSpec(memory_space=pl.ANY))
    if after is not None:
        ins.append(after)
        in_specs.append(pl.BlockSpec(memory_space=pl.ANY))
    scratch_shapes = [pltpu.VMEM(s, F32) for s in acc_shapes] + list(scratch)
    return pl.pallas_call(body, grid=grid, in_specs=in_specs, out_specs=[o[1] for o in outs], out_shape=[o[0] for o in outs],
                          scratch_shapes=scratch_shapes, input_output_aliases=aliases, compiler_params=_cp(3), name=name)(*ins)


def _my_index():
    return 4 * lax.axis_index("x") + 2 * lax.axis_index("y") + lax.axis_index("c")


def _peer(d):
    mx, my, mc = lax.axis_index("x"), lax.axis_index("y"), lax.axis_index("c")
    return (mx ^ ((d >> 2) & 1), my ^ ((d >> 1) & 1), mc ^ (d & 1))


def _win(ref, kind, j, n):
    if kind == "all":
        return ref
    if kind == "slot":
        return ref.at[j]
    if kind == "rows":
        return ref.at[pl.ds(pl.multiple_of(j * n, 8), n)]
    return ref.at[:, pl.ds(pl.multiple_of(j * n, 128), n)]


def _win7(ref, kind, n):
    if kind == "slot":
        return ref.at[pl.ds(0, 7)]
    if kind == "rows":
        return ref.at[pl.ds(0, 7 * n)]
    return ref.at[:, pl.ds(0, 7 * n)]


def _full_shape(shard_shape, kind):
    if kind == "slot":
        return (N_DEV,) + tuple(shard_shape)
    if kind == "rows":
        return (N_DEV * shard_shape[0],) + tuple(shard_shape[1:])
    return (shard_shape[0], N_DEV * shard_shape[1])


def _shard_shape(full_shape, kind, n):
    if kind == "all":
        return tuple(full_shape)
    if kind == "slot":
        return tuple(full_shape[1:])
    if kind == "rows":
        return (n,) + tuple(full_shape[1:])
    return (full_shape[0], n)


_HBM = pl.BlockSpec(memory_space=pltpu.HBM)
_SEM = pl.BlockSpec(memory_space=pltpu.SEMAPHORE)
_DATAFLOW = pltpu.SideEffectType.DATAFLOW_SIDE_EFFECTING


def _exchange_start(name, srcs, kinds, sizes, gather):
    n = len(srcs)
    if gather:
        lands = [lax.empty(_full_shape(s.shape, k), s.dtype) for s, k in zip(srcs, kinds)]
    else:
        lands = [lax.empty((N_DEV,) + _shard_shape(s.shape, k, z), s.dtype) for s, k, z in zip(srcs, kinds, sizes)]

    def body(*refs):
        src, land = refs[:n], refs[n:2 * n]
        send_sems, recv_sems, local_sems = refs[2 * n], refs[2 * n + 1], refs[2 * n + 2]
        token = refs[4 * n + 3]
        me = _my_index()
        for a in range(n):
            _local_copy(src[a], land[a], kinds[a], sizes[a], gather, me, local_sems.at[a]).start()
        for a in range(n):
            for d in range(1, N_DEV):
                px, py, pc = _peer(d)
                if gather:
                    s_ref, d_ref = src[a], _win(land[a], kinds[a], me, sizes[a])
                else:
                    s_ref, d_ref = _win(src[a], kinds[a], 4 * px + 2 * py + pc, sizes[a]), land[a].at[me]
                pltpu.make_async_remote_copy(src_ref=s_ref, dst_ref=d_ref, send_sem=send_sems.at[a], recv_sem=recv_sems.at[a],
                                             device_id=(px, py, pc), device_id_type=pl.DeviceIdType.MESH).start()
        token[...] = jnp.zeros_like(token)

    hbm = [pltpu.with_memory_space_constraint(a, pltpu.HBM) for a in list(srcs) + lands]
    out = pl.pallas_call(
        body, name=name, in_specs=[_HBM] * (2 * n),
        out_shape=[pltpu.SemaphoreType.DMA((n,))] * 3 + [pltpu.HBM(a.shape, a.dtype) for a in hbm] + [SDS((8, 128), F32)],
        out_specs=[_SEM] * 3 + [_HBM] * (2 * n) + [pl.BlockSpec(memory_space=pltpu.VMEM)],
        input_output_aliases={i: 3 + i for i in range(2 * n)},
        compiler_params=pltpu.CompilerParams(has_side_effects=_DATAFLOW))(*hbm)
    return out[0:3], out[3:3 + n], out[3 + n:3 + 2 * n], out[-1]


def _local_copy(src, land, kind, size, gather, me, sem):
    if gather:
        return pltpu.make_async_copy(src, _win(land, kind, me, size), sem)
    return pltpu.make_async_copy(_win(src, kind, me, size), land.at[me], sem)


def _exchange_wait(name, started, which, kinds, sizes, gather, after):
    sems, srcs, lands, _ = started
    n = len(which)

    def body(*refs):
        src, land = refs[:n], refs[n:2 * n]
        send_ref, recv_ref, local_ref = refs[2 * n:2 * n + 3]
        me = _my_index()
        my_id = (lax.axis_index("x"), lax.axis_index("y"), lax.axis_index("c"))
        for i, a in enumerate(which):
            seven = _win7(land[i], kinds[a], sizes[a]) if gather else land[i].at[pl.ds(0, 7)]
            pltpu.make_async_remote_copy(src_ref=seven, dst_ref=seven, send_sem=send_ref.at[a], recv_sem=recv_ref.at[a],
                                         device_id=my_id, device_id_type=pl.DeviceIdType.MESH).wait()
            _local_copy(src[i], land[i], kinds[a], sizes[a], gather, me, local_ref.at[a]).wait()

    hbm = [srcs[a] for a in which] + [lands[a] for a in which]
    out = pl.pallas_call(
        body, name=name, in_specs=[_HBM] * (2 * n) + [_SEM] * 3 + [pl.BlockSpec(memory_space=pl.ANY)],
        out_shape=[pltpu.HBM(a.shape, a.dtype) for a in hbm], out_specs=[_HBM] * (2 * n),
        input_output_aliases={i: i for i in range(2 * n)},
        compiler_params=pltpu.CompilerParams(has_side_effects=_DATAFLOW))(*hbm, *sems, after)
    return out[n:]


def _gather_small(small):
    def body(in_ref, out_ref, send_sem, recv_sem, local_sem):
        me = _my_index()
        my_id = (lax.axis_index("x"), lax.axis_index("y"), lax.axis_index("c"))
        cp = pltpu.make_async_copy(in_ref, out_ref.at[me], local_sem)
        cp.start()
        for d in range(1, N_DEV):
            pltpu.make_async_remote_copy(src_ref=in_ref, dst_ref=out_ref.at[me], send_sem=send_sem, recv_sem=recv_sem,
                                         device_id=_peer(d), device_id_type=pl.DeviceIdType.MESH).start()
        seven = out_ref.at[pl.ds(0, 7)]
        pltpu.make_async_remote_copy(src_ref=seven, dst_ref=seven, send_sem=send_sem, recv_sem=recv_sem, device_id=my_id,
                                     device_id_type=pl.DeviceIdType.MESH).wait()
        cp.wait()

    any_spec = pl.BlockSpec(memory_space=pl.ANY)
    return pl.pallas_call(body, in_specs=[any_spec], out_specs=any_spec, out_shape=SDS((N_DEV,) + small.shape, F32),
                          scratch_shapes=[pltpu.SemaphoreType.DMA] * 3, name="gather_small")(small)


def _adamw(name, recv, w, m, v):
    rows, cols = w.shape
    tr = max(c for c in range(16, 257, 16) if rows % c == 0) if rows % 16 == 0 else rows

    def body(r_ref, w_ref, m_ref, v_ref, g_ref, d_ref, nm_ref, nv_ref):
        g = r_ref[0].astype(F32)
        for s in range(1, N_DEV):
            g = g + r_ref[s].astype(F32)
        nm = ADAM_B1 * m_ref[...] + (1.0 - ADAM_B1) * g
        nv = ADAM_B2 * v_ref[...] + (1.0 - ADAM_B2) * (g * g)
        m_hat = nm / (1.0 - ADAM_B1 ** ADAM_STEP)
        v_hat = nv / (1.0 - ADAM_B2 ** ADAM_STEP)
        g_ref[...] = g
        d_ref[...] = -ADAM_LR * (m_hat / (jnp.sqrt(v_hat) + ADAM_EPS) + ADAM_WD * w_ref[...])
        nm_ref[...] = nm
        nv_ref[...] = nv

    blk = _bs((tr, cols), lambda i: (i, 0))
    return pl.pallas_call(
        body, grid=(rows // tr,), in_specs=[_bs((N_DEV, tr, cols), lambda i: (0, i, 0)), blk, blk, blk],
        out_specs=[blk] * 4, out_shape=[SDS((rows, cols), F32)] * 4, compiler_params=_cp(1), name=name)(recv, w, m, v)


def _rms_fwd(name, x, g, tm):
    t, d = x.shape

    def body(x_ref, g_ref, n_ref):
        xv = x_ref[...]
        r = lax.rsqrt(jnp.mean(xv * xv, axis=-1, keepdims=True) + EPS)
        n_ref[...] = (xv * r * g_ref[...]).astype(BF16)

    return pl.pallas_call(body, grid=(t // tm,), in_specs=[_bs((tm, d), lambda i: (i, 0)), _bs((1, d), lambda i: (0, 0))],
                          out_specs=_bs((tm, d), lambda i: (i, 0)), out_shape=SDS((t, d), BF16), compiler_params=_cp(1),
                          name=name)(x, g)


def _accumulate_rows(ref, part):
    @pl.when(pl.program_id(0) == 0)
    def _():
        ref[...] = part

    @pl.when(pl.program_id(0) > 0)
    def _():
        ref[...] += part


def _rms_bwd_epilogue(prods, extra_refs, out_refs, scratch_refs):
    dyv = prods[0]
    for p in prods[1:]:
        dyv = dyv + p
    if len(extra_refs) > 3:
        dyv = dyv + extra_refs[3][...]
    xv = extra_refs[0][...]
    r = lax.rsqrt(jnp.mean(xv * xv, axis=-1, keepdims=True) + EPS)
    xh = xv * r
    dxh = dyv * extra_refs[1][...]
    dx = extra_refs[2][...] + r * (dxh - xh * jnp.mean(dxh * xh, axis=-1, keepdims=True))
    for o in out_refs[:-1]:
        o[...] = dx.astype(o.dtype)
    _accumulate_rows(out_refs[-1], jnp.sum(dyv * xh, axis=0, keepdims=True))


def _out_norm_epilogue(prods, extra_refs, out_refs, scratch_refs):
    h = prods[0] + extra_refs[0][...]
    r = lax.rsqrt(jnp.mean(h * h, axis=-1, keepdims=True) + EPS)
    out_refs[0][...] = h
    out_refs[1][...] = (h * r * extra_refs[1][...]).astype(BF16)


def _glu_merge_epilogue(prods, extra_refs, out_refs, scratch_refs):
    ya, yb, ad = prods
    ga, gs = extra_refs[0][...].astype(F32), extra_refs[1][...].astype(F32)
    m = _sigmoid(ga) * ad + _sigmoid(gs) * (ya * _sigmoid(yb))
    out_refs[0][...] = m.astype(BF16)
    for o, val in zip(out_refs[1:], (ya, yb, ad)):
        o[...] = val.astype(o.dtype)


def _merge_bwd_epilogue(prods, extra_refs, out_refs, scratch_refs):
    dmv = prods[0]
    d = dmv.shape[1]
    ga, gs = _sigmoid(extra_refs[0][...].astype(F32)), _sigmoid(extra_refs[1][...].astype(F32))
    adv, yav = extra_refs[2][...].astype(F32), extra_refs[3][...].astype(F32)
    sb = _sigmoid(extra_refs[4][...].astype(F32))
    out_refs[0][:, 0:d] = (dmv * adv * ga * (1.0 - ga)).astype(BF16)
    out_refs[0][:, d:2 * d] = (dmv * (yav * sb) * gs * (1.0 - gs)).astype(BF16)
    dad = (dmv * ga).astype(BF16)
    dsd = dmv * gs
    dya = (dsd * sb).astype(BF16)
    dyb = (dsd * yav * sb * (1.0 - sb)).astype(BF16)
    out_refs[1][...], out_refs[2][...], out_refs[3][...] = dad, dya, dyb
    nt = _DNUMS["nt"]
    out_refs[4][...] = (lax.dot_general(dya, extra_refs[5][...], nt, preferred_element_type=F32)
                        + lax.dot_general(dyb, extra_refs[6][...], nt, preferred_element_type=F32))
    out_refs[5][...] = lax.dot_general(dad, extra_refs[7][...], nt, preferred_element_type=F32)


def _swiglu_epilogue(prods, extra_refs, out_refs, scratch_refs):
    gv, uv = prods
    out_refs[0][...] = (gv * _sigmoid(gv) * uv).astype(BF16)
    out_refs[1][...] = gv.astype(out_refs[1].dtype)
    out_refs[2][...] = uv.astype(out_refs[2].dtype)


def _swiglu_bwd_epilogue(prods, extra_refs, out_refs, scratch_refs):
    dav = prods[0]
    gv, uv = extra_refs[0][...].astype(F32), extra_refs[1][...].astype(F32)
    sg = _sigmoid(gv)
    out_refs[0][...] = (dav * uv * sg * (1.0 + gv * (1.0 - sg))).astype(BF16)
    out_refs[1][...] = (dav * gv * sg).astype(BF16)


def _head_epilogue(n_tiles):
    def epilogue(prods, extra_refs, out_refs, scratch_refs):
        h2 = prods[0] + extra_refs[0][...]
        h2_bf = h2.astype(BF16)
        out_refs[6][...] = h2_bf
        pgv = jnp.dot(h2_bf, extra_refs[3][...], preferred_element_type=F32)
        ppv = prods[1]
        d = pgv.shape[1]
        lacc = scratch_refs[0]
        sg = _sigmoid(pgv)
        h3 = h2 + sg * ppv
        r = lax.rsqrt(jnp.mean(h3 * h3, axis=-1, keepdims=True) + EPS)
        xh = h3 * r
        gv = extra_refs[1][...]
        diff = xh * gv - extra_refs[2][...]
        dout = diff * (1.0 / d)
        dxh = dout * gv
        dh3 = r * (dxh - xh * jnp.mean(dxh * xh, axis=-1, keepdims=True))
        dpg = (dh3 * ppv * sg * (1.0 - sg)).astype(BF16)
        dh2 = dh3 + lax.dot_general(dpg, extra_refs[3][...], _DNUMS["nt"], preferred_element_type=F32)
        out_refs[2][...] = dh2
        out_refs[3][...] = dh2.astype(BF16)
        out_refs[4][...] = (dh3 * sg).astype(BF16)
        out_refs[5][...] = dpg
        _accumulate_rows(out_refs[1], jnp.sum(dout * xh, axis=0, keepdims=True))
        _accumulate_rows(lacc, jnp.sum(diff * diff, axis=0, keepdims=True))

        @pl.when(pl.program_id(0) == n_tiles - 1)
        def _():
            out_refs[0][...] = (0.5 / d) * jnp.sum(lacc[...], axis=-1, keepdims=True)

    return epilogue


def _strided(r, n, d):
    return pl.ds(r, n, stride=d) if d > 1 else pl.ds(0, n)


def _rope_tables(pos_ref, invf_ref, c_s, s_s):
    ang = pos_ref[...].astype(F32) * invf_ref[...]
    lane = lax.broadcasted_iota(jnp.int32, ang.shape, 1)
    sn = jnp.sin(ang)
    c_s[...] = jnp.where(lane < ROPE_DIM, jnp.cos(ang), 1.0)
    s_s[...] = jnp.where(lane < ROPE_HALF, -sn, jnp.where(lane < ROPE_DIM, sn, 0.0))


def _rope_partner(xv, first_half):
    return jnp.where(first_half, pltpu.roll(xv, HEAD_DIM - ROPE_HALF, 1), pltpu.roll(xv, ROPE_HALF, 1))


def _rope_dilate_epilogue(tm):
    def epilogue(prods, extra_refs, out_refs, scratch_refs):
        zv = prods[0]
        pos_ref, invf_ref = extra_refs
        c_s, s_s, rot = scratch_refs
        c = pl.program_id(1)

        @pl.when(c == 0)
        def _():
            _rope_tables(pos_ref, invf_ref, c_s, s_s)

        @pl.when(c < 2)
        def _():
            cc, ss = c_s[...], s_s[...]
            first_half = lax.broadcasted_iota(jnp.int32, cc.shape, 1) < ROPE_HALF
            for h in range(QK_W // HEAD_DIM):
                xv = zv[:, h * HEAD_DIM:(h + 1) * HEAD_DIM]
                rot[h] = xv * cc + _rope_partner(xv, first_half) * ss

        @pl.when(c == 2)
        def _():
            for h in range(QK_W // HEAD_DIM):
                rot[h] = zv[:, h * HEAD_DIM:(h + 1) * HEAD_DIM]

        for g, (d, o_ref) in enumerate(zip(DILATIONS, out_refs)):
            n = tm // d
            for r in range(d):
                for hh in range(HEADS_PER_GROUP):
                    oc = r * GROUP_W + hh * HEAD_DIM
                    o_ref[:, oc:oc + HEAD_DIM] = rot[g * HEADS_PER_GROUP + hh, _strided(r, n, d), :].astype(BF16)

    return epilogue


def _band_masks(first_tile):
    qi = lax.broadcasted_iota(jnp.int32, (BLK, 2 * BLK), 0)
    kj = lax.broadcasted_iota(jnp.int32, (BLK, 2 * BLK), 1)
    band = (kj >= qi) & (kj <= qi + BLK)
    return band, band & ((kj >= BLK) | jnp.logical_not(first_tile))


def _attn_fwd(qkv, d, qt):
    ell = qkv.shape[1]
    nsub = qt // BLK
    scale = 1.0 / math.sqrt(HEAD_DIM)

    def body(q_ref, kc_ref, kp_ref, vc_ref, vp_ref, o_ref, lse_ref, kcat, vcat):
        nb = pl.program_id(1)
        kcat[0:BLK, :] = kp_ref[...]
        kcat[BLK:, :] = kc_ref[...]
        vcat[0:BLK, :] = vp_ref[...]
        vcat[BLK:, :] = vc_ref[...]
        lane = lax.broadcasted_iota(jnp.int32, (BLK, HEAD_DIM), 1)
        band, band_first = _band_masks(nb == 0)
        for b in range(nsub):
            valid = band_first if b == 0 else band
            lse_t = jnp.zeros((BLK, HEAD_DIM), F32)
            for hh in range(HEADS_PER_GROUP):
                cs = slice(hh * HEAD_DIM, (hh + 1) * HEAD_DIM)
                qb = q_ref[b * BLK:(b + 1) * BLK, cs]
                kk = kcat[b * BLK:(b + 2) * BLK, cs]
                vv = vcat[b * BLK:(b + 2) * BLK, cs]
                s = lax.dot_general(qb, kk, _DNUMS["nt"], preferred_element_type=F32) * scale
                s = jnp.where(valid, s, NEG)
                mx = jnp.max(s, axis=-1, keepdims=True)
                p = jnp.exp(s - mx)
                den = jnp.sum(p, axis=-1, keepdims=True)
                o = jnp.dot(p.astype(BF16), vv, preferred_element_type=F32) / den
                o_ref[b * BLK:(b + 1) * BLK, cs] = o
                lse_t = jnp.where(lane == hh, mx + jnp.log(den), lse_t)
            lse_ref[b * BLK:(b + 1) * BLK, :] = lse_t

    cur = lambda c: _bs((None, qt, GROUP_W), lambda r, nb: (c, nb, r))
    prev = lambda c: _bs((None, BLK, GROUP_W), lambda r, nb: (c, jnp.maximum(nb * nsub - 1, 0), r))
    return pl.pallas_call(
        body, grid=(d, ell // qt), in_specs=[cur(0), cur(1), prev(1), cur(2), prev(2)],
        out_specs=[_bs((qt, GROUP_W), lambda r, nb: (nb, r)), _bs((None, qt, HEAD_DIM), lambda r, nb: (r, nb, 0))],
        out_shape=[SDS((ell, d * GROUP_W), F32), SDS((d, ell, HEAD_DIM), F32)],
        scratch_shapes=[pltpu.VMEM((qt + BLK, GROUP_W), BF16)] * 2, compiler_params=_cp(2), name=f"attn_fwd_d{d}")(
            qkv, qkv, qkv, qkv, qkv)


def _attn_merge(outs, lses, tm):
    t = outs[0].shape[0]

    def body(o0, o1, o2, l0, l1, l2, attn_ref, attn_bf_ref, t0, t1, t2, so, sl, lt_s):
        for g, (d, o_ref, l_ref) in enumerate(zip(DILATIONS, (o0, o1, o2), (l0, l1, l2))):
            n = tm // d
            for r in range(d):
                rows = _strided(r, n, d)
                for hh in range(HEADS_PER_GROUP):
                    oc = r * GROUP_W + hh * HEAD_DIM
                    so[g * HEADS_PER_GROUP + hh, rows, :] = o_ref[:, oc:oc + HEAD_DIM]
                sl[g, rows, :] = l_ref[r]
        ls = [sl[g] for g in range(N_GROUPS)]
        mx = jnp.maximum(jnp.maximum(ls[0], ls[1]), ls[2])
        es = [jnp.exp(l - mx) for l in ls]
        den = es[0] + es[1] + es[2]
        ws = [e / den for e in es]
        lt_s[...] = mx + jnp.log(den)
        for hh in range(HEADS_PER_GROUP):
            cs = slice(hh * HEAD_DIM, (hh + 1) * HEAD_DIM)
            a = ws[0][:, hh:hh + 1] * so[hh]
            for g in range(1, N_GROUPS):
                a = a + ws[g][:, hh:hh + 1] * so[g * HEADS_PER_GROUP + hh]
            attn_ref[:, cs] = a
            attn_bf_ref[:, cs] = a.astype(BF16)
        for d, t_ref in zip(DILATIONS, (t0, t1, t2)):
            n = tm // d
            for r in range(d):
                t_ref[r] = lt_s[_strided(r, n, d), :]

    dil = lambda d: _bs((tm // d, d * GROUP_W), lambda i: (i, 0))
    lsp = lambda d: _bs((d, tm // d, HEAD_DIM), lambda i: (0, i, 0))
    row = _bs((tm, GROUP_W), lambda i: (i, 0))
    return pl.pallas_call(
        body, grid=(t // tm,),
        in_specs=[dil(d) for d in DILATIONS] + [lsp(d) for d in DILATIONS],
        out_specs=[row, row] + [lsp(d) for d in DILATIONS],
        out_shape=[SDS((t, GROUP_W), F32), SDS((t, GROUP_W), BF16)] + [SDS(l.shape, F32) for l in lses],
        scratch_shapes=[pltpu.VMEM((N_GROUPS * HEADS_PER_GROUP, tm, HEAD_DIM), F32), pltpu.VMEM((N_GROUPS, tm, HEAD_DIM), F32),
                        pltpu.VMEM((tm, HEAD_DIM), F32)],
        compiler_params=_cp(1), name="attn_merge")(*outs, *lses)


def _attn_bwd_pre(d_attn, attn, tm):
    t = attn.shape[0]

    def body(da_ref, a_ref, g0, g1, g2, e0, e1, e2, dl_s, da_s):
        lane = lax.broadcasted_iota(jnp.int32, (tm, HEAD_DIM), 1)
        dl = jnp.zeros((tm, HEAD_DIM), F32)
        for hh in range(HEADS_PER_GROUP):
            cs = slice(hh * HEAD_DIM, (hh + 1) * HEAD_DIM)
            dav = da_ref[:, cs]
            da_s[hh] = dav
            dl = jnp.where(lane == hh, jnp.sum(dav * a_ref[:, cs], axis=-1, keepdims=True), dl)
        dl_s[...] = dl
        for d, g_ref, e_ref in zip(DILATIONS, (g0, g1, g2), (e0, e1, e2)):
            n = tm // d
            for r in range(d):
                rows = _strided(r, n, d)
                for hh in range(HEADS_PER_GROUP):
                    oc = r * GROUP_W + hh * HEAD_DIM
                    g_ref[:, oc:oc + HEAD_DIM] = da_s[hh, rows, :].astype(BF16)
                e_ref[r] = dl_s[rows, :]

    row = _bs((tm, GROUP_W), lambda i: (i, 0))
    return pl.pallas_call(
        body, grid=(t // tm,), in_specs=[row, row],
        out_specs=[_bs((tm // d, d * GROUP_W), lambda i: (i, 0)) for d in DILATIONS]
        + [_bs((d, tm // d, HEAD_DIM), lambda i: (0, i, 0)) for d in DILATIONS],
        out_shape=[SDS((t // d, d * GROUP_W), BF16) for d in DILATIONS]
        + [SDS((d, t // d, HEAD_DIM), F32) for d in DILATIONS],
        scratch_shapes=[pltpu.VMEM((tm, HEAD_DIM), F32), pltpu.VMEM((HEADS_PER_GROUP, tm, HEAD_DIM), F32)],
        compiler_params=_cp(1), name="attn_bwd_pre")(d_attn, attn)


def _attn_bwd(qkv, d_a, lt, delta, d, qt):
    ell = qkv.shape[1]
    nsub = qt // BLK
    ntile = ell // qt
    nblk = ell // BLK
    scale = 1.0 / math.sqrt(HEAD_DIM)

    def body(q_ref, qn_ref, kc_ref, kp_ref, vc_ref, vp_ref, da_ref, dan_ref, lt_ref, ltn_ref, dl_ref, dln_ref, o_ref,
             kcat, vcat, dk_acc, dv_acc):
        nb = pl.program_id(1)
        kcat[0:BLK, :] = kp_ref[...]
        kcat[BLK:, :] = kc_ref[...]
        vcat[0:BLK, :] = vp_ref[...]
        vcat[BLK:, :] = vc_ref[...]
        qi = lax.broadcasted_iota(jnp.int32, (BLK, BLK), 0)
        kj = lax.broadcasted_iota(jnp.int32, (BLK, BLK), 1)
        valid_next = (kj >= qi) & (nb < ntile - 1)
        band, band_first = _band_masks(nb == 0)
        for hh in range(HEADS_PER_GROUP):
            cs = slice(hh * HEAD_DIM, (hh + 1) * HEAD_DIM)
            dk_acc[...] = jnp.zeros_like(dk_acc)
            dv_acc[...] = jnp.zeros_like(dv_acc)
            for b in range(nsub):
                rs = slice(b * BLK, (b + 1) * BLK)
                ks = slice(b * BLK, (b + 2) * BLK)
                valid = band_first if b == 0 else band
                qb, kk, vv, dab = q_ref[rs, cs], kcat[ks, cs], vcat[ks, cs], da_ref[rs, cs]
                s = lax.dot_general(qb, kk, _DNUMS["nt"], preferred_element_type=F32) * scale
                p = jnp.where(valid, jnp.exp(s - lt_ref[rs, hh:hh + 1]), 0.0)
                dp = lax.dot_general(dab, vv, _DNUMS["nt"], preferred_element_type=F32)
                ds = (p * (dp - dl_ref[rs, hh:hh + 1])).astype(BF16)
                o_ref[0, rs, cs] = jnp.dot(ds, kk, preferred_element_type=F32) * scale
                dk_acc[ks, :] += lax.dot_general(ds, qb, _DNUMS["tn"], preferred_element_type=F32) * scale
                dv_acc[ks, :] += lax.dot_general(p.astype(BF16), dab, _DNUMS["tn"], preferred_element_type=F32)
            ks = slice(nsub * BLK, (nsub + 1) * BLK)
            qn, kl, vl, dan = qn_ref[:, cs], kcat[ks, cs], vcat[ks, cs], dan_ref[:, cs]
            s = lax.dot_general(qn, kl, _DNUMS["nt"], preferred_element_type=F32) * scale
            p = jnp.where(valid_next, jnp.exp(s - ltn_ref[:, hh:hh + 1]), 0.0)
            dp = lax.dot_general(dan, vl, _DNUMS["nt"], preferred_element_type=F32)
            ds = (p * (dp - dln_ref[:, hh:hh + 1])).astype(BF16)
            dk_acc[ks, :] += lax.dot_general(ds, qn, _DNUMS["tn"], preferred_element_type=F32) * scale
            dv_acc[ks, :] += lax.dot_general(p.astype(BF16), dan, _DNUMS["tn"], preferred_element_type=F32)
            o_ref[1, :, cs] = dk_acc[BLK:, :]
            o_ref[2, :, cs] = dv_acc[BLK:, :]

    nxt = lambda nb: jnp.minimum((nb + 1) * nsub, nblk - 1)
    prv = lambda nb: jnp.maximum(nb * nsub - 1, 0)
    cur3 = lambda c: _bs((None, qt, GROUP_W), lambda r, nb: (c, nb, r))
    in_specs = [
        cur3(0), _bs((None, BLK, GROUP_W), lambda r, nb: (0, nxt(nb), r)),
        cur3(1), _bs((None, BLK, GROUP_W), lambda r, nb: (1, prv(nb), r)),
        cur3(2), _bs((None, BLK, GROUP_W), lambda r, nb: (2, prv(nb), r)),
        _bs((qt, GROUP_W), lambda r, nb: (nb, r)), _bs((BLK, GROUP_W), lambda r, nb: (nxt(nb), r)),
        _bs((None, qt, HEAD_DIM), lambda r, nb: (r, nb, 0)), _bs((None, BLK, HEAD_DIM), lambda r, nb: (r, nxt(nb), 0)),
        _bs((None, qt, HEAD_DIM), lambda r, nb: (r, nb, 0)), _bs((None, BLK, HEAD_DIM), lambda r, nb: (r, nxt(nb), 0)),
    ]
    return pl.pallas_call(
        body, grid=(d, ntile), in_specs=in_specs, out_specs=_bs((3, qt, GROUP_W), lambda r, nb: (0, nb, r)),
        out_shape=SDS((3, ell, d * GROUP_W), F32),
        scratch_shapes=[pltpu.VMEM((qt + BLK, GROUP_W), BF16)] * 2 + [pltpu.VMEM((qt + BLK, HEAD_DIM), F32)] * 2,
        compiler_params=_cp(2), name=f"attn_bwd_d{d}")(qkv, qkv, qkv, qkv, qkv, qkv, d_a, d_a, lt, lt, delta, delta)


def _undilate_rope_bwd(dqkvs, pos, invf, tm):
    t = pos.shape[0]

    def body(g0, g1, g2, pos_ref, invf_ref, o_ref, c_s, s_s, nat):
        c = pl.program_id(1)

        @pl.when(c == 0)
        def _():
            _rope_tables(pos_ref, invf_ref, c_s, s_s)

        for g, (d, g_ref) in enumerate(zip(DILATIONS, (g0, g1, g2))):
            n = tm // d
            for r in range(d):
                for hh in range(HEADS_PER_GROUP):
                    oc = r * GROUP_W + hh * HEAD_DIM
                    nat[g * HEADS_PER_GROUP + hh, _strided(r, n, d), :] = g_ref[:, oc:oc + HEAD_DIM]

        @pl.when(c < 2)
        def _():
            cc, ss = c_s[...], s_s[...]
            first_half = lax.broadcasted_iota(jnp.int32, cc.shape, 1) < ROPE_HALF
            for h in range(QK_W // HEAD_DIM):
                xv = nat[h]
                y = xv * cc - _rope_partner(xv, first_half) * ss
                o_ref[:, h * HEAD_DIM:(h + 1) * HEAD_DIM] = y.astype(BF16)

        @pl.when(c == 2)
        def _():
            for h in range(QK_W // HEAD_DIM):
                o_ref[:, h * HEAD_DIM:(h + 1) * HEAD_DIM] = nat[h].astype(BF16)

    return pl.pallas_call(
        body, grid=(t // tm, 3),
        in_specs=[_bs((None, tm // d, d * GROUP_W), lambda i, c: (c, i, 0)) for d in DILATIONS]
        + [_bs((tm, 1), lambda i, c: (i, 0)), _bs((1, HEAD_DIM), lambda i, c: (0, 0))],
        out_specs=_bs((tm, QK_W), lambda i, c: (i, c)), out_shape=SDS((t, 3 * QK_W), BF16),
        scratch_shapes=[pltpu.VMEM((tm, HEAD_DIM), F32)] * 2 + [pltpu.VMEM((QK_W // HEAD_DIM, tm, HEAD_DIM), F32)],
        compiler_params=_cp(2), name="undilate_rope_bwd")(*dqkvs, pos, invf)


def _cmul(ar, ai, br, bi):
    return ar * br - ai * bi, ar * bi + ai * br


def _ssm_disc(a_re, a_im, log_dt, nsq):
    def body(lr_ref, li_ref, ldt_ref, br_ref, bi_ref, zr_ref, zi_ref, pr_ref, pi_ref):
        lr, li = lr_ref[...], li_ref[...]
        dt = jnp.exp(ldt_ref[...])
        mag = jnp.exp(lr * dt)
        bar_re, bar_im = mag * jnp.cos(li * dt), mag * jnp.sin(li * dt)
        nr, ni = bar_re - 1.0, bar_im
        den = lr * lr + li * li
        br_ref[...], bi_ref[...] = bar_re, bar_im
        zr_ref[...] = (nr * lr + ni * li) / den
        zi_ref[...] = (ni * lr - nr * li) / den
        pr, pi = bar_re, bar_im
        for _ in range(nsq):
            pr, pi = _cmul(pr, pi, pr, pi)
        pr_ref[...], pi_ref[...] = pr, pi

    return pl.pallas_call(body, out_shape=[SDS(a_re.shape, F32)] * 6, name="ssm_discretise")(a_re, a_im, log_dt)


def _ssm_scale_b(z_re, z_im, b_re, b_im):
    def body(zr_ref, zi_ref, br_ref, bi_ref, or_ref, oi_ref):
        zr, zi, br, bi = zr_ref[...], zi_ref[...], br_ref[...], bi_ref[...]
        or_ref[...] = zr * br - zi * bi
        oi_ref[...] = zr * bi + zi * br

    return pl.pallas_call(body, out_shape=[SDS(b_re.shape, F32)] * 2, name="ssm_scale_b")(z_re, z_im, b_re, b_im)


def _ssm_scale_b_bwd(z_re, z_im, b_re, b_im, g_re, g_im):
    def body(zr_ref, zi_ref, br_ref, bi_ref, gr_ref, gi_ref, dbr_ref, dbi_ref, dzr_ref, dzi_ref):
        zr, zi, br, bi, gr, gi = zr_ref[...], zi_ref[...], br_ref[...], bi_ref[...], gr_ref[...], gi_ref[...]
        dbr_ref[...] = zr * gr + zi * gi
        dbi_ref[...] = zr * gi - zi * gr
        dzr_ref[...] = jnp.sum(br * gr + bi * gi, axis=-1, keepdims=True)
        dzi_ref[...] = jnp.sum(br * gi - bi * gr, axis=-1, keepdims=True)

    return pl.pallas_call(body, out_shape=[SDS(b_re.shape, F32)] * 2 + [SDS(z_re.shape, F32)] * 2,
                          name="ssm_scale_b_bwd")(z_re, z_im, b_re, b_im, g_re, g_im)


def _ssm_disc_bwd(a_re, a_im, log_dt, gb_re, gb_im, gz_re, gz_im):
    def body(lr_ref, li_ref, ldt_ref, gbr_ref, gbi_ref, gzr_ref, gzi_ref, dar_ref, dai_ref, dldt_ref):
        lr, li = lr_ref[...], li_ref[...]
        dt = jnp.exp(ldt_ref[...])
        mag = jnp.exp(lr * dt)
        bar_re, bar_im = mag * jnp.cos(li * dt), mag * jnp.sin(li * dt)
        nr, ni = bar_re - 1.0, bar_im
        den = lr * lr + li * li
        zr, zi = (nr * lr + ni * li) / den, (ni * lr - nr * li) / den
        gzr, gzi = gzr_ref[...], gzi_ref[...]
        gbr = gbr_ref[...] + (lr * gzr - li * gzi) / den
        gbi = gbi_ref[...] + (lr * gzi + li * gzr) / den
        qr, qi = (zr * lr + zi * li) / den, (zi * lr - zr * li) / den
        dar_ref[...] = dt * (bar_re * gbr + bar_im * gbi) - qr * gzr - qi * gzi
        dai_ref[...] = dt * (bar_re * gbi - bar_im * gbr) - qr * gzi + qi * gzr
        wr, wi = lr * bar_re - li * bar_im, lr * bar_im + li * bar_re
        dldt_ref[...] = dt * jnp.sum(wr * gbr + wi * gbi, axis=-1, keepdims=True)

    return pl.pallas_call(body, out_shape=[SDS(a_re.shape, F32)] * 2 + [SDS(log_dt.shape, F32)],
                          name="ssm_discretise_bwd")(a_re, a_im, log_dt, gb_re, gb_im, gz_re, gz_im)


def _interleave_epilogue(prods, extra_refs, out_refs, scratch_refs):
    uv = prods[0]
    tmp = scratch_refs[0]
    n = uv.shape[0] // N_DEV
    for b in range(SSM_W // BLK):
        cs = slice(b * BLK, (b + 1) * BLK)
        for j in range(N_DEV):
            tmp[b, pl.ds(j, n, stride=N_DEV), :] = uv[j * n:(j + 1) * n, cs]
        out_refs[0][:, cs] = tmp[b]
        out_refs[1][:, cs] = tmp[b].astype(BF16)


def _drive(src_ref, mat_ref, dst, mode):
    for kn in range(2 * SSM_NB):
        n = kn % SSM_NB
        a = src_ref[:, n * BLK:(n + 1) * BLK]
        dst[:, kn * 512:(kn + 1) * 512] = lax.dot_general(a, mat_ref[kn], _DNUMS[mode], preferred_element_type=F32)


def _scan_chunk(src, lam_ref, carry, *, reverse, store=None, h_ref=None, acc=None):
    steps = src.shape[0] // 8
    for c in range(NSTATE // SCAN_LANES):
        re = slice(c * SCAN_LANES, (c + 1) * SCAN_LANES)
        im = slice(NSTATE + c * SCAN_LANES, NSTATE + (c + 1) * SCAN_LANES)
        ar, ai = lam_ref[:, re], lam_ref[:, im]

        def step(s, val):
            i = (steps - 1 - s) if reverse else s
            rows = pl.ds(pl.multiple_of(i * 8, 8), 8)
            if acc is not None:
                hr, hi, dr, di = val
                pr, pi = h_ref[rows, re], h_ref[rows, im]
                dr = dr + hr * pr + hi * pi
                di = di + hi * pr - hr * pi
            else:
                hr, hi = val
            nr = ar * hr - ai * hi + src[rows, re]
            ni = ar * hi + ai * hr + src[rows, im]
            if store is not None:
                store[rows, re] = nr
                store[rows, im] = ni
            return (nr, ni, dr, di) if acc is not None else (nr, ni)

        init = (carry[:, re], carry[:, im])
        if acc is not None:
            init = init + (acc[:, re], acc[:, im])
        out = lax.fori_loop(0, steps, step, init, unroll=4)
        carry[:, re], carry[:, im] = out[0], out[1]
        if acc is not None:
            acc[:, re], acc[:, im] = out[2], out[3]


def _segment_carries(e_ref, pw_ref, out_ref, reverse):
    pr, pi = pw_ref[:, 0:NSTATE], pw_ref[:, NSTATE:]
    hr = jnp.zeros((1, NSTATE), F32)
    hi = jnp.zeros((1, NSTATE), F32)
    order = range(N_DEV - 1, -1, -1) if reverse else range(N_DEV)
    for j in order:
        out_ref[j:j + 1, 0:NSTATE] = hr
        out_ref[j:j + 1, NSTATE:] = hi
        tr, ti = _cmul(pr, pi, hr, hi)
        hr, hi = e_ref[j:j + 1, 0:NSTATE] + tr, e_ref[j:j + 1, NSTATE:] + ti


def _ssm_carries(name, src, mat, mode, lam8, pw, reverse):
    t = src.shape[0]
    nchunk = t // SCAN_ROWS

    def body(src_ref, mat_ref, lam_ref, pw_ref, out_ref, drive, carry):
        c = pl.program_id(0)

        @pl.when(c == 0)
        def _():
            carry[...] = jnp.zeros_like(carry)

        _drive(src_ref, mat_ref, drive, mode)
        _scan_chunk(drive, lam_ref, carry, reverse=reverse)

        @pl.when(c == nchunk - 1)
        def _():
            _segment_carries(carry, pw_ref, out_ref, reverse)

    blk = (lambda c: (nchunk - 1 - c, 0)) if reverse else (lambda c: (c, 0))
    return pl.pallas_call(
        body, grid=(nchunk,),
        in_specs=[_bs((SCAN_ROWS, SSM_W), blk), _bs(mat.shape, lambda c: (0, 0, 0)), _bs((8, 2 * NSTATE), lambda c: (0, 0)),
                  _bs((1, 2 * NSTATE), lambda c: (0, 0))],
        out_specs=_bs((8, 2 * NSTATE), lambda c: (0, 0)), out_shape=SDS((8, 2 * NSTATE), F32),
        scratch_shapes=[pltpu.VMEM((SCAN_ROWS, 2 * NSTATE), F32), pltpu.VMEM((8, 2 * NSTATE), F32)],
        compiler_params=_cp(1), name=name)(src, mat, lam8, pw)


def _ssm_fwd(u_bf, u, d_skip, bd, cd, lam8, start):
    t = u_bf.shape[0]
    nchunk = t // SCAN_ROWS
    per_seg = SCAN_ROWS // N_DEV

    def body(ub_ref, u_ref, d_ref, bd_ref, cd_ref, lam_ref, start_ref, h_ref, ys_ref, yg_ref, drive, carry, tmp):
        @pl.when(pl.program_id(0) == 0)
        def _():
            carry[...] = start_ref[...]

        _drive(ub_ref, bd_ref, drive, "nn")
        _scan_chunk(drive, lam_ref, carry, reverse=False, store=h_ref)
        for n in range(SSM_NB):
            cs = slice(n * BLK, (n + 1) * BLK)
            hr = h_ref[:, n * 512:(n + 1) * 512].astype(BF16)
            hi = h_ref[:, NSTATE + n * 512:NSTATE + (n + 1) * 512].astype(BF16)
            ys = (jnp.dot(hr, cd_ref[n], preferred_element_type=F32) + jnp.dot(hi, cd_ref[SSM_NB + n], preferred_element_type=F32)
                  + d_ref[:, cs] * u_ref[:, cs])
            ys_ref[:, cs] = ys
            tmp[n] = _gelu_parts(ys)[0]
            for j in range(N_DEV):
                yg_ref[j, :, cs] = tmp[n, pl.ds(j, per_seg, stride=N_DEV), :].astype(BF16)

    row = _bs((SCAN_ROWS, SSM_W), lambda c: (c, 0))
    h, ys, yg = pl.pallas_call(
        body, grid=(nchunk,),
        in_specs=[row, row, _bs((1, SSM_W), lambda c: (0, 0)), _bs(bd.shape, lambda c: (0, 0, 0)), _bs(cd.shape, lambda c: (0, 0, 0)),
                  _bs((8, 2 * NSTATE), lambda c: (0, 0)), _bs((8, 2 * NSTATE), lambda c: (0, 0))],
        out_specs=[_bs((SCAN_ROWS, 2 * NSTATE), lambda c: (c, 0)), row, _bs((N_DEV, per_seg, SSM_W), lambda c: (0, c, 0))],
        out_shape=[SDS((t, 2 * NSTATE), F32), SDS((t, SSM_W), F32), SDS((N_DEV, t // N_DEV, SSM_W), BF16)],
        scratch_shapes=[pltpu.VMEM((SCAN_ROWS, 2 * NSTATE), F32), pltpu.VMEM((8, 2 * NSTATE), F32),
                        pltpu.VMEM((SSM_NB, SCAN_ROWS, BLK), F32)],
        compiler_params=_cp(1), name="ssm_scan_fwd")(u_bf, u, d_skip, bd, cd, lam8, start)
    return h, ys, yg.reshape(t, SSM_W)


def _ssm_bwd(dys_bf, dys, d_skip, u_bf, h, bd, cd, lamc8, start):
    t = u_bf.shape[0]
    nchunk = t // SCAN_ROWS
    per_seg = SCAN_ROWS // N_DEV

    def body(dys_ref, dysf_ref, d_ref, u_ref, h_ref, bd_ref, cd_ref, lam_ref, start_ref, du_ref, dlam_ref, dbd_ref, dcd_ref,
             drive, adj, carry, tmp):
        c = pl.program_id(0)

        @pl.when(c == 0)
        def _():
            carry[...] = start_ref[...]
            dlam_ref[...] = jnp.zeros_like(dlam_ref)
            dbd_ref[...] = jnp.zeros_like(dbd_ref)
            dcd_ref[...] = jnp.zeros_like(dcd_ref)

        _drive(dys_ref, cd_ref, drive, "nt")
        _scan_chunk(drive, lam_ref, carry, reverse=True, store=adj, h_ref=h_ref, acc=dlam_ref)
        for n in range(SSM_NB):
            cs = slice(n * BLK, (n + 1) * BLK)
            acc = None
            for k in range(2):
                kn = k * SSM_NB + n
                ss = slice(kn * 512, (kn + 1) * 512)
                lam_b = adj[:, ss].astype(BF16)
                part = lax.dot_general(lam_b, bd_ref[kn], _DNUMS["nt"], preferred_element_type=F32)
                acc = part if acc is None else acc + part
                dbd_ref[kn] += lax.dot_general(u_ref[:, cs], lam_b, _DNUMS["tn"], preferred_element_type=F32)
                dcd_ref[kn] += lax.dot_general(h_ref[:, ss].astype(BF16), dys_ref[:, cs], _DNUMS["tn"],
                                               preferred_element_type=F32)
            tmp[n] = acc + d_ref[:, cs] * dysf_ref[:, cs]
            for j in range(N_DEV):
                du_ref[j, :, cs] = tmp[n, pl.ds(j, per_seg, stride=N_DEV), :].astype(BF16)

    rev = lambda c: (nchunk - 1 - c, 0)
    const2 = lambda c: (0, 0)
    const3 = lambda c: (0, 0, 0)
    row = _bs((SCAN_ROWS, SSM_W), rev)
    du, dlam, dbd, dcd = pl.pallas_call(
        body, grid=(nchunk,),
        in_specs=[row, row, _bs((1, SSM_W), const2), row, _bs((SCAN_ROWS, 2 * NSTATE), rev),
                  _bs(bd.shape, const3), _bs(cd.shape, const3), _bs((8, 2 * NSTATE), const2), _bs((8, 2 * NSTATE), const2)],
        out_specs=[_bs((N_DEV, per_seg, SSM_W), lambda c: (0, nchunk - 1 - c, 0)), _bs((8, 2 * NSTATE), const2),
                   _bs(bd.shape, const3), _bs(cd.shape, const3)],
        out_shape=[SDS((N_DEV, t // N_DEV, SSM_W), BF16), SDS((8, 2 * NSTATE), F32), SDS(bd.shape, F32), SDS(cd.shape, F32)],
        scratch_shapes=[pltpu.VMEM((SCAN_ROWS, 2 * NSTATE), F32), pltpu.VMEM((SCAN_ROWS, 2 * NSTATE), F32),
                        pltpu.VMEM((8, 2 * NSTATE), F32), pltpu.VMEM((SSM_NB, SCAN_ROWS, BLK), F32)],
        compiler_params=_cp(1), name="ssm_scan_bwd")(dys_bf, dys, d_skip, u_bf, h, bd, cd, lamc8, start)
    return du.reshape(t, SSM_W), dlam, dbd, dcd


def _gelu_parts(x):
    c0 = math.sqrt(2.0 / math.pi)
    inner = c0 * (x + 0.044715 * x * x * x)
    th = jnp.tanh(inner)
    val = 0.5 * x * (1.0 + th)
    grad = 0.5 * (1.0 + th) + 0.5 * x * (1.0 - th * th) * c0 * (1.0 + 3.0 * 0.044715 * x * x)
    return val, grad


def _ssm_out_bwd(d_yg, ys, u, tm):
    t = u.shape[0]
    seg = t // N_DEV

    def body(dg_ref, ys_ref, u_ref, dys_ref, dysb_ref, dd_ref, tmp):
        for n in range(SSM_W // BLK):
            for j in range(N_DEV):
                tmp[n, pl.ds(j, tm // N_DEV, stride=N_DEV), :] = dg_ref[j, :, n * BLK:(n + 1) * BLK]
        dyg = jnp.concatenate([tmp[n] for n in range(SSM_W // BLK)], axis=1)
        dys = dyg * _gelu_parts(ys_ref[...])[1]
        dys_ref[...] = dys
        dysb_ref[...] = dys.astype(BF16)
        part = jnp.sum(dys * u_ref[...], axis=0, keepdims=True)

        @pl.when(pl.program_id(0) == 0)
        def _():
            dd_ref[...] = part

        @pl.when(pl.program_id(0) > 0)
        def _():
            dd_ref[...] += part

    row = _bs((tm, SSM_W), lambda i: (i, 0))
    return pl.pallas_call(
        body, grid=(t // tm,), in_specs=[_bs((N_DEV, tm // N_DEV, SSM_W), lambda i: (0, i, 0)), row, row],
        out_specs=[row, row, _bs((1, SSM_W), lambda i: (0, 0))],
        out_shape=[SDS((t, SSM_W), F32), SDS((t, SSM_W), BF16), SDS((1, SSM_W), F32)],
        scratch_shapes=[pltpu.VMEM((SSM_W // BLK, tm, BLK), F32)], compiler_params=_cp(1), name="ssm_out_bwd")(
            d_yg.reshape(N_DEV, seg, SSM_W), ys, u)


def _block_diag(blocks):
    nb, ng, r, c = blocks.shape
    eye = jnp.eye(ng, dtype=blocks.dtype)
    return (blocks[:, :, :, None, :] * eye[None, :, None, :, None]).reshape(nb, ng * r, ng * c)


def _diag_blocks(full, r, c):
    k, nb = full.shape[:2]
    ng = full.shape[2] // r
    x = full.reshape(k, nb, ng, r, ng, c)
    eye = jnp.eye(ng, dtype=full.dtype)
    return jnp.sum(x * eye[None, None, :, None, :, None], axis=4).reshape(k, nb * ng, r, c)


_SMALL = ("a_re", "a_im", "log_dt", "b_re", "b_im", "c_re", "c_im", "d_skip", "g_ffn", "g_final")


def _pack_small(arrs):
    flat = jnp.concatenate([a.reshape(-1) for a in arrs])
    pad = (-flat.shape[0]) % (8 * 128)
    return jnp.pad(flat, (0, pad)).reshape(-1, 128)


def _unpack_small(packed, shapes):
    flat = packed.reshape(-1)
    out, off = [], 0
    for s in shapes:
        n = math.prod(s)
        out.append(flat[off:off + n].reshape(s))
        off += n
    return out


def kernel(x, p, positions, g_mix, w_in, a_re, a_im, log_dt, b_re, b_im, c_re, c_im, d_skip, w_attn_proj, w_glu_a, w_glu_b, w_out, g_ffn, w_ffn_gate, w_ffn_up, w_ffn_down, w_ple_gate, w_ple_proj, g_final, loss_target, m_g_mix, m_w_in, m_a_re, m_a_im, m_log_dt, m_b_re, m_b_im, m_c_re, m_c_im, m_d_skip, m_w_attn_proj, m_w_glu_a, m_w_glu_b, m_w_out, m_g_ffn, m_w_ffn_gate, m_w_ffn_up, m_w_ffn_down, m_w_ple_gate, m_w_ple_proj, m_g_final, v_g_mix, v_w_in, v_a_re, v_a_im, v_log_dt, v_b_re, v_b_im, v_c_re, v_c_im, v_d_skip, v_w_attn_proj, v_w_glu_a, v_w_glu_b, v_w_out, v_g_ffn, v_w_ffn_gate, v_w_ffn_up, v_w_ffn_down, v_w_ple_gate, v_w_ple_proj, v_g_final):
    args = dict(locals())
    t, d = x.shape[1], x.shape[2]
    inw = w_in.shape[2] * N_DEV
    fs = w_ffn_gate.shape[2]
    ff = fs * N_DEV
    ple = w_ple_proj.shape[1]
    seg = t // N_DEV
    assert inw == 3 * QK_W + SSM_W + 2 * d and t % (N_DEV * SCAN_ROWS // 8) == 0 and seg & (seg - 1) == 0
    tm = min(1024, t)
    te = min(512, t)
    tk = min(1024, t)
    ucol = (3 * QK_W) // SSM_W
    gcol = (3 * QK_W + SSM_W) // d
    assert (3 * QK_W + SSM_W) % d == 0

    x2, p2, tgt = x[0], p[0, 0], loss_target[0]
    pos = positions.reshape(t, 1)
    inv = ROPE_THETA ** (-jnp.arange(ROPE_HALF, dtype=F32) * 2.0 / ROPE_DIM)
    invf = jnp.concatenate([inv, inv, jnp.zeros((HEAD_DIM - ROPE_DIM,), F32)]).reshape(1, HEAD_DIM)

    wnames = ("w_in", "w_attn_proj", "w_glu_a", "w_glu_b", "w_out", "w_ffn_gate", "w_ffn_up", "w_ffn_down", "w_ple_gate",
              "w_ple_proj")
    kinds = ("cols", "cols", "cols", "cols", "rows", "slot", "slot", "rows", "rows", "cols")
    shards = [args[n][0].astype(BF16) for n in wnames]
    sizes = [s.shape[0] if k == "rows" else s.shape[-1] for s, k in zip(shards, kinds)]
    ag = _exchange_start("gather_weights_start", shards, kinds, sizes, True)

    row_d = _bs((tm, d), lambda i, j, k: (i, 0))
    row_e = _bs((te, d), lambda i, j, k: (i, 0))
    vec_d = _bs((1, d), lambda i, j, k: (0, 0))
    sq_w = _bs((d, d), lambda i, j, k: (0, 0))
    n1 = _rms_fwd("norm_mix", x2, g_mix + ag[3][0:1, 0:1], tm)
    W_in, = _exchange_wait("gather_w_in_wait", ag, [0], kinds, sizes, True, n1)
    qkv = _mm("qkv_proj", (t // tm, 3, 1), [("nn", n1, row_d, W_in, _bs((d, QK_W), lambda i, j, k: (0, j)))],
              [(SDS((3, t // dil, dil * GROUP_W), BF16), _bs((None, tm // dil, dil * GROUP_W), lambda i, j, k: (j, i, 0)))
               for dil in DILATIONS],
              extras=[(pos, _bs((tm, 1), lambda i, j, k: (i, 0))), (invf, _bs((1, HEAD_DIM), lambda i, j, k: (0, 0)))],
              epilogue=_rope_dilate_epilogue(tm),
              scratch=[pltpu.VMEM((tm, HEAD_DIM), F32)] * 2 + [pltpu.VMEM((QK_W // HEAD_DIM, tm, HEAD_DIM), F32)])
    row_s = _bs((tm, SSM_W), lambda i, j, k: (i, 0))
    u_perm, u_bf = _mm("u_proj", (t // tm, 1, 1),
                       [("nn", n1.reshape(N_DEV, seg, d), _bs((N_DEV, tm // N_DEV, d), lambda i, j, k: (0, i, 0)), W_in,
                         _bs((d, SSM_W), lambda i, j, k: (0, ucol)))],
                       [(SDS((t, SSM_W), F32), row_s), (SDS((t, SSM_W), BF16), row_s)], epilogue=_interleave_epilogue,
                       scratch=[pltpu.VMEM((SSM_W // BLK, tm, BLK), F32)])
    zg, = _mm("z_gates", (t // tm, 2, 1),
              [("nn", n1, row_d, W_in, _bs((d, d), lambda i, j, k: (0, gcol + j)))],
              [(SDS((t, 2 * d), BF16), _bs((tm, d), lambda i, j, k: (i, j)))])

    outs, lses = [], []
    for g, dil in enumerate(DILATIONS):
        o_g, l_g = _attn_fwd(qkv[g], dil, min(512, t // dil))
        outs.append(o_g)
        lses.append(l_g)
    merged = _attn_merge(outs, lses, te)
    attn, attn_bf, lts = merged[0], merged[1], merged[2:]

    nsq = seg.bit_length() - 1
    bar_re, bar_im, z_re, z_im, pw_re, pw_im = _ssm_disc(a_re[0], a_im[0], log_dt.reshape(SSM_GROUPS, 1), nsq)
    gp = SSM_GROUPS * SSM_STATE
    b_re2, b_im2 = b_re.reshape(gp, SSM_GROUP), b_im.reshape(gp, SSM_GROUP)
    bb_re, bb_im = _ssm_scale_b(z_re.reshape(gp, 1), z_im.reshape(gp, 1), b_re2, b_im2)

    def chunks(a, r, c):
        return a.reshape(SSM_NB, SSM_GROUPS // SSM_NB, r, c)

    bbt = lambda a: jnp.swapaxes(a.reshape(SSM_GROUPS, SSM_STATE, SSM_GROUP), 1, 2)
    bd = jnp.concatenate([_block_diag(chunks(bbt(bb_re), SSM_GROUP, SSM_STATE)),
                          _block_diag(chunks(bbt(bb_im), SSM_GROUP, SSM_STATE))]).astype(BF16)
    ct = lambda a: jnp.swapaxes(a[0], 1, 2)
    cd = jnp.concatenate([_block_diag(chunks(ct(c_re), SSM_STATE, SSM_GROUP)),
                          _block_diag(chunks(-ct(c_im), SSM_STATE, SSM_GROUP))]).astype(BF16)
    lam = jnp.concatenate([bar_re.reshape(1, gp), bar_im.reshape(1, gp)], axis=1)
    lamc = jnp.concatenate([bar_re.reshape(1, gp), -bar_im.reshape(1, gp)], axis=1)
    pw = jnp.concatenate([pw_re.reshape(1, gp), pw_im.reshape(1, gp)], axis=1)
    pwc = jnp.concatenate([pw_re.reshape(1, gp), -pw_im.reshape(1, gp)], axis=1)
    lam8, lamc8 = jnp.broadcast_to(lam, (8, 2 * gp)), jnp.broadcast_to(lamc, (8, 2 * gp))

    start_f = _ssm_carries("ssm_carries_fwd", u_bf, bd, "nn", lam8, pw, False)
    dsk = d_skip.reshape(1, SSM_W)
    h_all, ys, yg_bf = _ssm_fwd(u_bf, u_perm, dsk, bd, cd, lam8, start_f)
    W_ap, W_ga, W_gb, W_out, W_fg, W_fu, W_fd, W_pg, W_pp = _exchange_wait(
        "gather_rest_wait", ag, list(range(1, len(wnames))), kinds, sizes, True, yg_bf)
    W_fg = jnp.swapaxes(W_fg, 0, 1).reshape(d, ff)
    W_fu = jnp.swapaxes(W_fu, 0, 1).reshape(d, ff)

    glu_w = _bs((SSM_W, d), lambda i, j, k: (0, 0))
    row_s = _bs((tm, SSM_W), lambda i, j, k: (i, 0))
    gate_a = _bs((te, d), lambda i, j, k: (i, 0))
    gate_s = _bs((te, d), lambda i, j, k: (i, 1))
    td_f32, td_bf = SDS((t, d), F32), SDS((t, d), BF16)
    m_bf, ya, yb, attn_d = _mm(
        "glu_merge", (t // tm, 1, 1),
        [("nn", yg_bf, row_s, W_ga, glu_w), ("nn", yg_bf, row_s, W_gb, glu_w), ("nn", attn_bf, row_s, W_ap, glu_w)],
        [(td_bf, row_d)] * 4, extras=[(zg, row_d), (zg, _bs((tm, d), lambda i, j, k: (i, 1)))], epilogue=_glu_merge_epilogue)

    h1, n2 = _mm("out_proj", (t // tm, 1, 1), [("nn", m_bf, row_d, W_out, sq_w)], [(td_f32, row_d), (td_bf, row_d)],
                 extras=[(x2, row_d), (g_ffn, vec_d)], epilogue=_out_norm_epilogue)

    tn_f = ff // 2
    nf = ff // tn_f
    hid_o = _bs((tm, tn_f), lambda j, i, k: (i, j))
    tf_bf = SDS((t, ff), BF16)
    a_rows = _bs((tm, d), lambda j, i, k: (i, 0))
    w_cols = _bs((d, tn_f), lambda j, i, k: (0, j))
    act, fg, fu = _mm("ffn_gate_up", (nf, t // tm, 1), [("nn", n2, a_rows, W_fg, w_cols), ("nn", n2, a_rows, W_fu, w_cols)],
                      [(tf_bf, hid_o)] * 3, epilogue=_swiglu_epilogue)
    w_once = pl.BlockSpec((d, d), lambda i, j, k: (0, 0), pipeline_mode=pl.Buffered(1))
    loss_part, dg_final, dh2, dh2_bf, dpp_bf, dpg_bf, h2_bf = _mm(
        "ffn_down_head", (t // te, 1, 1),
        [("nn", act, _bs((te, ff), lambda i, j, k: (i, 0)), W_fd,
          pl.BlockSpec((ff, d), lambda i, j, k: (0, 0), pipeline_mode=pl.Buffered(1))),
         ("nn", p2, _bs((te, ple), lambda i, j, k: (i, 0)), W_pp, _bs((ple, d), lambda i, j, k: (0, 0)))],
        [(SDS((1, 1), F32), _bs((1, 1), lambda i, j, k: (0, 0))), (SDS((1, d), F32), vec_d), (td_f32, row_e), (td_bf, row_e),
         (td_bf, row_e), (td_bf, row_e), (td_bf, row_e)],
        extras=[(h1, row_e), (g_final.reshape(1, d), vec_d), (tgt, row_e), (W_pg, w_once)], epilogue=_head_epilogue(t // te),
        scratch=[pltpu.VMEM((1, d), F32)])
    loss = lax.psum(loss_part[0, 0], ("x", "y", "c"))

    nkt = t // tk
    tok_a = lambda w: _bs((tk, w), lambda i, j, k: (k, 0))

    def wgrad(name, a, wa, b, wb):
        return _mm(name, (1, 1, nkt), [("tn", a, tok_a(wa), b, tok_a(wb))],
                   [(SDS((wa, wb), BF16), _bs((wa, wb), lambda i, j, k: (0, 0)))])[0]

    dW_pp = wgrad("dw_ple_proj", p2, ple, dpp_bf, d)
    dW_pg = wgrad("dw_ple_gate", h2_bf, d, dpg_bf, d)
    dfg_bf, dfu_bf = _mm("d_ffn_down", (nf, t // tm, 1),
                         [("nt", dh2_bf, a_rows, W_fd, _bs((tn_f, d), lambda j, i, k: (j, 0)))],
                         [(tf_bf, hid_o), (tf_bf, hid_o)], extras=[(fg, hid_o), (fu, hid_o)], epilogue=_swiglu_bwd_epilogue)
    dW_fd, = _mm("dw_ffn_down", (nf, 1, nkt), [("tn", act, _bs((tk, tn_f), lambda i, j, k: (k, i)), dh2_bf, tok_a(d))],
                 [(SDS((ff, d), BF16), _bs((tn_f, d), lambda i, j, k: (i, 0)))])
    hid_t = _bs((tk, tn_f), lambda i, j, k: (k, j))
    wg_o = [(SDS((d, ff), BF16), _bs((d, tn_f), lambda i, j, k: (0, j)))]
    dW_fg, = _mm("dw_ffn_gate", (1, nf, nkt), [("tn", n2, tok_a(d), dfg_bf, hid_t)], wg_o)
    dW_fu, = _mm("dw_ffn_up", (1, nf, nkt), [("tn", n2, tok_a(d), dfu_bf, hid_t)], wg_o)
    dW_fg = jnp.swapaxes(dW_fg.reshape(d, N_DEV, fs), 0, 1)
    dW_fu = jnp.swapaxes(dW_fu.reshape(d, N_DEV, fs), 0, 1)
    group = lambda names: ([kinds[wnames.index(n)] for n in names], [sizes[wnames.index(n)] for n in names])
    ffn_names = ("w_ffn_gate", "w_ffn_up", "w_ffn_down", "w_ple_gate", "w_ple_proj")
    rs_ffn = _exchange_start("scatter_ffn_start", [dW_fg, dW_fu, dW_fd, dW_pg, dW_pp], *group(ffn_names), False)
    hid_all = _bs((te, ff), lambda i, j, k: (i, 0))
    w_all = pl.BlockSpec((d, ff), lambda i, j, k: (0, 0), pipeline_mode=pl.Buffered(1))
    dh1, dh1_bf, dg_ffn = _mm("d_ffn_gate_up", (t // te, 1, 1),
                              [("nt", dfg_bf, hid_all, W_fg, w_all), ("nt", dfu_bf, hid_all, W_fu, w_all)],
                              [(td_f32, row_e), (td_bf, row_e), (SDS((1, d), F32), vec_d)],
                              extras=[(h1, row_e), (g_ffn, vec_d), (dh2, row_e)], epilogue=_rms_bwd_epilogue, after=rs_ffn[3])

    dW_out = wgrad("dw_out", m_bf, d, dh1_bf, d)
    glu_once = pl.BlockSpec((SSM_W, d), lambda i, j, k: (0, 0), pipeline_mode=pl.Buffered(1))
    row_es = _bs((te, SSM_W), lambda i, j, k: (i, 0))
    ts_f32 = SDS((t, SSM_W), F32)
    dz_g, dad_bf, dya_bf, dyb_bf, d_yg, d_attn = _mm(
        "d_out_proj", (t // te, 1, 1), [("nt", dh1_bf, row_e, W_out, w_once)],
        [(SDS((t, 2 * d), BF16), _bs((te, 2 * d), lambda i, j, k: (i, 0))), (td_bf, row_e), (td_bf, row_e), (td_bf, row_e),
         (ts_f32, row_es), (ts_f32, row_es)],
        extras=[(zg, gate_a), (zg, gate_s), (attn_d, row_e), (ya, row_e), (yb, row_e), (W_ga, glu_once), (W_gb, glu_once),
                (W_ap, glu_once)], epilogue=_merge_bwd_epilogue)

    dW_ga = wgrad("dw_glu_a", yg_bf, SSM_W, dya_bf, d)
    dW_gb = wgrad("dw_glu_b", yg_bf, SSM_W, dyb_bf, d)
    dys, dys_bf, dd_skip = _ssm_out_bwd(d_yg, ys, u_perm, te)
    start_b = _ssm_carries("ssm_carries_bwd", dys_bf, cd, "nt", lamc8, pwc, True)
    dz_u, dlam8, dbd, dcd = _ssm_bwd(dys_bf, dys, dsk, u_bf, h_all, bd, cd, lamc8, start_b)
    dlam = jnp.sum(dlam8, axis=0)
    dbb = _diag_blocks(dbd.reshape(2, SSM_NB, BLK, 512), SSM_GROUP, SSM_STATE)
    dbb_re = jnp.swapaxes(dbb[0], 1, 2).reshape(gp, SSM_GROUP)
    dbb_im = jnp.swapaxes(dbb[1], 1, 2).reshape(gp, SSM_GROUP)
    dcc = _diag_blocks(dcd.reshape(2, SSM_NB, 512, BLK), SSM_STATE, SSM_GROUP)
    dc_re, dc_im = jnp.swapaxes(dcc[0], 1, 2), -jnp.swapaxes(dcc[1], 1, 2)
    db_re, db_im, dz_re, dz_im = _ssm_scale_b_bwd(z_re.reshape(gp, 1), z_im.reshape(gp, 1), b_re2, b_im2, dbb_re, dbb_im)
    gshape = (SSM_GROUPS, SSM_STATE)
    da_re, da_im, dlog_dt = _ssm_disc_bwd(a_re[0], a_im[0], log_dt.reshape(SSM_GROUPS, 1), dlam[:gp].reshape(gshape),
                                          dlam[gp:].reshape(gshape), dz_re.reshape(gshape), dz_im.reshape(gshape))

    dW_ap = wgrad("dw_attn_proj", attn_bf, GROUP_W, dad_bf, d)
    pre = _attn_bwd_pre(d_attn, attn, te)
    das, deltas = pre[:N_GROUPS], pre[N_GROUPS:]
    dqkvs = [_attn_bwd(qkv[g], das[g], lts[g], deltas[g], dil, min(512, t // dil)) for g, dil in enumerate(DILATIONS)]
    dz_qkv = _undilate_rope_bwd(dqkvs, pos, invf, tm)

    dW_in, = _mm("dw_in_qkv", (1, 3, nkt), [("tn", n1, tok_a(d), dz_qkv, _bs((tk, QK_W), lambda i, j, k: (k, j)))],
                 [(SDS((d, inw), BF16), _bs((d, QK_W), lambda i, j, k: (0, j)))])
    dW_in, = _mm("dw_in_u", (1, 1, nkt), [("tn", n1, tok_a(d), dz_u, tok_a(SSM_W))],
                 [(SDS((d, inw), BF16), _bs((d, SSM_W), lambda i, j, k: (0, ucol)))], alias_to_out0=dW_in)
    dW_in, = _mm("dw_in_gates", (1, 2, nkt), [("tn", n1, tok_a(d), dz_g, _bs((tk, d), lambda i, j, k: (k, j)))],
                 [(SDS((d, inw), BF16), _bs((d, d), lambda i, j, k: (0, gcol + j)))], alias_to_out0=dW_in)
    small_parts = dict(a_re=da_re, a_im=da_im, log_dt=dlog_dt, b_re=db_re, b_im=db_im, c_re=dc_re, c_im=dc_im,
                       d_skip=dd_skip, g_ffn=dg_ffn, g_final=dg_final)
    small = _pack_small([small_parts[n] for n in _SMALL])
    rest_names = ("w_in", "w_attn_proj", "w_glu_a", "w_glu_b", "w_out")
    rest_kinds, rest_sizes = group(rest_names)
    rs_in = _exchange_start("scatter_rest_start", [dW_in, dW_ap, dW_ga, dW_gb, dW_out, small], rest_kinds + ["all"],
                            rest_sizes + [0], False)
    w_piece = lambda w, cb: pl.BlockSpec((d, w), lambda i, j, k: (0, cb), pipeline_mode=pl.Buffered(1))
    dx, dg_mix = _mm(
        "d_z_proj", (t // te, 1, 1),
        [("nt", dz_qkv, _bs((te, 3 * QK_W), lambda i, j, k: (i, 0)), W_in, w_piece(3 * QK_W, 0)),
         ("nt", dz_u, _bs((te, SSM_W), lambda i, j, k: (i, 0)), W_in, w_piece(SSM_W, ucol)),
         ("nt", dz_g, _bs((te, d), lambda i, j, k: (i, 0)), W_in, w_piece(d, gcol)),
         ("nt", dz_g, _bs((te, d), lambda i, j, k: (i, 1)), W_in, w_piece(d, gcol + 1))],
        [(td_f32, row_e), (SDS((1, d), F32), vec_d)],
        extras=[(x2, row_e), (g_mix, vec_d), (dh1, row_e)], epilogue=_rms_bwd_epilogue, after=rs_in[3])

    received = dict(zip(ffn_names, _exchange_wait("scatter_ffn_wait", rs_ffn, list(range(len(ffn_names))), *group(ffn_names),
                                                  False, dx)))
    *landed, small_all = _exchange_wait("scatter_rest_wait", rs_in, list(range(len(rest_names) + 1)), rest_kinds + ["all"],
                                        rest_sizes + [0], False, dx)
    received.update(zip(rest_names, landed))

    new = {}
    for n in wnames:
        new[n] = [o.reshape(args[n].shape)
                  for o in _adamw("adamw_" + n, received[n], args[n][0], args["m_" + n][0], args["v_" + n][0])]
    g_mix_all = _gather_small(_pack_small([dg_mix]))
    pk = lambda pre: jnp.concatenate([_pack_small([args[pre + n] for n in _SMALL]), _pack_small([args[pre + "g_mix"]])])
    sm = _adamw("adamw_small", jnp.concatenate([small_all, g_mix_all], axis=1), pk(""), pk("m_"), pk("v_"))
    rows_a = small.shape[0]
    shapes = [args[n].shape for n in _SMALL]
    for n, vals in zip(_SMALL, zip(*[_unpack_small(o[:rows_a], shapes) for o in sm])):
        new[n] = list(vals)
    new["g_mix"] = [_unpack_small(o[rows_a:], [g_mix.shape])[0] for o in sm]

    order = ("g_mix", "w_in", "a_re", "a_im", "log_dt", "b_re", "b_im", "c_re", "c_im", "d_skip", "w_attn_proj", "w_glu_a",
             "w_glu_b", "w_out", "g_ffn", "w_ffn_gate", "w_ffn_up", "w_ffn_down", "w_ple_gate", "w_ple_proj", "g_final")
    return (loss, dx.reshape(x.shape), *[new[n][0] for n in order], *[new[n][1] for n in order],
            *[new[n][2] for n in order], *[new[n][3] for n in order])
```

```python
import functools
import math

import jax
import jax.numpy as jnp
from jax import lax
from jax.experimental import pallas as pl
from jax.experimental.pallas import tpu as pltpu

F32 = jnp.float32
BF16 = jnp.bfloat16
SDS = jax.ShapeDtypeStruct

N_DEV = 8
HEAD_DIM = 128
HEADS_PER_GROUP = 4
GROUP_W = HEADS_PER_GROUP * HEAD_DIM
DILATIONS = (1, 4, 16)
N_GROUPS = len(DILATIONS)
QK_W = N_GROUPS * GROUP_W
BLK = 128
ROPE_THETA = 500000.0
ROPE_DIM = HEAD_DIM // 4
ROPE_HALF = ROPE_DIM // 2
SSM_W = 512
SSM_GROUP = 16
SSM_GROUPS = SSM_W // SSM_GROUP
SSM_STATE = 64
NSTATE = SSM_GROUPS * SSM_STATE
SSM_NB = 4
EPS = 1e-6
ADAM_LR, ADAM_B1, ADAM_B2, ADAM_EPS, ADAM_WD, ADAM_STEP = 0.001, 0.9, 0.999, 1e-08, 0.01, 10
NEG = -1e30

VMEM_LIMIT = 52 * 1024 * 1024
SCAN_ROWS = 512
SCAN_LANES = 512


def _cp(n):
    return pltpu.CompilerParams(dimension_semantics=("arbitrary",) * n, vmem_limit_bytes=VMEM_LIMIT)


def _sigmoid(x):
    return 0.5 * jnp.tanh(0.5 * x) + 0.5


_DNUMS = {"nn": (((1,), (0,)), ((), ())), "nt": (((1,), (1,)), ((), ())), "tn": (((0,), (0,)), ((), ()))}


def _bs(shape, fn):
    return pl.BlockSpec(shape, fn)


def _store_all(prods, extra_refs, out_refs, scratch_refs):
    r = prods[0]
    for p in prods[1:]:
        r = r + p
    for e in extra_refs:
        r = r + e[...]
    for o in out_refs:
        o[...] = r.astype(o.dtype)


def _mm(name, grid, pairs, outs, extras=(), epilogue=_store_all, scratch=(), alias_to_out0=None, after=None):
    nk = grid[2]
    npair = len(pairs)
    steps = [p[5] if len(p) > 5 else nk for p in pairs]

    def block(spec):
        return tuple(s for s in spec.block_shape if s is not None)

    def rows2d(shape):
        return (math.prod(shape[:-1]), shape[-1]) if len(shape) == 3 else shape

    acc_shapes = [jax.eval_shape(lambda u, v, dn=_DNUMS[p[0]]: lax.dot_general(u, v, dn, preferred_element_type=F32),
                                 SDS(rows2d(block(p[2])), BF16), SDS(block(p[4]), BF16)).shape for p in pairs]
    if nk == 1:
        acc_shapes = []
    n_in = 2 * npair + len(extras) + (alias_to_out0 is not None) + (after is not None)

    def body(*refs):
        extra_refs = refs[2 * npair:2 * npair + len(extras)]
        out_refs = refs[n_in:n_in + len(outs)]
        rest = refs[n_in + len(outs):]
        acc_refs = rest[:len(acc_shapes)]
        scratch_refs = rest[len(acc_refs):]
        k = pl.program_id(2)

        def product(i):
            a = refs[2 * i][...]
            if a.ndim == 3:
                a = a.reshape(-1, a.shape[-1])
            return lax.dot_general(a.astype(BF16), refs[2 * i + 1][...].astype(BF16), _DNUMS[pairs[i][0]],
                                   preferred_element_type=F32)

        if nk == 1:
            epilogue([product(i) for i in range(npair)], extra_refs, out_refs, scratch_refs)
            return
        for i in range(npair):
            @pl.when(k == 0)
            def _(i=i):
                acc_refs[i][...] = product(i)

            @pl.when((k > 0) & (k < steps[i]))
            def _(i=i):
                acc_refs[i][...] += product(i)

        @pl.when(k == nk - 1)
        def _():
            epilogue([a[...] for a in acc_refs], extra_refs, out_refs, scratch_refs)

    ins, in_specs = [], []
    for p in pairs:
        ins += [p[1], p[3]]
        in_specs += [p[2], p[4]]
    ins += [e[0] for e in extras]
    in_specs += [e[1] for e in extras]
    aliases = {}
    if alias_to_out0 is not None:
        aliases = {len(ins): 0}
        ins.append(alias_to_out0)
        in_specs.append(pl.BlockSpec(memory_space=pl.ANY))
    if after is not None:
        ins.append(after)
        in_specs.append(pl.BlockSpec(memory_space=pl.ANY))
    scratch_shapes = [pltpu.VMEM(s, F32) for s in acc_shapes] + list(scratch)
    return pl.pallas_call(body, grid=grid, in_specs=in_specs, out_specs=[o[1] for o in outs], out_shape=[o[0] for o in outs],
                          scratch_shapes=scratch_shapes, input_output_aliases=aliases, compiler_params=_cp(3), name=name)(*ins)


def _my_index():
    return 4 * lax.axis_index("x") + 2 * lax.axis_index("y") + lax.axis_index("c")


def _peer(d):
    mx, my, mc = lax.axis_index("x"), lax.axis_index("y"), lax.axis_index("c")
    return (mx ^ ((d >> 2) & 1), my ^ ((d >> 1) & 1), mc ^ (d & 1))


def _win(ref, kind, j, n):
    if kind == "all":
        return ref
    if kind == "slot":
        return ref.at[j]
    if kind == "rows":
        return ref.at[pl.ds(pl.multiple_of(j * n, 8), n)]
    return ref.at[:, pl.ds(pl.multiple_of(j * n, 128), n)]


def _win7(ref, kind, n):
    if kind == "slot":
        return ref.at[pl.ds(0, 7)]
    if kind == "rows":
        return ref.at[pl.ds(0, 7 * n)]
    return ref.at[:, pl.ds(0, 7 * n)]


def _full_shape(shard_shape, kind):
    if kind == "slot":
        return (N_DEV,) + tuple(shard_shape)
    if kind == "rows":
        return (N_DEV * shard_shape[0],) + tuple(shard_shape[1:])
    return (shard_shape[0], N_DEV * shard_shape[1])


def _shard_shape(full_shape, kind, n):
    if kind == "all":
        return tuple(full_shape)
    if kind == "slot":
        return tuple(full_shape[1:])
    if kind == "rows":
        return (n,) + tuple(full_shape[1:])
    return (full_shape[0], n)


_HBM = pl.BlockSpec(memory_space=pltpu.HBM)
_SEM = pl.BlockSpec(memory_space=pltpu.SEMAPHORE)
_DATAFLOW = pltpu.SideEffectType.DATAFLOW_SIDE_EFFECTING


def _exchange_start(name, srcs, kinds, sizes, gather):
    n = len(srcs)
    if gather:
        lands = [lax.empty(_full_shape(s.shape, k), s.dtype) for s, k in zip(srcs, kinds)]
    else:
        lands = [lax.empty((N_DEV,) + _shard_shape(s.shape, k, z), s.dtype) for s, k, z in zip(srcs, kinds, sizes)]

    def body(*refs):
        src, land = refs[:n], refs[n:2 * n]
        send_sems, recv_sems, local_sems = refs[2 * n], refs[2 * n + 1], refs[2 * n + 2]
        token = refs[4 * n + 3]
        me = _my_index()
        for a in range(n):
            _local_copy(src[a], land[a], kinds[a], sizes[a], gather, me, local_sems.at[a]).start()
        for a in range(n):
            for d in range(1, N_DEV):
                px, py, pc = _peer(d)
                if gather:
                    s_ref, d_ref = src[a], _win(land[a], kinds[a], me, sizes[a])
                else:
                    s_ref, d_ref = _win(src[a], kinds[a], 4 * px + 2 * py + pc, sizes[a]), land[a].at[me]
                pltpu.make_async_remote_copy(src_ref=s_ref, dst_ref=d_ref, send_sem=send_sems.at[a], recv_sem=recv_sems.at[a],
                                             device_id=(px, py, pc), device_id_type=pl.DeviceIdType.MESH).start()
        token[...] = jnp.zeros_like(token)

    hbm = [pltpu.with_memory_space_constraint(a, pltpu.HBM) for a in list(srcs) + lands]
    out = pl.pallas_call(
        body, name=name, in_specs=[_HBM] * (2 * n),
        out_shape=[pltpu.SemaphoreType.DMA((n,))] * 3 + [pltpu.HBM(a.shape, a.dtype) for a in hbm] + [SDS((8, 128), F32)],
        out_specs=[_SEM] * 3 + [_HBM] * (2 * n) + [pl.BlockSpec(memory_space=pltpu.VMEM)],
        input_output_aliases={i: 3 + i for i in range(2 * n)},
        compiler_params=pltpu.CompilerParams(has_side_effects=_DATAFLOW))(*hbm)
    return out[0:3], out[3:3 + n], out[3 + n:3 + 2 * n], out[-1]


def _local_copy(src, land, kind, size, gather, me, sem):
    if gather:
        return pltpu.make_async_copy(src, _win(land, kind, me, size), sem)
    return pltpu.make_async_copy(_win(src, kind, me, size), land.at[me], sem)


def _exchange_wait(name, started, which, kinds, sizes, gather, after):
    sems, srcs, lands, _ = started
    n = len(which)

    def body(*refs):
        src, land = refs[:n], refs[n:2 * n]
        send_ref, recv_ref, local_ref = refs[2 * n:2 * n + 3]
        me = _my_index()
        my_id = (lax.axis_index("x"), lax.axis_index("y"), lax.axis_index("c"))
        for i, a in enumerate(which):
            seven = _win7(land[i], kinds[a], sizes[a]) if gather else land[i].at[pl.ds(0, 7)]
            pltpu.make_async_remote_copy(src_ref=seven, dst_ref=seven, send_sem=send_ref.at[a], recv_sem=recv_ref.at[a],
                                         device_id=my_id, device_id_type=pl.DeviceIdType.MESH).wait()
            _local_copy(src[i], land[i], kinds[a], sizes[a], gather, me, local_ref.at[a]).wait()

    hbm = [srcs[a] for a in which] + [lands[a] for a in which]
    out = pl.pallas_call(
        body, name=name, in_specs=[_HBM] * (2 * n) + [_SEM] * 3 + [pl.BlockSpec(memory_space=pl.ANY)],
        out_shape=[pltpu.HBM(a.shape, a.dtype) for a in hbm], out_specs=[_HBM] * (2 * n),
        input_output_aliases={i: i for i in range(2 * n)},
        compiler_params=pltpu.CompilerParams(has_side_effects=_DATAFLOW))(*hbm, *sems, after)
    return out[n:]


def _gather_small(small):
    def body(in_ref, out_ref, send_sem, recv_sem, local_sem):
        me = _my_index()
        my_id = (lax.axis_index("x"), lax.axis_index("y"), lax.axis_index("c"))
        cp = pltpu.make_async_copy(in_ref, out_ref.at[me], local_sem)
        cp.start()
        for d in range(1, N_DEV):
            pltpu.make_async_remote_copy(src_ref=in_ref, dst_ref=out_ref.at[me], send_sem=send_sem, recv_sem=recv_sem,
                                         device_id=_peer(d), device_id_type=pl.DeviceIdType.MESH).start()
        seven = out_ref.at[pl.ds(0, 7)]
        pltpu.make_async_remote_copy(src_ref=seven, dst_ref=seven, send_sem=send_sem, recv_sem=recv_sem, device_id=my_id,
                                     device_id_type=pl.DeviceIdType.MESH).wait()
        cp.wait()

    any_spec = pl.BlockSpec(memory_space=pl.ANY)
    return pl.pallas_call(body, in_specs=[any_spec], out_specs=any_spec, out_shape=SDS((N_DEV,) + small.shape, F32),
                          scratch_shapes=[pltpu.SemaphoreType.DMA] * 3, name="gather_small")(small)


def _adamw(name, recv, w, m, v):
    rows, cols = w.shape
    tr = max(c for c in range(16, 257, 16) if rows % c == 0) if rows % 16 == 0 else rows

    def body(r_ref, w_ref, m_ref, v_ref, g_ref, d_ref, nm_ref, nv_ref):
        g = r_ref[0].astype(F32)
        for s in range(1, N_DEV):
            g = g + r_ref[s].astype(F32)
        nm = ADAM_B1 * m_ref[...] + (1.0 - ADAM_B1) * g
        nv = ADAM_B2 * v_ref[...] + (1.0 - ADAM_B2) * (g * g)
        m_hat = nm / (1.0 - ADAM_B1 ** ADAM_STEP)
        v_hat = nv / (1.0 - ADAM_B2 ** ADAM_STEP)
        g_ref[...] = g
        d_ref[...] = -ADAM_LR * (m_hat / (jnp.sqrt(v_hat) + ADAM_EPS) + ADAM_WD * w_ref[...])
        nm_ref[...] = nm
        nv_ref[...] = nv

    blk = _bs((tr, cols), lambda i: (i, 0))
    return pl.pallas_call(
        body, grid=(rows // tr,), in_specs=[_bs((N_DEV, tr, cols), lambda i: (0, i, 0)), blk, blk, blk],
        out_specs=[blk] * 4, out_shape=[SDS((rows, cols), F32)] * 4, compiler_params=_cp(1), name=name)(recv, w, m, v)


def _rms_fwd(name, x, g, tm):
    t, d = x.shape

    def body(x_ref, g_ref, n_ref):
        xv = x_ref[...]
        r = lax.rsqrt(jnp.mean(xv * xv, axis=-1, keepdims=True) + EPS)
        n_ref[...] = (xv * r * g_ref[...]).astype(BF16)

    return pl.pallas_call(body, grid=(t // tm,), in_specs=[_bs((tm, d), lambda i: (i, 0)), _bs((1, d), lambda i: (0, 0))],
                          out_specs=_bs((tm, d), lambda i: (i, 0)), out_shape=SDS((t, d), BF16), compiler_params=_cp(1),
                          name=name)(x, g)


def _accumulate_rows(ref, part):
    @pl.when(pl.program_id(0) == 0)
    def _():
        ref[...] = part

    @pl.when(pl.program_id(0) > 0)
    def _():
        ref[...] += part


def _rms_bwd_epilogue(prods, extra_refs, out_refs, scratch_refs):
    dyv = prods[0]
    for p in prods[1:]:
        dyv = dyv + p
    if len(extra_refs) > 3:
        dyv = dyv + extra_refs[3][...]
    xv = extra_refs[0][...]
    r = lax.rsqrt(jnp.mean(xv * xv, axis=-1, keepdims=True) + EPS)
    xh = xv * r
    dxh = dyv * extra_refs[1][...]
    dx = extra_refs[2][...] + r * (dxh - xh * jnp.mean(dxh * xh, axis=-1, keepdims=True))
    for o in out_refs[:-1]:
        o[...] = dx.astype(o.dtype)
    _accumulate_rows(out_refs[-1], jnp.sum(dyv * xh, axis=0, keepdims=True))


def _out_norm_epilogue(prods, extra_refs, out_refs, scratch_refs):
    h = prods[0] + extra_refs[0][...]
    r = lax.rsqrt(jnp.mean(h * h, axis=-1, keepdims=True) + EPS)
    out_refs[0][...] = h
    out_refs[1][...] = (h * r * extra_refs[1][...]).astype(BF16)


def _glu_merge_epilogue(prods, extra_refs, out_refs, scratch_refs):
    ya, yb, ad = prods
    ga, gs = extra_refs[0][...].astype(F32), extra_refs[1][...].astype(F32)
    m = _sigmoid(ga) * ad + _sigmoid(gs) * (ya * _sigmoid(yb))
    out_refs[0][...] = m.astype(BF16)
    for o, val in zip(out_refs[1:], (ya, yb, ad)):
        o[...] = val.astype(o.dtype)


def _merge_bwd_epilogue(prods, extra_refs, out_refs, scratch_refs):
    dmv = prods[0]
    d = dmv.shape[1]
    ga, gs = _sigmoid(extra_refs[0][...].astype(F32)), _sigmoid(extra_refs[1][...].astype(F32))
    adv, yav = extra_refs[2][...].astype(F32), extra_refs[3][...].astype(F32)
    sb = _sigmoid(extra_refs[4][...].astype(F32))
    out_refs[0][:, 0:d] = (dmv * adv * ga * (1.0 - ga)).astype(BF16)
    out_refs[0][:, d:2 * d] = (dmv * (yav * sb) * gs * (1.0 - gs)).astype(BF16)
    dad = (dmv * ga).astype(BF16)
    dsd = dmv * gs
    dya = (dsd * sb).astype(BF16)
    dyb = (dsd * yav * sb * (1.0 - sb)).astype(BF16)
    out_refs[1][...], out_refs[2][...], out_refs[3][...] = dad, dya, dyb
    nt = _DNUMS["nt"]
    out_refs[4][...] = (lax.dot_general(dya, extra_refs[5][...], nt, preferred_element_type=F32)
                        + lax.dot_general(dyb, extra_refs[6][...], nt, preferred_element_type=F32))
    out_refs[5][...] = lax.dot_general(dad, extra_refs[7][...], nt, preferred_element_type=F32)


def _swiglu_epilogue(prods, extra_refs, out_refs, scratch_refs):
    gv, uv = prods
    out_refs[0][...] = (gv * _sigmoid(gv) * uv).astype(BF16)
    out_refs[1][...] = gv.astype(out_refs[1].dtype)
    out_refs[2][...] = uv.astype(out_refs[2].dtype)


def _swiglu_bwd_epilogue(prods, extra_refs, out_refs, scratch_refs):
    dav = prods[0]
    gv, uv = extra_refs[0][...].astype(F32), extra_refs[1][...].astype(F32)
    sg = _sigmoid(gv)
    out_refs[0][...] = (dav * uv * sg * (1.0 + gv * (1.0 - sg))).astype(BF16)
    out_refs[1][...] = (dav * gv * sg).astype(BF16)


def _head_epilogue(n_tiles):
    def epilogue(prods, extra_refs, out_refs, scratch_refs):
        h2 = prods[0] + extra_refs[0][...]
        h2_bf = h2.astype(BF16)
        out_refs[6][...] = h2_bf
        pgv = jnp.dot(h2_bf, extra_refs[3][...], preferred_element_type=F32)
        ppv = prods[1]
        d = pgv.shape[1]
        lacc = scratch_refs[0]
        sg = _sigmoid(pgv)
        h3 = h2 + sg * ppv
        r = lax.rsqrt(jnp.mean(h3 * h3, axis=-1, keepdims=True) + EPS)
        xh = h3 * r
        gv = extra_refs[1][...]
        diff = xh * gv - extra_refs[2][...]
        dout = diff * (1.0 / d)
        dxh = dout * gv
        dh3 = r * (dxh - xh * jnp.mean(dxh * xh, axis=-1, keepdims=True))
        dpg = (dh3 * ppv * sg * (1.0 - sg)).astype(BF16)
        dh2 = dh3 + lax.dot_general(dpg, extra_refs[3][...], _DNUMS["nt"], preferred_element_type=F32)
        out_refs[2][...] = dh2
        out_refs[3][...] = dh2.astype(BF16)
        out_refs[4][...] = (dh3 * sg).astype(BF16)
        out_refs[5][...] = dpg
        _accumulate_rows(out_refs[1], jnp.sum(dout * xh, axis=0, keepdims=True))
        _accumulate_rows(lacc, jnp.sum(diff * diff, axis=0, keepdims=True))

        @pl.when(pl.program_id(0) == n_tiles - 1)
        def _():
            out_refs[0][...] = (0.5 / d) * jnp.sum(lacc[...], axis=-1, keepdims=True)

    return epilogue


def _strided(r, n, d):
    return pl.ds(r, n, stride=d) if d > 1 else pl.ds(0, n)


def _rope_tables(pos_ref, invf_ref, c_s, s_s):
    ang = pos_ref[...].astype(F32) * invf_ref[...]
    lane = lax.broadcasted_iota(jnp.int32, ang.shape, 1)
    sn = jnp.sin(ang)
    c_s[...] = jnp.where(lane < ROPE_DIM, jnp.cos(ang), 1.0)
    s_s[...] = jnp.where(lane < ROPE_HALF, -sn, jnp.where(lane < ROPE_DIM, sn, 0.0))


def _rope_partner(xv, first_half):
    return jnp.where(first_half, pltpu.roll(xv, HEAD_DIM - ROPE_HALF, 1), pltpu.roll(xv, ROPE_HALF, 1))


def _rope_dilate_epilogue(tm):
    def epilogue(prods, extra_refs, out_refs, scratch_refs):
        zv = prods[0]
        pos_ref, invf_ref = extra_refs
        c_s, s_s, rot = scratch_refs
        c = pl.program_id(1)

        @pl.when(c == 0)
        def _():
            _rope_tables(pos_ref, invf_ref, c_s, s_s)

        @pl.when(c < 2)
        def _():
            cc, ss = c_s[...], s_s[...]
            first_half = lax.broadcasted_iota(jnp.int32, cc.shape, 1) < ROPE_HALF
            for h in range(QK_W // HEAD_DIM):
                xv = zv[:, h * HEAD_DIM:(h + 1) * HEAD_DIM]
                rot[h] = xv * cc + _rope_partner(xv, first_half) * ss

        @pl.when(c == 2)
        def _():
            for h in range(QK_W // HEAD_DIM):
                rot[h] = zv[:, h * HEAD_DIM:(h + 1) * HEAD_DIM]

        for g, (d, o_ref) in enumerate(zip(DILATIONS, out_refs)):
            n = tm // d
            for r in range(d):
                for hh in range(HEADS_PER_GROUP):
                    oc = r * GROUP_W + hh * HEAD_DIM
                    o_ref[:, oc:oc + HEAD_DIM] = rot[g * HEADS_PER_GROUP + hh, _strided(r, n, d), :].astype(BF16)

    return epilogue


def _band_masks(first_tile):
    qi = lax.broadcasted_iota(jnp.int32, (BLK, 2 * BLK), 0)
    kj = lax.broadcasted_iota(jnp.int32, (BLK, 2 * BLK), 1)
    band = (kj >= qi) & (kj <= qi + BLK)
    return band, band & ((kj >= BLK) | jnp.logical_not(first_tile))


def _attn_fwd(qkv, d, qt):
    ell = qkv.shape[1]
    nsub = qt // BLK
    scale = 1.0 / math.sqrt(HEAD_DIM)

    def body(q_ref, kc_ref, kp_ref, vc_ref, vp_ref, o_ref, lse_ref, kcat, vcat):
        nb = pl.program_id(1)
        kcat[0:BLK, :] = kp_ref[...]
        kcat[BLK:, :] = kc_ref[...]
        vcat[0:BLK, :] = vp_ref[...]
        vcat[BLK:, :] = vc_ref[...]
        lane = lax.broadcasted_iota(jnp.int32, (BLK, HEAD_DIM), 1)
        band, band_first = _band_masks(nb == 0)
        for b in range(nsub):
            valid = band_first if b == 0 else band
            lse_t = jnp.zeros((BLK, HEAD_DIM), F32)
            for hh in range(HEADS_PER_GROUP):
                cs = slice(hh * HEAD_DIM, (hh + 1) * HEAD_DIM)
                qb = q_ref[b * BLK:(b + 1) * BLK, cs]
                kk = kcat[b * BLK:(b + 2) * BLK, cs]
                vv = vcat[b * BLK:(b + 2) * BLK, cs]
                s = lax.dot_general(qb, kk, _DNUMS["nt"], preferred_element_type=F32) * scale
                s = jnp.where(valid, s, NEG)
                mx = jnp.max(s, axis=-1, keepdims=True)
                p = jnp.exp(s - mx)
                den = jnp.sum(p, axis=-1, keepdims=True)
                o = jnp.dot(p.astype(BF16), vv, preferred_element_type=F32) / den
                o_ref[b * BLK:(b + 1) * BLK, cs] = o
                lse_t = jnp.where(lane == hh, mx + jnp.log(den), lse_t)
            lse_ref[b * BLK:(b + 1) * BLK, :] = lse_t

    cur = lambda c: _bs((None, qt, GROUP_W), lambda r, nb: (c, nb, r))
    prev = lambda c: _bs((None, BLK, GROUP_W), lambda r, nb: (c, jnp.maximum(nb * nsub - 1, 0), r))
    return pl.pallas_call(
        body, grid=(d, ell // qt), in_specs=[cur(0), cur(1), prev(1), cur(2), prev(2)],
        out_specs=[_bs((qt, GROUP_W), lambda r, nb: (nb, r)), _bs((None, qt, HEAD_DIM), lambda r, nb: (r, nb, 0))],
        out_shape=[SDS((ell, d * GROUP_W), F32), SDS((d, ell, HEAD_DIM), F32)],
        scratch_shapes=[pltpu.VMEM((qt + BLK, GROUP_W), BF16)] * 2, compiler_params=_cp(2), name=f"attn_fwd_d{d}")(
            qkv, qkv, qkv, qkv, qkv)


def _attn_merge(outs, lses, tm):
    t = outs[0].shape[0]

    def body(o0, o1, o2, l0, l1, l2, attn_ref, attn_bf_ref, t0, t1, t2, so, sl, lt_s):
        for g, (d, o_ref, l_ref) in enumerate(zip(DILATIONS, (o0, o1, o2), (l0, l1, l2))):
            n = tm // d
            for r in range(d):
                rows = _strided(r, n, d)
                for hh in range(HEADS_PER_GROUP):
                    oc = r * GROUP_W + hh * HEAD_DIM
                    so[g * HEADS_PER_GROUP + hh, rows, :] = o_ref[:, oc:oc + HEAD_DIM]
                sl[g, rows, :] = l_ref[r]
        ls = [sl[g] for g in range(N_GROUPS)]
        mx = jnp.maximum(jnp.maximum(ls[0], ls[1]), ls[2])
        es = [jnp.exp(l - mx) for l in ls]
        den = es[0] + es[1] + es[2]
        ws = [e / den for e in es]
        lt_s[...] = mx + jnp.log(den)
        for hh in range(HEADS_PER_GROUP):
            cs = slice(hh * HEAD_DIM, (hh + 1) * HEAD_DIM)
            a = ws[0][:, hh:hh + 1] * so[hh]
            for g in range(1, N_GROUPS):
                a = a + ws[g][:, hh:hh + 1] * so[g * HEADS_PER_GROUP + hh]
            attn_ref[:, cs] = a
            attn_bf_ref[:, cs] = a.astype(BF16)
        for d, t_ref in zip(DILATIONS, (t0, t1, t2)):
            n = tm // d
            for r in range(d):
                t_ref[r] = lt_s[_strided(r, n, d), :]

    dil = lambda d: _bs((tm // d, d * GROUP_W), lambda i: (i, 0))
    lsp = lambda d: _bs((d, tm // d, HEAD_DIM), lambda i: (0, i, 0))
    row = _bs((tm, GROUP_W), lambda i: (i, 0))
    return pl.pallas_call(
        body, grid=(t // tm,),
        in_specs=[dil(d) for d in DILATIONS] + [lsp(d) for d in DILATIONS],
        out_specs=[row, row] + [lsp(d) for d in DILATIONS],
        out_shape=[SDS((t, GROUP_W), F32), SDS((t, GROUP_W), BF16)] + [SDS(l.shape, F32) for l in lses],
        scratch_shapes=[pltpu.VMEM((N_GROUPS * HEADS_PER_GROUP, tm, HEAD_DIM), F32), pltpu.VMEM((N_GROUPS, tm, HEAD_DIM), F32),
                        pltpu.VMEM((tm, HEAD_DIM), F32)],
        compiler_params=_cp(1), name="attn_merge")(*outs, *lses)


def _attn_bwd_pre(d_attn, attn, tm):
    t = attn.shape[0]

    def body(da_ref, a_ref, g0, g1, g2, e0, e1, e2, dl_s, da_s):
        lane = lax.broadcasted_iota(jnp.int32, (tm, HEAD_DIM), 1)
        dl = jnp.zeros((tm, HEAD_DIM), F32)
        for hh in range(HEADS_PER_GROUP):
            cs = slice(hh * HEAD_DIM, (hh + 1) * HEAD_DIM)
            dav = da_ref[:, cs]
            da_s[hh] = dav
            dl = jnp.where(lane == hh, jnp.sum(dav * a_ref[:, cs], axis=-1, keepdims=True), dl)
        dl_s[...] = dl
        for d, g_ref, e_ref in zip(DILATIONS, (g0, g1, g2), (e0, e1, e2)):
            n = tm // d
            for r in range(d):
                rows = _strided(r, n, d)
                for hh in range(HEADS_PER_GROUP):
                    oc = r * GROUP_W + hh * HEAD_DIM
                    g_ref[:, oc:oc + HEAD_DIM] = da_s[hh, rows, :].astype(BF16)
                e_ref[r] = dl_s[rows, :]

    row = _bs((tm, GROUP_W), lambda i: (i, 0))
    return pl.pallas_call(
        body, grid=(t // tm,), in_specs=[row, row],
        out_specs=[_bs((tm // d, d * GROUP_W), lambda i: (i, 0)) for d in DILATIONS]
        + [_bs((d, tm // d, HEAD_DIM), lambda i: (0, i, 0)) for d in DILATIONS],
        out_shape=[SDS((t // d, d * GROUP_W), BF16) for d in DILATIONS]
        + [SDS((d, t // d, HEAD_DIM), F32) for d in DILATIONS],
        scratch_shapes=[pltpu.VMEM((tm, HEAD_DIM), F32), pltpu.VMEM((HEADS_PER_GROUP, tm, HEAD_DIM), F32)],
        compiler_params=_cp(1), name="attn_bwd_pre")(d_attn, attn)


def _attn_bwd(qkv, d_a, lt, delta, d, qt):
    ell = qkv.shape[1]
    nsub = qt // BLK
    ntile = ell // qt
    nblk = ell // BLK
    scale = 1.0 / math.sqrt(HEAD_DIM)

    def body(q_ref, qn_ref, kc_ref, kp_ref, vc_ref, vp_ref, da_ref, dan_ref, lt_ref, ltn_ref, dl_ref, dln_ref, o_ref,
             kcat, vcat, dk_acc, dv_acc):
        nb = pl.program_id(1)
        kcat[0:BLK, :] = kp_ref[...]
        kcat[BLK:, :] = kc_ref[...]
        vcat[0:BLK, :] = vp_ref[...]
        vcat[BLK:, :] = vc_ref[...]
        qi = lax.broadcasted_iota(jnp.int32, (BLK, BLK), 0)
        kj = lax.broadcasted_iota(jnp.int32, (BLK, BLK), 1)
        valid_next = (kj >= qi) & (nb < ntile - 1)
        band, band_first = _band_masks(nb == 0)
        for hh in range(HEADS_PER_GROUP):
            cs = slice(hh * HEAD_DIM, (hh + 1) * HEAD_DIM)
            dk_acc[...] = jnp.zeros_like(dk_acc)
            dv_acc[...] = jnp.zeros_like(dv_acc)
            for b in range(nsub):
                rs = slice(b * BLK, (b + 1) * BLK)
                ks = slice(b * BLK, (b + 2) * BLK)
                valid = band_first if b == 0 else band
                qb, kk, vv, dab = q_ref[rs, cs], kcat[ks, cs], vcat[ks, cs], da_ref[rs, cs]
                s = lax.dot_general(qb, kk, _DNUMS["nt"], preferred_element_type=F32) * scale
                p = jnp.where(valid, jnp.exp(s - lt_ref[rs, hh:hh + 1]), 0.0)
                dp = lax.dot_general(dab, vv, _DNUMS["nt"], preferred_element_type=F32)
                ds = (p * (dp - dl_ref[rs, hh:hh + 1])).astype(BF16)
                o_ref[0, rs, cs] = jnp.dot(ds, kk, preferred_element_type=F32) * scale
                dk_acc[ks, :] += lax.dot_general(ds, qb, _DNUMS["tn"], preferred_element_type=F32) * scale
                dv_acc[ks, :] += lax.dot_general(p.astype(BF16), dab, _DNUMS["tn"], preferred_element_type=F32)
            ks = slice(nsub * BLK, (nsub + 1) * BLK)
            qn, kl, vl, dan = qn_ref[:, cs], kcat[ks, cs], vcat[ks, cs], dan_ref[:, cs]
            s = lax.dot_general(qn, kl, _DNUMS["nt"], preferred_element_type=F32) * scale
            p = jnp.where(valid_next, jnp.exp(s - ltn_ref[:, hh:hh + 1]), 0.0)
            dp = lax.dot_general(dan, vl, _DNUMS["nt"], preferred_element_type=F32)
            ds = (p * (dp - dln_ref[:, hh:hh + 1])).astype(BF16)
            dk_acc[ks, :] += lax.dot_general(ds, qn, _DNUMS["tn"], preferred_element_type=F32) * scale
            dv_acc[ks, :] += lax.dot_general(p.astype(BF16), dan, _DNUMS["tn"], preferred_element_type=F32)
            o_ref[1, :, cs] = dk_acc[BLK:, :]
            o_ref[2, :, cs] = dv_acc[BLK:, :]

    nxt = lambda nb: jnp.minimum((nb + 1) * nsub, nblk - 1)
    prv = lambda nb: jnp.maximum(nb * nsub - 1, 0)
    cur3 = lambda c: _bs((None, qt, GROUP_W), lambda r, nb: (c, nb, r))
    in_specs = [
        cur3(0), _bs((None, BLK, GROUP_W), lambda r, nb: (0, nxt(nb), r)),
        cur3(1), _bs((None, BLK, GROUP_W), lambda r, nb: (1, prv(nb), r)),
        cur3(2), _bs((None, BLK, GROUP_W), lambda r, nb: (2, prv(nb), r)),
        _bs((qt, GROUP_W), lambda r, nb: (nb, r)), _bs((BLK, GROUP_W), lambda r, nb: (nxt(nb), r)),
        _bs((None, qt, HEAD_DIM), lambda r, nb: (r, nb, 0)), _bs((None, BLK, HEAD_DIM), lambda r, nb: (r, nxt(nb), 0)),
        _bs((None, qt, HEAD_DIM), lambda r, nb: (r, nb, 0)), _bs((None, BLK, HEAD_DIM), lambda r, nb: (r, nxt(nb), 0)),
    ]
    return pl.pallas_call(
        body, grid=(d, ntile), in_specs=in_specs, out_specs=_bs((3, qt, GROUP_W), lambda r, nb: (0, nb, r)),
        out_shape=SDS((3, ell, d * GROUP_W), F32),
        scratch_shapes=[pltpu.VMEM((qt + BLK, GROUP_W), BF16)] * 2 + [pltpu.VMEM((qt + BLK, HEAD_DIM), F32)] * 2,
        compiler_params=_cp(2), name=f"attn_bwd_d{d}")(qkv, qkv, qkv, qkv, qkv, qkv, d_a, d_a, lt, lt, delta, delta)


def _undilate_rope_bwd(dqkvs, pos, invf, tm):
    t = pos.shape[0]

    def body(g0, g1, g2, pos_ref, invf_ref, o_ref, c_s, s_s, nat):
        c = pl.program_id(1)

        @pl.when(c == 0)
        def _():
            _rope_tables(pos_ref, invf_ref, c_s, s_s)

        for g, (d, g_ref) in enumerate(zip(DILATIONS, (g0, g1, g2))):
            n = tm // d
            for r in range(d):
                for hh in range(HEADS_PER_GROUP):
                    oc = r * GROUP_W + hh * HEAD_DIM
                    nat[g * HEADS_PER_GROUP + hh, _strided(r, n, d), :] = g_ref[:, oc:oc + HEAD_DIM]

        @pl.when(c < 2)
        def _():
            cc, ss = c_s[...], s_s[...]
            first_half = lax.broadcasted_iota(jnp.int32, cc.shape, 1) < ROPE_HALF
            for h in range(QK_W // HEAD_DIM):
                xv = nat[h]
                y = xv * cc - _rope_partner(xv, first_half) * ss
                o_ref[:, h * HEAD_DIM:(h + 1) * HEAD_DIM] = y.astype(BF16)

        @pl.when(c == 2)
        def _():
            for h in range(QK_W // HEAD_DIM):
                o_ref[:, h * HEAD_DIM:(h + 1) * HEAD_DIM] = nat[h].astype(BF16)

    return pl.pallas_call(
        body, grid=(t // tm, 3),
        in_specs=[_bs((None, tm // d, d * GROUP_W), lambda i, c: (c, i, 0)) for d in DILATIONS]
        + [_bs((tm, 1), lambda i, c: (i, 0)), _bs((1, HEAD_DIM), lambda i, c: (0, 0))],
        out_specs=_bs((tm, QK_W), lambda i, c: (i, c)), out_shape=SDS((t, 3 * QK_W), BF16),
        scratch_shapes=[pltpu.VMEM((tm, HEAD_DIM), F32)] * 2 + [pltpu.VMEM((QK_W // HEAD_DIM, tm, HEAD_DIM), F32)],
        compiler_params=_cp(2), name="undilate_rope_bwd")(*dqkvs, pos, invf)


def _cmul(ar, ai, br, bi):
    return ar * br - ai * bi, ar * bi + ai * br


def _ssm_disc(a_re, a_im, log_dt, nsq):
    def body(lr_ref, li_ref, ldt_ref, br_ref, bi_ref, zr_ref, zi_ref, pr_ref, pi_ref):
        lr, li = lr_ref[...], li_ref[...]
        dt = jnp.exp(ldt_ref[...])
        mag = jnp.exp(lr * dt)
        bar_re, bar_im = mag * jnp.cos(li * dt), mag * jnp.sin(li * dt)
        nr, ni = bar_re - 1.0, bar_im
        den = lr * lr + li * li
        br_ref[...], bi_ref[...] = bar_re, bar_im
        zr_ref[...] = (nr * lr + ni * li) / den
        zi_ref[...] = (ni * lr - nr * li) / den
        pr, pi = bar_re, bar_im
        for _ in range(nsq):
            pr, pi = _cmul(pr, pi, pr, pi)
        pr_ref[...], pi_ref[...] = pr, pi

    return pl.pallas_call(body, out_shape=[SDS(a_re.shape, F32)] * 6, name="ssm_discretise")(a_re, a_im, log_dt)


def _ssm_scale_b(z_re, z_im, b_re, b_im):
    def body(zr_ref, zi_ref, br_ref, bi_ref, or_ref, oi_ref):
        zr, zi, br, bi = zr_ref[...], zi_ref[...], br_ref[...], bi_ref[...]
        or_ref[...] = zr * br - zi * bi
        oi_ref[...] = zr * bi + zi * br

    return pl.pallas_call(body, out_shape=[SDS(b_re.shape, F32)] * 2, name="ssm_scale_b")(z_re, z_im, b_re, b_im)


def _ssm_scale_b_bwd(z_re, z_im, b_re, b_im, g_re, g_im):
    def body(zr_ref, zi_ref, br_ref, bi_ref, gr_ref, gi_ref, dbr_ref, dbi_ref, dzr_ref, dzi_ref):
        zr, zi, br, bi, gr, gi = zr_ref[...], zi_ref[...], br_ref[...], bi_ref[...], gr_ref[...], gi_ref[...]
        dbr_ref[...] = zr * gr + zi * gi
        dbi_ref[...] = zr * gi - zi * gr
        dzr_ref[...] = jnp.sum(br * gr + bi * gi, axis=-1, keepdims=True)
        dzi_ref[...] = jnp.sum(br * gi - bi * gr, axis=-1, keepdims=True)

    return pl.pallas_call(body, out_shape=[SDS(b_re.shape, F32)] * 2 + [SDS(z_re.shape, F32)] * 2,
                          name="ssm_scale_b_bwd")(z_re, z_im, b_re, b_im, g_re, g_im)


def _ssm_disc_bwd(a_re, a_im, log_dt, gb_re, gb_im, gz_re, gz_im):
    def body(lr_ref, li_ref, ldt_ref, gbr_ref, gbi_ref, gzr_ref, gzi_ref, dar_ref, dai_ref, dldt_ref):
        lr, li = lr_ref[...], li_ref[...]
        dt = jnp.exp(ldt_ref[...])
        mag = jnp.exp(lr * dt)
        bar_re, bar_im = mag * jnp.cos(li * dt), mag * jnp.sin(li * dt)
        nr, ni = bar_re - 1.0, bar_im
        den = lr * lr + li * li
        zr, zi = (nr * lr + ni * li) / den, (ni * lr - nr * li) / den
        gzr, gzi = gzr_ref[...], gzi_ref[...]
        gbr = gbr_ref[...] + (lr * gzr - li * gzi) / den
        gbi = gbi_ref[...] + (lr * gzi + li * gzr) / den
        qr, qi = (zr * lr + zi * li) / den, (zi * lr - zr * li) / den
        dar_ref[...] = dt * (bar_re * gbr + bar_im * gbi) - qr * gzr - qi * gzi
        dai_ref[...] = dt * (bar_re * gbi - bar_im * gbr) - qr * gzi + qi * gzr
        wr, wi = lr * bar_re - li * bar_im, lr * bar_im + li * bar_re
        dldt_ref[...] = dt * jnp.sum(wr * gbr + wi * gbi, axis=-1, keepdims=True)

    return pl.pallas_call(body, out_shape=[SDS(a_re.shape, F32)] * 2 + [SDS(log_dt.shape, F32)],
                          name="ssm_discretise_bwd")(a_re, a_im, log_dt, gb_re, gb_im, gz_re, gz_im)


def _interleave_epilogue(prods, extra_refs, out_refs, scratch_refs):
    uv = prods[0]
    tmp = scratch_refs[0]
    n = uv.shape[0] // N_DEV
    for b in range(SSM_W // BLK):
        cs = slice(b * BLK, (b + 1) * BLK)
        for j in range(N_DEV):
            tmp[b, pl.ds(j, n, stride=N_DEV), :] = uv[j * n:(j + 1) * n, cs]
        out_refs[0][:, cs] = tmp[b]
        out_refs[1][:, cs] = tmp[b].astype(BF16)


def _drive(src_ref, mat_ref, dst, mode):
    for kn in range(2 * SSM_NB):
        n = kn % SSM_NB
        a = src_ref[:, n * BLK:(n + 1) * BLK]
        dst[:, kn * 512:(kn + 1) * 512] = lax.dot_general(a, mat_ref[kn], _DNUMS[mode], preferred_element_type=F32)


def _scan_chunk(src, lam_ref, carry, *, reverse, store=None, h_ref=None, acc=None):
    steps = src.shape[0] // 8
    for c in range(NSTATE // SCAN_LANES):
        re = slice(c * SCAN_LANES, (c + 1) * SCAN_LANES)
        im = slice(NSTATE + c * SCAN_LANES, NSTATE + (c + 1) * SCAN_LANES)
        ar, ai = lam_ref[:, re], lam_ref[:, im]

        def step(s, val):
            i = (steps - 1 - s) if reverse else s
            rows = pl.ds(pl.multiple_of(i * 8, 8), 8)
            if acc is not None:
                hr, hi, dr, di = val
                pr, pi = h_ref[rows, re], h_ref[rows, im]
                dr = dr + hr * pr + hi * pi
                di = di + hi * pr - hr * pi
            else:
                hr, hi = val
            nr = ar * hr - ai * hi + src[rows, re]
            ni = ar * hi + ai * hr + src[rows, im]
            if store is not None:
                store[rows, re] = nr
                store[rows, im] = ni
            return (nr, ni, dr, di) if acc is not None else (nr, ni)

        init = (carry[:, re], carry[:, im])
        if acc is not None:
            init = init + (acc[:, re], acc[:, im])
        out = lax.fori_loop(0, steps, step, init, unroll=4)
        carry[:, re], carry[:, im] = out[0], out[1]
        if acc is not None:
            acc[:, re], acc[:, im] = out[2], out[3]


def _segment_carries(e_ref, pw_ref, out_ref, reverse):
    pr, pi = pw_ref[:, 0:NSTATE], pw_ref[:, NSTATE:]
    hr = jnp.zeros((1, NSTATE), F32)
    hi = jnp.zeros((1, NSTATE), F32)
    order = range(N_DEV - 1, -1, -1) if reverse else range(N_DEV)
    for j in order:
        out_ref[j:j + 1, 0:NSTATE] = hr
        out_ref[j:j + 1, NSTATE:] = hi
        tr, ti = _cmul(pr, pi, hr, hi)
        hr, hi = e_ref[j:j + 1, 0:NSTATE] + tr, e_ref[j:j + 1, NSTATE:] + ti


def _ssm_carries(name, src, mat, mode, lam8, pw, reverse):
    t = src.shape[0]
    nchunk = t // SCAN_ROWS

    def body(src_ref, mat_ref, lam_ref, pw_ref, out_ref, drive, carry):
        c = pl.program_id(0)

        @pl.when(c == 0)
        def _():
            carry[...] = jnp.zeros_like(carry)

        _drive(src_ref, mat_ref, drive, mode)
        _scan_chunk(drive, lam_ref, carry, reverse=reverse)

        @pl.when(c == nchunk - 1)
        def _():
            _segment_carries(carry, pw_ref, out_ref, reverse)

    blk = (lambda c: (nchunk - 1 - c, 0)) if reverse else (lambda c: (c, 0))
    return pl.pallas_call(
        body, grid=(nchunk,),
        in_specs=[_bs((SCAN_ROWS, SSM_W), blk), _bs(mat.shape, lambda c: (0, 0, 0)), _bs((8, 2 * NSTATE), lambda c: (0, 0)),
                  _bs((1, 2 * NSTATE), lambda c: (0, 0))],
        out_specs=_bs((8, 2 * NSTATE), lambda c: (0, 0)), out_shape=SDS((8, 2 * NSTATE), F32),
        scratch_shapes=[pltpu.VMEM((SCAN_ROWS, 2 * NSTATE), F32), pltpu.VMEM((8, 2 * NSTATE), F32)],
        compiler_params=_cp(1), name=name)(src, mat, lam8, pw)


def _ssm_fwd(u_bf, u, d_skip, bd, cd, lam8, start):
    t = u_bf.shape[0]
    nchunk = t // SCAN_ROWS
    per_seg = SCAN_ROWS // N_DEV

    def body(ub_ref, u_ref, d_ref, bd_ref, cd_ref, lam_ref, start_ref, h_ref, ys_ref, yg_ref, drive, carry, tmp):
        @pl.when(pl.program_id(0) == 0)
        def _():
            carry[...] = start_ref[...]

        _drive(ub_ref, bd_ref, drive, "nn")
        _scan_chunk(drive, lam_ref, carry, reverse=False, store=h_ref)
        for n in range(SSM_NB):
            cs = slice(n * BLK, (n + 1) * BLK)
            hr = h_ref[:, n * 512:(n + 1) * 512].astype(BF16)
            hi = h_ref[:, NSTATE + n * 512:NSTATE + (n + 1) * 512].astype(BF16)
            ys = (jnp.dot(hr, cd_ref[n], preferred_element_type=F32) + jnp.dot(hi, cd_ref[SSM_NB + n], preferred_element_type=F32)
                  + d_ref[:, cs] * u_ref[:, cs])
            ys_ref[:, cs] = ys
            tmp[n] = _gelu_parts(ys)[0]
            for j in range(N_DEV):
                yg_ref[j, :, cs] = tmp[n, pl.ds(j, per_seg, stride=N_DEV), :].astype(BF16)

    row = _bs((SCAN_ROWS, SSM_W), lambda c: (c, 0))
    h, ys, yg = pl.pallas_call(
        body, grid=(nchunk,),
        in_specs=[row, row, _bs((1, SSM_W), lambda c: (0, 0)), _bs(bd.shape, lambda c: (0, 0, 0)), _bs(cd.shape, lambda c: (0, 0, 0)),
                  _bs((8, 2 * NSTATE), lambda c: (0, 0)), _bs((8, 2 * NSTATE), lambda c: (0, 0))],
        out_specs=[_bs((SCAN_ROWS, 2 * NSTATE), lambda c: (c, 0)), row, _bs((N_DEV, per_seg, SSM_W), lambda c: (0, c, 0))],
        out_shape=[SDS((t, 2 * NSTATE), F32), SDS((t, SSM_W), F32), SDS((N_DEV, t // N_DEV, SSM_W), BF16)],
        scratch_shapes=[pltpu.VMEM((SCAN_ROWS, 2 * NSTATE), F32), pltpu.VMEM((8, 2 * NSTATE), F32),
                        pltpu.VMEM((SSM_NB, SCAN_ROWS, BLK), F32)],
        compiler_params=_cp(1), name="ssm_scan_fwd")(u_bf, u, d_skip, bd, cd, lam8, start)
    return h, ys, yg.reshape(t, SSM_W)


def _ssm_bwd(dys_bf, dys, d_skip, u_bf, h, bd, cd, lamc8, start):
    t = u_bf.shape[0]
    nchunk = t // SCAN_ROWS
    per_seg = SCAN_ROWS // N_DEV

    def body(dys_ref, dysf_ref, d_ref, u_ref, h_ref, bd_ref, cd_ref, lam_ref, start_ref, du_ref, dlam_ref, dbd_ref, dcd_ref,
             drive, adj, carry, tmp):
        c = pl.program_id(0)

        @pl.when(c == 0)
        def _():
            carry[...] = start_ref[...]
            dlam_ref[...] = jnp.zeros_like(dlam_ref)
            dbd_ref[...] = jnp.zeros_like(dbd_ref)
            dcd_ref[...] = jnp.zeros_like(dcd_ref)

        _drive(dys_ref, cd_ref, drive, "nt")
        _scan_chunk(drive, lam_ref, carry, reverse=True, store=adj, h_ref=h_ref, acc=dlam_ref)
        for n in range(SSM_NB):
            cs = slice(n * BLK, (n + 1) * BLK)
            acc = None
            for k in range(2):
                kn = k * SSM_NB + n
                ss = slice(kn * 512, (kn + 1) * 512)
                lam_b = adj[:, ss].astype(BF16)
                part = lax.dot_general(lam_b, bd_ref[kn], _DNUMS["nt"], preferred_element_type=F32)
                acc = part if acc is None else acc + part
                dbd_ref[kn] += lax.dot_general(u_ref[:, cs], lam_b, _DNUMS["tn"], preferred_element_type=F32)
                dcd_ref[kn] += lax.dot_general(h_ref[:, ss].astype(BF16), dys_ref[:, cs], _DNUMS["tn"],
                                               preferred_element_type=F32)
            tmp[n] = acc + d_ref[:, cs] * dysf_ref[:, cs]
            for j in range(N_DEV):
                du_ref[j, :, cs] = tmp[n, pl.ds(j, per_seg, stride=N_DEV), :].astype(BF16)

    rev = lambda c: (nchunk - 1 - c, 0)
    const2 = lambda c: (0, 0)
    const3 = lambda c: (0, 0, 0)
    row = _bs((SCAN_ROWS, SSM_W), rev)
    du, dlam, dbd, dcd = pl.pallas_call(
        body, grid=(nchunk,),
        in_specs=[row, row, _bs((1, SSM_W), const2), row, _bs((SCAN_ROWS, 2 * NSTATE), rev),
                  _bs(bd.shape, const3), _bs(cd.shape, const3), _bs((8, 2 * NSTATE), const2), _bs((8, 2 * NSTATE), const2)],
        out_specs=[_bs((N_DEV, per_seg, SSM_W), lambda c: (0, nchunk - 1 - c, 0)), _bs((8, 2 * NSTATE), const2),
                   _bs(bd.shape, const3), _bs(cd.shape, const3)],
        out_shape=[SDS((N_DEV, t // N_DEV, SSM_W), BF16), SDS((8, 2 * NSTATE), F32), SDS(bd.shape, F32), SDS(cd.shape, F32)],
        scratch_shapes=[pltpu.VMEM((SCAN_ROWS, 2 * NSTATE), F32), pltpu.VMEM((SCAN_ROWS, 2 * NSTATE), F32),
                        pltpu.VMEM((8, 2 * NSTATE), F32), pltpu.VMEM((SSM_NB, SCAN_ROWS, BLK), F32)],
        compiler_params=_cp(1), name="ssm_scan_bwd")(dys_bf, dys, d_skip, u_bf, h, bd, cd, lamc8, start)
    return du.reshape(t, SSM_W), dlam, dbd, dcd


def _gelu_parts(x):
    c0 = math.sqrt(2.0 / math.pi)
    inner = c0 * (x + 0.044715 * x * x * x)
    th = jnp.tanh(inner)
    val = 0.5 * x * (1.0 + th)
    grad = 0.5 * (1.0 + th) + 0.5 * x * (1.0 - th * th) * c0 * (1.0 + 3.0 * 0.044715 * x * x)
    return val, grad


def _ssm_out_bwd(d_yg, ys, u, tm):
    t = u.shape[0]
    seg = t // N_DEV

    def body(dg_ref, ys_ref, u_ref, dys_ref, dysb_ref, dd_ref, tmp):
        for n in range(SSM_W // BLK):
            for j in range(N_DEV):
                tmp[n, pl.ds(j, tm // N_DEV, stride=N_DEV), :] = dg_ref[j, :, n * BLK:(n + 1) * BLK]
        dyg = jnp.concatenate([tmp[n] for n in range(SSM_W // BLK)], axis=1)
        dys = dyg * _gelu_parts(ys_ref[...])[1]
        dys_ref[...] = dys
        dysb_ref[...] = dys.astype(BF16)
        part = jnp.sum(dys * u_ref[...], axis=0, keepdims=True)

        @pl.when(pl.program_id(0) == 0)
        def _():
            dd_ref[...] = part

        @pl.when(pl.program_id(0) > 0)
        def _():
            dd_ref[...] += part

    row = _bs((tm, SSM_W), lambda i: (i, 0))
    return pl.pallas_call(
        body, grid=(t // tm,), in_specs=[_bs((N_DEV, tm // N_DEV, SSM_W), lambda i: (0, i, 0)), row, row],
        out_specs=[row, row, _bs((1, SSM_W), lambda i: (0, 0))],
        out_shape=[SDS((t, SSM_W), F32), SDS((t, SSM_W), BF16), SDS((1, SSM_W), F32)],
        scratch_shapes=[pltpu.VMEM((SSM_W // BLK, tm, BLK), F32)], compiler_params=_cp(1), name="ssm_out_bwd")(
            d_yg.reshape(N_DEV, seg, SSM_W), ys, u)


def _block_diag(blocks):
    nb, ng, r, c = blocks.shape
    eye = jnp.eye(ng, dtype=blocks.dtype)
    return (blocks[:, :, :, None, :] * eye[None, :, None, :, None]).reshape(nb, ng * r, ng * c)


def _diag_blocks(full, r, c):
    k, nb = full.shape[:2]
    ng = full.shape[2] // r
    x = full.reshape(k, nb, ng, r, ng, c)
    eye = jnp.eye(ng, dtype=full.dtype)
    return jnp.sum(x * eye[None, None, :, None, :, None], axis=4).reshape(k, nb * ng, r, c)


_SMALL = ("a_re", "a_im", "log_dt", "b_re", "b_im", "c_re", "c_im", "d_skip", "g_ffn", "g_final")


def _pack_small(arrs):
    flat = jnp.concatenate([a.reshape(-1) for a in arrs])
    pad = (-flat.shape[0]) % (8 * 128)
    return jnp.pad(flat, (0, pad)).reshape(-1, 128)


def _unpack_small(packed, shapes):
    flat = packed.reshape(-1)
    out, off = [], 0
    for s in shapes:
        n = math.prod(s)
        out.append(flat[off:off + n].reshape(s))
        off += n
    return out


def kernel(x, p, positions, g_mix, w_in, a_re, a_im, log_dt, b_re, b_im, c_re, c_im, d_skip, w_attn_proj, w_glu_a, w_glu_b, w_out, g_ffn, w_ffn_gate, w_ffn_up, w_ffn_down, w_ple_gate, w_ple_proj, g_final, loss_target, m_g_mix, m_w_in, m_a_re, m_a_im, m_log_dt, m_b_re, m_b_im, m_c_re, m_c_im, m_d_skip, m_w_attn_proj, m_w_glu_a, m_w_glu_b, m_w_out, m_g_ffn, m_w_ffn_gate, m_w_ffn_up, m_w_ffn_down, m_w_ple_gate, m_w_ple_proj, m_g_final, v_g_mix, v_w_in, v_a_re, v_a_im, v_log_dt, v_b_re, v_b_im, v_c_re, v_c_im, v_d_skip, v_w_attn_proj, v_w_glu_a, v_w_glu_b, v_w_out, v_g_ffn, v_w_ffn_gate, v_w_ffn_up, v_w_ffn_down, v_w_ple_gate, v_w_ple_proj, v_g_final):
    args = dict(locals())
    t, d = x.shape[1], x.shape[2]
    inw = w_in.shape[2] * N_DEV
    fs = w_ffn_gate.shape[2]
    ff = fs * N_DEV
    ple = w_ple_proj.shape[1]
    seg = t // N_DEV
    assert inw == 3 * QK_W + SSM_W + 2 * d and t % (N_DEV * SCAN_ROWS // 8) == 0 and seg & (seg - 1) == 0
    tm = min(1024, t)
    te = min(512, t)
    tk = min(1024, t)
    ucol = (3 * QK_W) // SSM_W
    gcol = (3 * QK_W + SSM_W) // d
    assert (3 * QK_W + SSM_W) % d == 0

    x2, p2, tgt = x[0], p[0, 0], loss_target[0]
    pos = positions.reshape(t, 1)
    inv = ROPE_THETA ** (-jnp.arange(ROPE_HALF, dtype=F32) * 2.0 / ROPE_DIM)
    invf = jnp.concatenate([inv, inv, jnp.zeros((HEAD_DIM - ROPE_DIM,), F32)]).reshape(1, HEAD_DIM)

    wnames = ("w_in", "w_attn_proj", "w_glu_a", "w_glu_b", "w_out", "w_ffn_gate", "w_ffn_up", "w_ffn_down", "w_ple_gate",
              "w_ple_proj")
    kinds = ("cols", "cols", "cols", "cols", "rows", "slot", "slot", "rows", "rows", "cols")
    shards = [args[n][0].astype(BF16) for n in wnames]
    sizes = [s.shape[0] if k == "rows" else s.shape[-1] for s, k in zip(shards, kinds)]
    ag = _exchange_start("gather_weights_start", shards, kinds, sizes, True)

    row_d = _bs((tm, d), lambda i, j, k: (i, 0))
    row_e = _bs((te, d), lambda i, j, k: (i, 0))
    vec_d = _bs((1, d), lambda i, j, k: (0, 0))
    sq_w = _bs((d, d), lambda i, j, k: (0, 0))
    n1 = _rms_fwd("norm_mix", x2, g_mix + ag[3][0:1, 0:1], tm)
    W_in, = _exchange_wait("gather_w_in_wait", ag, [0], kinds, sizes, True, n1)
    qkv = _mm("qkv_proj", (t // tm, 3, 1), [("nn", n1, row_d, W_in, _bs((d, QK_W), lambda i, j, k: (0, j)))],
              [(SDS((3, t // dil, dil * GROUP_W), BF16), _bs((None, tm // dil, dil * GROUP_W), lambda i, j, k: (j, i, 0)))
               for dil in DILATIONS],
              extras=[(pos, _bs((tm, 1), lambda i, j, k: (i, 0))), (invf, _bs((1, HEAD_DIM), lambda i, j, k: (0, 0)))],
              epilogue=_rope_dilate_epilogue(tm),
              scratch=[pltpu.VMEM((tm, HEAD_DIM), F32)] * 2 + [pltpu.VMEM((QK_W // HEAD_DIM, tm, HEAD_DIM), F32)])
    row_s = _bs((tm, SSM_W), lambda i, j, k: (i, 0))
    u_perm, u_bf = _mm("u_proj", (t // tm, 1, 1),
                       [("nn", n1.reshape(N_DEV, seg, d), _bs((N_DEV, tm // N_DEV, d), lambda i, j, k: (0, i, 0)), W_in,
                         _bs((d, SSM_W), lambda i, j, k: (0, ucol)))],
                       [(SDS((t, SSM_W), F32), row_s), (SDS((t, SSM_W), BF16), row_s)], epilogue=_interleave_epilogue,
                       scratch=[pltpu.VMEM((SSM_W // BLK, tm, BLK), F32)])
    zg, = _mm("z_gates", (t // tm, 2, 1),
              [("nn", n1, row_d, W_in, _bs((d, d), lambda i, j, k: (0, gcol + j)))],
              [(SDS((t, 2 * d), BF16), _bs((tm, d), lambda i, j, k: (i, j)))])

    outs, lses = [], []
    for g, dil in enumerate(DILATIONS):
        o_g, l_g = _attn_fwd(qkv[g], dil, min(1024, t // dil))
        outs.append(o_g)
        lses.append(l_g)
    merged = _attn_merge(outs, lses, te)
    attn, attn_bf, lts = merged[0], merged[1], merged[2:]

    nsq = seg.bit_length() - 1
    bar_re, bar_im, z_re, z_im, pw_re, pw_im = _ssm_disc(a_re[0], a_im[0], log_dt.reshape(SSM_GROUPS, 1), nsq)
    gp = SSM_GROUPS * SSM_STATE
    b_re2, b_im2 = b_re.reshape(gp, SSM_GROUP), b_im.reshape(gp, SSM_GROUP)
    bb_re, bb_im = _ssm_scale_b(z_re.reshape(gp, 1), z_im.reshape(gp, 1), b_re2, b_im2)

    def chunks(a, r, c):
        return a.reshape(SSM_NB, SSM_GROUPS // SSM_NB, r, c)

    bbt = lambda a: jnp.swapaxes(a.reshape(SSM_GROUPS, SSM_STATE, SSM_GROUP), 1, 2)
    bd = jnp.concatenate([_block_diag(chunks(bbt(bb_re), SSM_GROUP, SSM_STATE)),
                          _block_diag(chunks(bbt(bb_im), SSM_GROUP, SSM_STATE))]).astype(BF16)
    ct = lambda a: jnp.swapaxes(a[0], 1, 2)
    cd = jnp.concatenate([_block_diag(chunks(ct(c_re), SSM_STATE, SSM_GROUP)),
                          _block_diag(chunks(-ct(c_im), SSM_STATE, SSM_GROUP))]).astype(BF16)
    lam = jnp.concatenate([bar_re.reshape(1, gp), bar_im.reshape(1, gp)], axis=1)
    lamc = jnp.concatenate([bar_re.reshape(1, gp), -bar_im.reshape(1, gp)], axis=1)
    pw = jnp.concatenate([pw_re.reshape(1, gp), pw_im.reshape(1, gp)], axis=1)
    pwc = jnp.concatenate([pw_re.reshape(1, gp), -pw_im.reshape(1, gp)], axis=1)
    lam8, lamc8 = jnp.broadcast_to(lam, (8, 2 * gp)), jnp.broadcast_to(lamc, (8, 2 * gp))

    start_f = _ssm_carries("ssm_carries_fwd", u_bf, bd, "nn", lam8, pw, False)
    dsk = d_skip.reshape(1, SSM_W)
    h_all, ys, yg_bf = _ssm_fwd(u_bf, u_perm, dsk, bd, cd, lam8, start_f)
    W_ap, W_ga, W_gb, W_out, W_fg, W_fu, W_fd, W_pg, W_pp = _exchange_wait(
        "gather_rest_wait", ag, list(range(1, len(wnames))), kinds, sizes, True, yg_bf)
    W_fg = jnp.swapaxes(W_fg, 0, 1).reshape(d, ff)
    W_fu = jnp.swapaxes(W_fu, 0, 1).reshape(d, ff)

    glu_w = _bs((SSM_W, d), lambda i, j, k: (0, 0))
    row_s = _bs((tm, SSM_W), lambda i, j, k: (i, 0))
    gate_a = _bs((te, d), lambda i, j, k: (i, 0))
    gate_s = _bs((te, d), lambda i, j, k: (i, 1))
    td_f32, td_bf = SDS((t, d), F32), SDS((t, d), BF16)
    m_bf, ya, yb, attn_d = _mm(
        "glu_merge", (t // tm, 1, 1),
        [("nn", yg_bf, row_s, W_ga, glu_w), ("nn", yg_bf, row_s, W_gb, glu_w), ("nn", attn_bf, row_s, W_ap, glu_w)],
        [(td_bf, row_d)] * 4, extras=[(zg, row_d), (zg, _bs((tm, d), lambda i, j, k: (i, 1)))], epilogue=_glu_merge_epilogue)

    h1, n2 = _mm("out_proj", (t // tm, 1, 1), [("nn", m_bf, row_d, W_out, sq_w)], [(td_f32, row_d), (td_bf, row_d)],
                 extras=[(x2, row_d), (g_ffn, vec_d)], epilogue=_out_norm_epilogue)

    tn_f = ff // 2
    nf = ff // tn_f
    hid_o = _bs((tm, tn_f), lambda j, i, k: (i, j))
    tf_bf = SDS((t, ff), BF16)
    a_rows = _bs((tm, d), lambda j, i, k: (i, 0))
    w_cols = _bs((d, tn_f), lambda j, i, k: (0, j))
    act, fg, fu = _mm("ffn_gate_up", (nf, t // tm, 1), [("nn", n2, a_rows, W_fg, w_cols), ("nn", n2, a_rows, W_fu, w_cols)],
                      [(tf_bf, hid_o)] * 3, epilogue=_swiglu_epilogue)
    w_once = pl.BlockSpec((d, d), lambda i, j, k: (0, 0), pipeline_mode=pl.Buffered(1))
    loss_part, dg_final, dh2, dh2_bf, dpp_bf, dpg_bf, h2_bf = _mm(
        "ffn_down_head", (t // te, 1, 1),
        [("nn", act, _bs((te, ff), lambda i, j, k: (i, 0)), W_fd,
          pl.BlockSpec((ff, d), lambda i, j, k: (0, 0), pipeline_mode=pl.Buffered(1))),
         ("nn", p2, _bs((te, ple), lambda i, j, k: (i, 0)), W_pp, _bs((ple, d), lambda i, j, k: (0, 0)))],
        [(SDS((1, 1), F32), _bs((1, 1), lambda i, j, k: (0, 0))), (SDS((1, d), F32), vec_d), (td_f32, row_e), (td_bf, row_e),
         (td_bf, row_e), (td_bf, row_e), (td_bf, row_e)],
        extras=[(h1, row_e), (g_final.reshape(1, d), vec_d), (tgt, row_e), (W_pg, w_once)], epilogue=_head_epilogue(t // te),
        scratch=[pltpu.VMEM((1, d), F32)])
    loss = lax.psum(loss_part[0, 0], ("x", "y", "c"))

    nkt = t // tk
    tok_a = lambda w: _bs((tk, w), lambda i, j, k: (k, 0))

    def wgrad(name, a, wa, b, wb):
        return _mm(name, (1, 1, nkt), [("tn", a, tok_a(wa), b, tok_a(wb))],
                   [(SDS((wa, wb), BF16), _bs((wa, wb), lambda i, j, k: (0, 0)))])[0]

    dW_pp = wgrad("dw_ple_proj", p2, ple, dpp_bf, d)
    dW_pg = wgrad("dw_ple_gate", h2_bf, d, dpg_bf, d)
    dfg_bf, dfu_bf = _mm("d_ffn_down", (nf, t // tm, 1),
                         [("nt", dh2_bf, a_rows, W_fd, _bs((tn_f, d), lambda j, i, k: (j, 0)))],
                         [(tf_bf, hid_o), (tf_bf, hid_o)], extras=[(fg, hid_o), (fu, hid_o)], epilogue=_swiglu_bwd_epilogue)
    dW_fd, = _mm("dw_ffn_down", (nf, 1, nkt), [("tn", act, _bs((tk, tn_f), lambda i, j, k: (k, i)), dh2_bf, tok_a(d))],
                 [(SDS((ff, d), BF16), _bs((tn_f, d), lambda i, j, k: (i, 0)))])
    hid_t = _bs((tk, tn_f), lambda i, j, k: (k, j))
    wg_o = [(SDS((d, ff), BF16), _bs((d, tn_f), lambda i, j, k: (0, j)))]
    dW_fg, = _mm("dw_ffn_gate", (1, nf, nkt), [("tn", n2, tok_a(d), dfg_bf, hid_t)], wg_o)
    dW_fu, = _mm("dw_ffn_up", (1, nf, nkt), [("tn", n2, tok_a(d), dfu_bf, hid_t)], wg_o)
    dW_fg = jnp.swapaxes(dW_fg.reshape(d, N_DEV, fs), 0, 1)
    dW_fu = jnp.swapaxes(dW_fu.reshape(d, N_DEV, fs), 0, 1)
    group = lambda names: ([kinds[wnames.index(n)] for n in names], [sizes[wnames.index(n)] for n in names])
    ffn_names = ("w_ffn_gate", "w_ffn_up", "w_ffn_down", "w_ple_gate", "w_ple_proj")
    rs_ffn = _exchange_start("scatter_ffn_start", [dW_fg, dW_fu, dW_fd, dW_pg, dW_pp], *group(ffn_names), False)
    hid_all = _bs((te, ff), lambda i, j, k: (i, 0))
    w_all = pl.BlockSpec((d, ff), lambda i, j, k: (0, 0), pipeline_mode=pl.Buffered(1))
    dh1, dh1_bf, dg_ffn = _mm("d_ffn_gate_up", (t // te, 1, 1),
                              [("nt", dfg_bf, hid_all, W_fg, w_all), ("nt", dfu_bf, hid_all, W_fu, w_all)],
                              [(td_f32, row_e), (td_bf, row_e), (SDS((1, d), F32), vec_d)],
                              extras=[(h1, row_e), (g_ffn, vec_d), (dh2, row_e)], epilogue=_rms_bwd_epilogue, after=rs_ffn[3])

    dW_out = wgrad("dw_out", m_bf, d, dh1_bf, d)
    glu_once = pl.BlockSpec((SSM_W, d), lambda i, j, k: (0, 0), pipeline_mode=pl.Buffered(1))
    row_es = _bs((te, SSM_W), lambda i, j, k: (i, 0))
    ts_f32 = SDS((t, SSM_W), F32)
    dz_g, dad_bf, dya_bf, dyb_bf, d_yg, d_attn = _mm(
        "d_out_proj", (t // te, 1, 1), [("nt", dh1_bf, row_e, W_out, w_once)],
        [(SDS((t, 2 * d), BF16), _bs((te, 2 * d), lambda i, j, k: (i, 0))), (td_bf, row_e), (td_bf, row_e), (td_bf, row_e),
         (ts_f32, row_es), (ts_f32, row_es)],
        extras=[(zg, gate_a), (zg, gate_s), (attn_d, row_e), (ya, row_e), (yb, row_e), (W_ga, glu_once), (W_gb, glu_once),
                (W_ap, glu_once)], epilogue=_merge_bwd_epilogue)

    dW_ga = wgrad("dw_glu_a", yg_bf, SSM_W, dya_bf, d)
    dW_gb = wgrad("dw_glu_b", yg_bf, SSM_W, dyb_bf, d)
    dys, dys_bf, dd_skip = _ssm_out_bwd(d_yg, ys, u_perm, te)
    start_b = _ssm_carries("ssm_carries_bwd", dys_bf, cd, "nt", lamc8, pwc, True)
    dz_u, dlam8, dbd, dcd = _ssm_bwd(dys_bf, dys, dsk, u_bf, h_all, bd, cd, lamc8, start_b)
    dlam = jnp.sum(dlam8, axis=0)
    dbb = _diag_blocks(dbd.reshape(2, SSM_NB, BLK, 512), SSM_GROUP, SSM_STATE)
    dbb_re = jnp.swapaxes(dbb[0], 1, 2).reshape(gp, SSM_GROUP)
    dbb_im = jnp.swapaxes(dbb[1], 1, 2).reshape(gp, SSM_GROUP)
    dcc = _diag_blocks(dcd.reshape(2, SSM_NB, 512, BLK), SSM_STATE, SSM_GROUP)
    dc_re, dc_im = jnp.swapaxes(dcc[0], 1, 2), -jnp.swapaxes(dcc[1], 1, 2)
    db_re, db_im, dz_re, dz_im = _ssm_scale_b_bwd(z_re.reshape(gp, 1), z_im.reshape(gp, 1), b_re2, b_im2, dbb_re, dbb_im)
    gshape = (SSM_GROUPS, SSM_STATE)
    da_re, da_im, dlog_dt = _ssm_disc_bwd(a_re[0], a_im[0], log_dt.reshape(SSM_GROUPS, 1), dlam[:gp].reshape(gshape),
                                          dlam[gp:].reshape(gshape), dz_re.reshape(gshape), dz_im.reshape(gshape))

    dW_ap = wgrad("dw_attn_proj", attn_bf, GROUP_W, dad_bf, d)
    pre = _attn_bwd_pre(d_attn, attn, te)
    das, deltas = pre[:N_GROUPS], pre[N_GROUPS:]
    dqkvs = [_attn_bwd(qkv[g], das[g], lts[g], deltas[g], dil, min(1024, t // dil)) for g, dil in enumerate(DILATIONS)]
    dz_qkv = _undilate_rope_bwd(dqkvs, pos, invf, tm)

    dW_in, = _mm("dw_in_qkv", (1, 3, nkt), [("tn", n1, tok_a(d), dz_qkv, _bs((tk, QK_W), lambda i, j, k: (k, j)))],
                 [(SDS((d, inw), BF16), _bs((d, QK_W), lambda i, j, k: (0, j)))])
    dW_in, = _mm("dw_in_u", (1, 1, nkt), [("tn", n1, tok_a(d), dz_u, tok_a(SSM_W))],
                 [(SDS((d, inw), BF16), _bs((d, SSM_W), lambda i, j, k: (0, ucol)))], alias_to_out0=dW_in)
    dW_in, = _mm("dw_in_gates", (1, 2, nkt), [("tn", n1, tok_a(d), dz_g, _bs((tk, d), lambda i, j, k: (k, j)))],
                 [(SDS((d, inw), BF16), _bs((d, d), lambda i, j, k: (0, gcol + j)))], alias_to_out0=dW_in)
    small_parts = dict(a_re=da_re, a_im=da_im, log_dt=dlog_dt, b_re=db_re, b_im=db_im, c_re=dc_re, c_im=dc_im,
                       d_skip=dd_skip, g_ffn=dg_ffn, g_final=dg_final)
    small = _pack_small([small_parts[n] for n in _SMALL])
    rest_names = ("w_in", "w_attn_proj", "w_glu_a", "w_glu_b", "w_out")
    rest_kinds, rest_sizes = group(rest_names)
    rs_in = _exchange_start("scatter_rest_start", [dW_in, dW_ap, dW_ga, dW_gb, dW_out, small], rest_kinds + ["all"],
                            rest_sizes + [0], False)
    w_piece = lambda w, cb: pl.BlockSpec((d, w), lambda i, j, k: (0, cb), pipeline_mode=pl.Buffered(1))
    dx, dg_mix = _mm(
        "d_z_proj", (t // te, 1, 1),
        [("nt", dz_qkv, _bs((te, 3 * QK_W), lambda i, j, k: (i, 0)), W_in, w_piece(3 * QK_W, 0)),
         ("nt", dz_u, _bs((te, SSM_W), lambda i, j, k: (i, 0)), W_in, w_piece(SSM_W, ucol)),
         ("nt", dz_g, _bs((te, d), lambda i, j, k: (i, 0)), W_in, w_piece(d, gcol)),
         ("nt", dz_g, _bs((te, d), lambda i, j, k: (i, 1)), W_in, w_piece(d, gcol + 1))],
        [(td_f32, row_e), (SDS((1, d), F32), vec_d)],
        extras=[(x2, row_e), (g_mix, vec_d), (dh1, row_e)], epilogue=_rms_bwd_epilogue, after=rs_in[3])

    received = dict(zip(ffn_names, _exchange_wait("scatter_ffn_wait", rs_ffn, list(range(len(ffn_names))), *group(ffn_names),
                                                  False, dx)))
    *landed, small_all = _exchange_wait("scatter_rest_wait", rs_in, list(range(len(rest_names) + 1)), rest_kinds + ["all"],
                                        rest_sizes + [0], False, dx)
    received.update(zip(rest_names, landed))

    new = {}
    for n in wnames:
        new[n] = [o.reshape(args[n].shape)
                  for o in _adamw("adamw_" + n, received[n], args[n][0], args["m_" + n][0], args["v_" + n][0])]
    g_mix_all = _gather_small(_pack_small([dg_mix]))
    pk = lambda pre: jnp.concatenate([_pack_small([args[pre + n] for n in _SMALL]), _pack_small([args[pre + "g_mix"]])])
    sm = _adamw("adamw_small", jnp.concatenate([small_all, g_mix_all], axis=1), pk(""), pk("m_"), pk("v_"))
    rows_a = small.shape[0]
    shapes = [args[n].shape for n in _SMALL]
    for n, vals in zip(_SMALL, zip(*[_unpack_small(o[:rows_a], shapes) for o in sm])):
        new[n] = list(vals)
    new["g_mix"] = [_unpack_small(o[rows_a:], [g_mix.shape])[0] for o in sm]

    order = ("g_mix", "w_in", "a_re", "a_im", "log_dt", "b_re", "b_im", "c_re", "c_im", "d_skip", "w_attn_proj", "w_glu_a",
             "w_glu_b", "w_out", "g_ffn", "w_ffn_gate", "w_ffn_up", "w_ffn_down", "w_ple_gate", "w_ple_proj", "g_final")
    return (loss, dx.reshape(x.shape), *[new[n][0] for n in order], *[new[n][1] for n in order],
            *[new[n][2] for n in order], *[new[n][3] for n in order])
```

```python
import functools
import math

import jax
import jax.numpy as jnp
from jax import lax
from jax.experimental import pallas as pl
from jax.experimental.pallas import tpu as pltpu

F32 = jnp.float32
BF16 = jnp.bfloat16
SDS = jax.ShapeDtypeStruct

N_DEV = 8
HEAD_DIM = 128
HEADS_PER_GROUP = 4
GROUP_W = HEADS_PER_GROUP * HEAD_DIM
DILATIONS = (1, 4, 16)
N_GROUPS = len(DILATIONS)
QK_W = N_GROUPS * GROUP_W
BLK = 128
ROPE_THETA = 500000.0
ROPE_DIM = HEAD_DIM // 4
ROPE_HALF = ROPE_DIM // 2
SSM_W = 512
SSM_GROUP = 16
SSM_GROUPS = SSM_W // SSM_GROUP
SSM_STATE = 64
NSTATE = SSM_GROUPS * SSM_STATE
SSM_NB = 4
EPS = 1e-6
ADAM_LR, ADAM_B1, ADAM_B2, ADAM_EPS, ADAM_WD, ADAM_STEP = 0.001, 0.9, 0.999, 1e-08, 0.01, 10
NEG = -1e30

VMEM_LIMIT = 52 * 1024 * 1024
SCAN_ROWS = 512
SCAN_LANES = 512


def _cp(n):
    return pltpu.CompilerParams(dimension_semantics=("arbitrary",) * n, vmem_limit_bytes=VMEM_LIMIT)


def _sigmoid(x):
    return 0.5 * jnp.tanh(0.5 * x) + 0.5


_DNUMS = {"nn": (((1,), (0,)), ((), ())), "nt": (((1,), (1,)), ((), ())), "tn": (((0,), (0,)), ((), ()))}


def _bs(shape, fn):
    return pl.BlockSpec(shape, fn)


def _store_all(prods, extra_refs, out_refs, scratch_refs):
    r = prods[0]
    for p in prods[1:]:
        r = r + p
    for e in extra_refs:
        r = r + e[...]
    for o in out_refs:
        o[...] = r.astype(o.dtype)


def _mm(name, grid, pairs, outs, extras=(), epilogue=_store_all, scratch=(), alias_to_out0=None, after=None):
    nk = grid[2]
    npair = len(pairs)
    steps = [p[5] if len(p) > 5 else nk for p in pairs]

    def block(spec):
        return tuple(s for s in spec.block_shape if s is not None)

    def rows2d(shape):
        return (math.prod(shape[:-1]), shape[-1]) if len(shape) == 3 else shape

    acc_shapes = [jax.eval_shape(lambda u, v, dn=_DNUMS[p[0]]: lax.dot_general(u, v, dn, preferred_element_type=F32),
                                 SDS(rows2d(block(p[2])), BF16), SDS(block(p[4]), BF16)).shape for p in pairs]
    if nk == 1:
        acc_shapes = []
    n_in = 2 * npair + len(extras) + (alias_to_out0 is not None) + (after is not None)

    def body(*refs):
        extra_refs = refs[2 * npair:2 * npair + len(extras)]
        out_refs = refs[n_in:n_in + len(outs)]
        rest = refs[n_in + len(outs):]
        acc_refs = rest[:len(acc_shapes)]
        scratch_refs = rest[len(acc_refs):]
        k = pl.program_id(2)

        def product(i):
            a = refs[2 * i][...]
            if a.ndim == 3:
                a = a.reshape(-1, a.shape[-1])
            return lax.dot_general(a.astype(BF16), refs[2 * i + 1][...].astype(BF16), _DNUMS[pairs[i][0]],
                                   preferred_element_type=F32)

        if nk == 1:
            epilogue([product(i) for i in range(npair)], extra_refs, out_refs, scratch_refs)
            return
        for i in range(npair):
            @pl.when(k == 0)
            def _(i=i):
                acc_refs[i][...] = product(i)

            @pl.when((k > 0) & (k < steps[i]))
            def _(i=i):
                acc_refs[i][...] += product(i)

        @pl.when(k == nk - 1)
        def _():
            epilogue([a[...] for a in acc_refs], extra_refs, out_refs, scratch_refs)

    ins, in_specs = [], []
    for p in pairs:
        ins += [p[1], p[3]]
        in_specs += [p[2], p[4]]
    ins += [e[0] for e in extras]
    in_specs += [e[1] for e in extras]
    aliases = {}
    if alias_to_out0 is not None:
        aliases = {len(ins): 0}
        ins.append(alias_to_out0)
        in_specs.append(pl.BlockSpec(memory_space=pl.ANY))
    if after is not None:
        ins.append(after)
        in_specs.append(pl.BlockSpec(memory_space=pl.ANY))
    scratch_shapes = [pltpu.VMEM(s, F32) for s in acc_shapes] + list(scratch)
    return pl.pallas_call(body, grid=grid, in_specs=in_specs, out_specs=[o[1] for o in outs], out_shape=[o[0] for o in outs],
                          scratch_shapes=scratch_shapes, input_output_aliases=aliases, compiler_params=_cp(3), name=name)(*ins)


def _my_index():
    return 4 * lax.axis_index("x") + 2 * lax.axis_index("y") + lax.axis_index("c")


def _peer(d):
    mx, my, mc = lax.axis_index("x"), lax.axis_index("y"), lax.axis_index("c")
    return (mx ^ ((d >> 2) & 1), my ^ ((d >> 1) & 1), mc ^ (d & 1))


def _win(ref, kind, j, n):
    if kind == "all":
        return ref
    if kind == "slot":
        return ref.at[j]
    if kind == "rows":
        return ref.at[pl.ds(pl.multiple_of(j * n, 8), n)]
    return ref.at[:, pl.ds(pl.multiple_of(j * n, 128), n)]


def _win7(ref, kind, n):
    if kind == "slot":
        return ref.at[pl.ds(0, 7)]
    if kind == "rows":
        return ref.at[pl.ds(0, 7 * n)]
    return ref.at[:, pl.ds(0, 7 * n)]


def _full_shape(shard_shape, kind):
    if kind == "slot":
        return (N_DEV,) + tuple(shard_shape)
    if kind == "rows":
        return (N_DEV * shard_shape[0],) + tuple(shard_shape[1:])
    return (shard_shape[0], N_DEV * shard_shape[1])


def _shard_shape(full_shape, kind, n):
    if kind == "all":
        return tuple(full_shape)
    if kind == "slot":
        return tuple(full_shape[1:])
    if kind == "rows":
        return (n,) + tuple(full_shape[1:])
    return (full_shape[0], n)


_HBM = pl.BlockSpec(memory_space=pltpu.HBM)
_SEM = pl.BlockSpec(memory_space=pltpu.SEMAPHORE)
_DATAFLOW = pltpu.SideEffectType.DATAFLOW_SIDE_EFFECTING


def _exchange_start(name, srcs, kinds, sizes, gather):
    n = len(srcs)
    if gather:
        lands = [lax.empty(_full_shape(s.shape, k), s.dtype) for s, k in zip(srcs, kinds)]
    else:
        lands = [lax.empty((N_DEV,) + _shard_shape(s.shape, k, z), s.dtype) for s, k, z in zip(srcs, kinds, sizes)]

    def body(*refs):
        src, land = refs[:n], refs[n:2 * n]
        send_sems, recv_sems, local_sems = refs[2 * n], refs[2 * n + 1], refs[2 * n + 2]
        token = refs[4 * n + 3]
        me = _my_index()
        for a in range(n):
            _local_copy(src[a], land[a], kinds[a], sizes[a], gather, me, local_sems.at[a]).start()
        for a in range(n):
            for d in range(1, N_DEV):
                px, py, pc = _peer(d)
                if gather:
                    s_ref, d_ref = src[a], _win(land[a], kinds[a], me, sizes[a])
                else:
                    s_ref, d_ref = _win(src[a], kinds[a], 4 * px + 2 * py + pc, sizes[a]), land[a].at[me]
                pltpu.make_async_remote_copy(src_ref=s_ref, dst_ref=d_ref, send_sem=send_sems.at[a], recv_sem=recv_sems.at[a],
                                             device_id=(px, py, pc), device_id_type=pl.DeviceIdType.MESH).start()
        token[...] = jnp.zeros_like(token)

    hbm = [pltpu.with_memory_space_constraint(a, pltpu.HBM) for a in list(srcs) + lands]
    out = pl.pallas_call(
        body, name=name, in_specs=[_HBM] * (2 * n),
        out_shape=[pltpu.SemaphoreType.DMA((n,))] * 3 + [pltpu.HBM(a.shape, a.dtype) for a in hbm] + [SDS((8, 128), F32)],
        out_specs=[_SEM] * 3 + [_HBM] * (2 * n) + [pl.BlockSpec(memory_space=pltpu.VMEM)],
        input_output_aliases={i: 3 + i for i in range(2 * n)},
        compiler_params=pltpu.CompilerParams(has_side_effects=_DATAFLOW))(*hbm)
    return out[0:3], out[3:3 + n], out[3 + n:3 + 2 * n], out[-1]


def _local_copy(src, land, kind, size, gather, me, sem):
    if gather:
        return pltpu.make_async_copy(src, _win(land, kind, me, size), sem)
    return pltpu.make_async_copy(_win(src, kind, me, size), land.at[me], sem)


def _exchange_wait(name, started, which, kinds, sizes, gather, after):
    sems, srcs, lands, _ = started
    n = len(which)

    def body(*refs):
        src, land = refs[:n], refs[n:2 * n]
        send_ref, recv_ref, local_ref = refs[2 * n:2 * n + 3]
        me = _my_index()
        my_id = (lax.axis_index("x"), lax.axis_index("y"), lax.axis_index("c"))
        for i, a in enumerate(which):
            seven = _win7(land[i], kinds[a], sizes[a]) if gather else land[i].at[pl.ds(0, 7)]
            pltpu.make_async_remote_copy(src_ref=seven, dst_ref=seven, send_sem=send_ref.at[a], recv_sem=recv_ref.at[a],
                                         device_id=my_id, device_id_type=pl.DeviceIdType.MESH).wait()
            _local_copy(src[i], land[i], kinds[a], sizes[a], gather, me, local_ref.at[a]).wait()

    hbm = [srcs[a] for a in which] + [lands[a] for a in which]
    out = pl.pallas_call(
        body, name=name, in_specs=[_HBM] * (2 * n) + [_SEM] * 3 + [pl.BlockSpec(memory_space=pl.ANY)],
        out_shape=[pltpu.HBM(a.shape, a.dtype) for a in hbm], out_specs=[_HBM] * (2 * n),
        input_output_aliases={i: i for i in range(2 * n)},
        compiler_params=pltpu.CompilerParams(has_side_effects=_DATAFLOW))(*hbm, *sems, after)
    return out[n:]


def _gather_small(small):
    def body(in_ref, out_ref, send_sem, recv_sem, local_sem):
        me = _my_index()
        my_id = (lax.axis_index("x"), lax.axis_index("y"), lax.axis_index("c"))
        cp = pltpu.make_async_copy(in_ref, out_ref.at[me], local_sem)
        cp.start()
        for d in range(1, N_DEV):
            pltpu.make_async_remote_copy(src_ref=in_ref, dst_ref=out_ref.at[me], send_sem=send_sem, recv_sem=recv_sem,
                                         device_id=_peer(d), device_id_type=pl.DeviceIdType.MESH).start()
        seven = out_ref.at[pl.ds(0, 7)]
        pltpu.make_async_remote_copy(src_ref=seven, dst_ref=seven, send_sem=send_sem, recv_sem=recv_sem, device_id=my_id,
                                     device_id_type=pl.DeviceIdType.MESH).wait()
        cp.wait()

    any_spec = pl.BlockSpec(memory_space=pl.ANY)
    return pl.pallas_call(body, in_specs=[any_spec], out_specs=any_spec, out_shape=SDS((N_DEV,) + small.shape, F32),
                          scratch_shapes=[pltpu.SemaphoreType.DMA] * 3, name="gather_small")(small)


def _adamw(name, recv, w, m, v):
    rows, cols = w.shape
    tr = max(c for c in range(16, 257, 16) if rows % c == 0) if rows % 16 == 0 else rows

    def body(r_ref, w_ref, m_ref, v_ref, g_ref, d_ref, nm_ref, nv_ref):
        g = r_ref[0].astype(F32)
        for s in range(1, N_DEV):
            g = g + r_ref[s].astype(F32)
        nm = ADAM_B1 * m_ref[...] + (1.0 - ADAM_B1) * g
        nv = ADAM_B2 * v_ref[...] + (1.0 - ADAM_B2) * (g * g)
        m_hat = nm / (1.0 - ADAM_B1 ** ADAM_STEP)
        v_hat = nv / (1.0 - ADAM_B2 ** ADAM_STEP)
        g_ref[...] = g
        d_ref[...] = -ADAM_LR * (m_hat / (jnp.sqrt(v_hat) + ADAM_EPS) + ADAM_WD * w_ref[...])
        nm_ref[...] = nm
        nv_ref[...] = nv

    blk = _bs((tr, cols), lambda i: (i, 0))
    return pl.pallas_call(
        body, grid=(rows // tr,), in_specs=[_bs((N_DEV, tr, cols), lambda i: (0, i, 0)), blk, blk, blk],
        out_specs=[blk] * 4, out_shape=[SDS((rows, cols), F32)] * 4, compiler_params=_cp(1), name=name)(recv, w, m, v)


def _rms_fwd(name, x, g, tm):
    t, d = x.shape

    def body(x_ref, g_ref, n_ref):
        xv = x_ref[...]
        r = lax.rsqrt(jnp.mean(xv * xv, axis=-1, keepdims=True) + EPS)
        n_ref[...] = (xv * r * g_ref[...]).astype(BF16)

    return pl.pallas_call(body, grid=(t // tm,), in_specs=[_bs((tm, d), lambda i: (i, 0)), _bs((1, d), lambda i: (0, 0))],
                          out_specs=_bs((tm, d), lambda i: (i, 0)), out_shape=SDS((t, d), BF16), compiler_params=_cp(1),
                          name=name)(x, g)


def _accumulate_rows(ref, part):
    @pl.when(pl.program_id(0) == 0)
    def _():
        ref[...] = part

    @pl.when(pl.program_id(0) > 0)
    def _():
        ref[...] += part


def _rms_bwd_epilogue(prods, extra_refs, out_refs, scratch_refs):
    dyv = prods[0]
    for p in prods[1:]:
        dyv = dyv + p
    if len(extra_refs) > 3:
        dyv = dyv + extra_refs[3][...]
    xv = extra_refs[0][...]
    r = lax.rsqrt(jnp.mean(xv * xv, axis=-1, keepdims=True) + EPS)
    xh = xv * r
    dxh = dyv * extra_refs[1][...]
    dx = extra_refs[2][...] + r * (dxh - xh * jnp.mean(dxh * xh, axis=-1, keepdims=True))
    for o in out_refs[:-1]:
        o[...] = dx.astype(o.dtype)
    _accumulate_rows(out_refs[-1], jnp.sum(dyv * xh, axis=0, keepdims=True))


def _out_norm_epilogue(prods, extra_refs, out_refs, scratch_refs):
    h = prods[0] + extra_refs[0][...]
    r = lax.rsqrt(jnp.mean(h * h, axis=-1, keepdims=True) + EPS)
    out_refs[0][...] = h
    out_refs[1][...] = (h * r * extra_refs[1][...]).astype(BF16)


def _glu_merge_epilogue(prods, extra_refs, out_refs, scratch_refs):
    ya, yb, ad = prods
    ga, gs = extra_refs[0][...].astype(F32), extra_refs[1][...].astype(F32)
    m = _sigmoid(ga) * ad + _sigmoid(gs) * (ya * _sigmoid(yb))
    out_refs[0][...] = m.astype(BF16)
    for o, val in zip(out_refs[1:], (ya, yb, ad)):
        o[...] = val.astype(o.dtype)


def _merge_bwd_epilogue(prods, extra_refs, out_refs, scratch_refs):
    dmv = prods[0]
    d = dmv.shape[1]
    ga, gs = _sigmoid(extra_refs[0][...].astype(F32)), _sigmoid(extra_refs[1][...].astype(F32))
    adv, yav = extra_refs[2][...].astype(F32), extra_refs[3][...].astype(F32)
    sb = _sigmoid(extra_refs[4][...].astype(F32))
    out_refs[0][:, 0:d] = (dmv * adv * ga * (1.0 - ga)).astype(BF16)
    out_refs[0][:, d:2 * d] = (dmv * (yav * sb) * gs * (1.0 - gs)).astype(BF16)
    dad = (dmv * ga).astype(BF16)
    dsd = dmv * gs
    dya = (dsd * sb).astype(BF16)
    dyb = (dsd * yav * sb * (1.0 - sb)).astype(BF16)
    out_refs[1][...], out_refs[2][...], out_refs[3][...] = dad, dya, dyb
    nt = _DNUMS["nt"]
    out_refs[4][...] = (lax.dot_general(dya, extra_refs[5][...], nt, preferred_element_type=F32)
                        + lax.dot_general(dyb, extra_refs[6][...], nt, preferred_element_type=F32))
    out_refs[5][...] = lax.dot_general(dad, extra_refs[7][...], nt, preferred_element_type=F32)


def _swiglu_epilogue(prods, extra_refs, out_refs, scratch_refs):
    gv, uv = prods
    out_refs[0][...] = (gv * _sigmoid(gv) * uv).astype(BF16)
    out_refs[1][...] = gv.astype(out_refs[1].dtype)
    out_refs[2][...] = uv.astype(out_refs[2].dtype)


def _swiglu_bwd_epilogue(prods, extra_refs, out_refs, scratch_refs):
    dav = prods[0]
    gv, uv = extra_refs[0][...].astype(F32), extra_refs[1][...].astype(F32)
    sg = _sigmoid(gv)
    out_refs[0][...] = (dav * uv * sg * (1.0 + gv * (1.0 - sg))).astype(BF16)
    out_refs[1][...] = (dav * gv * sg).astype(BF16)


def _head_epilogue(n_tiles):
    def epilogue(prods, extra_refs, out_refs, scratch_refs):
        h2 = prods[0] + extra_refs[0][...]
        h2_bf = h2.astype(BF16)
        out_refs[6][...] = h2_bf
        pgv = jnp.dot(h2_bf, extra_refs[3][...], preferred_element_type=F32)
        ppv = prods[1]
        d = pgv.shape[1]
        lacc = scratch_refs[0]
        sg = _sigmoid(pgv)
        h3 = h2 + sg * ppv
        r = lax.rsqrt(jnp.mean(h3 * h3, axis=-1, keepdims=True) + EPS)
        xh = h3 * r
        gv = extra_refs[1][...]
        diff = xh * gv - extra_refs[2][...]
        dout = diff * (1.0 / d)
        dxh = dout * gv
        dh3 = r * (dxh - xh * jnp.mean(dxh * xh, axis=-1, keepdims=True))
        dpg = (dh3 * ppv * sg * (1.0 - sg)).astype(BF16)
        dh2 = dh3 + lax.dot_general(dpg, extra_refs[3][...], _DNUMS["nt"], preferred_element_type=F32)
        out_refs[2][...] = dh2
        out_refs[3][...] = dh2.astype(BF16)
        out_refs[4][...] = (dh3 * sg).astype(BF16)
        out_refs[5][...] = dpg
        _accumulate_rows(out_refs[1], jnp.sum(dout * xh, axis=0, keepdims=True))
        _accumulate_rows(lacc, jnp.sum(diff * diff, axis=0, keepdims=True))

        @pl.when(pl.program_id(0) == n_tiles - 1)
        def _():
            out_refs[0][...] = (0.5 / d) * jnp.sum(lacc[...], axis=-1, keepdims=True)

    return epilogue


def _strided(r, n, d):
    return pl.ds(r, n, stride=d) if d > 1 else pl.ds(0, n)


def _rope_tables(pos_ref, invf_ref, c_s, s_s):
    ang = pos_ref[...].astype(F32) * invf_ref[...]
    lane = lax.broadcasted_iota(jnp.int32, ang.shape, 1)
    sn = jnp.sin(ang)
    c_s[...] = jnp.where(lane < ROPE_DIM, jnp.cos(ang), 1.0)
    s_s[...] = jnp.where(lane < ROPE_HALF, -sn, jnp.where(lane < ROPE_DIM, sn, 0.0))


def _rope_partner(xv, first_half):
    return jnp.where(first_half, pltpu.roll(xv, HEAD_DIM - ROPE_HALF, 1), pltpu.roll(xv, ROPE_HALF, 1))


def _rope_dilate_epilogue(tm):
    def epilogue(prods, extra_refs, out_refs, scratch_refs):
        zv = prods[0]
        pos_ref, invf_ref = extra_refs
        c_s, s_s, rot = scratch_refs
        c = pl.program_id(1)

        @pl.when(c == 0)
        def _():
            _rope_tables(pos_ref, invf_ref, c_s, s_s)

        @pl.when(c < 2)
        def _():
            cc, ss = c_s[...], s_s[...]
            first_half = lax.broadcasted_iota(jnp.int32, cc.shape, 1) < ROPE_HALF
            for h in range(QK_W // HEAD_DIM):
                xv = zv[:, h * HEAD_DIM:(h + 1) * HEAD_DIM]
                rot[h] = xv * cc + _rope_partner(xv, first_half) * ss

        @pl.when(c == 2)
        def _():
            for h in range(QK_W // HEAD_DIM):
                rot[h] = zv[:, h * HEAD_DIM:(h + 1) * HEAD_DIM]

        for g, (d, o_ref) in enumerate(zip(DILATIONS, out_refs)):
            n = tm // d
            for r in range(d):
                for hh in range(HEADS_PER_GROUP):
                    oc = r * GROUP_W + hh * HEAD_DIM
                    o_ref[:, oc:oc + HEAD_DIM] = rot[g * HEADS_PER_GROUP + hh, _strided(r, n, d), :].astype(BF16)

    return epilogue


def _band_masks(first_tile):
    qi = lax.broadcasted_iota(jnp.int32, (BLK, 2 * BLK), 0)
    kj = lax.broadcasted_iota(jnp.int32, (BLK, 2 * BLK), 1)
    band = (kj >= qi) & (kj <= qi + BLK)
    return band, band & ((kj >= BLK) | jnp.logical_not(first_tile))


def _attn_fwd(qkv, d, qt):
    ell = qkv.shape[1]
    nsub = qt // BLK
    scale = 1.0 / math.sqrt(HEAD_DIM)

    def body(q_ref, kc_ref, kp_ref, vc_ref, vp_ref, o_ref, lse_ref, kcat, vcat):
        nb = pl.program_id(1)
        kcat[0:BLK, :] = kp_ref[...]
        kcat[BLK:, :] = kc_ref[...]
        vcat[0:BLK, :] = vp_ref[...]
        vcat[BLK:, :] = vc_ref[...]
        lane = lax.broadcasted_iota(jnp.int32, (BLK, HEAD_DIM), 1)
        band, band_first = _band_masks(nb == 0)
        for b in range(nsub):
            valid = band_first if b == 0 else band
            lse_t = jnp.zeros((BLK, HEAD_DIM), F32)
            for hh in range(HEADS_PER_GROUP):
                cs = slice(hh * HEAD_DIM, (hh + 1) * HEAD_DIM)
                qb = q_ref[b * BLK:(b + 1) * BLK, cs]
                kk = kcat[b * BLK:(b + 2) * BLK, cs]
                vv = vcat[b * BLK:(b + 2) * BLK, cs]
                s = lax.dot_general(qb, kk, _DNUMS["nt"], preferred_element_type=F32) * scale
                s = jnp.where(valid, s, NEG)
                mx = jnp.max(s, axis=-1, keepdims=True)
                p = jnp.exp(s - mx)
                den = jnp.sum(p, axis=-1, keepdims=True)
                o = jnp.dot(p.astype(BF16), vv, preferred_element_type=F32) / den
                o_ref[b * BLK:(b + 1) * BLK, cs] = o
                lse_t = jnp.where(lane == hh, mx + jnp.log(den), lse_t)
            lse_ref[b * BLK:(b + 1) * BLK, :] = lse_t

    cur = lambda c: _bs((None, qt, GROUP_W), lambda r, nb: (c, nb, r))
    prev = lambda c: _bs((None, BLK, GROUP_W), lambda r, nb: (c, jnp.maximum(nb * nsub - 1, 0), r))
    return pl.pallas_call(
        body, grid=(d, ell // qt), in_specs=[cur(0), cur(1), prev(1), cur(2), prev(2)],
        out_specs=[_bs((qt, GROUP_W), lambda r, nb: (nb, r)), _bs((None, qt, HEAD_DIM), lambda r, nb: (r, nb, 0))],
        out_shape=[SDS((ell, d * GROUP_W), F32), SDS((d, ell, HEAD_DIM), F32)],
        scratch_shapes=[pltpu.VMEM((qt + BLK, GROUP_W), BF16)] * 2, compiler_params=_cp(2), name=f"attn_fwd_d{d}")(
            qkv, qkv, qkv, qkv, qkv)


def _attn_merge(outs, lses, tm):
    t = outs[0].shape[0]

    def body(o0, o1, o2, l0, l1, l2, attn_ref, attn_bf_ref, t0, t1, t2, so, sl, lt_s):
        for g, (d, o_ref, l_ref) in enumerate(zip(DILATIONS, (o0, o1, o2), (l0, l1, l2))):
            n = tm // d
            for r in range(d):
                rows = _strided(r, n, d)
                for hh in range(HEADS_PER_GROUP):
                    oc = r * GROUP_W + hh * HEAD_DIM
                    so[g * HEADS_PER_GROUP + hh, rows, :] = o_ref[:, oc:oc + HEAD_DIM]
                sl[g, rows, :] = l_ref[r]
        ls = [sl[g] for g in range(N_GROUPS)]
        mx = jnp.maximum(jnp.maximum(ls[0], ls[1]), ls[2])
        es = [jnp.exp(l - mx) for l in ls]
        den = es[0] + es[1] + es[2]
        ws = [e / den for e in es]
        lt_s[...] = mx + jnp.log(den)
        for hh in range(HEADS_PER_GROUP):
            cs = slice(hh * HEAD_DIM, (hh + 1) * HEAD_DIM)
            a = ws[0][:, hh:hh + 1] * so[hh]
            for g in range(1, N_GROUPS):
                a = a + ws[g][:, hh:hh + 1] * so[g * HEADS_PER_GROUP + hh]
            attn_ref[:, cs] = a
            attn_bf_ref[:, cs] = a.astype(BF16)
        for d, t_ref in zip(DILATIONS, (t0, t1, t2)):
            n = tm // d
            for r in range(d):
                t_ref[r] = lt_s[_strided(r, n, d), :]

    dil = lambda d: _bs((tm // d, d * GROUP_W), lambda i: (i, 0))
    lsp = lambda d: _bs((d, tm // d, HEAD_DIM), lambda i: (0, i, 0))
    row = _bs((tm, GROUP_W), lambda i: (i, 0))
    return pl.pallas_call(
        body, grid=(t // tm,),
        in_specs=[dil(d) for d in DILATIONS] + [lsp(d) for d in DILATIONS],
        out_specs=[row, row] + [lsp(d) for d in DILATIONS],
        out_shape=[SDS((t, GROUP_W), F32), SDS((t, GROUP_W), BF16)] + [SDS(l.shape, F32) for l in lses],
        scratch_shapes=[pltpu.VMEM((N_GROUPS * HEADS_PER_GROUP, tm, HEAD_DIM), F32), pltpu.VMEM((N_GROUPS, tm, HEAD_DIM), F32),
                        pltpu.VMEM((tm, HEAD_DIM), F32)],
        compiler_params=_cp(1), name="attn_merge")(*outs, *lses)


def _attn_bwd_pre(d_attn, attn, tm):
    t = attn.shape[0]

    def body(da_ref, a_ref, g0, g1, g2, e0, e1, e2, dl_s, da_s):
        lane = lax.broadcasted_iota(jnp.int32, (tm, HEAD_DIM), 1)
        dl = jnp.zeros((tm, HEAD_DIM), F32)
        for hh in range(HEADS_PER_GROUP):
            cs = slice(hh * HEAD_DIM, (hh + 1) * HEAD_DIM)
            dav = da_ref[:, cs]
            da_s[hh] = dav
            dl = jnp.where(lane == hh, jnp.sum(dav * a_ref[:, cs], axis=-1, keepdims=True), dl)
        dl_s[...] = dl
        for d, g_ref, e_ref in zip(DILATIONS, (g0, g1, g2), (e0, e1, e2)):
            n = tm // d
            for r in range(d):
                rows = _strided(r, n, d)
                for hh in range(HEADS_PER_GROUP):
                    oc = r * GROUP_W + hh * HEAD_DIM
                    g_ref[:, oc:oc + HEAD_DIM] = da_s[hh, rows, :].astype(BF16)
                e_ref[r] = dl_s[rows, :]

    row = _bs((tm, GROUP_W), lambda i: (i, 0))
    return pl.pallas_call(
        body, grid=(t // tm,), in_specs=[row, row],
        out_specs=[_bs((tm // d, d * GROUP_W), lambda i: (i, 0)) for d in DILATIONS]
        + [_bs((d, tm // d, HEAD_DIM), lambda i: (0, i, 0)) for d in DILATIONS],
        out_shape=[SDS((t // d, d * GROUP_W), BF16) for d in DILATIONS]
        + [SDS((d, t // d, HEAD_DIM), F32) for d in DILATIONS],
        scratch_shapes=[pltpu.VMEM((tm, HEAD_DIM), F32), pltpu.VMEM((HEADS_PER_GROUP, tm, HEAD_DIM), F32)],
        compiler_params=_cp(1), name="attn_bwd_pre")(d_attn, attn)


def _attn_bwd(qkv, d_a, lt, delta, d, qt):
    ell = qkv.shape[1]
    nsub = qt // BLK
    ntile = ell // qt
    nblk = ell // BLK
    scale = 1.0 / math.sqrt(HEAD_DIM)

    def body(q_ref, qn_ref, kc_ref, kp_ref, vc_ref, vp_ref, da_ref, dan_ref, lt_ref, ltn_ref, dl_ref, dln_ref, o_ref,
             kcat, vcat, dk_acc, dv_acc):
        nb = pl.program_id(1)
        kcat[0:BLK, :] = kp_ref[...]
        kcat[BLK:, :] = kc_ref[...]
        vcat[0:BLK, :] = vp_ref[...]
        vcat[BLK:, :] = vc_ref[...]
        qi = lax.broadcasted_iota(jnp.int32, (BLK, BLK), 0)
        kj = lax.broadcasted_iota(jnp.int32, (BLK, BLK), 1)
        valid_next = (kj >= qi) & (nb < ntile - 1)
        band, band_first = _band_masks(nb == 0)
        for hh in range(HEADS_PER_GROUP):
            cs = slice(hh * HEAD_DIM, (hh + 1) * HEAD_DIM)
            dk_acc[...] = jnp.zeros_like(dk_acc)
            dv_acc[...] = jnp.zeros_like(dv_acc)
            for b in range(nsub):
                rs = slice(b * BLK, (b + 1) * BLK)
                ks = slice(b * BLK, (b + 2) * BLK)
                valid = band_first if b == 0 else band
                qb, kk, vv, dab = q_ref[rs, cs], kcat[ks, cs], vcat[ks, cs], da_ref[rs, cs]
                s = lax.dot_general(qb, kk, _DNUMS["nt"], preferred_element_type=F32) * scale
                p = jnp.where(valid, jnp.exp(s - lt_ref[rs, hh:hh + 1]), 0.0)
                dp = lax.dot_general(dab, vv, _DNUMS["nt"], preferred_element_type=F32)
                ds = (p * (dp - dl_ref[rs, hh:hh + 1])).astype(BF16)
                o_ref[0, rs, cs] = jnp.dot(ds, kk, preferred_element_type=F32) * scale
                dk_acc[ks, :] += lax.dot_general(ds, qb, _DNUMS["tn"], preferred_element_type=F32) * scale
                dv_acc[ks, :] += lax.dot_general(p.astype(BF16), dab, _DNUMS["tn"], preferred_element_type=F32)
            ks = slice(nsub * BLK, (nsub + 1) * BLK)
            qn, kl, vl, dan = qn_ref[:, cs], kcat[ks, cs], vcat[ks, cs], dan_ref[:, cs]
            s = lax.dot_general(qn, kl, _DNUMS["nt"], preferred_element_type=F32) * scale
            p = jnp.where(valid_next, jnp.exp(s - ltn_ref[:, hh:hh + 1]), 0.0)
            dp = lax.dot_general(dan, vl, _DNUMS["nt"], preferred_element_type=F32)
            ds = (p * (dp - dln_ref[:, hh:hh + 1])).astype(BF16)
            dk_acc[ks, :] += lax.dot_general(ds, qn, _DNUMS["tn"], preferred_element_type=F32) * scale
            dv_acc[ks, :] += lax.dot_general(p.astype(BF16), dan, _DNUMS["tn"], preferred_element_type=F32)
            o_ref[1, :, cs] = dk_acc[BLK:, :]
            o_ref[2, :, cs] = dv_acc[BLK:, :]

    nxt = lambda nb: jnp.minimum((nb + 1) * nsub, nblk - 1)
    prv = lambda nb: jnp.maximum(nb * nsub - 1, 0)
    cur3 = lambda c: _bs((None, qt, GROUP_W), lambda r, nb: (c, nb, r))
    in_specs = [
        cur3(0), _bs((None, BLK, GROUP_W), lambda r, nb: (0, nxt(nb), r)),
        cur3(1), _bs((None, BLK, GROUP_W), lambda r, nb: (1, prv(nb), r)),
        cur3(2), _bs((None, BLK, GROUP_W), lambda r, nb: (2, prv(nb), r)),
        _bs((qt, GROUP_W), lambda r, nb: (nb, r)), _bs((BLK, GROUP_W), lambda r, nb: (nxt(nb), r)),
        _bs((None, qt, HEAD_DIM), lambda r, nb: (r, nb, 0)), _bs((None, BLK, HEAD_DIM), lambda r, nb: (r, nxt(nb), 0)),
        _bs((None, qt, HEAD_DIM), lambda r, nb: (r, nb, 0)), _bs((None, BLK, HEAD_DIM), lambda r, nb: (r, nxt(nb), 0)),
    ]
    return pl.pallas_call(
        body, grid=(d, ntile), in_specs=in_specs, out_specs=_bs((3, qt, GROUP_W), lambda r, nb: (0, nb, r)),
        out_shape=SDS((3, ell, d * GROUP_W), F32),
        scratch_shapes=[pltpu.VMEM((qt + BLK, GROUP_W), BF16)] * 2 + [pltpu.VMEM((qt + BLK, HEAD_DIM), F32)] * 2,
        compiler_params=_cp(2), name=f"attn_bwd_d{d}")(qkv, qkv, qkv, qkv, qkv, qkv, d_a, d_a, lt, lt, delta, delta)


def _undilate_rope_bwd(dqkvs, pos, invf, tm):
    t = pos.shape[0]

    def body(g0, g1, g2, pos_ref, invf_ref, o_ref, c_s, s_s, nat):
        c = pl.program_id(1)

        @pl.when(c == 0)
        def _():
            _rope_tables(pos_ref, invf_ref, c_s, s_s)

        for g, (d, g_ref) in enumerate(zip(DILATIONS, (g0, g1, g2))):
            n = tm // d
            for r in range(d):
                for hh in range(HEADS_PER_GROUP):
                    oc = r * GROUP_W + hh * HEAD_DIM
                    nat[g * HEADS_PER_GROUP + hh, _strided(r, n, d), :] = g_ref[:, oc:oc + HEAD_DIM]

        @pl.when(c < 2)
        def _():
            cc, ss = c_s[...], s_s[...]
            first_half = lax.broadcasted_iota(jnp.int32, cc.shape, 1) < ROPE_HALF
            for h in range(QK_W // HEAD_DIM):
                xv = nat[h]
                y = xv * cc - _rope_partner(xv, first_half) * ss
                o_ref[:, h * HEAD_DIM:(h + 1) * HEAD_DIM] = y.astype(BF16)

        @pl.when(c == 2)
        def _():
            for h in range(QK_W // HEAD_DIM):
                o_ref[:, h * HEAD_DIM:(h + 1) * HEAD_DIM] = nat[h].astype(BF16)

    return pl.pallas_call(
        body, grid=(t // tm, 3),
        in_specs=[_bs((None, tm // d, d * GROUP_W), lambda i, c: (c, i, 0)) for d in DILATIONS]
        + [_bs((tm, 1), lambda i, c: (i, 0)), _bs((1, HEAD_DIM), lambda i, c: (0, 0))],
        out_specs=_bs((tm, QK_W), lambda i, c: (i, c)), out_shape=SDS((t, 3 * QK_W), BF16),
        scratch_shapes=[pltpu.VMEM((tm, HEAD_DIM), F32)] * 2 + [pltpu.VMEM((QK_W // HEAD_DIM, tm, HEAD_DIM), F32)],
        compiler_params=_cp(2), name="undilate_rope_bwd")(*dqkvs, pos, invf)


def _cmul(ar, ai, br, bi):
    return ar * br - ai * bi, ar * bi + ai * br


def _ssm_disc(a_re, a_im, log_dt, nsq):
    def body(lr_ref, li_ref, ldt_ref, br_ref, bi_ref, zr_ref, zi_ref, pr_ref, pi_ref):
        lr, li = lr_ref[...], li_ref[...]
        dt = jnp.exp(ldt_ref[...])
        mag = jnp.exp(lr * dt)
        bar_re, bar_im = mag * jnp.cos(li * dt), mag * jnp.sin(li * dt)
        nr, ni = bar_re - 1.0, bar_im
        den = lr * lr + li * li
        br_ref[...], bi_ref[...] = bar_re, bar_im
        zr_ref[...] = (nr * lr + ni * li) / den
        zi_ref[...] = (ni * lr - nr * li) / den
        pr, pi = bar_re, bar_im
        for _ in range(nsq):
            pr, pi = _cmul(pr, pi, pr, pi)
        pr_ref[...], pi_ref[...] = pr, pi

    return pl.pallas_call(body, out_shape=[SDS(a_re.shape, F32)] * 6, name="ssm_discretise")(a_re, a_im, log_dt)


def _ssm_scale_b(z_re, z_im, b_re, b_im):
    def body(zr_ref, zi_ref, br_ref, bi_ref, or_ref, oi_ref):
        zr, zi, br, bi = zr_ref[...], zi_ref[...], br_ref[...], bi_ref[...]
        or_ref[...] = zr * br - zi * bi
        oi_ref[...] = zr * bi + zi * br

    return pl.pallas_call(body, out_shape=[SDS(b_re.shape, F32)] * 2, name="ssm_scale_b")(z_re, z_im, b_re, b_im)


def _ssm_scale_b_bwd(z_re, z_im, b_re, b_im, g_re, g_im):
    def body(zr_ref, zi_ref, br_ref, bi_ref, gr_ref, gi_ref, dbr_ref, dbi_ref, dzr_ref, dzi_ref):
        zr, zi, br, bi, gr, gi = zr_ref[...], zi_ref[...], br_ref[...], bi_ref[...], gr_ref[...], gi_ref[...]
        dbr_ref[...] = zr * gr + zi * gi
        dbi_ref[...] = zr * gi - zi * gr
        dzr_ref[...] = jnp.sum(br * gr + bi * gi, axis=-1, keepdims=True)
        dzi_ref[...] = jnp.sum(br * gi - bi * gr, axis=-1, keepdims=True)

    return pl.pallas_call(body, out_shape=[SDS(b_re.shape, F32)] * 2 + [SDS(z_re.shape, F32)] * 2,
                          name="ssm_scale_b_bwd")(z_re, z_im, b_re, b_im, g_re, g_im)


def _ssm_disc_bwd(a_re, a_im, log_dt, gb_re, gb_im, gz_re, gz_im):
    def body(lr_ref, li_ref, ldt_ref, gbr_ref, gbi_ref, gzr_ref, gzi_ref, dar_ref, dai_ref, dldt_ref):
        lr, li = lr_ref[...], li_ref[...]
        dt = jnp.exp(ldt_ref[...])
        mag = jnp.exp(lr * dt)
        bar_re, bar_im = mag * jnp.cos(li * dt), mag * jnp.sin(li * dt)
        nr, ni = bar_re - 1.0, bar_im
        den = lr * lr + li * li
        zr, zi = (nr * lr + ni * li) / den, (ni * lr - nr * li) / den
        gzr, gzi = gzr_ref[...], gzi_ref[...]
        gbr = gbr_ref[...] + (lr * gzr - li * gzi) / den
        gbi = gbi_ref[...] + (lr * gzi + li * gzr) / den
        qr, qi = (zr * lr + zi * li) / den, (zi * lr - zr * li) / den
        dar_ref[...] = dt * (bar_re * gbr + bar_im * gbi) - qr * gzr - qi * gzi
        dai_ref[...] = dt * (bar_re * gbi - bar_im * gbr) - qr * gzi + qi * gzr
        wr, wi = lr * bar_re - li * bar_im, lr * bar_im + li * bar_re
        dldt_ref[...] = dt * jnp.sum(wr * gbr + wi * gbi, axis=-1, keepdims=True)

    return pl.pallas_call(body, out_shape=[SDS(a_re.shape, F32)] * 2 + [SDS(log_dt.shape, F32)],
                          name="ssm_discretise_bwd")(a_re, a_im, log_dt, gb_re, gb_im, gz_re, gz_im)


def _interleave_epilogue(prods, extra_refs, out_refs, scratch_refs):
    uv = prods[0]
    tmp = scratch_refs[0]
    n = uv.shape[0] // N_DEV
    for b in range(SSM_W // BLK):
        cs = slice(b * BLK, (b + 1) * BLK)
        for j in range(N_DEV):
            tmp[b, pl.ds(j, n, stride=N_DEV), :] = uv[j * n:(j + 1) * n, cs]
        out_refs[0][:, cs] = tmp[b]
        out_refs[1][:, cs] = tmp[b].astype(BF16)


def _drive(src_ref, mat_ref, dst, mode):
    for kn in range(2 * SSM_NB):
        n = kn % SSM_NB
        a = src_ref[:, n * BLK:(n + 1) * BLK]
        dst[:, kn * 512:(kn + 1) * 512] = lax.dot_general(a, mat_ref[kn], _DNUMS[mode], preferred_element_type=F32)


def _scan_chunk(src, lam_ref, carry, *, reverse, store=None, h_ref=None, acc=None):
    steps = src.shape[0] // 8
    for c in range(NSTATE // SCAN_LANES):
        re = slice(c * SCAN_LANES, (c + 1) * SCAN_LANES)
        im = slice(NSTATE + c * SCAN_LANES, NSTATE + (c + 1) * SCAN_LANES)
        ar, ai = lam_ref[:, re], lam_ref[:, im]

        def step(s, val):
            i = (steps - 1 - s) if reverse else s
            rows = pl.ds(pl.multiple_of(i * 8, 8), 8)
            if acc is not None:
                hr, hi, dr, di = val
                pr, pi = h_ref[rows, re], h_ref[rows, im]
                dr = dr + hr * pr + hi * pi
                di = di + hi * pr - hr * pi
            else:
                hr, hi = val
            nr = ar * hr - ai * hi + src[rows, re]
            ni = ar * hi + ai * hr + src[rows, im]
            if store is not None:
                store[rows, re] = nr
                store[rows, im] = ni
            return (nr, ni, dr, di) if acc is not None else (nr, ni)

        init = (carry[:, re], carry[:, im])
        if acc is not None:
            init = init + (acc[:, re], acc[:, im])
        out = lax.fori_loop(0, steps, step, init, unroll=4)
        carry[:, re], carry[:, im] = out[0], out[1]
        if acc is not None:
            acc[:, re], acc[:, im] = out[2], out[3]


def _segment_carries(e_ref, pw_ref, out_ref, reverse):
    pr, pi = pw_ref[:, 0:NSTATE], pw_ref[:, NSTATE:]
    hr = jnp.zeros((1, NSTATE), F32)
    hi = jnp.zeros((1, NSTATE), F32)
    order = range(N_DEV - 1, -1, -1) if reverse else range(N_DEV)
    for j in order:
        out_ref[j:j + 1, 0:NSTATE] = hr
        out_ref[j:j + 1, NSTATE:] = hi
        tr, ti = _cmul(pr, pi, hr, hi)
        hr, hi = e_ref[j:j + 1, 0:NSTATE] + tr, e_ref[j:j + 1, NSTATE:] + ti


def _ssm_carries(name, src, mat, mode, lam8, pw, reverse):
    t = src.shape[0]
    nchunk = t // SCAN_ROWS

    def body(src_ref, mat_ref, lam_ref, pw_ref, out_ref, drive, carry):
        c = pl.program_id(0)

        @pl.when(c == 0)
        def _():
            carry[...] = jnp.zeros_like(carry)

        _drive(src_ref, mat_ref, drive, mode)
        _scan_chunk(drive, lam_ref, carry, reverse=reverse)

        @pl.when(c == nchunk - 1)
        def _():
            _segment_carries(carry, pw_ref, out_ref, reverse)

    blk = (lambda c: (nchunk - 1 - c, 0)) if reverse else (lambda c: (c, 0))
    return pl.pallas_call(
        body, grid=(nchunk,),
        in_specs=[_bs((SCAN_ROWS, SSM_W), blk), _bs(mat.shape, lambda c: (0, 0, 0)), _bs((8, 2 * NSTATE), lambda c: (0, 0)),
                  _bs((1, 2 * NSTATE), lambda c: (0, 0))],
        out_specs=_bs((8, 2 * NSTATE), lambda c: (0, 0)), out_shape=SDS((8, 2 * NSTATE), F32),
        scratch_shapes=[pltpu.VMEM((SCAN_ROWS, 2 * NSTATE), F32), pltpu.VMEM((8, 2 * NSTATE), F32)],
        compiler_params=_cp(1), name=name)(src, mat, lam8, pw)


def _ssm_fwd(u_bf, u, d_skip, bd, cd, lam8, start):
    t = u_bf.shape[0]
    nchunk = t // SCAN_ROWS
    per_seg = SCAN_ROWS // N_DEV

    def body(ub_ref, u_ref, d_ref, bd_ref, cd_ref, lam_ref, start_ref, h_ref, ys_ref, yg_ref, drive, carry, tmp):
        @pl.when(pl.program_id(0) == 0)
        def _():
            carry[...] = start_ref[...]

        _drive(ub_ref, bd_ref, drive, "nn")
        _scan_chunk(drive, lam_ref, carry, reverse=False, store=h_ref)
        for n in range(SSM_NB):
            cs = slice(n * BLK, (n + 1) * BLK)
            hr = h_ref[:, n * 512:(n + 1) * 512].astype(BF16)
            hi = h_ref[:, NSTATE + n * 512:NSTATE + (n + 1) * 512].astype(BF16)
            ys = (jnp.dot(hr, cd_ref[n], preferred_element_type=F32) + jnp.dot(hi, cd_ref[SSM_NB + n], preferred_element_type=F32)
                  + d_ref[:, cs] * u_ref[:, cs])
            ys_ref[:, cs] = ys
            tmp[n] = _gelu_parts(ys)[0]
            for j in range(N_DEV):
                yg_ref[j, :, cs] = tmp[n, pl.ds(j, per_seg, stride=N_DEV), :].astype(BF16)

    row = _bs((SCAN_ROWS, SSM_W), lambda c: (c, 0))
    h, ys, yg = pl.pallas_call(
        body, grid=(nchunk,),
        in_specs=[row, row, _bs((1, SSM_W), lambda c: (0, 0)), _bs(bd.shape, lambda c: (0, 0, 0)), _bs(cd.shape, lambda c: (0, 0, 0)),
                  _bs((8, 2 * NSTATE), lambda c: (0, 0)), _bs((8, 2 * NSTATE), lambda c: (0, 0))],
        out_specs=[_bs((SCAN_ROWS, 2 * NSTATE), lambda c: (c, 0)), row, _bs((N_DEV, per_seg, SSM_W), lambda c: (0, c, 0))],
        out_shape=[SDS((t, 2 * NSTATE), F32), SDS((t, SSM_W), F32), SDS((N_DEV, t // N_DEV, SSM_W), BF16)],
        scratch_shapes=[pltpu.VMEM((SCAN_ROWS, 2 * NSTATE), F32), pltpu.VMEM((8, 2 * NSTATE), F32),
                        pltpu.VMEM((SSM_NB, SCAN_ROWS, BLK), F32)],
        compiler_params=_cp(1), name="ssm_scan_fwd")(u_bf, u, d_skip, bd, cd, lam8, start)
    return h, ys, yg.reshape(t, SSM_W)


def _ssm_bwd(dys_bf, dys, d_skip, u_bf, h, bd, cd, lamc8, start):
    t = u_bf.shape[0]
    nchunk = t // SCAN_ROWS
    per_seg = SCAN_ROWS // N_DEV

    def body(dys_ref, dysf_ref, d_ref, u_ref, h_ref, bd_ref, cd_ref, lam_ref, start_ref, du_ref, dlam_ref, dbd_ref, dcd_ref,
             drive, adj, carry, tmp):
        c = pl.program_id(0)

        @pl.when(c == 0)
        def _():
            carry[...] = start_ref[...]
            dlam_ref[...] = jnp.zeros_like(dlam_ref)
            dbd_ref[...] = jnp.zeros_like(dbd_ref)
            dcd_ref[...] = jnp.zeros_like(dcd_ref)

        _drive(dys_ref, cd_ref, drive, "nt")
        _scan_chunk(drive, lam_ref, carry, reverse=True, store=adj, h_ref=h_ref, acc=dlam_ref)
        for n in range(SSM_NB):
            cs = slice(n * BLK, (n + 1) * BLK)
            acc = None
            for k in range(2):
                kn = k * SSM_NB + n
                ss = slice(kn * 512, (kn + 1) * 512)
                lam_b = adj[:, ss].astype(BF16)
                part = lax.dot_general(lam_b, bd_ref[kn], _DNUMS["nt"], preferred_element_type=F32)
                acc = part if acc is None else acc + part
                dbd_ref[kn] += lax.dot_general(u_ref[:, cs], lam_b, _DNUMS["tn"], preferred_element_type=F32)
                dcd_ref[kn] += lax.dot_general(h_ref[:, ss].astype(BF16), dys_ref[:, cs], _DNUMS["tn"],
                                               preferred_element_type=F32)
            tmp[n] = acc + d_ref[:, cs] * dysf_ref[:, cs]
            for j in range(N_DEV):
                du_ref[j, :, cs] = tmp[n, pl.ds(j, per_seg, stride=N_DEV), :].astype(BF16)

    rev = lambda c: (nchunk - 1 - c, 0)
    const2 = lambda c: (0, 0)
    const3 = lambda c: (0, 0, 0)
    row = _bs((SCAN_ROWS, SSM_W), rev)
    du, dlam, dbd, dcd = pl.pallas_call(
        body, grid=(nchunk,),
        in_specs=[row, row, _bs((1, SSM_W), const2), row, _bs((SCAN_ROWS, 2 * NSTATE), rev),
                  _bs(bd.shape, const3), _bs(cd.shape, const3), _bs((8, 2 * NSTATE), const2), _bs((8, 2 * NSTATE), const2)],
        out_specs=[_bs((N_DEV, per_seg, SSM_W), lambda c: (0, nchunk - 1 - c, 0)), _bs((8, 2 * NSTATE), const2),
                   _bs(bd.shape, const3), _bs(cd.shape, const3)],
        out_shape=[SDS((N_DEV, t // N_DEV, SSM_W), BF16), SDS((8, 2 * NSTATE), F32), SDS(bd.shape, F32), SDS(cd.shape, F32)],
        scratch_shapes=[pltpu.VMEM((SCAN_ROWS, 2 * NSTATE), F32), pltpu.VMEM((SCAN_ROWS, 2 * NSTATE), F32),
                        pltpu.VMEM((8, 2 * NSTATE), F32), pltpu.VMEM((SSM_NB, SCAN_ROWS, BLK), F32)],
        compiler_params=_cp(1), name="ssm_scan_bwd")(dys_bf, dys, d_skip, u_bf, h, bd, cd, lamc8, start)
    return du.reshape(t, SSM_W), dlam, dbd, dcd


def _gelu_parts(x):
    c0 = math.sqrt(2.0 / math.pi)
    inner = c0 * (x + 0.044715 * x * x * x)
    th = jnp.tanh(inner)
    val = 0.5 * x * (1.0 + th)
    grad = 0.5 * (1.0 + th) + 0.5 * x * (1.0 - th * th) * c0 * (1.0 + 3.0 * 0.044715 * x * x)
    return val, grad


def _ssm_out_bwd(d_yg, ys, u, tm):
    t = u.shape[0]
    seg = t // N_DEV

    def body(dg_ref, ys_ref, u_ref, dys_ref, dysb_ref, dd_ref, tmp):
        for n in range(SSM_W // BLK):
            for j in range(N_DEV):
                tmp[n, pl.ds(j, tm // N_DEV, stride=N_DEV), :] = dg_ref[j, :, n * BLK:(n + 1) * BLK]
        dyg = jnp.concatenate([tmp[n] for n in range(SSM_W // BLK)], axis=1)
        dys = dyg * _gelu_parts(ys_ref[...])[1]
        dys_ref[...] = dys
        dysb_ref[...] = dys.astype(BF16)
        part = jnp.sum(dys * u_ref[...], axis=0, keepdims=True)

        @pl.when(pl.program_id(0) == 0)
        def _():
            dd_ref[...] = part

        @pl.when(pl.program_id(0) > 0)
        def _():
            dd_ref[...] += part

    row = _bs((tm, SSM_W), lambda i: (i, 0))
    return pl.pallas_call(
        body, grid=(t // tm,), in_specs=[_bs((N_DEV, tm // N_DEV, SSM_W), lambda i: (0, i, 0)), row, row],
        out_specs=[row, row, _bs((1, SSM_W), lambda i: (0, 0))],
        out_shape=[SDS((t, SSM_W), F32), SDS((t, SSM_W), BF16), SDS((1, SSM_W), F32)],
        scratch_shapes=[pltpu.VMEM((SSM_W // BLK, tm, BLK), F32)], compiler_params=_cp(1), name="ssm_out_bwd")(
            d_yg.reshape(N_DEV, seg, SSM_W), ys, u)


def _block_diag(blocks):
    nb, ng, r, c = blocks.shape
    eye = jnp.eye(ng, dtype=blocks.dtype)
    return (blocks[:, :, :, None, :] * eye[None, :, None, :, None]).reshape(nb, ng * r, ng * c)


def _diag_blocks(full, r, c):
    k, nb = full.shape[:2]
    ng = full.shape[2] // r
    x = full.reshape(k, nb, ng, r, ng, c)
    eye = jnp.eye(ng, dtype=full.dtype)
    return jnp.sum(x * eye[None, None, :, None, :, None], axis=4).reshape(k, nb * ng, r, c)


_SMALL = ("a_re", "a_im", "log_dt", "b_re", "b_im", "c_re", "c_im", "d_skip", "g_ffn", "g_final")


def _pack_small(arrs):
    flat = jnp.concatenate([a.reshape(-1) for a in arrs])
    pad = (-flat.shape[0]) % (8 * 128)
    return jnp.pad(flat, (0, pad)).reshape(-1, 128)


def _unpack_small(packed, shapes):
    flat = packed.reshape(-1)
    out, off = [], 0
    for s in shapes:
        n = math.prod(s)
        out.append(flat[off:off + n].reshape(s))
        off += n
    return out


def kernel(x, p, positions, g_mix, w_in, a_re, a_im, log_dt, b_re, b_im, c_re, c_im, d_skip, w_attn_proj, w_glu_a, w_glu_b, w_out, g_ffn, w_ffn_gate, w_ffn_up, w_ffn_down, w_ple_gate, w_ple_proj, g_final, loss_target, m_g_mix, m_w_in, m_a_re, m_a_im, m_log_dt, m_b_re, m_b_im, m_c_re, m_c_im, m_d_skip, m_w_attn_proj, m_w_glu_a, m_w_glu_b, m_w_out, m_g_ffn, m_w_ffn_gate, m_w_ffn_up, m_w_ffn_down, m_w_ple_gate, m_w_ple_proj, m_g_final, v_g_mix, v_w_in, v_a_re, v_a_im, v_log_dt, v_b_re, v_b_im, v_c_re, v_c_im, v_d_skip, v_w_attn_proj, v_w_glu_a, v_w_glu_b, v_w_out, v_g_ffn, v_w_ffn_gate, v_w_ffn_up, v_w_ffn_down, v_w_ple_gate, v_w_ple_proj, v_g_final):
    args = dict(locals())
    t, d = x.shape[1], x.shape[2]
    inw = w_in.shape[2] * N_DEV
    fs = w_ffn_gate.shape[2]
    ff = fs * N_DEV
    ple = w_ple_proj.shape[1]
    seg = t // N_DEV
    assert inw == 3 * QK_W + SSM_W + 2 * d and t % (N_DEV * SCAN_ROWS // 8) == 0 and seg & (seg - 1) == 0
    tm = min(1024, t)
    te = min(512, t)
    tk = min(2048, t)
    ucol = (3 * QK_W) // SSM_W
    gcol = (3 * QK_W + SSM_W) // d
    assert (3 * QK_W + SSM_W) % d == 0

    x2, p2, tgt = x[0], p[0, 0], loss_target[0]
    pos = positions.reshape(t, 1)
    inv = ROPE_THETA ** (-jnp.arange(ROPE_HALF, dtype=F32) * 2.0 / ROPE_DIM)
    invf = jnp.concatenate([inv, inv, jnp.zeros((HEAD_DIM - ROPE_DIM,), F32)]).reshape(1, HEAD_DIM)

    wnames = ("w_in", "w_attn_proj", "w_glu_a", "w_glu_b", "w_out", "w_ffn_gate", "w_ffn_up", "w_ffn_down", "w_ple_gate",
              "w_ple_proj")
    kinds = ("cols", "cols", "cols", "cols", "rows", "slot", "slot", "rows", "rows", "cols")
    shards = [args[n][0].astype(BF16) for n in wnames]
    sizes = [s.shape[0] if k == "rows" else s.shape[-1] for s, k in zip(shards, kinds)]
    ag = _exchange_start("gather_weights_start", shards, kinds, sizes, True)

    row_d = _bs((tm, d), lambda i, j, k: (i, 0))
    row_e = _bs((te, d), lambda i, j, k: (i, 0))
    vec_d = _bs((1, d), lambda i, j, k: (0, 0))
    sq_w = _bs((d, d), lambda i, j, k: (0, 0))
    n1 = _rms_fwd("norm_mix", x2, g_mix + ag[3][0:1, 0:1], tm)
    W_in, = _exchange_wait("gather_w_in_wait", ag, [0], kinds, sizes, True, n1)
    qkv = _mm("qkv_proj", (t // tm, 3, 1), [("nn", n1, row_d, W_in, _bs((d, QK_W), lambda i, j, k: (0, j)))],
              [(SDS((3, t // dil, dil * GROUP_W), BF16), _bs((None, tm // dil, dil * GROUP_W), lambda i, j, k: (j, i, 0)))
               for dil in DILATIONS],
              extras=[(pos, _bs((tm, 1), lambda i, j, k: (i, 0))), (invf, _bs((1, HEAD_DIM), lambda i, j, k: (0, 0)))],
              epilogue=_rope_dilate_epilogue(tm),
              scratch=[pltpu.VMEM((tm, HEAD_DIM), F32)] * 2 + [pltpu.VMEM((QK_W // HEAD_DIM, tm, HEAD_DIM), F32)])
    row_s = _bs((tm, SSM_W), lambda i, j, k: (i, 0))
    u_perm, u_bf = _mm("u_proj", (t // tm, 1, 1),
                       [("nn", n1.reshape(N_DEV, seg, d), _bs((N_DEV, tm // N_DEV, d), lambda i, j, k: (0, i, 0)), W_in,
                         _bs((d, SSM_W), lambda i, j, k: (0, ucol)))],
                       [(SDS((t, SSM_W), F32), row_s), (SDS((t, SSM_W), BF16), row_s)], epilogue=_interleave_epilogue,
                       scratch=[pltpu.VMEM((SSM_W // BLK, tm, BLK), F32)])
    zg, = _mm("z_gates", (t // tm, 2, 1),
              [("nn", n1, row_d, W_in, _bs((d, d), lambda i, j, k: (0, gcol + j)))],
              [(SDS((t, 2 * d), BF16), _bs((tm, d), lambda i, j, k: (i, j)))])

    outs, lses = [], []
    for g, dil in enumerate(DILATIONS):
        o_g, l_g = _attn_fwd(qkv[g], dil, min(1024, t // dil))
        outs.append(o_g)
        lses.append(l_g)
    merged = _attn_merge(outs, lses, tm)
    attn, attn_bf, lts = merged[0], merged[1], merged[2:]

    nsq = seg.bit_length() - 1
    bar_re, bar_im, z_re, z_im, pw_re, pw_im = _ssm_disc(a_re[0], a_im[0], log_dt.reshape(SSM_GROUPS, 1), nsq)
    gp = SSM_GROUPS * SSM_STATE
    b_re2, b_im2 = b_re.reshape(gp, SSM_GROUP), b_im.reshape(gp, SSM_GROUP)
    bb_re, bb_im = _ssm_scale_b(z_re.reshape(gp, 1), z_im.reshape(gp, 1), b_re2, b_im2)

    def chunks(a, r, c):
        return a.reshape(SSM_NB, SSM_GROUPS // SSM_NB, r, c)

    bbt = lambda a: jnp.swapaxes(a.reshape(SSM_GROUPS, SSM_STATE, SSM_GROUP), 1, 2)
    bd = jnp.concatenate([_block_diag(chunks(bbt(bb_re), SSM_GROUP, SSM_STATE)),
                          _block_diag(chunks(bbt(bb_im), SSM_GROUP, SSM_STATE))]).astype(BF16)
    ct = lambda a: jnp.swapaxes(a[0], 1, 2)
    cd = jnp.concatenate([_block_diag(chunks(ct(c_re), SSM_STATE, SSM_GROUP)),
                          _block_diag(chunks(-ct(c_im), SSM_STATE, SSM_GROUP))]).astype(BF16)
    lam = jnp.concatenate([bar_re.reshape(1, gp), bar_im.reshape(1, gp)], axis=1)
    lamc = jnp.concatenate([bar_re.reshape(1, gp), -bar_im.reshape(1, gp)], axis=1)
    pw = jnp.concatenate([pw_re.reshape(1, gp), pw_im.reshape(1, gp)], axis=1)
    pwc = jnp.concatenate([pw_re.reshape(1, gp), -pw_im.reshape(1, gp)], axis=1)
    lam8, lamc8 = jnp.broadcast_to(lam, (8, 2 * gp)), jnp.broadcast_to(lamc, (8, 2 * gp))

    start_f = _ssm_carries("ssm_carries_fwd", u_bf, bd, "nn", lam8, pw, False)
    dsk = d_skip.reshape(1, SSM_W)
    h_all, ys, yg_bf = _ssm_fwd(u_bf, u_perm, dsk, bd, cd, lam8, start_f)
    W_ap, W_ga, W_gb, W_out, W_fg, W_fu, W_fd, W_pg, W_pp = _exchange_wait(
        "gather_rest_wait", ag, list(range(1, len(wnames))), kinds, sizes, True, yg_bf)
    W_fg = jnp.swapaxes(W_fg, 0, 1).reshape(d, ff)
    W_fu = jnp.swapaxes(W_fu, 0, 1).reshape(d, ff)

    glu_w = _bs((SSM_W, d), lambda i, j, k: (0, 0))
    row_s = _bs((tm, SSM_W), lambda i, j, k: (i, 0))
    gate_a = _bs((te, d), lambda i, j, k: (i, 0))
    gate_s = _bs((te, d), lambda i, j, k: (i, 1))
    td_f32, td_bf = SDS((t, d), F32), SDS((t, d), BF16)
    m_bf, ya, yb, attn_d = _mm(
        "glu_merge", (t // tm, 1, 1),
        [("nn", yg_bf, row_s, W_ga, glu_w), ("nn", yg_bf, row_s, W_gb, glu_w), ("nn", attn_bf, row_s, W_ap, glu_w)],
        [(td_bf, row_d)] * 4, extras=[(zg, row_d), (zg, _bs((tm, d), lambda i, j, k: (i, 1)))], epilogue=_glu_merge_epilogue)

    h1, n2 = _mm("out_proj", (t // tm, 1, 1), [("nn", m_bf, row_d, W_out, sq_w)], [(td_f32, row_d), (td_bf, row_d)],
                 extras=[(x2, row_d), (g_ffn, vec_d)], epilogue=_out_norm_epilogue)

    tn_f = ff // 2
    nf = ff // tn_f
    hid_o = _bs((tm, tn_f), lambda j, i, k: (i, j))
    tf_bf = SDS((t, ff), BF16)
    a_rows = _bs((tm, d), lambda j, i, k: (i, 0))
    w_cols = _bs((d, tn_f), lambda j, i, k: (0, j))
    act, fg, fu = _mm("ffn_gate_up", (nf, t // tm, 1), [("nn", n2, a_rows, W_fg, w_cols), ("nn", n2, a_rows, W_fu, w_cols)],
                      [(tf_bf, hid_o)] * 3, epilogue=_swiglu_epilogue)
    w_once = pl.BlockSpec((d, d), lambda i, j, k: (0, 0), pipeline_mode=pl.Buffered(1))
    loss_part, dg_final, dh2, dh2_bf, dpp_bf, dpg_bf, h2_bf = _mm(
        "ffn_down_head", (t // te, 1, 1),
        [("nn", act, _bs((te, ff), lambda i, j, k: (i, 0)), W_fd,
          pl.BlockSpec((ff, d), lambda i, j, k: (0, 0), pipeline_mode=pl.Buffered(1))),
         ("nn", p2, _bs((te, ple), lambda i, j, k: (i, 0)), W_pp, _bs((ple, d), lambda i, j, k: (0, 0)))],
        [(SDS((1, 1), F32), _bs((1, 1), lambda i, j, k: (0, 0))), (SDS((1, d), F32), vec_d), (td_f32, row_e), (td_bf, row_e),
         (td_bf, row_e), (td_bf, row_e), (td_bf, row_e)],
        extras=[(h1, row_e), (g_final.reshape(1, d), vec_d), (tgt, row_e), (W_pg, w_once)], epilogue=_head_epilogue(t // te),
        scratch=[pltpu.VMEM((1, d), F32)])
    loss = lax.psum(loss_part[0, 0], ("x", "y", "c"))

    nkt = t // tk
    tok_a = lambda w: _bs((tk, w), lambda i, j, k: (k, 0))

    def wgrad(name, a, wa, b, wb):
        return _mm(name, (1, 1, nkt), [("tn", a, tok_a(wa), b, tok_a(wb))],
                   [(SDS((wa, wb), BF16), _bs((wa, wb), lambda i, j, k: (0, 0)))])[0]

    dW_pp = wgrad("dw_ple_proj", p2, ple, dpp_bf, d)
    dW_pg = wgrad("dw_ple_gate", h2_bf, d, dpg_bf, d)
    dfg_bf, dfu_bf = _mm("d_ffn_down", (nf, t // tm, 1),
                         [("nt", dh2_bf, a_rows, W_fd, _bs((tn_f, d), lambda j, i, k: (j, 0)))],
                         [(tf_bf, hid_o), (tf_bf, hid_o)], extras=[(fg, hid_o), (fu, hid_o)], epilogue=_swiglu_bwd_epilogue)
    dW_fd, = _mm("dw_ffn_down", (nf, 1, nkt), [("tn", act, _bs((tk, tn_f), lambda i, j, k: (k, i)), dh2_bf, tok_a(d))],
                 [(SDS((ff, d), BF16), _bs((tn_f, d), lambda i, j, k: (i, 0)))])
    hid_t = _bs((tk, tn_f), lambda i, j, k: (k, j))
    wg_o = [(SDS((d, ff), BF16), _bs((d, tn_f), lambda i, j, k: (0, j)))]
    dW_fg, = _mm("dw_ffn_gate", (1, nf, nkt), [("tn", n2, tok_a(d), dfg_bf, hid_t)], wg_o)
    dW_fu, = _mm("dw_ffn_up", (1, nf, nkt), [("tn", n2, tok_a(d), dfu_bf, hid_t)], wg_o)
    dW_fg = jnp.swapaxes(dW_fg.reshape(d, N_DEV, fs), 0, 1)
    dW_fu = jnp.swapaxes(dW_fu.reshape(d, N_DEV, fs), 0, 1)
    group = lambda names: ([kinds[wnames.index(n)] for n in names], [sizes[wnames.index(n)] for n in names])
    ffn_names = ("w_ffn_gate", "w_ffn_up", "w_ffn_down", "w_ple_gate", "w_ple_proj")
    rs_ffn = _exchange_start("scatter_ffn_start", [dW_fg, dW_fu, dW_fd, dW_pg, dW_pp], *group(ffn_names), False)
    hid_all = _bs((te, ff), lambda i, j, k: (i, 0))
    w_all = pl.BlockSpec((d, ff), lambda i, j, k: (0, 0), pipeline_mode=pl.Buffered(1))
    dh1, dh1_bf, dg_ffn = _mm("d_ffn_gate_up", (t // te, 1, 1),
                              [("nt", dfg_bf, hid_all, W_fg, w_all), ("nt", dfu_bf, hid_all, W_fu, w_all)],
                              [(td_f32, row_e), (td_bf, row_e), (SDS((1, d), F32), vec_d)],
                              extras=[(h1, row_e), (g_ffn, vec_d), (dh2, row_e)], epilogue=_rms_bwd_epilogue, after=rs_ffn[3])

    dW_out = wgrad("dw_out", m_bf, d, dh1_bf, d)
    glu_once = pl.BlockSpec((SSM_W, d), lambda i, j, k: (0, 0), pipeline_mode=pl.Buffered(1))
    row_es = _bs((te, SSM_W), lambda i, j, k: (i, 0))
    ts_f32 = SDS((t, SSM_W), F32)
    dz_g, dad_bf, dya_bf, dyb_bf, d_yg, d_attn = _mm(
        "d_out_proj", (t // te, 1, 1), [("nt", dh1_bf, row_e, W_out, w_once)],
        [(SDS((t, 2 * d), BF16), _bs((te, 2 * d), lambda i, j, k: (i, 0))), (td_bf, row_e), (td_bf, row_e), (td_bf, row_e),
         (ts_f32, row_es), (ts_f32, row_es)],
        extras=[(zg, gate_a), (zg, gate_s), (attn_d, row_e), (ya, row_e), (yb, row_e), (W_ga, glu_once), (W_gb, glu_once),
                (W_ap, glu_once)], epilogue=_merge_bwd_epilogue)

    dW_ga = wgrad("dw_glu_a", yg_bf, SSM_W, dya_bf, d)
    dW_gb = wgrad("dw_glu_b", yg_bf, SSM_W, dyb_bf, d)
    dys, dys_bf, dd_skip = _ssm_out_bwd(d_yg, ys, u_perm, te)
    start_b = _ssm_carries("ssm_carries_bwd", dys_bf, cd, "nt", lamc8, pwc, True)
    dz_u, dlam8, dbd, dcd = _ssm_bwd(dys_bf, dys, dsk, u_bf, h_all, bd, cd, lamc8, start_b)
    dlam = jnp.sum(dlam8, axis=0)
    dbb = _diag_blocks(dbd.reshape(2, SSM_NB, BLK, 512), SSM_GROUP, SSM_STATE)
    dbb_re = jnp.swapaxes(dbb[0], 1, 2).reshape(gp, SSM_GROUP)
    dbb_im = jnp.swapaxes(dbb[1], 1, 2).reshape(gp, SSM_GROUP)
    dcc = _diag_blocks(dcd.reshape(2, SSM_NB, 512, BLK), SSM_STATE, SSM_GROUP)
    dc_re, dc_im = jnp.swapaxes(dcc[0], 1, 2), -jnp.swapaxes(dcc[1], 1, 2)
    db_re, db_im, dz_re, dz_im = _ssm_scale_b_bwd(z_re.reshape(gp, 1), z_im.reshape(gp, 1), b_re2, b_im2, dbb_re, dbb_im)
    gshape = (SSM_GROUPS, SSM_STATE)
    da_re, da_im, dlog_dt = _ssm_disc_bwd(a_re[0], a_im[0], log_dt.reshape(SSM_GROUPS, 1), dlam[:gp].reshape(gshape),
                                          dlam[gp:].reshape(gshape), dz_re.reshape(gshape), dz_im.reshape(gshape))

    dW_ap = wgrad("dw_attn_proj", attn_bf, GROUP_W, dad_bf, d)
    pre = _attn_bwd_pre(d_attn, attn, tm)
    das, deltas = pre[:N_GROUPS], pre[N_GROUPS:]
    dqkvs = [_attn_bwd(qkv[g], das[g], lts[g], deltas[g], dil, min(1024, t // dil)) for g, dil in enumerate(DILATIONS)]
    dz_qkv = _undilate_rope_bwd(dqkvs, pos, invf, tm)

    dW_in, = _mm("dw_in_qkv", (1, 3, nkt), [("tn", n1, tok_a(d), dz_qkv, _bs((tk, QK_W), lambda i, j, k: (k, j)))],
                 [(SDS((d, inw), BF16), _bs((d, QK_W), lambda i, j, k: (0, j)))])
    dW_in, = _mm("dw_in_u", (1, 1, nkt), [("tn", n1, tok_a(d), dz_u, tok_a(SSM_W))],
                 [(SDS((d, inw), BF16), _bs((d, SSM_W), lambda i, j, k: (0, ucol)))], alias_to_out0=dW_in)
    dW_in, = _mm("dw_in_gates", (1, 2, nkt), [("tn", n1, tok_a(d), dz_g, _bs((tk, d), lambda i, j, k: (k, j)))],
                 [(SDS((d, inw), BF16), _bs((d, d), lambda i, j, k: (0, gcol + j)))], alias_to_out0=dW_in)
    small_parts = dict(a_re=da_re, a_im=da_im, log_dt=dlog_dt, b_re=db_re, b_im=db_im, c_re=dc_re, c_im=dc_im,
                       d_skip=dd_skip, g_ffn=dg_ffn, g_final=dg_final)
    small = _pack_small([small_parts[n] for n in _SMALL])
    rest_names = ("w_in", "w_attn_proj", "w_glu_a", "w_glu_b", "w_out")
    rest_kinds, rest_sizes = group(rest_names)
    rs_in = _exchange_start("scatter_rest_start", [dW_in, dW_ap, dW_ga, dW_gb, dW_out, small], rest_kinds + ["all"],
                            rest_sizes + [0], False)
    w_piece = lambda w, cb: pl.BlockSpec((d, w), lambda i, j, k: (0, cb), pipeline_mode=pl.Buffered(1))
    dx, dg_mix = _mm(
        "d_z_proj", (t // te, 1, 1),
        [("nt", dz_qkv, _bs((te, 3 * QK_W), lambda i, j, k: (i, 0)), W_in, w_piece(3 * QK_W, 0)),
         ("nt", dz_u, _bs((te, SSM_W), lambda i, j, k: (i, 0)), W_in, w_piece(SSM_W, ucol)),
         ("nt", dz_g, _bs((te, d), lambda i, j, k: (i, 0)), W_in, w_piece(d, gcol)),
         ("nt", dz_g, _bs((te, d), lambda i, j, k: (i, 1)), W_in, w_piece(d, gcol + 1))],
        [(td_f32, row_e), (SDS((1, d), F32), vec_d)],
        extras=[(x2, row_e), (g_mix, vec_d), (dh1, row_e)], epilogue=_rms_bwd_epilogue, after=rs_in[3])

    received = dict(zip(ffn_names, _exchange_wait("scatter_ffn_wait", rs_ffn, list(range(len(ffn_names))), *group(ffn_names),
                                                  False, dx)))
    *landed, small_all = _exchange_wait("scatter_rest_wait", rs_in, list(range(len(rest_names) + 1)), rest_kinds + ["all"],
                                        rest_sizes + [0], False, dx)
    received.update(zip(rest_names, landed))

    new = {}
    for n in wnames:
        new[n] = [o.reshape(args[n].shape)
                  for o in _adamw("adamw_" + n, received[n], args[n][0], args["m_" + n][0], args["v_" + n][0])]
    g_mix_all = _gather_small(_pack_small([dg_mix]))
    pk = lambda pre: jnp.concatenate([_pack_small([args[pre + n] for n in _SMALL]), _pack_small([args[pre + "g_mix"]])])
    sm = _adamw("adamw_small", jnp.concatenate([small_all, g_mix_all], axis=1), pk(""), pk("m_"), pk("v_"))
    rows_a = small.shape[0]
    shapes = [args[n].shape for n in _SMALL]
    for n, vals in zip(_SMALL, zip(*[_unpack_small(o[:rows_a], shapes) for o in sm])):
        new[n] = list(vals)
    new["g_mix"] = [_unpack_small(o[rows_a:], [g_mix.shape])[0] for o in sm]

    order = ("g_mix", "w_in", "a_re", "a_im", "log_dt", "b_re", "b_im", "c_re", "c_im", "d_skip", "w_attn_proj", "w_glu_a",
             "w_glu_b", "w_out", "g_ffn", "w_ffn_gate", "w_ffn_up", "w_ffn_down", "w_ple_gate", "w_ple_proj", "g_final")
    return (loss, dx.reshape(x.shape), *[new[n][0] for n in order], *[new[n][1] for n in order],
            *[new[n][2] for n in order], *[new[n][3] for n in order])
```

```python
import functools
import math

import jax
import jax.numpy as jnp
from jax import lax
from jax.experimental import pallas as pl
from jax.experimental.pallas import tpu as pltpu

F32 = jnp.float32
BF16 = jnp.bfloat16
SDS = jax.ShapeDtypeStruct

N_DEV = 8
HEAD_DIM = 128
HEADS_PER_GROUP = 4
GROUP_W = HEADS_PER_GROUP * HEAD_DIM
DILATIONS = (1, 4, 16)
N_GROUPS = len(DILATIONS)
QK_W = N_GROUPS * GROUP_W
BLK = 128
ROPE_THETA = 500000.0
ROPE_DIM = HEAD_DIM // 4
ROPE_HALF = ROPE_DIM // 2
SSM_W = 512
SSM_GROUP = 16
SSM_GROUPS = SSM_W // SSM_GROUP
SSM_STATE = 64
NSTATE = SSM_GROUPS * SSM_STATE
SSM_NB = 4
EPS = 1e-6
ADAM_LR, ADAM_B1, ADAM_B2, ADAM_EPS, ADAM_WD, ADAM_STEP = 0.001, 0.9, 0.999, 1e-08, 0.01, 10
NEG = -1e30

VMEM_LIMIT = 52 * 1024 * 1024
SCAN_ROWS = 512
SCAN_LANES = 512


def _cp(n):
    return pltpu.CompilerParams(dimension_semantics=("arbitrary",) * n, vmem_limit_bytes=VMEM_LIMIT)


def _sigmoid(x):
    return 0.5 * jnp.tanh(0.5 * x) + 0.5


_DNUMS = {"nn": (((1,), (0,)), ((), ())), "nt": (((1,), (1,)), ((), ())), "tn": (((0,), (0,)), ((), ()))}


def _bs(shape, fn):
    return pl.BlockSpec(shape, fn)


def _store_all(prods, extra_refs, out_refs, scratch_refs):
    r = prods[0]
    for p in prods[1:]:
        r = r + p
    for e in extra_refs:
        r = r + e[...]
    for o in out_refs:
        o[...] = r.astype(o.dtype)


def _mm(name, grid, pairs, outs, extras=(), epilogue=_store_all, scratch=(), alias_to_out0=None, after=None):
    nk = grid[2]
    npair = len(pairs)
    steps = [p[5] if len(p) > 5 else nk for p in pairs]

    def block(spec):
        return tuple(s for s in spec.block_shape if s is not None)

    def rows2d(shape):
        return (math.prod(shape[:-1]), shape[-1]) if len(shape) == 3 else shape

    acc_shapes = [jax.eval_shape(lambda u, v, dn=_DNUMS[p[0]]: lax.dot_general(u, v, dn, preferred_element_type=F32),
                                 SDS(rows2d(block(p[2])), BF16), SDS(block(p[4]), BF16)).shape for p in pairs]
    if nk == 1:
        acc_shapes = []
    n_in = 2 * npair + len(extras) + (alias_to_out0 is not None) + (after is not None)

    def body(*refs):
        extra_refs = refs[2 * npair:2 * npair + len(extras)]
        out_refs = refs[n_in:n_in + len(outs)]
        rest = refs[n_in + len(outs):]
        acc_refs = rest[:len(acc_shapes)]
        scratch_refs = rest[len(acc_refs):]
        k = pl.program_id(2)

        def product(i):
            a = refs[2 * i][...]
            if a.ndim == 3:
                a = a.reshape(-1, a.shape[-1])
            return lax.dot_general(a.astype(BF16), refs[2 * i + 1][...].astype(BF16), _DNUMS[pairs[i][0]],
                                   preferred_element_type=F32)

        if nk == 1:
            epilogue([product(i) for i in range(npair)], extra_refs, out_refs, scratch_refs)
            return
        for i in range(npair):
            @pl.when(k == 0)
            def _(i=i):
                acc_refs[i][...] = product(i)

            @pl.when((k > 0) & (k < steps[i]))
            def _(i=i):
                acc_refs[i][...] += product(i)

        @pl.when(k == nk - 1)
        def _():
            epilogue([a[...] for a in acc_refs], extra_refs, out_refs, scratch_refs)

    ins, in_specs = [], []
    for p in pairs:
        ins += [p[1], p[3]]
        in_specs += [p[2], p[4]]
    ins += [e[0] for e in extras]
    in_specs += [e[1] for e in extras]
    aliases = {}
    if alias_to_out0 is not None:
        aliases = {len(ins): 0}
        ins.append(alias_to_out0)
        in_specs.append(pl.BlockSpec(memory_space=pl.ANY))
    if after is not None:
        ins.append(after)
        in_specs.append(pl.BlockSpec(memory_space=pl.ANY))
    scratch_shapes = [pltpu.VMEM(s, F32) for s in acc_shapes] + list(scratch)
    return pl.pallas_call(body, grid=grid, in_specs=in_specs, out_specs=[o[1] for o in outs], out_shape=[o[0] for o in outs],
                          scratch_shapes=scratch_shapes, input_output_aliases=aliases, compiler_params=_cp(3), name=name)(*ins)


def _my_index():
    return 4 * lax.axis_index("x") + 2 * lax.axis_index("y") + lax.axis_index("c")


def _peer(d):
    mx, my, mc = lax.axis_index("x"), lax.axis_index("y"), lax.axis_index("c")
    return (mx ^ ((d >> 2) & 1), my ^ ((d >> 1) & 1), mc ^ (d & 1))


def _win(ref, kind, j, n):
    if kind == "all":
        return ref
    if kind == "slot":
        return ref.at[j]
    if kind == "rows":
        return ref.at[pl.ds(pl.multiple_of(j * n, 8), n)]
    return ref.at[:, pl.ds(pl.multiple_of(j * n, 128), n)]


def _win7(ref, kind, n):
    if kind == "slot":
        return ref.at[pl.ds(0, 7)]
    if kind == "rows":
        return ref.at[pl.ds(0, 7 * n)]
    return ref.at[:, pl.ds(0, 7 * n)]


def _full_shape(shard_shape, kind):
    if kind == "slot":
        return (N_DEV,) + tuple(shard_shape)
    if kind == "rows":
        return (N_DEV * shard_shape[0],) + tuple(shard_shape[1:])
    return (shard_shape[0], N_DEV * shard_shape[1])


def _shard_shape(full_shape, kind, n):
    if kind == "all":
        return tuple(full_shape)
    if kind == "slot":
        return tuple(full_shape[1:])
    if kind == "rows":
        return (n,) + tuple(full_shape[1:])
    return (full_shape[0], n)


_HBM = pl.BlockSpec(memory_space=pltpu.HBM)
_SEM = pl.BlockSpec(memory_space=pltpu.SEMAPHORE)
_DATAFLOW = pltpu.SideEffectType.DATAFLOW_SIDE_EFFECTING


def _exchange_start(name, srcs, kinds, sizes, gather):
    n = len(srcs)
    if gather:
        lands = [lax.empty(_full_shape(s.shape, k), s.dtype) for s, k in zip(srcs, kinds)]
    else:
        lands = [lax.empty((N_DEV,) + _shard_shape(s.shape, k, z), s.dtype) for s, k, z in zip(srcs, kinds, sizes)]

    def body(*refs):
        src, land = refs[:n], refs[n:2 * n]
        send_sems, recv_sems, local_sems = refs[2 * n], refs[2 * n + 1], refs[2 * n + 2]
        token = refs[4 * n + 3]
        me = _my_index()
        for a in range(n):
            _local_copy(src[a], land[a], kinds[a], sizes[a], gather, me, local_sems.at[a]).start()
        for a in range(n):
            for d in range(1, N_DEV):
                px, py, pc = _peer(d)
                if gather:
                    s_ref, d_ref = src[a], _win(land[a], kinds[a], me, sizes[a])
                else:
                    s_ref, d_ref = _win(src[a], kinds[a], 4 * px + 2 * py + pc, sizes[a]), land[a].at[me]
                pltpu.make_async_remote_copy(src_ref=s_ref, dst_ref=d_ref, send_sem=send_sems.at[a], recv_sem=recv_sems.at[a],
                                             device_id=(px, py, pc), device_id_type=pl.DeviceIdType.MESH).start()
        token[...] = jnp.zeros_like(token)

    hbm = [pltpu.with_memory_space_constraint(a, pltpu.HBM) for a in list(srcs) + lands]
    out = pl.pallas_call(
        body, name=name, in_specs=[_HBM] * (2 * n),
        out_shape=[pltpu.SemaphoreType.DMA((n,))] * 3 + [pltpu.HBM(a.shape, a.dtype) for a in hbm] + [SDS((8, 128), F32)],
        out_specs=[_SEM] * 3 + [_HBM] * (2 * n) + [pl.BlockSpec(memory_space=pltpu.VMEM)],
        input_output_aliases={i: 3 + i for i in range(2 * n)},
        compiler_params=pltpu.CompilerParams(has_side_effects=_DATAFLOW))(*hbm)
    return out[0:3], out[3:3 + n], out[3 + n:3 + 2 * n], out[-1]


def _local_copy(src, land, kind, size, gather, me, sem):
    if gather:
        return pltpu.make_async_copy(src, _win(land, kind, me, size), sem)
    return pltpu.make_async_copy(_win(src, kind, me, size), land.at[me], sem)


def _exchange_wait(name, started, which, kinds, sizes, gather, after):
    sems, srcs, lands, _ = started
    n = len(which)

    def body(*refs):
        src, land = refs[:n], refs[n:2 * n]
        send_ref, recv_ref, local_ref = refs[2 * n:2 * n + 3]
        me = _my_index()
        my_id = (lax.axis_index("x"), lax.axis_index("y"), lax.axis_index("c"))
        for i, a in enumerate(which):
            seven = _win7(land[i], kinds[a], sizes[a]) if gather else land[i].at[pl.ds(0, 7)]
            pltpu.make_async_remote_copy(src_ref=seven, dst_ref=seven, send_sem=send_ref.at[a], recv_sem=recv_ref.at[a],
                                         device_id=my_id, device_id_type=pl.DeviceIdType.MESH).wait()
            _local_copy(src[i], land[i], kinds[a], sizes[a], gather, me, local_ref.at[a]).wait()

    hbm = [srcs[a] for a in which] + [lands[a] for a in which]
    out = pl.pallas_call(
        body, name=name, in_specs=[_HBM] * (2 * n) + [_SEM] * 3 + [pl.BlockSpec(memory_space=pl.ANY)],
        out_shape=[pltpu.HBM(a.shape, a.dtype) for a in hbm], out_specs=[_HBM] * (2 * n),
        input_output_aliases={i: i for i in range(2 * n)},
        compiler_params=pltpu.CompilerParams(has_side_effects=_DATAFLOW))(*hbm, *sems, after)
    return out[n:]


def _gather_small(small):
    def body(in_ref, out_ref, send_sem, recv_sem, local_sem):
        me = _my_index()
        my_id = (lax.axis_index("x"), lax.axis_index("y"), lax.axis_index("c"))
        cp = pltpu.make_async_copy(in_ref, out_ref.at[me], local_sem)
        cp.start()
        for d in range(1, N_DEV):
            pltpu.make_async_remote_copy(src_ref=in_ref, dst_ref=out_ref.at[me], send_sem=send_sem, recv_sem=recv_sem,
                                         device_id=_peer(d), device_id_type=pl.DeviceIdType.MESH).start()
        seven = out_ref.at[pl.ds(0, 7)]
        pltpu.make_async_remote_copy(src_ref=seven, dst_ref=seven, send_sem=send_sem, recv_sem=recv_sem, device_id=my_id,
                                     device_id_type=pl.DeviceIdType.MESH).wait()
        cp.wait()

    any_spec = pl.BlockSpec(memory_space=pl.ANY)
    return pl.pallas_call(body, in_specs=[any_spec], out_specs=any_spec, out_shape=SDS((N_DEV,) + small.shape, F32),
                          scratch_shapes=[pltpu.SemaphoreType.DMA] * 3, name="gather_small")(small)


def _adamw(name, recv, w, m, v):
    rows, cols = w.shape
    tr = max(c for c in range(16, 257, 16) if rows % c == 0) if rows % 16 == 0 else rows

    def body(r_ref, w_ref, m_ref, v_ref, g_ref, d_ref, nm_ref, nv_ref):
        g = r_ref[0].astype(F32)
        for s in range(1, N_DEV):
            g = g + r_ref[s].astype(F32)
        nm = ADAM_B1 * m_ref[...] + (1.0 - ADAM_B1) * g
        nv = ADAM_B2 * v_ref[...] + (1.0 - ADAM_B2) * (g * g)
        m_hat = nm / (1.0 - ADAM_B1 ** ADAM_STEP)
        v_hat = nv / (1.0 - ADAM_B2 ** ADAM_STEP)
        g_ref[...] = g
        d_ref[...] = -ADAM_LR * (m_hat / (jnp.sqrt(v_hat) + ADAM_EPS) + ADAM_WD * w_ref[...])
        nm_ref[...] = nm
        nv_ref[...] = nv

    blk = _bs((tr, cols), lambda i: (i, 0))
    return pl.pallas_call(
        body, grid=(rows // tr,), in_specs=[_bs((N_DEV, tr, cols), lambda i: (0, i, 0)), blk, blk, blk],
        out_specs=[blk] * 4, out_shape=[SDS((rows, cols), F32)] * 4, compiler_params=_cp(1), name=name)(recv, w, m, v)


def _rms_fwd(name, x, g, tm):
    t, d = x.shape

    def body(x_ref, g_ref, n_ref):
        xv = x_ref[...]
        r = lax.rsqrt(jnp.mean(xv * xv, axis=-1, keepdims=True) + EPS)
        n_ref[...] = (xv * r * g_ref[...]).astype(BF16)

    return pl.pallas_call(body, grid=(t // tm,), in_specs=[_bs((tm, d), lambda i: (i, 0)), _bs((1, d), lambda i: (0, 0))],
                          out_specs=_bs((tm, d), lambda i: (i, 0)), out_shape=SDS((t, d), BF16), compiler_params=_cp(1),
                          name=name)(x, g)


def _accumulate_rows(ref, part):
    @pl.when(pl.program_id(0) == 0)
    def _():
        ref[...] = part

    @pl.when(pl.program_id(0) > 0)
    def _():
        ref[...] += part


def _rms_bwd_epilogue(prods, extra_refs, out_refs, scratch_refs):
    dyv = prods[0]
    for p in prods[1:]:
        dyv = dyv + p
    if len(extra_refs) > 3:
        dyv = dyv + extra_refs[3][...]
    xv = extra_refs[0][...]
    r = lax.rsqrt(jnp.mean(xv * xv, axis=-1, keepdims=True) + EPS)
    xh = xv * r
    dxh = dyv * extra_refs[1][...]
    dx = extra_refs[2][...] + r * (dxh - xh * jnp.mean(dxh * xh, axis=-1, keepdims=True))
    for o in out_refs[:-1]:
        o[...] = dx.astype(o.dtype)
    _accumulate_rows(out_refs[-1], jnp.sum(dyv * xh, axis=0, keepdims=True))


def _out_norm_epilogue(prods, extra_refs, out_refs, scratch_refs):
    h = prods[0] + extra_refs[0][...]
    r = lax.rsqrt(jnp.mean(h * h, axis=-1, keepdims=True) + EPS)
    out_refs[0][...] = h
    out_refs[1][...] = (h * r * extra_refs[1][...]).astype(BF16)


def _glu_merge_epilogue(prods, extra_refs, out_refs, scratch_refs):
    ya, yb, ad = prods
    ga, gs = extra_refs[0][...].astype(F32), extra_refs[1][...].astype(F32)
    m = _sigmoid(ga) * ad + _sigmoid(gs) * (ya * _sigmoid(yb))
    out_refs[0][...] = m.astype(BF16)
    for o, val in zip(out_refs[1:], (ya, yb, ad)):
        o[...] = val.astype(o.dtype)


def _merge_bwd_epilogue(prods, extra_refs, out_refs, scratch_refs):
    dmv = prods[0]
    d = dmv.shape[1]
    ga, gs = _sigmoid(extra_refs[0][...].astype(F32)), _sigmoid(extra_refs[1][...].astype(F32))
    adv, yav = extra_refs[2][...].astype(F32), extra_refs[3][...].astype(F32)
    sb = _sigmoid(extra_refs[4][...].astype(F32))
    out_refs[0][:, 0:d] = (dmv * adv * ga * (1.0 - ga)).astype(BF16)
    out_refs[0][:, d:2 * d] = (dmv * (yav * sb) * gs * (1.0 - gs)).astype(BF16)
    dad = (dmv * ga).astype(BF16)
    dsd = dmv * gs
    dya = (dsd * sb).astype(BF16)
    dyb = (dsd * yav * sb * (1.0 - sb)).astype(BF16)
    out_refs[1][...], out_refs[2][...], out_refs[3][...] = dad, dya, dyb
    nt = _DNUMS["nt"]
    out_refs[4][...] = (lax.dot_general(dya, extra_refs[5][...], nt, preferred_element_type=F32)
                        + lax.dot_general(dyb, extra_refs[6][...], nt, preferred_element_type=F32))
    out_refs[5][...] = lax.dot_general(dad, extra_refs[7][...], nt, preferred_element_type=F32)


def _swiglu_epilogue(prods, extra_refs, out_refs, scratch_refs):
    gv, uv = prods
    out_refs[0][...] = (gv * _sigmoid(gv) * uv).astype(BF16)
    out_refs[1][...] = gv.astype(out_refs[1].dtype)
    out_refs[2][...] = uv.astype(out_refs[2].dtype)


def _swiglu_bwd_epilogue(prods, extra_refs, out_refs, scratch_refs):
    dav = prods[0]
    gv, uv = extra_refs[0][...].astype(F32), extra_refs[1][...].astype(F32)
    sg = _sigmoid(gv)
    out_refs[0][...] = (dav * uv * sg * (1.0 + gv * (1.0 - sg))).astype(BF16)
    out_refs[1][...] = (dav * gv * sg).astype(BF16)


def _head_epilogue(n_tiles):
    def epilogue(prods, extra_refs, out_refs, scratch_refs):
        h2 = prods[0] + extra_refs[0][...]
        h2_bf = h2.astype(BF16)
        out_refs[6][...] = h2_bf
        pgv = jnp.dot(h2_bf, extra_refs[3][...], preferred_element_type=F32)
        ppv = prods[1]
        d = pgv.shape[1]
        lacc = scratch_refs[0]
        sg = _sigmoid(pgv)
        h3 = h2 + sg * ppv
        r = lax.rsqrt(jnp.mean(h3 * h3, axis=-1, keepdims=True) + EPS)
        xh = h3 * r
        gv = extra_refs[1][...]
        diff = xh * gv - extra_refs[2][...]
        dout = diff * (1.0 / d)
        dxh = dout * gv
        dh3 = r * (dxh - xh * jnp.mean(dxh * xh, axis=-1, keepdims=True))
        dpg = (dh3 * ppv * sg * (1.0 - sg)).astype(BF16)
        dh2 = dh3 + lax.dot_general(dpg, extra_refs[3][...], _DNUMS["nt"], preferred_element_type=F32)
        out_refs[2][...] = dh2
        out_refs[3][...] = dh2.astype(BF16)
        out_refs[4][...] = (dh3 * sg).astype(BF16)
        out_refs[5][...] = dpg
        _accumulate_rows(out_refs[1], jnp.sum(dout * xh, axis=0, keepdims=True))
        _accumulate_rows(lacc, jnp.sum(diff * diff, axis=0, keepdims=True))

        @pl.when(pl.program_id(0) == n_tiles - 1)
        def _():
            out_refs[0][...] = (0.5 / d) * jnp.sum(lacc[...], axis=-1, keepdims=True)

    return epilogue


def _strided(r, n, d):
    return pl.ds(r, n, stride=d) if d > 1 else pl.ds(0, n)


def _rope_tables(pos_ref, invf_ref, c_s, s_s):
    ang = pos_ref[...].astype(F32) * invf_ref[...]
    lane = lax.broadcasted_iota(jnp.int32, ang.shape, 1)
    sn = jnp.sin(ang)
    c_s[...] = jnp.where(lane < ROPE_DIM, jnp.cos(ang), 1.0)
    s_s[...] = jnp.where(lane < ROPE_HALF, -sn, jnp.where(lane < ROPE_DIM, sn, 0.0))


def _rope_partner(xv, first_half):
    return jnp.where(first_half, pltpu.roll(xv, HEAD_DIM - ROPE_HALF, 1), pltpu.roll(xv, ROPE_HALF, 1))


def _rope_dilate_epilogue(tm):
    def epilogue(prods, extra_refs, out_refs, scratch_refs):
        zv = prods[0]
        pos_ref, invf_ref = extra_refs
        c_s, s_s, rot = scratch_refs
        c = pl.program_id(1)

        @pl.when(c == 0)
        def _():
            _rope_tables(pos_ref, invf_ref, c_s, s_s)

        @pl.when(c < 2)
        def _():
            cc, ss = c_s[...], s_s[...]
            first_half = lax.broadcasted_iota(jnp.int32, cc.shape, 1) < ROPE_HALF
            for h in range(QK_W // HEAD_DIM):
                xv = zv[:, h * HEAD_DIM:(h + 1) * HEAD_DIM]
                rot[h] = xv * cc + _rope_partner(xv, first_half) * ss

        @pl.when(c == 2)
        def _():
            for h in range(QK_W // HEAD_DIM):
                rot[h] = zv[:, h * HEAD_DIM:(h + 1) * HEAD_DIM]

        for g, (d, o_ref) in enumerate(zip(DILATIONS, out_refs)):
            n = tm // d
            for r in range(d):
                for hh in range(HEADS_PER_GROUP):
                    oc = r * GROUP_W + hh * HEAD_DIM
                    o_ref[:, oc:oc + HEAD_DIM] = rot[g * HEADS_PER_GROUP + hh, _strided(r, n, d), :].astype(BF16)

    return epilogue


def _band_masks(first_tile):
    qi = lax.broadcasted_iota(jnp.int32, (BLK, 2 * BLK), 0)
    kj = lax.broadcasted_iota(jnp.int32, (BLK, 2 * BLK), 1)
    band = (kj >= qi) & (kj <= qi + BLK)
    return band, band & ((kj >= BLK) | jnp.logical_not(first_tile))


def _attn_fwd(qkv, d, qt):
    ell = qkv.shape[1]
    nsub = qt // BLK
    scale = 1.0 / math.sqrt(HEAD_DIM)

    def body(q_ref, kc_ref, kp_ref, vc_ref, vp_ref, o_ref, lse_ref, kcat, vcat):
        nb = pl.program_id(1)
        kcat[0:BLK, :] = kp_ref[...]
        kcat[BLK:, :] = kc_ref[...]
        vcat[0:BLK, :] = vp_ref[...]
        vcat[BLK:, :] = vc_ref[...]
        lane = lax.broadcasted_iota(jnp.int32, (BLK, HEAD_DIM), 1)
        band, band_first = _band_masks(nb == 0)
        for b in range(nsub):
            valid = band_first if b == 0 else band
            lse_t = jnp.zeros((BLK, HEAD_DIM), F32)
            for hh in range(HEADS_PER_GROUP):
                cs = slice(hh * HEAD_DIM, (hh + 1) * HEAD_DIM)
                qb = q_ref[b * BLK:(b + 1) * BLK, cs]
                kk = kcat[b * BLK:(b + 2) * BLK, cs]
                vv = vcat[b * BLK:(b + 2) * BLK, cs]
                s = lax.dot_general(qb, kk, _DNUMS["nt"], preferred_element_type=F32) * scale
                s = jnp.where(valid, s, NEG)
                mx = jnp.max(s, axis=-1, keepdims=True)
                p = jnp.exp(s - mx)
                den = jnp.sum(p, axis=-1, keepdims=True)
                o = jnp.dot(p.astype(BF16), vv, preferred_element_type=F32) / den
                o_ref[b * BLK:(b + 1) * BLK, cs] = o
                lse_t = jnp.where(lane == hh, mx + jnp.log(den), lse_t)
            lse_ref[b * BLK:(b + 1) * BLK, :] = lse_t

    cur = lambda c: _bs((None, qt, GROUP_W), lambda r, nb: (c, nb, r))
    prev = lambda c: _bs((None, BLK, GROUP_W), lambda r, nb: (c, jnp.maximum(nb * nsub - 1, 0), r))
    return pl.pallas_call(
        body, grid=(d, ell // qt), in_specs=[cur(0), cur(1), prev(1), cur(2), prev(2)],
        out_specs=[_bs((qt, GROUP_W), lambda r, nb: (nb, r)), _bs((None, qt, HEAD_DIM), lambda r, nb: (r, nb, 0))],
        out_shape=[SDS((ell, d * GROUP_W), F32), SDS((d, ell, HEAD_DIM), F32)],
        scratch_shapes=[pltpu.VMEM((qt + BLK, GROUP_W), BF16)] * 2, compiler_params=_cp(2), name=f"attn_fwd_d{d}")(
            qkv, qkv, qkv, qkv, qkv)


def _attn_merge(outs, lses, tm):
    t = outs[0].shape[0]

    def body(o0, o1, o2, l0, l1, l2, attn_ref, attn_bf_ref, t0, t1, t2, so, sl, lt_s):
        for g, (d, o_ref, l_ref) in enumerate(zip(DILATIONS, (o0, o1, o2), (l0, l1, l2))):
            n = tm // d
            for r in range(d):
                rows = _strided(r, n, d)
                for hh in range(HEADS_PER_GROUP):
                    oc = r * GROUP_W + hh * HEAD_DIM
                    so[g * HEADS_PER_GROUP + hh, rows, :] = o_ref[:, oc:oc + HEAD_DIM]
                sl[g, rows, :] = l_ref[r]
        ls = [sl[g] for g in range(N_GROUPS)]
        mx = jnp.maximum(jnp.maximum(ls[0], ls[1]), ls[2])
        es = [jnp.exp(l - mx) for l in ls]
        den = es[0] + es[1] + es[2]
        ws = [e / den for e in es]
        lt_s[...] = mx + jnp.log(den)
        for hh in range(HEADS_PER_GROUP):
            cs = slice(hh * HEAD_DIM, (hh + 1) * HEAD_DIM)
            a = ws[0][:, hh:hh + 1] * so[hh]
            for g in range(1, N_GROUPS):
                a = a + ws[g][:, hh:hh + 1] * so[g * HEADS_PER_GROUP + hh]
            attn_ref[:, cs] = a
            attn_bf_ref[:, cs] = a.astype(BF16)
        for d, t_ref in zip(DILATIONS, (t0, t1, t2)):
            n = tm // d
            for r in range(d):
                t_ref[r] = lt_s[_strided(r, n, d), :]

    dil = lambda d: _bs((tm // d, d * GROUP_W), lambda i: (i, 0))
    lsp = lambda d: _bs((d, tm // d, HEAD_DIM), lambda i: (0, i, 0))
    row = _bs((tm, GROUP_W), lambda i: (i, 0))
    return pl.pallas_call(
        body, grid=(t // tm,),
        in_specs=[dil(d) for d in DILATIONS] + [lsp(d) for d in DILATIONS],
        out_specs=[row, row] + [lsp(d) for d in DILATIONS],
        out_shape=[SDS((t, GROUP_W), F32), SDS((t, GROUP_W), BF16)] + [SDS(l.shape, F32) for l in lses],
        scratch_shapes=[pltpu.VMEM((N_GROUPS * HEADS_PER_GROUP, tm, HEAD_DIM), F32), pltpu.VMEM((N_GROUPS, tm, HEAD_DIM), F32),
                        pltpu.VMEM((tm, HEAD_DIM), F32)],
        compiler_params=_cp(1), name="attn_merge")(*outs, *lses)


def _attn_bwd_pre(d_attn, attn, tm):
    t = attn.shape[0]

    def body(da_ref, a_ref, g0, g1, g2, e0, e1, e2, dl_s, da_s):
        lane = lax.broadcasted_iota(jnp.int32, (tm, HEAD_DIM), 1)
        dl = jnp.zeros((tm, HEAD_DIM), F32)
        for hh in range(HEADS_PER_GROUP):
            cs = slice(hh * HEAD_DIM, (hh + 1) * HEAD_DIM)
            dav = da_ref[:, cs]
            da_s[hh] = dav
            dl = jnp.where(lane == hh, jnp.sum(dav * a_ref[:, cs], axis=-1, keepdims=True), dl)
        dl_s[...] = dl
        for d, g_ref, e_ref in zip(DILATIONS, (g0, g1, g2), (e0, e1, e2)):
            n = tm // d
            for r in range(d):
                rows = _strided(r, n, d)
                for hh in range(HEADS_PER_GROUP):
                    oc = r * GROUP_W + hh * HEAD_DIM
                    g_ref[:, oc:oc + HEAD_DIM] = da_s[hh, rows, :].astype(BF16)
                e_ref[r] = dl_s[rows, :]

    row = _bs((tm, GROUP_W), lambda i: (i, 0))
    return pl.pallas_call(
        body, grid=(t // tm,), in_specs=[row, row],
        out_specs=[_bs((tm // d, d * GROUP_W), lambda i: (i, 0)) for d in DILATIONS]
        + [_bs((d, tm // d, HEAD_DIM), lambda i: (0, i, 0)) for d in DILATIONS],
        out_shape=[SDS((t // d, d * GROUP_W), BF16) for d in DILATIONS]
        + [SDS((d, t // d, HEAD_DIM), F32) for d in DILATIONS],
        scratch_shapes=[pltpu.VMEM((tm, HEAD_DIM), F32), pltpu.VMEM((HEADS_PER_GROUP, tm, HEAD_DIM), F32)],
        compiler_params=_cp(1), name="attn_bwd_pre")(d_attn, attn)


def _attn_bwd(qkv, d_a, lt, delta, d, qt):
    ell = qkv.shape[1]
    nsub = qt // BLK
    ntile = ell // qt
    nblk = ell // BLK
    scale = 1.0 / math.sqrt(HEAD_DIM)

    def body(q_ref, qn_ref, kc_ref, kp_ref, vc_ref, vp_ref, da_ref, dan_ref, lt_ref, ltn_ref, dl_ref, dln_ref, o_ref,
             kcat, vcat, dk_acc, dv_acc):
        nb = pl.program_id(1)
        kcat[0:BLK, :] = kp_ref[...]
        kcat[BLK:, :] = kc_ref[...]
        vcat[0:BLK, :] = vp_ref[...]
        vcat[BLK:, :] = vc_ref[...]
        qi = lax.broadcasted_iota(jnp.int32, (BLK, BLK), 0)
        kj = lax.broadcasted_iota(jnp.int32, (BLK, BLK), 1)
        valid_next = (kj >= qi) & (nb < ntile - 1)
        band, band_first = _band_masks(nb == 0)
        for hh in range(HEADS_PER_GROUP):
            cs = slice(hh * HEAD_DIM, (hh + 1) * HEAD_DIM)
            dk_acc[...] = jnp.zeros_like(dk_acc)
            dv_acc[...] = jnp.zeros_like(dv_acc)
            for b in range(nsub):
                rs = slice(b * BLK, (b + 1) * BLK)
                ks = slice(b * BLK, (b + 2) * BLK)
                valid = band_first if b == 0 else band
                qb, kk, vv, dab = q_ref[rs, cs], kcat[ks, cs], vcat[ks, cs], da_ref[rs, cs]
                s = lax.dot_general(qb, kk, _DNUMS["nt"], preferred_element_type=F32) * scale
                p = jnp.where(valid, jnp.exp(s - lt_ref[rs, hh:hh + 1]), 0.0)
                dp = lax.dot_general(dab, vv, _DNUMS["nt"], preferred_element_type=F32)
                ds = (p * (dp - dl_ref[rs, hh:hh + 1])).astype(BF16)
                o_ref[0, rs, cs] = jnp.dot(ds, kk, preferred_element_type=F32) * scale
                dk_acc[ks, :] += lax.dot_general(ds, qb, _DNUMS["tn"], preferred_element_type=F32) * scale
                dv_acc[ks, :] += lax.dot_general(p.astype(BF16), dab, _DNUMS["tn"], preferred_element_type=F32)
            ks = slice(nsub * BLK, (nsub + 1) * BLK)
            qn, kl, vl, dan = qn_ref[:, cs], kcat[ks, cs], vcat[ks, cs], dan_ref[:, cs]
            s = lax.dot_general(qn, kl, _DNUMS["nt"], preferred_element_type=F32) * scale
            p = jnp.where(valid_next, jnp.exp(s - ltn_ref[:, hh:hh + 1]), 0.0)
            dp = lax.dot_general(dan, vl, _DNUMS["nt"], preferred_element_type=F32)
            ds = (p * (dp - dln_ref[:, hh:hh + 1])).astype(BF16)
            dk_acc[ks, :] += lax.dot_general(ds, qn, _DNUMS["tn"], preferred_element_type=F32) * scale
            dv_acc[ks, :] += lax.dot_general(p.astype(BF16), dan, _DNUMS["tn"], preferred_element_type=F32)
            o_ref[1, :, cs] = dk_acc[BLK:, :]
            o_ref[2, :, cs] = dv_acc[BLK:, :]

    nxt = lambda nb: jnp.minimum((nb + 1) * nsub, nblk - 1)
    prv = lambda nb: jnp.maximum(nb * nsub - 1, 0)
    cur3 = lambda c: _bs((None, qt, GROUP_W), lambda r, nb: (c, nb, r))
    in_specs = [
        cur3(0), _bs((None, BLK, GROUP_W), lambda r, nb: (0, nxt(nb), r)),
        cur3(1), _bs((None, BLK, GROUP_W), lambda r, nb: (1, prv(nb), r)),
        cur3(2), _bs((None, BLK, GROUP_W), lambda r, nb: (2, prv(nb), r)),
        _bs((qt, GROUP_W), lambda r, nb: (nb, r)), _bs((BLK, GROUP_W), lambda r, nb: (nxt(nb), r)),
        _bs((None, qt, HEAD_DIM), lambda r, nb: (r, nb, 0)), _bs((None, BLK, HEAD_DIM), lambda r, nb: (r, nxt(nb), 0)),
        _bs((None, qt, HEAD_DIM), lambda r, nb: (r, nb, 0)), _bs((None, BLK, HEAD_DIM), lambda r, nb: (r, nxt(nb), 0)),
    ]
    return pl.pallas_call(
        body, grid=(d, ntile), in_specs=in_specs, out_specs=_bs((3, qt, GROUP_W), lambda r, nb: (0, nb, r)),
        out_shape=SDS((3, ell, d * GROUP_W), F32),
        scratch_shapes=[pltpu.VMEM((qt + BLK, GROUP_W), BF16)] * 2 + [pltpu.VMEM((qt + BLK, HEAD_DIM), F32)] * 2,
        compiler_params=_cp(2), name=f"attn_bwd_d{d}")(qkv, qkv, qkv, qkv, qkv, qkv, d_a, d_a, lt, lt, delta, delta)


def _undilate_rope_bwd(dqkvs, pos, invf, tm):
    t = pos.shape[0]

    def body(g0, g1, g2, pos_ref, invf_ref, o_ref, c_s, s_s, nat):
        c = pl.program_id(1)

        @pl.when(c == 0)
        def _():
            _rope_tables(pos_ref, invf_ref, c_s, s_s)

        for g, (d, g_ref) in enumerate(zip(DILATIONS, (g0, g1, g2))):
            n = tm // d
            for r in range(d):
                for hh in range(HEADS_PER_GROUP):
                    oc = r * GROUP_W + hh * HEAD_DIM
                    nat[g * HEADS_PER_GROUP + hh, _strided(r, n, d), :] = g_ref[:, oc:oc + HEAD_DIM]

        @pl.when(c < 2)
        def _():
            cc, ss = c_s[...], s_s[...]
            first_half = lax.broadcasted_iota(jnp.int32, cc.shape, 1) < ROPE_HALF
            for h in range(QK_W // HEAD_DIM):
                xv = nat[h]
                y = xv * cc - _rope_partner(xv, first_half) * ss
                o_ref[:, h * HEAD_DIM:(h + 1) * HEAD_DIM] = y.astype(BF16)

        @pl.when(c == 2)
        def _():
            for h in range(QK_W // HEAD_DIM):
                o_ref[:, h * HEAD_DIM:(h + 1) * HEAD_DIM] = nat[h].astype(BF16)

    return pl.pallas_call(
        body, grid=(t // tm, 3),
        in_specs=[_bs((None, tm // d, d * GROUP_W), lambda i, c: (c, i, 0)) for d in DILATIONS]
        + [_bs((tm, 1), lambda i, c: (i, 0)), _bs((1, HEAD_DIM), lambda i, c: (0, 0))],
        out_specs=_bs((tm, QK_W), lambda i, c: (i, c)), out_shape=SDS((t, 3 * QK_W), BF16),
        scratch_shapes=[pltpu.VMEM((tm, HEAD_DIM), F32)] * 2 + [pltpu.VMEM((QK_W // HEAD_DIM, tm, HEAD_DIM), F32)],
        compiler_params=_cp(2), name="undilate_rope_bwd")(*dqkvs, pos, invf)


def _cmul(ar, ai, br, bi):
    return ar * br - ai * bi, ar * bi + ai * br


def _ssm_disc(a_re, a_im, log_dt, nsq):
    def body(lr_ref, li_ref, ldt_ref, br_ref, bi_ref, zr_ref, zi_ref, pr_ref, pi_ref):
        lr, li = lr_ref[...], li_ref[...]
        dt = jnp.exp(ldt_ref[...])
        mag = jnp.exp(lr * dt)
        bar_re, bar_im = mag * jnp.cos(li * dt), mag * jnp.sin(li * dt)
        nr, ni = bar_re - 1.0, bar_im
        den = lr * lr + li * li
        br_ref[...], bi_ref[...] = bar_re, bar_im
        zr_ref[...] = (nr * lr + ni * li) / den
        zi_ref[...] = (ni * lr - nr * li) / den
        pr, pi = bar_re, bar_im
        for _ in range(nsq):
            pr, pi = _cmul(pr, pi, pr, pi)
        pr_ref[...], pi_ref[...] = pr, pi

    return pl.pallas_call(body, out_shape=[SDS(a_re.shape, F32)] * 6, name="ssm_discretise")(a_re, a_im, log_dt)


def _ssm_scale_b(z_re, z_im, b_re, b_im):
    def body(zr_ref, zi_ref, br_ref, bi_ref, or_ref, oi_ref):
        zr, zi, br, bi = zr_ref[...], zi_ref[...], br_ref[...], bi_ref[...]
        or_ref[...] = zr * br - zi * bi
        oi_ref[...] = zr * bi + zi * br

    return pl.pallas_call(body, out_shape=[SDS(b_re.shape, F32)] * 2, name="ssm_scale_b")(z_re, z_im, b_re, b_im)


def _ssm_scale_b_bwd(z_re, z_im, b_re, b_im, g_re, g_im):
    def body(zr_ref, zi_ref, br_ref, bi_ref, gr_ref, gi_ref, dbr_ref, dbi_ref, dzr_ref, dzi_ref):
        zr, zi, br, bi, gr, gi = zr_ref[...], zi_ref[...], br_ref[...], bi_ref[...], gr_ref[...], gi_ref[...]
        dbr_ref[...] = zr * gr + zi * gi
        dbi_ref[...] = zr * gi - zi * gr
        dzr_ref[...] = jnp.sum(br * gr + bi * gi, axis=-1, keepdims=True)
        dzi_ref[...] = jnp.sum(br * gi - bi * gr, axis=-1, keepdims=True)

    return pl.pallas_call(body, out_shape=[SDS(b_re.shape, F32)] * 2 + [SDS(z_re.shape, F32)] * 2,
                          name="ssm_scale_b_bwd")(z_re, z_im, b_re, b_im, g_re, g_im)


def _ssm_disc_bwd(a_re, a_im, log_dt, gb_re, gb_im, gz_re, gz_im):
    def body(lr_ref, li_ref, ldt_ref, gbr_ref, gbi_ref, gzr_ref, gzi_ref, dar_ref, dai_ref, dldt_ref):
        lr, li = lr_ref[...], li_ref[...]
        dt = jnp.exp(ldt_ref[...])
        mag = jnp.exp(lr * dt)
        bar_re, bar_im = mag * jnp.cos(li * dt), mag * jnp.sin(li * dt)
        nr, ni = bar_re - 1.0, bar_im
        den = lr * lr + li * li
        zr, zi = (nr * lr + ni * li) / den, (ni * lr - nr * li) / den
        gzr, gzi = gzr_ref[...], gzi_ref[...]
        gbr = gbr_ref[...] + (lr * gzr - li * gzi) / den
        gbi = gbi_ref[...] + (lr * gzi + li * gzr) / den
        qr, qi = (zr * lr + zi * li) / den, (zi * lr - zr * li) / den
        dar_ref[...] = dt * (bar_re * gbr + bar_im * gbi) - qr * gzr - qi * gzi
        dai_ref[...] = dt * (bar_re * gbi - bar_im * gbr) - qr * gzi + qi * gzr
        wr, wi = lr * bar_re - li * bar_im, lr * bar_im + li * bar_re
        dldt_ref[...] = dt * jnp.sum(wr * gbr + wi * gbi, axis=-1, keepdims=True)

    return pl.pallas_call(body, out_shape=[SDS(a_re.shape, F32)] * 2 + [SDS(log_dt.shape, F32)],
                          name="ssm_discretise_bwd")(a_re, a_im, log_dt, gb_re, gb_im, gz_re, gz_im)


def _interleave_epilogue(prods, extra_refs, out_refs, scratch_refs):
    uv = prods[0]
    tmp = scratch_refs[0]
    n = uv.shape[0] // N_DEV
    for b in range(SSM_W // BLK):
        cs = slice(b * BLK, (b + 1) * BLK)
        for j in range(N_DEV):
            tmp[b, pl.ds(j, n, stride=N_DEV), :] = uv[j * n:(j + 1) * n, cs]
        out_refs[0][:, cs] = tmp[b]
        out_refs[1][:, cs] = tmp[b].astype(BF16)


def _drive(src_ref, mat_ref, dst, mode):
    for kn in range(2 * SSM_NB):
        n = kn % SSM_NB
        a = src_ref[:, n * BLK:(n + 1) * BLK]
        dst[:, kn * 512:(kn + 1) * 512] = lax.dot_general(a, mat_ref[kn], _DNUMS[mode], preferred_element_type=F32)


def _scan_chunk(src, lam_ref, carry, *, reverse, store=None, h_ref=None, acc=None):
    steps = src.shape[0] // 8
    for c in range(NSTATE // SCAN_LANES):
        re = slice(c * SCAN_LANES, (c + 1) * SCAN_LANES)
        im = slice(NSTATE + c * SCAN_LANES, NSTATE + (c + 1) * SCAN_LANES)
        ar, ai = lam_ref[:, re], lam_ref[:, im]

        def step(s, val):
            i = (steps - 1 - s) if reverse else s
            rows = pl.ds(pl.multiple_of(i * 8, 8), 8)
            if acc is not None:
                hr, hi, dr, di = val
                pr, pi = h_ref[rows, re], h_ref[rows, im]
                dr = dr + hr * pr + hi * pi
                di = di + hi * pr - hr * pi
            else:
                hr, hi = val
            nr = ar * hr - ai * hi + src[rows, re]
            ni = ar * hi + ai * hr + src[rows, im]
            if store is not None:
                store[rows, re] = nr
                store[rows, im] = ni
            return (nr, ni, dr, di) if acc is not None else (nr, ni)

        init = (carry[:, re], carry[:, im])
        if acc is not None:
            init = init + (acc[:, re], acc[:, im])
        out = lax.fori_loop(0, steps, step, init, unroll=4)
        carry[:, re], carry[:, im] = out[0], out[1]
        if acc is not None:
            acc[:, re], acc[:, im] = out[2], out[3]


def _segment_carries(e_ref, pw_ref, out_ref, reverse):
    pr, pi = pw_ref[:, 0:NSTATE], pw_ref[:, NSTATE:]
    hr = jnp.zeros((1, NSTATE), F32)
    hi = jnp.zeros((1, NSTATE), F32)
    order = range(N_DEV - 1, -1, -1) if reverse else range(N_DEV)
    for j in order:
        out_ref[j:j + 1, 0:NSTATE] = hr
        out_ref[j:j + 1, NSTATE:] = hi
        tr, ti = _cmul(pr, pi, hr, hi)
        hr, hi = e_ref[j:j + 1, 0:NSTATE] + tr, e_ref[j:j + 1, NSTATE:] + ti


def _ssm_carries(name, src, mat, mode, lam8, pw, reverse):
    t = src.shape[0]
    nchunk = t // SCAN_ROWS

    def body(src_ref, mat_ref, lam_ref, pw_ref, out_ref, drive, carry):
        c = pl.program_id(0)

        @pl.when(c == 0)
        def _():
            carry[...] = jnp.zeros_like(carry)

        _drive(src_ref, mat_ref, drive, mode)
        _scan_chunk(drive, lam_ref, carry, reverse=reverse)

        @pl.when(c == nchunk - 1)
        def _():
            _segment_carries(carry, pw_ref, out_ref, reverse)

    blk = (lambda c: (nchunk - 1 - c, 0)) if reverse else (lambda c: (c, 0))
    return pl.pallas_call(
        body, grid=(nchunk,),
        in_specs=[_bs((SCAN_ROWS, SSM_W), blk), _bs(mat.shape, lambda c: (0, 0, 0)), _bs((8, 2 * NSTATE), lambda c: (0, 0)),
                  _bs((1, 2 * NSTATE), lambda c: (0, 0))],
        out_specs=_bs((8, 2 * NSTATE), lambda c: (0, 0)), out_shape=SDS((8, 2 * NSTATE), F32),
        scratch_shapes=[pltpu.VMEM((SCAN_ROWS, 2 * NSTATE), F32), pltpu.VMEM((8, 2 * NSTATE), F32)],
        compiler_params=_cp(1), name=name)(src, mat, lam8, pw)


def _ssm_fwd(u_bf, u, d_skip, bd, cd, lam8, start):
    t = u_bf.shape[0]
    nchunk = t // SCAN_ROWS
    per_seg = SCAN_ROWS // N_DEV

    def body(ub_ref, u_ref, d_ref, bd_ref, cd_ref, lam_ref, start_ref, h_ref, ys_ref, yg_ref, drive, carry, tmp, hs):
        @pl.when(pl.program_id(0) == 0)
        def _():
            carry[...] = start_ref[...]

        _drive(ub_ref, bd_ref, drive, "nn")
        _scan_chunk(drive, lam_ref, carry, reverse=False, store=hs)
        h_ref[...] = hs[...].astype(BF16)
        for n in range(SSM_NB):
            cs = slice(n * BLK, (n + 1) * BLK)
            hr = h_ref[:, n * 512:(n + 1) * 512]
            hi = h_ref[:, NSTATE + n * 512:NSTATE + (n + 1) * 512]
            ys = (jnp.dot(hr, cd_ref[n], preferred_element_type=F32) + jnp.dot(hi, cd_ref[SSM_NB + n], preferred_element_type=F32)
                  + d_ref[:, cs] * u_ref[:, cs])
            ys_ref[:, cs] = ys
            tmp[n] = _gelu_parts(ys)[0]
            for j in range(N_DEV):
                yg_ref[j, :, cs] = tmp[n, pl.ds(j, per_seg, stride=N_DEV), :].astype(BF16)

    row = _bs((SCAN_ROWS, SSM_W), lambda c: (c, 0))
    h, ys, yg = pl.pallas_call(
        body, grid=(nchunk,),
        in_specs=[row, row, _bs((1, SSM_W), lambda c: (0, 0)), _bs(bd.shape, lambda c: (0, 0, 0)), _bs(cd.shape, lambda c: (0, 0, 0)),
                  _bs((8, 2 * NSTATE), lambda c: (0, 0)), _bs((8, 2 * NSTATE), lambda c: (0, 0))],
        out_specs=[_bs((SCAN_ROWS, 2 * NSTATE), lambda c: (c, 0)), row, _bs((N_DEV, per_seg, SSM_W), lambda c: (0, c, 0))],
        out_shape=[SDS((t, 2 * NSTATE), BF16), SDS((t, SSM_W), F32), SDS((N_DEV, t // N_DEV, SSM_W), BF16)],
        scratch_shapes=[pltpu.VMEM((SCAN_ROWS, 2 * NSTATE), F32), pltpu.VMEM((8, 2 * NSTATE), F32),
                        pltpu.VMEM((SSM_NB, SCAN_ROWS, BLK), F32), pltpu.VMEM((SCAN_ROWS, 2 * NSTATE), F32)],
        compiler_params=_cp(1), name="ssm_scan_fwd")(u_bf, u, d_skip, bd, cd, lam8, start)
    return h, ys, yg.reshape(t, SSM_W)


def _ssm_bwd(dys_bf, dys, d_skip, u_bf, h, bd, cd, lamc8, start):
    t = u_bf.shape[0]
    nchunk = t // SCAN_ROWS
    per_seg = SCAN_ROWS // N_DEV

    def body(dys_ref, dysf_ref, d_ref, u_ref, h_ref, bd_ref, cd_ref, lam_ref, start_ref, du_ref, dlam_ref, dbd_ref, dcd_ref,
             drive, adj, carry, tmp, hs):
        c = pl.program_id(0)

        @pl.when(c == 0)
        def _():
            carry[...] = start_ref[...]
            dlam_ref[...] = jnp.zeros_like(dlam_ref)
            dbd_ref[...] = jnp.zeros_like(dbd_ref)
            dcd_ref[...] = jnp.zeros_like(dcd_ref)

        hs[...] = h_ref[...].astype(F32)
        _drive(dys_ref, cd_ref, drive, "nt")
        _scan_chunk(drive, lam_ref, carry, reverse=True, store=adj, h_ref=hs, acc=dlam_ref)
        for n in range(SSM_NB):
            cs = slice(n * BLK, (n + 1) * BLK)
            acc = None
            for k in range(2):
                kn = k * SSM_NB + n
                ss = slice(kn * 512, (kn + 1) * 512)
                lam_b = adj[:, ss].astype(BF16)
                part = lax.dot_general(lam_b, bd_ref[kn], _DNUMS["nt"], preferred_element_type=F32)
                acc = part if acc is None else acc + part
                dbd_ref[kn] += lax.dot_general(u_ref[:, cs], lam_b, _DNUMS["tn"], preferred_element_type=F32)
                dcd_ref[kn] += lax.dot_general(h_ref[:, ss], dys_ref[:, cs], _DNUMS["tn"], preferred_element_type=F32)
            tmp[n] = acc + d_ref[:, cs] * dysf_ref[:, cs]
            for j in range(N_DEV):
                du_ref[j, :, cs] = tmp[n, pl.ds(j, per_seg, stride=N_DEV), :].astype(BF16)

    rev = lambda c: (nchunk - 1 - c, 0)
    const2 = lambda c: (0, 0)
    const3 = lambda c: (0, 0, 0)
    row = _bs((SCAN_ROWS, SSM_W), rev)
    du, dlam, dbd, dcd = pl.pallas_call(
        body, grid=(nchunk,),
        in_specs=[row, row, _bs((1, SSM_W), const2), row, _bs((SCAN_ROWS, 2 * NSTATE), rev),
                  _bs(bd.shape, const3), _bs(cd.shape, const3), _bs((8, 2 * NSTATE), const2), _bs((8, 2 * NSTATE), const2)],
        out_specs=[_bs((N_DEV, per_seg, SSM_W), lambda c: (0, nchunk - 1 - c, 0)), _bs((8, 2 * NSTATE), const2),
                   _bs(bd.shape, const3), _bs(cd.shape, const3)],
        out_shape=[SDS((N_DEV, t // N_DEV, SSM_W), BF16), SDS((8, 2 * NSTATE), F32), SDS(bd.shape, F32), SDS(cd.shape, F32)],
        scratch_shapes=[pltpu.VMEM((SCAN_ROWS, 2 * NSTATE), F32), pltpu.VMEM((SCAN_ROWS, 2 * NSTATE), F32),
                        pltpu.VMEM((8, 2 * NSTATE), F32), pltpu.VMEM((SSM_NB, SCAN_ROWS, BLK), F32),
                        pltpu.VMEM((SCAN_ROWS, 2 * NSTATE), F32)],
        compiler_params=_cp(1), name="ssm_scan_bwd")(dys_bf, dys, d_skip, u_bf, h, bd, cd, lamc8, start)
    return du.reshape(t, SSM_W), dlam, dbd, dcd


def _gelu_parts(x):
    c0 = math.sqrt(2.0 / math.pi)
    inner = c0 * (x + 0.044715 * x * x * x)
    th = jnp.tanh(inner)
    val = 0.5 * x * (1.0 + th)
    grad = 0.5 * (1.0 + th) + 0.5 * x * (1.0 - th * th) * c0 * (1.0 + 3.0 * 0.044715 * x * x)
    return val, grad


def _ssm_out_bwd(d_yg, ys, u, tm):
    t = u.shape[0]
    seg = t // N_DEV

    def body(dg_ref, ys_ref, u_ref, dys_ref, dysb_ref, dd_ref, tmp):
        for n in range(SSM_W // BLK):
            for j in range(N_DEV):
                tmp[n, pl.ds(j, tm // N_DEV, stride=N_DEV), :] = dg_ref[j, :, n * BLK:(n + 1) * BLK]
        dyg = jnp.concatenate([tmp[n] for n in range(SSM_W // BLK)], axis=1)
        dys = dyg * _gelu_parts(ys_ref[...])[1]
        dys_ref[...] = dys
        dysb_ref[...] = dys.astype(BF16)
        part = jnp.sum(dys * u_ref[...], axis=0, keepdims=True)

        @pl.when(pl.program_id(0) == 0)
        def _():
            dd_ref[...] = part

        @pl.when(pl.program_id(0) > 0)
        def _():
            dd_ref[...] += part

    row = _bs((tm, SSM_W), lambda i: (i, 0))
    return pl.pallas_call(
        body, grid=(t // tm,), in_specs=[_bs((N_DEV, tm // N_DEV, SSM_W), lambda i: (0, i, 0)), row, row],
        out_specs=[row, row, _bs((1, SSM_W), lambda i: (0, 0))],
        out_shape=[SDS((t, SSM_W), F32), SDS((t, SSM_W), BF16), SDS((1, SSM_W), F32)],
        scratch_shapes=[pltpu.VMEM((SSM_W // BLK, tm, BLK), F32)], compiler_params=_cp(1), name="ssm_out_bwd")(
            d_yg.reshape(N_DEV, seg, SSM_W), ys, u)


def _block_diag(blocks):
    nb, ng, r, c = blocks.shape
    eye = jnp.eye(ng, dtype=blocks.dtype)
    return (blocks[:, :, :, None, :] * eye[None, :, None, :, None]).reshape(nb, ng * r, ng * c)


def _diag_blocks(full, r, c):
    k, nb = full.shape[:2]
    ng = full.shape[2] // r
    x = full.reshape(k, nb, ng, r, ng, c)
    eye = jnp.eye(ng, dtype=full.dtype)
    return jnp.sum(x * eye[None, None, :, None, :, None], axis=4).reshape(k, nb * ng, r, c)


_SMALL = ("a_re", "a_im", "log_dt", "b_re", "b_im", "c_re", "c_im", "d_skip", "g_ffn", "g_final")


def _pack_small(arrs):
    flat = jnp.concatenate([a.reshape(-1) for a in arrs])
    pad = (-flat.shape[0]) % (8 * 128)
    return jnp.pad(flat, (0, pad)).reshape(-1, 128)


def _unpack_small(packed, shapes):
    flat = packed.reshape(-1)
    out, off = [], 0
    for s in shapes:
        n = math.prod(s)
        out.append(flat[off:off + n].reshape(s))
        off += n
    return out


def kernel(x, p, positions, g_mix, w_in, a_re, a_im, log_dt, b_re, b_im, c_re, c_im, d_skip, w_attn_proj, w_glu_a, w_glu_b, w_out, g_ffn, w_ffn_gate, w_ffn_up, w_ffn_down, w_ple_gate, w_ple_proj, g_final, loss_target, m_g_mix, m_w_in, m_a_re, m_a_im, m_log_dt, m_b_re, m_b_im, m_c_re, m_c_im, m_d_skip, m_w_attn_proj, m_w_glu_a, m_w_glu_b, m_w_out, m_g_ffn, m_w_ffn_gate, m_w_ffn_up, m_w_ffn_down, m_w_ple_gate, m_w_ple_proj, m_g_final, v_g_mix, v_w_in, v_a_re, v_a_im, v_log_dt, v_b_re, v_b_im, v_c_re, v_c_im, v_d_skip, v_w_attn_proj, v_w_glu_a, v_w_glu_b, v_w_out, v_g_ffn, v_w_ffn_gate, v_w_ffn_up, v_w_ffn_down, v_w_ple_gate, v_w_ple_proj, v_g_final):
    args = dict(locals())
    t, d = x.shape[1], x.shape[2]
    inw = w_in.shape[2] * N_DEV
    fs = w_ffn_gate.shape[2]
    ff = fs * N_DEV
    ple = w_ple_proj.shape[1]
    seg = t // N_DEV
    assert inw == 3 * QK_W + SSM_W + 2 * d and t % (N_DEV * SCAN_ROWS // 8) == 0 and seg & (seg - 1) == 0
    tm = min(1024, t)
    te = min(512, t)
    tk = min(2048, t)
    ucol = (3 * QK_W) // SSM_W
    gcol = (3 * QK_W + SSM_W) // d
    assert (3 * QK_W + SSM_W) % d == 0

    x2, p2, tgt = x[0], p[0, 0], loss_target[0]
    pos = positions.reshape(t, 1)
    inv = ROPE_THETA ** (-jnp.arange(ROPE_HALF, dtype=F32) * 2.0 / ROPE_DIM)
    invf = jnp.concatenate([inv, inv, jnp.zeros((HEAD_DIM - ROPE_DIM,), F32)]).reshape(1, HEAD_DIM)

    wnames = ("w_in", "w_attn_proj", "w_glu_a", "w_glu_b", "w_out", "w_ffn_gate", "w_ffn_up", "w_ffn_down", "w_ple_gate",
              "w_ple_proj")
    kinds = ("cols", "cols", "cols", "cols", "rows", "slot", "slot", "rows", "rows", "cols")
    shards = [args[n][0].astype(BF16) for n in wnames]
    sizes = [s.shape[0] if k == "rows" else s.shape[-1] for s, k in zip(shards, kinds)]
    ag = _exchange_start("gather_weights_start", shards, kinds, sizes, True)

    row_d = _bs((tm, d), lambda i, j, k: (i, 0))
    row_e = _bs((te, d), lambda i, j, k: (i, 0))
    vec_d = _bs((1, d), lambda i, j, k: (0, 0))
    sq_w = _bs((d, d), lambda i, j, k: (0, 0))
    n1 = _rms_fwd("norm_mix", x2, g_mix + ag[3][0:1, 0:1], tm)
    W_in, = _exchange_wait("gather_w_in_wait", ag, [0], kinds, sizes, True, n1)
    qkv = _mm("qkv_proj", (t // tm, 3, 1), [("nn", n1, row_d, W_in, _bs((d, QK_W), lambda i, j, k: (0, j)))],
              [(SDS((3, t // dil, dil * GROUP_W), BF16), _bs((None, tm // dil, dil * GROUP_W), lambda i, j, k: (j, i, 0)))
               for dil in DILATIONS],
              extras=[(pos, _bs((tm, 1), lambda i, j, k: (i, 0))), (invf, _bs((1, HEAD_DIM), lambda i, j, k: (0, 0)))],
              epilogue=_rope_dilate_epilogue(tm),
              scratch=[pltpu.VMEM((tm, HEAD_DIM), F32)] * 2 + [pltpu.VMEM((QK_W // HEAD_DIM, tm, HEAD_DIM), F32)])
    row_s = _bs((tm, SSM_W), lambda i, j, k: (i, 0))
    u_perm, u_bf = _mm("u_proj", (t // tm, 1, 1),
                       [("nn", n1.reshape(N_DEV, seg, d), _bs((N_DEV, tm // N_DEV, d), lambda i, j, k: (0, i, 0)), W_in,
                         _bs((d, SSM_W), lambda i, j, k: (0, ucol)))],
                       [(SDS((t, SSM_W), F32), row_s), (SDS((t, SSM_W), BF16), row_s)], epilogue=_interleave_epilogue,
                       scratch=[pltpu.VMEM((SSM_W // BLK, tm, BLK), F32)])
    zg, = _mm("z_gates", (t // tm, 2, 1),
              [("nn", n1, row_d, W_in, _bs((d, d), lambda i, j, k: (0, gcol + j)))],
              [(SDS((t, 2 * d), BF16), _bs((tm, d), lambda i, j, k: (i, j)))])

    outs, lses = [], []
    for g, dil in enumerate(DILATIONS):
        o_g, l_g = _attn_fwd(qkv[g], dil, min(1024, t // dil))
        outs.append(o_g)
        lses.append(l_g)
    merged = _attn_merge(outs, lses, tm)
    attn, attn_bf, lts = merged[0], merged[1], merged[2:]

    nsq = seg.bit_length() - 1
    bar_re, bar_im, z_re, z_im, pw_re, pw_im = _ssm_disc(a_re[0], a_im[0], log_dt.reshape(SSM_GROUPS, 1), nsq)
    gp = SSM_GROUPS * SSM_STATE
    b_re2, b_im2 = b_re.reshape(gp, SSM_GROUP), b_im.reshape(gp, SSM_GROUP)
    bb_re, bb_im = _ssm_scale_b(z_re.reshape(gp, 1), z_im.reshape(gp, 1), b_re2, b_im2)

    def chunks(a, r, c):
        return a.reshape(SSM_NB, SSM_GROUPS // SSM_NB, r, c)

    bbt = lambda a: jnp.swapaxes(a.reshape(SSM_GROUPS, SSM_STATE, SSM_GROUP), 1, 2)
    bd = jnp.concatenate([_block_diag(chunks(bbt(bb_re), SSM_GROUP, SSM_STATE)),
                          _block_diag(chunks(bbt(bb_im), SSM_GROUP, SSM_STATE))]).astype(BF16)
    ct = lambda a: jnp.swapaxes(a[0], 1, 2)
    cd = jnp.concatenate([_block_diag(chunks(ct(c_re), SSM_STATE, SSM_GROUP)),
                          _block_diag(chunks(-ct(c_im), SSM_STATE, SSM_GROUP))]).astype(BF16)
    lam = jnp.concatenate([bar_re.reshape(1, gp), bar_im.reshape(1, gp)], axis=1)
    lamc = jnp.concatenate([bar_re.reshape(1, gp), -bar_im.reshape(1, gp)], axis=1)
    pw = jnp.concatenate([pw_re.reshape(1, gp), pw_im.reshape(1, gp)], axis=1)
    pwc = jnp.concatenate([pw_re.reshape(1, gp), -pw_im.reshape(1, gp)], axis=1)
    lam8, lamc8 = jnp.broadcast_to(lam, (8, 2 * gp)), jnp.broadcast_to(lamc, (8, 2 * gp))

    start_f = _ssm_carries("ssm_carries_fwd", u_bf, bd, "nn", lam8, pw, False)
    dsk = d_skip.reshape(1, SSM_W)
    h_all, ys, yg_bf = _ssm_fwd(u_bf, u_perm, dsk, bd, cd, lam8, start_f)
    W_ap, W_ga, W_gb, W_out, W_fg, W_fu, W_fd, W_pg, W_pp = _exchange_wait(
        "gather_rest_wait", ag, list(range(1, len(wnames))), kinds, sizes, True, yg_bf)
    W_fg = jnp.swapaxes(W_fg, 0, 1).reshape(d, ff)
    W_fu = jnp.swapaxes(W_fu, 0, 1).reshape(d, ff)

    glu_w = _bs((SSM_W, d), lambda i, j, k: (0, 0))
    row_s = _bs((tm, SSM_W), lambda i, j, k: (i, 0))
    gate_a = _bs((te, d), lambda i, j, k: (i, 0))
    gate_s = _bs((te, d), lambda i, j, k: (i, 1))
    td_f32, td_bf = SDS((t, d), F32), SDS((t, d), BF16)
    m_bf, ya, yb, attn_d = _mm(
        "glu_merge", (t // tm, 1, 1),
        [("nn", yg_bf, row_s, W_ga, glu_w), ("nn", yg_bf, row_s, W_gb, glu_w), ("nn", attn_bf, row_s, W_ap, glu_w)],
        [(td_bf, row_d)] * 4, extras=[(zg, row_d), (zg, _bs((tm, d), lambda i, j, k: (i, 1)))], epilogue=_glu_merge_epilogue)

    h1, n2 = _mm("out_proj", (t // tm, 1, 1), [("nn", m_bf, row_d, W_out, sq_w)], [(td_f32, row_d), (td_bf, row_d)],
                 extras=[(x2, row_d), (g_ffn, vec_d)], epilogue=_out_norm_epilogue)

    tn_f = ff // 2
    nf = ff // tn_f
    hid_o = _bs((tm, tn_f), lambda j, i, k: (i, j))
    tf_bf = SDS((t, ff), BF16)
    a_rows = _bs((tm, d), lambda j, i, k: (i, 0))
    w_cols = _bs((d, tn_f), lambda j, i, k: (0, j))
    act, fg, fu = _mm("ffn_gate_up", (nf, t // tm, 1), [("nn", n2, a_rows, W_fg, w_cols), ("nn", n2, a_rows, W_fu, w_cols)],
                      [(tf_bf, hid_o)] * 3, epilogue=_swiglu_epilogue)
    w_once = pl.BlockSpec((d, d), lambda i, j, k: (0, 0), pipeline_mode=pl.Buffered(1))
    loss_part, dg_final, dh2, dh2_bf, dpp_bf, dpg_bf, h2_bf = _mm(
        "ffn_down_head", (t // te, 1, 1),
        [("nn", act, _bs((te, ff), lambda i, j, k: (i, 0)), W_fd,
          pl.BlockSpec((ff, d), lambda i, j, k: (0, 0), pipeline_mode=pl.Buffered(1))),
         ("nn", p2, _bs((te, ple), lambda i, j, k: (i, 0)), W_pp, _bs((ple, d), lambda i, j, k: (0, 0)))],
        [(SDS((1, 1), F32), _bs((1, 1), lambda i, j, k: (0, 0))), (SDS((1, d), F32), vec_d), (td_f32, row_e), (td_bf, row_e),
         (td_bf, row_e), (td_bf, row_e), (td_bf, row_e)],
        extras=[(h1, row_e), (g_final.reshape(1, d), vec_d), (tgt, row_e), (W_pg, w_once)], epilogue=_head_epilogue(t // te),
        scratch=[pltpu.VMEM((1, d), F32)])
    loss = lax.psum(loss_part[0, 0], ("x", "y", "c"))

    nkt = t // tk
    tok_a = lambda w: _bs((tk, w), lambda i, j, k: (k, 0))

    def wgrad(name, a, wa, b, wb):
        return _mm(name, (1, 1, nkt), [("tn", a, tok_a(wa), b, tok_a(wb))],
                   [(SDS((wa, wb), BF16), _bs((wa, wb), lambda i, j, k: (0, 0)))])[0]

    dW_pp = wgrad("dw_ple_proj", p2, ple, dpp_bf, d)
    dW_pg = wgrad("dw_ple_gate", h2_bf, d, dpg_bf, d)
    dfg_bf, dfu_bf = _mm("d_ffn_down", (nf, t // tm, 1),
                         [("nt", dh2_bf, a_rows, W_fd, _bs((tn_f, d), lambda j, i, k: (j, 0)))],
                         [(tf_bf, hid_o), (tf_bf, hid_o)], extras=[(fg, hid_o), (fu, hid_o)], epilogue=_swiglu_bwd_epilogue)
    dW_fd, = _mm("dw_ffn_down", (nf, 1, nkt), [("tn", act, _bs((tk, tn_f), lambda i, j, k: (k, i)), dh2_bf, tok_a(d))],
                 [(SDS((ff, d), BF16), _bs((tn_f, d), lambda i, j, k: (i, 0)))])
    hid_t = _bs((tk, tn_f), lambda i, j, k: (k, j))
    wg_o = [(SDS((d, ff), BF16), _bs((d, tn_f), lambda i, j, k: (0, j)))]
    dW_fg, = _mm("dw_ffn_gate", (1, nf, nkt), [("tn", n2, tok_a(d), dfg_bf, hid_t)], wg_o)
    dW_fu, = _mm("dw_ffn_up", (1, nf, nkt), [("tn", n2, tok_a(d), dfu_bf, hid_t)], wg_o)
    dW_fg = jnp.swapaxes(dW_fg.reshape(d, N_DEV, fs), 0, 1)
    dW_fu = jnp.swapaxes(dW_fu.reshape(d, N_DEV, fs), 0, 1)
    group = lambda names: ([kinds[wnames.index(n)] for n in names], [sizes[wnames.index(n)] for n in names])
    ffn_names = ("w_ffn_gate", "w_ffn_up", "w_ffn_down", "w_ple_gate", "w_ple_proj")
    rs_ffn = _exchange_start("scatter_ffn_start", [dW_fg, dW_fu, dW_fd, dW_pg, dW_pp], *group(ffn_names), False)
    hid_all = _bs((te, ff), lambda i, j, k: (i, 0))
    w_all = pl.BlockSpec((d, ff), lambda i, j, k: (0, 0), pipeline_mode=pl.Buffered(1))
    dh1, dh1_bf, dg_ffn = _mm("d_ffn_gate_up", (t // te, 1, 1),
                              [("nt", dfg_bf, hid_all, W_fg, w_all), ("nt", dfu_bf, hid_all, W_fu, w_all)],
                              [(td_f32, row_e), (td_bf, row_e), (SDS((1, d), F32), vec_d)],
                              extras=[(h1, row_e), (g_ffn, vec_d), (dh2, row_e)], epilogue=_rms_bwd_epilogue, after=rs_ffn[3])

    dW_out = wgrad("dw_out", m_bf, d, dh1_bf, d)
    glu_once = pl.BlockSpec((SSM_W, d), lambda i, j, k: (0, 0), pipeline_mode=pl.Buffered(1))
    row_es = _bs((te, SSM_W), lambda i, j, k: (i, 0))
    ts_f32 = SDS((t, SSM_W), F32)
    dz_g, dad_bf, dya_bf, dyb_bf, d_yg, d_attn = _mm(
        "d_out_proj", (t // te, 1, 1), [("nt", dh1_bf, row_e, W_out, w_once)],
        [(SDS((t, 2 * d), BF16), _bs((te, 2 * d), lambda i, j, k: (i, 0))), (td_bf, row_e), (td_bf, row_e), (td_bf, row_e),
         (ts_f32, row_es), (ts_f32, row_es)],
        extras=[(zg, gate_a), (zg, gate_s), (attn_d, row_e), (ya, row_e), (yb, row_e), (W_ga, glu_once), (W_gb, glu_once),
                (W_ap, glu_once)], epilogue=_merge_bwd_epilogue)

    dW_ga = wgrad("dw_glu_a", yg_bf, SSM_W, dya_bf, d)
    dW_gb = wgrad("dw_glu_b", yg_bf, SSM_W, dyb_bf, d)
    dys, dys_bf, dd_skip = _ssm_out_bwd(d_yg, ys, u_perm, te)
    start_b = _ssm_carries("ssm_carries_bwd", dys_bf, cd, "nt", lamc8, pwc, True)
    dz_u, dlam8, dbd, dcd = _ssm_bwd(dys_bf, dys, dsk, u_bf, h_all, bd, cd, lamc8, start_b)
    dlam = jnp.sum(dlam8, axis=0)
    dbb = _diag_blocks(dbd.reshape(2, SSM_NB, BLK, 512), SSM_GROUP, SSM_STATE)
    dbb_re = jnp.swapaxes(dbb[0], 1, 2).reshape(gp, SSM_GROUP)
    dbb_im = jnp.swapaxes(dbb[1], 1, 2).reshape(gp, SSM_GROUP)
    dcc = _diag_blocks(dcd.reshape(2, SSM_NB, 512, BLK), SSM_STATE, SSM_GROUP)
    dc_re, dc_im = jnp.swapaxes(dcc[0], 1, 2), -jnp.swapaxes(dcc[1], 1, 2)
    db_re, db_im, dz_re, dz_im = _ssm_scale_b_bwd(z_re.reshape(gp, 1), z_im.reshape(gp, 1), b_re2, b_im2, dbb_re, dbb_im)
    gshape = (SSM_GROUPS, SSM_STATE)
    da_re, da_im, dlog_dt = _ssm_disc_bwd(a_re[0], a_im[0], log_dt.reshape(SSM_GROUPS, 1), dlam[:gp].reshape(gshape),
                                          dlam[gp:].reshape(gshape), dz_re.reshape(gshape), dz_im.reshape(gshape))

    dW_ap = wgrad("dw_attn_proj", attn_bf, GROUP_W, dad_bf, d)
    pre = _attn_bwd_pre(d_attn, attn, tm)
    das, deltas = pre[:N_GROUPS], pre[N_GROUPS:]
    dqkvs = [_attn_bwd(qkv[g], das[g], lts[g], deltas[g], dil, min(1024, t // dil)) for g, dil in enumerate(DILATIONS)]
    dz_qkv = _undilate_rope_bwd(dqkvs, pos, invf, tm)

    dW_in, = _mm("dw_in_qkv", (1, 3, nkt), [("tn", n1, tok_a(d), dz_qkv, _bs((tk, QK_W), lambda i, j, k: (k, j)))],
                 [(SDS((d, inw), BF16), _bs((d, QK_W), lambda i, j, k: (0, j)))])
    dW_in, = _mm("dw_in_u", (1, 1, nkt), [("tn", n1, tok_a(d), dz_u, tok_a(SSM_W))],
                 [(SDS((d, inw), BF16), _bs((d, SSM_W), lambda i, j, k: (0, ucol)))], alias_to_out0=dW_in)
    dW_in, = _mm("dw_in_gates", (1, 2, nkt), [("tn", n1, tok_a(d), dz_g, _bs((tk, d), lambda i, j, k: (k, j)))],
                 [(SDS((d, inw), BF16), _bs((d, d), lambda i, j, k: (0, gcol + j)))], alias_to_out0=dW_in)
    small_parts = dict(a_re=da_re, a_im=da_im, log_dt=dlog_dt, b_re=db_re, b_im=db_im, c_re=dc_re, c_im=dc_im,
                       d_skip=dd_skip, g_ffn=dg_ffn, g_final=dg_final)
    small = _pack_small([small_parts[n] for n in _SMALL])
    rest_names = ("w_in", "w_attn_proj", "w_glu_a", "w_glu_b", "w_out")
    rest_kinds, rest_sizes = group(rest_names)
    rs_in = _exchange_start("scatter_rest_start", [dW_in, dW_ap, dW_ga, dW_gb, dW_out, small], rest_kinds + ["all"],
                            rest_sizes + [0], False)
    w_piece = lambda w, cb: pl.BlockSpec((d, w), lambda i, j, k: (0, cb), pipeline_mode=pl.Buffered(1))
    dx, dg_mix = _mm(
        "d_z_proj", (t // te, 1, 1),
        [("nt", dz_qkv, _bs((te, 3 * QK_W), lambda i, j, k: (i, 0)), W_in, w_piece(3 * QK_W, 0)),
         ("nt", dz_u, _bs((te, SSM_W), lambda i, j, k: (i, 0)), W_in, w_piece(SSM_W, ucol)),
         ("nt", dz_g, _bs((te, d), lambda i, j, k: (i, 0)), W_in, w_piece(d, gcol)),
         ("nt", dz_g, _bs((te, d), lambda i, j, k: (i, 1)), W_in, w_piece(d, gcol + 1))],
        [(td_f32, row_e), (SDS((1, d), F32), vec_d)],
        extras=[(x2, row_e), (g_mix, vec_d), (dh1, row_e)], epilogue=_rms_bwd_epilogue, after=rs_in[3])

    received = dict(zip(ffn_names, _exchange_wait("scatter_ffn_wait", rs_ffn, list(range(len(ffn_names))), *group(ffn_names),
                                                  False, dx)))
    *landed, small_all = _exchange_wait("scatter_rest_wait", rs_in, list(range(len(rest_names) + 1)), rest_kinds + ["all"],
                                        rest_sizes + [0], False, dx)
    received.update(zip(rest_names, landed))

    new = {}
    for n in wnames:
        new[n] = [o.reshape(args[n].shape)
                  for o in _adamw("adamw_" + n, received[n], args[n][0], args["m_" + n][0], args["v_" + n][0])]
    g_mix_all = _gather_small(_pack_small([dg_mix]))
    pk = lambda pre: jnp.concatenate([_pack_small([args[pre + n] for n in _SMALL]), _pack_small([args[pre + "g_mix"]])])
    sm = _adamw("adamw_small", jnp.concatenate([small_all, g_mix_all], axis=1), pk(""), pk("m_"), pk("v_"))
    rows_a = small.shape[0]
    shapes = [args[n].shape for n in _SMALL]
    for n, vals in zip(_SMALL, zip(*[_unpack_small(o[:rows_a], shapes) for o in sm])):
        new[n] = list(vals)
    new["g_mix"] = [_unpack_small(o[rows_a:], [g_mix.shape])[0] for o in sm]

    order = ("g_mix", "w_in", "a_re", "a_im", "log_dt", "b_re", "b_im", "c_re", "c_im", "d_skip", "w_attn_proj", "w_glu_a",
             "w_glu_b", "w_out", "g_ffn", "w_ffn_gate", "w_ffn_up", "w_ffn_down", "w_ple_gate", "w_ple_proj", "g_final")
    return (loss, dx.reshape(x.shape), *[new[n][0] for n in order], *[new[n][1] for n in order],
            *[new[n][2] for n in order], *[new[n][3] for n in order])
```

```python
import functools
import math

import jax
import jax.numpy as jnp
from jax import lax
from jax.experimental import pallas as pl
from jax.experimental.pallas import tpu as pltpu

F32 = jnp.float32
BF16 = jnp.bfloat16
SDS = jax.ShapeDtypeStruct

N_DEV = 8
HEAD_DIM = 128
HEADS_PER_GROUP = 4
GROUP_W = HEADS_PER_GROUP * HEAD_DIM
DILATIONS = (1, 4, 16)
N_GROUPS = len(DILATIONS)
QK_W = N_GROUPS * GROUP_W
BLK = 128
ROPE_THETA = 500000.0
ROPE_DIM = HEAD_DIM // 4
ROPE_HALF = ROPE_DIM // 2
SSM_W = 512
SSM_GROUP = 16
SSM_GROUPS = SSM_W // SSM_GROUP
SSM_STATE = 64
NSTATE = SSM_GROUPS * SSM_STATE
SSM_NB = 4
EPS = 1e-6
ADAM_LR, ADAM_B1, ADAM_B2, ADAM_EPS, ADAM_WD, ADAM_STEP = 0.001, 0.9, 0.999, 1e-08, 0.01, 10
NEG = -1e30

VMEM_LIMIT = 52 * 1024 * 1024
SCAN_ROWS = 512
SCAN_LANES = 512


def _cp(n):
    return pltpu.CompilerParams(dimension_semantics=("arbitrary",) * n, vmem_limit_bytes=VMEM_LIMIT)


def _sigmoid(x):
    return 0.5 * jnp.tanh(0.5 * x) + 0.5


_DNUMS = {"nn": (((1,), (0,)), ((), ())), "nt": (((1,), (1,)), ((), ())), "tn": (((0,), (0,)), ((), ()))}


def _bs(shape, fn):
    return pl.BlockSpec(shape, fn)


def _store_all(prods, extra_refs, out_refs, scratch_refs):
    r = prods[0]
    for p in prods[1:]:
        r = r + p
    for e in extra_refs:
        r = r + e[...]
    for o in out_refs:
        o[...] = r.astype(o.dtype)


def _mm(name, grid, pairs, outs, extras=(), epilogue=_store_all, scratch=(), alias_to_out0=None, after=None):
    nk = grid[2]
    npair = len(pairs)
    steps = [p[5] if len(p) > 5 else nk for p in pairs]

    def block(spec):
        return tuple(s for s in spec.block_shape if s is not None)

    def rows2d(shape):
        return (math.prod(shape[:-1]), shape[-1]) if len(shape) == 3 else shape

    acc_shapes = [jax.eval_shape(lambda u, v, dn=_DNUMS[p[0]]: lax.dot_general(u, v, dn, preferred_element_type=F32),
                                 SDS(rows2d(block(p[2])), BF16), SDS(block(p[4]), BF16)).shape for p in pairs]
    if nk == 1:
        acc_shapes = []
    n_in = 2 * npair + len(extras) + (alias_to_out0 is not None) + (after is not None)

    def body(*refs):
        extra_refs = refs[2 * npair:2 * npair + len(extras)]
        out_refs = refs[n_in:n_in + len(outs)]
        rest = refs[n_in + len(outs):]
        acc_refs = rest[:len(acc_shapes)]
        scratch_refs = rest[len(acc_refs):]
        k = pl.program_id(2)

        def product(i):
            a = refs[2 * i][...]
            if a.ndim == 3:
                a = a.reshape(-1, a.shape[-1])
            return lax.dot_general(a.astype(BF16), refs[2 * i + 1][...].astype(BF16), _DNUMS[pairs[i][0]],
                                   preferred_element_type=F32)

        if nk == 1:
            epilogue([product(i) for i in range(npair)], extra_refs, out_refs, scratch_refs)
            return
        for i in range(npair):
            @pl.when(k == 0)
            def _(i=i):
                acc_refs[i][...] = product(i)

            @pl.when((k > 0) & (k < steps[i]))
            def _(i=i):
                acc_refs[i][...] += product(i)

        @pl.when(k == nk - 1)
        def _():
            epilogue([a[...] for a in acc_refs], extra_refs, out_refs, scratch_refs)

    ins, in_specs = [], []
    for p in pairs:
        ins += [p[1], p[3]]
        in_specs += [p[2], p[4]]
    ins += [e[0] for e in extras]
    in_specs += [e[1] for e in extras]
    aliases = {}
    if alias_to_out0 is not None:
        aliases = {len(ins): 0}
        ins.append(alias_to_out0)
        in_specs.append(pl.BlockSpec(memory_space=pl.ANY))
    if after is not None:
        ins.append(after)
        in_specs.append(pl.BlockSpec(memory_space=pl.ANY))
    scratch_shapes = [pltpu.VMEM(s, F32) for s in acc_shapes] + list(scratch)
    return pl.pallas_call(body, grid=grid, in_specs=in_specs, out_specs=[o[1] for o in outs], out_shape=[o[0] for o in outs],
                          scratch_shapes=scratch_shapes, input_output_aliases=aliases, compiler_params=_cp(3), name=name)(*ins)


def _my_index():
    return 4 * lax.axis_index("x") + 2 * lax.axis_index("y") + lax.axis_index("c")


def _peer(d):
    mx, my, mc = lax.axis_index("x"), lax.axis_index("y"), lax.axis_index("c")
    return (mx ^ ((d >> 2) & 1), my ^ ((d >> 1) & 1), mc ^ (d & 1))


def _win(ref, kind, j, n):
    if kind == "all":
        return ref
    if kind == "slot":
        return ref.at[j]
    if kind == "rows":
        return ref.at[pl.ds(pl.multiple_of(j * n, 8), n)]
    return ref.at[:, pl.ds(pl.multiple_of(j * n, 128), n)]


def _win7(ref, kind, n):
    if kind == "slot":
        return ref.at[pl.ds(0, 7)]
    if kind == "rows":
        return ref.at[pl.ds(0, 7 * n)]
    return ref.at[:, pl.ds(0, 7 * n)]


def _full_shape(shard_shape, kind):
    if kind == "slot":
        return (N_DEV,) + tuple(shard_shape)
    if kind == "rows":
        return (N_DEV * shard_shape[0],) + tuple(shard_shape[1:])
    return (shard_shape[0], N_DEV * shard_shape[1])


def _shard_shape(full_shape, kind, n):
    if kind == "all":
        return tuple(full_shape)
    if kind == "slot":
        return tuple(full_shape[1:])
    if kind == "rows":
        return (n,) + tuple(full_shape[1:])
    return (full_shape[0], n)


_HBM = pl.BlockSpec(memory_space=pltpu.HBM)
_SEM = pl.BlockSpec(memory_space=pltpu.SEMAPHORE)
_DATAFLOW = pltpu.SideEffectType.DATAFLOW_SIDE_EFFECTING


def _exchange_start(name, srcs, kinds, sizes, gather):
    n = len(srcs)
    if gather:
        lands = [lax.empty(_full_shape(s.shape, k), s.dtype) for s, k in zip(srcs, kinds)]
    else:
        lands = [lax.empty((N_DEV,) + _shard_shape(s.shape, k, z), s.dtype) for s, k, z in zip(srcs, kinds, sizes)]

    def body(*refs):
        src, land = refs[:n], refs[n:2 * n]
        send_sems, recv_sems, local_sems = refs[2 * n], refs[2 * n + 1], refs[2 * n + 2]
        token = refs[4 * n + 3]
        me = _my_index()
        for a in range(n):
            _local_copy(src[a], land[a], kinds[a], sizes[a], gather, me, local_sems.at[a]).start()
        for a in range(n):
            for d in range(1, N_DEV):
                px, py, pc = _peer(d)
                if gather:
                    s_ref, d_ref = src[a], _win(land[a], kinds[a], me, sizes[a])
                else:
                    s_ref, d_ref = _win(src[a], kinds[a], 4 * px + 2 * py + pc, sizes[a]), land[a].at[me]
                pltpu.make_async_remote_copy(src_ref=s_ref, dst_ref=d_ref, send_sem=send_sems.at[a], recv_sem=recv_sems.at[a],
                                             device_id=(px, py, pc), device_id_type=pl.DeviceIdType.MESH).start()
        token[...] = jnp.zeros_like(token)

    hbm = [pltpu.with_memory_space_constraint(a, pltpu.HBM) for a in list(srcs) + lands]
    out = pl.pallas_call(
        body, name=name, in_specs=[_HBM] * (2 * n),
        out_shape=[pltpu.SemaphoreType.DMA((n,))] * 3 + [pltpu.HBM(a.shape, a.dtype) for a in hbm] + [SDS((8, 128), F32)],
        out_specs=[_SEM] * 3 + [_HBM] * (2 * n) + [pl.BlockSpec(memory_space=pltpu.VMEM)],
        input_output_aliases={i: 3 + i for i in range(2 * n)},
        compiler_params=pltpu.CompilerParams(has_side_effects=_DATAFLOW))(*hbm)
    return out[0:3], out[3:3 + n], out[3 + n:3 + 2 * n], out[-1]


def _local_copy(src, land, kind, size, gather, me, sem):
    if gather:
        return pltpu.make_async_copy(src, _win(land, kind, me, size), sem)
    return pltpu.make_async_copy(_win(src, kind, me, size), land.at[me], sem)


def _exchange_wait(name, started, which, kinds, sizes, gather, after):
    sems, srcs, lands, _ = started
    n = len(which)
    after = list(after) if isinstance(after, (list, tuple)) else [after]

    def body(*refs):
        src, land = refs[:n], refs[n:2 * n]
        send_ref, recv_ref, local_ref = refs[2 * n:2 * n + 3]
        me = _my_index()
        my_id = (lax.axis_index("x"), lax.axis_index("y"), lax.axis_index("c"))
        for i, a in enumerate(which):
            seven = _win7(land[i], kinds[a], sizes[a]) if gather else land[i].at[pl.ds(0, 7)]
            pltpu.make_async_remote_copy(src_ref=seven, dst_ref=seven, send_sem=send_ref.at[a], recv_sem=recv_ref.at[a],
                                         device_id=my_id, device_id_type=pl.DeviceIdType.MESH).wait()
            _local_copy(src[i], land[i], kinds[a], sizes[a], gather, me, local_ref.at[a]).wait()

    hbm = [srcs[a] for a in which] + [lands[a] for a in which]
    out = pl.pallas_call(
        body, name=name, in_specs=[_HBM] * (2 * n) + [_SEM] * 3 + [pl.BlockSpec(memory_space=pl.ANY)] * len(after),
        out_shape=[pltpu.HBM(a.shape, a.dtype) for a in hbm], out_specs=[_HBM] * (2 * n),
        input_output_aliases={i: i for i in range(2 * n)},
        compiler_params=pltpu.CompilerParams(has_side_effects=_DATAFLOW))(*hbm, *sems, *after)
    return out[n:]


def _gather_small(small):
    def body(in_ref, out_ref, send_sem, recv_sem, local_sem):
        me = _my_index()
        my_id = (lax.axis_index("x"), lax.axis_index("y"), lax.axis_index("c"))
        cp = pltpu.make_async_copy(in_ref, out_ref.at[me], local_sem)
        cp.start()
        for d in range(1, N_DEV):
            pltpu.make_async_remote_copy(src_ref=in_ref, dst_ref=out_ref.at[me], send_sem=send_sem, recv_sem=recv_sem,
                                         device_id=_peer(d), device_id_type=pl.DeviceIdType.MESH).start()
        seven = out_ref.at[pl.ds(0, 7)]
        pltpu.make_async_remote_copy(src_ref=seven, dst_ref=seven, send_sem=send_sem, recv_sem=recv_sem, device_id=my_id,
                                     device_id_type=pl.DeviceIdType.MESH).wait()
        cp.wait()

    any_spec = pl.BlockSpec(memory_space=pl.ANY)
    return pl.pallas_call(body, in_specs=[any_spec], out_specs=any_spec, out_shape=SDS((N_DEV,) + small.shape, F32),
                          scratch_shapes=[pltpu.SemaphoreType.DMA] * 3, name="gather_small")(small)


def _adamw(name, recv, w, m, v):
    rows, cols = w.shape
    tr = max(c for c in range(16, 257, 16) if rows % c == 0) if rows % 16 == 0 else rows

    def body(r_ref, w_ref, m_ref, v_ref, g_ref, d_ref, nm_ref, nv_ref):
        g = r_ref[0].astype(F32)
        for s in range(1, N_DEV):
            g = g + r_ref[s].astype(F32)
        nm = ADAM_B1 * m_ref[...] + (1.0 - ADAM_B1) * g
        nv = ADAM_B2 * v_ref[...] + (1.0 - ADAM_B2) * (g * g)
        m_hat = nm / (1.0 - ADAM_B1 ** ADAM_STEP)
        v_hat = nv / (1.0 - ADAM_B2 ** ADAM_STEP)
        g_ref[...] = g
        d_ref[...] = -ADAM_LR * (m_hat / (jnp.sqrt(v_hat) + ADAM_EPS) + ADAM_WD * w_ref[...])
        nm_ref[...] = nm
        nv_ref[...] = nv

    blk = _bs((tr, cols), lambda i: (i, 0))
    return pl.pallas_call(
        body, grid=(rows // tr,), in_specs=[_bs((N_DEV, tr, cols), lambda i: (0, i, 0)), blk, blk, blk],
        out_specs=[blk] * 4, out_shape=[SDS((rows, cols), F32)] * 4, compiler_params=_cp(1), name=name)(recv, w, m, v)


def _rms_fwd(name, x, g, tm):
    t, d = x.shape

    def body(x_ref, g_ref, n_ref):
        xv = x_ref[...]
        r = lax.rsqrt(jnp.mean(xv * xv, axis=-1, keepdims=True) + EPS)
        n_ref[...] = (xv * r * g_ref[...]).astype(BF16)

    return pl.pallas_call(body, grid=(t // tm,), in_specs=[_bs((tm, d), lambda i: (i, 0)), _bs((1, d), lambda i: (0, 0))],
                          out_specs=_bs((tm, d), lambda i: (i, 0)), out_shape=SDS((t, d), BF16), compiler_params=_cp(1),
                          name=name)(x, g)


def _accumulate_rows(ref, part):
    @pl.when(pl.program_id(0) == 0)
    def _():
        ref[...] = part

    @pl.when(pl.program_id(0) > 0)
    def _():
        ref[...] += part


def _rms_bwd_epilogue(prods, extra_refs, out_refs, scratch_refs):
    dyv = prods[0]
    for p in prods[1:]:
        dyv = dyv + p
    if len(extra_refs) > 3:
        dyv = dyv + extra_refs[3][...]
    xv = extra_refs[0][...]
    r = lax.rsqrt(jnp.mean(xv * xv, axis=-1, keepdims=True) + EPS)
    xh = xv * r
    dxh = dyv * extra_refs[1][...]
    dx = extra_refs[2][...] + r * (dxh - xh * jnp.mean(dxh * xh, axis=-1, keepdims=True))
    for o in out_refs[:-1]:
        o[...] = dx.astype(o.dtype)
    _accumulate_rows(out_refs[-1], jnp.sum(dyv * xh, axis=0, keepdims=True))


def _out_norm_epilogue(prods, extra_refs, out_refs, scratch_refs):
    h = prods[0] + extra_refs[0][...]
    r = lax.rsqrt(jnp.mean(h * h, axis=-1, keepdims=True) + EPS)
    out_refs[0][...] = h
    out_refs[1][...] = (h * r * extra_refs[1][...]).astype(BF16)


def _glu_merge_epilogue(prods, extra_refs, out_refs, scratch_refs):
    ya, yb, ad = prods
    ga, gs = extra_refs[0][...].astype(F32), extra_refs[1][...].astype(F32)
    m = _sigmoid(ga) * ad + _sigmoid(gs) * (ya * _sigmoid(yb))
    out_refs[0][...] = m.astype(BF16)
    for o, val in zip(out_refs[1:], (ya, yb, ad)):
        o[...] = val.astype(o.dtype)


def _merge_bwd_epilogue(prods, extra_refs, out_refs, scratch_refs):
    dmv = prods[0]
    d = dmv.shape[1]
    ga, gs = _sigmoid(extra_refs[0][...].astype(F32)), _sigmoid(extra_refs[1][...].astype(F32))
    adv, yav = extra_refs[2][...].astype(F32), extra_refs[3][...].astype(F32)
    sb = _sigmoid(extra_refs[4][...].astype(F32))
    out_refs[0][:, 0:d] = (dmv * adv * ga * (1.0 - ga)).astype(BF16)
    out_refs[0][:, d:2 * d] = (dmv * (yav * sb) * gs * (1.0 - gs)).astype(BF16)
    dad = (dmv * ga).astype(BF16)
    dsd = dmv * gs
    dya = (dsd * sb).astype(BF16)
    dyb = (dsd * yav * sb * (1.0 - sb)).astype(BF16)
    out_refs[1][...], out_refs[2][...], out_refs[3][...] = dad, dya, dyb
    nt = _DNUMS["nt"]
    out_refs[4][...] = (lax.dot_general(dya, extra_refs[5][...], nt, preferred_element_type=F32)
                        + lax.dot_general(dyb, extra_refs[6][...], nt, preferred_element_type=F32))
    out_refs[5][...] = lax.dot_general(dad, extra_refs[7][...], nt, preferred_element_type=F32)


def _swiglu_epilogue(prods, extra_refs, out_refs, scratch_refs):
    gv, uv = prods
    out_refs[0][...] = (gv * _sigmoid(gv) * uv).astype(BF16)
    out_refs[1][...] = gv.astype(out_refs[1].dtype)
    out_refs[2][...] = uv.astype(out_refs[2].dtype)


def _swiglu_bwd_epilogue(prods, extra_refs, out_refs, scratch_refs):
    dav = prods[0]
    gv, uv = extra_refs[0][...].astype(F32), extra_refs[1][...].astype(F32)
    sg = _sigmoid(gv)
    out_refs[0][...] = (dav * uv * sg * (1.0 + gv * (1.0 - sg))).astype(BF16)
    out_refs[1][...] = (dav * gv * sg).astype(BF16)


def _head_epilogue(n_tiles):
    def epilogue(prods, extra_refs, out_refs, scratch_refs):
        h2 = prods[0] + extra_refs[0][...]
        h2_bf = h2.astype(BF16)
        out_refs[6][...] = h2_bf
        pgv = jnp.dot(h2_bf, extra_refs[3][...], preferred_element_type=F32)
        ppv = prods[1]
        d = pgv.shape[1]
        lacc = scratch_refs[0]
        sg = _sigmoid(pgv)
        h3 = h2 + sg * ppv
        r = lax.rsqrt(jnp.mean(h3 * h3, axis=-1, keepdims=True) + EPS)
        xh = h3 * r
        gv = extra_refs[1][...]
        diff = xh * gv - extra_refs[2][...]
        dout = diff * (1.0 / d)
        dxh = dout * gv
        dh3 = r * (dxh - xh * jnp.mean(dxh * xh, axis=-1, keepdims=True))
        dpg = (dh3 * ppv * sg * (1.0 - sg)).astype(BF16)
        dh2 = dh3 + lax.dot_general(dpg, extra_refs[3][...], _DNUMS["nt"], preferred_element_type=F32)
        out_refs[2][...] = dh2
        out_refs[3][...] = dh2.astype(BF16)
        out_refs[4][...] = (dh3 * sg).astype(BF16)
        out_refs[5][...] = dpg
        _accumulate_rows(out_refs[1], jnp.sum(dout * xh, axis=0, keepdims=True))
        _accumulate_rows(lacc, jnp.sum(diff * diff, axis=0, keepdims=True))

        @pl.when(pl.program_id(0) == n_tiles - 1)
        def _():
            out_refs[0][...] = (0.5 / d) * jnp.sum(lacc[...], axis=-1, keepdims=True)

    return epilogue


def _strided(r, n, d):
    return pl.ds(r, n, stride=d) if d > 1 else pl.ds(0, n)


def _rope_tables(pos_ref, invf_ref, c_s, s_s):
    ang = pos_ref[...].astype(F32) * invf_ref[...]
    lane = lax.broadcasted_iota(jnp.int32, ang.shape, 1)
    sn = jnp.sin(ang)
    c_s[...] = jnp.where(lane < ROPE_DIM, jnp.cos(ang), 1.0)
    s_s[...] = jnp.where(lane < ROPE_HALF, -sn, jnp.where(lane < ROPE_DIM, sn, 0.0))


def _rope_partner(xv, first_half):
    return jnp.where(first_half, pltpu.roll(xv, HEAD_DIM - ROPE_HALF, 1), pltpu.roll(xv, ROPE_HALF, 1))


def _rope_dilate_epilogue(tm):
    def epilogue(prods, extra_refs, out_refs, scratch_refs):
        zv = prods[0]
        pos_ref, invf_ref = extra_refs
        c_s, s_s, rot = scratch_refs
        c = pl.program_id(1)

        @pl.when(c == 0)
        def _():
            _rope_tables(pos_ref, invf_ref, c_s, s_s)

        @pl.when(c < 2)
        def _():
            cc, ss = c_s[...], s_s[...]
            first_half = lax.broadcasted_iota(jnp.int32, cc.shape, 1) < ROPE_HALF
            for h in range(QK_W // HEAD_DIM):
                xv = zv[:, h * HEAD_DIM:(h + 1) * HEAD_DIM]
                rot[h] = xv * cc + _rope_partner(xv, first_half) * ss

        @pl.when(c == 2)
        def _():
            for h in range(QK_W // HEAD_DIM):
                rot[h] = zv[:, h * HEAD_DIM:(h + 1) * HEAD_DIM]

        for g, (d, o_ref) in enumerate(zip(DILATIONS, out_refs)):
            n = tm // d
            for r in range(d):
                for hh in range(HEADS_PER_GROUP):
                    oc = r * GROUP_W + hh * HEAD_DIM
                    o_ref[:, oc:oc + HEAD_DIM] = rot[g * HEADS_PER_GROUP + hh, _strided(r, n, d), :].astype(BF16)

    return epilogue


def _band_masks(first_tile):
    qi = lax.broadcasted_iota(jnp.int32, (BLK, 2 * BLK), 0)
    kj = lax.broadcasted_iota(jnp.int32, (BLK, 2 * BLK), 1)
    band = (kj >= qi) & (kj <= qi + BLK)
    return band, band & ((kj >= BLK) | jnp.logical_not(first_tile))


def _attn_fwd(qkv, d, qt):
    ell = qkv.shape[1]
    nsub = qt // BLK
    scale = 1.0 / math.sqrt(HEAD_DIM)

    def body(q_ref, kc_ref, kp_ref, vc_ref, vp_ref, o_ref, lse_ref, kcat, vcat):
        nb = pl.program_id(1)
        kcat[0:BLK, :] = kp_ref[...]
        kcat[BLK:, :] = kc_ref[...]
        vcat[0:BLK, :] = vp_ref[...]
        vcat[BLK:, :] = vc_ref[...]
        lane = lax.broadcasted_iota(jnp.int32, (BLK, HEAD_DIM), 1)
        band, band_first = _band_masks(nb == 0)
        for b in range(nsub):
            valid = band_first if b == 0 else band
            lse_t = jnp.zeros((BLK, HEAD_DIM), F32)
            for hh in range(HEADS_PER_GROUP):
                cs = slice(hh * HEAD_DIM, (hh + 1) * HEAD_DIM)
                qb = q_ref[b * BLK:(b + 1) * BLK, cs]
                kk = kcat[b * BLK:(b + 2) * BLK, cs]
                vv = vcat[b * BLK:(b + 2) * BLK, cs]
                s = lax.dot_general(qb, kk, _DNUMS["nt"], preferred_element_type=F32) * scale
                s = jnp.where(valid, s, NEG)
                mx = jnp.max(s, axis=-1, keepdims=True)
                p = jnp.exp(s - mx)
                den = jnp.sum(p, axis=-1, keepdims=True)
                o = jnp.dot(p.astype(BF16), vv, preferred_element_type=F32) / den
                o_ref[b * BLK:(b + 1) * BLK, cs] = o
                lse_t = jnp.where(lane == hh, mx + jnp.log(den), lse_t)
            lse_ref[b * BLK:(b + 1) * BLK, :] = lse_t

    cur = lambda c: _bs((None, qt, GROUP_W), lambda r, nb: (c, nb, r))
    prev = lambda c: _bs((None, BLK, GROUP_W), lambda r, nb: (c, jnp.maximum(nb * nsub - 1, 0), r))
    return pl.pallas_call(
        body, grid=(d, ell // qt), in_specs=[cur(0), cur(1), prev(1), cur(2), prev(2)],
        out_specs=[_bs((qt, GROUP_W), lambda r, nb: (nb, r)), _bs((None, qt, HEAD_DIM), lambda r, nb: (r, nb, 0))],
        out_shape=[SDS((ell, d * GROUP_W), F32), SDS((d, ell, HEAD_DIM), F32)],
        scratch_shapes=[pltpu.VMEM((qt + BLK, GROUP_W), BF16)] * 2, compiler_params=_cp(2), name=f"attn_fwd_d{d}")(
            qkv, qkv, qkv, qkv, qkv)


def _attn_merge(outs, lses, tm):
    t = outs[0].shape[0]

    def body(o0, o1, o2, l0, l1, l2, attn_ref, attn_bf_ref, t0, t1, t2, so, sl, lt_s):
        for g, (d, o_ref, l_ref) in enumerate(zip(DILATIONS, (o0, o1, o2), (l0, l1, l2))):
            n = tm // d
            for r in range(d):
                rows = _strided(r, n, d)
                for hh in range(HEADS_PER_GROUP):
                    oc = r * GROUP_W + hh * HEAD_DIM
                    so[g * HEADS_PER_GROUP + hh, rows, :] = o_ref[:, oc:oc + HEAD_DIM]
                sl[g, rows, :] = l_ref[r]
        ls = [sl[g] for g in range(N_GROUPS)]
        mx = jnp.maximum(jnp.maximum(ls[0], ls[1]), ls[2])
        es = [jnp.exp(l - mx) for l in ls]
        den = es[0] + es[1] + es[2]
        ws = [e / den for e in es]
        lt_s[...] = mx + jnp.log(den)
        for hh in range(HEADS_PER_GROUP):
            cs = slice(hh * HEAD_DIM, (hh + 1) * HEAD_DIM)
            a = ws[0][:, hh:hh + 1] * so[hh]
            for g in range(1, N_GROUPS):
                a = a + ws[g][:, hh:hh + 1] * so[g * HEADS_PER_GROUP + hh]
            attn_ref[:, cs] = a
            attn_bf_ref[:, cs] = a.astype(BF16)
        for d, t_ref in zip(DILATIONS, (t0, t1, t2)):
            n = tm // d
            for r in range(d):
                t_ref[r] = lt_s[_strided(r, n, d), :]

    dil = lambda d: _bs((tm // d, d * GROUP_W), lambda i: (i, 0))
    lsp = lambda d: _bs((d, tm // d, HEAD_DIM), lambda i: (0, i, 0))
    row = _bs((tm, GROUP_W), lambda i: (i, 0))
    return pl.pallas_call(
        body, grid=(t // tm,),
        in_specs=[dil(d) for d in DILATIONS] + [lsp(d) for d in DILATIONS],
        out_specs=[row, row] + [lsp(d) for d in DILATIONS],
        out_shape=[SDS((t, GROUP_W), F32), SDS((t, GROUP_W), BF16)] + [SDS(l.shape, F32) for l in lses],
        scratch_shapes=[pltpu.VMEM((N_GROUPS * HEADS_PER_GROUP, tm, HEAD_DIM), F32), pltpu.VMEM((N_GROUPS, tm, HEAD_DIM), F32),
                        pltpu.VMEM((tm, HEAD_DIM), F32)],
        compiler_params=_cp(1), name="attn_merge")(*outs, *lses)


def _attn_bwd_pre(d_attn, attn, tm):
    t = attn.shape[0]

    def body(da_ref, a_ref, g0, g1, g2, e0, e1, e2, dl_s, da_s):
        lane = lax.broadcasted_iota(jnp.int32, (tm, HEAD_DIM), 1)
        dl = jnp.zeros((tm, HEAD_DIM), F32)
        for hh in range(HEADS_PER_GROUP):
            cs = slice(hh * HEAD_DIM, (hh + 1) * HEAD_DIM)
            dav = da_ref[:, cs]
            da_s[hh] = dav
            dl = jnp.where(lane == hh, jnp.sum(dav * a_ref[:, cs], axis=-1, keepdims=True), dl)
        dl_s[...] = dl
        for d, g_ref, e_ref in zip(DILATIONS, (g0, g1, g2), (e0, e1, e2)):
            n = tm // d
            for r in range(d):
                rows = _strided(r, n, d)
                for hh in range(HEADS_PER_GROUP):
                    oc = r * GROUP_W + hh * HEAD_DIM
                    g_ref[:, oc:oc + HEAD_DIM] = da_s[hh, rows, :].astype(BF16)
                e_ref[r] = dl_s[rows, :]

    row = _bs((tm, GROUP_W), lambda i: (i, 0))
    return pl.pallas_call(
        body, grid=(t // tm,), in_specs=[row, row],
        out_specs=[_bs((tm // d, d * GROUP_W), lambda i: (i, 0)) for d in DILATIONS]
        + [_bs((d, tm // d, HEAD_DIM), lambda i: (0, i, 0)) for d in DILATIONS],
        out_shape=[SDS((t // d, d * GROUP_W), BF16) for d in DILATIONS]
        + [SDS((d, t // d, HEAD_DIM), F32) for d in DILATIONS],
        scratch_shapes=[pltpu.VMEM((tm, HEAD_DIM), F32), pltpu.VMEM((HEADS_PER_GROUP, tm, HEAD_DIM), F32)],
        compiler_params=_cp(1), name="attn_bwd_pre")(d_attn, attn)


def _attn_bwd(qkv, d_a, lt, delta, d, qt):
    ell = qkv.shape[1]
    nsub = qt // BLK
    ntile = ell // qt
    nblk = ell // BLK
    scale = 1.0 / math.sqrt(HEAD_DIM)

    def body(q_ref, qn_ref, kc_ref, kp_ref, vc_ref, vp_ref, da_ref, dan_ref, lt_ref, ltn_ref, dl_ref, dln_ref, o_ref,
             kcat, vcat, dk_acc, dv_acc):
        nb = pl.program_id(1)
        kcat[0:BLK, :] = kp_ref[...]
        kcat[BLK:, :] = kc_ref[...]
        vcat[0:BLK, :] = vp_ref[...]
        vcat[BLK:, :] = vc_ref[...]
        qi = lax.broadcasted_iota(jnp.int32, (BLK, BLK), 0)
        kj = lax.broadcasted_iota(jnp.int32, (BLK, BLK), 1)
        valid_next = (kj >= qi) & (nb < ntile - 1)
        band, band_first = _band_masks(nb == 0)
        for hh in range(HEADS_PER_GROUP):
            cs = slice(hh * HEAD_DIM, (hh + 1) * HEAD_DIM)
            dk_acc[...] = jnp.zeros_like(dk_acc)
            dv_acc[...] = jnp.zeros_like(dv_acc)
            for b in range(nsub):
                rs = slice(b * BLK, (b + 1) * BLK)
                ks = slice(b * BLK, (b + 2) * BLK)
                valid = band_first if b == 0 else band
                qb, kk, vv, dab = q_ref[rs, cs], kcat[ks, cs], vcat[ks, cs], da_ref[rs, cs]
                s = lax.dot_general(qb, kk, _DNUMS["nt"], preferred_element_type=F32) * scale
                p = jnp.where(valid, jnp.exp(s - lt_ref[rs, hh:hh + 1]), 0.0)
                dp = lax.dot_general(dab, vv, _DNUMS["nt"], preferred_element_type=F32)
                ds = (p * (dp - dl_ref[rs, hh:hh + 1])).astype(BF16)
                o_ref[0, rs, cs] = jnp.dot(ds, kk, preferred_element_type=F32) * scale
                dk_acc[ks, :] += lax.dot_general(ds, qb, _DNUMS["tn"], preferred_element_type=F32) * scale
                dv_acc[ks, :] += lax.dot_general(p.astype(BF16), dab, _DNUMS["tn"], preferred_element_type=F32)
            ks = slice(nsub * BLK, (nsub + 1) * BLK)
            qn, kl, vl, dan = qn_ref[:, cs], kcat[ks, cs], vcat[ks, cs], dan_ref[:, cs]
            s = lax.dot_general(qn, kl, _DNUMS["nt"], preferred_element_type=F32) * scale
            p = jnp.where(valid_next, jnp.exp(s - ltn_ref[:, hh:hh + 1]), 0.0)
            dp = lax.dot_general(dan, vl, _DNUMS["nt"], preferred_element_type=F32)
            ds = (p * (dp - dln_ref[:, hh:hh + 1])).astype(BF16)
            dk_acc[ks, :] += lax.dot_general(ds, qn, _DNUMS["tn"], preferred_element_type=F32) * scale
            dv_acc[ks, :] += lax.dot_general(p.astype(BF16), dan, _DNUMS["tn"], preferred_element_type=F32)
            o_ref[1, :, cs] = dk_acc[BLK:, :]
            o_ref[2, :, cs] = dv_acc[BLK:, :]

    nxt = lambda nb: jnp.minimum((nb + 1) * nsub, nblk - 1)
    prv = lambda nb: jnp.maximum(nb * nsub - 1, 0)
    cur3 = lambda c: _bs((None, qt, GROUP_W), lambda r, nb: (c, nb, r))
    in_specs = [
        cur3(0), _bs((None, BLK, GROUP_W), lambda r, nb: (0, nxt(nb), r)),
        cur3(1), _bs((None, BLK, GROUP_W), lambda r, nb: (1, prv(nb), r)),
        cur3(2), _bs((None, BLK, GROUP_W), lambda r, nb: (2, prv(nb), r)),
        _bs((qt, GROUP_W), lambda r, nb: (nb, r)), _bs((BLK, GROUP_W), lambda r, nb: (nxt(nb), r)),
        _bs((None, qt, HEAD_DIM), lambda r, nb: (r, nb, 0)), _bs((None, BLK, HEAD_DIM), lambda r, nb: (r, nxt(nb), 0)),
        _bs((None, qt, HEAD_DIM), lambda r, nb: (r, nb, 0)), _bs((None, BLK, HEAD_DIM), lambda r, nb: (r, nxt(nb), 0)),
    ]
    return pl.pallas_call(
        body, grid=(d, ntile), in_specs=in_specs, out_specs=_bs((3, qt, GROUP_W), lambda r, nb: (0, nb, r)),
        out_shape=SDS((3, ell, d * GROUP_W), F32),
        scratch_shapes=[pltpu.VMEM((qt + BLK, GROUP_W), BF16)] * 2 + [pltpu.VMEM((qt + BLK, HEAD_DIM), F32)] * 2,
        compiler_params=_cp(2), name=f"attn_bwd_d{d}")(qkv, qkv, qkv, qkv, qkv, qkv, d_a, d_a, lt, lt, delta, delta)


def _undilate_rope_bwd(dqkvs, pos, invf, tm):
    t = pos.shape[0]

    def body(g0, g1, g2, pos_ref, invf_ref, o_ref, c_s, s_s, nat):
        c = pl.program_id(1)

        @pl.when(c == 0)
        def _():
            _rope_tables(pos_ref, invf_ref, c_s, s_s)

        for g, (d, g_ref) in enumerate(zip(DILATIONS, (g0, g1, g2))):
            n = tm // d
            for r in range(d):
                for hh in range(HEADS_PER_GROUP):
                    oc = r * GROUP_W + hh * HEAD_DIM
                    nat[g * HEADS_PER_GROUP + hh, _strided(r, n, d), :] = g_ref[:, oc:oc + HEAD_DIM]

        @pl.when(c < 2)
        def _():
            cc, ss = c_s[...], s_s[...]
            first_half = lax.broadcasted_iota(jnp.int32, cc.shape, 1) < ROPE_HALF
            for h in range(QK_W // HEAD_DIM):
                xv = nat[h]
                y = xv * cc - _rope_partner(xv, first_half) * ss
                o_ref[:, h * HEAD_DIM:(h + 1) * HEAD_DIM] = y.astype(BF16)

        @pl.when(c == 2)
        def _():
            for h in range(QK_W // HEAD_DIM):
                o_ref[:, h * HEAD_DIM:(h + 1) * HEAD_DIM] = nat[h].astype(BF16)

    return pl.pallas_call(
        body, grid=(t // tm, 3),
        in_specs=[_bs((None, tm // d, d * GROUP_W), lambda i, c: (c, i, 0)) for d in DILATIONS]
        + [_bs((tm, 1), lambda i, c: (i, 0)), _bs((1, HEAD_DIM), lambda i, c: (0, 0))],
        out_specs=_bs((tm, QK_W), lambda i, c: (i, c)), out_shape=SDS((t, 3 * QK_W), BF16),
        scratch_shapes=[pltpu.VMEM((tm, HEAD_DIM), F32)] * 2 + [pltpu.VMEM((QK_W // HEAD_DIM, tm, HEAD_DIM), F32)],
        compiler_params=_cp(2), name="undilate_rope_bwd")(*dqkvs, pos, invf)


def _cmul(ar, ai, br, bi):
    return ar * br - ai * bi, ar * bi + ai * br


def _ssm_disc(a_re, a_im, log_dt, nsq):
    def body(lr_ref, li_ref, ldt_ref, br_ref, bi_ref, zr_ref, zi_ref, pr_ref, pi_ref):
        lr, li = lr_ref[...], li_ref[...]
        dt = jnp.exp(ldt_ref[...])
        mag = jnp.exp(lr * dt)
        bar_re, bar_im = mag * jnp.cos(li * dt), mag * jnp.sin(li * dt)
        nr, ni = bar_re - 1.0, bar_im
        den = lr * lr + li * li
        br_ref[...], bi_ref[...] = bar_re, bar_im
        zr_ref[...] = (nr * lr + ni * li) / den
        zi_ref[...] = (ni * lr - nr * li) / den
        pr, pi = bar_re, bar_im
        for _ in range(nsq):
            pr, pi = _cmul(pr, pi, pr, pi)
        pr_ref[...], pi_ref[...] = pr, pi

    return pl.pallas_call(body, out_shape=[SDS(a_re.shape, F32)] * 6, name="ssm_discretise")(a_re, a_im, log_dt)


def _ssm_scale_b(z_re, z_im, b_re, b_im):
    def body(zr_ref, zi_ref, br_ref, bi_ref, or_ref, oi_ref):
        zr, zi, br, bi = zr_ref[...], zi_ref[...], br_ref[...], bi_ref[...]
        or_ref[...] = zr * br - zi * bi
        oi_ref[...] = zr * bi + zi * br

    return pl.pallas_call(body, out_shape=[SDS(b_re.shape, F32)] * 2, name="ssm_scale_b")(z_re, z_im, b_re, b_im)


def _ssm_scale_b_bwd(z_re, z_im, b_re, b_im, g_re, g_im):
    def body(zr_ref, zi_ref, br_ref, bi_ref, gr_ref, gi_ref, dbr_ref, dbi_ref, dzr_ref, dzi_ref):
        zr, zi, br, bi, gr, gi = zr_ref[...], zi_ref[...], br_ref[...], bi_ref[...], gr_ref[...], gi_ref[...]
        dbr_ref[...] = zr * gr + zi * gi
        dbi_ref[...] = zr * gi - zi * gr
        dzr_ref[...] = jnp.sum(br * gr + bi * gi, axis=-1, keepdims=True)
        dzi_ref[...] = jnp.sum(br * gi - bi * gr, axis=-1, keepdims=True)

    return pl.pallas_call(body, out_shape=[SDS(b_re.shape, F32)] * 2 + [SDS(z_re.shape, F32)] * 2,
                          name="ssm_scale_b_bwd")(z_re, z_im, b_re, b_im, g_re, g_im)


def _ssm_disc_bwd(a_re, a_im, log_dt, gb_re, gb_im, gz_re, gz_im):
    def body(lr_ref, li_ref, ldt_ref, gbr_ref, gbi_ref, gzr_ref, gzi_ref, dar_ref, dai_ref, dldt_ref):
        lr, li = lr_ref[...], li_ref[...]
        dt = jnp.exp(ldt_ref[...])
        mag = jnp.exp(lr * dt)
        bar_re, bar_im = mag * jnp.cos(li * dt), mag * jnp.sin(li * dt)
        nr, ni = bar_re - 1.0, bar_im
        den = lr * lr + li * li
        zr, zi = (nr * lr + ni * li) / den, (ni * lr - nr * li) / den
        gzr, gzi = gzr_ref[...], gzi_ref[...]
        gbr = gbr_ref[...] + (lr * gzr - li * gzi) / den
        gbi = gbi_ref[...] + (lr * gzi + li * gzr) / den
        qr, qi = (zr * lr + zi * li) / den, (zi * lr - zr * li) / den
        dar_ref[...] = dt * (bar_re * gbr + bar_im * gbi) - qr * gzr - qi * gzi
        dai_ref[...] = dt * (bar_re * gbi - bar_im * gbr) - qr * gzi + qi * gzr
        wr, wi = lr * bar_re - li * bar_im, lr * bar_im + li * bar_re
        dldt_ref[...] = dt * jnp.sum(wr * gbr + wi * gbi, axis=-1, keepdims=True)

    return pl.pallas_call(body, out_shape=[SDS(a_re.shape, F32)] * 2 + [SDS(log_dt.shape, F32)],
                          name="ssm_discretise_bwd")(a_re, a_im, log_dt, gb_re, gb_im, gz_re, gz_im)


def _interleave_epilogue(prods, extra_refs, out_refs, scratch_refs):
    uv = prods[0]
    tmp = scratch_refs[0]
    n = uv.shape[0] // N_DEV
    for b in range(SSM_W // BLK):
        cs = slice(b * BLK, (b + 1) * BLK)
        for j in range(N_DEV):
            tmp[b, pl.ds(j, n, stride=N_DEV), :] = uv[j * n:(j + 1) * n, cs]
        out_refs[0][:, cs] = tmp[b]
        out_refs[1][:, cs] = tmp[b].astype(BF16)


def _drive(src_ref, mat_ref, dst, mode):
    for kn in range(2 * SSM_NB):
        n = kn % SSM_NB
        a = src_ref[:, n * BLK:(n + 1) * BLK]
        dst[:, kn * 512:(kn + 1) * 512] = lax.dot_general(a, mat_ref[kn], _DNUMS[mode], preferred_element_type=F32)


def _scan_chunk(src, lam_ref, carry, *, reverse, store=None, h_ref=None, acc=None):
    steps = src.shape[0] // 8
    for c in range(NSTATE // SCAN_LANES):
        re = slice(c * SCAN_LANES, (c + 1) * SCAN_LANES)
        im = slice(NSTATE + c * SCAN_LANES, NSTATE + (c + 1) * SCAN_LANES)
        ar, ai = lam_ref[:, re], lam_ref[:, im]

        def step(s, val):
            i = (steps - 1 - s) if reverse else s
            rows = pl.ds(pl.multiple_of(i * 8, 8), 8)
            if acc is not None:
                hr, hi, dr, di = val
                pr, pi = h_ref[rows, re], h_ref[rows, im]
                dr = dr + hr * pr + hi * pi
                di = di + hi * pr - hr * pi
            else:
                hr, hi = val
            nr = ar * hr - ai * hi + src[rows, re]
            ni = ar * hi + ai * hr + src[rows, im]
            if store is not None:
                store[rows, re] = nr
                store[rows, im] = ni
            return (nr, ni, dr, di) if acc is not None else (nr, ni)

        init = (carry[:, re], carry[:, im])
        if acc is not None:
            init = init + (acc[:, re], acc[:, im])
        out = lax.fori_loop(0, steps, step, init, unroll=4)
        carry[:, re], carry[:, im] = out[0], out[1]
        if acc is not None:
            acc[:, re], acc[:, im] = out[2], out[3]


def _segment_carries(e_ref, pw_ref, out_ref, reverse):
    pr, pi = pw_ref[:, 0:NSTATE], pw_ref[:, NSTATE:]
    hr = jnp.zeros((1, NSTATE), F32)
    hi = jnp.zeros((1, NSTATE), F32)
    order = range(N_DEV - 1, -1, -1) if reverse else range(N_DEV)
    for j in order:
        out_ref[j:j + 1, 0:NSTATE] = hr
        out_ref[j:j + 1, NSTATE:] = hi
        tr, ti = _cmul(pr, pi, hr, hi)
        hr, hi = e_ref[j:j + 1, 0:NSTATE] + tr, e_ref[j:j + 1, NSTATE:] + ti


def _ssm_carries(name, src, mat, mode, lam8, pw, reverse):
    t = src.shape[0]
    nchunk = t // SCAN_ROWS

    def body(src_ref, mat_ref, lam_ref, pw_ref, out_ref, drive, carry):
        c = pl.program_id(0)

        @pl.when(c == 0)
        def _():
            carry[...] = jnp.zeros_like(carry)

        _drive(src_ref, mat_ref, drive, mode)
        _scan_chunk(drive, lam_ref, carry, reverse=reverse)

        @pl.when(c == nchunk - 1)
        def _():
            _segment_carries(carry, pw_ref, out_ref, reverse)

    blk = (lambda c: (nchunk - 1 - c, 0)) if reverse else (lambda c: (c, 0))
    return pl.pallas_call(
        body, grid=(nchunk,),
        in_specs=[_bs((SCAN_ROWS, SSM_W), blk), _bs(mat.shape, lambda c: (0, 0, 0)), _bs((8, 2 * NSTATE), lambda c: (0, 0)),
                  _bs((1, 2 * NSTATE), lambda c: (0, 0))],
        out_specs=_bs((8, 2 * NSTATE), lambda c: (0, 0)), out_shape=SDS((8, 2 * NSTATE), F32),
        scratch_shapes=[pltpu.VMEM((SCAN_ROWS, 2 * NSTATE), F32), pltpu.VMEM((8, 2 * NSTATE), F32)],
        compiler_params=_cp(1), name=name)(src, mat, lam8, pw)


def _ssm_fwd(u_bf, u, d_skip, bd, cd, lam8, start):
    t = u_bf.shape[0]
    nchunk = t // SCAN_ROWS
    per_seg = SCAN_ROWS // N_DEV

    def body(ub_ref, u_ref, d_ref, bd_ref, cd_ref, lam_ref, start_ref, h_ref, ys_ref, yg_ref, drive, carry, tmp):
        @pl.when(pl.program_id(0) == 0)
        def _():
            carry[...] = start_ref[...]

        _drive(ub_ref, bd_ref, drive, "nn")
        _scan_chunk(drive, lam_ref, carry, reverse=False, store=h_ref)
        for n in range(SSM_NB):
            cs = slice(n * BLK, (n + 1) * BLK)
            hr = h_ref[:, n * 512:(n + 1) * 512].astype(BF16)
            hi = h_ref[:, NSTATE + n * 512:NSTATE + (n + 1) * 512].astype(BF16)
            ys = (jnp.dot(hr, cd_ref[n], preferred_element_type=F32) + jnp.dot(hi, cd_ref[SSM_NB + n], preferred_element_type=F32)
                  + d_ref[:, cs] * u_ref[:, cs])
            ys_ref[:, cs] = ys
            tmp[n] = _gelu_parts(ys)[0]
            for j in range(N_DEV):
                yg_ref[j, :, cs] = tmp[n, pl.ds(j, per_seg, stride=N_DEV), :].astype(BF16)

    row = _bs((SCAN_ROWS, SSM_W), lambda c: (c, 0))
    h, ys, yg = pl.pallas_call(
        body, grid=(nchunk,),
        in_specs=[row, row, _bs((1, SSM_W), lambda c: (0, 0)), _bs(bd.shape, lambda c: (0, 0, 0)), _bs(cd.shape, lambda c: (0, 0, 0)),
                  _bs((8, 2 * NSTATE), lambda c: (0, 0)), _bs((8, 2 * NSTATE), lambda c: (0, 0))],
        out_specs=[_bs((SCAN_ROWS, 2 * NSTATE), lambda c: (c, 0)), row, _bs((N_DEV, per_seg, SSM_W), lambda c: (0, c, 0))],
        out_shape=[SDS((t, 2 * NSTATE), F32), SDS((t, SSM_W), F32), SDS((N_DEV, t // N_DEV, SSM_W), BF16)],
        scratch_shapes=[pltpu.VMEM((SCAN_ROWS, 2 * NSTATE), F32), pltpu.VMEM((8, 2 * NSTATE), F32),
                        pltpu.VMEM((SSM_NB, SCAN_ROWS, BLK), F32)],
        compiler_params=_cp(1), name="ssm_scan_fwd")(u_bf, u, d_skip, bd, cd, lam8, start)
    return h, ys, yg.reshape(t, SSM_W)


def _ssm_bwd(dys_bf, dys, d_skip, u_bf, h, bd, cd, lamc8, start):
    t = u_bf.shape[0]
    nchunk = t // SCAN_ROWS
    per_seg = SCAN_ROWS // N_DEV

    def body(dys_ref, dysf_ref, d_ref, u_ref, h_ref, bd_ref, cd_ref, lam_ref, start_ref, du_ref, dlam_ref, dbd_ref, dcd_ref,
             drive, adj, carry, tmp):
        c = pl.program_id(0)

        @pl.when(c == 0)
        def _():
            carry[...] = start_ref[...]
            dlam_ref[...] = jnp.zeros_like(dlam_ref)
            dbd_ref[...] = jnp.zeros_like(dbd_ref)
            dcd_ref[...] = jnp.zeros_like(dcd_ref)

        _drive(dys_ref, cd_ref, drive, "nt")
        _scan_chunk(drive, lam_ref, carry, reverse=True, store=adj, h_ref=h_ref, acc=dlam_ref)
        for n in range(SSM_NB):
            cs = slice(n * BLK, (n + 1) * BLK)
            acc = None
            for k in range(2):
                kn = k * SSM_NB + n
                ss = slice(kn * 512, (kn + 1) * 512)
                lam_b = adj[:, ss].astype(BF16)
                part = lax.dot_general(lam_b, bd_ref[kn], _DNUMS["nt"], preferred_element_type=F32)
                acc = part if acc is None else acc + part
                dbd_ref[kn] += lax.dot_general(u_ref[:, cs], lam_b, _DNUMS["tn"], preferred_element_type=F32)
                dcd_ref[kn] += lax.dot_general(h_ref[:, ss].astype(BF16), dys_ref[:, cs], _DNUMS["tn"],
                                               preferred_element_type=F32)
            tmp[n] = acc + d_ref[:, cs] * dysf_ref[:, cs]
            for j in range(N_DEV):
                du_ref[j, :, cs] = tmp[n, pl.ds(j, per_seg, stride=N_DEV), :].astype(BF16)

    rev = lambda c: (nchunk - 1 - c, 0)
    const2 = lambda c: (0, 0)
    const3 = lambda c: (0, 0, 0)
    row = _bs((SCAN_ROWS, SSM_W), rev)
    du, dlam, dbd, dcd = pl.pallas_call(
        body, grid=(nchunk,),
        in_specs=[row, row, _bs((1, SSM_W), const2), row, _bs((SCAN_ROWS, 2 * NSTATE), rev),
                  _bs(bd.shape, const3), _bs(cd.shape, const3), _bs((8, 2 * NSTATE), const2), _bs((8, 2 * NSTATE), const2)],
        out_specs=[_bs((N_DEV, per_seg, SSM_W), lambda c: (0, nchunk - 1 - c, 0)), _bs((8, 2 * NSTATE), const2),
                   _bs(bd.shape, const3), _bs(cd.shape, const3)],
        out_shape=[SDS((N_DEV, t // N_DEV, SSM_W), BF16), SDS((8, 2 * NSTATE), F32), SDS(bd.shape, F32), SDS(cd.shape, F32)],
        scratch_shapes=[pltpu.VMEM((SCAN_ROWS, 2 * NSTATE), F32), pltpu.VMEM((SCAN_ROWS, 2 * NSTATE), F32),
                        pltpu.VMEM((8, 2 * NSTATE), F32), pltpu.VMEM((SSM_NB, SCAN_ROWS, BLK), F32)],
        compiler_params=_cp(1), name="ssm_scan_bwd")(dys_bf, dys, d_skip, u_bf, h, bd, cd, lamc8, start)
    return du.reshape(t, SSM_W), dlam, dbd, dcd


def _gelu_parts(x):
    c0 = math.sqrt(2.0 / math.pi)
    inner = c0 * (x + 0.044715 * x * x * x)
    th = jnp.tanh(inner)
    val = 0.5 * x * (1.0 + th)
    grad = 0.5 * (1.0 + th) + 0.5 * x * (1.0 - th * th) * c0 * (1.0 + 3.0 * 0.044715 * x * x)
    return val, grad


def _ssm_out_bwd(d_yg, ys, u, tm):
    t = u.shape[0]
    seg = t // N_DEV

    def body(dg_ref, ys_ref, u_ref, dys_ref, dysb_ref, dd_ref, tmp):
        for n in range(SSM_W // BLK):
            for j in range(N_DEV):
                tmp[n, pl.ds(j, tm // N_DEV, stride=N_DEV), :] = dg_ref[j, :, n * BLK:(n + 1) * BLK]
        dyg = jnp.concatenate([tmp[n] for n in range(SSM_W // BLK)], axis=1)
        dys = dyg * _gelu_parts(ys_ref[...])[1]
        dys_ref[...] = dys
        dysb_ref[...] = dys.astype(BF16)
        part = jnp.sum(dys * u_ref[...], axis=0, keepdims=True)

        @pl.when(pl.program_id(0) == 0)
        def _():
            dd_ref[...] = part

        @pl.when(pl.program_id(0) > 0)
        def _():
            dd_ref[...] += part

    row = _bs((tm, SSM_W), lambda i: (i, 0))
    return pl.pallas_call(
        body, grid=(t // tm,), in_specs=[_bs((N_DEV, tm // N_DEV, SSM_W), lambda i: (0, i, 0)), row, row],
        out_specs=[row, row, _bs((1, SSM_W), lambda i: (0, 0))],
        out_shape=[SDS((t, SSM_W), F32), SDS((t, SSM_W), BF16), SDS((1, SSM_W), F32)],
        scratch_shapes=[pltpu.VMEM((SSM_W // BLK, tm, BLK), F32)], compiler_params=_cp(1), name="ssm_out_bwd")(
            d_yg.reshape(N_DEV, seg, SSM_W), ys, u)


def _block_diag(blocks):
    nb, ng, r, c = blocks.shape
    eye = jnp.eye(ng, dtype=blocks.dtype)
    return (blocks[:, :, :, None, :] * eye[None, :, None, :, None]).reshape(nb, ng * r, ng * c)


def _diag_blocks(full, r, c):
    k, nb = full.shape[:2]
    ng = full.shape[2] // r
    x = full.reshape(k, nb, ng, r, ng, c)
    eye = jnp.eye(ng, dtype=full.dtype)
    return jnp.sum(x * eye[None, None, :, None, :, None], axis=4).reshape(k, nb * ng, r, c)


_SMALL = ("a_re", "a_im", "log_dt", "b_re", "b_im", "c_re", "c_im", "d_skip", "g_ffn", "g_final")


def _pack_small(arrs):
    flat = jnp.concatenate([a.reshape(-1) for a in arrs])
    pad = (-flat.shape[0]) % (8 * 128)
    return jnp.pad(flat, (0, pad)).reshape(-1, 128)


def _unpack_small(packed, shapes):
    flat = packed.reshape(-1)
    out, off = [], 0
    for s in shapes:
        n = math.prod(s)
        out.append(flat[off:off + n].reshape(s))
        off += n
    return out


def kernel(x, p, positions, g_mix, w_in, a_re, a_im, log_dt, b_re, b_im, c_re, c_im, d_skip, w_attn_proj, w_glu_a, w_glu_b, w_out, g_ffn, w_ffn_gate, w_ffn_up, w_ffn_down, w_ple_gate, w_ple_proj, g_final, loss_target, m_g_mix, m_w_in, m_a_re, m_a_im, m_log_dt, m_b_re, m_b_im, m_c_re, m_c_im, m_d_skip, m_w_attn_proj, m_w_glu_a, m_w_glu_b, m_w_out, m_g_ffn, m_w_ffn_gate, m_w_ffn_up, m_w_ffn_down, m_w_ple_gate, m_w_ple_proj, m_g_final, v_g_mix, v_w_in, v_a_re, v_a_im, v_log_dt, v_b_re, v_b_im, v_c_re, v_c_im, v_d_skip, v_w_attn_proj, v_w_glu_a, v_w_glu_b, v_w_out, v_g_ffn, v_w_ffn_gate, v_w_ffn_up, v_w_ffn_down, v_w_ple_gate, v_w_ple_proj, v_g_final):
    args = dict(locals())
    t, d = x.shape[1], x.shape[2]
    inw = w_in.shape[2] * N_DEV
    fs = w_ffn_gate.shape[2]
    ff = fs * N_DEV
    ple = w_ple_proj.shape[1]
    seg = t // N_DEV
    assert inw == 3 * QK_W + SSM_W + 2 * d and t % (N_DEV * SCAN_ROWS // 8) == 0 and seg & (seg - 1) == 0
    tm = min(1024, t)
    te = min(512, t)
    tk = min(2048, t)
    ucol = (3 * QK_W) // SSM_W
    gcol = (3 * QK_W + SSM_W) // d
    assert (3 * QK_W + SSM_W) % d == 0

    x2, p2, tgt = x[0], p[0, 0], loss_target[0]
    pos = positions.reshape(t, 1)
    inv = ROPE_THETA ** (-jnp.arange(ROPE_HALF, dtype=F32) * 2.0 / ROPE_DIM)
    invf = jnp.concatenate([inv, inv, jnp.zeros((HEAD_DIM - ROPE_DIM,), F32)]).reshape(1, HEAD_DIM)

    wnames = ("w_in", "w_attn_proj", "w_glu_a", "w_glu_b", "w_out", "w_ffn_gate", "w_ffn_up", "w_ffn_down", "w_ple_gate",
              "w_ple_proj")
    kinds = ("cols", "cols", "cols", "cols", "rows", "slot", "slot", "rows", "rows", "cols")
    shards = [args[n][0].astype(BF16) for n in wnames]
    sizes = [s.shape[0] if k == "rows" else s.shape[-1] for s, k in zip(shards, kinds)]
    ag = _exchange_start("gather_weights_start", shards, kinds, sizes, True)

    row_d = _bs((tm, d), lambda i, j, k: (i, 0))
    row_e = _bs((te, d), lambda i, j, k: (i, 0))
    vec_d = _bs((1, d), lambda i, j, k: (0, 0))
    sq_w = _bs((d, d), lambda i, j, k: (0, 0))
    n1 = _rms_fwd("norm_mix", x2, g_mix + ag[3][0:1, 0:1], tm)

    nsq = seg.bit_length() - 1
    bar_re, bar_im, z_re, z_im, pw_re, pw_im = _ssm_disc(a_re[0], a_im[0], log_dt.reshape(SSM_GROUPS, 1), nsq)
    gp = SSM_GROUPS * SSM_STATE
    b_re2, b_im2 = b_re.reshape(gp, SSM_GROUP), b_im.reshape(gp, SSM_GROUP)
    bb_re, bb_im = _ssm_scale_b(z_re.reshape(gp, 1), z_im.reshape(gp, 1), b_re2, b_im2)

    def chunks(a, r, c):
        return a.reshape(SSM_NB, SSM_GROUPS // SSM_NB, r, c)

    bbt = lambda a: jnp.swapaxes(a.reshape(SSM_GROUPS, SSM_STATE, SSM_GROUP), 1, 2)
    bd = jnp.concatenate([_block_diag(chunks(bbt(bb_re), SSM_GROUP, SSM_STATE)),
                          _block_diag(chunks(bbt(bb_im), SSM_GROUP, SSM_STATE))]).astype(BF16)
    ct = lambda a: jnp.swapaxes(a[0], 1, 2)
    cd = jnp.concatenate([_block_diag(chunks(ct(c_re), SSM_STATE, SSM_GROUP)),
                          _block_diag(chunks(-ct(c_im), SSM_STATE, SSM_GROUP))]).astype(BF16)
    lam = jnp.concatenate([bar_re.reshape(1, gp), bar_im.reshape(1, gp)], axis=1)
    lamc = jnp.concatenate([bar_re.reshape(1, gp), -bar_im.reshape(1, gp)], axis=1)
    pw = jnp.concatenate([pw_re.reshape(1, gp), pw_im.reshape(1, gp)], axis=1)
    pwc = jnp.concatenate([pw_re.reshape(1, gp), -pw_im.reshape(1, gp)], axis=1)
    lam8, lamc8 = jnp.broadcast_to(lam, (8, 2 * gp)), jnp.broadcast_to(lamc, (8, 2 * gp))

    pk = lambda pre: jnp.concatenate([_pack_small([args[pre + n] for n in _SMALL]), _pack_small([args[pre + "g_mix"]])])
    packed = [pk(""), pk("m_"), pk("v_")]

    W_in, = _exchange_wait("gather_w_in_wait", ag, [0], kinds, sizes, True, [n1, bd, cd, lam8, lamc8, pw, pwc] + packed)
    qkv = _mm("qkv_proj", (t // tm, 3, 1), [("nn", n1, row_d, W_in, _bs((d, QK_W), lambda i, j, k: (0, j)))],
              [(SDS((3, t // dil, dil * GROUP_W), BF16), _bs((None, tm // dil, dil * GROUP_W), lambda i, j, k: (j, i, 0)))
               for dil in DILATIONS],
              extras=[(pos, _bs((tm, 1), lambda i, j, k: (i, 0))), (invf, _bs((1, HEAD_DIM), lambda i, j, k: (0, 0)))],
              epilogue=_rope_dilate_epilogue(tm),
              scratch=[pltpu.VMEM((tm, HEAD_DIM), F32)] * 2 + [pltpu.VMEM((QK_W // HEAD_DIM, tm, HEAD_DIM), F32)])
    row_s = _bs((tm, SSM_W), lambda i, j, k: (i, 0))
    u_perm, u_bf = _mm("u_proj", (t // tm, 1, 1),
                       [("nn", n1.reshape(N_DEV, seg, d), _bs((N_DEV, tm // N_DEV, d), lambda i, j, k: (0, i, 0)), W_in,
                         _bs((d, SSM_W), lambda i, j, k: (0, ucol)))],
                       [(SDS((t, SSM_W), F32), row_s), (SDS((t, SSM_W), BF16), row_s)], epilogue=_interleave_epilogue,
                       scratch=[pltpu.VMEM((SSM_W // BLK, tm, BLK), F32)])
    zg, = _mm("z_gates", (t // tm, 2, 1),
              [("nn", n1, row_d, W_in, _bs((d, d), lambda i, j, k: (0, gcol + j)))],
              [(SDS((t, 2 * d), BF16), _bs((tm, d), lambda i, j, k: (i, j)))])

    outs, lses = [], []
    for g, dil in enumerate(DILATIONS):
        o_g, l_g = _attn_fwd(qkv[g], dil, min(1024, t // dil))
        outs.append(o_g)
        lses.append(l_g)
    merged = _attn_merge(outs, lses, tm)
    attn, attn_bf, lts = merged[0], merged[1], merged[2:]

    start_f = _ssm_carries("ssm_carries_fwd", u_bf, bd, "nn", lam8, pw, False)
    dsk = d_skip.reshape(1, SSM_W)
    h_all, ys, yg_bf = _ssm_fwd(u_bf, u_perm, dsk, bd, cd, lam8, start_f)
    W_ap, W_ga, W_gb, W_out, W_fg, W_fu, W_fd, W_pg, W_pp = _exchange_wait(
        "gather_rest_wait", ag, list(range(1, len(wnames))), kinds, sizes, True, yg_bf)
    W_fg = jnp.swapaxes(W_fg, 0, 1).reshape(d, ff)
    W_fu = jnp.swapaxes(W_fu, 0, 1).reshape(d, ff)

    glu_w = _bs((SSM_W, d), lambda i, j, k: (0, 0))
    row_s = _bs((tm, SSM_W), lambda i, j, k: (i, 0))
    gate_a = _bs((te, d), lambda i, j, k: (i, 0))
    gate_s = _bs((te, d), lambda i, j, k: (i, 1))
    td_f32, td_bf = SDS((t, d), F32), SDS((t, d), BF16)
    m_bf, ya, yb, attn_d = _mm(
        "glu_merge", (t // tm, 1, 1),
        [("nn", yg_bf, row_s, W_ga, glu_w), ("nn", yg_bf, row_s, W_gb, glu_w), ("nn", attn_bf, row_s, W_ap, glu_w)],
        [(td_bf, row_d)] * 4, extras=[(zg, row_d), (zg, _bs((tm, d), lambda i, j, k: (i, 1)))], epilogue=_glu_merge_epilogue)

    h1, n2 = _mm("out_proj", (t // tm, 1, 1), [("nn", m_bf, row_d, W_out, sq_w)], [(td_f32, row_d), (td_bf, row_d)],
                 extras=[(x2, row_d), (g_ffn, vec_d)], epilogue=_out_norm_epilogue)

    tn_f = ff // 2
    nf = ff // tn_f
    hid_o = _bs((tm, tn_f), lambda j, i, k: (i, j))
    tf_bf = SDS((t, ff), BF16)
    a_rows = _bs((tm, d), lambda j, i, k: (i, 0))
    w_cols = _bs((d, tn_f), lambda j, i, k: (0, j))
    act, fg, fu = _mm("ffn_gate_up", (nf, t // tm, 1), [("nn", n2, a_rows, W_fg, w_cols), ("nn", n2, a_rows, W_fu, w_cols)],
                      [(tf_bf, hid_o)] * 3, epilogue=_swiglu_epilogue)
    w_once = pl.BlockSpec((d, d), lambda i, j, k: (0, 0), pipeline_mode=pl.Buffered(1))
    loss_part, dg_final, dh2, dh2_bf, dpp_bf, dpg_bf, h2_bf = _mm(
        "ffn_down_head", (t // te, 1, 1),
        [("nn", act, _bs((te, ff), lambda i, j, k: (i, 0)), W_fd,
          pl.BlockSpec((ff, d), lambda i, j, k: (0, 0), pipeline_mode=pl.Buffered(1))),
         ("nn", p2, _bs((te, ple), lambda i, j, k: (i, 0)), W_pp, _bs((ple, d), lambda i, j, k: (0, 0)))],
        [(SDS((1, 1), F32), _bs((1, 1), lambda i, j, k: (0, 0))), (SDS((1, d), F32), vec_d), (td_f32, row_e), (td_bf, row_e),
         (td_bf, row_e), (td_bf, row_e), (td_bf, row_e)],
        extras=[(h1, row_e), (g_final.reshape(1, d), vec_d), (tgt, row_e), (W_pg, w_once)], epilogue=_head_epilogue(t // te),
        scratch=[pltpu.VMEM((1, d), F32)])
    loss = lax.psum(loss_part[0, 0], ("x", "y", "c"))

    nkt = t // tk
    tok_a = lambda w: _bs((tk, w), lambda i, j, k: (k, 0))

    def wgrad(name, a, wa, b, wb):
        return _mm(name, (1, 1, nkt), [("tn", a, tok_a(wa), b, tok_a(wb))],
                   [(SDS((wa, wb), BF16), _bs((wa, wb), lambda i, j, k: (0, 0)))])[0]

    dW_pp = wgrad("dw_ple_proj", p2, ple, dpp_bf, d)
    dW_pg = wgrad("dw_ple_gate", h2_bf, d, dpg_bf, d)
    dfg_bf, dfu_bf = _mm("d_ffn_down", (nf, t // tm, 1),
                         [("nt", dh2_bf, a_rows, W_fd, _bs((tn_f, d), lambda j, i, k: (j, 0)))],
                         [(tf_bf, hid_o), (tf_bf, hid_o)], extras=[(fg, hid_o), (fu, hid_o)], epilogue=_swiglu_bwd_epilogue)
    dW_fd, = _mm("dw_ffn_down", (nf, 1, nkt), [("tn", act, _bs((tk, tn_f), lambda i, j, k: (k, i)), dh2_bf, tok_a(d))],
                 [(SDS((ff, d), BF16), _bs((tn_f, d), lambda i, j, k: (i, 0)))])
    hid_t = _bs((tk, tn_f), lambda i, j, k: (k, j))
    wg_o = [(SDS((d, ff), BF16), _bs((d, tn_f), lambda i, j, k: (0, j)))]
    dW_fg, = _mm("dw_ffn_gate", (1, nf, nkt), [("tn", n2, tok_a(d), dfg_bf, hid_t)], wg_o)
    dW_fu, = _mm("dw_ffn_up", (1, nf, nkt), [("tn", n2, tok_a(d), dfu_bf, hid_t)], wg_o)
    dW_fg = jnp.swapaxes(dW_fg.reshape(d, N_DEV, fs), 0, 1)
    dW_fu = jnp.swapaxes(dW_fu.reshape(d, N_DEV, fs), 0, 1)
    group = lambda names: ([kinds[wnames.index(n)] for n in names], [sizes[wnames.index(n)] for n in names])
    ffn_names = ("w_ffn_gate", "w_ffn_up", "w_ffn_down", "w_ple_gate", "w_ple_proj")
    rs_ffn = _exchange_start("scatter_ffn_start", [dW_fg, dW_fu, dW_fd, dW_pg, dW_pp], *group(ffn_names), False)
    hid_all = _bs((te, ff), lambda i, j, k: (i, 0))
    w_all = pl.BlockSpec((d, ff), lambda i, j, k: (0, 0), pipeline_mode=pl.Buffered(1))
    dh1, dh1_bf, dg_ffn = _mm("d_ffn_gate_up", (t // te, 1, 1),
                              [("nt", dfg_bf, hid_all, W_fg, w_all), ("nt", dfu_bf, hid_all, W_fu, w_all)],
                              [(td_f32, row_e), (td_bf, row_e), (SDS((1, d), F32), vec_d)],
                              extras=[(h1, row_e), (g_ffn, vec_d), (dh2, row_e)], epilogue=_rms_bwd_epilogue, after=rs_ffn[3])

    dW_out = wgrad("dw_out", m_bf, d, dh1_bf, d)
    glu_once = pl.BlockSpec((SSM_W, d), lambda i, j, k: (0, 0), pipeline_mode=pl.Buffered(1))
    row_es = _bs((te, SSM_W), lambda i, j, k: (i, 0))
    ts_f32 = SDS((t, SSM_W), F32)
    dz_g, dad_bf, dya_bf, dyb_bf, d_yg, d_attn = _mm(
        "d_out_proj", (t // te, 1, 1), [("nt", dh1_bf, row_e, W_out, w_once)],
        [(SDS((t, 2 * d), BF16), _bs((te, 2 * d), lambda i, j, k: (i, 0))), (td_bf, row_e), (td_bf, row_e), (td_bf, row_e),
         (ts_f32, row_es), (ts_f32, row_es)],
        extras=[(zg, gate_a), (zg, gate_s), (attn_d, row_e), (ya, row_e), (yb, row_e), (W_ga, glu_once), (W_gb, glu_once),
                (W_ap, glu_once)], epilogue=_merge_bwd_epilogue)

    dW_ga = wgrad("dw_glu_a", yg_bf, SSM_W, dya_bf, d)
    dW_gb = wgrad("dw_glu_b", yg_bf, SSM_W, dyb_bf, d)
    dys, dys_bf, dd_skip = _ssm_out_bwd(d_yg, ys, u_perm, te)
    start_b = _ssm_carries("ssm_carries_bwd", dys_bf, cd, "nt", lamc8, pwc, True)
    dz_u, dlam8, dbd, dcd = _ssm_bwd(dys_bf, dys, dsk, u_bf, h_all, bd, cd, lamc8, start_b)
    dlam = jnp.sum(dlam8, axis=0)
    dbb = _diag_blocks(dbd.reshape(2, SSM_NB, BLK, 512), SSM_GROUP, SSM_STATE)
    dbb_re = jnp.swapaxes(dbb[0], 1, 2).reshape(gp, SSM_GROUP)
    dbb_im = jnp.swapaxes(dbb[1], 1, 2).reshape(gp, SSM_GROUP)
    dcc = _diag_blocks(dcd.reshape(2, SSM_NB, 512, BLK), SSM_STATE, SSM_GROUP)
    dc_re, dc_im = jnp.swapaxes(dcc[0], 1, 2), -jnp.swapaxes(dcc[1], 1, 2)
    db_re, db_im, dz_re, dz_im = _ssm_scale_b_bwd(z_re.reshape(gp, 1), z_im.reshape(gp, 1), b_re2, b_im2, dbb_re, dbb_im)
    gshape = (SSM_GROUPS, SSM_STATE)
    da_re, da_im, dlog_dt = _ssm_disc_bwd(a_re[0], a_im[0], log_dt.reshape(SSM_GROUPS, 1), dlam[:gp].reshape(gshape),
                                          dlam[gp:].reshape(gshape), dz_re.reshape(gshape), dz_im.reshape(gshape))

    dW_ap = wgrad("dw_attn_proj", attn_bf, GROUP_W, dad_bf, d)
    pre = _attn_bwd_pre(d_attn, attn, tm)
    das, deltas = pre[:N_GROUPS], pre[N_GROUPS:]
    dqkvs = [_attn_bwd(qkv[g], das[g], lts[g], deltas[g], dil, min(1024, t // dil)) for g, dil in enumerate(DILATIONS)]
    dz_qkv = _undilate_rope_bwd(dqkvs, pos, invf, tm)

    dW_in, = _mm("dw_in_qkv", (1, 3, nkt), [("tn", n1, tok_a(d), dz_qkv, _bs((tk, QK_W), lambda i, j, k: (k, j)))],
                 [(SDS((d, inw), BF16), _bs((d, QK_W), lambda i, j, k: (0, j)))])
    dW_in, = _mm("dw_in_u", (1, 1, nkt), [("tn", n1, tok_a(d), dz_u, tok_a(SSM_W))],
                 [(SDS((d, inw), BF16), _bs((d, SSM_W), lambda i, j, k: (0, ucol)))], alias_to_out0=dW_in)
    dW_in, = _mm("dw_in_gates", (1, 2, nkt), [("tn", n1, tok_a(d), dz_g, _bs((tk, d), lambda i, j, k: (k, j)))],
                 [(SDS((d, inw), BF16), _bs((d, d), lambda i, j, k: (0, gcol + j)))], alias_to_out0=dW_in)
    small_parts = dict(a_re=da_re, a_im=da_im, log_dt=dlog_dt, b_re=db_re, b_im=db_im, c_re=dc_re, c_im=dc_im,
                       d_skip=dd_skip, g_ffn=dg_ffn, g_final=dg_final)
    small = _pack_small([small_parts[n] for n in _SMALL])
    rest_names = ("w_in", "w_attn_proj", "w_glu_a", "w_glu_b", "w_out")
    rest_kinds, rest_sizes = group(rest_names)
    rs_in = _exchange_start("scatter_rest_start", [dW_in, dW_ap, dW_ga, dW_gb, dW_out, small], rest_kinds + ["all"],
                            rest_sizes + [0], False)
    w_piece = lambda w, cb: pl.BlockSpec((d, w), lambda i, j, k: (0, cb), pipeline_mode=pl.Buffered(1))
    dx, dg_mix = _mm(
        "d_z_proj", (t // te, 1, 1),
        [("nt", dz_qkv, _bs((te, 3 * QK_W), lambda i, j, k: (i, 0)), W_in, w_piece(3 * QK_W, 0)),
         ("nt", dz_u, _bs((te, SSM_W), lambda i, j, k: (i, 0)), W_in, w_piece(SSM_W, ucol)),
         ("nt", dz_g, _bs((te, d), lambda i, j, k: (i, 0)), W_in, w_piece(d, gcol)),
         ("nt", dz_g, _bs((te, d), lambda i, j, k: (i, 1)), W_in, w_piece(d, gcol + 1))],
        [(td_f32, row_e), (SDS((1, d), F32), vec_d)],
        extras=[(x2, row_e), (g_mix, vec_d), (dh1, row_e)], epilogue=_rms_bwd_epilogue, after=rs_in[3])

    received = dict(zip(ffn_names, _exchange_wait("scatter_ffn_wait", rs_ffn, list(range(len(ffn_names))), *group(ffn_names),
                                                  False, dx)))
    *landed, small_all = _exchange_wait("scatter_rest_wait", rs_in, list(range(len(rest_names) + 1)), rest_kinds + ["all"],
                                        rest_sizes + [0], False, dx)
    received.update(zip(rest_names, landed))

    new = {}
    for n in wnames:
        new[n] = [o.reshape(args[n].shape)
                  for o in _adamw("adamw_" + n, received[n], args[n][0], args["m_" + n][0], args["v_" + n][0])]
    g_mix_all = _gather_small(_pack_small([dg_mix]))
    sm = _adamw("adamw_small", jnp.concatenate([small_all, g_mix_all], axis=1), *packed)
    rows_a = small.shape[0]
    shapes = [args[n].shape for n in _SMALL]
    for n, vals in zip(_SMALL, zip(*[_unpack_small(o[:rows_a], shapes) for o in sm])):
        new[n] = list(vals)
    new["g_mix"] = [_unpack_small(o[rows_a:], [g_mix.shape])[0] for o in sm]

    order = ("g_mix", "w_in", "a_re", "a_im", "log_dt", "b_re", "b_im", "c_re", "c_im", "d_skip", "w_attn_proj", "w_glu_a",
             "w_glu_b", "w_out", "g_ffn", "w_ffn_gate", "w_ffn_up", "w_ffn_down", "w_ple_gate", "w_ple_proj", "g_final")
    return (loss, dx.reshape(x.shape), *[new[n][0] for n in order], *[new[n][1] for n in order],
            *[new[n][2] for n in order], *[new[n][3] for n in order])
```

```python
import functools
import math

import jax
import jax.numpy as jnp
from jax import lax
from jax.experimental import pallas as pl
from jax.experimental.pallas import tpu as pltpu

F32 = jnp.float32
BF16 = jnp.bfloat16
SDS = jax.ShapeDtypeStruct

N_DEV = 8
HEAD_DIM = 128
HEADS_PER_GROUP = 4
GROUP_W = HEADS_PER_GROUP * HEAD_DIM
DILATIONS = (1, 4, 16)
N_GROUPS = len(DILATIONS)
QK_W = N_GROUPS * GROUP_W
BLK = 128
ROPE_THETA = 500000.0
ROPE_DIM = HEAD_DIM // 4
ROPE_HALF = ROPE_DIM // 2
SSM_W = 512
SSM_GROUP = 16
SSM_GROUPS = SSM_W // SSM_GROUP
SSM_STATE = 64
NSTATE = SSM_GROUPS * SSM_STATE
SSM_NB = 4
EPS = 1e-6
ADAM_LR, ADAM_B1, ADAM_B2, ADAM_EPS, ADAM_WD, ADAM_STEP = 0.001, 0.9, 0.999, 1e-08, 0.01, 10
NEG = -1e30

VMEM_LIMIT = 52 * 1024 * 1024
SCAN_ROWS = 512
SCAN_LANES = 512


def _cp(n):
    return pltpu.CompilerParams(dimension_semantics=("arbitrary",) * n, vmem_limit_bytes=VMEM_LIMIT)


def _sigmoid(x):
    return 0.5 * jnp.tanh(0.5 * x) + 0.5


_DNUMS = {"nn": (((1,), (0,)), ((), ())), "nt": (((1,), (1,)), ((), ())), "tn": (((0,), (0,)), ((), ()))}


def _bs(shape, fn):
    return pl.BlockSpec(shape, fn)


def _store_all(prods, extra_refs, out_refs, scratch_refs):
    r = prods[0]
    for p in prods[1:]:
        r = r + p
    for e in extra_refs:
        r = r + e[...]
    for o in out_refs:
        o[...] = r.astype(o.dtype)


def _mm(name, grid, pairs, outs, extras=(), epilogue=_store_all, scratch=(), alias_to_out0=None, after=None):
    nk = grid[2]
    npair = len(pairs)
    steps = [p[5] if len(p) > 5 else nk for p in pairs]

    def block(spec):
        return tuple(s for s in spec.block_shape if s is not None)

    def rows2d(shape):
        return (math.prod(shape[:-1]), shape[-1]) if len(shape) == 3 else shape

    acc_shapes = [jax.eval_shape(lambda u, v, dn=_DNUMS[p[0]]: lax.dot_general(u, v, dn, preferred_element_type=F32),
                                 SDS(rows2d(block(p[2])), BF16), SDS(block(p[4]), BF16)).shape for p in pairs]
    if nk == 1:
        acc_shapes = []
    n_in = 2 * npair + len(extras) + (alias_to_out0 is not None) + (after is not None)

    def body(*refs):
        extra_refs = refs[2 * npair:2 * npair + len(extras)]
        out_refs = refs[n_in:n_in + len(outs)]
        rest = refs[n_in + len(outs):]
        acc_refs = rest[:len(acc_shapes)]
        scratch_refs = rest[len(acc_refs):]
        k = pl.program_id(2)

        def product(i):
            a = refs[2 * i][...]
            if a.ndim == 3:
                a = a.reshape(-1, a.shape[-1])
            return lax.dot_general(a.astype(BF16), refs[2 * i + 1][...].astype(BF16), _DNUMS[pairs[i][0]],
                                   preferred_element_type=F32)

        if nk == 1:
            epilogue([product(i) for i in range(npair)], extra_refs, out_refs, scratch_refs)
            return
        for i in range(npair):
            @pl.when(k == 0)
            def _(i=i):
                acc_refs[i][...] = product(i)

            @pl.when((k > 0) & (k < steps[i]))
            def _(i=i):
                acc_refs[i][...] += product(i)

        @pl.when(k == nk - 1)
        def _():
            epilogue([a[...] for a in acc_refs], extra_refs, out_refs, scratch_refs)

    ins, in_specs = [], []
    for p in pairs:
        ins += [p[1], p[3]]
        in_specs += [p[2], p[4]]
    ins += [e[0] for e in extras]
    in_specs += [e[1] for e in extras]
    aliases = {}
    if alias_to_out0 is not None:
        aliases = {len(ins): 0}
        ins.append(alias_to_out0)
        in_specs.append(pl.BlockSpec(memory_space=pl.ANY))
    if after is not None:
        ins.append(after)
        in_specs.append(pl.BlockSpec(memory_space=pl.ANY))
    scratch_shapes = [pltpu.VMEM(s, F32) for s in acc_shapes] + list(scratch)
    return pl.pallas_call(body, grid=grid, in_specs=in_specs, out_specs=[o[1] for o in outs], out_shape=[o[0] for o in outs],
                          scratch_shapes=scratch_shapes, input_output_aliases=aliases, compiler_params=_cp(3), name=name)(*ins)


def _my_index():
    return 4 * lax.axis_index("x") + 2 * lax.axis_index("y") + lax.axis_index("c")


def _peer(d):
    mx, my, mc = lax.axis_index("x"), lax.axis_index("y"), lax.axis_index("c")
    return (mx ^ ((d >> 2) & 1), my ^ ((d >> 1) & 1), mc ^ (d & 1))


def _win(ref, kind, j, n):
    if kind == "all":
        return ref
    if kind == "slot":
        return ref.at[j]
    if kind == "rows":
        return ref.at[pl.ds(pl.multiple_of(j * n, 8), n)]
    return ref.at[:, pl.ds(pl.multiple_of(j * n, 128), n)]


def _win7(ref, kind, n):
    if kind == "slot":
        return ref.at[pl.ds(0, 7)]
    if kind == "rows":
        return ref.at[pl.ds(0, 7 * n)]
    return ref.at[:, pl.ds(0, 7 * n)]


def _full_shape(shard_shape, kind):
    if kind == "slot":
        return (N_DEV,) + tuple(shard_shape)
    if kind == "rows":
        return (N_DEV * shard_shape[0],) + tuple(shard_shape[1:])
    return (shard_shape[0], N_DEV * shard_shape[1])


def _shard_shape(full_shape, kind, n):
    if kind == "all":
        return tuple(full_shape)
    if kind == "slot":
        return tuple(full_shape[1:])
    if kind == "rows":
        return (n,) + tuple(full_shape[1:])
    return (full_shape[0], n)


_HBM = pl.BlockSpec(memory_space=pltpu.HBM)
_SEM = pl.BlockSpec(memory_space=pltpu.SEMAPHORE)
_DATAFLOW = pltpu.SideEffectType.DATAFLOW_SIDE_EFFECTING


def _exchange_start(name, srcs, kinds, sizes, gather):
    n = len(srcs)
    if gather:
        lands = [lax.empty(_full_shape(s.shape, k), s.dtype) for s, k in zip(srcs, kinds)]
    else:
        lands = [lax.empty((N_DEV,) + _shard_shape(s.shape, k, z), s.dtype) for s, k, z in zip(srcs, kinds, sizes)]

    def body(*refs):
        src, land = refs[:n], refs[n:2 * n]
        send_sems, recv_sems, local_sems = refs[2 * n], refs[2 * n + 1], refs[2 * n + 2]
        token = refs[4 * n + 3]
        me = _my_index()
        for a in range(n):
            _local_copy(src[a], land[a], kinds[a], sizes[a], gather, me, local_sems.at[a]).start()
        for a in range(n):
            for d in range(1, N_DEV):
                px, py, pc = _peer(d)
                if gather:
                    s_ref, d_ref = src[a], _win(land[a], kinds[a], me, sizes[a])
                else:
                    s_ref, d_ref = _win(src[a], kinds[a], 4 * px + 2 * py + pc, sizes[a]), land[a].at[me]
                pltpu.make_async_remote_copy(src_ref=s_ref, dst_ref=d_ref, send_sem=send_sems.at[a], recv_sem=recv_sems.at[a],
                                             device_id=(px, py, pc), device_id_type=pl.DeviceIdType.MESH).start()
        token[...] = jnp.zeros_like(token)

    hbm = [pltpu.with_memory_space_constraint(a, pltpu.HBM) for a in list(srcs) + lands]
    out = pl.pallas_call(
        body, name=name, in_specs=[_HBM] * (2 * n),
        out_shape=[pltpu.SemaphoreType.DMA((n,))] * 3 + [pltpu.HBM(a.shape, a.dtype) for a in hbm] + [SDS((8, 128), F32)],
        out_specs=[_SEM] * 3 + [_HBM] * (2 * n) + [pl.BlockSpec(memory_space=pltpu.VMEM)],
        input_output_aliases={i: 3 + i for i in range(2 * n)},
        compiler_params=pltpu.CompilerParams(has_side_effects=_DATAFLOW))(*hbm)
    return out[0:3], out[3:3 + n], out[3 + n:3 + 2 * n], out[-1]


def _local_copy(src, land, kind, size, gather, me, sem):
    if gather:
        return pltpu.make_async_copy(src, _win(land, kind, me, size), sem)
    return pltpu.make_async_copy(_win(src, kind, me, size), land.at[me], sem)


def _exchange_wait(name, started, which, kinds, sizes, gather, after):
    sems, srcs, lands, _ = started
    n = len(which)
    after = list(after) if isinstance(after, (list, tuple)) else [after]

    def body(*refs):
        src, land = refs[:n], refs[n:2 * n]
        send_ref, recv_ref, local_ref = refs[2 * n:2 * n + 3]
        me = _my_index()
        my_id = (lax.axis_index("x"), lax.axis_index("y"), lax.axis_index("c"))
        for i, a in enumerate(which):
            seven = _win7(land[i], kinds[a], sizes[a]) if gather else land[i].at[pl.ds(0, 7)]
            pltpu.make_async_remote_copy(src_ref=seven, dst_ref=seven, send_sem=send_ref.at[a], recv_sem=recv_ref.at[a],
                                         device_id=my_id, device_id_type=pl.DeviceIdType.MESH).wait()
            _local_copy(src[i], land[i], kinds[a], sizes[a], gather, me, local_ref.at[a]).wait()

    hbm = [srcs[a] for a in which] + [lands[a] for a in which]
    out = pl.pallas_call(
        body, name=name, in_specs=[_HBM] * (2 * n) + [_SEM] * 3 + [pl.BlockSpec(memory_space=pl.ANY)] * len(after),
        out_shape=[pltpu.HBM(a.shape, a.dtype) for a in hbm], out_specs=[_HBM] * (2 * n),
        input_output_aliases={i: i for i in range(2 * n)},
        compiler_params=pltpu.CompilerParams(has_side_effects=_DATAFLOW))(*hbm, *sems, *after)
    return out[n:]


def _gather_small(small):
    def body(in_ref, out_ref, send_sem, recv_sem, local_sem):
        me = _my_index()
        my_id = (lax.axis_index("x"), lax.axis_index("y"), lax.axis_index("c"))
        cp = pltpu.make_async_copy(in_ref, out_ref.at[me], local_sem)
        cp.start()
        for d in range(1, N_DEV):
            pltpu.make_async_remote_copy(src_ref=in_ref, dst_ref=out_ref.at[me], send_sem=send_sem, recv_sem=recv_sem,
                                         device_id=_peer(d), device_id_type=pl.DeviceIdType.MESH).start()
        seven = out_ref.at[pl.ds(0, 7)]
        pltpu.make_async_remote_copy(src_ref=seven, dst_ref=seven, send_sem=send_sem, recv_sem=recv_sem, device_id=my_id,
                                     device_id_type=pl.DeviceIdType.MESH).wait()
        cp.wait()

    any_spec = pl.BlockSpec(memory_space=pl.ANY)
    return pl.pallas_call(body, in_specs=[any_spec], out_specs=any_spec, out_shape=SDS((N_DEV,) + small.shape, F32),
                          scratch_shapes=[pltpu.SemaphoreType.DMA] * 3, name="gather_small")(small)


def _adamw(name, recv, w, m, v):
    rows, cols = w.shape
    tr = max(c for c in range(16, 257, 16) if rows % c == 0) if rows % 16 == 0 else rows

    def body(r_ref, w_ref, m_ref, v_ref, g_ref, d_ref, nm_ref, nv_ref):
        g = r_ref[0].astype(F32)
        for s in range(1, N_DEV):
            g = g + r_ref[s].astype(F32)
        nm = ADAM_B1 * m_ref[...] + (1.0 - ADAM_B1) * g
        nv = ADAM_B2 * v_ref[...] + (1.0 - ADAM_B2) * (g * g)
        m_hat = nm / (1.0 - ADAM_B1 ** ADAM_STEP)
        v_hat = nv / (1.0 - ADAM_B2 ** ADAM_STEP)
        g_ref[...] = g
        d_ref[...] = -ADAM_LR * (m_hat / (jnp.sqrt(v_hat) + ADAM_EPS) + ADAM_WD * w_ref[...])
        nm_ref[...] = nm
        nv_ref[...] = nv

    blk = _bs((tr, cols), lambda i: (i, 0))
    return pl.pallas_call(
        body, grid=(rows // tr,), in_specs=[_bs((N_DEV, tr, cols), lambda i: (0, i, 0)), blk, blk, blk],
        out_specs=[blk] * 4, out_shape=[SDS((rows, cols), F32)] * 4, compiler_params=_cp(1), name=name)(recv, w, m, v)


def _rms_fwd(name, x, g, tm):
    t, d = x.shape

    def body(x_ref, g_ref, n_ref):
        xv = x_ref[...]
        r = lax.rsqrt(jnp.mean(xv * xv, axis=-1, keepdims=True) + EPS)
        n_ref[...] = (xv * r * g_ref[...]).astype(BF16)

    return pl.pallas_call(body, grid=(t // tm,), in_specs=[_bs((tm, d), lambda i: (i, 0)), _bs((1, d), lambda i: (0, 0))],
                          out_specs=_bs((tm, d), lambda i: (i, 0)), out_shape=SDS((t, d), BF16), compiler_params=_cp(1),
                          name=name)(x, g)


def _accumulate_rows(ref, part):
    @pl.when(pl.program_id(0) == 0)
    def _():
        ref[...] = part

    @pl.when(pl.program_id(0) > 0)
    def _():
        ref[...] += part


def _rms_bwd_epilogue(prods, extra_refs, out_refs, scratch_refs):
    dyv = prods[0]
    for p in prods[1:]:
        dyv = dyv + p
    if len(extra_refs) > 3:
        dyv = dyv + extra_refs[3][...]
    xv = extra_refs[0][...]
    r = lax.rsqrt(jnp.mean(xv * xv, axis=-1, keepdims=True) + EPS)
    xh = xv * r
    dxh = dyv * extra_refs[1][...]
    dx = extra_refs[2][...] + r * (dxh - xh * jnp.mean(dxh * xh, axis=-1, keepdims=True))
    for o in out_refs[:-1]:
        o[...] = dx.astype(o.dtype)
    _accumulate_rows(out_refs[-1], jnp.sum(dyv * xh, axis=0, keepdims=True))


def _out_norm_epilogue(prods, extra_refs, out_refs, scratch_refs):
    h = prods[0] + extra_refs[0][...]
    r = lax.rsqrt(jnp.mean(h * h, axis=-1, keepdims=True) + EPS)
    out_refs[0][...] = h
    out_refs[1][...] = (h * r * extra_refs[1][...]).astype(BF16)


def _glu_merge_epilogue(prods, extra_refs, out_refs, scratch_refs):
    ya, yb, ad = prods
    ga, gs = extra_refs[0][...].astype(F32), extra_refs[1][...].astype(F32)
    m = _sigmoid(ga) * ad + _sigmoid(gs) * (ya * _sigmoid(yb))
    out_refs[0][...] = m.astype(BF16)
    for o, val in zip(out_refs[1:], (ya, yb, ad)):
        o[...] = val.astype(o.dtype)


def _merge_bwd_epilogue(prods, extra_refs, out_refs, scratch_refs):
    dmv = prods[0]
    d = dmv.shape[1]
    ga, gs = _sigmoid(extra_refs[0][...].astype(F32)), _sigmoid(extra_refs[1][...].astype(F32))
    adv, yav = extra_refs[2][...].astype(F32), extra_refs[3][...].astype(F32)
    sb = _sigmoid(extra_refs[4][...].astype(F32))
    out_refs[0][:, 0:d] = (dmv * adv * ga * (1.0 - ga)).astype(BF16)
    out_refs[0][:, d:2 * d] = (dmv * (yav * sb) * gs * (1.0 - gs)).astype(BF16)
    dad = (dmv * ga).astype(BF16)
    dsd = dmv * gs
    dya = (dsd * sb).astype(BF16)
    dyb = (dsd * yav * sb * (1.0 - sb)).astype(BF16)
    out_refs[1][...], out_refs[2][...], out_refs[3][...] = dad, dya, dyb
    nt = _DNUMS["nt"]
    out_refs[4][...] = (lax.dot_general(dya, extra_refs[5][...], nt, preferred_element_type=F32)
                        + lax.dot_general(dyb, extra_refs[6][...], nt, preferred_element_type=F32))
    out_refs[5][...] = lax.dot_general(dad, extra_refs[7][...], nt, preferred_element_type=F32)


def _swiglu_epilogue(prods, extra_refs, out_refs, scratch_refs):
    gv, uv = prods
    out_refs[0][...] = (gv * _sigmoid(gv) * uv).astype(BF16)
    out_refs[1][...] = gv.astype(out_refs[1].dtype)
    out_refs[2][...] = uv.astype(out_refs[2].dtype)


def _swiglu_bwd_epilogue(prods, extra_refs, out_refs, scratch_refs):
    dav = prods[0]
    gv, uv = extra_refs[0][...].astype(F32), extra_refs[1][...].astype(F32)
    sg = _sigmoid(gv)
    out_refs[0][...] = (dav * uv * sg * (1.0 + gv * (1.0 - sg))).astype(BF16)
    out_refs[1][...] = (dav * gv * sg).astype(BF16)


def _head_epilogue(n_tiles):
    def epilogue(prods, extra_refs, out_refs, scratch_refs):
        h2 = prods[0] + extra_refs[0][...]
        h2_bf = h2.astype(BF16)
        out_refs[6][...] = h2_bf
        pgv = jnp.dot(h2_bf, extra_refs[3][...], preferred_element_type=F32)
        ppv = prods[1]
        d = pgv.shape[1]
        lacc = scratch_refs[0]
        sg = _sigmoid(pgv)
        h3 = h2 + sg * ppv
        r = lax.rsqrt(jnp.mean(h3 * h3, axis=-1, keepdims=True) + EPS)
        xh = h3 * r
        gv = extra_refs[1][...]
        diff = xh * gv - extra_refs[2][...]
        dout = diff * (1.0 / d)
        dxh = dout * gv
        dh3 = r * (dxh - xh * jnp.mean(dxh * xh, axis=-1, keepdims=True))
        dpg = (dh3 * ppv * sg * (1.0 - sg)).astype(BF16)
        dh2 = dh3 + lax.dot_general(dpg, extra_refs[3][...], _DNUMS["nt"], preferred_element_type=F32)
        out_refs[2][...] = dh2
        out_refs[3][...] = dh2.astype(BF16)
        out_refs[4][...] = (dh3 * sg).astype(BF16)
        out_refs[5][...] = dpg
        _accumulate_rows(out_refs[1], jnp.sum(dout * xh, axis=0, keepdims=True))
        _accumulate_rows(lacc, jnp.sum(diff * diff, axis=0, keepdims=True))

        @pl.when(pl.program_id(0) == n_tiles - 1)
        def _():
            out_refs[0][...] = (0.5 / d) * jnp.sum(lacc[...], axis=-1, keepdims=True)

    return epilogue


def _strided(r, n, d):
    return pl.ds(r, n, stride=d) if d > 1 else pl.ds(0, n)


def _rope_tables(pos_ref, invf_ref, c_s, s_s):
    ang = pos_ref[...].astype(F32) * invf_ref[...]
    lane = lax.broadcasted_iota(jnp.int32, ang.shape, 1)
    sn = jnp.sin(ang)
    c_s[...] = jnp.where(lane < ROPE_DIM, jnp.cos(ang), 1.0)
    s_s[...] = jnp.where(lane < ROPE_HALF, -sn, jnp.where(lane < ROPE_DIM, sn, 0.0))


def _rope_partner(xv, first_half):
    return jnp.where(first_half, pltpu.roll(xv, HEAD_DIM - ROPE_HALF, 1), pltpu.roll(xv, ROPE_HALF, 1))


def _rope_dilate_epilogue(tm):
    def epilogue(prods, extra_refs, out_refs, scratch_refs):
        zv = prods[0]
        pos_ref, invf_ref = extra_refs
        c_s, s_s, rot = scratch_refs
        c = pl.program_id(1)

        @pl.when(c == 0)
        def _():
            _rope_tables(pos_ref, invf_ref, c_s, s_s)

        @pl.when(c < 2)
        def _():
            cc, ss = c_s[...], s_s[...]
            first_half = lax.broadcasted_iota(jnp.int32, cc.shape, 1) < ROPE_HALF
            for h in range(QK_W // HEAD_DIM):
                xv = zv[:, h * HEAD_DIM:(h + 1) * HEAD_DIM]
                rot[h] = xv * cc + _rope_partner(xv, first_half) * ss

        @pl.when(c == 2)
        def _():
            for h in range(QK_W // HEAD_DIM):
                rot[h] = zv[:, h * HEAD_DIM:(h + 1) * HEAD_DIM]

        for g, (d, o_ref) in enumerate(zip(DILATIONS, out_refs)):
            n = tm // d
            for r in range(d):
                for hh in range(HEADS_PER_GROUP):
                    oc = r * GROUP_W + hh * HEAD_DIM
                    o_ref[:, oc:oc + HEAD_DIM] = rot[g * HEADS_PER_GROUP + hh, _strided(r, n, d), :].astype(BF16)

    return epilogue


def _band_masks(first_tile):
    qi = lax.broadcasted_iota(jnp.int32, (BLK, 2 * BLK), 0)
    kj = lax.broadcasted_iota(jnp.int32, (BLK, 2 * BLK), 1)
    band = (kj >= qi) & (kj <= qi + BLK)
    return band, band & ((kj >= BLK) | jnp.logical_not(first_tile))


def _attn_fwd(qkv, d, qt):
    ell = qkv.shape[1]
    nsub = qt // BLK
    scale = 1.0 / math.sqrt(HEAD_DIM)

    def body(q_ref, kc_ref, kp_ref, vc_ref, vp_ref, o_ref, lse_ref, kcat, vcat):
        nb = pl.program_id(1)
        kcat[0:BLK, :] = kp_ref[...]
        kcat[BLK:, :] = kc_ref[...]
        vcat[0:BLK, :] = vp_ref[...]
        vcat[BLK:, :] = vc_ref[...]
        lane = lax.broadcasted_iota(jnp.int32, (BLK, HEAD_DIM), 1)
        band, band_first = _band_masks(nb == 0)
        for b in range(nsub):
            valid = band_first if b == 0 else band
            lse_t = jnp.zeros((BLK, HEAD_DIM), F32)
            for hh in range(HEADS_PER_GROUP):
                cs = slice(hh * HEAD_DIM, (hh + 1) * HEAD_DIM)
                qb = q_ref[b * BLK:(b + 1) * BLK, cs]
                kk = kcat[b * BLK:(b + 2) * BLK, cs]
                vv = vcat[b * BLK:(b + 2) * BLK, cs]
                s = lax.dot_general(qb, kk, _DNUMS["nt"], preferred_element_type=F32) * scale
                s = jnp.where(valid, s, NEG)
                mx = jnp.max(s, axis=-1, keepdims=True)
                p = jnp.exp(s - mx)
                den = jnp.sum(p, axis=-1, keepdims=True)
                o = jnp.dot(p.astype(BF16), vv, preferred_element_type=F32) / den
                o_ref[b * BLK:(b + 1) * BLK, cs] = o
                lse_t = jnp.where(lane == hh, mx + jnp.log(den), lse_t)
            lse_ref[b * BLK:(b + 1) * BLK, :] = lse_t

    cur = lambda c: _bs((None, qt, GROUP_W), lambda r, nb: (c, nb, r))
    prev = lambda c: _bs((None, BLK, GROUP_W), lambda r, nb: (c, jnp.maximum(nb * nsub - 1, 0), r))
    return pl.pallas_call(
        body, grid=(d, ell // qt), in_specs=[cur(0), cur(1), prev(1), cur(2), prev(2)],
        out_specs=[_bs((qt, GROUP_W), lambda r, nb: (nb, r)), _bs((None, qt, HEAD_DIM), lambda r, nb: (r, nb, 0))],
        out_shape=[SDS((ell, d * GROUP_W), F32), SDS((d, ell, HEAD_DIM), F32)],
        scratch_shapes=[pltpu.VMEM((qt + BLK, GROUP_W), BF16)] * 2, compiler_params=_cp(2), name=f"attn_fwd_d{d}")(
            qkv, qkv, qkv, qkv, qkv)


def _attn_merge(outs, lses, tm):
    t = outs[0].shape[0]

    def body(o0, o1, o2, l0, l1, l2, attn_ref, attn_bf_ref, t0, t1, t2, so, sl, lt_s):
        for g, (d, o_ref, l_ref) in enumerate(zip(DILATIONS, (o0, o1, o2), (l0, l1, l2))):
            n = tm // d
            for r in range(d):
                rows = _strided(r, n, d)
                for hh in range(HEADS_PER_GROUP):
                    oc = r * GROUP_W + hh * HEAD_DIM
                    so[g * HEADS_PER_GROUP + hh, rows, :] = o_ref[:, oc:oc + HEAD_DIM]
                sl[g, rows, :] = l_ref[r]
        ls = [sl[g] for g in range(N_GROUPS)]
        mx = jnp.maximum(jnp.maximum(ls[0], ls[1]), ls[2])
        es = [jnp.exp(l - mx) for l in ls]
        den = es[0] + es[1] + es[2]
        ws = [e / den for e in es]
        lt_s[...] = mx + jnp.log(den)
        for hh in range(HEADS_PER_GROUP):
            cs = slice(hh * HEAD_DIM, (hh + 1) * HEAD_DIM)
            a = ws[0][:, hh:hh + 1] * so[hh]
            for g in range(1, N_GROUPS):
                a = a + ws[g][:, hh:hh + 1] * so[g * HEADS_PER_GROUP + hh]
            attn_ref[:, cs] = a
            attn_bf_ref[:, cs] = a.astype(BF16)
        for d, t_ref in zip(DILATIONS, (t0, t1, t2)):
            n = tm // d
            for r in range(d):
                t_ref[r] = lt_s[_strided(r, n, d), :]

    dil = lambda d: _bs((tm // d, d * GROUP_W), lambda i: (i, 0))
    lsp = lambda d: _bs((d, tm // d, HEAD_DIM), lambda i: (0, i, 0))
    row = _bs((tm, GROUP_W), lambda i: (i, 0))
    return pl.pallas_call(
        body, grid=(t // tm,),
        in_specs=[dil(d) for d in DILATIONS] + [lsp(d) for d in DILATIONS],
        out_specs=[row, row] + [lsp(d) for d in DILATIONS],
        out_shape=[SDS((t, GROUP_W), F32), SDS((t, GROUP_W), BF16)] + [SDS(l.shape, F32) for l in lses],
        scratch_shapes=[pltpu.VMEM((N_GROUPS * HEADS_PER_GROUP, tm, HEAD_DIM), F32), pltpu.VMEM((N_GROUPS, tm, HEAD_DIM), F32),
                        pltpu.VMEM((tm, HEAD_DIM), F32)],
        compiler_params=_cp(1), name="attn_merge")(*outs, *lses)


def _attn_bwd_pre(d_attn, attn, tm):
    t = attn.shape[0]

    def body(da_ref, a_ref, g0, g1, g2, e0, e1, e2, dl_s, da_s):
        lane = lax.broadcasted_iota(jnp.int32, (tm, HEAD_DIM), 1)
        dl = jnp.zeros((tm, HEAD_DIM), F32)
        for hh in range(HEADS_PER_GROUP):
            cs = slice(hh * HEAD_DIM, (hh + 1) * HEAD_DIM)
            dav = da_ref[:, cs]
            da_s[hh] = dav
            dl = jnp.where(lane == hh, jnp.sum(dav * a_ref[:, cs], axis=-1, keepdims=True), dl)
        dl_s[...] = dl
        for d, g_ref, e_ref in zip(DILATIONS, (g0, g1, g2), (e0, e1, e2)):
            n = tm // d
            for r in range(d):
                rows = _strided(r, n, d)
                for hh in range(HEADS_PER_GROUP):
                    oc = r * GROUP_W + hh * HEAD_DIM
                    g_ref[:, oc:oc + HEAD_DIM] = da_s[hh, rows, :].astype(BF16)
                e_ref[r] = dl_s[rows, :]

    row = _bs((tm, GROUP_W), lambda i: (i, 0))
    return pl.pallas_call(
        body, grid=(t // tm,), in_specs=[row, row],
        out_specs=[_bs((tm // d, d * GROUP_W), lambda i: (i, 0)) for d in DILATIONS]
        + [_bs((d, tm // d, HEAD_DIM), lambda i: (0, i, 0)) for d in DILATIONS],
        out_shape=[SDS((t // d, d * GROUP_W), BF16) for d in DILATIONS]
        + [SDS((d, t // d, HEAD_DIM), F32) for d in DILATIONS],
        scratch_shapes=[pltpu.VMEM((tm, HEAD_DIM), F32), pltpu.VMEM((HEADS_PER_GROUP, tm, HEAD_DIM), F32)],
        compiler_params=_cp(1), name="attn_bwd_pre")(d_attn, attn)


def _attn_bwd(qkv, d_a, lt, delta, d, qt):
    ell = qkv.shape[1]
    nsub = qt // BLK
    ntile = ell // qt
    nblk = ell // BLK
    scale = 1.0 / math.sqrt(HEAD_DIM)

    def body(q_ref, qn_ref, kc_ref, kp_ref, vc_ref, vp_ref, da_ref, dan_ref, lt_ref, ltn_ref, dl_ref, dln_ref, o_ref,
             kcat, vcat, dk_acc, dv_acc):
        nb = pl.program_id(1)
        kcat[0:BLK, :] = kp_ref[...]
        kcat[BLK:, :] = kc_ref[...]
        vcat[0:BLK, :] = vp_ref[...]
        vcat[BLK:, :] = vc_ref[...]
        qi = lax.broadcasted_iota(jnp.int32, (BLK, BLK), 0)
        kj = lax.broadcasted_iota(jnp.int32, (BLK, BLK), 1)
        valid_next = (kj >= qi) & (nb < ntile - 1)
        band, band_first = _band_masks(nb == 0)
        for hh in range(HEADS_PER_GROUP):
            cs = slice(hh * HEAD_DIM, (hh + 1) * HEAD_DIM)
            dk_acc[...] = jnp.zeros_like(dk_acc)
            dv_acc[...] = jnp.zeros_like(dv_acc)
            for b in range(nsub):
                rs = slice(b * BLK, (b + 1) * BLK)
                ks = slice(b * BLK, (b + 2) * BLK)
                valid = band_first if b == 0 else band
                qb, kk, vv, dab = q_ref[rs, cs], kcat[ks, cs], vcat[ks, cs], da_ref[rs, cs]
                s = lax.dot_general(qb, kk, _DNUMS["nt"], preferred_element_type=F32) * scale
                p = jnp.where(valid, jnp.exp(s - lt_ref[rs, hh:hh + 1]), 0.0)
                dp = lax.dot_general(dab, vv, _DNUMS["nt"], preferred_element_type=F32)
                ds = (p * (dp - dl_ref[rs, hh:hh + 1])).astype(BF16)
                o_ref[0, rs, cs] = jnp.dot(ds, kk, preferred_element_type=F32) * scale
                dk_acc[ks, :] += lax.dot_general(ds, qb, _DNUMS["tn"], preferred_element_type=F32) * scale
                dv_acc[ks, :] += lax.dot_general(p.astype(BF16), dab, _DNUMS["tn"], preferred_element_type=F32)
            ks = slice(nsub * BLK, (nsub + 1) * BLK)
            qn, kl, vl, dan = qn_ref[:, cs], kcat[ks, cs], vcat[ks, cs], dan_ref[:, cs]
            s = lax.dot_general(qn, kl, _DNUMS["nt"], preferred_element_type=F32) * scale
            p = jnp.where(valid_next, jnp.exp(s - ltn_ref[:, hh:hh + 1]), 0.0)
            dp = lax.dot_general(dan, vl, _DNUMS["nt"], preferred_element_type=F32)
            ds = (p * (dp - dln_ref[:, hh:hh + 1])).astype(BF16)
            dk_acc[ks, :] += lax.dot_general(ds, qn, _DNUMS["tn"], preferred_element_type=F32) * scale
            dv_acc[ks, :] += lax.dot_general(p.astype(BF16), dan, _DNUMS["tn"], preferred_element_type=F32)
            o_ref[1, :, cs] = dk_acc[BLK:, :]
            o_ref[2, :, cs] = dv_acc[BLK:, :]

    nxt = lambda nb: jnp.minimum((nb + 1) * nsub, nblk - 1)
    prv = lambda nb: jnp.maximum(nb * nsub - 1, 0)
    cur3 = lambda c: _bs((None, qt, GROUP_W), lambda r, nb: (c, nb, r))
    in_specs = [
        cur3(0), _bs((None, BLK, GROUP_W), lambda r, nb: (0, nxt(nb), r)),
        cur3(1), _bs((None, BLK, GROUP_W), lambda r, nb: (1, prv(nb), r)),
        cur3(2), _bs((None, BLK, GROUP_W), lambda r, nb: (2, prv(nb), r)),
        _bs((qt, GROUP_W), lambda r, nb: (nb, r)), _bs((BLK, GROUP_W), lambda r, nb: (nxt(nb), r)),
        _bs((None, qt, HEAD_DIM), lambda r, nb: (r, nb, 0)), _bs((None, BLK, HEAD_DIM), lambda r, nb: (r, nxt(nb), 0)),
        _bs((None, qt, HEAD_DIM), lambda r, nb: (r, nb, 0)), _bs((None, BLK, HEAD_DIM), lambda r, nb: (r, nxt(nb), 0)),
    ]
    return pl.pallas_call(
        body, grid=(d, ntile), in_specs=in_specs, out_specs=_bs((3, qt, GROUP_W), lambda r, nb: (0, nb, r)),
        out_shape=SDS((3, ell, d * GROUP_W), F32),
        scratch_shapes=[pltpu.VMEM((qt + BLK, GROUP_W), BF16)] * 2 + [pltpu.VMEM((qt + BLK, HEAD_DIM), F32)] * 2,
        compiler_params=_cp(2), name=f"attn_bwd_d{d}")(qkv, qkv, qkv, qkv, qkv, qkv, d_a, d_a, lt, lt, delta, delta)


def _undilate_rope_bwd(dqkvs, pos, invf, tm):
    t = pos.shape[0]

    def body(g0, g1, g2, pos_ref, invf_ref, o_ref, c_s, s_s, nat):
        c = pl.program_id(1)

        @pl.when(c == 0)
        def _():
            _rope_tables(pos_ref, invf_ref, c_s, s_s)

        for g, (d, g_ref) in enumerate(zip(DILATIONS, (g0, g1, g2))):
            n = tm // d
            for r in range(d):
                for hh in range(HEADS_PER_GROUP):
                    oc = r * GROUP_W + hh * HEAD_DIM
                    nat[g * HEADS_PER_GROUP + hh, _strided(r, n, d), :] = g_ref[:, oc:oc + HEAD_DIM]

        @pl.when(c < 2)
        def _():
            cc, ss = c_s[...], s_s[...]
            first_half = lax.broadcasted_iota(jnp.int32, cc.shape, 1) < ROPE_HALF
            for h in range(QK_W // HEAD_DIM):
                xv = nat[h]
                y = xv * cc - _rope_partner(xv, first_half) * ss
                o_ref[:, h * HEAD_DIM:(h + 1) * HEAD_DIM] = y.astype(BF16)

        @pl.when(c == 2)
        def _():
            for h in range(QK_W // HEAD_DIM):
                o_ref[:, h * HEAD_DIM:(h + 1) * HEAD_DIM] = nat[h].astype(BF16)

    return pl.pallas_call(
        body, grid=(t // tm, 3),
        in_specs=[_bs((None, tm // d, d * GROUP_W), lambda i, c: (c, i, 0)) for d in DILATIONS]
        + [_bs((tm, 1), lambda i, c: (i, 0)), _bs((1, HEAD_DIM), lambda i, c: (0, 0))],
        out_specs=_bs((tm, QK_W), lambda i, c: (i, c)), out_shape=SDS((t, 3 * QK_W), BF16),
        scratch_shapes=[pltpu.VMEM((tm, HEAD_DIM), F32)] * 2 + [pltpu.VMEM((QK_W // HEAD_DIM, tm, HEAD_DIM), F32)],
        compiler_params=_cp(2), name="undilate_rope_bwd")(*dqkvs, pos, invf)


def _cmul(ar, ai, br, bi):
    return ar * br - ai * bi, ar * bi + ai * br


def _ssm_disc(a_re, a_im, log_dt, nsq):
    def body(lr_ref, li_ref, ldt_ref, br_ref, bi_ref, zr_ref, zi_ref, pr_ref, pi_ref):
        lr, li = lr_ref[...], li_ref[...]
        dt = jnp.exp(ldt_ref[...])
        mag = jnp.exp(lr * dt)
        bar_re, bar_im = mag * jnp.cos(li * dt), mag * jnp.sin(li * dt)
        nr, ni = bar_re - 1.0, bar_im
        den = lr * lr + li * li
        br_ref[...], bi_ref[...] = bar_re, bar_im
        zr_ref[...] = (nr * lr + ni * li) / den
        zi_ref[...] = (ni * lr - nr * li) / den
        pr, pi = bar_re, bar_im
        for _ in range(nsq):
            pr, pi = _cmul(pr, pi, pr, pi)
        pr_ref[...], pi_ref[...] = pr, pi

    return pl.pallas_call(body, out_shape=[SDS(a_re.shape, F32)] * 6, name="ssm_discretise")(a_re, a_im, log_dt)


def _ssm_scale_b(z_re, z_im, b_re, b_im):
    def body(zr_ref, zi_ref, br_ref, bi_ref, or_ref, oi_ref):
        zr, zi, br, bi = zr_ref[...], zi_ref[...], br_ref[...], bi_ref[...]
        or_ref[...] = zr * br - zi * bi
        oi_ref[...] = zr * bi + zi * br

    return pl.pallas_call(body, out_shape=[SDS(b_re.shape, F32)] * 2, name="ssm_scale_b")(z_re, z_im, b_re, b_im)


def _ssm_scale_b_bwd(z_re, z_im, b_re, b_im, g_re, g_im):
    def body(zr_ref, zi_ref, br_ref, bi_ref, gr_ref, gi_ref, dbr_ref, dbi_ref, dzr_ref, dzi_ref):
        zr, zi, br, bi, gr, gi = zr_ref[...], zi_ref[...], br_ref[...], bi_ref[...], gr_ref[...], gi_ref[...]
        dbr_ref[...] = zr * gr + zi * gi
        dbi_ref[...] = zr * gi - zi * gr
        dzr_ref[...] = jnp.sum(br * gr + bi * gi, axis=-1, keepdims=True)
        dzi_ref[...] = jnp.sum(br * gi - bi * gr, axis=-1, keepdims=True)

    return pl.pallas_call(body, out_shape=[SDS(b_re.shape, F32)] * 2 + [SDS(z_re.shape, F32)] * 2,
                          name="ssm_scale_b_bwd")(z_re, z_im, b_re, b_im, g_re, g_im)


def _ssm_disc_bwd(a_re, a_im, log_dt, gb_re, gb_im, gz_re, gz_im):
    def body(lr_ref, li_ref, ldt_ref, gbr_ref, gbi_ref, gzr_ref, gzi_ref, dar_ref, dai_ref, dldt_ref):
        lr, li = lr_ref[...], li_ref[...]
        dt = jnp.exp(ldt_ref[...])
        mag = jnp.exp(lr * dt)
        bar_re, bar_im = mag * jnp.cos(li * dt), mag * jnp.sin(li * dt)
        nr, ni = bar_re - 1.0, bar_im
        den = lr * lr + li * li
        zr, zi = (nr * lr + ni * li) / den, (ni * lr - nr * li) / den
        gzr, gzi = gzr_ref[...], gzi_ref[...]
        gbr = gbr_ref[...] + (lr * gzr - li * gzi) / den
        gbi = gbi_ref[...] + (lr * gzi + li * gzr) / den
        qr, qi = (zr * lr + zi * li) / den, (zi * lr - zr * li) / den
        dar_ref[...] = dt * (bar_re * gbr + bar_im * gbi) - qr * gzr - qi * gzi
        dai_ref[...] = dt * (bar_re * gbi - bar_im * gbr) - qr * gzi + qi * gzr
        wr, wi = lr * bar_re - li * bar_im, lr * bar_im + li * bar_re
        dldt_ref[...] = dt * jnp.sum(wr * gbr + wi * gbi, axis=-1, keepdims=True)

    return pl.pallas_call(body, out_shape=[SDS(a_re.shape, F32)] * 2 + [SDS(log_dt.shape, F32)],
                          name="ssm_discretise_bwd")(a_re, a_im, log_dt, gb_re, gb_im, gz_re, gz_im)


def _interleave_epilogue(prods, extra_refs, out_refs, scratch_refs):
    uv = prods[0]
    tmp = scratch_refs[0]
    n = uv.shape[0] // N_DEV
    for b in range(SSM_W // BLK):
        cs = slice(b * BLK, (b + 1) * BLK)
        for j in range(N_DEV):
            tmp[b, pl.ds(j, n, stride=N_DEV), :] = uv[j * n:(j + 1) * n, cs]
        out_refs[0][:, cs] = tmp[b]
        out_refs[1][:, cs] = tmp[b].astype(BF16)


def _drive(src_ref, mat_ref, dst, mode):
    for kn in range(2 * SSM_NB):
        n = kn % SSM_NB
        a = src_ref[:, n * BLK:(n + 1) * BLK]
        dst[:, kn * 512:(kn + 1) * 512] = lax.dot_general(a, mat_ref[kn], _DNUMS[mode], preferred_element_type=F32)


def _scan_chunk(src, lam_ref, carry, *, reverse, store=None, h_ref=None, acc=None):
    steps = src.shape[0] // 8
    for c in range(NSTATE // SCAN_LANES):
        re = slice(c * SCAN_LANES, (c + 1) * SCAN_LANES)
        im = slice(NSTATE + c * SCAN_LANES, NSTATE + (c + 1) * SCAN_LANES)
        ar, ai = lam_ref[:, re], lam_ref[:, im]

        def step(s, val):
            i = (steps - 1 - s) if reverse else s
            rows = pl.ds(pl.multiple_of(i * 8, 8), 8)
            if acc is not None:
                hr, hi, dr, di = val
                pr, pi = h_ref[rows, re], h_ref[rows, im]
                dr = dr + hr * pr + hi * pi
                di = di + hi * pr - hr * pi
            else:
                hr, hi = val
            nr = ar * hr - ai * hi + src[rows, re]
            ni = ar * hi + ai * hr + src[rows, im]
            if store is not None:
                store[rows, re] = nr
                store[rows, im] = ni
            return (nr, ni, dr, di) if acc is not None else (nr, ni)

        init = (carry[:, re], carry[:, im])
        if acc is not None:
            init = init + (acc[:, re], acc[:, im])
        out = lax.fori_loop(0, steps, step, init, unroll=4)
        carry[:, re], carry[:, im] = out[0], out[1]
        if acc is not None:
            acc[:, re], acc[:, im] = out[2], out[3]


def _segment_carries(e_ref, pw_ref, out_ref, reverse):
    pr, pi = pw_ref[:, 0:NSTATE], pw_ref[:, NSTATE:]
    hr = jnp.zeros((1, NSTATE), F32)
    hi = jnp.zeros((1, NSTATE), F32)
    order = range(N_DEV - 1, -1, -1) if reverse else range(N_DEV)
    for j in order:
        out_ref[j:j + 1, 0:NSTATE] = hr
        out_ref[j:j + 1, NSTATE:] = hi
        tr, ti = _cmul(pr, pi, hr, hi)
        hr, hi = e_ref[j:j + 1, 0:NSTATE] + tr, e_ref[j:j + 1, NSTATE:] + ti


def _ssm_carries(name, src, mat, mode, lam8, pw, reverse):
    t = src.shape[0]
    nchunk = t // SCAN_ROWS

    def body(src_ref, mat_ref, lam_ref, pw_ref, out_ref, drive, carry):
        c = pl.program_id(0)

        @pl.when(c == 0)
        def _():
            carry[...] = jnp.zeros_like(carry)

        _drive(src_ref, mat_ref, drive, mode)
        _scan_chunk(drive, lam_ref, carry, reverse=reverse)

        @pl.when(c == nchunk - 1)
        def _():
            _segment_carries(carry, pw_ref, out_ref, reverse)

    blk = (lambda c: (nchunk - 1 - c, 0)) if reverse else (lambda c: (c, 0))
    return pl.pallas_call(
        body, grid=(nchunk,),
        in_specs=[_bs((SCAN_ROWS, SSM_W), blk), _bs(mat.shape, lambda c: (0, 0, 0)), _bs((8, 2 * NSTATE), lambda c: (0, 0)),
                  _bs((1, 2 * NSTATE), lambda c: (0, 0))],
        out_specs=_bs((8, 2 * NSTATE), lambda c: (0, 0)), out_shape=SDS((8, 2 * NSTATE), F32),
        scratch_shapes=[pltpu.VMEM((SCAN_ROWS, 2 * NSTATE), F32), pltpu.VMEM((8, 2 * NSTATE), F32)],
        compiler_params=_cp(1), name=name)(src, mat, lam8, pw)


def _ssm_fwd(u_bf, u, d_skip, bd, cd, lam8, start):
    t = u_bf.shape[0]
    nchunk = t // SCAN_ROWS
    per_seg = SCAN_ROWS // N_DEV

    def body(ub_ref, u_ref, d_ref, bd_ref, cd_ref, lam_ref, start_ref, h_ref, ys_ref, yg_ref, drive, carry, tmp):
        @pl.when(pl.program_id(0) == 0)
        def _():
            carry[...] = start_ref[...]

        _drive(ub_ref, bd_ref, drive, "nn")
        _scan_chunk(drive, lam_ref, carry, reverse=False, store=h_ref)
        for n in range(SSM_NB):
            cs = slice(n * BLK, (n + 1) * BLK)
            hr = h_ref[:, n * 512:(n + 1) * 512].astype(BF16)
            hi = h_ref[:, NSTATE + n * 512:NSTATE + (n + 1) * 512].astype(BF16)
            ys = (jnp.dot(hr, cd_ref[n], preferred_element_type=F32) + jnp.dot(hi, cd_ref[SSM_NB + n], preferred_element_type=F32)
                  + d_ref[:, cs] * u_ref[:, cs])
            ys_ref[:, cs] = ys
            tmp[n] = _gelu_parts(ys)[0]
            for j in range(N_DEV):
                yg_ref[j, :, cs] = tmp[n, pl.ds(j, per_seg, stride=N_DEV), :].astype(BF16)

    row = _bs((SCAN_ROWS, SSM_W), lambda c: (c, 0))
    h, ys, yg = pl.pallas_call(
        body, grid=(nchunk,),
        in_specs=[row, row, _bs((1, SSM_W), lambda c: (0, 0)), _bs(bd.shape, lambda c: (0, 0, 0)), _bs(cd.shape, lambda c: (0, 0, 0)),
                  _bs((8, 2 * NSTATE), lambda c: (0, 0)), _bs((8, 2 * NSTATE), lambda c: (0, 0))],
        out_specs=[_bs((SCAN_ROWS, 2 * NSTATE), lambda c: (c, 0)), row, _bs((N_DEV, per_seg, SSM_W), lambda c: (0, c, 0))],
        out_shape=[SDS((t, 2 * NSTATE), F32), SDS((t, SSM_W), F32), SDS((N_DEV, t // N_DEV, SSM_W), BF16)],
        scratch_shapes=[pltpu.VMEM((SCAN_ROWS, 2 * NSTATE), F32), pltpu.VMEM((8, 2 * NSTATE), F32),
                        pltpu.VMEM((SSM_NB, SCAN_ROWS, BLK), F32)],
        compiler_params=_cp(1), name="ssm_scan_fwd")(u_bf, u, d_skip, bd, cd, lam8, start)
    return h, ys, yg.reshape(t, SSM_W)


def _ssm_bwd(dys_bf, dys, d_skip, u_bf, h, bd, cd, lamc8, start):
    t = u_bf.shape[0]
    nchunk = t // SCAN_ROWS
    per_seg = SCAN_ROWS // N_DEV

    def body(dys_ref, dysf_ref, d_ref, u_ref, h_ref, bd_ref, cd_ref, lam_ref, start_ref, du_ref, dlam_ref, dbd_ref, dcd_ref,
             drive, adj, carry, tmp):
        c = pl.program_id(0)

        @pl.when(c == 0)
        def _():
            carry[...] = start_ref[...]
            dlam_ref[...] = jnp.zeros_like(dlam_ref)
            dbd_ref[...] = jnp.zeros_like(dbd_ref)
            dcd_ref[...] = jnp.zeros_like(dcd_ref)

        _drive(dys_ref, cd_ref, drive, "nt")
        _scan_chunk(drive, lam_ref, carry, reverse=True, store=adj, h_ref=h_ref, acc=dlam_ref)
        for n in range(SSM_NB):
            cs = slice(n * BLK, (n + 1) * BLK)
            acc = None
            for k in range(2):
                kn = k * SSM_NB + n
                ss = slice(kn * 512, (kn + 1) * 512)
                lam_b = adj[:, ss].astype(BF16)
                part = lax.dot_general(lam_b, bd_ref[kn], _DNUMS["nt"], preferred_element_type=F32)
                acc = part if acc is None else acc + part
                dbd_ref[kn] += lax.dot_general(u_ref[:, cs], lam_b, _DNUMS["tn"], preferred_element_type=F32)
                dcd_ref[kn] += lax.dot_general(h_ref[:, ss].astype(BF16), dys_ref[:, cs], _DNUMS["tn"],
                                               preferred_element_type=F32)
            tmp[n] = acc + d_ref[:, cs] * dysf_ref[:, cs]
            for j in range(N_DEV):
                du_ref[j, :, cs] = tmp[n, pl.ds(j, per_seg, stride=N_DEV), :].astype(BF16)

    rev = lambda c: (nchunk - 1 - c, 0)
    const2 = lambda c: (0, 0)
    const3 = lambda c: (0, 0, 0)
    row = _bs((SCAN_ROWS, SSM_W), rev)
    du, dlam, dbd, dcd = pl.pallas_call(
        body, grid=(nchunk,),
        in_specs=[row, row, _bs((1, SSM_W), const2), row, _bs((SCAN_ROWS, 2 * NSTATE), rev),
                  _bs(bd.shape, const3), _bs(cd.shape, const3), _bs((8, 2 * NSTATE), const2), _bs((8, 2 * NSTATE), const2)],
        out_specs=[_bs((N_DEV, per_seg, SSM_W), lambda c: (0, nchunk - 1 - c, 0)), _bs((8, 2 * NSTATE), const2),
                   _bs(bd.shape, const3), _bs(cd.shape, const3)],
        out_shape=[SDS((N_DEV, t // N_DEV, SSM_W), BF16), SDS((8, 2 * NSTATE), F32), SDS(bd.shape, F32), SDS(cd.shape, F32)],
        scratch_shapes=[pltpu.VMEM((SCAN_ROWS, 2 * NSTATE), F32), pltpu.VMEM((SCAN_ROWS, 2 * NSTATE), F32),
                        pltpu.VMEM((8, 2 * NSTATE), F32), pltpu.VMEM((SSM_NB, SCAN_ROWS, BLK), F32)],
        compiler_params=_cp(1), name="ssm_scan_bwd")(dys_bf, dys, d_skip, u_bf, h, bd, cd, lamc8, start)
    return du.reshape(t, SSM_W), dlam, dbd, dcd


def _gelu_parts(x):
    c0 = math.sqrt(2.0 / math.pi)
    inner = c0 * (x + 0.044715 * x * x * x)
    th = jnp.tanh(inner)
    val = 0.5 * x * (1.0 + th)
    grad = 0.5 * (1.0 + th) + 0.5 * x * (1.0 - th * th) * c0 * (1.0 + 3.0 * 0.044715 * x * x)
    return val, grad


def _ssm_carries_bwd(d_yg, ys, u, cd, lamc8, pwc):
    t = u.shape[0]
    nchunk = t // SCAN_ROWS
    per_seg = SCAN_ROWS // N_DEV

    def body(dg_ref, ys_ref, u_ref, cd_ref, lam_ref, pw_ref, out_ref, dys_ref, dysb_ref, dd_ref, drive, carry, tmp):
        c = pl.program_id(0)

        @pl.when(c == 0)
        def _():
            carry[...] = jnp.zeros_like(carry)

        for n in range(SSM_W // BLK):
            for j in range(N_DEV):
                tmp[n, pl.ds(j, per_seg, stride=N_DEV), :] = dg_ref[j, :, n * BLK:(n + 1) * BLK]
        dyg = jnp.concatenate([tmp[n] for n in range(SSM_W // BLK)], axis=1)
        dys = dyg * _gelu_parts(ys_ref[...])[1]
        dys_ref[...] = dys
        dysb_ref[...] = dys.astype(BF16)
        _accumulate_rows(dd_ref, jnp.sum(dys * u_ref[...], axis=0, keepdims=True))
        _drive(dysb_ref, cd_ref, drive, "nt")
        _scan_chunk(drive, lam_ref, carry, reverse=True)

        @pl.when(c == nchunk - 1)
        def _():
            _segment_carries(carry, pw_ref, out_ref, True)

    rev = lambda c: (nchunk - 1 - c, 0)
    row = _bs((SCAN_ROWS, SSM_W), rev)
    const2 = lambda c: (0, 0)
    return pl.pallas_call(
        body, grid=(nchunk,),
        in_specs=[_bs((N_DEV, per_seg, SSM_W), lambda c: (0, nchunk - 1 - c, 0)), row, row, _bs(cd.shape, lambda c: (0, 0, 0)),
                  _bs((8, 2 * NSTATE), const2), _bs((1, 2 * NSTATE), const2)],
        out_specs=[_bs((8, 2 * NSTATE), const2), row, row, _bs((1, SSM_W), const2)],
        out_shape=[SDS((8, 2 * NSTATE), F32), SDS((t, SSM_W), F32), SDS((t, SSM_W), BF16), SDS((1, SSM_W), F32)],
        scratch_shapes=[pltpu.VMEM((SCAN_ROWS, 2 * NSTATE), F32), pltpu.VMEM((8, 2 * NSTATE), F32),
                        pltpu.VMEM((SSM_W // BLK, SCAN_ROWS, BLK), F32)],
        compiler_params=_cp(1), name="ssm_carries_bwd")(d_yg.reshape(N_DEV, t // N_DEV, SSM_W), ys, u, cd, lamc8, pwc)


def _block_diag(blocks):
    nb, ng, r, c = blocks.shape
    eye = jnp.eye(ng, dtype=blocks.dtype)
    return (blocks[:, :, :, None, :] * eye[None, :, None, :, None]).reshape(nb, ng * r, ng * c)


def _diag_blocks(full, r, c):
    k, nb = full.shape[:2]
    ng = full.shape[2] // r
    x = full.reshape(k, nb, ng, r, ng, c)
    eye = jnp.eye(ng, dtype=full.dtype)
    return jnp.sum(x * eye[None, None, :, None, :, None], axis=4).reshape(k, nb * ng, r, c)


_SMALL = ("a_re", "a_im", "log_dt", "b_re", "b_im", "c_re", "c_im", "d_skip", "g_ffn", "g_final")


def _pack_small(arrs):
    flat = jnp.concatenate([a.reshape(-1) for a in arrs])
    pad = (-flat.shape[0]) % (8 * 128)
    return jnp.pad(flat, (0, pad)).reshape(-1, 128)


def _unpack_small(packed, shapes):
    flat = packed.reshape(-1)
    out, off = [], 0
    for s in shapes:
        n = math.prod(s)
        out.append(flat[off:off + n].reshape(s))
        off += n
    return out


def kernel(x, p, positions, g_mix, w_in, a_re, a_im, log_dt, b_re, b_im, c_re, c_im, d_skip, w_attn_proj, w_glu_a, w_glu_b, w_out, g_ffn, w_ffn_gate, w_ffn_up, w_ffn_down, w_ple_gate, w_ple_proj, g_final, loss_target, m_g_mix, m_w_in, m_a_re, m_a_im, m_log_dt, m_b_re, m_b_im, m_c_re, m_c_im, m_d_skip, m_w_attn_proj, m_w_glu_a, m_w_glu_b, m_w_out, m_g_ffn, m_w_ffn_gate, m_w_ffn_up, m_w_ffn_down, m_w_ple_gate, m_w_ple_proj, m_g_final, v_g_mix, v_w_in, v_a_re, v_a_im, v_log_dt, v_b_re, v_b_im, v_c_re, v_c_im, v_d_skip, v_w_attn_proj, v_w_glu_a, v_w_glu_b, v_w_out, v_g_ffn, v_w_ffn_gate, v_w_ffn_up, v_w_ffn_down, v_w_ple_gate, v_w_ple_proj, v_g_final):
    args = dict(locals())
    t, d = x.shape[1], x.shape[2]
    inw = w_in.shape[2] * N_DEV
    fs = w_ffn_gate.shape[2]
    ff = fs * N_DEV
    ple = w_ple_proj.shape[1]
    seg = t // N_DEV
    assert inw == 3 * QK_W + SSM_W + 2 * d and t % (N_DEV * SCAN_ROWS // 8) == 0 and seg & (seg - 1) == 0
    tm = min(1024, t)
    te = min(512, t)
    tk = min(2048, t)
    ucol = (3 * QK_W) // SSM_W
    gcol = (3 * QK_W + SSM_W) // d
    assert (3 * QK_W + SSM_W) % d == 0

    x2, p2, tgt = x[0], p[0, 0], loss_target[0]
    pos = positions.reshape(t, 1)
    inv = ROPE_THETA ** (-jnp.arange(ROPE_HALF, dtype=F32) * 2.0 / ROPE_DIM)
    invf = jnp.concatenate([inv, inv, jnp.zeros((HEAD_DIM - ROPE_DIM,), F32)]).reshape(1, HEAD_DIM)

    wnames = ("w_in", "w_attn_proj", "w_glu_a", "w_glu_b", "w_out", "w_ffn_gate", "w_ffn_up", "w_ffn_down", "w_ple_gate",
              "w_ple_proj")
    kinds = ("cols", "cols", "cols", "cols", "rows", "slot", "slot", "rows", "rows", "cols")
    shards = [args[n][0].astype(BF16) for n in wnames]
    sizes = [s.shape[0] if k == "rows" else s.shape[-1] for s, k in zip(shards, kinds)]
    ag = _exchange_start("gather_weights_start", shards, kinds, sizes, True)

    row_d = _bs((tm, d), lambda i, j, k: (i, 0))
    row_e = _bs((te, d), lambda i, j, k: (i, 0))
    vec_d = _bs((1, d), lambda i, j, k: (0, 0))
    sq_w = _bs((d, d), lambda i, j, k: (0, 0))
    n1 = _rms_fwd("norm_mix", x2, g_mix + ag[3][0:1, 0:1], tm)

    nsq = seg.bit_length() - 1
    bar_re, bar_im, z_re, z_im, pw_re, pw_im = _ssm_disc(a_re[0], a_im[0], log_dt.reshape(SSM_GROUPS, 1), nsq)
    gp = SSM_GROUPS * SSM_STATE
    b_re2, b_im2 = b_re.reshape(gp, SSM_GROUP), b_im.reshape(gp, SSM_GROUP)
    bb_re, bb_im = _ssm_scale_b(z_re.reshape(gp, 1), z_im.reshape(gp, 1), b_re2, b_im2)

    def chunks(a, r, c):
        return a.reshape(SSM_NB, SSM_GROUPS // SSM_NB, r, c)

    bbt = lambda a: jnp.swapaxes(a.reshape(SSM_GROUPS, SSM_STATE, SSM_GROUP), 1, 2)
    bd = jnp.concatenate([_block_diag(chunks(bbt(bb_re), SSM_GROUP, SSM_STATE)),
                          _block_diag(chunks(bbt(bb_im), SSM_GROUP, SSM_STATE))]).astype(BF16)
    ct = lambda a: jnp.swapaxes(a[0], 1, 2)
    cd = jnp.concatenate([_block_diag(chunks(ct(c_re), SSM_STATE, SSM_GROUP)),
                          _block_diag(chunks(-ct(c_im), SSM_STATE, SSM_GROUP))]).astype(BF16)
    lam = jnp.concatenate([bar_re.reshape(1, gp), bar_im.reshape(1, gp)], axis=1)
    lamc = jnp.concatenate([bar_re.reshape(1, gp), -bar_im.reshape(1, gp)], axis=1)
    pw = jnp.concatenate([pw_re.reshape(1, gp), pw_im.reshape(1, gp)], axis=1)
    pwc = jnp.concatenate([pw_re.reshape(1, gp), -pw_im.reshape(1, gp)], axis=1)
    lam8, lamc8 = jnp.broadcast_to(lam, (8, 2 * gp)), jnp.broadcast_to(lamc, (8, 2 * gp))

    pk = lambda pre: jnp.concatenate([_pack_small([args[pre + n] for n in _SMALL]), _pack_small([args[pre + "g_mix"]])])
    packed = [pk(""), pk("m_"), pk("v_")]

    W_in, = _exchange_wait("gather_w_in_wait", ag, [0], kinds, sizes, True, [n1, bd, cd, lam8, lamc8, pw, pwc] + packed)
    qkv = _mm("qkv_proj", (t // tm, 3, 1), [("nn", n1, row_d, W_in, _bs((d, QK_W), lambda i, j, k: (0, j)))],
              [(SDS((3, t // dil, dil * GROUP_W), BF16), _bs((None, tm // dil, dil * GROUP_W), lambda i, j, k: (j, i, 0)))
               for dil in DILATIONS],
              extras=[(pos, _bs((tm, 1), lambda i, j, k: (i, 0))), (invf, _bs((1, HEAD_DIM), lambda i, j, k: (0, 0)))],
              epilogue=_rope_dilate_epilogue(tm),
              scratch=[pltpu.VMEM((tm, HEAD_DIM), F32)] * 2 + [pltpu.VMEM((QK_W // HEAD_DIM, tm, HEAD_DIM), F32)])
    row_s = _bs((tm, SSM_W), lambda i, j, k: (i, 0))
    u_perm, u_bf = _mm("u_proj", (t // tm, 1, 1),
                       [("nn", n1.reshape(N_DEV, seg, d), _bs((N_DEV, tm // N_DEV, d), lambda i, j, k: (0, i, 0)), W_in,
                         _bs((d, SSM_W), lambda i, j, k: (0, ucol)))],
                       [(SDS((t, SSM_W), F32), row_s), (SDS((t, SSM_W), BF16), row_s)], epilogue=_interleave_epilogue,
                       scratch=[pltpu.VMEM((SSM_W // BLK, tm, BLK), F32)])
    zg, = _mm("z_gates", (t // tm, 2, 1),
              [("nn", n1, row_d, W_in, _bs((d, d), lambda i, j, k: (0, gcol + j)))],
              [(SDS((t, 2 * d), BF16), _bs((tm, d), lambda i, j, k: (i, j)))])

    outs, lses = [], []
    for g, dil in enumerate(DILATIONS):
        o_g, l_g = _attn_fwd(qkv[g], dil, min(1024, t // dil))
        outs.append(o_g)
        lses.append(l_g)
    merged = _attn_merge(outs, lses, tm)
    attn, attn_bf, lts = merged[0], merged[1], merged[2:]

    start_f = _ssm_carries("ssm_carries_fwd", u_bf, bd, "nn", lam8, pw, False)
    dsk = d_skip.reshape(1, SSM_W)
    h_all, ys, yg_bf = _ssm_fwd(u_bf, u_perm, dsk, bd, cd, lam8, start_f)
    W_ap, W_ga, W_gb, W_out, W_fg, W_fu, W_fd, W_pg, W_pp = _exchange_wait(
        "gather_rest_wait", ag, list(range(1, len(wnames))), kinds, sizes, True, yg_bf)
    W_fg = jnp.swapaxes(W_fg, 0, 1).reshape(d, ff)
    W_fu = jnp.swapaxes(W_fu, 0, 1).reshape(d, ff)

    glu_w = _bs((SSM_W, d), lambda i, j, k: (0, 0))
    row_s = _bs((tm, SSM_W), lambda i, j, k: (i, 0))
    gate_a = _bs((te, d), lambda i, j, k: (i, 0))
    gate_s = _bs((te, d), lambda i, j, k: (i, 1))
    td_f32, td_bf = SDS((t, d), F32), SDS((t, d), BF16)
    m_bf, ya, yb, attn_d = _mm(
        "glu_merge", (t // tm, 1, 1),
        [("nn", yg_bf, row_s, W_ga, glu_w), ("nn", yg_bf, row_s, W_gb, glu_w), ("nn", attn_bf, row_s, W_ap, glu_w)],
        [(td_bf, row_d)] * 4, extras=[(zg, row_d), (zg, _bs((tm, d), lambda i, j, k: (i, 1)))], epilogue=_glu_merge_epilogue)

    h1, n2 = _mm("out_proj", (t // tm, 1, 1), [("nn", m_bf, row_d, W_out, sq_w)], [(td_f32, row_d), (td_bf, row_d)],
                 extras=[(x2, row_d), (g_ffn, vec_d)], epilogue=_out_norm_epilogue)

    tn_f = ff // 2
    nf = ff // tn_f
    hid_o = _bs((tm, tn_f), lambda j, i, k: (i, j))
    tf_bf = SDS((t, ff), BF16)
    a_rows = _bs((tm, d), lambda j, i, k: (i, 0))
    w_cols = _bs((d, tn_f), lambda j, i, k: (0, j))
    act, fg, fu = _mm("ffn_gate_up", (nf, t // tm, 1), [("nn", n2, a_rows, W_fg, w_cols), ("nn", n2, a_rows, W_fu, w_cols)],
                      [(tf_bf, hid_o)] * 3, epilogue=_swiglu_epilogue)
    w_once = pl.BlockSpec((d, d), lambda i, j, k: (0, 0), pipeline_mode=pl.Buffered(1))
    loss_part, dg_final, dh2, dh2_bf, dpp_bf, dpg_bf, h2_bf = _mm(
        "ffn_down_head", (t // te, 1, 1),
        [("nn", act, _bs((te, ff), lambda i, j, k: (i, 0)), W_fd,
          pl.BlockSpec((ff, d), lambda i, j, k: (0, 0), pipeline_mode=pl.Buffered(1))),
         ("nn", p2, _bs((te, ple), lambda i, j, k: (i, 0)), W_pp, _bs((ple, d), lambda i, j, k: (0, 0)))],
        [(SDS((1, 1), F32), _bs((1, 1), lambda i, j, k: (0, 0))), (SDS((1, d), F32), vec_d), (td_f32, row_e), (td_bf, row_e),
         (td_bf, row_e), (td_bf, row_e), (td_bf, row_e)],
        extras=[(h1, row_e), (g_final.reshape(1, d), vec_d), (tgt, row_e), (W_pg, w_once)], epilogue=_head_epilogue(t // te),
        scratch=[pltpu.VMEM((1, d), F32)])
    loss = lax.psum(loss_part[0, 0], ("x", "y", "c"))

    nkt = t // tk
    tok_a = lambda w: _bs((tk, w), lambda i, j, k: (k, 0))

    def wgrad(name, a, wa, b, wb):
        return _mm(name, (1, 1, nkt), [("tn", a, tok_a(wa), b, tok_a(wb))],
                   [(SDS((wa, wb), BF16), _bs((wa, wb), lambda i, j, k: (0, 0)))])[0]

    dW_pp = wgrad("dw_ple_proj", p2, ple, dpp_bf, d)
    dW_pg = wgrad("dw_ple_gate", h2_bf, d, dpg_bf, d)
    dfg_bf, dfu_bf = _mm("d_ffn_down", (nf, t // tm, 1),
                         [("nt", dh2_bf, a_rows, W_fd, _bs((tn_f, d), lambda j, i, k: (j, 0)))],
                         [(tf_bf, hid_o), (tf_bf, hid_o)], extras=[(fg, hid_o), (fu, hid_o)], epilogue=_swiglu_bwd_epilogue)
    dW_fd, = _mm("dw_ffn_down", (nf, 1, nkt), [("tn", act, _bs((tk, tn_f), lambda i, j, k: (k, i)), dh2_bf, tok_a(d))],
                 [(SDS((ff, d), BF16), _bs((tn_f, d), lambda i, j, k: (i, 0)))])
    hid_t = _bs((tk, tn_f), lambda i, j, k: (k, j))
    wg_o = [(SDS((d, ff), BF16), _bs((d, tn_f), lambda i, j, k: (0, j)))]
    dW_fg, = _mm("dw_ffn_gate", (1, nf, nkt), [("tn", n2, tok_a(d), dfg_bf, hid_t)], wg_o)
    dW_fu, = _mm("dw_ffn_up", (1, nf, nkt), [("tn", n2, tok_a(d), dfu_bf, hid_t)], wg_o)
    dW_fg = jnp.swapaxes(dW_fg.reshape(d, N_DEV, fs), 0, 1)
    dW_fu = jnp.swapaxes(dW_fu.reshape(d, N_DEV, fs), 0, 1)
    group = lambda names: ([kinds[wnames.index(n)] for n in names], [sizes[wnames.index(n)] for n in names])
    ffn_names = ("w_ffn_gate", "w_ffn_up", "w_ffn_down", "w_ple_gate", "w_ple_proj")
    rs_ffn = _exchange_start("scatter_ffn_start", [dW_fg, dW_fu, dW_fd, dW_pg, dW_pp], *group(ffn_names), False)
    hid_all = _bs((te, ff), lambda i, j, k: (i, 0))
    w_all = pl.BlockSpec((d, ff), lambda i, j, k: (0, 0), pipeline_mode=pl.Buffered(1))
    dh1, dh1_bf, dg_ffn = _mm("d_ffn_gate_up", (t // te, 1, 1),
                              [("nt", dfg_bf, hid_all, W_fg, w_all), ("nt", dfu_bf, hid_all, W_fu, w_all)],
                              [(td_f32, row_e), (td_bf, row_e), (SDS((1, d), F32), vec_d)],
                              extras=[(h1, row_e), (g_ffn, vec_d), (dh2, row_e)], epilogue=_rms_bwd_epilogue, after=rs_ffn[3])

    dW_out = wgrad("dw_out", m_bf, d, dh1_bf, d)
    glu_once = pl.BlockSpec((SSM_W, d), lambda i, j, k: (0, 0), pipeline_mode=pl.Buffered(1))
    row_es = _bs((te, SSM_W), lambda i, j, k: (i, 0))
    ts_f32 = SDS((t, SSM_W), F32)
    dz_g, dad_bf, dya_bf, dyb_bf, d_yg, d_attn = _mm(
        "d_out_proj", (t // te, 1, 1), [("nt", dh1_bf, row_e, W_out, w_once)],
        [(SDS((t, 2 * d), BF16), _bs((te, 2 * d), lambda i, j, k: (i, 0))), (td_bf, row_e), (td_bf, row_e), (td_bf, row_e),
         (ts_f32, row_es), (ts_f32, row_es)],
        extras=[(zg, gate_a), (zg, gate_s), (attn_d, row_e), (ya, row_e), (yb, row_e), (W_ga, glu_once), (W_gb, glu_once),
                (W_ap, glu_once)], epilogue=_merge_bwd_epilogue)

    dW_ga = wgrad("dw_glu_a", yg_bf, SSM_W, dya_bf, d)
    dW_gb = wgrad("dw_glu_b", yg_bf, SSM_W, dyb_bf, d)
    start_b, dys, dys_bf, dd_skip = _ssm_carries_bwd(d_yg, ys, u_perm, cd, lamc8, pwc)
    dz_u, dlam8, dbd, dcd = _ssm_bwd(dys_bf, dys, dsk, u_bf, h_all, bd, cd, lamc8, start_b)
    dlam = jnp.sum(dlam8, axis=0)
    dbb = _diag_blocks(dbd.reshape(2, SSM_NB, BLK, 512), SSM_GROUP, SSM_STATE)
    dbb_re = jnp.swapaxes(dbb[0], 1, 2).reshape(gp, SSM_GROUP)
    dbb_im = jnp.swapaxes(dbb[1], 1, 2).reshape(gp, SSM_GROUP)
    dcc = _diag_blocks(dcd.reshape(2, SSM_NB, 512, BLK), SSM_STATE, SSM_GROUP)
    dc_re, dc_im = jnp.swapaxes(dcc[0], 1, 2), -jnp.swapaxes(dcc[1], 1, 2)
    db_re, db_im, dz_re, dz_im = _ssm_scale_b_bwd(z_re.reshape(gp, 1), z_im.reshape(gp, 1), b_re2, b_im2, dbb_re, dbb_im)
    gshape = (SSM_GROUPS, SSM_STATE)
    da_re, da_im, dlog_dt = _ssm_disc_bwd(a_re[0], a_im[0], log_dt.reshape(SSM_GROUPS, 1), dlam[:gp].reshape(gshape),
                                          dlam[gp:].reshape(gshape), dz_re.reshape(gshape), dz_im.reshape(gshape))

    dW_ap = wgrad("dw_attn_proj", attn_bf, GROUP_W, dad_bf, d)
    pre = _attn_bwd_pre(d_attn, attn, tm)
    das, deltas = pre[:N_GROUPS], pre[N_GROUPS:]
    dqkvs = [_attn_bwd(qkv[g], das[g], lts[g], deltas[g], dil, min(1024, t // dil)) for g, dil in enumerate(DILATIONS)]
    dz_qkv = _undilate_rope_bwd(dqkvs, pos, invf, tm)

    dW_in, = _mm("dw_in_qkv", (1, 3, nkt), [("tn", n1, tok_a(d), dz_qkv, _bs((tk, QK_W), lambda i, j, k: (k, j)))],
                 [(SDS((d, inw), BF16), _bs((d, QK_W), lambda i, j, k: (0, j)))])
    dW_in, = _mm("dw_in_u", (1, 1, nkt), [("tn", n1, tok_a(d), dz_u, tok_a(SSM_W))],
                 [(SDS((d, inw), BF16), _bs((d, SSM_W), lambda i, j, k: (0, ucol)))], alias_to_out0=dW_in)
    dW_in, = _mm("dw_in_gates", (1, 2, nkt), [("tn", n1, tok_a(d), dz_g, _bs((tk, d), lambda i, j, k: (k, j)))],
                 [(SDS((d, inw), BF16), _bs((d, d), lambda i, j, k: (0, gcol + j)))], alias_to_out0=dW_in)
    small_parts = dict(a_re=da_re, a_im=da_im, log_dt=dlog_dt, b_re=db_re, b_im=db_im, c_re=dc_re, c_im=dc_im,
                       d_skip=dd_skip, g_ffn=dg_ffn, g_final=dg_final)
    small = _pack_small([small_parts[n] for n in _SMALL])
    rest_names = ("w_in", "w_attn_proj", "w_glu_a", "w_glu_b", "w_out")
    rest_kinds, rest_sizes = group(rest_names)
    rs_in = _exchange_start("scatter_rest_start", [dW_in, dW_ap, dW_ga, dW_gb, dW_out, small], rest_kinds + ["all"],
                            rest_sizes + [0], False)
    w_piece = lambda w, cb: pl.BlockSpec((d, w), lambda i, j, k: (0, cb), pipeline_mode=pl.Buffered(1))
    dx, dg_mix = _mm(
        "d_z_proj", (t // te, 1, 1),
        [("nt", dz_qkv, _bs((te, 3 * QK_W), lambda i, j, k: (i, 0)), W_in, w_piece(3 * QK_W, 0)),
         ("nt", dz_u, _bs((te, SSM_W), lambda i, j, k: (i, 0)), W_in, w_piece(SSM_W, ucol)),
         ("nt", dz_g, _bs((te, d), lambda i, j, k: (i, 0)), W_in, w_piece(d, gcol)),
         ("nt", dz_g, _bs((te, d), lambda i, j, k: (i, 1)), W_in, w_piece(d, gcol + 1))],
        [(td_f32, row_e), (SDS((1, d), F32), vec_d)],
        extras=[(x2, row_e), (g_mix, vec_d), (dh1, row_e)], epilogue=_rms_bwd_epilogue, after=rs_in[3])

    received = dict(zip(ffn_names, _exchange_wait("scatter_ffn_wait", rs_ffn, list(range(len(ffn_names))), *group(ffn_names),
                                                  False, dx)))
    *landed, small_all = _exchange_wait("scatter_rest_wait", rs_in, list(range(len(rest_names) + 1)), rest_kinds + ["all"],
                                        rest_sizes + [0], False, dx)
    received.update(zip(rest_names, landed))

    new = {}
    for n in wnames:
        new[n] = [o.reshape(args[n].shape)
                  for o in _adamw("adamw_" + n, received[n], args[n][0], args["m_" + n][0], args["v_" + n][0])]
    g_mix_all = _gather_small(_pack_small([dg_mix]))
    sm = _adamw("adamw_small", jnp.concatenate([small_all, g_mix_all], axis=1), *packed)
    rows_a = small.shape[0]
    shapes = [args[n].shape for n in _SMALL]
    for n, vals in zip(_SMALL, zip(*[_unpack_small(o[:rows_a], shapes) for o in sm])):
        new[n] = list(vals)
    new["g_mix"] = [_unpack_small(o[rows_a:], [g_mix.shape])[0] for o in sm]

    order = ("g_mix", "w_in", "a_re", "a_im", "log_dt", "b_re", "b_im", "c_re", "c_im", "d_skip", "w_attn_proj", "w_glu_a",
             "w_glu_b", "w_out", "g_ffn", "w_ffn_gate", "w_ffn_up", "w_ffn_down", "w_ple_gate", "w_ple_proj", "g_final")
    return (loss, dx.reshape(x.shape), *[new[n][0] for n in order], *[new[n][1] for n in order],
            *[new[n][2] for n in order], *[new[n][3] for n in order])
```

```python
import functools
import math

import jax
import jax.numpy as jnp
from jax import lax
from jax.experimental import pallas as pl
from jax.experimental.pallas import tpu as pltpu

F32 = jnp.float32
BF16 = jnp.bfloat16
SDS = jax.ShapeDtypeStruct

N_DEV = 8
HEAD_DIM = 128
HEADS_PER_GROUP = 4
GROUP_W = HEADS_PER_GROUP * HEAD_DIM
DILATIONS = (1, 4, 16)
N_GROUPS = len(DILATIONS)
QK_W = N_GROUPS * GROUP_W
BLK = 128
ROPE_THETA = 500000.0
ROPE_DIM = HEAD_DIM // 4
ROPE_HALF = ROPE_DIM // 2
SSM_W = 512
SSM_GROUP = 16
SSM_GROUPS = SSM_W // SSM_GROUP
SSM_STATE = 64
NSTATE = SSM_GROUPS * SSM_STATE
SSM_NB = 4
EPS = 1e-6
ADAM_LR, ADAM_B1, ADAM_B2, ADAM_EPS, ADAM_WD, ADAM_STEP = 0.001, 0.9, 0.999, 1e-08, 0.01, 10
NEG = -1e30

VMEM_LIMIT = 52 * 1024 * 1024
SCAN_ROWS = 512
SCAN_LANES = 512


def _cp(n):
    return pltpu.CompilerParams(dimension_semantics=("arbitrary",) * n, vmem_limit_bytes=VMEM_LIMIT)


def _sigmoid(x):
    return 0.5 * jnp.tanh(0.5 * x) + 0.5


_DNUMS = {"nn": (((1,), (0,)), ((), ())), "nt": (((1,), (1,)), ((), ())), "tn": (((0,), (0,)), ((), ()))}


def _bs(shape, fn):
    return pl.BlockSpec(shape, fn)


def _store_all(prods, extra_refs, out_refs, scratch_refs):
    r = prods[0]
    for p in prods[1:]:
        r = r + p
    for e in extra_refs:
        r = r + e[...]
    for o in out_refs:
        o[...] = r.astype(o.dtype)


def _mm(name, grid, pairs, outs, extras=(), epilogue=_store_all, scratch=(), alias_to_out0=None, after=None):
    nk = grid[2]
    npair = len(pairs)
    steps = [p[5] if len(p) > 5 else nk for p in pairs]

    def block(spec):
        return tuple(s for s in spec.block_shape if s is not None)

    def rows2d(shape):
        return (math.prod(shape[:-1]), shape[-1]) if len(shape) == 3 else shape

    acc_shapes = [jax.eval_shape(lambda u, v, dn=_DNUMS[p[0]]: lax.dot_general(u, v, dn, preferred_element_type=F32),
                                 SDS(rows2d(block(p[2])), BF16), SDS(block(p[4]), BF16)).shape for p in pairs]
    if nk == 1:
        acc_shapes = []
    n_in = 2 * npair + len(extras) + (alias_to_out0 is not None) + (after is not None)

    def body(*refs):
        extra_refs = refs[2 * npair:2 * npair + len(extras)]
        out_refs = refs[n_in:n_in + len(outs)]
        rest = refs[n_in + len(outs):]
        acc_refs = rest[:len(acc_shapes)]
        scratch_refs = rest[len(acc_refs):]
        k = pl.program_id(2)

        def product(i):
            a = refs[2 * i][...]
            if a.ndim == 3:
                a = a.reshape(-1, a.shape[-1])
            return lax.dot_general(a.astype(BF16), refs[2 * i + 1][...].astype(BF16), _DNUMS[pairs[i][0]],
                                   preferred_element_type=F32)

        if nk == 1:
            epilogue([product(i) for i in range(npair)], extra_refs, out_refs, scratch_refs)
            return
        for i in range(npair):
            @pl.when(k == 0)
            def _(i=i):
                acc_refs[i][...] = product(i)

            @pl.when((k > 0) & (k < steps[i]))
            def _(i=i):
                acc_refs[i][...] += product(i)

        @pl.when(k == nk - 1)
        def _():
            epilogue([a[...] for a in acc_refs], extra_refs, out_refs, scratch_refs)

    ins, in_specs = [], []
    for p in pairs:
        ins += [p[1], p[3]]
        in_specs += [p[2], p[4]]
    ins += [e[0] for e in extras]
    in_specs += [e[1] for e in extras]
    aliases = {}
    if alias_to_out0 is not None:
        aliases = {len(ins): 0}
        ins.append(alias_to_out0)
        in_specs.append(pl.BlockSpec(memory_space=pl.ANY))
    if after is not None:
        ins.append(after)
        in_specs.append(pl.BlockSpec(memory_space=pl.ANY))
    scratch_shapes = [pltpu.VMEM(s, F32) for s in acc_shapes] + list(scratch)
    return pl.pallas_call(body, grid=grid, in_specs=in_specs, out_specs=[o[1] for o in outs], out_shape=[o[0] for o in outs],
                          scratch_shapes=scratch_shapes, input_output_aliases=aliases, compiler_params=_cp(3), name=name)(*ins)


def _my_index():
    return 4 * lax.axis_index("x") + 2 * lax.axis_index("y") + lax.axis_index("c")


def _peer(d):
    mx, my, mc = lax.axis_index("x"), lax.axis_index("y"), lax.axis_index("c")
    return (mx ^ ((d >> 2) & 1), my ^ ((d >> 1) & 1), mc ^ (d & 1))


def _win(ref, kind, j, n):
    if kind == "all":
        return ref
    if kind == "slot":
        return ref.at[j]
    if kind == "rows":
        return ref.at[pl.ds(pl.multiple_of(j * n, 8), n)]
    return ref.at[:, pl.ds(pl.multiple_of(j * n, 128), n)]


def _win7(ref, kind, n):
    if kind == "slot":
        return ref.at[pl.ds(0, 7)]
    if kind == "rows":
        return ref.at[pl.ds(0, 7 * n)]
    return ref.at[:, pl.ds(0, 7 * n)]


def _full_shape(shard_shape, kind):
    if kind == "slot":
        return (N_DEV,) + tuple(shard_shape)
    if kind == "rows":
        return (N_DEV * shard_shape[0],) + tuple(shard_shape[1:])
    return (shard_shape[0], N_DEV * shard_shape[1])


def _shard_shape(full_shape, kind, n):
    if kind == "all":
        return tuple(full_shape)
    if kind == "slot":
        return tuple(full_shape[1:])
    if kind == "rows":
        return (n,) + tuple(full_shape[1:])
    return (full_shape[0], n)


_HBM = pl.BlockSpec(memory_space=pltpu.HBM)
_SEM = pl.BlockSpec(memory_space=pltpu.SEMAPHORE)
_DATAFLOW = pltpu.SideEffectType.DATAFLOW_SIDE_EFFECTING


def _exchange_start(name, srcs, kinds, sizes, gather):
    n = len(srcs)
    if gather:
        lands = [lax.empty(_full_shape(s.shape, k), s.dtype) for s, k in zip(srcs, kinds)]
    else:
        lands = [lax.empty((N_DEV,) + _shard_shape(s.shape, k, z), s.dtype) for s, k, z in zip(srcs, kinds, sizes)]

    def body(*refs):
        src, land = refs[:n], refs[n:2 * n]
        send_sems, recv_sems, local_sems = refs[2 * n], refs[2 * n + 1], refs[2 * n + 2]
        token = refs[4 * n + 3]
        me = _my_index()
        for a in range(n):
            _local_copy(src[a], land[a], kinds[a], sizes[a], gather, me, local_sems.at[a]).start()
        for a in range(n):
            for d in range(1, N_DEV):
                px, py, pc = _peer(d)
                if gather:
                    s_ref, d_ref = src[a], _win(land[a], kinds[a], me, sizes[a])
                else:
                    s_ref, d_ref = _win(src[a], kinds[a], 4 * px + 2 * py + pc, sizes[a]), land[a].at[me]
                pltpu.make_async_remote_copy(src_ref=s_ref, dst_ref=d_ref, send_sem=send_sems.at[a], recv_sem=recv_sems.at[a],
                                             device_id=(px, py, pc), device_id_type=pl.DeviceIdType.MESH).start()
        token[...] = jnp.zeros_like(token)

    hbm = [pltpu.with_memory_space_constraint(a, pltpu.HBM) for a in list(srcs) + lands]
    out = pl.pallas_call(
        body, name=name, in_specs=[_HBM] * (2 * n),
        out_shape=[pltpu.SemaphoreType.DMA((n,))] * 3 + [pltpu.HBM(a.shape, a.dtype) for a in hbm] + [SDS((8, 128), F32)],
        out_specs=[_SEM] * 3 + [_HBM] * (2 * n) + [pl.BlockSpec(memory_space=pltpu.VMEM)],
        input_output_aliases={i: 3 + i for i in range(2 * n)},
        compiler_params=pltpu.CompilerParams(has_side_effects=_DATAFLOW))(*hbm)
    return out[0:3], out[3:3 + n], out[3 + n:3 + 2 * n], out[-1]


def _local_copy(src, land, kind, size, gather, me, sem):
    if gather:
        return pltpu.make_async_copy(src, _win(land, kind, me, size), sem)
    return pltpu.make_async_copy(_win(src, kind, me, size), land.at[me], sem)


def _exchange_wait(name, started, which, kinds, sizes, gather, after):
    sems, srcs, lands, _ = started
    n = len(which)
    after = list(after) if isinstance(after, (list, tuple)) else [after]

    def body(*refs):
        src, land = refs[:n], refs[n:2 * n]
        send_ref, recv_ref, local_ref = refs[2 * n:2 * n + 3]
        me = _my_index()
        my_id = (lax.axis_index("x"), lax.axis_index("y"), lax.axis_index("c"))
        for i, a in enumerate(which):
            seven = _win7(land[i], kinds[a], sizes[a]) if gather else land[i].at[pl.ds(0, 7)]
            pltpu.make_async_remote_copy(src_ref=seven, dst_ref=seven, send_sem=send_ref.at[a], recv_sem=recv_ref.at[a],
                                         device_id=my_id, device_id_type=pl.DeviceIdType.MESH).wait()
            _local_copy(src[i], land[i], kinds[a], sizes[a], gather, me, local_ref.at[a]).wait()

    hbm = [srcs[a] for a in which] + [lands[a] for a in which]
    out = pl.pallas_call(
        body, name=name, in_specs=[_HBM] * (2 * n) + [_SEM] * 3 + [pl.BlockSpec(memory_space=pl.ANY)] * len(after),
        out_shape=[pltpu.HBM(a.shape, a.dtype) for a in hbm], out_specs=[_HBM] * (2 * n),
        input_output_aliases={i: i for i in range(2 * n)},
        compiler_params=pltpu.CompilerParams(has_side_effects=_DATAFLOW))(*hbm, *sems, *after)
    return out[n:]


def _gather_small(small):
    def body(in_ref, out_ref, send_sem, recv_sem, local_sem):
        me = _my_index()
        my_id = (lax.axis_index("x"), lax.axis_index("y"), lax.axis_index("c"))
        cp = pltpu.make_async_copy(in_ref, out_ref.at[me], local_sem)
        cp.start()
        for d in range(1, N_DEV):
            pltpu.make_async_remote_copy(src_ref=in_ref, dst_ref=out_ref.at[me], send_sem=send_sem, recv_sem=recv_sem,
                                         device_id=_peer(d), device_id_type=pl.DeviceIdType.MESH).start()
        seven = out_ref.at[pl.ds(0, 7)]
        pltpu.make_async_remote_copy(src_ref=seven, dst_ref=seven, send_sem=send_sem, recv_sem=recv_sem, device_id=my_id,
                                     device_id_type=pl.DeviceIdType.MESH).wait()
        cp.wait()

    any_spec = pl.BlockSpec(memory_space=pl.ANY)
    return pl.pallas_call(body, in_specs=[any_spec], out_specs=any_spec, out_shape=SDS((N_DEV,) + small.shape, F32),
                          scratch_shapes=[pltpu.SemaphoreType.DMA] * 3, name="gather_small")(small)


def _adamw(name, recv, w, m, v):
    rows, cols = w.shape
    tr = max(c for c in range(16, 257, 16) if rows % c == 0) if rows % 16 == 0 else rows

    def body(r_ref, w_ref, m_ref, v_ref, g_ref, d_ref, nm_ref, nv_ref):
        g = r_ref[0].astype(F32)
        for s in range(1, N_DEV):
            g = g + r_ref[s].astype(F32)
        nm = ADAM_B1 * m_ref[...] + (1.0 - ADAM_B1) * g
        nv = ADAM_B2 * v_ref[...] + (1.0 - ADAM_B2) * (g * g)
        m_hat = nm / (1.0 - ADAM_B1 ** ADAM_STEP)
        v_hat = nv / (1.0 - ADAM_B2 ** ADAM_STEP)
        g_ref[...] = g
        d_ref[...] = -ADAM_LR * (m_hat / (jnp.sqrt(v_hat) + ADAM_EPS) + ADAM_WD * w_ref[...])
        nm_ref[...] = nm
        nv_ref[...] = nv

    blk = _bs((tr, cols), lambda i: (i, 0))
    return pl.pallas_call(
        body, grid=(rows // tr,), in_specs=[_bs((N_DEV, tr, cols), lambda i: (0, i, 0)), blk, blk, blk],
        out_specs=[blk] * 4, out_shape=[SDS((rows, cols), F32)] * 4, compiler_params=_cp(1), name=name)(recv, w, m, v)


def _rms_fwd(name, x, g, tm):
    t, d = x.shape

    def body(x_ref, g_ref, n_ref):
        xv = x_ref[...]
        r = lax.rsqrt(jnp.mean(xv * xv, axis=-1, keepdims=True) + EPS)
        n_ref[...] = (xv * r * g_ref[...]).astype(BF16)

    return pl.pallas_call(body, grid=(t // tm,), in_specs=[_bs((tm, d), lambda i: (i, 0)), _bs((1, d), lambda i: (0, 0))],
                          out_specs=_bs((tm, d), lambda i: (i, 0)), out_shape=SDS((t, d), BF16), compiler_params=_cp(1),
                          name=name)(x, g)


def _accumulate_rows(ref, part):
    @pl.when(pl.program_id(0) == 0)
    def _():
        ref[...] = part

    @pl.when(pl.program_id(0) > 0)
    def _():
        ref[...] += part


def _rms_bwd_epilogue(prods, extra_refs, out_refs, scratch_refs):
    dyv = prods[0]
    for p in prods[1:]:
        dyv = dyv + p
    if len(extra_refs) > 3:
        dyv = dyv + extra_refs[3][...]
    xv = extra_refs[0][...]
    r = lax.rsqrt(jnp.mean(xv * xv, axis=-1, keepdims=True) + EPS)
    xh = xv * r
    dxh = dyv * extra_refs[1][...]
    dx = extra_refs[2][...] + r * (dxh - xh * jnp.mean(dxh * xh, axis=-1, keepdims=True))
    for o in out_refs[:-1]:
        o[...] = dx.astype(o.dtype)
    _accumulate_rows(out_refs[-1], jnp.sum(dyv * xh, axis=0, keepdims=True))


def _out_norm_epilogue(prods, extra_refs, out_refs, scratch_refs):
    h = prods[0] + extra_refs[0][...]
    r = lax.rsqrt(jnp.mean(h * h, axis=-1, keepdims=True) + EPS)
    out_refs[0][...] = h
    out_refs[1][...] = (h * r * extra_refs[1][...]).astype(BF16)


def _glu_merge_epilogue(prods, extra_refs, out_refs, scratch_refs):
    ya, yb, ad = prods
    ga, gs = extra_refs[0][...].astype(F32), extra_refs[1][...].astype(F32)
    m = _sigmoid(ga) * ad + _sigmoid(gs) * (ya * _sigmoid(yb))
    out_refs[0][...] = m.astype(BF16)
    for o, val in zip(out_refs[1:], (ya, yb, ad)):
        o[...] = val.astype(o.dtype)


def _merge_bwd_epilogue(prods, extra_refs, out_refs, scratch_refs):
    dmv = prods[0]
    d = dmv.shape[1]
    ga, gs = _sigmoid(extra_refs[0][...].astype(F32)), _sigmoid(extra_refs[1][...].astype(F32))
    adv, yav = extra_refs[2][...].astype(F32), extra_refs[3][...].astype(F32)
    sb = _sigmoid(extra_refs[4][...].astype(F32))
    out_refs[0][:, 0:d] = (dmv * adv * ga * (1.0 - ga)).astype(BF16)
    out_refs[0][:, d:2 * d] = (dmv * (yav * sb) * gs * (1.0 - gs)).astype(BF16)
    dad = (dmv * ga).astype(BF16)
    dsd = dmv * gs
    dya = (dsd * sb).astype(BF16)
    dyb = (dsd * yav * sb * (1.0 - sb)).astype(BF16)
    out_refs[1][...], out_refs[2][...], out_refs[3][...] = dad, dya, dyb
    nt = _DNUMS["nt"]
    out_refs[4][...] = (lax.dot_general(dya, extra_refs[5][...], nt, preferred_element_type=F32)
                        + lax.dot_general(dyb, extra_refs[6][...], nt, preferred_element_type=F32))
    out_refs[5][...] = lax.dot_general(dad, extra_refs[7][...], nt, preferred_element_type=F32)


def _swiglu_epilogue(prods, extra_refs, out_refs, scratch_refs):
    gv, uv = prods
    out_refs[0][...] = (gv * _sigmoid(gv) * uv).astype(BF16)
    out_refs[1][...] = gv.astype(out_refs[1].dtype)
    out_refs[2][...] = uv.astype(out_refs[2].dtype)


def _swiglu_bwd_epilogue(prods, extra_refs, out_refs, scratch_refs):
    dav = prods[0]
    gv, uv = extra_refs[0][...].astype(F32), extra_refs[1][...].astype(F32)
    sg = _sigmoid(gv)
    out_refs[0][...] = (dav * uv * sg * (1.0 + gv * (1.0 - sg))).astype(BF16)
    out_refs[1][...] = (dav * gv * sg).astype(BF16)


def _head_epilogue(n_tiles):
    def epilogue(prods, extra_refs, out_refs, scratch_refs):
        h2 = prods[0] + extra_refs[0][...]
        h2_bf = h2.astype(BF16)
        out_refs[6][...] = h2_bf
        pgv = jnp.dot(h2_bf, extra_refs[3][...], preferred_element_type=F32)
        ppv = prods[1]
        d = pgv.shape[1]
        lacc = scratch_refs[0]
        sg = _sigmoid(pgv)
        h3 = h2 + sg * ppv
        r = lax.rsqrt(jnp.mean(h3 * h3, axis=-1, keepdims=True) + EPS)
        xh = h3 * r
        gv = extra_refs[1][...]
        diff = xh * gv - extra_refs[2][...]
        dout = diff * (1.0 / d)
        dxh = dout * gv
        dh3 = r * (dxh - xh * jnp.mean(dxh * xh, axis=-1, keepdims=True))
        dpg = (dh3 * ppv * sg * (1.0 - sg)).astype(BF16)
        dh2 = dh3 + lax.dot_general(dpg, extra_refs[3][...], _DNUMS["nt"], preferred_element_type=F32)
        out_refs[2][...] = dh2
        out_refs[3][...] = dh2.astype(BF16)
        out_refs[4][...] = (dh3 * sg).astype(BF16)
        out_refs[5][...] = dpg
        _accumulate_rows(out_refs[1], jnp.sum(dout * xh, axis=0, keepdims=True))
        _accumulate_rows(lacc, jnp.sum(diff * diff, axis=0, keepdims=True))

        @pl.when(pl.program_id(0) == n_tiles - 1)
        def _():
            out_refs[0][...] = (0.5 / d) * jnp.sum(lacc[...], axis=-1, keepdims=True)

    return epilogue


def _strided(r, n, d):
    return pl.ds(r, n, stride=d) if d > 1 else pl.ds(0, n)


def _rope_tables(pos_ref, invf_ref, c_s, s_s):
    ang = pos_ref[...].astype(F32) * invf_ref[...]
    lane = lax.broadcasted_iota(jnp.int32, ang.shape, 1)
    sn = jnp.sin(ang)
    c_s[...] = jnp.where(lane < ROPE_DIM, jnp.cos(ang), 1.0)
    s_s[...] = jnp.where(lane < ROPE_HALF, -sn, jnp.where(lane < ROPE_DIM, sn, 0.0))


def _rope_partner(xv, first_half):
    return jnp.where(first_half, pltpu.roll(xv, HEAD_DIM - ROPE_HALF, 1), pltpu.roll(xv, ROPE_HALF, 1))


def _rope_dilate_epilogue(tm):
    def epilogue(prods, extra_refs, out_refs, scratch_refs):
        zv = prods[0]
        pos_ref, invf_ref = extra_refs
        c_s, s_s = out_refs[N_GROUPS:]
        rot = scratch_refs[0]
        c = pl.program_id(1)

        @pl.when(c == 0)
        def _():
            _rope_tables(pos_ref, invf_ref, c_s, s_s)

        @pl.when(c < 2)
        def _():
            cc, ss = c_s[...], s_s[...]
            first_half = lax.broadcasted_iota(jnp.int32, cc.shape, 1) < ROPE_HALF
            for h in range(QK_W // HEAD_DIM):
                xv = zv[:, h * HEAD_DIM:(h + 1) * HEAD_DIM]
                rot[h] = xv * cc + _rope_partner(xv, first_half) * ss

        @pl.when(c == 2)
        def _():
            for h in range(QK_W // HEAD_DIM):
                rot[h] = zv[:, h * HEAD_DIM:(h + 1) * HEAD_DIM]

        for g, (d, o_ref) in enumerate(zip(DILATIONS, out_refs[:N_GROUPS])):
            n = tm // d
            for r in range(d):
                for hh in range(HEADS_PER_GROUP):
                    oc = r * GROUP_W + hh * HEAD_DIM
                    o_ref[:, oc:oc + HEAD_DIM] = rot[g * HEADS_PER_GROUP + hh, _strided(r, n, d), :].astype(BF16)

    return epilogue


def _band_masks(first_tile):
    qi = lax.broadcasted_iota(jnp.int32, (BLK, 2 * BLK), 0)
    kj = lax.broadcasted_iota(jnp.int32, (BLK, 2 * BLK), 1)
    band = (kj >= qi) & (kj <= qi + BLK)
    return band, band & ((kj >= BLK) | jnp.logical_not(first_tile))


def _attn_fwd(qkv, d, qt):
    ell = qkv.shape[1]
    nsub = qt // BLK
    scale = 1.0 / math.sqrt(HEAD_DIM)

    def body(q_ref, kc_ref, kp_ref, vc_ref, vp_ref, o_ref, lse_ref, kcat, vcat):
        nb = pl.program_id(1)
        kcat[0:BLK, :] = kp_ref[...]
        kcat[BLK:, :] = kc_ref[...]
        vcat[0:BLK, :] = vp_ref[...]
        vcat[BLK:, :] = vc_ref[...]
        lane = lax.broadcasted_iota(jnp.int32, (BLK, HEAD_DIM), 1)
        band, band_first = _band_masks(nb == 0)
        for b in range(nsub):
            valid = band_first if b == 0 else band
            lse_t = jnp.zeros((BLK, HEAD_DIM), F32)
            for hh in range(HEADS_PER_GROUP):
                cs = slice(hh * HEAD_DIM, (hh + 1) * HEAD_DIM)
                qb = q_ref[b * BLK:(b + 1) * BLK, cs]
                kk = kcat[b * BLK:(b + 2) * BLK, cs]
                vv = vcat[b * BLK:(b + 2) * BLK, cs]
                s = lax.dot_general(qb, kk, _DNUMS["nt"], preferred_element_type=F32) * scale
                s = jnp.where(valid, s, NEG)
                mx = jnp.max(s, axis=-1, keepdims=True)
                p = jnp.exp(s - mx)
                den = jnp.sum(p, axis=-1, keepdims=True)
                o = jnp.dot(p.astype(BF16), vv, preferred_element_type=F32) / den
                o_ref[b * BLK:(b + 1) * BLK, cs] = o
                lse_t = jnp.where(lane == hh, mx + jnp.log(den), lse_t)
            lse_ref[b * BLK:(b + 1) * BLK, :] = lse_t

    cur = lambda c: _bs((None, qt, GROUP_W), lambda r, nb: (c, nb, r))
    prev = lambda c: _bs((None, BLK, GROUP_W), lambda r, nb: (c, jnp.maximum(nb * nsub - 1, 0), r))
    return pl.pallas_call(
        body, grid=(d, ell // qt), in_specs=[cur(0), cur(1), prev(1), cur(2), prev(2)],
        out_specs=[_bs((qt, GROUP_W), lambda r, nb: (nb, r)), _bs((None, qt, HEAD_DIM), lambda r, nb: (r, nb, 0))],
        out_shape=[SDS((ell, d * GROUP_W), F32), SDS((d, ell, HEAD_DIM), F32)],
        scratch_shapes=[pltpu.VMEM((qt + BLK, GROUP_W), BF16)] * 2, compiler_params=_cp(2), name=f"attn_fwd_d{d}")(
            qkv, qkv, qkv, qkv, qkv)


def _attn_merge(outs, lses, tm):
    t = outs[0].shape[0]

    def body(o0, o1, o2, l0, l1, l2, attn_ref, attn_bf_ref, t0, t1, t2, so, sl, lt_s):
        for g, (d, o_ref, l_ref) in enumerate(zip(DILATIONS, (o0, o1, o2), (l0, l1, l2))):
            n = tm // d
            for r in range(d):
                rows = _strided(r, n, d)
                for hh in range(HEADS_PER_GROUP):
                    oc = r * GROUP_W + hh * HEAD_DIM
                    so[g * HEADS_PER_GROUP + hh, rows, :] = o_ref[:, oc:oc + HEAD_DIM]
                sl[g, rows, :] = l_ref[r]
        ls = [sl[g] for g in range(N_GROUPS)]
        mx = jnp.maximum(jnp.maximum(ls[0], ls[1]), ls[2])
        es = [jnp.exp(l - mx) for l in ls]
        den = es[0] + es[1] + es[2]
        ws = [e / den for e in es]
        lt_s[...] = mx + jnp.log(den)
        for hh in range(HEADS_PER_GROUP):
            cs = slice(hh * HEAD_DIM, (hh + 1) * HEAD_DIM)
            a = ws[0][:, hh:hh + 1] * so[hh]
            for g in range(1, N_GROUPS):
                a = a + ws[g][:, hh:hh + 1] * so[g * HEADS_PER_GROUP + hh]
            attn_ref[:, cs] = a
            attn_bf_ref[:, cs] = a.astype(BF16)
        for d, t_ref in zip(DILATIONS, (t0, t1, t2)):
            n = tm // d
            for r in range(d):
                t_ref[r] = lt_s[_strided(r, n, d), :]

    dil = lambda d: _bs((tm // d, d * GROUP_W), lambda i: (i, 0))
    lsp = lambda d: _bs((d, tm // d, HEAD_DIM), lambda i: (0, i, 0))
    row = _bs((tm, GROUP_W), lambda i: (i, 0))
    return pl.pallas_call(
        body, grid=(t // tm,),
        in_specs=[dil(d) for d in DILATIONS] + [lsp(d) for d in DILATIONS],
        out_specs=[row, row] + [lsp(d) for d in DILATIONS],
        out_shape=[SDS((t, GROUP_W), F32), SDS((t, GROUP_W), BF16)] + [SDS(l.shape, F32) for l in lses],
        scratch_shapes=[pltpu.VMEM((N_GROUPS * HEADS_PER_GROUP, tm, HEAD_DIM), F32), pltpu.VMEM((N_GROUPS, tm, HEAD_DIM), F32),
                        pltpu.VMEM((tm, HEAD_DIM), F32)],
        compiler_params=_cp(1), name="attn_merge")(*outs, *lses)


def _attn_bwd_pre(d_attn, attn, tm):
    t = attn.shape[0]

    def body(da_ref, a_ref, g0, g1, g2, e0, e1, e2, dl_s, da_s):
        lane = lax.broadcasted_iota(jnp.int32, (tm, HEAD_DIM), 1)
        dl = jnp.zeros((tm, HEAD_DIM), F32)
        for hh in range(HEADS_PER_GROUP):
            cs = slice(hh * HEAD_DIM, (hh + 1) * HEAD_DIM)
            dav = da_ref[:, cs]
            da_s[hh] = dav
            dl = jnp.where(lane == hh, jnp.sum(dav * a_ref[:, cs], axis=-1, keepdims=True), dl)
        dl_s[...] = dl
        for d, g_ref, e_ref in zip(DILATIONS, (g0, g1, g2), (e0, e1, e2)):
            n = tm // d
            for r in range(d):
                rows = _strided(r, n, d)
                for hh in range(HEADS_PER_GROUP):
                    oc = r * GROUP_W + hh * HEAD_DIM
                    g_ref[:, oc:oc + HEAD_DIM] = da_s[hh, rows, :].astype(BF16)
                e_ref[r] = dl_s[rows, :]

    row = _bs((tm, GROUP_W), lambda i: (i, 0))
    return pl.pallas_call(
        body, grid=(t // tm,), in_specs=[row, row],
        out_specs=[_bs((tm // d, d * GROUP_W), lambda i: (i, 0)) for d in DILATIONS]
        + [_bs((d, tm // d, HEAD_DIM), lambda i: (0, i, 0)) for d in DILATIONS],
        out_shape=[SDS((t // d, d * GROUP_W), BF16) for d in DILATIONS]
        + [SDS((d, t // d, HEAD_DIM), F32) for d in DILATIONS],
        scratch_shapes=[pltpu.VMEM((tm, HEAD_DIM), F32), pltpu.VMEM((HEADS_PER_GROUP, tm, HEAD_DIM), F32)],
        compiler_params=_cp(1), name="attn_bwd_pre")(d_attn, attn)


def _attn_bwd(qkv, d_a, lt, delta, d, qt):
    ell = qkv.shape[1]
    nsub = qt // BLK
    ntile = ell // qt
    nblk = ell // BLK
    scale = 1.0 / math.sqrt(HEAD_DIM)

    def body(q_ref, qn_ref, kc_ref, kp_ref, vc_ref, vp_ref, da_ref, dan_ref, lt_ref, ltn_ref, dl_ref, dln_ref, o_ref,
             kcat, vcat, dk_acc, dv_acc):
        nb = pl.program_id(1)
        kcat[0:BLK, :] = kp_ref[...]
        kcat[BLK:, :] = kc_ref[...]
        vcat[0:BLK, :] = vp_ref[...]
        vcat[BLK:, :] = vc_ref[...]
        qi = lax.broadcasted_iota(jnp.int32, (BLK, BLK), 0)
        kj = lax.broadcasted_iota(jnp.int32, (BLK, BLK), 1)
        valid_next = (kj >= qi) & (nb < ntile - 1)
        band, band_first = _band_masks(nb == 0)
        for hh in range(HEADS_PER_GROUP):
            cs = slice(hh * HEAD_DIM, (hh + 1) * HEAD_DIM)
            dk_acc[...] = jnp.zeros_like(dk_acc)
            dv_acc[...] = jnp.zeros_like(dv_acc)
            for b in range(nsub):
                rs = slice(b * BLK, (b + 1) * BLK)
                ks = slice(b * BLK, (b + 2) * BLK)
                valid = band_first if b == 0 else band
                qb, kk, vv, dab = q_ref[rs, cs], kcat[ks, cs], vcat[ks, cs], da_ref[rs, cs]
                s = lax.dot_general(qb, kk, _DNUMS["nt"], preferred_element_type=F32) * scale
                p = jnp.where(valid, jnp.exp(s - lt_ref[rs, hh:hh + 1]), 0.0)
                dp = lax.dot_general(dab, vv, _DNUMS["nt"], preferred_element_type=F32)
                ds = (p * (dp - dl_ref[rs, hh:hh + 1])).astype(BF16)
                o_ref[0, rs, cs] = jnp.dot(ds, kk, preferred_element_type=F32) * scale
                dk_acc[ks, :] += lax.dot_general(ds, qb, _DNUMS["tn"], preferred_element_type=F32) * scale
                dv_acc[ks, :] += lax.dot_general(p.astype(BF16), dab, _DNUMS["tn"], preferred_element_type=F32)
            ks = slice(nsub * BLK, (nsub + 1) * BLK)
            qn, kl, vl, dan = qn_ref[:, cs], kcat[ks, cs], vcat[ks, cs], dan_ref[:, cs]
            s = lax.dot_general(qn, kl, _DNUMS["nt"], preferred_element_type=F32) * scale
            p = jnp.where(valid_next, jnp.exp(s - ltn_ref[:, hh:hh + 1]), 0.0)
            dp = lax.dot_general(dan, vl, _DNUMS["nt"], preferred_element_type=F32)
            ds = (p * (dp - dln_ref[:, hh:hh + 1])).astype(BF16)
            dk_acc[ks, :] += lax.dot_general(ds, qn, _DNUMS["tn"], preferred_element_type=F32) * scale
            dv_acc[ks, :] += lax.dot_general(p.astype(BF16), dan, _DNUMS["tn"], preferred_element_type=F32)
            o_ref[1, :, cs] = dk_acc[BLK:, :]
            o_ref[2, :, cs] = dv_acc[BLK:, :]

    nxt = lambda nb: jnp.minimum((nb + 1) * nsub, nblk - 1)
    prv = lambda nb: jnp.maximum(nb * nsub - 1, 0)
    cur3 = lambda c: _bs((None, qt, GROUP_W), lambda r, nb: (c, nb, r))
    in_specs = [
        cur3(0), _bs((None, BLK, GROUP_W), lambda r, nb: (0, nxt(nb), r)),
        cur3(1), _bs((None, BLK, GROUP_W), lambda r, nb: (1, prv(nb), r)),
        cur3(2), _bs((None, BLK, GROUP_W), lambda r, nb: (2, prv(nb), r)),
        _bs((qt, GROUP_W), lambda r, nb: (nb, r)), _bs((BLK, GROUP_W), lambda r, nb: (nxt(nb), r)),
        _bs((None, qt, HEAD_DIM), lambda r, nb: (r, nb, 0)), _bs((None, BLK, HEAD_DIM), lambda r, nb: (r, nxt(nb), 0)),
        _bs((None, qt, HEAD_DIM), lambda r, nb: (r, nb, 0)), _bs((None, BLK, HEAD_DIM), lambda r, nb: (r, nxt(nb), 0)),
    ]
    return pl.pallas_call(
        body, grid=(d, ntile), in_specs=in_specs, out_specs=_bs((3, qt, GROUP_W), lambda r, nb: (0, nb, r)),
        out_shape=SDS((3, ell, d * GROUP_W), F32),
        scratch_shapes=[pltpu.VMEM((qt + BLK, GROUP_W), BF16)] * 2 + [pltpu.VMEM((qt + BLK, HEAD_DIM), F32)] * 2,
        compiler_params=_cp(2), name=f"attn_bwd_d{d}")(qkv, qkv, qkv, qkv, qkv, qkv, d_a, d_a, lt, lt, delta, delta)


def _undilate_rope_bwd(dqkvs, rope_c, rope_s, tm):
    t = rope_c.shape[0]

    def body(g0, g1, g2, c_s, s_s, o_ref, nat):
        c = pl.program_id(1)
        for g, (d, g_ref) in enumerate(zip(DILATIONS, (g0, g1, g2))):
            n = tm // d
            for r in range(d):
                for hh in range(HEADS_PER_GROUP):
                    oc = r * GROUP_W + hh * HEAD_DIM
                    nat[g * HEADS_PER_GROUP + hh, _strided(r, n, d), :] = g_ref[:, oc:oc + HEAD_DIM]

        @pl.when(c < 2)
        def _():
            cc, ss = c_s[...], s_s[...]
            first_half = lax.broadcasted_iota(jnp.int32, cc.shape, 1) < ROPE_HALF
            for h in range(QK_W // HEAD_DIM):
                xv = nat[h]
                y = xv * cc - _rope_partner(xv, first_half) * ss
                o_ref[:, h * HEAD_DIM:(h + 1) * HEAD_DIM] = y.astype(BF16)

        @pl.when(c == 2)
        def _():
            for h in range(QK_W // HEAD_DIM):
                o_ref[:, h * HEAD_DIM:(h + 1) * HEAD_DIM] = nat[h].astype(BF16)

    return pl.pallas_call(
        body, grid=(t // tm, 3),
        in_specs=[_bs((None, tm // d, d * GROUP_W), lambda i, c: (c, i, 0)) for d in DILATIONS]
        + [_bs((tm, HEAD_DIM), lambda i, c: (i, 0))] * 2,
        out_specs=_bs((tm, QK_W), lambda i, c: (i, c)), out_shape=SDS((t, 3 * QK_W), BF16),
        scratch_shapes=[pltpu.VMEM((QK_W // HEAD_DIM, tm, HEAD_DIM), F32)],
        compiler_params=_cp(2), name="undilate_rope_bwd")(*dqkvs, rope_c, rope_s)


def _cmul(ar, ai, br, bi):
    return ar * br - ai * bi, ar * bi + ai * br


def _ssm_disc(a_re, a_im, log_dt, nsq):
    def body(lr_ref, li_ref, ldt_ref, br_ref, bi_ref, zr_ref, zi_ref, pr_ref, pi_ref):
        lr, li = lr_ref[...], li_ref[...]
        dt = jnp.exp(ldt_ref[...])
        mag = jnp.exp(lr * dt)
        bar_re, bar_im = mag * jnp.cos(li * dt), mag * jnp.sin(li * dt)
        nr, ni = bar_re - 1.0, bar_im
        den = lr * lr + li * li
        br_ref[...], bi_ref[...] = bar_re, bar_im
        zr_ref[...] = (nr * lr + ni * li) / den
        zi_ref[...] = (ni * lr - nr * li) / den
        pr, pi = bar_re, bar_im
        for _ in range(nsq):
            pr, pi = _cmul(pr, pi, pr, pi)
        pr_ref[...], pi_ref[...] = pr, pi

    return pl.pallas_call(body, out_shape=[SDS(a_re.shape, F32)] * 6, name="ssm_discretise")(a_re, a_im, log_dt)


def _ssm_scale_b(z_re, z_im, b_re, b_im):
    def body(zr_ref, zi_ref, br_ref, bi_ref, or_ref, oi_ref):
        zr, zi, br, bi = zr_ref[...], zi_ref[...], br_ref[...], bi_ref[...]
        or_ref[...] = zr * br - zi * bi
        oi_ref[...] = zr * bi + zi * br

    return pl.pallas_call(body, out_shape=[SDS(b_re.shape, F32)] * 2, name="ssm_scale_b")(z_re, z_im, b_re, b_im)


def _ssm_scale_b_bwd(z_re, z_im, b_re, b_im, g_re, g_im):
    def body(zr_ref, zi_ref, br_ref, bi_ref, gr_ref, gi_ref, dbr_ref, dbi_ref, dzr_ref, dzi_ref):
        zr, zi, br, bi, gr, gi = zr_ref[...], zi_ref[...], br_ref[...], bi_ref[...], gr_ref[...], gi_ref[...]
        dbr_ref[...] = zr * gr + zi * gi
        dbi_ref[...] = zr * gi - zi * gr
        dzr_ref[...] = jnp.sum(br * gr + bi * gi, axis=-1, keepdims=True)
        dzi_ref[...] = jnp.sum(br * gi - bi * gr, axis=-1, keepdims=True)

    return pl.pallas_call(body, out_shape=[SDS(b_re.shape, F32)] * 2 + [SDS(z_re.shape, F32)] * 2,
                          name="ssm_scale_b_bwd")(z_re, z_im, b_re, b_im, g_re, g_im)


def _ssm_disc_bwd(a_re, a_im, log_dt, gb_re, gb_im, gz_re, gz_im):
    def body(lr_ref, li_ref, ldt_ref, gbr_ref, gbi_ref, gzr_ref, gzi_ref, dar_ref, dai_ref, dldt_ref):
        lr, li = lr_ref[...], li_ref[...]
        dt = jnp.exp(ldt_ref[...])
        mag = jnp.exp(lr * dt)
        bar_re, bar_im = mag * jnp.cos(li * dt), mag * jnp.sin(li * dt)
        nr, ni = bar_re - 1.0, bar_im
        den = lr * lr + li * li
        zr, zi = (nr * lr + ni * li) / den, (ni * lr - nr * li) / den
        gzr, gzi = gzr_ref[...], gzi_ref[...]
        gbr = gbr_ref[...] + (lr * gzr - li * gzi) / den
        gbi = gbi_ref[...] + (lr * gzi + li * gzr) / den
        qr, qi = (zr * lr + zi * li) / den, (zi * lr - zr * li) / den
        dar_ref[...] = dt * (bar_re * gbr + bar_im * gbi) - qr * gzr - qi * gzi
        dai_ref[...] = dt * (bar_re * gbi - bar_im * gbr) - qr * gzi + qi * gzr
        wr, wi = lr * bar_re - li * bar_im, lr * bar_im + li * bar_re
        dldt_ref[...] = dt * jnp.sum(wr * gbr + wi * gbi, axis=-1, keepdims=True)

    return pl.pallas_call(body, out_shape=[SDS(a_re.shape, F32)] * 2 + [SDS(log_dt.shape, F32)],
                          name="ssm_discretise_bwd")(a_re, a_im, log_dt, gb_re, gb_im, gz_re, gz_im)


def _interleave_epilogue(prods, extra_refs, out_refs, scratch_refs):
    uv = prods[0]
    tmp = scratch_refs[0]
    n = uv.shape[0] // N_DEV
    for b in range(SSM_W // BLK):
        cs = slice(b * BLK, (b + 1) * BLK)
        for j in range(N_DEV):
            tmp[b, pl.ds(j, n, stride=N_DEV), :] = uv[j * n:(j + 1) * n, cs]
        out_refs[0][:, cs] = tmp[b]
        out_refs[1][:, cs] = tmp[b].astype(BF16)


def _drive(src_ref, mat_ref, dst, mode):
    for kn in range(2 * SSM_NB):
        n = kn % SSM_NB
        a = src_ref[:, n * BLK:(n + 1) * BLK]
        dst[:, kn * 512:(kn + 1) * 512] = lax.dot_general(a, mat_ref[kn], _DNUMS[mode], preferred_element_type=F32)


def _scan_chunk(src, lam_ref, carry, *, reverse, store=None, h_ref=None, acc=None):
    steps = src.shape[0] // 8
    for c in range(NSTATE // SCAN_LANES):
        re = slice(c * SCAN_LANES, (c + 1) * SCAN_LANES)
        im = slice(NSTATE + c * SCAN_LANES, NSTATE + (c + 1) * SCAN_LANES)
        ar, ai = lam_ref[:, re], lam_ref[:, im]

        def step(s, val):
            i = (steps - 1 - s) if reverse else s
            rows = pl.ds(pl.multiple_of(i * 8, 8), 8)
            if acc is not None:
                hr, hi, dr, di = val
                pr, pi = h_ref[rows, re], h_ref[rows, im]
                dr = dr + hr * pr + hi * pi
                di = di + hi * pr - hr * pi
            else:
                hr, hi = val
            nr = ar * hr - ai * hi + src[rows, re]
            ni = ar * hi + ai * hr + src[rows, im]
            if store is not None:
                store[rows, re] = nr
                store[rows, im] = ni
            return (nr, ni, dr, di) if acc is not None else (nr, ni)

        init = (carry[:, re], carry[:, im])
        if acc is not None:
            init = init + (acc[:, re], acc[:, im])
        out = lax.fori_loop(0, steps, step, init, unroll=4)
        carry[:, re], carry[:, im] = out[0], out[1]
        if acc is not None:
            acc[:, re], acc[:, im] = out[2], out[3]


def _segment_carries(e_ref, pw_ref, out_ref, reverse):
    pr, pi = pw_ref[:, 0:NSTATE], pw_ref[:, NSTATE:]
    hr = jnp.zeros((1, NSTATE), F32)
    hi = jnp.zeros((1, NSTATE), F32)
    order = range(N_DEV - 1, -1, -1) if reverse else range(N_DEV)
    for j in order:
        out_ref[j:j + 1, 0:NSTATE] = hr
        out_ref[j:j + 1, NSTATE:] = hi
        tr, ti = _cmul(pr, pi, hr, hi)
        hr, hi = e_ref[j:j + 1, 0:NSTATE] + tr, e_ref[j:j + 1, NSTATE:] + ti


def _ssm_carries(name, src, mat, mode, lam8, pw, reverse):
    t = src.shape[0]
    nchunk = t // SCAN_ROWS

    def body(src_ref, mat_ref, lam_ref, pw_ref, out_ref, drive, carry):
        c = pl.program_id(0)

        @pl.when(c == 0)
        def _():
            carry[...] = jnp.zeros_like(carry)

        _drive(src_ref, mat_ref, drive, mode)
        _scan_chunk(drive, lam_ref, carry, reverse=reverse)

        @pl.when(c == nchunk - 1)
        def _():
            _segment_carries(carry, pw_ref, out_ref, reverse)

    blk = (lambda c: (nchunk - 1 - c, 0)) if reverse else (lambda c: (c, 0))
    return pl.pallas_call(
        body, grid=(nchunk,),
        in_specs=[_bs((SCAN_ROWS, SSM_W), blk), _bs(mat.shape, lambda c: (0, 0, 0)), _bs((8, 2 * NSTATE), lambda c: (0, 0)),
                  _bs((1, 2 * NSTATE), lambda c: (0, 0))],
        out_specs=_bs((8, 2 * NSTATE), lambda c: (0, 0)), out_shape=SDS((8, 2 * NSTATE), F32),
        scratch_shapes=[pltpu.VMEM((SCAN_ROWS, 2 * NSTATE), F32), pltpu.VMEM((8, 2 * NSTATE), F32)],
        compiler_params=_cp(1), name=name)(src, mat, lam8, pw)


def _ssm_fwd(u_bf, u, d_skip, bd, cd, lam8, start):
    t = u_bf.shape[0]
    nchunk = t // SCAN_ROWS
    per_seg = SCAN_ROWS // N_DEV

    def body(ub_ref, u_ref, d_ref, bd_ref, cd_ref, lam_ref, start_ref, h_ref, ys_ref, yg_ref, drive, carry, tmp):
        @pl.when(pl.program_id(0) == 0)
        def _():
            carry[...] = start_ref[...]

        _drive(ub_ref, bd_ref, drive, "nn")
        _scan_chunk(drive, lam_ref, carry, reverse=False, store=h_ref)
        for n in range(SSM_NB):
            cs = slice(n * BLK, (n + 1) * BLK)
            hr = h_ref[:, n * 512:(n + 1) * 512].astype(BF16)
            hi = h_ref[:, NSTATE + n * 512:NSTATE + (n + 1) * 512].astype(BF16)
            ys = (jnp.dot(hr, cd_ref[n], preferred_element_type=F32) + jnp.dot(hi, cd_ref[SSM_NB + n], preferred_element_type=F32)
                  + d_ref[:, cs] * u_ref[:, cs])
            ys_ref[:, cs] = ys
            tmp[n] = _gelu_parts(ys)[0]
            for j in range(N_DEV):
                yg_ref[j, :, cs] = tmp[n, pl.ds(j, per_seg, stride=N_DEV), :].astype(BF16)

    row = _bs((SCAN_ROWS, SSM_W), lambda c: (c, 0))
    h, ys, yg = pl.pallas_call(
        body, grid=(nchunk,),
        in_specs=[row, row, _bs((1, SSM_W), lambda c: (0, 0)), _bs(bd.shape, lambda c: (0, 0, 0)), _bs(cd.shape, lambda c: (0, 0, 0)),
                  _bs((8, 2 * NSTATE), lambda c: (0, 0)), _bs((8, 2 * NSTATE), lambda c: (0, 0))],
        out_specs=[_bs((SCAN_ROWS, 2 * NSTATE), lambda c: (c, 0)), row, _bs((N_DEV, per_seg, SSM_W), lambda c: (0, c, 0))],
        out_shape=[SDS((t, 2 * NSTATE), F32), SDS((t, SSM_W), F32), SDS((N_DEV, t // N_DEV, SSM_W), BF16)],
        scratch_shapes=[pltpu.VMEM((SCAN_ROWS, 2 * NSTATE), F32), pltpu.VMEM((8, 2 * NSTATE), F32),
                        pltpu.VMEM((SSM_NB, SCAN_ROWS, BLK), F32)],
        compiler_params=_cp(1), name="ssm_scan_fwd")(u_bf, u, d_skip, bd, cd, lam8, start)
    return h, ys, yg.reshape(t, SSM_W)


def _ssm_bwd(dys_bf, dys, d_skip, u_bf, h, bd, cd, lamc8, start):
    t = u_bf.shape[0]
    nchunk = t // SCAN_ROWS
    per_seg = SCAN_ROWS // N_DEV

    def body(dys_ref, dysf_ref, d_ref, u_ref, h_ref, bd_ref, cd_ref, lam_ref, start_ref, du_ref, dlam_ref, dbd_ref, dcd_ref,
             drive, adj, carry, tmp):
        c = pl.program_id(0)

        @pl.when(c == 0)
        def _():
            carry[...] = start_ref[...]
            dlam_ref[...] = jnp.zeros_like(dlam_ref)
            dbd_ref[...] = jnp.zeros_like(dbd_ref)
            dcd_ref[...] = jnp.zeros_like(dcd_ref)

        _drive(dys_ref, cd_ref, drive, "nt")
        _scan_chunk(drive, lam_ref, carry, reverse=True, store=adj, h_ref=h_ref, acc=dlam_ref)
        for n in range(SSM_NB):
            cs = slice(n * BLK, (n + 1) * BLK)
            acc = None
            for k in range(2):
                kn = k * SSM_NB + n
                ss = slice(kn * 512, (kn + 1) * 512)
                lam_b = adj[:, ss].astype(BF16)
                part = lax.dot_general(lam_b, bd_ref[kn], _DNUMS["nt"], preferred_element_type=F32)
                acc = part if acc is None else acc + part
                dbd_ref[kn] += lax.dot_general(u_ref[:, cs], lam_b, _DNUMS["tn"], preferred_element_type=F32)
                dcd_ref[kn] += lax.dot_general(h_ref[:, ss].astype(BF16), dys_ref[:, cs], _DNUMS["tn"],
                                               preferred_element_type=F32)
            tmp[n] = acc + d_ref[:, cs] * dysf_ref[:, cs]
            for j in range(N_DEV):
                du_ref[j, :, cs] = tmp[n, pl.ds(j, per_seg, stride=N_DEV), :].astype(BF16)

    rev = lambda c: (nchunk - 1 - c, 0)
    const2 = lambda c: (0, 0)
    const3 = lambda c: (0, 0, 0)
    row = _bs((SCAN_ROWS, SSM_W), rev)
    du, dlam, dbd, dcd = pl.pallas_call(
        body, grid=(nchunk,),
        in_specs=[row, row, _bs((1, SSM_W), const2), row, _bs((SCAN_ROWS, 2 * NSTATE), rev),
                  _bs(bd.shape, const3), _bs(cd.shape, const3), _bs((8, 2 * NSTATE), const2), _bs((8, 2 * NSTATE), const2)],
        out_specs=[_bs((N_DEV, per_seg, SSM_W), lambda c: (0, nchunk - 1 - c, 0)), _bs((8, 2 * NSTATE), const2),
                   _bs(bd.shape, const3), _bs(cd.shape, const3)],
        out_shape=[SDS((N_DEV, t // N_DEV, SSM_W), BF16), SDS((8, 2 * NSTATE), F32), SDS(bd.shape, F32), SDS(cd.shape, F32)],
        scratch_shapes=[pltpu.VMEM((SCAN_ROWS, 2 * NSTATE), F32), pltpu.VMEM((SCAN_ROWS, 2 * NSTATE), F32),
                        pltpu.VMEM((8, 2 * NSTATE), F32), pltpu.VMEM((SSM_NB, SCAN_ROWS, BLK), F32)],
        compiler_params=_cp(1), name="ssm_scan_bwd")(dys_bf, dys, d_skip, u_bf, h, bd, cd, lamc8, start)
    return du.reshape(t, SSM_W), dlam, dbd, dcd


def _gelu_parts(x):
    c0 = math.sqrt(2.0 / math.pi)
    inner = c0 * (x + 0.044715 * x * x * x)
    th = jnp.tanh(inner)
    val = 0.5 * x * (1.0 + th)
    grad = 0.5 * (1.0 + th) + 0.5 * x * (1.0 - th * th) * c0 * (1.0 + 3.0 * 0.044715 * x * x)
    return val, grad


def _ssm_carries_bwd(d_yg, ys, u, cd, lamc8, pwc):
    t = u.shape[0]
    nchunk = t // SCAN_ROWS
    per_seg = SCAN_ROWS // N_DEV

    def body(dg_ref, ys_ref, u_ref, cd_ref, lam_ref, pw_ref, out_ref, dys_ref, dysb_ref, dd_ref, drive, carry, tmp):
        c = pl.program_id(0)

        @pl.when(c == 0)
        def _():
            carry[...] = jnp.zeros_like(carry)

        for n in range(SSM_W // BLK):
            for j in range(N_DEV):
                tmp[n, pl.ds(j, per_seg, stride=N_DEV), :] = dg_ref[j, :, n * BLK:(n + 1) * BLK]
        dyg = jnp.concatenate([tmp[n] for n in range(SSM_W // BLK)], axis=1)
        dys = dyg * _gelu_parts(ys_ref[...])[1]
        dys_ref[...] = dys
        dysb_ref[...] = dys.astype(BF16)
        _accumulate_rows(dd_ref, jnp.sum(dys * u_ref[...], axis=0, keepdims=True))
        _drive(dysb_ref, cd_ref, drive, "nt")
        _scan_chunk(drive, lam_ref, carry, reverse=True)

        @pl.when(c == nchunk - 1)
        def _():
            _segment_carries(carry, pw_ref, out_ref, True)

    rev = lambda c: (nchunk - 1 - c, 0)
    row = _bs((SCAN_ROWS, SSM_W), rev)
    const2 = lambda c: (0, 0)
    return pl.pallas_call(
        body, grid=(nchunk,),
        in_specs=[_bs((N_DEV, per_seg, SSM_W), lambda c: (0, nchunk - 1 - c, 0)), row, row, _bs(cd.shape, lambda c: (0, 0, 0)),
                  _bs((8, 2 * NSTATE), const2), _bs((1, 2 * NSTATE), const2)],
        out_specs=[_bs((8, 2 * NSTATE), const2), row, row, _bs((1, SSM_W), const2)],
        out_shape=[SDS((8, 2 * NSTATE), F32), SDS((t, SSM_W), F32), SDS((t, SSM_W), BF16), SDS((1, SSM_W), F32)],
        scratch_shapes=[pltpu.VMEM((SCAN_ROWS, 2 * NSTATE), F32), pltpu.VMEM((8, 2 * NSTATE), F32),
                        pltpu.VMEM((SSM_W // BLK, SCAN_ROWS, BLK), F32)],
        compiler_params=_cp(1), name="ssm_carries_bwd")(d_yg.reshape(N_DEV, t // N_DEV, SSM_W), ys, u, cd, lamc8, pwc)


def _block_diag(blocks):
    nb, ng, r, c = blocks.shape
    eye = jnp.eye(ng, dtype=blocks.dtype)
    return (blocks[:, :, :, None, :] * eye[None, :, None, :, None]).reshape(nb, ng * r, ng * c)


def _diag_blocks(full, r, c):
    k, nb = full.shape[:2]
    ng = full.shape[2] // r
    x = full.reshape(k, nb, ng, r, ng, c)
    eye = jnp.eye(ng, dtype=full.dtype)
    return jnp.sum(x * eye[None, None, :, None, :, None], axis=4).reshape(k, nb * ng, r, c)


_SMALL = ("a_re", "a_im", "log_dt", "b_re", "b_im", "c_re", "c_im", "d_skip", "g_ffn", "g_final")


def _pack_small(arrs):
    flat = jnp.concatenate([a.reshape(-1) for a in arrs])
    pad = (-flat.shape[0]) % (8 * 128)
    return jnp.pad(flat, (0, pad)).reshape(-1, 128)


def _unpack_small(packed, shapes):
    flat = packed.reshape(-1)
    out, off = [], 0
    for s in shapes:
        n = math.prod(s)
        out.append(flat[off:off + n].reshape(s))
        off += n
    return out


def kernel(x, p, positions, g_mix, w_in, a_re, a_im, log_dt, b_re, b_im, c_re, c_im, d_skip, w_attn_proj, w_glu_a, w_glu_b, w_out, g_ffn, w_ffn_gate, w_ffn_up, w_ffn_down, w_ple_gate, w_ple_proj, g_final, loss_target, m_g_mix, m_w_in, m_a_re, m_a_im, m_log_dt, m_b_re, m_b_im, m_c_re, m_c_im, m_d_skip, m_w_attn_proj, m_w_glu_a, m_w_glu_b, m_w_out, m_g_ffn, m_w_ffn_gate, m_w_ffn_up, m_w_ffn_down, m_w_ple_gate, m_w_ple_proj, m_g_final, v_g_mix, v_w_in, v_a_re, v_a_im, v_log_dt, v_b_re, v_b_im, v_c_re, v_c_im, v_d_skip, v_w_attn_proj, v_w_glu_a, v_w_glu_b, v_w_out, v_g_ffn, v_w_ffn_gate, v_w_ffn_up, v_w_ffn_down, v_w_ple_gate, v_w_ple_proj, v_g_final):
    args = dict(locals())
    t, d = x.shape[1], x.shape[2]
    inw = w_in.shape[2] * N_DEV
    fs = w_ffn_gate.shape[2]
    ff = fs * N_DEV
    ple = w_ple_proj.shape[1]
    seg = t // N_DEV
    assert inw == 3 * QK_W + SSM_W + 2 * d and t % (N_DEV * SCAN_ROWS // 8) == 0 and seg & (seg - 1) == 0
    tm = min(1024, t)
    te = min(512, t)
    tk = min(2048, t)
    ucol = (3 * QK_W) // SSM_W
    gcol = (3 * QK_W + SSM_W) // d
    assert (3 * QK_W + SSM_W) % d == 0

    x2, p2, tgt = x[0], p[0, 0], loss_target[0]
    pos = positions.reshape(t, 1)
    inv = ROPE_THETA ** (-jnp.arange(ROPE_HALF, dtype=F32) * 2.0 / ROPE_DIM)
    invf = jnp.concatenate([inv, inv, jnp.zeros((HEAD_DIM - ROPE_DIM,), F32)]).reshape(1, HEAD_DIM)

    wnames = ("w_in", "w_attn_proj", "w_glu_a", "w_glu_b", "w_out", "w_ffn_gate", "w_ffn_up", "w_ffn_down", "w_ple_gate",
              "w_ple_proj")
    kinds = ("cols", "cols", "cols", "cols", "rows", "slot", "slot", "rows", "rows", "cols")
    shards = [args[n][0].astype(BF16) for n in wnames]
    sizes = [s.shape[0] if k == "rows" else s.shape[-1] for s, k in zip(shards, kinds)]
    ag = _exchange_start("gather_weights_start", shards, kinds, sizes, True)

    row_d = _bs((tm, d), lambda i, j, k: (i, 0))
    row_e = _bs((te, d), lambda i, j, k: (i, 0))
    vec_d = _bs((1, d), lambda i, j, k: (0, 0))
    sq_w = _bs((d, d), lambda i, j, k: (0, 0))
    n1 = _rms_fwd("norm_mix", x2, g_mix + ag[3][0:1, 0:1], tm)

    nsq = seg.bit_length() - 1
    bar_re, bar_im, z_re, z_im, pw_re, pw_im = _ssm_disc(a_re[0], a_im[0], log_dt.reshape(SSM_GROUPS, 1), nsq)
    gp = SSM_GROUPS * SSM_STATE
    b_re2, b_im2 = b_re.reshape(gp, SSM_GROUP), b_im.reshape(gp, SSM_GROUP)
    bb_re, bb_im = _ssm_scale_b(z_re.reshape(gp, 1), z_im.reshape(gp, 1), b_re2, b_im2)

    def chunks(a, r, c):
        return a.reshape(SSM_NB, SSM_GROUPS // SSM_NB, r, c)

    bbt = lambda a: jnp.swapaxes(a.reshape(SSM_GROUPS, SSM_STATE, SSM_GROUP), 1, 2)
    bd = jnp.concatenate([_block_diag(chunks(bbt(bb_re), SSM_GROUP, SSM_STATE)),
                          _block_diag(chunks(bbt(bb_im), SSM_GROUP, SSM_STATE))]).astype(BF16)
    ct = lambda a: jnp.swapaxes(a[0], 1, 2)
    cd = jnp.concatenate([_block_diag(chunks(ct(c_re), SSM_STATE, SSM_GROUP)),
                          _block_diag(chunks(-ct(c_im), SSM_STATE, SSM_GROUP))]).astype(BF16)
    lam = jnp.concatenate([bar_re.reshape(1, gp), bar_im.reshape(1, gp)], axis=1)
    lamc = jnp.concatenate([bar_re.reshape(1, gp), -bar_im.reshape(1, gp)], axis=1)
    pw = jnp.concatenate([pw_re.reshape(1, gp), pw_im.reshape(1, gp)], axis=1)
    pwc = jnp.concatenate([pw_re.reshape(1, gp), -pw_im.reshape(1, gp)], axis=1)
    lam8, lamc8 = jnp.broadcast_to(lam, (8, 2 * gp)), jnp.broadcast_to(lamc, (8, 2 * gp))

    pk = lambda pre: jnp.concatenate([_pack_small([args[pre + n] for n in _SMALL]), _pack_small([args[pre + "g_mix"]])])
    packed = [pk(""), pk("m_"), pk("v_")]

    W_in, = _exchange_wait("gather_w_in_wait", ag, [0], kinds, sizes, True, [n1, bd, cd, lam8, lamc8, pw, pwc] + packed)
    tab = (SDS((t, HEAD_DIM), F32), _bs((tm, HEAD_DIM), lambda i, j, k: (i, 0)))
    *qkv, rope_c, rope_s = _mm(
        "qkv_proj", (t // tm, 3, 1), [("nn", n1, row_d, W_in, _bs((d, QK_W), lambda i, j, k: (0, j)))],
        [(SDS((3, t // dil, dil * GROUP_W), BF16), _bs((None, tm // dil, dil * GROUP_W), lambda i, j, k: (j, i, 0)))
         for dil in DILATIONS] + [tab, tab],
        extras=[(pos, _bs((tm, 1), lambda i, j, k: (i, 0))), (invf, _bs((1, HEAD_DIM), lambda i, j, k: (0, 0)))],
        epilogue=_rope_dilate_epilogue(tm), scratch=[pltpu.VMEM((QK_W // HEAD_DIM, tm, HEAD_DIM), F32)])
    row_s = _bs((tm, SSM_W), lambda i, j, k: (i, 0))
    u_perm, u_bf = _mm("u_proj", (t // tm, 1, 1),
                       [("nn", n1.reshape(N_DEV, seg, d), _bs((N_DEV, tm // N_DEV, d), lambda i, j, k: (0, i, 0)), W_in,
                         _bs((d, SSM_W), lambda i, j, k: (0, ucol)))],
                       [(SDS((t, SSM_W), F32), row_s), (SDS((t, SSM_W), BF16), row_s)], epilogue=_interleave_epilogue,
                       scratch=[pltpu.VMEM((SSM_W // BLK, tm, BLK), F32)])
    zg, = _mm("z_gates", (t // tm, 2, 1),
              [("nn", n1, row_d, W_in, _bs((d, d), lambda i, j, k: (0, gcol + j)))],
              [(SDS((t, 2 * d), BF16), _bs((tm, d), lambda i, j, k: (i, j)))])

    outs, lses = [], []
    for g, dil in enumerate(DILATIONS):
        o_g, l_g = _attn_fwd(qkv[g], dil, min(1024, t // dil))
        outs.append(o_g)
        lses.append(l_g)
    merged = _attn_merge(outs, lses, tm)
    attn, attn_bf, lts = merged[0], merged[1], merged[2:]

    start_f = _ssm_carries("ssm_carries_fwd", u_bf, bd, "nn", lam8, pw, False)
    dsk = d_skip.reshape(1, SSM_W)
    h_all, ys, yg_bf = _ssm_fwd(u_bf, u_perm, dsk, bd, cd, lam8, start_f)
    W_ap, W_ga, W_gb, W_out, W_fg, W_fu, W_fd, W_pg, W_pp = _exchange_wait(
        "gather_rest_wait", ag, list(range(1, len(wnames))), kinds, sizes, True, yg_bf)
    W_fg = jnp.swapaxes(W_fg, 0, 1).reshape(d, ff)
    W_fu = jnp.swapaxes(W_fu, 0, 1).reshape(d, ff)

    glu_w = _bs((SSM_W, d), lambda i, j, k: (0, 0))
    row_s = _bs((tm, SSM_W), lambda i, j, k: (i, 0))
    gate_a = _bs((te, d), lambda i, j, k: (i, 0))
    gate_s = _bs((te, d), lambda i, j, k: (i, 1))
    td_f32, td_bf = SDS((t, d), F32), SDS((t, d), BF16)
    m_bf, ya, yb, attn_d = _mm(
        "glu_merge", (t // tm, 1, 1),
        [("nn", yg_bf, row_s, W_ga, glu_w), ("nn", yg_bf, row_s, W_gb, glu_w), ("nn", attn_bf, row_s, W_ap, glu_w)],
        [(td_bf, row_d)] * 4, extras=[(zg, row_d), (zg, _bs((tm, d), lambda i, j, k: (i, 1)))], epilogue=_glu_merge_epilogue)

    h1, n2 = _mm("out_proj", (t // tm, 1, 1), [("nn", m_bf, row_d, W_out, sq_w)], [(td_f32, row_d), (td_bf, row_d)],
                 extras=[(x2, row_d), (g_ffn, vec_d)], epilogue=_out_norm_epilogue)

    tn_f = ff // 2
    nf = ff // tn_f
    hid_o = _bs((tm, tn_f), lambda j, i, k: (i, j))
    tf_bf = SDS((t, ff), BF16)
    a_rows = _bs((tm, d), lambda j, i, k: (i, 0))
    w_cols = _bs((d, tn_f), lambda j, i, k: (0, j))
    act, fg, fu = _mm("ffn_gate_up", (nf, t // tm, 1), [("nn", n2, a_rows, W_fg, w_cols), ("nn", n2, a_rows, W_fu, w_cols)],
                      [(tf_bf, hid_o)] * 3, epilogue=_swiglu_epilogue)
    w_once = pl.BlockSpec((d, d), lambda i, j, k: (0, 0), pipeline_mode=pl.Buffered(1))
    loss_part, dg_final, dh2, dh2_bf, dpp_bf, dpg_bf, h2_bf = _mm(
        "ffn_down_head", (t // te, 1, 1),
        [("nn", act, _bs((te, ff), lambda i, j, k: (i, 0)), W_fd,
          pl.BlockSpec((ff, d), lambda i, j, k: (0, 0), pipeline_mode=pl.Buffered(1))),
         ("nn", p2, _bs((te, ple), lambda i, j, k: (i, 0)), W_pp, _bs((ple, d), lambda i, j, k: (0, 0)))],
        [(SDS((1, 1), F32), _bs((1, 1), lambda i, j, k: (0, 0))), (SDS((1, d), F32), vec_d), (td_f32, row_e), (td_bf, row_e),
         (td_bf, row_e), (td_bf, row_e), (td_bf, row_e)],
        extras=[(h1, row_e), (g_final.reshape(1, d), vec_d), (tgt, row_e), (W_pg, w_once)], epilogue=_head_epilogue(t // te),
        scratch=[pltpu.VMEM((1, d), F32)])
    loss = lax.psum(loss_part[0, 0], ("x", "y", "c"))

    nkt = t // tk
    tok_a = lambda w: _bs((tk, w), lambda i, j, k: (k, 0))

    def wgrad(name, a, wa, b, wb):
        return _mm(name, (1, 1, nkt), [("tn", a, tok_a(wa), b, tok_a(wb))],
                   [(SDS((wa, wb), BF16), _bs((wa, wb), lambda i, j, k: (0, 0)))])[0]

    dW_pp = wgrad("dw_ple_proj", p2, ple, dpp_bf, d)
    dW_pg = wgrad("dw_ple_gate", h2_bf, d, dpg_bf, d)
    dfg_bf, dfu_bf = _mm("d_ffn_down", (nf, t // tm, 1),
                         [("nt", dh2_bf, a_rows, W_fd, _bs((tn_f, d), lambda j, i, k: (j, 0)))],
                         [(tf_bf, hid_o), (tf_bf, hid_o)], extras=[(fg, hid_o), (fu, hid_o)], epilogue=_swiglu_bwd_epilogue)
    dW_fd, = _mm("dw_ffn_down", (nf, 1, nkt), [("tn", act, _bs((tk, tn_f), lambda i, j, k: (k, i)), dh2_bf, tok_a(d))],
                 [(SDS((ff, d), BF16), _bs((tn_f, d), lambda i, j, k: (i, 0)))])
    hid_t = _bs((tk, tn_f), lambda i, j, k: (k, j))
    wg_o = [(SDS((d, ff), BF16), _bs((d, tn_f), lambda i, j, k: (0, j)))]
    dW_fg, = _mm("dw_ffn_gate", (1, nf, nkt), [("tn", n2, tok_a(d), dfg_bf, hid_t)], wg_o)
    dW_fu, = _mm("dw_ffn_up", (1, nf, nkt), [("tn", n2, tok_a(d), dfu_bf, hid_t)], wg_o)
    dW_fg = jnp.swapaxes(dW_fg.reshape(d, N_DEV, fs), 0, 1)
    dW_fu = jnp.swapaxes(dW_fu.reshape(d, N_DEV, fs), 0, 1)
    group = lambda names: ([kinds[wnames.index(n)] for n in names], [sizes[wnames.index(n)] for n in names])
    ffn_names = ("w_ffn_gate", "w_ffn_up", "w_ffn_down", "w_ple_gate", "w_ple_proj")
    rs_ffn = _exchange_start("scatter_ffn_start", [dW_fg, dW_fu, dW_fd, dW_pg, dW_pp], *group(ffn_names), False)
    hid_all = _bs((te, ff), lambda i, j, k: (i, 0))
    w_all = pl.BlockSpec((d, ff), lambda i, j, k: (0, 0), pipeline_mode=pl.Buffered(1))
    dh1, dh1_bf, dg_ffn = _mm("d_ffn_gate_up", (t // te, 1, 1),
                              [("nt", dfg_bf, hid_all, W_fg, w_all), ("nt", dfu_bf, hid_all, W_fu, w_all)],
                              [(td_f32, row_e), (td_bf, row_e), (SDS((1, d), F32), vec_d)],
                              extras=[(h1, row_e), (g_ffn, vec_d), (dh2, row_e)], epilogue=_rms_bwd_epilogue, after=rs_ffn[3])

    dW_out = wgrad("dw_out", m_bf, d, dh1_bf, d)
    glu_once = pl.BlockSpec((SSM_W, d), lambda i, j, k: (0, 0), pipeline_mode=pl.Buffered(1))
    row_es = _bs((te, SSM_W), lambda i, j, k: (i, 0))
    ts_f32 = SDS((t, SSM_W), F32)
    dz_g, dad_bf, dya_bf, dyb_bf, d_yg, d_attn = _mm(
        "d_out_proj", (t // te, 1, 1), [("nt", dh1_bf, row_e, W_out, w_once)],
        [(SDS((t, 2 * d), BF16), _bs((te, 2 * d), lambda i, j, k: (i, 0))), (td_bf, row_e), (td_bf, row_e), (td_bf, row_e),
         (ts_f32, row_es), (ts_f32, row_es)],
        extras=[(zg, gate_a), (zg, gate_s), (attn_d, row_e), (ya, row_e), (yb, row_e), (W_ga, glu_once), (W_gb, glu_once),
                (W_ap, glu_once)], epilogue=_merge_bwd_epilogue)

    dW_ga = wgrad("dw_glu_a", yg_bf, SSM_W, dya_bf, d)
    dW_gb = wgrad("dw_glu_b", yg_bf, SSM_W, dyb_bf, d)
    start_b, dys, dys_bf, dd_skip = _ssm_carries_bwd(d_yg, ys, u_perm, cd, lamc8, pwc)
    dz_u, dlam8, dbd, dcd = _ssm_bwd(dys_bf, dys, dsk, u_bf, h_all, bd, cd, lamc8, start_b)
    dlam = jnp.sum(dlam8, axis=0)
    dbb = _diag_blocks(dbd.reshape(2, SSM_NB, BLK, 512), SSM_GROUP, SSM_STATE)
    dbb_re = jnp.swapaxes(dbb[0], 1, 2).reshape(gp, SSM_GROUP)
    dbb_im = jnp.swapaxes(dbb[1], 1, 2).reshape(gp, SSM_GROUP)
    dcc = _diag_blocks(dcd.reshape(2, SSM_NB, 512, BLK), SSM_STATE, SSM_GROUP)
    dc_re, dc_im = jnp.swapaxes(dcc[0], 1, 2), -jnp.swapaxes(dcc[1], 1, 2)
    db_re, db_im, dz_re, dz_im = _ssm_scale_b_bwd(z_re.reshape(gp, 1), z_im.reshape(gp, 1), b_re2, b_im2, dbb_re, dbb_im)
    gshape = (SSM_GROUPS, SSM_STATE)
    da_re, da_im, dlog_dt = _ssm_disc_bwd(a_re[0], a_im[0], log_dt.reshape(SSM_GROUPS, 1), dlam[:gp].reshape(gshape),
                                          dlam[gp:].reshape(gshape), dz_re.reshape(gshape), dz_im.reshape(gshape))

    dW_ap = wgrad("dw_attn_proj", attn_bf, GROUP_W, dad_bf, d)
    pre = _attn_bwd_pre(d_attn, attn, tm)
    das, deltas = pre[:N_GROUPS], pre[N_GROUPS:]
    dqkvs = [_attn_bwd(qkv[g], das[g], lts[g], deltas[g], dil, min(1024, t // dil)) for g, dil in enumerate(DILATIONS)]
    dz_qkv = _undilate_rope_bwd(dqkvs, rope_c, rope_s, tm)

    dW_in, = _mm("dw_in_qkv", (1, 3, nkt), [("tn", n1, tok_a(d), dz_qkv, _bs((tk, QK_W), lambda i, j, k: (k, j)))],
                 [(SDS((d, inw), BF16), _bs((d, QK_W), lambda i, j, k: (0, j)))])
    dW_in, = _mm("dw_in_u", (1, 1, nkt), [("tn", n1, tok_a(d), dz_u, tok_a(SSM_W))],
                 [(SDS((d, inw), BF16), _bs((d, SSM_W), lambda i, j, k: (0, ucol)))], alias_to_out0=dW_in)
    dW_in, = _mm("dw_in_gates", (1, 2, nkt), [("tn", n1, tok_a(d), dz_g, _bs((tk, d), lambda i, j, k: (k, j)))],
                 [(SDS((d, inw), BF16), _bs((d, d), lambda i, j, k: (0, gcol + j)))], alias_to_out0=dW_in)
    small_parts = dict(a_re=da_re, a_im=da_im, log_dt=dlog_dt, b_re=db_re, b_im=db_im, c_re=dc_re, c_im=dc_im,
                       d_skip=dd_skip, g_ffn=dg_ffn, g_final=dg_final)
    small = _pack_small([small_parts[n] for n in _SMALL])
    rest_names = ("w_in", "w_attn_proj", "w_glu_a", "w_glu_b", "w_out")
    rest_kinds, rest_sizes = group(rest_names)
    rs_in = _exchange_start("scatter_rest_start", [dW_in, dW_ap, dW_ga, dW_gb, dW_out, small], rest_kinds + ["all"],
                            rest_sizes + [0], False)
    w_piece = lambda w, cb: pl.BlockSpec((d, w), lambda i, j, k: (0, cb), pipeline_mode=pl.Buffered(1))
    dx, dg_mix = _mm(
        "d_z_proj", (t // te, 1, 1),
        [("nt", dz_qkv, _bs((te, 3 * QK_W), lambda i, j, k: (i, 0)), W_in, w_piece(3 * QK_W, 0)),
         ("nt", dz_u, _bs((te, SSM_W), lambda i, j, k: (i, 0)), W_in, w_piece(SSM_W, ucol)),
         ("nt", dz_g, _bs((te, d), lambda i, j, k: (i, 0)), W_in, w_piece(d, gcol)),
         ("nt", dz_g, _bs((te, d), lambda i, j, k: (i, 1)), W_in, w_piece(d, gcol + 1))],
        [(td_f32, row_e), (SDS((1, d), F32), vec_d)],
        extras=[(x2, row_e), (g_mix, vec_d), (dh1, row_e)], epilogue=_rms_bwd_epilogue, after=rs_in[3])

    received = dict(zip(ffn_names, _exchange_wait("scatter_ffn_wait", rs_ffn, list(range(len(ffn_names))), *group(ffn_names),
                                                  False, dx)))
    *landed, small_all = _exchange_wait("scatter_rest_wait", rs_in, list(range(len(rest_names) + 1)), rest_kinds + ["all"],
                                        rest_sizes + [0], False, dx)
    received.update(zip(rest_names, landed))

    new = {}
    for n in wnames:
        new[n] = [o.reshape(args[n].shape)
                  for o in _adamw("adamw_" + n, received[n], args[n][0], args["m_" + n][0], args["v_" + n][0])]
    g_mix_all = _gather_small(_pack_small([dg_mix]))
    sm = _adamw("adamw_small", jnp.concatenate([small_all, g_mix_all], axis=1), *packed)
    rows_a = small.shape[0]
    shapes = [args[n].shape for n in _SMALL]
    for n, vals in zip(_SMALL, zip(*[_unpack_small(o[:rows_a], shapes) for o in sm])):
        new[n] = list(vals)
    new["g_mix"] = [_unpack_small(o[rows_a:], [g_mix.shape])[0] for o in sm]

    order = ("g_mix", "w_in", "a_re", "a_im", "log_dt", "b_re", "b_im", "c_re", "c_im", "d_skip", "w_attn_proj", "w_glu_a",
             "w_glu_b", "w_out", "g_ffn", "w_ffn_gate", "w_ffn_up", "w_ffn_down", "w_ple_gate", "w_ple_proj", "g_final")
    return (loss, dx.reshape(x.shape), *[new[n][0] for n in order], *[new[n][1] for n in order],
            *[new[n][2] for n in order], *[new[n][3] for n in order])
```

```python
import functools
import math

import jax
import jax.numpy as jnp
from jax import lax
from jax.experimental import pallas as pl
from jax.experimental.pallas import tpu as pltpu

F32 = jnp.float32
BF16 = jnp.bfloat16
SDS = jax.ShapeDtypeStruct

N_DEV = 8
HEAD_DIM = 128
HEADS_PER_GROUP = 4
GROUP_W = HEADS_PER_GROUP * HEAD_DIM
DILATIONS = (1, 4, 16)
N_GROUPS = len(DILATIONS)
QK_W = N_GROUPS * GROUP_W
BLK = 128
ROPE_THETA = 500000.0
ROPE_DIM = HEAD_DIM // 4
ROPE_HALF = ROPE_DIM // 2
SSM_W = 512
SSM_GROUP = 16
SSM_GROUPS = SSM_W // SSM_GROUP
SSM_STATE = 64
NSTATE = SSM_GROUPS * SSM_STATE
SSM_NB = 4
EPS = 1e-6
ADAM_LR, ADAM_B1, ADAM_B2, ADAM_EPS, ADAM_WD, ADAM_STEP = 0.001, 0.9, 0.999, 1e-08, 0.01, 10
NEG = -1e30

VMEM_LIMIT = 52 * 1024 * 1024
SCAN_ROWS = 512
SCAN_LANES = 512


def _cp(n):
    return pltpu.CompilerParams(dimension_semantics=("arbitrary",) * n, vmem_limit_bytes=VMEM_LIMIT)


def _sigmoid(x):
    return 0.5 * jnp.tanh(0.5 * x) + 0.5


_DNUMS = {"nn": (((1,), (0,)), ((), ())), "nt": (((1,), (1,)), ((), ())), "tn": (((0,), (0,)), ((), ()))}


def _bs(shape, fn):
    return pl.BlockSpec(shape, fn)


def _store_all(prods, extra_refs, out_refs, scratch_refs):
    r = prods[0]
    for p in prods[1:]:
        r = r + p
    for e in extra_refs:
        r = r + e[...]
    for o in out_refs:
        o[...] = r.astype(o.dtype)


def _mm(name, grid, pairs, outs, extras=(), epilogue=_store_all, scratch=(), alias_to_out0=None, after=None):
    nk = grid[2]
    npair = len(pairs)
    steps = [p[5] if len(p) > 5 else nk for p in pairs]

    def block(spec):
        return tuple(s for s in spec.block_shape if s is not None)

    def rows2d(shape):
        return (math.prod(shape[:-1]), shape[-1]) if len(shape) == 3 else shape

    acc_shapes = [jax.eval_shape(lambda u, v, dn=_DNUMS[p[0]]: lax.dot_general(u, v, dn, preferred_element_type=F32),
                                 SDS(rows2d(block(p[2])), BF16), SDS(block(p[4]), BF16)).shape for p in pairs]
    if nk == 1:
        acc_shapes = []
    n_in = 2 * npair + len(extras) + (alias_to_out0 is not None) + (after is not None)

    def body(*refs):
        extra_refs = refs[2 * npair:2 * npair + len(extras)]
        out_refs = refs[n_in:n_in + len(outs)]
        rest = refs[n_in + len(outs):]
        acc_refs = rest[:len(acc_shapes)]
        scratch_refs = rest[len(acc_refs):]
        k = pl.program_id(2)

        def product(i):
            a = refs[2 * i][...]
            if a.ndim == 3:
                a = a.reshape(-1, a.shape[-1])
            return lax.dot_general(a.astype(BF16), refs[2 * i + 1][...].astype(BF16), _DNUMS[pairs[i][0]],
                                   preferred_element_type=F32)

        if nk == 1:
            epilogue([product(i) for i in range(npair)], extra_refs, out_refs, scratch_refs)
            return
        for i in range(npair):
            @pl.when(k == 0)
            def _(i=i):
                acc_refs[i][...] = product(i)

            @pl.when((k > 0) & (k < steps[i]))
            def _(i=i):
                acc_refs[i][...] += product(i)

        @pl.when(k == nk - 1)
        def _():
            epilogue([a[...] for a in acc_refs], extra_refs, out_refs, scratch_refs)

    ins, in_specs = [], []
    for p in pairs:
        ins += [p[1], p[3]]
        in_specs += [p[2], p[4]]
    ins += [e[0] for e in extras]
    in_specs += [e[1] for e in extras]
    aliases = {}
    if alias_to_out0 is not None:
        aliases = {len(ins): 0}
        ins.append(alias_to_out0)
        in_specs.append(pl.BlockSpec(memory_space=pl.ANY))
    if after is not None:
        ins.append(after)
        in_specs.append(pl.BlockSpec(memory_space=pl.ANY))
    scratch_shapes = [pltpu.VMEM(s, F32) for s in acc_shapes] + list(scratch)
    return pl.pallas_call(body, grid=grid, in_specs=in_specs, out_specs=[o[1] for o in outs], out_shape=[o[0] for o in outs],
                          scratch_shapes=scratch_shapes, input_output_aliases=aliases, compiler_params=_cp(3), name=name)(*ins)


def _my_index():
    return 4 * lax.axis_index("x") + 2 * lax.axis_index("y") + lax.axis_index("c")


def _peer(d):
    mx, my, mc = lax.axis_index("x"), lax.axis_index("y"), lax.axis_index("c")
    return (mx ^ ((d >> 2) & 1), my ^ ((d >> 1) & 1), mc ^ (d & 1))


def _win(ref, kind, j, n):
    if kind == "all":
        return ref
    if kind == "slot":
        return ref.at[j]
    if kind == "rows":
        return ref.at[pl.ds(pl.multiple_of(j * n, 8), n)]
    return ref.at[:, pl.ds(pl.multiple_of(j * n, 128), n)]


def _win7(ref, kind, n):
    if kind == "slot":
        return ref.at[pl.ds(0, 7)]
    if kind == "rows":
        return ref.at[pl.ds(0, 7 * n)]
    return ref.at[:, pl.ds(0, 7 * n)]


def _full_shape(shard_shape, kind):
    if kind == "slot":
        return (N_DEV,) + tuple(shard_shape)
    if kind == "rows":
        return (N_DEV * shard_shape[0],) + tuple(shard_shape[1:])
    return (shard_shape[0], N_DEV * shard_shape[1])


def _shard_shape(full_shape, kind, n):
    if kind == "all":
        return tuple(full_shape)
    if kind == "slot":
        return tuple(full_shape[1:])
    if kind == "rows":
        return (n,) + tuple(full_shape[1:])
    return (full_shape[0], n)


_HBM = pl.BlockSpec(memory_space=pltpu.HBM)
_SEM = pl.BlockSpec(memory_space=pltpu.SEMAPHORE)
_DATAFLOW = pltpu.SideEffectType.DATAFLOW_SIDE_EFFECTING


def _exchange_start(name, srcs, kinds, sizes, gather):
    n = len(srcs)
    if gather:
        lands = [lax.empty(_full_shape(s.shape, k), s.dtype) for s, k in zip(srcs, kinds)]
    else:
        lands = [lax.empty((N_DEV,) + _shard_shape(s.shape, k, z), s.dtype) for s, k, z in zip(srcs, kinds, sizes)]

    def body(*refs):
        src, land = refs[:n], refs[n:2 * n]
        send_sems, recv_sems, local_sems = refs[2 * n], refs[2 * n + 1], refs[2 * n + 2]
        token = refs[4 * n + 3]
        me = _my_index()
        for a in range(n):
            _local_copy(src[a], land[a], kinds[a], sizes[a], gather, me, local_sems.at[a]).start()
        for a in range(n):
            for d in range(1, N_DEV):
                px, py, pc = _peer(d)
                if gather:
                    s_ref, d_ref = src[a], _win(land[a], kinds[a], me, sizes[a])
                else:
                    s_ref, d_ref = _win(src[a], kinds[a], 4 * px + 2 * py + pc, sizes[a]), land[a].at[me]
                pltpu.make_async_remote_copy(src_ref=s_ref, dst_ref=d_ref, send_sem=send_sems.at[a], recv_sem=recv_sems.at[a],
                                             device_id=(px, py, pc), device_id_type=pl.DeviceIdType.MESH).start()
        token[...] = jnp.zeros_like(token)

    hbm = [pltpu.with_memory_space_constraint(a, pltpu.HBM) for a in list(srcs) + lands]
    out = pl.pallas_call(
        body, name=name, in_specs=[_HBM] * (2 * n),
        out_shape=[pltpu.SemaphoreType.DMA((n,))] * 3 + [pltpu.HBM(a.shape, a.dtype) for a in hbm] + [SDS((8, 128), F32)],
        out_specs=[_SEM] * 3 + [_HBM] * (2 * n) + [pl.BlockSpec(memory_space=pltpu.VMEM)],
        input_output_aliases={i: 3 + i for i in range(2 * n)},
        compiler_params=pltpu.CompilerParams(has_side_effects=_DATAFLOW))(*hbm)
    return out[0:3], out[3:3 + n], out[3 + n:3 + 2 * n], out[-1]


def _local_copy(src, land, kind, size, gather, me, sem):
    if gather:
        return pltpu.make_async_copy(src, _win(land, kind, me, size), sem)
    return pltpu.make_async_copy(_win(src, kind, me, size), land.at[me], sem)


def _exchange_wait(name, started, which, kinds, sizes, gather, after):
    sems, srcs, lands, _ = started
    n = len(which)
    after = list(after) if isinstance(after, (list, tuple)) else [after]

    def body(*refs):
        src, land = refs[:n], refs[n:2 * n]
        send_ref, recv_ref, local_ref = refs[2 * n:2 * n + 3]
        me = _my_index()
        my_id = (lax.axis_index("x"), lax.axis_index("y"), lax.axis_index("c"))
        for i, a in enumerate(which):
            seven = _win7(land[i], kinds[a], sizes[a]) if gather else land[i].at[pl.ds(0, 7)]
            pltpu.make_async_remote_copy(src_ref=seven, dst_ref=seven, send_sem=send_ref.at[a], recv_sem=recv_ref.at[a],
                                         device_id=my_id, device_id_type=pl.DeviceIdType.MESH).wait()
            _local_copy(src[i], land[i], kinds[a], sizes[a], gather, me, local_ref.at[a]).wait()

    hbm = [srcs[a] for a in which] + [lands[a] for a in which]
    out = pl.pallas_call(
        body, name=name, in_specs=[_HBM] * (2 * n) + [_SEM] * 3 + [pl.BlockSpec(memory_space=pl.ANY)] * len(after),
        out_shape=[pltpu.HBM(a.shape, a.dtype) for a in hbm], out_specs=[_HBM] * (2 * n),
        input_output_aliases={i: i for i in range(2 * n)},
        compiler_params=pltpu.CompilerParams(has_side_effects=_DATAFLOW))(*hbm, *sems, *after)
    return out[n:]


def _gather_small(small):
    def body(in_ref, out_ref, send_sem, recv_sem, local_sem):
        me = _my_index()
        my_id = (lax.axis_index("x"), lax.axis_index("y"), lax.axis_index("c"))
        cp = pltpu.make_async_copy(in_ref, out_ref.at[me], local_sem)
        cp.start()
        for d in range(1, N_DEV):
            pltpu.make_async_remote_copy(src_ref=in_ref, dst_ref=out_ref.at[me], send_sem=send_sem, recv_sem=recv_sem,
                                         device_id=_peer(d), device_id_type=pl.DeviceIdType.MESH).start()
        seven = out_ref.at[pl.ds(0, 7)]
        pltpu.make_async_remote_copy(src_ref=seven, dst_ref=seven, send_sem=send_sem, recv_sem=recv_sem, device_id=my_id,
                                     device_id_type=pl.DeviceIdType.MESH).wait()
        cp.wait()

    any_spec = pl.BlockSpec(memory_space=pl.ANY)
    return pl.pallas_call(body, in_specs=[any_spec], out_specs=any_spec, out_shape=SDS((N_DEV,) + small.shape, F32),
                          scratch_shapes=[pltpu.SemaphoreType.DMA] * 3, name="gather_small")(small)


def _adamw(name, recv, w, m, v):
    rows, cols = w.shape
    tr = max(c for c in range(16, 257, 16) if rows % c == 0) if rows % 16 == 0 else rows

    def body(r_ref, w_ref, m_ref, v_ref, g_ref, d_ref, nm_ref, nv_ref):
        g = r_ref[0].astype(F32)
        for s in range(1, N_DEV):
            g = g + r_ref[s].astype(F32)
        nm = ADAM_B1 * m_ref[...] + (1.0 - ADAM_B1) * g
        nv = ADAM_B2 * v_ref[...] + (1.0 - ADAM_B2) * (g * g)
        m_hat = nm / (1.0 - ADAM_B1 ** ADAM_STEP)
        v_hat = nv / (1.0 - ADAM_B2 ** ADAM_STEP)
        g_ref[...] = g
        d_ref[...] = -ADAM_LR * (m_hat / (jnp.sqrt(v_hat) + ADAM_EPS) + ADAM_WD * w_ref[...])
        nm_ref[...] = nm
        nv_ref[...] = nv

    blk = _bs((tr, cols), lambda i: (i, 0))
    return pl.pallas_call(
        body, grid=(rows // tr,), in_specs=[_bs((N_DEV, tr, cols), lambda i: (0, i, 0)), blk, blk, blk],
        out_specs=[blk] * 4, out_shape=[SDS((rows, cols), F32)] * 4, compiler_params=_cp(1), name=name)(recv, w, m, v)


def _rms_fwd(name, x, g, tm):
    t, d = x.shape

    def body(x_ref, g_ref, n_ref):
        xv = x_ref[...]
        r = lax.rsqrt(jnp.mean(xv * xv, axis=-1, keepdims=True) + EPS)
        n_ref[...] = (xv * r * g_ref[...]).astype(BF16)

    return pl.pallas_call(body, grid=(t // tm,), in_specs=[_bs((tm, d), lambda i: (i, 0)), _bs((1, d), lambda i: (0, 0))],
                          out_specs=_bs((tm, d), lambda i: (i, 0)), out_shape=SDS((t, d), BF16), compiler_params=_cp(1),
                          name=name)(x, g)


def _accumulate_rows(ref, part):
    @pl.when(pl.program_id(0) == 0)
    def _():
        ref[...] = part

    @pl.when(pl.program_id(0) > 0)
    def _():
        ref[...] += part


def _rms_bwd_epilogue(prods, extra_refs, out_refs, scratch_refs):
    dyv = prods[0]
    for p in prods[1:]:
        dyv = dyv + p
    if len(extra_refs) > 3:
        dyv = dyv + extra_refs[3][...]
    xv = extra_refs[0][...]
    r = lax.rsqrt(jnp.mean(xv * xv, axis=-1, keepdims=True) + EPS)
    xh = xv * r
    dxh = dyv * extra_refs[1][...]
    dx = extra_refs[2][...] + r * (dxh - xh * jnp.mean(dxh * xh, axis=-1, keepdims=True))
    for o in out_refs[:-1]:
        o[...] = dx.astype(o.dtype)
    _accumulate_rows(out_refs[-1], jnp.sum(dyv * xh, axis=0, keepdims=True))


def _out_norm_epilogue(prods, extra_refs, out_refs, scratch_refs):
    h = prods[0] + extra_refs[0][...]
    r = lax.rsqrt(jnp.mean(h * h, axis=-1, keepdims=True) + EPS)
    out_refs[0][...] = h
    out_refs[1][...] = (h * r * extra_refs[1][...]).astype(BF16)


def _glu_merge_epilogue(prods, extra_refs, out_refs, scratch_refs):
    ya, yb, ad = prods
    ga, gs = extra_refs[0][...].astype(F32), extra_refs[1][...].astype(F32)
    m = _sigmoid(ga) * ad + _sigmoid(gs) * (ya * _sigmoid(yb))
    out_refs[0][...] = m.astype(BF16)
    for o, val in zip(out_refs[1:], (ya, yb, ad)):
        o[...] = val.astype(o.dtype)


def _merge_bwd_epilogue(prods, extra_refs, out_refs, scratch_refs):
    dmv = prods[0]
    d = dmv.shape[1]
    ga, gs = _sigmoid(extra_refs[0][...].astype(F32)), _sigmoid(extra_refs[1][...].astype(F32))
    adv, yav = extra_refs[2][...].astype(F32), extra_refs[3][...].astype(F32)
    sb = _sigmoid(extra_refs[4][...].astype(F32))
    out_refs[0][:, 0:d] = (dmv * adv * ga * (1.0 - ga)).astype(BF16)
    out_refs[0][:, d:2 * d] = (dmv * (yav * sb) * gs * (1.0 - gs)).astype(BF16)
    dad = (dmv * ga).astype(BF16)
    dsd = dmv * gs
    dya = (dsd * sb).astype(BF16)
    dyb = (dsd * yav * sb * (1.0 - sb)).astype(BF16)
    out_refs[1][...], out_refs[2][...], out_refs[3][...] = dad, dya, dyb
    nt = _DNUMS["nt"]
    out_refs[4][...] = (lax.dot_general(dya, extra_refs[5][...], nt, preferred_element_type=F32)
                        + lax.dot_general(dyb, extra_refs[6][...], nt, preferred_element_type=F32))
    out_refs[5][...] = lax.dot_general(dad, extra_refs[7][...], nt, preferred_element_type=F32)


def _swiglu_epilogue(prods, extra_refs, out_refs, scratch_refs):
    gv, uv = prods
    out_refs[0][...] = (gv * _sigmoid(gv) * uv).astype(BF16)
    out_refs[1][...] = gv.astype(out_refs[1].dtype)
    out_refs[2][...] = uv.astype(out_refs[2].dtype)


def _swiglu_bwd_epilogue(prods, extra_refs, out_refs, scratch_refs):
    dav = prods[0]
    gv, uv = extra_refs[0][...].astype(F32), extra_refs[1][...].astype(F32)
    sg = _sigmoid(gv)
    out_refs[0][...] = (dav * uv * sg * (1.0 + gv * (1.0 - sg))).astype(BF16)
    out_refs[1][...] = (dav * gv * sg).astype(BF16)


def _head_epilogue(n_tiles):
    def epilogue(prods, extra_refs, out_refs, scratch_refs):
        h2 = prods[0] + extra_refs[0][...]
        h2_bf = h2.astype(BF16)
        out_refs[6][...] = h2_bf
        pgv = jnp.dot(h2_bf, extra_refs[3][...], preferred_element_type=F32)
        ppv = prods[1]
        d = pgv.shape[1]
        lacc = scratch_refs[0]
        sg = _sigmoid(pgv)
        h3 = h2 + sg * ppv
        r = lax.rsqrt(jnp.mean(h3 * h3, axis=-1, keepdims=True) + EPS)
        xh = h3 * r
        gv = extra_refs[1][...]
        diff = xh * gv - extra_refs[2][...]
        dout = diff * (1.0 / d)
        dxh = dout * gv
        dh3 = r * (dxh - xh * jnp.mean(dxh * xh, axis=-1, keepdims=True))
        dpg = (dh3 * ppv * sg * (1.0 - sg)).astype(BF16)
        dh2 = dh3 + lax.dot_general(dpg, extra_refs[3][...], _DNUMS["nt"], preferred_element_type=F32)
        out_refs[2][...] = dh2
        out_refs[3][...] = dh2.astype(BF16)
        out_refs[4][...] = (dh3 * sg).astype(BF16)
        out_refs[5][...] = dpg
        _accumulate_rows(out_refs[1], jnp.sum(dout * xh, axis=0, keepdims=True))
        _accumulate_rows(lacc, jnp.sum(diff * diff, axis=0, keepdims=True))

        @pl.when(pl.program_id(0) == n_tiles - 1)
        def _():
            out_refs[0][...] = (0.5 / d) * jnp.sum(lacc[...], axis=-1, keepdims=True)

    return epilogue


def _strided(r, n, d):
    return pl.ds(r, n, stride=d) if d > 1 else pl.ds(0, n)


def _rope_tables(pos, invf, tm):
    t = pos.shape[0]

    def body(pos_ref, invf_ref, c_ref, s_ref):
        ang = pos_ref[...].astype(F32) * invf_ref[...]
        lane = lax.broadcasted_iota(jnp.int32, ang.shape, 1)
        sn = jnp.sin(ang)
        c_ref[...] = jnp.where(lane < ROPE_DIM, jnp.cos(ang), 1.0)
        s_ref[...] = jnp.where(lane < ROPE_HALF, -sn, jnp.where(lane < ROPE_DIM, sn, 0.0))

    tab = _bs((tm, HEAD_DIM), lambda i: (i, 0))
    return pl.pallas_call(body, grid=(t // tm,), in_specs=[_bs((tm, 1), lambda i: (i, 0)), _bs((1, HEAD_DIM), lambda i: (0, 0))],
                          out_specs=[tab, tab], out_shape=[SDS((t, HEAD_DIM), F32)] * 2, compiler_params=_cp(1),
                          name="rope_tables")(pos, invf)


def _rope_partner(xv, first_half):
    return jnp.where(first_half, pltpu.roll(xv, HEAD_DIM - ROPE_HALF, 1), pltpu.roll(xv, ROPE_HALF, 1))


def _rope_dilate_epilogue(tm):
    def epilogue(prods, extra_refs, out_refs, scratch_refs):
        zv = prods[0]
        c_s, s_s = extra_refs
        rot = scratch_refs[0]
        c = pl.program_id(1)

        @pl.when(c < 2)
        def _():
            cc, ss = c_s[...], s_s[...]
            first_half = lax.broadcasted_iota(jnp.int32, cc.shape, 1) < ROPE_HALF
            for h in range(QK_W // HEAD_DIM):
                xv = zv[:, h * HEAD_DIM:(h + 1) * HEAD_DIM]
                rot[h] = xv * cc + _rope_partner(xv, first_half) * ss

        @pl.when(c == 2)
        def _():
            for h in range(QK_W // HEAD_DIM):
                rot[h] = zv[:, h * HEAD_DIM:(h + 1) * HEAD_DIM]

        for g, (d, o_ref) in enumerate(zip(DILATIONS, out_refs)):
            n = tm // d
            for r in range(d):
                for hh in range(HEADS_PER_GROUP):
                    oc = r * GROUP_W + hh * HEAD_DIM
                    o_ref[:, oc:oc + HEAD_DIM] = rot[g * HEADS_PER_GROUP + hh, _strided(r, n, d), :].astype(BF16)

    return epilogue


def _band_masks(first_tile):
    qi = lax.broadcasted_iota(jnp.int32, (BLK, 2 * BLK), 0)
    kj = lax.broadcasted_iota(jnp.int32, (BLK, 2 * BLK), 1)
    band = (kj >= qi) & (kj <= qi + BLK)
    return band, band & ((kj >= BLK) | jnp.logical_not(first_tile))


def _attn_fwd(qkv, d, qt):
    ell = qkv.shape[1]
    nsub = qt // BLK
    scale = 1.0 / math.sqrt(HEAD_DIM)

    def body(q_ref, kc_ref, kp_ref, vc_ref, vp_ref, o_ref, lse_ref, kcat, vcat):
        nb = pl.program_id(1)
        kcat[0:BLK, :] = kp_ref[...]
        kcat[BLK:, :] = kc_ref[...]
        vcat[0:BLK, :] = vp_ref[...]
        vcat[BLK:, :] = vc_ref[...]
        lane = lax.broadcasted_iota(jnp.int32, (BLK, HEAD_DIM), 1)
        band, band_first = _band_masks(nb == 0)
        for b in range(nsub):
            valid = band_first if b == 0 else band
            lse_t = jnp.zeros((BLK, HEAD_DIM), F32)
            for hh in range(HEADS_PER_GROUP):
                cs = slice(hh * HEAD_DIM, (hh + 1) * HEAD_DIM)
                qb = q_ref[b * BLK:(b + 1) * BLK, cs]
                kk = kcat[b * BLK:(b + 2) * BLK, cs]
                vv = vcat[b * BLK:(b + 2) * BLK, cs]
                s = lax.dot_general(qb, kk, _DNUMS["nt"], preferred_element_type=F32) * scale
                s = jnp.where(valid, s, NEG)
                mx = jnp.max(s, axis=-1, keepdims=True)
                p = jnp.exp(s - mx)
                den = jnp.sum(p, axis=-1, keepdims=True)
                o = jnp.dot(p.astype(BF16), vv, preferred_element_type=F32) / den
                o_ref[b * BLK:(b + 1) * BLK, cs] = o
                lse_t = jnp.where(lane == hh, mx + jnp.log(den), lse_t)
            lse_ref[b * BLK:(b + 1) * BLK, :] = lse_t

    cur = lambda c: _bs((None, qt, GROUP_W), lambda r, nb: (c, nb, r))
    prev = lambda c: _bs((None, BLK, GROUP_W), lambda r, nb: (c, jnp.maximum(nb * nsub - 1, 0), r))
    return pl.pallas_call(
        body, grid=(d, ell // qt), in_specs=[cur(0), cur(1), prev(1), cur(2), prev(2)],
        out_specs=[_bs((qt, GROUP_W), lambda r, nb: (nb, r)), _bs((None, qt, HEAD_DIM), lambda r, nb: (r, nb, 0))],
        out_shape=[SDS((ell, d * GROUP_W), F32), SDS((d, ell, HEAD_DIM), F32)],
        scratch_shapes=[pltpu.VMEM((qt + BLK, GROUP_W), BF16)] * 2, compiler_params=_cp(2), name=f"attn_fwd_d{d}")(
            qkv, qkv, qkv, qkv, qkv)


def _attn_merge(outs, lses, tm):
    t = outs[0].shape[0]

    def body(o0, o1, o2, l0, l1, l2, attn_ref, attn_bf_ref, t0, t1, t2, so, sl, lt_s):
        for g, (d, o_ref, l_ref) in enumerate(zip(DILATIONS, (o0, o1, o2), (l0, l1, l2))):
            n = tm // d
            for r in range(d):
                rows = _strided(r, n, d)
                for hh in range(HEADS_PER_GROUP):
                    oc = r * GROUP_W + hh * HEAD_DIM
                    so[g * HEADS_PER_GROUP + hh, rows, :] = o_ref[:, oc:oc + HEAD_DIM]
                sl[g, rows, :] = l_ref[r]
        ls = [sl[g] for g in range(N_GROUPS)]
        mx = jnp.maximum(jnp.maximum(ls[0], ls[1]), ls[2])
        es = [jnp.exp(l - mx) for l in ls]
        den = es[0] + es[1] + es[2]
        ws = [e / den for e in es]
        lt_s[...] = mx + jnp.log(den)
        for hh in range(HEADS_PER_GROUP):
            cs = slice(hh * HEAD_DIM, (hh + 1) * HEAD_DIM)
            a = ws[0][:, hh:hh + 1] * so[hh]
            for g in range(1, N_GROUPS):
                a = a + ws[g][:, hh:hh + 1] * so[g * HEADS_PER_GROUP + hh]
            attn_ref[:, cs] = a
            attn_bf_ref[:, cs] = a.astype(BF16)
        for d, t_ref in zip(DILATIONS, (t0, t1, t2)):
            n = tm // d
            for r in range(d):
                t_ref[r] = lt_s[_strided(r, n, d), :]

    dil = lambda d: _bs((tm // d, d * GROUP_W), lambda i: (i, 0))
    lsp = lambda d: _bs((d, tm // d, HEAD_DIM), lambda i: (0, i, 0))
    row = _bs((tm, GROUP_W), lambda i: (i, 0))
    return pl.pallas_call(
        body, grid=(t // tm,),
        in_specs=[dil(d) for d in DILATIONS] + [lsp(d) for d in DILATIONS],
        out_specs=[row, row] + [lsp(d) for d in DILATIONS],
        out_shape=[SDS((t, GROUP_W), F32), SDS((t, GROUP_W), BF16)] + [SDS(l.shape, F32) for l in lses],
        scratch_shapes=[pltpu.VMEM((N_GROUPS * HEADS_PER_GROUP, tm, HEAD_DIM), F32), pltpu.VMEM((N_GROUPS, tm, HEAD_DIM), F32),
                        pltpu.VMEM((tm, HEAD_DIM), F32)],
        compiler_params=_cp(1), name="attn_merge")(*outs, *lses)


def _attn_bwd_pre(d_attn, attn, tm):
    t = attn.shape[0]

    def body(da_ref, a_ref, g0, g1, g2, e0, e1, e2, dl_s, da_s):
        lane = lax.broadcasted_iota(jnp.int32, (tm, HEAD_DIM), 1)
        dl = jnp.zeros((tm, HEAD_DIM), F32)
        for hh in range(HEADS_PER_GROUP):
            cs = slice(hh * HEAD_DIM, (hh + 1) * HEAD_DIM)
            dav = da_ref[:, cs]
            da_s[hh] = dav
            dl = jnp.where(lane == hh, jnp.sum(dav * a_ref[:, cs], axis=-1, keepdims=True), dl)
        dl_s[...] = dl
        for d, g_ref, e_ref in zip(DILATIONS, (g0, g1, g2), (e0, e1, e2)):
            n = tm // d
            for r in range(d):
                rows = _strided(r, n, d)
                for hh in range(HEADS_PER_GROUP):
                    oc = r * GROUP_W + hh * HEAD_DIM
                    g_ref[:, oc:oc + HEAD_DIM] = da_s[hh, rows, :].astype(BF16)
                e_ref[r] = dl_s[rows, :]

    row = _bs((tm, GROUP_W), lambda i: (i, 0))
    return pl.pallas_call(
        body, grid=(t // tm,), in_specs=[row, row],
        out_specs=[_bs((tm // d, d * GROUP_W), lambda i: (i, 0)) for d in DILATIONS]
        + [_bs((d, tm // d, HEAD_DIM), lambda i: (0, i, 0)) for d in DILATIONS],
        out_shape=[SDS((t // d, d * GROUP_W), BF16) for d in DILATIONS]
        + [SDS((d, t // d, HEAD_DIM), F32) for d in DILATIONS],
        scratch_shapes=[pltpu.VMEM((tm, HEAD_DIM), F32), pltpu.VMEM((HEADS_PER_GROUP, tm, HEAD_DIM), F32)],
        compiler_params=_cp(1), name="attn_bwd_pre")(d_attn, attn)


def _attn_bwd(qkv, d_a, lt, delta, d, qt):
    ell = qkv.shape[1]
    nsub = qt // BLK
    ntile = ell // qt
    nblk = ell // BLK
    scale = 1.0 / math.sqrt(HEAD_DIM)

    def body(q_ref, qn_ref, kc_ref, kp_ref, vc_ref, vp_ref, da_ref, dan_ref, lt_ref, ltn_ref, dl_ref, dln_ref, o_ref,
             kcat, vcat, dk_acc, dv_acc):
        nb = pl.program_id(1)
        kcat[0:BLK, :] = kp_ref[...]
        kcat[BLK:, :] = kc_ref[...]
        vcat[0:BLK, :] = vp_ref[...]
        vcat[BLK:, :] = vc_ref[...]
        qi = lax.broadcasted_iota(jnp.int32, (BLK, BLK), 0)
        kj = lax.broadcasted_iota(jnp.int32, (BLK, BLK), 1)
        valid_next = (kj >= qi) & (nb < ntile - 1)
        band, band_first = _band_masks(nb == 0)
        for hh in range(HEADS_PER_GROUP):
            cs = slice(hh * HEAD_DIM, (hh + 1) * HEAD_DIM)
            dk_acc[...] = jnp.zeros_like(dk_acc)
            dv_acc[...] = jnp.zeros_like(dv_acc)
            for b in range(nsub):
                rs = slice(b * BLK, (b + 1) * BLK)
                ks = slice(b * BLK, (b + 2) * BLK)
                valid = band_first if b == 0 else band
                qb, kk, vv, dab = q_ref[rs, cs], kcat[ks, cs], vcat[ks, cs], da_ref[rs, cs]
                s = lax.dot_general(qb, kk, _DNUMS["nt"], preferred_element_type=F32) * scale
                p = jnp.where(valid, jnp.exp(s - lt_ref[rs, hh:hh + 1]), 0.0)
                dp = lax.dot_general(dab, vv, _DNUMS["nt"], preferred_element_type=F32)
                ds = (p * (dp - dl_ref[rs, hh:hh + 1])).astype(BF16)
                o_ref[0, rs, cs] = jnp.dot(ds, kk, preferred_element_type=F32) * scale
                dk_acc[ks, :] += lax.dot_general(ds, qb, _DNUMS["tn"], preferred_element_type=F32) * scale
                dv_acc[ks, :] += lax.dot_general(p.astype(BF16), dab, _DNUMS["tn"], preferred_element_type=F32)
            ks = slice(nsub * BLK, (nsub + 1) * BLK)
            qn, kl, vl, dan = qn_ref[:, cs], kcat[ks, cs], vcat[ks, cs], dan_ref[:, cs]
            s = lax.dot_general(qn, kl, _DNUMS["nt"], preferred_element_type=F32) * scale
            p = jnp.where(valid_next, jnp.exp(s - ltn_ref[:, hh:hh + 1]), 0.0)
            dp = lax.dot_general(dan, vl, _DNUMS["nt"], preferred_element_type=F32)
            ds = (p * (dp - dln_ref[:, hh:hh + 1])).astype(BF16)
            dk_acc[ks, :] += lax.dot_general(ds, qn, _DNUMS["tn"], preferred_element_type=F32) * scale
            dv_acc[ks, :] += lax.dot_general(p.astype(BF16), dan, _DNUMS["tn"], preferred_element_type=F32)
            o_ref[1, :, cs] = dk_acc[BLK:, :]
            o_ref[2, :, cs] = dv_acc[BLK:, :]

    nxt = lambda nb: jnp.minimum((nb + 1) * nsub, nblk - 1)
    prv = lambda nb: jnp.maximum(nb * nsub - 1, 0)
    cur3 = lambda c: _bs((None, qt, GROUP_W), lambda r, nb: (c, nb, r))
    in_specs = [
        cur3(0), _bs((None, BLK, GROUP_W), lambda r, nb: (0, nxt(nb), r)),
        cur3(1), _bs((None, BLK, GROUP_W), lambda r, nb: (1, prv(nb), r)),
        cur3(2), _bs((None, BLK, GROUP_W), lambda r, nb: (2, prv(nb), r)),
        _bs((qt, GROUP_W), lambda r, nb: (nb, r)), _bs((BLK, GROUP_W), lambda r, nb: (nxt(nb), r)),
        _bs((None, qt, HEAD_DIM), lambda r, nb: (r, nb, 0)), _bs((None, BLK, HEAD_DIM), lambda r, nb: (r, nxt(nb), 0)),
        _bs((None, qt, HEAD_DIM), lambda r, nb: (r, nb, 0)), _bs((None, BLK, HEAD_DIM), lambda r, nb: (r, nxt(nb), 0)),
    ]
    return pl.pallas_call(
        body, grid=(d, ntile), in_specs=in_specs, out_specs=_bs((3, qt, GROUP_W), lambda r, nb: (0, nb, r)),
        out_shape=SDS((3, ell, d * GROUP_W), F32),
        scratch_shapes=[pltpu.VMEM((qt + BLK, GROUP_W), BF16)] * 2 + [pltpu.VMEM((qt + BLK, HEAD_DIM), F32)] * 2,
        compiler_params=_cp(2), name=f"attn_bwd_d{d}")(qkv, qkv, qkv, qkv, qkv, qkv, d_a, d_a, lt, lt, delta, delta)


def _undilate_rope_bwd(dqkvs, rope_c, rope_s, tm):
    t = rope_c.shape[0]

    def body(g0, g1, g2, c_s, s_s, o_ref, nat):
        c = pl.program_id(1)
        for g, (d, g_ref) in enumerate(zip(DILATIONS, (g0, g1, g2))):
            n = tm // d
            for r in range(d):
                for hh in range(HEADS_PER_GROUP):
                    oc = r * GROUP_W + hh * HEAD_DIM
                    nat[g * HEADS_PER_GROUP + hh, _strided(r, n, d), :] = g_ref[:, oc:oc + HEAD_DIM]

        @pl.when(c < 2)
        def _():
            cc, ss = c_s[...], s_s[...]
            first_half = lax.broadcasted_iota(jnp.int32, cc.shape, 1) < ROPE_HALF
            for h in range(QK_W // HEAD_DIM):
                xv = nat[h]
                y = xv * cc - _rope_partner(xv, first_half) * ss
                o_ref[:, h * HEAD_DIM:(h + 1) * HEAD_DIM] = y.astype(BF16)

        @pl.when(c == 2)
        def _():
            for h in range(QK_W // HEAD_DIM):
                o_ref[:, h * HEAD_DIM:(h + 1) * HEAD_DIM] = nat[h].astype(BF16)

    return pl.pallas_call(
        body, grid=(t // tm, 3),
        in_specs=[_bs((None, tm // d, d * GROUP_W), lambda i, c: (c, i, 0)) for d in DILATIONS]
        + [_bs((tm, HEAD_DIM), lambda i, c: (i, 0))] * 2,
        out_specs=_bs((tm, QK_W), lambda i, c: (i, c)), out_shape=SDS((t, 3 * QK_W), BF16),
        scratch_shapes=[pltpu.VMEM((QK_W // HEAD_DIM, tm, HEAD_DIM), F32)],
        compiler_params=_cp(2), name="undilate_rope_bwd")(*dqkvs, rope_c, rope_s)


def _cmul(ar, ai, br, bi):
    return ar * br - ai * bi, ar * bi + ai * br


def _ssm_disc(a_re, a_im, log_dt, nsq):
    def body(lr_ref, li_ref, ldt_ref, br_ref, bi_ref, zr_ref, zi_ref, pr_ref, pi_ref):
        lr, li = lr_ref[...], li_ref[...]
        dt = jnp.exp(ldt_ref[...])
        mag = jnp.exp(lr * dt)
        bar_re, bar_im = mag * jnp.cos(li * dt), mag * jnp.sin(li * dt)
        nr, ni = bar_re - 1.0, bar_im
        den = lr * lr + li * li
        br_ref[...], bi_ref[...] = bar_re, bar_im
        zr_ref[...] = (nr * lr + ni * li) / den
        zi_ref[...] = (ni * lr - nr * li) / den
        pr, pi = bar_re, bar_im
        for _ in range(nsq):
            pr, pi = _cmul(pr, pi, pr, pi)
        pr_ref[...], pi_ref[...] = pr, pi

    return pl.pallas_call(body, out_shape=[SDS(a_re.shape, F32)] * 6, name="ssm_discretise")(a_re, a_im, log_dt)


def _ssm_scale_b(z_re, z_im, b_re, b_im):
    def body(zr_ref, zi_ref, br_ref, bi_ref, or_ref, oi_ref):
        zr, zi, br, bi = zr_ref[...], zi_ref[...], br_ref[...], bi_ref[...]
        or_ref[...] = zr * br - zi * bi
        oi_ref[...] = zr * bi + zi * br

    return pl.pallas_call(body, out_shape=[SDS(b_re.shape, F32)] * 2, name="ssm_scale_b")(z_re, z_im, b_re, b_im)


def _ssm_scale_b_bwd(z_re, z_im, b_re, b_im, g_re, g_im):
    def body(zr_ref, zi_ref, br_ref, bi_ref, gr_ref, gi_ref, dbr_ref, dbi_ref, dzr_ref, dzi_ref):
        zr, zi, br, bi, gr, gi = zr_ref[...], zi_ref[...], br_ref[...], bi_ref[...], gr_ref[...], gi_ref[...]
        dbr_ref[...] = zr * gr + zi * gi
        dbi_ref[...] = zr * gi - zi * gr
        dzr_ref[...] = jnp.sum(br * gr + bi * gi, axis=-1, keepdims=True)
        dzi_ref[...] = jnp.sum(br * gi - bi * gr, axis=-1, keepdims=True)

    return pl.pallas_call(body, out_shape=[SDS(b_re.shape, F32)] * 2 + [SDS(z_re.shape, F32)] * 2,
                          name="ssm_scale_b_bwd")(z_re, z_im, b_re, b_im, g_re, g_im)


def _ssm_disc_bwd(a_re, a_im, log_dt, gb_re, gb_im, gz_re, gz_im):
    def body(lr_ref, li_ref, ldt_ref, gbr_ref, gbi_ref, gzr_ref, gzi_ref, dar_ref, dai_ref, dldt_ref):
        lr, li = lr_ref[...], li_ref[...]
        dt = jnp.exp(ldt_ref[...])
        mag = jnp.exp(lr * dt)
        bar_re, bar_im = mag * jnp.cos(li * dt), mag * jnp.sin(li * dt)
        nr, ni = bar_re - 1.0, bar_im
        den = lr * lr + li * li
        zr, zi = (nr * lr + ni * li) / den, (ni * lr - nr * li) / den
        gzr, gzi = gzr_ref[...], gzi_ref[...]
        gbr = gbr_ref[...] + (lr * gzr - li * gzi) / den
        gbi = gbi_ref[...] + (lr * gzi + li * gzr) / den
        qr, qi = (zr * lr + zi * li) / den, (zi * lr - zr * li) / den
        dar_ref[...] = dt * (bar_re * gbr + bar_im * gbi) - qr * gzr - qi * gzi
        dai_ref[...] = dt * (bar_re * gbi - bar_im * gbr) - qr * gzi + qi * gzr
        wr, wi = lr * bar_re - li * bar_im, lr * bar_im + li * bar_re
        dldt_ref[...] = dt * jnp.sum(wr * gbr + wi * gbi, axis=-1, keepdims=True)

    return pl.pallas_call(body, out_shape=[SDS(a_re.shape, F32)] * 2 + [SDS(log_dt.shape, F32)],
                          name="ssm_discretise_bwd")(a_re, a_im, log_dt, gb_re, gb_im, gz_re, gz_im)


def _interleave_epilogue(prods, extra_refs, out_refs, scratch_refs):
    uv = prods[0]
    tmp = scratch_refs[0]
    n = uv.shape[0] // N_DEV
    for b in range(SSM_W // BLK):
        cs = slice(b * BLK, (b + 1) * BLK)
        for j in range(N_DEV):
            tmp[b, pl.ds(j, n, stride=N_DEV), :] = uv[j * n:(j + 1) * n, cs]
        out_refs[0][:, cs] = tmp[b]
        out_refs[1][:, cs] = tmp[b].astype(BF16)


def _drive(src_ref, mat_ref, dst, mode):
    for kn in range(2 * SSM_NB):
        n = kn % SSM_NB
        a = src_ref[:, n * BLK:(n + 1) * BLK]
        dst[:, kn * 512:(kn + 1) * 512] = lax.dot_general(a, mat_ref[kn], _DNUMS[mode], preferred_element_type=F32)


def _scan_chunk(src, lam_ref, carry, *, reverse, store=None, h_ref=None, acc=None):
    steps = src.shape[0] // 8
    for c in range(NSTATE // SCAN_LANES):
        re = slice(c * SCAN_LANES, (c + 1) * SCAN_LANES)
        im = slice(NSTATE + c * SCAN_LANES, NSTATE + (c + 1) * SCAN_LANES)
        ar, ai = lam_ref[:, re], lam_ref[:, im]

        def step(s, val):
            i = (steps - 1 - s) if reverse else s
            rows = pl.ds(pl.multiple_of(i * 8, 8), 8)
            if acc is not None:
                hr, hi, dr, di = val
                pr, pi = h_ref[rows, re], h_ref[rows, im]
                dr = dr + hr * pr + hi * pi
                di = di + hi * pr - hr * pi
            else:
                hr, hi = val
            nr = ar * hr - ai * hi + src[rows, re]
            ni = ar * hi + ai * hr + src[rows, im]
            if store is not None:
                store[rows, re] = nr
                store[rows, im] = ni
            return (nr, ni, dr, di) if acc is not None else (nr, ni)

        init = (carry[:, re], carry[:, im])
        if acc is not None:
            init = init + (acc[:, re], acc[:, im])
        out = lax.fori_loop(0, steps, step, init, unroll=4)
        carry[:, re], carry[:, im] = out[0], out[1]
        if acc is not None:
            acc[:, re], acc[:, im] = out[2], out[3]


def _segment_carries(e_ref, pw_ref, out_ref, reverse):
    pr, pi = pw_ref[:, 0:NSTATE], pw_ref[:, NSTATE:]
    hr = jnp.zeros((1, NSTATE), F32)
    hi = jnp.zeros((1, NSTATE), F32)
    order = range(N_DEV - 1, -1, -1) if reverse else range(N_DEV)
    for j in order:
        out_ref[j:j + 1, 0:NSTATE] = hr
        out_ref[j:j + 1, NSTATE:] = hi
        tr, ti = _cmul(pr, pi, hr, hi)
        hr, hi = e_ref[j:j + 1, 0:NSTATE] + tr, e_ref[j:j + 1, NSTATE:] + ti


def _ssm_carries(name, src, mat, mode, lam8, pw, reverse):
    t = src.shape[0]
    nchunk = t // SCAN_ROWS

    def body(src_ref, mat_ref, lam_ref, pw_ref, out_ref, drive, carry):
        c = pl.program_id(0)

        @pl.when(c == 0)
        def _():
            carry[...] = jnp.zeros_like(carry)

        _drive(src_ref, mat_ref, drive, mode)
        _scan_chunk(drive, lam_ref, carry, reverse=reverse)

        @pl.when(c == nchunk - 1)
        def _():
            _segment_carries(carry, pw_ref, out_ref, reverse)

    blk = (lambda c: (nchunk - 1 - c, 0)) if reverse else (lambda c: (c, 0))
    return pl.pallas_call(
        body, grid=(nchunk,),
        in_specs=[_bs((SCAN_ROWS, SSM_W), blk), _bs(mat.shape, lambda c: (0, 0, 0)), _bs((8, 2 * NSTATE), lambda c: (0, 0)),
                  _bs((1, 2 * NSTATE), lambda c: (0, 0))],
        out_specs=_bs((8, 2 * NSTATE), lambda c: (0, 0)), out_shape=SDS((8, 2 * NSTATE), F32),
        scratch_shapes=[pltpu.VMEM((SCAN_ROWS, 2 * NSTATE), F32), pltpu.VMEM((8, 2 * NSTATE), F32)],
        compiler_params=_cp(1), name=name)(src, mat, lam8, pw)


def _ssm_fwd(u_bf, u, d_skip, bd, cd, lam8, start):
    t = u_bf.shape[0]
    nchunk = t // SCAN_ROWS
    per_seg = SCAN_ROWS // N_DEV

    def body(ub_ref, u_ref, d_ref, bd_ref, cd_ref, lam_ref, start_ref, h_ref, ys_ref, yg_ref, drive, carry, tmp):
        @pl.when(pl.program_id(0) == 0)
        def _():
            carry[...] = start_ref[...]

        _drive(ub_ref, bd_ref, drive, "nn")
        _scan_chunk(drive, lam_ref, carry, reverse=False, store=h_ref)
        for n in range(SSM_NB):
            cs = slice(n * BLK, (n + 1) * BLK)
            hr = h_ref[:, n * 512:(n + 1) * 512].astype(BF16)
            hi = h_ref[:, NSTATE + n * 512:NSTATE + (n + 1) * 512].astype(BF16)
            ys = (jnp.dot(hr, cd_ref[n], preferred_element_type=F32) + jnp.dot(hi, cd_ref[SSM_NB + n], preferred_element_type=F32)
                  + d_ref[:, cs] * u_ref[:, cs])
            ys_ref[:, cs] = ys
            tmp[n] = _gelu_parts(ys)[0]
            for j in range(N_DEV):
                yg_ref[j, :, cs] = tmp[n, pl.ds(j, per_seg, stride=N_DEV), :].astype(BF16)

    row = _bs((SCAN_ROWS, SSM_W), lambda c: (c, 0))
    h, ys, yg = pl.pallas_call(
        body, grid=(nchunk,),
        in_specs=[row, row, _bs((1, SSM_W), lambda c: (0, 0)), _bs(bd.shape, lambda c: (0, 0, 0)), _bs(cd.shape, lambda c: (0, 0, 0)),
                  _bs((8, 2 * NSTATE), lambda c: (0, 0)), _bs((8, 2 * NSTATE), lambda c: (0, 0))],
        out_specs=[_bs((SCAN_ROWS, 2 * NSTATE), lambda c: (c, 0)), row, _bs((N_DEV, per_seg, SSM_W), lambda c: (0, c, 0))],
        out_shape=[SDS((t, 2 * NSTATE), F32), SDS((t, SSM_W), F32), SDS((N_DEV, t // N_DEV, SSM_W), BF16)],
        scratch_shapes=[pltpu.VMEM((SCAN_ROWS, 2 * NSTATE), F32), pltpu.VMEM((8, 2 * NSTATE), F32),
                        pltpu.VMEM((SSM_NB, SCAN_ROWS, BLK), F32)],
        compiler_params=_cp(1), name="ssm_scan_fwd")(u_bf, u, d_skip, bd, cd, lam8, start)
    return h, ys, yg.reshape(t, SSM_W)


def _ssm_bwd(dys_bf, dys, d_skip, u_bf, h, bd, cd, lamc8, start):
    t = u_bf.shape[0]
    nchunk = t // SCAN_ROWS
    per_seg = SCAN_ROWS // N_DEV

    def body(dys_ref, dysf_ref, d_ref, u_ref, h_ref, bd_ref, cd_ref, lam_ref, start_ref, du_ref, dlam_ref, dbd_ref, dcd_ref,
             drive, adj, carry, tmp):
        c = pl.program_id(0)

        @pl.when(c == 0)
        def _():
            carry[...] = start_ref[...]
            dlam_ref[...] = jnp.zeros_like(dlam_ref)
            dbd_ref[...] = jnp.zeros_like(dbd_ref)
            dcd_ref[...] = jnp.zeros_like(dcd_ref)

        _drive(dys_ref, cd_ref, drive, "nt")
        _scan_chunk(drive, lam_ref, carry, reverse=True, store=adj, h_ref=h_ref, acc=dlam_ref)
        for n in range(SSM_NB):
            cs = slice(n * BLK, (n + 1) * BLK)
            acc = None
            for k in range(2):
                kn = k * SSM_NB + n
                ss = slice(kn * 512, (kn + 1) * 512)
                lam_b = adj[:, ss].astype(BF16)
                part = lax.dot_general(lam_b, bd_ref[kn], _DNUMS["nt"], preferred_element_type=F32)
                acc = part if acc is None else acc + part
                dbd_ref[kn] += lax.dot_general(u_ref[:, cs], lam_b, _DNUMS["tn"], preferred_element_type=F32)
                dcd_ref[kn] += lax.dot_general(h_ref[:, ss].astype(BF16), dys_ref[:, cs], _DNUMS["tn"],
                                               preferred_element_type=F32)
            tmp[n] = acc + d_ref[:, cs] * dysf_ref[:, cs]
            for j in range(N_DEV):
                du_ref[j, :, cs] = tmp[n, pl.ds(j, per_seg, stride=N_DEV), :].astype(BF16)

    rev = lambda c: (nchunk - 1 - c, 0)
    const2 = lambda c: (0, 0)
    const3 = lambda c: (0, 0, 0)
    row = _bs((SCAN_ROWS, SSM_W), rev)
    du, dlam, dbd, dcd = pl.pallas_call(
        body, grid=(nchunk,),
        in_specs=[row, row, _bs((1, SSM_W), const2), row, _bs((SCAN_ROWS, 2 * NSTATE), rev),
                  _bs(bd.shape, const3), _bs(cd.shape, const3), _bs((8, 2 * NSTATE), const2), _bs((8, 2 * NSTATE), const2)],
        out_specs=[_bs((N_DEV, per_seg, SSM_W), lambda c: (0, nchunk - 1 - c, 0)), _bs((8, 2 * NSTATE), const2),
                   _bs(bd.shape, const3), _bs(cd.shape, const3)],
        out_shape=[SDS((N_DEV, t // N_DEV, SSM_W), BF16), SDS((8, 2 * NSTATE), F32), SDS(bd.shape, F32), SDS(cd.shape, F32)],
        scratch_shapes=[pltpu.VMEM((SCAN_ROWS, 2 * NSTATE), F32), pltpu.VMEM((SCAN_ROWS, 2 * NSTATE), F32),
                        pltpu.VMEM((8, 2 * NSTATE), F32), pltpu.VMEM((SSM_NB, SCAN_ROWS, BLK), F32)],
        compiler_params=_cp(1), name="ssm_scan_bwd")(dys_bf, dys, d_skip, u_bf, h, bd, cd, lamc8, start)
    return du.reshape(t, SSM_W), dlam, dbd, dcd


def _gelu_parts(x):
    c0 = math.sqrt(2.0 / math.pi)
    inner = c0 * (x + 0.044715 * x * x * x)
    th = jnp.tanh(inner)
    val = 0.5 * x * (1.0 + th)
    grad = 0.5 * (1.0 + th) + 0.5 * x * (1.0 - th * th) * c0 * (1.0 + 3.0 * 0.044715 * x * x)
    return val, grad


def _ssm_carries_bwd(d_yg, ys, u, cd, lamc8, pwc):
    t = u.shape[0]
    nchunk = t // SCAN_ROWS
    per_seg = SCAN_ROWS // N_DEV

    def body(dg_ref, ys_ref, u_ref, cd_ref, lam_ref, pw_ref, out_ref, dys_ref, dysb_ref, dd_ref, drive, carry, tmp):
        c = pl.program_id(0)

        @pl.when(c == 0)
        def _():
            carry[...] = jnp.zeros_like(carry)

        for n in range(SSM_W // BLK):
            for j in range(N_DEV):
                tmp[n, pl.ds(j, per_seg, stride=N_DEV), :] = dg_ref[j, :, n * BLK:(n + 1) * BLK]
        dyg = jnp.concatenate([tmp[n] for n in range(SSM_W // BLK)], axis=1)
        dys = dyg * _gelu_parts(ys_ref[...])[1]
        dys_ref[...] = dys
        dysb_ref[...] = dys.astype(BF16)
        _accumulate_rows(dd_ref, jnp.sum(dys * u_ref[...], axis=0, keepdims=True))
        _drive(dysb_ref, cd_ref, drive, "nt")
        _scan_chunk(drive, lam_ref, carry, reverse=True)

        @pl.when(c == nchunk - 1)
        def _():
            _segment_carries(carry, pw_ref, out_ref, True)

    rev = lambda c: (nchunk - 1 - c, 0)
    row = _bs((SCAN_ROWS, SSM_W), rev)
    const2 = lambda c: (0, 0)
    return pl.pallas_call(
        body, grid=(nchunk,),
        in_specs=[_bs((N_DEV, per_seg, SSM_W), lambda c: (0, nchunk - 1 - c, 0)), row, row, _bs(cd.shape, lambda c: (0, 0, 0)),
                  _bs((8, 2 * NSTATE), const2), _bs((1, 2 * NSTATE), const2)],
        out_specs=[_bs((8, 2 * NSTATE), const2), row, row, _bs((1, SSM_W), const2)],
        out_shape=[SDS((8, 2 * NSTATE), F32), SDS((t, SSM_W), F32), SDS((t, SSM_W), BF16), SDS((1, SSM_W), F32)],
        scratch_shapes=[pltpu.VMEM((SCAN_ROWS, 2 * NSTATE), F32), pltpu.VMEM((8, 2 * NSTATE), F32),
                        pltpu.VMEM((SSM_W // BLK, SCAN_ROWS, BLK), F32)],
        compiler_params=_cp(1), name="ssm_carries_bwd")(d_yg.reshape(N_DEV, t // N_DEV, SSM_W), ys, u, cd, lamc8, pwc)


def _block_diag(blocks):
    nb, ng, r, c = blocks.shape
    eye = jnp.eye(ng, dtype=blocks.dtype)
    return (blocks[:, :, :, None, :] * eye[None, :, None, :, None]).reshape(nb, ng * r, ng * c)


def _diag_blocks(full, r, c):
    k, nb = full.shape[:2]
    ng = full.shape[2] // r
    x = full.reshape(k, nb, ng, r, ng, c)
    eye = jnp.eye(ng, dtype=full.dtype)
    return jnp.sum(x * eye[None, None, :, None, :, None], axis=4).reshape(k, nb * ng, r, c)


_SMALL = ("a_re", "a_im", "log_dt", "b_re", "b_im", "c_re", "c_im", "d_skip", "g_ffn", "g_final")


def _pack_small(arrs):
    flat = jnp.concatenate([a.reshape(-1) for a in arrs])
    pad = (-flat.shape[0]) % (8 * 128)
    return jnp.pad(flat, (0, pad)).reshape(-1, 128)


def _unpack_small(packed, shapes):
    flat = packed.reshape(-1)
    out, off = [], 0
    for s in shapes:
        n = math.prod(s)
        out.append(flat[off:off + n].reshape(s))
        off += n
    return out


def kernel(x, p, positions, g_mix, w_in, a_re, a_im, log_dt, b_re, b_im, c_re, c_im, d_skip, w_attn_proj, w_glu_a, w_glu_b, w_out, g_ffn, w_ffn_gate, w_ffn_up, w_ffn_down, w_ple_gate, w_ple_proj, g_final, loss_target, m_g_mix, m_w_in, m_a_re, m_a_im, m_log_dt, m_b_re, m_b_im, m_c_re, m_c_im, m_d_skip, m_w_attn_proj, m_w_glu_a, m_w_glu_b, m_w_out, m_g_ffn, m_w_ffn_gate, m_w_ffn_up, m_w_ffn_down, m_w_ple_gate, m_w_ple_proj, m_g_final, v_g_mix, v_w_in, v_a_re, v_a_im, v_log_dt, v_b_re, v_b_im, v_c_re, v_c_im, v_d_skip, v_w_attn_proj, v_w_glu_a, v_w_glu_b, v_w_out, v_g_ffn, v_w_ffn_gate, v_w_ffn_up, v_w_ffn_down, v_w_ple_gate, v_w_ple_proj, v_g_final):
    args = dict(locals())
    t, d = x.shape[1], x.shape[2]
    inw = w_in.shape[2] * N_DEV
    fs = w_ffn_gate.shape[2]
    ff = fs * N_DEV
    ple = w_ple_proj.shape[1]
    seg = t // N_DEV
    assert inw == 3 * QK_W + SSM_W + 2 * d and t % (N_DEV * SCAN_ROWS // 8) == 0 and seg & (seg - 1) == 0
    tm = min(1024, t)
    te = min(512, t)
    tk = min(2048, t)
    ucol = (3 * QK_W) // SSM_W
    gcol = (3 * QK_W + SSM_W) // d
    assert (3 * QK_W + SSM_W) % d == 0

    x2, p2, tgt = x[0], p[0, 0], loss_target[0]
    pos = positions.reshape(t, 1)
    inv = ROPE_THETA ** (-jnp.arange(ROPE_HALF, dtype=F32) * 2.0 / ROPE_DIM)
    invf = jnp.concatenate([inv, inv, jnp.zeros((HEAD_DIM - ROPE_DIM,), F32)]).reshape(1, HEAD_DIM)

    wnames = ("w_in", "w_attn_proj", "w_glu_a", "w_glu_b", "w_out", "w_ffn_gate", "w_ffn_up", "w_ffn_down", "w_ple_gate",
              "w_ple_proj")
    kinds = ("cols", "cols", "cols", "cols", "rows", "slot", "slot", "rows", "rows", "cols")
    shards = [args[n][0].astype(BF16) for n in wnames]
    sizes = [s.shape[0] if k == "rows" else s.shape[-1] for s, k in zip(shards, kinds)]
    ag = _exchange_start("gather_weights_start", shards, kinds, sizes, True)

    row_d = _bs((tm, d), lambda i, j, k: (i, 0))
    row_e = _bs((te, d), lambda i, j, k: (i, 0))
    vec_d = _bs((1, d), lambda i, j, k: (0, 0))
    sq_w = _bs((d, d), lambda i, j, k: (0, 0))
    n1 = _rms_fwd("norm_mix", x2, g_mix + ag[3][0:1, 0:1], tm)

    nsq = seg.bit_length() - 1
    bar_re, bar_im, z_re, z_im, pw_re, pw_im = _ssm_disc(a_re[0], a_im[0], log_dt.reshape(SSM_GROUPS, 1), nsq)
    gp = SSM_GROUPS * SSM_STATE
    b_re2, b_im2 = b_re.reshape(gp, SSM_GROUP), b_im.reshape(gp, SSM_GROUP)
    bb_re, bb_im = _ssm_scale_b(z_re.reshape(gp, 1), z_im.reshape(gp, 1), b_re2, b_im2)

    def chunks(a, r, c):
        return a.reshape(SSM_NB, SSM_GROUPS // SSM_NB, r, c)

    bbt = lambda a: jnp.swapaxes(a.reshape(SSM_GROUPS, SSM_STATE, SSM_GROUP), 1, 2)
    bd = jnp.concatenate([_block_diag(chunks(bbt(bb_re), SSM_GROUP, SSM_STATE)),
                          _block_diag(chunks(bbt(bb_im), SSM_GROUP, SSM_STATE))]).astype(BF16)
    ct = lambda a: jnp.swapaxes(a[0], 1, 2)
    cd = jnp.concatenate([_block_diag(chunks(ct(c_re), SSM_STATE, SSM_GROUP)),
                          _block_diag(chunks(-ct(c_im), SSM_STATE, SSM_GROUP))]).astype(BF16)
    lam = jnp.concatenate([bar_re.reshape(1, gp), bar_im.reshape(1, gp)], axis=1)
    lamc = jnp.concatenate([bar_re.reshape(1, gp), -bar_im.reshape(1, gp)], axis=1)
    pw = jnp.concatenate([pw_re.reshape(1, gp), pw_im.reshape(1, gp)], axis=1)
    pwc = jnp.concatenate([pw_re.reshape(1, gp), -pw_im.reshape(1, gp)], axis=1)
    lam8, lamc8 = jnp.broadcast_to(lam, (8, 2 * gp)), jnp.broadcast_to(lamc, (8, 2 * gp))

    pk = lambda pre: jnp.concatenate([_pack_small([args[pre + n] for n in _SMALL]), _pack_small([args[pre + "g_mix"]])])
    packed = [pk(""), pk("m_"), pk("v_")]

    rope_c, rope_s = _rope_tables(pos, invf, tm)

    W_in, = _exchange_wait("gather_w_in_wait", ag, [0], kinds, sizes, True,
                           [n1, bd, cd, lam8, lamc8, pw, pwc, rope_c, rope_s] + packed)
    tab = _bs((tm, HEAD_DIM), lambda i, j, k: (i, 0))
    qkv = _mm(
        "qkv_proj", (t // tm, 3, 1), [("nn", n1, row_d, W_in, _bs((d, QK_W), lambda i, j, k: (0, j)))],
        [(SDS((3, t // dil, dil * GROUP_W), BF16), _bs((None, tm // dil, dil * GROUP_W), lambda i, j, k: (j, i, 0)))
         for dil in DILATIONS],
        extras=[(rope_c, tab), (rope_s, tab)],
        epilogue=_rope_dilate_epilogue(tm), scratch=[pltpu.VMEM((QK_W // HEAD_DIM, tm, HEAD_DIM), F32)])
    row_s = _bs((tm, SSM_W), lambda i, j, k: (i, 0))
    u_perm, u_bf = _mm("u_proj", (t // tm, 1, 1),
                       [("nn", n1.reshape(N_DEV, seg, d), _bs((N_DEV, tm // N_DEV, d), lambda i, j, k: (0, i, 0)), W_in,
                         _bs((d, SSM_W), lambda i, j, k: (0, ucol)))],
                       [(SDS((t, SSM_W), F32), row_s), (SDS((t, SSM_W), BF16), row_s)], epilogue=_interleave_epilogue,
                       scratch=[pltpu.VMEM((SSM_W // BLK, tm, BLK), F32)])
    zg, = _mm("z_gates", (t // tm, 2, 1),
              [("nn", n1, row_d, W_in, _bs((d, d), lambda i, j, k: (0, gcol + j)))],
              [(SDS((t, 2 * d), BF16), _bs((tm, d), lambda i, j, k: (i, j)))])

    outs, lses = [], []
    for g, dil in enumerate(DILATIONS):
        o_g, l_g = _attn_fwd(qkv[g], dil, min(1024, t // dil))
        outs.append(o_g)
        lses.append(l_g)
    merged = _attn_merge(outs, lses, tm)
    attn, attn_bf, lts = merged[0], merged[1], merged[2:]

    start_f = _ssm_carries("ssm_carries_fwd", u_bf, bd, "nn", lam8, pw, False)
    dsk = d_skip.reshape(1, SSM_W)
    h_all, ys, yg_bf = _ssm_fwd(u_bf, u_perm, dsk, bd, cd, lam8, start_f)
    W_ap, W_ga, W_gb, W_out, W_fg, W_fu, W_fd, W_pg, W_pp = _exchange_wait(
        "gather_rest_wait", ag, list(range(1, len(wnames))), kinds, sizes, True, yg_bf)
    W_fg = jnp.swapaxes(W_fg, 0, 1).reshape(d, ff)
    W_fu = jnp.swapaxes(W_fu, 0, 1).reshape(d, ff)

    glu_w = _bs((SSM_W, d), lambda i, j, k: (0, 0))
    row_s = _bs((tm, SSM_W), lambda i, j, k: (i, 0))
    gate_a = _bs((te, d), lambda i, j, k: (i, 0))
    gate_s = _bs((te, d), lambda i, j, k: (i, 1))
    td_f32, td_bf = SDS((t, d), F32), SDS((t, d), BF16)
    m_bf, ya, yb, attn_d = _mm(
        "glu_merge", (t // tm, 1, 1),
        [("nn", yg_bf, row_s, W_ga, glu_w), ("nn", yg_bf, row_s, W_gb, glu_w), ("nn", attn_bf, row_s, W_ap, glu_w)],
        [(td_bf, row_d)] * 4, extras=[(zg, row_d), (zg, _bs((tm, d), lambda i, j, k: (i, 1)))], epilogue=_glu_merge_epilogue)

    h1, n2 = _mm("out_proj", (t // tm, 1, 1), [("nn", m_bf, row_d, W_out, sq_w)], [(td_f32, row_d), (td_bf, row_d)],
                 extras=[(x2, row_d), (g_ffn, vec_d)], epilogue=_out_norm_epilogue)

    tn_f = ff // 2
    nf = ff // tn_f
    hid_o = _bs((tm, tn_f), lambda j, i, k: (i, j))
    tf_bf = SDS((t, ff), BF16)
    a_rows = _bs((tm, d), lambda j, i, k: (i, 0))
    w_cols = _bs((d, tn_f), lambda j, i, k: (0, j))
    act, fg, fu = _mm("ffn_gate_up", (nf, t // tm, 1), [("nn", n2, a_rows, W_fg, w_cols), ("nn", n2, a_rows, W_fu, w_cols)],
                      [(tf_bf, hid_o)] * 3, epilogue=_swiglu_epilogue)
    w_once = pl.BlockSpec((d, d), lambda i, j, k: (0, 0), pipeline_mode=pl.Buffered(1))
    loss_part, dg_final, dh2, dh2_bf, dpp_bf, dpg_bf, h2_bf = _mm(
        "ffn_down_head", (t // te, 1, 1),
        [("nn", act, _bs((te, ff), lambda i, j, k: (i, 0)), W_fd,
          pl.BlockSpec((ff, d), lambda i, j, k: (0, 0), pipeline_mode=pl.Buffered(1))),
         ("nn", p2, _bs((te, ple), lambda i, j, k: (i, 0)), W_pp, _bs((ple, d), lambda i, j, k: (0, 0)))],
        [(SDS((1, 1), F32), _bs((1, 1), lambda i, j, k: (0, 0))), (SDS((1, d), F32), vec_d), (td_f32, row_e), (td_bf, row_e),
         (td_bf, row_e), (td_bf, row_e), (td_bf, row_e)],
        extras=[(h1, row_e), (g_final.reshape(1, d), vec_d), (tgt, row_e), (W_pg, w_once)], epilogue=_head_epilogue(t // te),
        scratch=[pltpu.VMEM((1, d), F32)])
    loss = lax.psum(loss_part[0, 0], ("x", "y", "c"))

    nkt = t // tk
    tok_a = lambda w: _bs((tk, w), lambda i, j, k: (k, 0))

    def wgrad(name, a, wa, b, wb):
        return _mm(name, (1, 1, nkt), [("tn", a, tok_a(wa), b, tok_a(wb))],
                   [(SDS((wa, wb), BF16), _bs((wa, wb), lambda i, j, k: (0, 0)))])[0]

    dW_pp = wgrad("dw_ple_proj", p2, ple, dpp_bf, d)
    dW_pg = wgrad("dw_ple_gate", h2_bf, d, dpg_bf, d)
    dfg_bf, dfu_bf = _mm("d_ffn_down", (nf, t // tm, 1),
                         [("nt", dh2_bf, a_rows, W_fd, _bs((tn_f, d), lambda j, i, k: (j, 0)))],
                         [(tf_bf, hid_o), (tf_bf, hid_o)], extras=[(fg, hid_o), (fu, hid_o)], epilogue=_swiglu_bwd_epilogue)
    dW_fd, = _mm("dw_ffn_down", (nf, 1, nkt), [("tn", act, _bs((tk, tn_f), lambda i, j, k: (k, i)), dh2_bf, tok_a(d))],
                 [(SDS((ff, d), BF16), _bs((tn_f, d), lambda i, j, k: (i, 0)))])
    hid_t = _bs((tk, tn_f), lambda i, j, k: (k, j))
    wg_o = [(SDS((d, ff), BF16), _bs((d, tn_f), lambda i, j, k: (0, j)))]
    dW_fg, = _mm("dw_ffn_gate", (1, nf, nkt), [("tn", n2, tok_a(d), dfg_bf, hid_t)], wg_o)
    dW_fu, = _mm("dw_ffn_up", (1, nf, nkt), [("tn", n2, tok_a(d), dfu_bf, hid_t)], wg_o)
    dW_fg = jnp.swapaxes(dW_fg.reshape(d, N_DEV, fs), 0, 1)
    dW_fu = jnp.swapaxes(dW_fu.reshape(d, N_DEV, fs), 0, 1)
    group = lambda names: ([kinds[wnames.index(n)] for n in names], [sizes[wnames.index(n)] for n in names])
    ffn_names = ("w_ffn_gate", "w_ffn_up", "w_ffn_down", "w_ple_gate", "w_ple_proj")
    rs_ffn = _exchange_start("scatter_ffn_start", [dW_fg, dW_fu, dW_fd, dW_pg, dW_pp], *group(ffn_names), False)
    hid_all = _bs((te, ff), lambda i, j, k: (i, 0))
    w_all = pl.BlockSpec((d, ff), lambda i, j, k: (0, 0), pipeline_mode=pl.Buffered(1))
    dh1, dh1_bf, dg_ffn = _mm("d_ffn_gate_up", (t // te, 1, 1),
                              [("nt", dfg_bf, hid_all, W_fg, w_all), ("nt", dfu_bf, hid_all, W_fu, w_all)],
                              [(td_f32, row_e), (td_bf, row_e), (SDS((1, d), F32), vec_d)],
                              extras=[(h1, row_e), (g_ffn, vec_d), (dh2, row_e)], epilogue=_rms_bwd_epilogue, after=rs_ffn[3])

    dW_out = wgrad("dw_out", m_bf, d, dh1_bf, d)
    glu_once = pl.BlockSpec((SSM_W, d), lambda i, j, k: (0, 0), pipeline_mode=pl.Buffered(1))
    row_es = _bs((te, SSM_W), lambda i, j, k: (i, 0))
    ts_f32 = SDS((t, SSM_W), F32)
    dz_g, dad_bf, dya_bf, dyb_bf, d_yg, d_attn = _mm(
        "d_out_proj", (t // te, 1, 1), [("nt", dh1_bf, row_e, W_out, w_once)],
        [(SDS((t, 2 * d), BF16), _bs((te, 2 * d), lambda i, j, k: (i, 0))), (td_bf, row_e), (td_bf, row_e), (td_bf, row_e),
         (ts_f32, row_es), (ts_f32, row_es)],
        extras=[(zg, gate_a), (zg, gate_s), (attn_d, row_e), (ya, row_e), (yb, row_e), (W_ga, glu_once), (W_gb, glu_once),
                (W_ap, glu_once)], epilogue=_merge_bwd_epilogue)

    dW_ga = wgrad("dw_glu_a", yg_bf, SSM_W, dya_bf, d)
    dW_gb = wgrad("dw_glu_b", yg_bf, SSM_W, dyb_bf, d)
    start_b, dys, dys_bf, dd_skip = _ssm_carries_bwd(d_yg, ys, u_perm, cd, lamc8, pwc)
    dz_u, dlam8, dbd, dcd = _ssm_bwd(dys_bf, dys, dsk, u_bf, h_all, bd, cd, lamc8, start_b)
    dlam = jnp.sum(dlam8, axis=0)
    dbb = _diag_blocks(dbd.reshape(2, SSM_NB, BLK, 512), SSM_GROUP, SSM_STATE)
    dbb_re = jnp.swapaxes(dbb[0], 1, 2).reshape(gp, SSM_GROUP)
    dbb_im = jnp.swapaxes(dbb[1], 1, 2).reshape(gp, SSM_GROUP)
    dcc = _diag_blocks(dcd.reshape(2, SSM_NB, 512, BLK), SSM_STATE, SSM_GROUP)
    dc_re, dc_im = jnp.swapaxes(dcc[0], 1, 2), -jnp.swapaxes(dcc[1], 1, 2)
    db_re, db_im, dz_re, dz_im = _ssm_scale_b_bwd(z_re.reshape(gp, 1), z_im.reshape(gp, 1), b_re2, b_im2, dbb_re, dbb_im)
    gshape = (SSM_GROUPS, SSM_STATE)
    da_re, da_im, dlog_dt = _ssm_disc_bwd(a_re[0], a_im[0], log_dt.reshape(SSM_GROUPS, 1), dlam[:gp].reshape(gshape),
                                          dlam[gp:].reshape(gshape), dz_re.reshape(gshape), dz_im.reshape(gshape))

    dW_ap = wgrad("dw_attn_proj", attn_bf, GROUP_W, dad_bf, d)
    pre = _attn_bwd_pre(d_attn, attn, tm)
    das, deltas = pre[:N_GROUPS], pre[N_GROUPS:]
    dqkvs = [_attn_bwd(qkv[g], das[g], lts[g], deltas[g], dil, min(1024, t // dil)) for g, dil in enumerate(DILATIONS)]
    dz_qkv = _undilate_rope_bwd(dqkvs, rope_c, rope_s, tm)

    dW_in, = _mm("dw_in_qkv", (1, 3, nkt), [("tn", n1, tok_a(d), dz_qkv, _bs((tk, QK_W), lambda i, j, k: (k, j)))],
                 [(SDS((d, inw), BF16), _bs((d, QK_W), lambda i, j, k: (0, j)))])
    dW_in, = _mm("dw_in_u", (1, 1, nkt), [("tn", n1, tok_a(d), dz_u, tok_a(SSM_W))],
                 [(SDS((d, inw), BF16), _bs((d, SSM_W), lambda i, j, k: (0, ucol)))], alias_to_out0=dW_in)
    dW_in, = _mm("dw_in_gates", (1, 2, nkt), [("tn", n1, tok_a(d), dz_g, _bs((tk, d), lambda i, j, k: (k, j)))],
                 [(SDS((d, inw), BF16), _bs((d, d), lambda i, j, k: (0, gcol + j)))], alias_to_out0=dW_in)
    small_parts = dict(a_re=da_re, a_im=da_im, log_dt=dlog_dt, b_re=db_re, b_im=db_im, c_re=dc_re, c_im=dc_im,
                       d_skip=dd_skip, g_ffn=dg_ffn, g_final=dg_final)
    small = _pack_small([small_parts[n] for n in _SMALL])
    rest_names = ("w_in", "w_attn_proj", "w_glu_a", "w_glu_b", "w_out")
    rest_kinds, rest_sizes = group(rest_names)
    rs_in = _exchange_start("scatter_rest_start", [dW_in, dW_ap, dW_ga, dW_gb, dW_out, small], rest_kinds + ["all"],
                            rest_sizes + [0], False)
    w_piece = lambda w, cb: pl.BlockSpec((d, w), lambda i, j, k: (0, cb), pipeline_mode=pl.Buffered(1))
    dx, dg_mix = _mm(
        "d_z_proj", (t // te, 1, 1),
        [("nt", dz_qkv, _bs((te, 3 * QK_W), lambda i, j, k: (i, 0)), W_in, w_piece(3 * QK_W, 0)),
         ("nt", dz_u, _bs((te, SSM_W), lambda i, j, k: (i, 0)), W_in, w_piece(SSM_W, ucol)),
         ("nt", dz_g, _bs((te, d), lambda i, j, k: (i, 0)), W_in, w_piece(d, gcol)),
         ("nt", dz_g, _bs((te, d), lambda i, j, k: (i, 1)), W_in, w_piece(d, gcol + 1))],
        [(td_f32, row_e), (SDS((1, d), F32), vec_d)],
        extras=[(x2, row_e), (g_mix, vec_d), (dh1, row_e)], epilogue=_rms_bwd_epilogue, after=rs_in[3])

    received = dict(zip(ffn_names, _exchange_wait("scatter_ffn_wait", rs_ffn, list(range(len(ffn_names))), *group(ffn_names),
                                                  False, dx)))
    *landed, small_all = _exchange_wait("scatter_rest_wait", rs_in, list(range(len(rest_names) + 1)), rest_kinds + ["all"],
                                        rest_sizes + [0], False, dx)
    received.update(zip(rest_names, landed))

    new = {}
    for n in wnames:
        new[n] = [o.reshape(args[n].shape)
                  for o in _adamw("adamw_" + n, received[n], args[n][0], args["m_" + n][0], args["v_" + n][0])]
    g_mix_all = _gather_small(_pack_small([dg_mix]))
    sm = _adamw("adamw_small", jnp.concatenate([small_all, g_mix_all], axis=1), *packed)
    rows_a = small.shape[0]
    shapes = [args[n].shape for n in _SMALL]
    for n, vals in zip(_SMALL, zip(*[_unpack_small(o[:rows_a], shapes) for o in sm])):
        new[n] = list(vals)
    new["g_mix"] = [_unpack_small(o[rows_a:], [g_mix.shape])[0] for o in sm]

    order = ("g_mix", "w_in", "a_re", "a_im", "log_dt", "b_re", "b_im", "c_re", "c_im", "d_skip", "w_attn_proj", "w_glu_a",
             "w_glu_b", "w_out", "g_ffn", "w_ffn_gate", "w_ffn_up", "w_ffn_down", "w_ple_gate", "w_ple_proj", "g_final")
    return (loss, dx.reshape(x.shape), *[new[n][0] for n in order], *[new[n][1] for n in order],
            *[new[n][2] for n in order], *[new[n][3] for n in order])
```

```python
import functools
import math

import jax
import jax.numpy as jnp
from jax import lax
from jax.experimental import pallas as pl
from jax.experimental.pallas import tpu as pltpu

F32 = jnp.float32
BF16 = jnp.bfloat16
SDS = jax.ShapeDtypeStruct

N_DEV = 8
HEAD_DIM = 128
HEADS_PER_GROUP = 4
GROUP_W = HEADS_PER_GROUP * HEAD_DIM
DILATIONS = (1, 4, 16)
N_GROUPS = len(DILATIONS)
QK_W = N_GROUPS * GROUP_W
BLK = 128
ROPE_THETA = 500000.0
ROPE_DIM = HEAD_DIM // 4
ROPE_HALF = ROPE_DIM // 2
SSM_W = 512
SSM_GROUP = 16
SSM_GROUPS = SSM_W // SSM_GROUP
SSM_STATE = 64
NSTATE = SSM_GROUPS * SSM_STATE
SSM_NB = 4
EPS = 1e-6
ADAM_LR, ADAM_B1, ADAM_B2, ADAM_EPS, ADAM_WD, ADAM_STEP = 0.001, 0.9, 0.999, 1e-08, 0.01, 10
NEG = -1e30

VMEM_LIMIT = 52 * 1024 * 1024
SCAN_ROWS = 512
SCAN_LANES = 512


def _cp(n):
    return pltpu.CompilerParams(dimension_semantics=("arbitrary",) * n, vmem_limit_bytes=VMEM_LIMIT)


def _sigmoid(x):
    return 0.5 * jnp.tanh(0.5 * x) + 0.5


_DNUMS = {"nn": (((1,), (0,)), ((), ())), "nt": (((1,), (1,)), ((), ())), "tn": (((0,), (0,)), ((), ()))}


def _bs(shape, fn):
    return pl.BlockSpec(shape, fn)


def _store_all(prods, extra_refs, out_refs, scratch_refs):
    r = prods[0]
    for p in prods[1:]:
        r = r + p
    for e in extra_refs:
        r = r + e[...]
    for o in out_refs:
        o[...] = r.astype(o.dtype)


def _mm(name, grid, pairs, outs, extras=(), epilogue=_store_all, scratch=(), alias_to_out0=None, after=None,
        by_columns=False):
    nk = grid[2]
    npair = len(pairs)
    steps = [p[5] if len(p) > 5 else nk for p in pairs]

    def block(spec):
        return tuple(s for s in spec.block_shape if s is not None)

    def rows2d(shape):
        return (math.prod(shape[:-1]), shape[-1]) if len(shape) == 3 else shape

    acc_shapes = [jax.eval_shape(lambda u, v, dn=_DNUMS[p[0]]: lax.dot_general(u, v, dn, preferred_element_type=F32),
                                 SDS(rows2d(block(p[2])), BF16), SDS(block(p[4]), BF16)).shape for p in pairs]
    if nk == 1:
        acc_shapes = []
    n_in = 2 * npair + len(extras) + (alias_to_out0 is not None) + (after is not None)

    def body(*refs):
        extra_refs = refs[2 * npair:2 * npair + len(extras)]
        out_refs = refs[n_in:n_in + len(outs)]
        rest = refs[n_in + len(outs):]
        acc_refs = rest[:len(acc_shapes)]
        scratch_refs = rest[len(acc_refs):]
        k = pl.program_id(2)

        def product(i, cols=None):
            a = refs[2 * i][...]
            if a.ndim == 3:
                a = a.reshape(-1, a.shape[-1])
            b = refs[2 * i + 1][...] if cols is None else refs[2 * i + 1][:, cols]
            return lax.dot_general(a.astype(BF16), b.astype(BF16), _DNUMS[pairs[i][0]], preferred_element_type=F32)

        if nk == 1 and by_columns:
            epilogue([functools.partial(product, i) for i in range(npair)], extra_refs, out_refs, scratch_refs)
            return
        if nk == 1:
            epilogue([product(i) for i in range(npair)], extra_refs, out_refs, scratch_refs)
            return
        for i in range(npair):
            @pl.when(k == 0)
            def _(i=i):
                acc_refs[i][...] = product(i)

            @pl.when((k > 0) & (k < steps[i]))
            def _(i=i):
                acc_refs[i][...] += product(i)

        @pl.when(k == nk - 1)
        def _():
            epilogue([a[...] for a in acc_refs], extra_refs, out_refs, scratch_refs)

    ins, in_specs = [], []
    for p in pairs:
        ins += [p[1], p[3]]
        in_specs += [p[2], p[4]]
    ins += [e[0] for e in extras]
    in_specs += [e[1] for e in extras]
    aliases = {}
    if alias_to_out0 is not None:
        aliases = {len(ins): 0}
        ins.append(alias_to_out0)
        in_specs.append(pl.BlockSpec(memory_space=pl.ANY))
    if after is not None:
        ins.append(after)
        in_specs.append(pl.BlockSpec(memory_space=pl.ANY))
    scratch_shapes = [pltpu.VMEM(s, F32) for s in acc_shapes] + list(scratch)
    return pl.pallas_call(body, grid=grid, in_specs=in_specs, out_specs=[o[1] for o in outs], out_shape=[o[0] for o in outs],
                          scratch_shapes=scratch_shapes, input_output_aliases=aliases, compiler_params=_cp(3), name=name)(*ins)


def _my_index():
    return 4 * lax.axis_index("x") + 2 * lax.axis_index("y") + lax.axis_index("c")


def _peer(d):
    mx, my, mc = lax.axis_index("x"), lax.axis_index("y"), lax.axis_index("c")
    return (mx ^ ((d >> 2) & 1), my ^ ((d >> 1) & 1), mc ^ (d & 1))


def _win(ref, kind, j, n):
    if kind == "all":
        return ref
    if kind == "slot":
        return ref.at[j]
    if kind == "rows":
        return ref.at[pl.ds(pl.multiple_of(j * n, 8), n)]
    return ref.at[:, pl.ds(pl.multiple_of(j * n, 128), n)]


def _win7(ref, kind, n):
    if kind == "slot":
        return ref.at[pl.ds(0, 7)]
    if kind == "rows":
        return ref.at[pl.ds(0, 7 * n)]
    return ref.at[:, pl.ds(0, 7 * n)]


def _full_shape(shard_shape, kind):
    if kind == "slot":
        return (N_DEV,) + tuple(shard_shape)
    if kind == "rows":
        return (N_DEV * shard_shape[0],) + tuple(shard_shape[1:])
    return (shard_shape[0], N_DEV * shard_shape[1])


def _shard_shape(full_shape, kind, n):
    if kind == "all":
        return tuple(full_shape)
    if kind == "slot":
        return tuple(full_shape[1:])
    if kind == "rows":
        return (n,) + tuple(full_shape[1:])
    return (full_shape[0], n)


_HBM = pl.BlockSpec(memory_space=pltpu.HBM)
_SEM = pl.BlockSpec(memory_space=pltpu.SEMAPHORE)
_DATAFLOW = pltpu.SideEffectType.DATAFLOW_SIDE_EFFECTING


def _exchange_start(name, srcs, kinds, sizes, gather):
    n = len(srcs)
    if gather:
        lands = [lax.empty(_full_shape(s.shape, k), s.dtype) for s, k in zip(srcs, kinds)]
    else:
        lands = [lax.empty((N_DEV,) + _shard_shape(s.shape, k, z), s.dtype) for s, k, z in zip(srcs, kinds, sizes)]

    def body(*refs):
        src, land = refs[:n], refs[n:2 * n]
        send_sems, recv_sems, local_sems = refs[2 * n], refs[2 * n + 1], refs[2 * n + 2]
        token = refs[4 * n + 3]
        me = _my_index()
        for a in range(n):
            _local_copy(src[a], land[a], kinds[a], sizes[a], gather, me, local_sems.at[a]).start()
        for a in range(n):
            for d in range(1, N_DEV):
                px, py, pc = _peer(d)
                if gather:
                    s_ref, d_ref = src[a], _win(land[a], kinds[a], me, sizes[a])
                else:
                    s_ref, d_ref = _win(src[a], kinds[a], 4 * px + 2 * py + pc, sizes[a]), land[a].at[me]
                pltpu.make_async_remote_copy(src_ref=s_ref, dst_ref=d_ref, send_sem=send_sems.at[a], recv_sem=recv_sems.at[a],
                                             device_id=(px, py, pc), device_id_type=pl.DeviceIdType.MESH).start()
        token[...] = jnp.zeros_like(token)

    hbm = [pltpu.with_memory_space_constraint(a, pltpu.HBM) for a in list(srcs) + lands]
    out = pl.pallas_call(
        body, name=name, in_specs=[_HBM] * (2 * n),
        out_shape=[pltpu.SemaphoreType.DMA((n,))] * 3 + [pltpu.HBM(a.shape, a.dtype) for a in hbm] + [SDS((8, 128), F32)],
        out_specs=[_SEM] * 3 + [_HBM] * (2 * n) + [pl.BlockSpec(memory_space=pltpu.VMEM)],
        input_output_aliases={i: 3 + i for i in range(2 * n)},
        compiler_params=pltpu.CompilerParams(has_side_effects=_DATAFLOW))(*hbm)
    return out[0:3], out[3:3 + n], out[3 + n:3 + 2 * n], out[-1]


def _local_copy(src, land, kind, size, gather, me, sem):
    if gather:
        return pltpu.make_async_copy(src, _win(land, kind, me, size), sem)
    return pltpu.make_async_copy(_win(src, kind, me, size), land.at[me], sem)


def _exchange_wait(name, started, which, kinds, sizes, gather, after):
    sems, srcs, lands, _ = started
    n = len(which)
    after = list(after) if isinstance(after, (list, tuple)) else [after]

    def body(*refs):
        src, land = refs[:n], refs[n:2 * n]
        send_ref, recv_ref, local_ref = refs[2 * n:2 * n + 3]
        me = _my_index()
        my_id = (lax.axis_index("x"), lax.axis_index("y"), lax.axis_index("c"))
        for i, a in enumerate(which):
            seven = _win7(land[i], kinds[a], sizes[a]) if gather else land[i].at[pl.ds(0, 7)]
            pltpu.make_async_remote_copy(src_ref=seven, dst_ref=seven, send_sem=send_ref.at[a], recv_sem=recv_ref.at[a],
                                         device_id=my_id, device_id_type=pl.DeviceIdType.MESH).wait()
            _local_copy(src[i], land[i], kinds[a], sizes[a], gather, me, local_ref.at[a]).wait()

    hbm = [srcs[a] for a in which] + [lands[a] for a in which]
    out = pl.pallas_call(
        body, name=name, in_specs=[_HBM] * (2 * n) + [_SEM] * 3 + [pl.BlockSpec(memory_space=pl.ANY)] * len(after),
        out_shape=[pltpu.HBM(a.shape, a.dtype) for a in hbm], out_specs=[_HBM] * (2 * n),
        input_output_aliases={i: i for i in range(2 * n)},
        compiler_params=pltpu.CompilerParams(has_side_effects=_DATAFLOW))(*hbm, *sems, *after)
    return out[n:]


def _gather_small(small):
    def body(in_ref, out_ref, send_sem, recv_sem, local_sem):
        me = _my_index()
        my_id = (lax.axis_index("x"), lax.axis_index("y"), lax.axis_index("c"))
        cp = pltpu.make_async_copy(in_ref, out_ref.at[me], local_sem)
        cp.start()
        for d in range(1, N_DEV):
            pltpu.make_async_remote_copy(src_ref=in_ref, dst_ref=out_ref.at[me], send_sem=send_sem, recv_sem=recv_sem,
                                         device_id=_peer(d), device_id_type=pl.DeviceIdType.MESH).start()
        seven = out_ref.at[pl.ds(0, 7)]
        pltpu.make_async_remote_copy(src_ref=seven, dst_ref=seven, send_sem=send_sem, recv_sem=recv_sem, device_id=my_id,
                                     device_id_type=pl.DeviceIdType.MESH).wait()
        cp.wait()

    any_spec = pl.BlockSpec(memory_space=pl.ANY)
    return pl.pallas_call(body, in_specs=[any_spec], out_specs=any_spec, out_shape=SDS((N_DEV,) + small.shape, F32),
                          scratch_shapes=[pltpu.SemaphoreType.DMA] * 3, name="gather_small")(small)


def _adamw(name, recv, w, m, v):
    rows, cols = w.shape
    tr = max(c for c in range(16, 257, 16) if rows % c == 0) if rows % 16 == 0 else rows

    def body(r_ref, w_ref, m_ref, v_ref, g_ref, d_ref, nm_ref, nv_ref):
        g = r_ref[0].astype(F32)
        for s in range(1, N_DEV):
            g = g + r_ref[s].astype(F32)
        nm = ADAM_B1 * m_ref[...] + (1.0 - ADAM_B1) * g
        nv = ADAM_B2 * v_ref[...] + (1.0 - ADAM_B2) * (g * g)
        m_hat = nm / (1.0 - ADAM_B1 ** ADAM_STEP)
        v_hat = nv / (1.0 - ADAM_B2 ** ADAM_STEP)
        g_ref[...] = g
        d_ref[...] = -ADAM_LR * (m_hat / (jnp.sqrt(v_hat) + ADAM_EPS) + ADAM_WD * w_ref[...])
        nm_ref[...] = nm
        nv_ref[...] = nv

    blk = _bs((tr, cols), lambda i: (i, 0))
    return pl.pallas_call(
        body, grid=(rows // tr,), in_specs=[_bs((N_DEV, tr, cols), lambda i: (0, i, 0)), blk, blk, blk],
        out_specs=[blk] * 4, out_shape=[SDS((rows, cols), F32)] * 4, compiler_params=_cp(1), name=name)(recv, w, m, v)


def _rms_fwd(name, x, g, tm):
    t, d = x.shape

    def body(x_ref, g_ref, n_ref):
        xv = x_ref[...]
        r = lax.rsqrt(jnp.mean(xv * xv, axis=-1, keepdims=True) + EPS)
        n_ref[...] = (xv * r * g_ref[...]).astype(BF16)

    return pl.pallas_call(body, grid=(t // tm,), in_specs=[_bs((tm, d), lambda i: (i, 0)), _bs((1, d), lambda i: (0, 0))],
                          out_specs=_bs((tm, d), lambda i: (i, 0)), out_shape=SDS((t, d), BF16), compiler_params=_cp(1),
                          name=name)(x, g)


def _accumulate_rows(ref, part):
    @pl.when(pl.program_id(0) == 0)
    def _():
        ref[...] = part

    @pl.when(pl.program_id(0) > 0)
    def _():
        ref[...] += part


def _rms_bwd_epilogue(prods, extra_refs, out_refs, scratch_refs):
    dyv = prods[0]
    for p in prods[1:]:
        dyv = dyv + p
    if len(extra_refs) > 3:
        dyv = dyv + extra_refs[3][...]
    xv = extra_refs[0][...]
    r = lax.rsqrt(jnp.mean(xv * xv, axis=-1, keepdims=True) + EPS)
    xh = xv * r
    dxh = dyv * extra_refs[1][...]
    dx = extra_refs[2][...] + r * (dxh - xh * jnp.mean(dxh * xh, axis=-1, keepdims=True))
    for o in out_refs[:-1]:
        o[...] = dx.astype(o.dtype)
    _accumulate_rows(out_refs[-1], jnp.sum(dyv * xh, axis=0, keepdims=True))


def _out_norm_epilogue(prods, extra_refs, out_refs, scratch_refs):
    h = prods[0] + extra_refs[0][...]
    r = lax.rsqrt(jnp.mean(h * h, axis=-1, keepdims=True) + EPS)
    out_refs[0][...] = h
    out_refs[1][...] = (h * r * extra_refs[1][...]).astype(BF16)


def _glu_merge_epilogue(prods, extra_refs, out_refs, scratch_refs):
    ya, yb, ad = prods
    ga, gs = extra_refs[0][...].astype(F32), extra_refs[1][...].astype(F32)
    m = _sigmoid(ga) * ad + _sigmoid(gs) * (ya * _sigmoid(yb))
    out_refs[0][...] = m.astype(BF16)
    for o, val in zip(out_refs[1:], (ya, yb, ad)):
        o[...] = val.astype(o.dtype)


def _merge_bwd_epilogue(prods, extra_refs, out_refs, scratch_refs):
    dmv = prods[0]
    d = dmv.shape[1]
    ga, gs = _sigmoid(extra_refs[0][...].astype(F32)), _sigmoid(extra_refs[1][...].astype(F32))
    adv, yav = extra_refs[2][...].astype(F32), extra_refs[3][...].astype(F32)
    sb = _sigmoid(extra_refs[4][...].astype(F32))
    out_refs[0][:, 0:d] = (dmv * adv * ga * (1.0 - ga)).astype(BF16)
    out_refs[0][:, d:2 * d] = (dmv * (yav * sb) * gs * (1.0 - gs)).astype(BF16)
    dad = (dmv * ga).astype(BF16)
    dsd = dmv * gs
    dya = (dsd * sb).astype(BF16)
    dyb = (dsd * yav * sb * (1.0 - sb)).astype(BF16)
    out_refs[1][...], out_refs[2][...], out_refs[3][...] = dad, dya, dyb
    nt = _DNUMS["nt"]
    out_refs[4][...] = (lax.dot_general(dya, extra_refs[5][...], nt, preferred_element_type=F32)
                        + lax.dot_general(dyb, extra_refs[6][...], nt, preferred_element_type=F32))
    out_refs[5][...] = lax.dot_general(dad, extra_refs[7][...], nt, preferred_element_type=F32)


def _swiglu_epilogue(prods, extra_refs, out_refs, scratch_refs):
    gv, uv = prods
    out_refs[0][...] = (gv * _sigmoid(gv) * uv).astype(BF16)
    out_refs[1][...] = gv.astype(out_refs[1].dtype)
    out_refs[2][...] = uv.astype(out_refs[2].dtype)


def _swiglu_bwd_epilogue(prods, extra_refs, out_refs, scratch_refs):
    dav = prods[0]
    gv, uv = extra_refs[0][...].astype(F32), extra_refs[1][...].astype(F32)
    sg = _sigmoid(gv)
    out_refs[0][...] = (dav * uv * sg * (1.0 + gv * (1.0 - sg))).astype(BF16)
    out_refs[1][...] = (dav * gv * sg).astype(BF16)


def _head_epilogue(n_tiles):
    def epilogue(prods, extra_refs, out_refs, scratch_refs):
        h2 = prods[0] + extra_refs[0][...]
        h2_bf = h2.astype(BF16)
        out_refs[6][...] = h2_bf
        pgv = jnp.dot(h2_bf, extra_refs[3][...], preferred_element_type=F32)
        ppv = prods[1]
        d = pgv.shape[1]
        lacc = scratch_refs[0]
        sg = _sigmoid(pgv)
        h3 = h2 + sg * ppv
        r = lax.rsqrt(jnp.mean(h3 * h3, axis=-1, keepdims=True) + EPS)
        xh = h3 * r
        gv = extra_refs[1][...]
        diff = xh * gv - extra_refs[2][...]
        dout = diff * (1.0 / d)
        dxh = dout * gv
        dh3 = r * (dxh - xh * jnp.mean(dxh * xh, axis=-1, keepdims=True))
        dpg = (dh3 * ppv * sg * (1.0 - sg)).astype(BF16)
        dh2 = dh3 + lax.dot_general(dpg, extra_refs[3][...], _DNUMS["nt"], preferred_element_type=F32)
        out_refs[2][...] = dh2
        out_refs[3][...] = dh2.astype(BF16)
        out_refs[4][...] = (dh3 * sg).astype(BF16)
        out_refs[5][...] = dpg
        _accumulate_rows(out_refs[1], jnp.sum(dout * xh, axis=0, keepdims=True))
        _accumulate_rows(lacc, jnp.sum(diff * diff, axis=0, keepdims=True))

        @pl.when(pl.program_id(0) == n_tiles - 1)
        def _():
            out_refs[0][...] = (0.5 / d) * jnp.sum(lacc[...], axis=-1, keepdims=True)

    return epilogue


def _strided(r, n, d):
    return pl.ds(r, n, stride=d) if d > 1 else pl.ds(0, n)


def _rope_tables(pos, invf, tm):
    t = pos.shape[0]

    def body(pos_ref, invf_ref, c_ref, s_ref):
        ang = pos_ref[...].astype(F32) * invf_ref[...]
        lane = lax.broadcasted_iota(jnp.int32, ang.shape, 1)
        sn = jnp.sin(ang)
        c_ref[...] = jnp.where(lane < ROPE_DIM, jnp.cos(ang), 1.0)
        s_ref[...] = jnp.where(lane < ROPE_HALF, -sn, jnp.where(lane < ROPE_DIM, sn, 0.0))

    tab = _bs((tm, HEAD_DIM), lambda i: (i, 0))
    return pl.pallas_call(body, grid=(t // tm,), in_specs=[_bs((tm, 1), lambda i: (i, 0)), _bs((1, HEAD_DIM), lambda i: (0, 0))],
                          out_specs=[tab, tab], out_shape=[SDS((t, HEAD_DIM), F32)] * 2, compiler_params=_cp(1),
                          name="rope_tables")(pos, invf)


def _rope_partner(xv, first_half):
    return jnp.where(first_half, pltpu.roll(xv, HEAD_DIM - ROPE_HALF, 1), pltpu.roll(xv, ROPE_HALF, 1))


def _rope_dilate_epilogue(tm):
    def epilogue(prods, extra_refs, out_refs, scratch_refs):
        chunk = prods[0]
        c_s, s_s = extra_refs
        rot = scratch_refs[0]
        c = pl.program_id(1)
        pairs_of_heads = QK_W // (2 * HEAD_DIM)

        @pl.when(c < 2)
        def _():
            cc, ss = c_s[...], s_s[...]
            first_half = lax.broadcasted_iota(jnp.int32, cc.shape, 1) < ROPE_HALF
            for hp in range(pairs_of_heads):
                zp = chunk(slice(2 * hp * HEAD_DIM, 2 * (hp + 1) * HEAD_DIM))
                for s in range(2):
                    xv = zp[:, s * HEAD_DIM:(s + 1) * HEAD_DIM]
                    rot[2 * hp + s] = xv * cc + _rope_partner(xv, first_half) * ss

        @pl.when(c == 2)
        def _():
            for hp in range(pairs_of_heads):
                zp = chunk(slice(2 * hp * HEAD_DIM, 2 * (hp + 1) * HEAD_DIM))
                for s in range(2):
                    rot[2 * hp + s] = zp[:, s * HEAD_DIM:(s + 1) * HEAD_DIM]

        for g, (d, o_ref) in enumerate(zip(DILATIONS, out_refs)):
            n = tm // d
            for r in range(d):
                for hh in range(HEADS_PER_GROUP):
                    oc = r * GROUP_W + hh * HEAD_DIM
                    o_ref[:, oc:oc + HEAD_DIM] = rot[g * HEADS_PER_GROUP + hh, _strided(r, n, d), :].astype(BF16)

    return epilogue


def _band_masks(first_tile):
    qi = lax.broadcasted_iota(jnp.int32, (BLK, 2 * BLK), 0)
    kj = lax.broadcasted_iota(jnp.int32, (BLK, 2 * BLK), 1)
    band = (kj >= qi) & (kj <= qi + BLK)
    return band, band & ((kj >= BLK) | jnp.logical_not(first_tile))


def _attn_fwd(qkv, d, qt):
    ell = qkv.shape[1]
    nsub = qt // BLK
    scale = 1.0 / math.sqrt(HEAD_DIM)

    def body(q_ref, kc_ref, kp_ref, vc_ref, vp_ref, o_ref, lse_ref, kcat, vcat):
        nb = pl.program_id(1)
        kcat[0:BLK, :] = kp_ref[...]
        kcat[BLK:, :] = kc_ref[...]
        vcat[0:BLK, :] = vp_ref[...]
        vcat[BLK:, :] = vc_ref[...]
        lane = lax.broadcasted_iota(jnp.int32, (BLK, HEAD_DIM), 1)
        band, band_first = _band_masks(nb == 0)
        for b in range(nsub):
            valid = band_first if b == 0 else band
            lse_t = jnp.zeros((BLK, HEAD_DIM), F32)
            for hh in range(HEADS_PER_GROUP):
                cs = slice(hh * HEAD_DIM, (hh + 1) * HEAD_DIM)
                qb = q_ref[b * BLK:(b + 1) * BLK, cs]
                kk = kcat[b * BLK:(b + 2) * BLK, cs]
                vv = vcat[b * BLK:(b + 2) * BLK, cs]
                s = lax.dot_general(qb, kk, _DNUMS["nt"], preferred_element_type=F32) * scale
                s = jnp.where(valid, s, NEG)
                mx = jnp.max(s, axis=-1, keepdims=True)
                p = jnp.exp(s - mx)
                den = jnp.sum(p, axis=-1, keepdims=True)
                o = jnp.dot(p.astype(BF16), vv, preferred_element_type=F32) / den
                o_ref[b * BLK:(b + 1) * BLK, cs] = o
                lse_t = jnp.where(lane == hh, mx + jnp.log(den), lse_t)
            lse_ref[b * BLK:(b + 1) * BLK, :] = lse_t

    cur = lambda c: _bs((None, qt, GROUP_W), lambda r, nb: (c, nb, r))
    prev = lambda c: _bs((None, BLK, GROUP_W), lambda r, nb: (c, jnp.maximum(nb * nsub - 1, 0), r))
    return pl.pallas_call(
        body, grid=(d, ell // qt), in_specs=[cur(0), cur(1), prev(1), cur(2), prev(2)],
        out_specs=[_bs((qt, GROUP_W), lambda r, nb: (nb, r)), _bs((None, qt, HEAD_DIM), lambda r, nb: (r, nb, 0))],
        out_shape=[SDS((ell, d * GROUP_W), F32), SDS((d, ell, HEAD_DIM), F32)],
        scratch_shapes=[pltpu.VMEM((qt + BLK, GROUP_W), BF16)] * 2, compiler_params=_cp(2), name=f"attn_fwd_d{d}")(
            qkv, qkv, qkv, qkv, qkv)


def _attn_merge(outs, lses, tm):
    t = outs[0].shape[0]

    def body(o0, o1, o2, l0, l1, l2, attn_ref, attn_bf_ref, t0, t1, t2, so, sl, lt_s):
        for g, (d, o_ref, l_ref) in enumerate(zip(DILATIONS, (o0, o1, o2), (l0, l1, l2))):
            n = tm // d
            for r in range(d):
                rows = _strided(r, n, d)
                for hh in range(HEADS_PER_GROUP):
                    oc = r * GROUP_W + hh * HEAD_DIM
                    so[g * HEADS_PER_GROUP + hh, rows, :] = o_ref[:, oc:oc + HEAD_DIM]
                sl[g, rows, :] = l_ref[r]
        ls = [sl[g] for g in range(N_GROUPS)]
        mx = jnp.maximum(jnp.maximum(ls[0], ls[1]), ls[2])
        es = [jnp.exp(l - mx) for l in ls]
        den = es[0] + es[1] + es[2]
        ws = [e / den for e in es]
        lt_s[...] = mx + jnp.log(den)
        for hh in range(HEADS_PER_GROUP):
            cs = slice(hh * HEAD_DIM, (hh + 1) * HEAD_DIM)
            a = ws[0][:, hh:hh + 1] * so[hh]
            for g in range(1, N_GROUPS):
                a = a + ws[g][:, hh:hh + 1] * so[g * HEADS_PER_GROUP + hh]
            attn_ref[:, cs] = a
            attn_bf_ref[:, cs] = a.astype(BF16)
        for d, t_ref in zip(DILATIONS, (t0, t1, t2)):
            n = tm // d
            for r in range(d):
                t_ref[r] = lt_s[_strided(r, n, d), :]

    dil = lambda d: _bs((tm // d, d * GROUP_W), lambda i: (i, 0))
    lsp = lambda d: _bs((d, tm // d, HEAD_DIM), lambda i: (0, i, 0))
    row = _bs((tm, GROUP_W), lambda i: (i, 0))
    return pl.pallas_call(
        body, grid=(t // tm,),
        in_specs=[dil(d) for d in DILATIONS] + [lsp(d) for d in DILATIONS],
        out_specs=[row, row] + [lsp(d) for d in DILATIONS],
        out_shape=[SDS((t, GROUP_W), F32), SDS((t, GROUP_W), BF16)] + [SDS(l.shape, F32) for l in lses],
        scratch_shapes=[pltpu.VMEM((N_GROUPS * HEADS_PER_GROUP, tm, HEAD_DIM), F32), pltpu.VMEM((N_GROUPS, tm, HEAD_DIM), F32),
                        pltpu.VMEM((tm, HEAD_DIM), F32)],
        compiler_params=_cp(1), name="attn_merge")(*outs, *lses)


def _attn_bwd_pre(d_attn, attn, tm):
    t = attn.shape[0]

    def body(da_ref, a_ref, g0, g1, g2, e0, e1, e2, dl_s, da_s):
        lane = lax.broadcasted_iota(jnp.int32, (tm, HEAD_DIM), 1)
        dl = jnp.zeros((tm, HEAD_DIM), F32)
        for hh in range(HEADS_PER_GROUP):
            cs = slice(hh * HEAD_DIM, (hh + 1) * HEAD_DIM)
            dav = da_ref[:, cs]
            da_s[hh] = dav
            dl = jnp.where(lane == hh, jnp.sum(dav * a_ref[:, cs], axis=-1, keepdims=True), dl)
        dl_s[...] = dl
        for d, g_ref, e_ref in zip(DILATIONS, (g0, g1, g2), (e0, e1, e2)):
            n = tm // d
            for r in range(d):
                rows = _strided(r, n, d)
                for hh in range(HEADS_PER_GROUP):
                    oc = r * GROUP_W + hh * HEAD_DIM
                    g_ref[:, oc:oc + HEAD_DIM] = da_s[hh, rows, :].astype(BF16)
                e_ref[r] = dl_s[rows, :]

    row = _bs((tm, GROUP_W), lambda i: (i, 0))
    return pl.pallas_call(
        body, grid=(t // tm,), in_specs=[row, row],
        out_specs=[_bs((tm // d, d * GROUP_W), lambda i: (i, 0)) for d in DILATIONS]
        + [_bs((d, tm // d, HEAD_DIM), lambda i: (0, i, 0)) for d in DILATIONS],
        out_shape=[SDS((t // d, d * GROUP_W), BF16) for d in DILATIONS]
        + [SDS((d, t // d, HEAD_DIM), F32) for d in DILATIONS],
        scratch_shapes=[pltpu.VMEM((tm, HEAD_DIM), F32), pltpu.VMEM((HEADS_PER_GROUP, tm, HEAD_DIM), F32)],
        compiler_params=_cp(1), name="attn_bwd_pre")(d_attn, attn)


def _attn_bwd(qkv, d_a, lt, delta, d, qt):
    ell = qkv.shape[1]
    nsub = qt // BLK
    ntile = ell // qt
    nblk = ell // BLK
    scale = 1.0 / math.sqrt(HEAD_DIM)

    def body(q_ref, qn_ref, kc_ref, kp_ref, vc_ref, vp_ref, da_ref, dan_ref, lt_ref, ltn_ref, dl_ref, dln_ref, o_ref,
             kcat, vcat, dk_acc, dv_acc):
        nb = pl.program_id(1)
        kcat[0:BLK, :] = kp_ref[...]
        kcat[BLK:, :] = kc_ref[...]
        vcat[0:BLK, :] = vp_ref[...]
        vcat[BLK:, :] = vc_ref[...]
        qi = lax.broadcasted_iota(jnp.int32, (BLK, BLK), 0)
        kj = lax.broadcasted_iota(jnp.int32, (BLK, BLK), 1)
        valid_next = (kj >= qi) & (nb < ntile - 1)
        band, band_first = _band_masks(nb == 0)
        for hh in range(HEADS_PER_GROUP):
            cs = slice(hh * HEAD_DIM, (hh + 1) * HEAD_DIM)
            dk_acc[...] = jnp.zeros_like(dk_acc)
            dv_acc[...] = jnp.zeros_like(dv_acc)
            for b in range(nsub):
                rs = slice(b * BLK, (b + 1) * BLK)
                ks = slice(b * BLK, (b + 2) * BLK)
                valid = band_first if b == 0 else band
                qb, kk, vv, dab = q_ref[rs, cs], kcat[ks, cs], vcat[ks, cs], da_ref[rs, cs]
                s = lax.dot_general(qb, kk, _DNUMS["nt"], preferred_element_type=F32) * scale
                p = jnp.where(valid, jnp.exp(s - lt_ref[rs, hh:hh + 1]), 0.0)
                dp = lax.dot_general(dab, vv, _DNUMS["nt"], preferred_element_type=F32)
                ds = (p * (dp - dl_ref[rs, hh:hh + 1])).astype(BF16)
                o_ref[0, rs, cs] = jnp.dot(ds, kk, preferred_element_type=F32) * scale
                dk_acc[ks, :] += lax.dot_general(ds, qb, _DNUMS["tn"], preferred_element_type=F32) * scale
                dv_acc[ks, :] += lax.dot_general(p.astype(BF16), dab, _DNUMS["tn"], preferred_element_type=F32)
            ks = slice(nsub * BLK, (nsub + 1) * BLK)
            qn, kl, vl, dan = qn_ref[:, cs], kcat[ks, cs], vcat[ks, cs], dan_ref[:, cs]
            s = lax.dot_general(qn, kl, _DNUMS["nt"], preferred_element_type=F32) * scale
            p = jnp.where(valid_next, jnp.exp(s - ltn_ref[:, hh:hh + 1]), 0.0)
            dp = lax.dot_general(dan, vl, _DNUMS["nt"], preferred_element_type=F32)
            ds = (p * (dp - dln_ref[:, hh:hh + 1])).astype(BF16)
            dk_acc[ks, :] += lax.dot_general(ds, qn, _DNUMS["tn"], preferred_element_type=F32) * scale
            dv_acc[ks, :] += lax.dot_general(p.astype(BF16), dan, _DNUMS["tn"], preferred_element_type=F32)
            o_ref[1, :, cs] = dk_acc[BLK:, :]
            o_ref[2, :, cs] = dv_acc[BLK:, :]

    nxt = lambda nb: jnp.minimum((nb + 1) * nsub, nblk - 1)
    prv = lambda nb: jnp.maximum(nb * nsub - 1, 0)
    cur3 = lambda c: _bs((None, qt, GROUP_W), lambda r, nb: (c, nb, r))
    in_specs = [
        cur3(0), _bs((None, BLK, GROUP_W), lambda r, nb: (0, nxt(nb), r)),
        cur3(1), _bs((None, BLK, GROUP_W), lambda r, nb: (1, prv(nb), r)),
        cur3(2), _bs((None, BLK, GROUP_W), lambda r, nb: (2, prv(nb), r)),
        _bs((qt, GROUP_W), lambda r, nb: (nb, r)), _bs((BLK, GROUP_W), lambda r, nb: (nxt(nb), r)),
        _bs((None, qt, HEAD_DIM), lambda r, nb: (r, nb, 0)), _bs((None, BLK, HEAD_DIM), lambda r, nb: (r, nxt(nb), 0)),
        _bs((None, qt, HEAD_DIM), lambda r, nb: (r, nb, 0)), _bs((None, BLK, HEAD_DIM), lambda r, nb: (r, nxt(nb), 0)),
    ]
    return pl.pallas_call(
        body, grid=(d, ntile), in_specs=in_specs, out_specs=_bs((3, qt, GROUP_W), lambda r, nb: (0, nb, r)),
        out_shape=SDS((3, ell, d * GROUP_W), F32),
        scratch_shapes=[pltpu.VMEM((qt + BLK, GROUP_W), BF16)] * 2 + [pltpu.VMEM((qt + BLK, HEAD_DIM), F32)] * 2,
        compiler_params=_cp(2), name=f"attn_bwd_d{d}")(qkv, qkv, qkv, qkv, qkv, qkv, d_a, d_a, lt, lt, delta, delta)


def _undilate_rope_bwd(dqkvs, rope_c, rope_s, tm):
    t = rope_c.shape[0]

    def body(g0, g1, g2, c_s, s_s, o_ref, nat):
        c = pl.program_id(1)
        for g, (d, g_ref) in enumerate(zip(DILATIONS, (g0, g1, g2))):
            n = tm // d
            for r in range(d):
                for hh in range(HEADS_PER_GROUP):
                    oc = r * GROUP_W + hh * HEAD_DIM
                    nat[g * HEADS_PER_GROUP + hh, _strided(r, n, d), :] = g_ref[:, oc:oc + HEAD_DIM]

        @pl.when(c < 2)
        def _():
            cc, ss = c_s[...], s_s[...]
            first_half = lax.broadcasted_iota(jnp.int32, cc.shape, 1) < ROPE_HALF
            for h in range(QK_W // HEAD_DIM):
                xv = nat[h]
                y = xv * cc - _rope_partner(xv, first_half) * ss
                o_ref[:, h * HEAD_DIM:(h + 1) * HEAD_DIM] = y.astype(BF16)

        @pl.when(c == 2)
        def _():
            for h in range(QK_W // HEAD_DIM):
                o_ref[:, h * HEAD_DIM:(h + 1) * HEAD_DIM] = nat[h].astype(BF16)

    return pl.pallas_call(
        body, grid=(t // tm, 3),
        in_specs=[_bs((None, tm // d, d * GROUP_W), lambda i, c: (c, i, 0)) for d in DILATIONS]
        + [_bs((tm, HEAD_DIM), lambda i, c: (i, 0))] * 2,
        out_specs=_bs((tm, QK_W), lambda i, c: (i, c)), out_shape=SDS((t, 3 * QK_W), BF16),
        scratch_shapes=[pltpu.VMEM((QK_W // HEAD_DIM, tm, HEAD_DIM), F32)],
        compiler_params=_cp(2), name="undilate_rope_bwd")(*dqkvs, rope_c, rope_s)


def _cmul(ar, ai, br, bi):
    return ar * br - ai * bi, ar * bi + ai * br


def _ssm_disc(a_re, a_im, log_dt, nsq):
    def body(lr_ref, li_ref, ldt_ref, br_ref, bi_ref, zr_ref, zi_ref, pr_ref, pi_ref):
        lr, li = lr_ref[...], li_ref[...]
        dt = jnp.exp(ldt_ref[...])
        mag = jnp.exp(lr * dt)
        bar_re, bar_im = mag * jnp.cos(li * dt), mag * jnp.sin(li * dt)
        nr, ni = bar_re - 1.0, bar_im
        den = lr * lr + li * li
        br_ref[...], bi_ref[...] = bar_re, bar_im
        zr_ref[...] = (nr * lr + ni * li) / den
        zi_ref[...] = (ni * lr - nr * li) / den
        pr, pi = bar_re, bar_im
        for _ in range(nsq):
            pr, pi = _cmul(pr, pi, pr, pi)
        pr_ref[...], pi_ref[...] = pr, pi

    return pl.pallas_call(body, out_shape=[SDS(a_re.shape, F32)] * 6, name="ssm_discretise")(a_re, a_im, log_dt)


def _ssm_scale_b(z_re, z_im, b_re, b_im):
    def body(zr_ref, zi_ref, br_ref, bi_ref, or_ref, oi_ref):
        zr, zi, br, bi = zr_ref[...], zi_ref[...], br_ref[...], bi_ref[...]
        or_ref[...] = zr * br - zi * bi
        oi_ref[...] = zr * bi + zi * br

    return pl.pallas_call(body, out_shape=[SDS(b_re.shape, F32)] * 2, name="ssm_scale_b")(z_re, z_im, b_re, b_im)


def _ssm_scale_b_bwd(z_re, z_im, b_re, b_im, g_re, g_im):
    def body(zr_ref, zi_ref, br_ref, bi_ref, gr_ref, gi_ref, dbr_ref, dbi_ref, dzr_ref, dzi_ref):
        zr, zi, br, bi, gr, gi = zr_ref[...], zi_ref[...], br_ref[...], bi_ref[...], gr_ref[...], gi_ref[...]
        dbr_ref[...] = zr * gr + zi * gi
        dbi_ref[...] = zr * gi - zi * gr
        dzr_ref[...] = jnp.sum(br * gr + bi * gi, axis=-1, keepdims=True)
        dzi_ref[...] = jnp.sum(br * gi - bi * gr, axis=-1, keepdims=True)

    return pl.pallas_call(body, out_shape=[SDS(b_re.shape, F32)] * 2 + [SDS(z_re.shape, F32)] * 2,
                          name="ssm_scale_b_bwd")(z_re, z_im, b_re, b_im, g_re, g_im)


def _ssm_disc_bwd(a_re, a_im, log_dt, gb_re, gb_im, gz_re, gz_im):
    def body(lr_ref, li_ref, ldt_ref, gbr_ref, gbi_ref, gzr_ref, gzi_ref, dar_ref, dai_ref, dldt_ref):
        lr, li = lr_ref[...], li_ref[...]
        dt = jnp.exp(ldt_ref[...])
        mag = jnp.exp(lr * dt)
        bar_re, bar_im = mag * jnp.cos(li * dt), mag * jnp.sin(li * dt)
        nr, ni = bar_re - 1.0, bar_im
        den = lr * lr + li * li
        zr, zi = (nr * lr + ni * li) / den, (ni * lr - nr * li) / den
        gzr, gzi = gzr_ref[...], gzi_ref[...]
        gbr = gbr_ref[...] + (lr * gzr - li * gzi) / den
        gbi = gbi_ref[...] + (lr * gzi + li * gzr) / den
        qr, qi = (zr * lr + zi * li) / den, (zi * lr - zr * li) / den
        dar_ref[...] = dt * (bar_re * gbr + bar_im * gbi) - qr * gzr - qi * gzi
        dai_ref[...] = dt * (bar_re * gbi - bar_im * gbr) - qr * gzi + qi * gzr
        wr, wi = lr * bar_re - li * bar_im, lr * bar_im + li * bar_re
        dldt_ref[...] = dt * jnp.sum(wr * gbr + wi * gbi, axis=-1, keepdims=True)

    return pl.pallas_call(body, out_shape=[SDS(a_re.shape, F32)] * 2 + [SDS(log_dt.shape, F32)],
                          name="ssm_discretise_bwd")(a_re, a_im, log_dt, gb_re, gb_im, gz_re, gz_im)


def _interleave_epilogue(prods, extra_refs, out_refs, scratch_refs):
    uv = prods[0]
    tmp = scratch_refs[0]
    n = uv.shape[0] // N_DEV
    for b in range(SSM_W // BLK):
        cs = slice(b * BLK, (b + 1) * BLK)
        for j in range(N_DEV):
            tmp[b, pl.ds(j, n, stride=N_DEV), :] = uv[j * n:(j + 1) * n, cs]
        out_refs[0][:, cs] = tmp[b]
        out_refs[1][:, cs] = tmp[b].astype(BF16)


def _drive(src_ref, mat_ref, dst, mode):
    for kn in range(2 * SSM_NB):
        n = kn % SSM_NB
        a = src_ref[:, n * BLK:(n + 1) * BLK]
        dst[:, kn * 512:(kn + 1) * 512] = lax.dot_general(a, mat_ref[kn], _DNUMS[mode], preferred_element_type=F32)


def _scan_chunk(src, lam_ref, carry, *, reverse, store=None, h_ref=None, acc=None):
    steps = src.shape[0] // 8
    for c in range(NSTATE // SCAN_LANES):
        re = slice(c * SCAN_LANES, (c + 1) * SCAN_LANES)
        im = slice(NSTATE + c * SCAN_LANES, NSTATE + (c + 1) * SCAN_LANES)
        ar, ai = lam_ref[:, re], lam_ref[:, im]

        def step(s, val):
            i = (steps - 1 - s) if reverse else s
            rows = pl.ds(pl.multiple_of(i * 8, 8), 8)
            if acc is not None:
                hr, hi, dr, di = val
                pr, pi = h_ref[rows, re], h_ref[rows, im]
                dr = dr + hr * pr + hi * pi
                di = di + hi * pr - hr * pi
            else:
                hr, hi = val
            nr = ar * hr - ai * hi + src[rows, re]
            ni = ar * hi + ai * hr + src[rows, im]
            if store is not None:
                store[rows, re] = nr
                store[rows, im] = ni
            return (nr, ni, dr, di) if acc is not None else (nr, ni)

        init = (carry[:, re], carry[:, im])
        if acc is not None:
            init = init + (acc[:, re], acc[:, im])
        out = lax.fori_loop(0, steps, step, init, unroll=4)
        carry[:, re], carry[:, im] = out[0], out[1]
        if acc is not None:
            acc[:, re], acc[:, im] = out[2], out[3]


def _segment_carries(e_ref, pw_ref, out_ref, reverse):
    pr, pi = pw_ref[:, 0:NSTATE], pw_ref[:, NSTATE:]
    hr = jnp.zeros((1, NSTATE), F32)
    hi = jnp.zeros((1, NSTATE), F32)
    order = range(N_DEV - 1, -1, -1) if reverse else range(N_DEV)
    for j in order:
        out_ref[j:j + 1, 0:NSTATE] = hr
        out_ref[j:j + 1, NSTATE:] = hi
        tr, ti = _cmul(pr, pi, hr, hi)
        hr, hi = e_ref[j:j + 1, 0:NSTATE] + tr, e_ref[j:j + 1, NSTATE:] + ti


def _ssm_carries(name, src, mat, mode, lam8, pw, reverse):
    t = src.shape[0]
    nchunk = t // SCAN_ROWS

    def body(src_ref, mat_ref, lam_ref, pw_ref, out_ref, drive, carry):
        c = pl.program_id(0)

        @pl.when(c == 0)
        def _():
            carry[...] = jnp.zeros_like(carry)

        _drive(src_ref, mat_ref, drive, mode)
        _scan_chunk(drive, lam_ref, carry, reverse=reverse)

        @pl.when(c == nchunk - 1)
        def _():
            _segment_carries(carry, pw_ref, out_ref, reverse)

    blk = (lambda c: (nchunk - 1 - c, 0)) if reverse else (lambda c: (c, 0))
    return pl.pallas_call(
        body, grid=(nchunk,),
        in_specs=[_bs((SCAN_ROWS, SSM_W), blk), _bs(mat.shape, lambda c: (0, 0, 0)), _bs((8, 2 * NSTATE), lambda c: (0, 0)),
                  _bs((1, 2 * NSTATE), lambda c: (0, 0))],
        out_specs=_bs((8, 2 * NSTATE), lambda c: (0, 0)), out_shape=SDS((8, 2 * NSTATE), F32),
        scratch_shapes=[pltpu.VMEM((SCAN_ROWS, 2 * NSTATE), F32), pltpu.VMEM((8, 2 * NSTATE), F32)],
        compiler_params=_cp(1), name=name)(src, mat, lam8, pw)


def _ssm_fwd(u_bf, u, d_skip, bd, cd, lam8, start):
    t = u_bf.shape[0]
    nchunk = t // SCAN_ROWS
    per_seg = SCAN_ROWS // N_DEV

    def body(ub_ref, u_ref, d_ref, bd_ref, cd_ref, lam_ref, start_ref, h_ref, ys_ref, yg_ref, drive, carry, tmp):
        @pl.when(pl.program_id(0) == 0)
        def _():
            carry[...] = start_ref[...]

        _drive(ub_ref, bd_ref, drive, "nn")
        _scan_chunk(drive, lam_ref, carry, reverse=False, store=h_ref)
        for n in range(SSM_NB):
            cs = slice(n * BLK, (n + 1) * BLK)
            hr = h_ref[:, n * 512:(n + 1) * 512].astype(BF16)
            hi = h_ref[:, NSTATE + n * 512:NSTATE + (n + 1) * 512].astype(BF16)
            ys = (jnp.dot(hr, cd_ref[n], preferred_element_type=F32) + jnp.dot(hi, cd_ref[SSM_NB + n], preferred_element_type=F32)
                  + d_ref[:, cs] * u_ref[:, cs])
            ys_ref[:, cs] = ys
            tmp[n] = _gelu_parts(ys)[0]
            for j in range(N_DEV):
                yg_ref[j, :, cs] = tmp[n, pl.ds(j, per_seg, stride=N_DEV), :].astype(BF16)

    row = _bs((SCAN_ROWS, SSM_W), lambda c: (c, 0))
    h, ys, yg = pl.pallas_call(
        body, grid=(nchunk,),
        in_specs=[row, row, _bs((1, SSM_W), lambda c: (0, 0)), _bs(bd.shape, lambda c: (0, 0, 0)), _bs(cd.shape, lambda c: (0, 0, 0)),
                  _bs((8, 2 * NSTATE), lambda c: (0, 0)), _bs((8, 2 * NSTATE), lambda c: (0, 0))],
        out_specs=[_bs((SCAN_ROWS, 2 * NSTATE), lambda c: (c, 0)), row, _bs((N_DEV, per_seg, SSM_W), lambda c: (0, c, 0))],
        out_shape=[SDS((t, 2 * NSTATE), F32), SDS((t, SSM_W), F32), SDS((N_DEV, t // N_DEV, SSM_W), BF16)],
        scratch_shapes=[pltpu.VMEM((SCAN_ROWS, 2 * NSTATE), F32), pltpu.VMEM((8, 2 * NSTATE), F32),
                        pltpu.VMEM((SSM_NB, SCAN_ROWS, BLK), F32)],
        compiler_params=_cp(1), name="ssm_scan_fwd")(u_bf, u, d_skip, bd, cd, lam8, start)
    return h, ys, yg.reshape(t, SSM_W)


def _ssm_bwd(dys_bf, dys, d_skip, u_bf, h, bd, cd, lamc8, start):
    t = u_bf.shape[0]
    nchunk = t // SCAN_ROWS
    per_seg = SCAN_ROWS // N_DEV

    def body(dys_ref, dysf_ref, d_ref, u_ref, h_ref, bd_ref, cd_ref, lam_ref, start_ref, du_ref, dlam_ref, dbd_ref, dcd_ref,
             drive, adj, carry, tmp):
        c = pl.program_id(0)

        @pl.when(c == 0)
        def _():
            carry[...] = start_ref[...]
            dlam_ref[...] = jnp.zeros_like(dlam_ref)
            dbd_ref[...] = jnp.zeros_like(dbd_ref)
            dcd_ref[...] = jnp.zeros_like(dcd_ref)

        _drive(dys_ref, cd_ref, drive, "nt")
        _scan_chunk(drive, lam_ref, carry, reverse=True, store=adj, h_ref=h_ref, acc=dlam_ref)
        for n in range(SSM_NB):
            cs = slice(n * BLK, (n + 1) * BLK)
            acc = None
            for k in range(2):
                kn = k * SSM_NB + n
                ss = slice(kn * 512, (kn + 1) * 512)
                lam_b = adj[:, ss].astype(BF16)
                part = lax.dot_general(lam_b, bd_ref[kn], _DNUMS["nt"], preferred_element_type=F32)
                acc = part if acc is None else acc + part
                dbd_ref[kn] += lax.dot_general(u_ref[:, cs], lam_b, _DNUMS["tn"], preferred_element_type=F32)
                dcd_ref[kn] += lax.dot_general(h_ref[:, ss].astype(BF16), dys_ref[:, cs], _DNUMS["tn"],
                                               preferred_element_type=F32)
            tmp[n] = acc + d_ref[:, cs] * dysf_ref[:, cs]
            for j in range(N_DEV):
                du_ref[j, :, cs] = tmp[n, pl.ds(j, per_seg, stride=N_DEV), :].astype(BF16)

    rev = lambda c: (nchunk - 1 - c, 0)
    const2 = lambda c: (0, 0)
    const3 = lambda c: (0, 0, 0)
    row = _bs((SCAN_ROWS, SSM_W), rev)
    du, dlam, dbd, dcd = pl.pallas_call(
        body, grid=(nchunk,),
        in_specs=[row, row, _bs((1, SSM_W), const2), row, _bs((SCAN_ROWS, 2 * NSTATE), rev),
                  _bs(bd.shape, const3), _bs(cd.shape, const3), _bs((8, 2 * NSTATE), const2), _bs((8, 2 * NSTATE), const2)],
        out_specs=[_bs((N_DEV, per_seg, SSM_W), lambda c: (0, nchunk - 1 - c, 0)), _bs((8, 2 * NSTATE), const2),
                   _bs(bd.shape, const3), _bs(cd.shape, const3)],
        out_shape=[SDS((N_DEV, t // N_DEV, SSM_W), BF16), SDS((8, 2 * NSTATE), F32), SDS(bd.shape, F32), SDS(cd.shape, F32)],
        scratch_shapes=[pltpu.VMEM((SCAN_ROWS, 2 * NSTATE), F32), pltpu.VMEM((SCAN_ROWS, 2 * NSTATE), F32),
                        pltpu.VMEM((8, 2 * NSTATE), F32), pltpu.VMEM((SSM_NB, SCAN_ROWS, BLK), F32)],
        compiler_params=_cp(1), name="ssm_scan_bwd")(dys_bf, dys, d_skip, u_bf, h, bd, cd, lamc8, start)
    return du.reshape(t, SSM_W), dlam, dbd, dcd


def _gelu_parts(x):
    c0 = math.sqrt(2.0 / math.pi)
    inner = c0 * (x + 0.044715 * x * x * x)
    th = jnp.tanh(inner)
    val = 0.5 * x * (1.0 + th)
    grad = 0.5 * (1.0 + th) + 0.5 * x * (1.0 - th * th) * c0 * (1.0 + 3.0 * 0.044715 * x * x)
    return val, grad


def _ssm_carries_bwd(d_yg, ys, u, cd, lamc8, pwc):
    t = u.shape[0]
    nchunk = t // SCAN_ROWS
    per_seg = SCAN_ROWS // N_DEV

    def body(dg_ref, ys_ref, u_ref, cd_ref, lam_ref, pw_ref, out_ref, dys_ref, dysb_ref, dd_ref, drive, carry, tmp):
        c = pl.program_id(0)

        @pl.when(c == 0)
        def _():
            carry[...] = jnp.zeros_like(carry)

        for n in range(SSM_W // BLK):
            for j in range(N_DEV):
                tmp[n, pl.ds(j, per_seg, stride=N_DEV), :] = dg_ref[j, :, n * BLK:(n + 1) * BLK]
        dyg = jnp.concatenate([tmp[n] for n in range(SSM_W // BLK)], axis=1)
        dys = dyg * _gelu_parts(ys_ref[...])[1]
        dys_ref[...] = dys
        dysb_ref[...] = dys.astype(BF16)
        _accumulate_rows(dd_ref, jnp.sum(dys * u_ref[...], axis=0, keepdims=True))
        _drive(dysb_ref, cd_ref, drive, "nt")
        _scan_chunk(drive, lam_ref, carry, reverse=True)

        @pl.when(c == nchunk - 1)
        def _():
            _segment_carries(carry, pw_ref, out_ref, True)

    rev = lambda c: (nchunk - 1 - c, 0)
    row = _bs((SCAN_ROWS, SSM_W), rev)
    const2 = lambda c: (0, 0)
    return pl.pallas_call(
        body, grid=(nchunk,),
        in_specs=[_bs((N_DEV, per_seg, SSM_W), lambda c: (0, nchunk - 1 - c, 0)), row, row, _bs(cd.shape, lambda c: (0, 0, 0)),
                  _bs((8, 2 * NSTATE), const2), _bs((1, 2 * NSTATE), const2)],
        out_specs=[_bs((8, 2 * NSTATE), const2), row, row, _bs((1, SSM_W), const2)],
        out_shape=[SDS((8, 2 * NSTATE), F32), SDS((t, SSM_W), F32), SDS((t, SSM_W), BF16), SDS((1, SSM_W), F32)],
        scratch_shapes=[pltpu.VMEM((SCAN_ROWS, 2 * NSTATE), F32), pltpu.VMEM((8, 2 * NSTATE), F32),
                        pltpu.VMEM((SSM_W // BLK, SCAN_ROWS, BLK), F32)],
        compiler_params=_cp(1), name="ssm_carries_bwd")(d_yg.reshape(N_DEV, t // N_DEV, SSM_W), ys, u, cd, lamc8, pwc)


def _block_diag(blocks):
    nb, ng, r, c = blocks.shape
    eye = jnp.eye(ng, dtype=blocks.dtype)
    return (blocks[:, :, :, None, :] * eye[None, :, None, :, None]).reshape(nb, ng * r, ng * c)


def _diag_blocks(full, r, c):
    k, nb = full.shape[:2]
    ng = full.shape[2] // r
    x = full.reshape(k, nb, ng, r, ng, c)
    eye = jnp.eye(ng, dtype=full.dtype)
    return jnp.sum(x * eye[None, None, :, None, :, None], axis=4).reshape(k, nb * ng, r, c)


_SMALL = ("a_re", "a_im", "log_dt", "b_re", "b_im", "c_re", "c_im", "d_skip", "g_ffn", "g_final")


def _pack_small(arrs):
    flat = jnp.concatenate([a.reshape(-1) for a in arrs])
    pad = (-flat.shape[0]) % (8 * 128)
    return jnp.pad(flat, (0, pad)).reshape(-1, 128)


def _unpack_small(packed, shapes):
    flat = packed.reshape(-1)
    out, off = [], 0
    for s in shapes:
        n = math.prod(s)
        out.append(flat[off:off + n].reshape(s))
        off += n
    return out


def kernel(x, p, positions, g_mix, w_in, a_re, a_im, log_dt, b_re, b_im, c_re, c_im, d_skip, w_attn_proj, w_glu_a, w_glu_b, w_out, g_ffn, w_ffn_gate, w_ffn_up, w_ffn_down, w_ple_gate, w_ple_proj, g_final, loss_target, m_g_mix, m_w_in, m_a_re, m_a_im, m_log_dt, m_b_re, m_b_im, m_c_re, m_c_im, m_d_skip, m_w_attn_proj, m_w_glu_a, m_w_glu_b, m_w_out, m_g_ffn, m_w_ffn_gate, m_w_ffn_up, m_w_ffn_down, m_w_ple_gate, m_w_ple_proj, m_g_final, v_g_mix, v_w_in, v_a_re, v_a_im, v_log_dt, v_b_re, v_b_im, v_c_re, v_c_im, v_d_skip, v_w_attn_proj, v_w_glu_a, v_w_glu_b, v_w_out, v_g_ffn, v_w_ffn_gate, v_w_ffn_up, v_w_ffn_down, v_w_ple_gate, v_w_ple_proj, v_g_final):
    args = dict(locals())
    t, d = x.shape[1], x.shape[2]
    inw = w_in.shape[2] * N_DEV
    fs = w_ffn_gate.shape[2]
    ff = fs * N_DEV
    ple = w_ple_proj.shape[1]
    seg = t // N_DEV
    assert inw == 3 * QK_W + SSM_W + 2 * d and t % (N_DEV * SCAN_ROWS // 8) == 0 and seg & (seg - 1) == 0
    tm = min(1024, t)
    te = min(512, t)
    tk = min(2048, t)
    ucol = (3 * QK_W) // SSM_W
    gcol = (3 * QK_W + SSM_W) // d
    assert (3 * QK_W + SSM_W) % d == 0

    x2, p2, tgt = x[0], p[0, 0], loss_target[0]
    pos = positions.reshape(t, 1)
    inv = ROPE_THETA ** (-jnp.arange(ROPE_HALF, dtype=F32) * 2.0 / ROPE_DIM)
    invf = jnp.concatenate([inv, inv, jnp.zeros((HEAD_DIM - ROPE_DIM,), F32)]).reshape(1, HEAD_DIM)

    wnames = ("w_in", "w_attn_proj", "w_glu_a", "w_glu_b", "w_out", "w_ffn_gate", "w_ffn_up", "w_ffn_down", "w_ple_gate",
              "w_ple_proj")
    kinds = ("cols", "cols", "cols", "cols", "rows", "slot", "slot", "rows", "rows", "cols")
    shards = [args[n][0].astype(BF16) for n in wnames]
    sizes = [s.shape[0] if k == "rows" else s.shape[-1] for s, k in zip(shards, kinds)]
    ag = _exchange_start("gather_weights_start", shards, kinds, sizes, True)

    row_d = _bs((tm, d), lambda i, j, k: (i, 0))
    row_e = _bs((te, d), lambda i, j, k: (i, 0))
    vec_d = _bs((1, d), lambda i, j, k: (0, 0))
    sq_w = _bs((d, d), lambda i, j, k: (0, 0))
    n1 = _rms_fwd("norm_mix", x2, g_mix + ag[3][0:1, 0:1], tm)

    nsq = seg.bit_length() - 1
    bar_re, bar_im, z_re, z_im, pw_re, pw_im = _ssm_disc(a_re[0], a_im[0], log_dt.reshape(SSM_GROUPS, 1), nsq)
    gp = SSM_GROUPS * SSM_STATE
    b_re2, b_im2 = b_re.reshape(gp, SSM_GROUP), b_im.reshape(gp, SSM_GROUP)
    bb_re, bb_im = _ssm_scale_b(z_re.reshape(gp, 1), z_im.reshape(gp, 1), b_re2, b_im2)

    def chunks(a, r, c):
        return a.reshape(SSM_NB, SSM_GROUPS // SSM_NB, r, c)

    bbt = lambda a: jnp.swapaxes(a.reshape(SSM_GROUPS, SSM_STATE, SSM_GROUP), 1, 2)
    bd = jnp.concatenate([_block_diag(chunks(bbt(bb_re), SSM_GROUP, SSM_STATE)),
                          _block_diag(chunks(bbt(bb_im), SSM_GROUP, SSM_STATE))]).astype(BF16)
    ct = lambda a: jnp.swapaxes(a[0], 1, 2)
    cd = jnp.concatenate([_block_diag(chunks(ct(c_re), SSM_STATE, SSM_GROUP)),
                          _block_diag(chunks(-ct(c_im), SSM_STATE, SSM_GROUP))]).astype(BF16)
    lam = jnp.concatenate([bar_re.reshape(1, gp), bar_im.reshape(1, gp)], axis=1)
    lamc = jnp.concatenate([bar_re.reshape(1, gp), -bar_im.reshape(1, gp)], axis=1)
    pw = jnp.concatenate([pw_re.reshape(1, gp), pw_im.reshape(1, gp)], axis=1)
    pwc = jnp.concatenate([pw_re.reshape(1, gp), -pw_im.reshape(1, gp)], axis=1)
    lam8, lamc8 = jnp.broadcast_to(lam, (8, 2 * gp)), jnp.broadcast_to(lamc, (8, 2 * gp))

    pk = lambda pre: jnp.concatenate([_pack_small([args[pre + n] for n in _SMALL]), _pack_small([args[pre + "g_mix"]])])
    packed = [pk(""), pk("m_"), pk("v_")]

    rope_c, rope_s = _rope_tables(pos, invf, tm)

    W_in, = _exchange_wait("gather_w_in_wait", ag, [0], kinds, sizes, True,
                           [n1, bd, cd, lam8, lamc8, pw, pwc, rope_c, rope_s] + packed)
    tab = _bs((tm, HEAD_DIM), lambda i, j, k: (i, 0))
    qkv = _mm(
        "qkv_proj", (t // tm, 3, 1), [("nn", n1, row_d, W_in, _bs((d, QK_W), lambda i, j, k: (0, j)))],
        [(SDS((3, t // dil, dil * GROUP_W), BF16), _bs((None, tm // dil, dil * GROUP_W), lambda i, j, k: (j, i, 0)))
         for dil in DILATIONS],
        extras=[(rope_c, tab), (rope_s, tab)],
        epilogue=_rope_dilate_epilogue(tm), scratch=[pltpu.VMEM((QK_W // HEAD_DIM, tm, HEAD_DIM), F32)], by_columns=True)
    row_s = _bs((tm, SSM_W), lambda i, j, k: (i, 0))
    u_perm, u_bf = _mm("u_proj", (t // tm, 1, 1),
                       [("nn", n1.reshape(N_DEV, seg, d), _bs((N_DEV, tm // N_DEV, d), lambda i, j, k: (0, i, 0)), W_in,
                         _bs((d, SSM_W), lambda i, j, k: (0, ucol)))],
                       [(SDS((t, SSM_W), F32), row_s), (SDS((t, SSM_W), BF16), row_s)], epilogue=_interleave_epilogue,
                       scratch=[pltpu.VMEM((SSM_W // BLK, tm, BLK), F32)])
    zg, = _mm("z_gates", (t // tm, 2, 1),
              [("nn", n1, row_d, W_in, _bs((d, d), lambda i, j, k: (0, gcol + j)))],
              [(SDS((t, 2 * d), BF16), _bs((tm, d), lambda i, j, k: (i, j)))])

    outs, lses = [], []
    for g, dil in enumerate(DILATIONS):
        o_g, l_g = _attn_fwd(qkv[g], dil, min(1024, t // dil))
        outs.append(o_g)
        lses.append(l_g)
    merged = _attn_merge(outs, lses, tm)
    attn, attn_bf, lts = merged[0], merged[1], merged[2:]

    start_f = _ssm_carries("ssm_carries_fwd", u_bf, bd, "nn", lam8, pw, False)
    dsk = d_skip.reshape(1, SSM_W)
    h_all, ys, yg_bf = _ssm_fwd(u_bf, u_perm, dsk, bd, cd, lam8, start_f)
    W_ap, W_ga, W_gb, W_out, W_fg, W_fu, W_fd, W_pg, W_pp = _exchange_wait(
        "gather_rest_wait", ag, list(range(1, len(wnames))), kinds, sizes, True, yg_bf)
    W_fg = jnp.swapaxes(W_fg, 0, 1).reshape(d, ff)
    W_fu = jnp.swapaxes(W_fu, 0, 1).reshape(d, ff)

    glu_w = _bs((SSM_W, d), lambda i, j, k: (0, 0))
    row_s = _bs((tm, SSM_W), lambda i, j, k: (i, 0))
    gate_a = _bs((te, d), lambda i, j, k: (i, 0))
    gate_s = _bs((te, d), lambda i, j, k: (i, 1))
    td_f32, td_bf = SDS((t, d), F32), SDS((t, d), BF16)
    m_bf, ya, yb, attn_d = _mm(
        "glu_merge", (t // tm, 1, 1),
        [("nn", yg_bf, row_s, W_ga, glu_w), ("nn", yg_bf, row_s, W_gb, glu_w), ("nn", attn_bf, row_s, W_ap, glu_w)],
        [(td_bf, row_d)] * 4, extras=[(zg, row_d), (zg, _bs((tm, d), lambda i, j, k: (i, 1)))], epilogue=_glu_merge_epilogue)

    h1, n2 = _mm("out_proj", (t // tm, 1, 1), [("nn", m_bf, row_d, W_out, sq_w)], [(td_f32, row_d), (td_bf, row_d)],
                 extras=[(x2, row_d), (g_ffn, vec_d)], epilogue=_out_norm_epilogue)

    tn_f = ff // 2
    nf = ff // tn_f
    hid_o = _bs((tm, tn_f), lambda j, i, k: (i, j))
    tf_bf = SDS((t, ff), BF16)
    a_rows = _bs((tm, d), lambda j, i, k: (i, 0))
    w_cols = _bs((d, tn_f), lambda j, i, k: (0, j))
    act, fg, fu = _mm("ffn_gate_up", (nf, t // tm, 1), [("nn", n2, a_rows, W_fg, w_cols), ("nn", n2, a_rows, W_fu, w_cols)],
                      [(tf_bf, hid_o)] * 3, epilogue=_swiglu_epilogue)
    w_once = pl.BlockSpec((d, d), lambda i, j, k: (0, 0), pipeline_mode=pl.Buffered(1))
    loss_part, dg_final, dh2, dh2_bf, dpp_bf, dpg_bf, h2_bf = _mm(
        "ffn_down_head", (t // te, 1, 1),
        [("nn", act, _bs((te, ff), lambda i, j, k: (i, 0)), W_fd,
          pl.BlockSpec((ff, d), lambda i, j, k: (0, 0), pipeline_mode=pl.Buffered(1))),
         ("nn", p2, _bs((te, ple), lambda i, j, k: (i, 0)), W_pp, _bs((ple, d), lambda i, j, k: (0, 0)))],
        [(SDS((1, 1), F32), _bs((1, 1), lambda i, j, k: (0, 0))), (SDS((1, d), F32), vec_d), (td_f32, row_e), (td_bf, row_e),
         (td_bf, row_e), (td_bf, row_e), (td_bf, row_e)],
        extras=[(h1, row_e), (g_final.reshape(1, d), vec_d), (tgt, row_e), (W_pg, w_once)], epilogue=_head_epilogue(t // te),
        scratch=[pltpu.VMEM((1, d), F32)])
    loss = lax.psum(loss_part[0, 0], ("x", "y", "c"))

    nkt = t // tk
    tok_a = lambda w: _bs((tk, w), lambda i, j, k: (k, 0))

    def wgrad(name, a, wa, b, wb):
        return _mm(name, (1, 1, nkt), [("tn", a, tok_a(wa), b, tok_a(wb))],
                   [(SDS((wa, wb), BF16), _bs((wa, wb), lambda i, j, k: (0, 0)))])[0]

    dW_pp = wgrad("dw_ple_proj", p2, ple, dpp_bf, d)
    dW_pg = wgrad("dw_ple_gate", h2_bf, d, dpg_bf, d)
    dfg_bf, dfu_bf = _mm("d_ffn_down", (nf, t // tm, 1),
                         [("nt", dh2_bf, a_rows, W_fd, _bs((tn_f, d), lambda j, i, k: (j, 0)))],
                         [(tf_bf, hid_o), (tf_bf, hid_o)], extras=[(fg, hid_o), (fu, hid_o)], epilogue=_swiglu_bwd_epilogue)
    dW_fd, = _mm("dw_ffn_down", (nf, 1, nkt), [("tn", act, _bs((tk, tn_f), lambda i, j, k: (k, i)), dh2_bf, tok_a(d))],
                 [(SDS((ff, d), BF16), _bs((tn_f, d), lambda i, j, k: (i, 0)))])
    hid_t = _bs((tk, tn_f), lambda i, j, k: (k, j))
    wg_o = [(SDS((d, ff), BF16), _bs((d, tn_f), lambda i, j, k: (0, j)))]
    dW_fg, = _mm("dw_ffn_gate", (1, nf, nkt), [("tn", n2, tok_a(d), dfg_bf, hid_t)], wg_o)
    dW_fu, = _mm("dw_ffn_up", (1, nf, nkt), [("tn", n2, tok_a(d), dfu_bf, hid_t)], wg_o)
    dW_fg = jnp.swapaxes(dW_fg.reshape(d, N_DEV, fs), 0, 1)
    dW_fu = jnp.swapaxes(dW_fu.reshape(d, N_DEV, fs), 0, 1)
    group = lambda names: ([kinds[wnames.index(n)] for n in names], [sizes[wnames.index(n)] for n in names])
    ffn_names = ("w_ffn_gate", "w_ffn_up", "w_ffn_down", "w_ple_gate", "w_ple_proj")
    rs_ffn = _exchange_start("scatter_ffn_start", [dW_fg, dW_fu, dW_fd, dW_pg, dW_pp], *group(ffn_names), False)
    hid_all = _bs((te, ff), lambda i, j, k: (i, 0))
    w_all = pl.BlockSpec((d, ff), lambda i, j, k: (0, 0), pipeline_mode=pl.Buffered(1))
    dh1, dh1_bf, dg_ffn = _mm("d_ffn_gate_up", (t // te, 1, 1),
                              [("nt", dfg_bf, hid_all, W_fg, w_all), ("nt", dfu_bf, hid_all, W_fu, w_all)],
                              [(td_f32, row_e), (td_bf, row_e), (SDS((1, d), F32), vec_d)],
                              extras=[(h1, row_e), (g_ffn, vec_d), (dh2, row_e)], epilogue=_rms_bwd_epilogue, after=rs_ffn[3])

    dW_out = wgrad("dw_out", m_bf, d, dh1_bf, d)
    glu_once = pl.BlockSpec((SSM_W, d), lambda i, j, k: (0, 0), pipeline_mode=pl.Buffered(1))
    row_es = _bs((te, SSM_W), lambda i, j, k: (i, 0))
    ts_f32 = SDS((t, SSM_W), F32)
    dz_g, dad_bf, dya_bf, dyb_bf, d_yg, d_attn = _mm(
        "d_out_proj", (t // te, 1, 1), [("nt", dh1_bf, row_e, W_out, w_once)],
        [(SDS((t, 2 * d), BF16), _bs((te, 2 * d), lambda i, j, k: (i, 0))), (td_bf, row_e), (td_bf, row_e), (td_bf, row_e),
         (ts_f32, row_es), (ts_f32, row_es)],
        extras=[(zg, gate_a), (zg, gate_s), (attn_d, row_e), (ya, row_e), (yb, row_e), (W_ga, glu_once), (W_gb, glu_once),
                (W_ap, glu_once)], epilogue=_merge_bwd_epilogue)

    dW_ga = wgrad("dw_glu_a", yg_bf, SSM_W, dya_bf, d)
    dW_gb = wgrad("dw_glu_b", yg_bf, SSM_W, dyb_bf, d)
    start_b, dys, dys_bf, dd_skip = _ssm_carries_bwd(d_yg, ys, u_perm, cd, lamc8, pwc)
    dz_u, dlam8, dbd, dcd = _ssm_bwd(dys_bf, dys, dsk, u_bf, h_all, bd, cd, lamc8, start_b)
    dlam = jnp.sum(dlam8, axis=0)
    dbb = _diag_blocks(dbd.reshape(2, SSM_NB, BLK, 512), SSM_GROUP, SSM_STATE)
    dbb_re = jnp.swapaxes(dbb[0], 1, 2).reshape(gp, SSM_GROUP)
    dbb_im = jnp.swapaxes(dbb[1], 1, 2).reshape(gp, SSM_GROUP)
    dcc = _diag_blocks(dcd.reshape(2, SSM_NB, 512, BLK), SSM_STATE, SSM_GROUP)
    dc_re, dc_im = jnp.swapaxes(dcc[0], 1, 2), -jnp.swapaxes(dcc[1], 1, 2)
    db_re, db_im, dz_re, dz_im = _ssm_scale_b_bwd(z_re.reshape(gp, 1), z_im.reshape(gp, 1), b_re2, b_im2, dbb_re, dbb_im)
    gshape = (SSM_GROUPS, SSM_STATE)
    da_re, da_im, dlog_dt = _ssm_disc_bwd(a_re[0], a_im[0], log_dt.reshape(SSM_GROUPS, 1), dlam[:gp].reshape(gshape),
                                          dlam[gp:].reshape(gshape), dz_re.reshape(gshape), dz_im.reshape(gshape))

    dW_ap = wgrad("dw_attn_proj", attn_bf, GROUP_W, dad_bf, d)
    pre = _attn_bwd_pre(d_attn, attn, tm)
    das, deltas = pre[:N_GROUPS], pre[N_GROUPS:]
    dqkvs = [_attn_bwd(qkv[g], das[g], lts[g], deltas[g], dil, min(1024, t // dil)) for g, dil in enumerate(DILATIONS)]
    dz_qkv = _undilate_rope_bwd(dqkvs, rope_c, rope_s, tm)

    dW_in, = _mm("dw_in_qkv", (1, 3, nkt), [("tn", n1, tok_a(d), dz_qkv, _bs((tk, QK_W), lambda i, j, k: (k, j)))],
                 [(SDS((d, inw), BF16), _bs((d, QK_W), lambda i, j, k: (0, j)))])
    dW_in, = _mm("dw_in_u", (1, 1, nkt), [("tn", n1, tok_a(d), dz_u, tok_a(SSM_W))],
                 [(SDS((d, inw), BF16), _bs((d, SSM_W), lambda i, j, k: (0, ucol)))], alias_to_out0=dW_in)
    dW_in, = _mm("dw_in_gates", (1, 2, nkt), [("tn", n1, tok_a(d), dz_g, _bs((tk, d), lambda i, j, k: (k, j)))],
                 [(SDS((d, inw), BF16), _bs((d, d), lambda i, j, k: (0, gcol + j)))], alias_to_out0=dW_in)
    small_parts = dict(a_re=da_re, a_im=da_im, log_dt=dlog_dt, b_re=db_re, b_im=db_im, c_re=dc_re, c_im=dc_im,
                       d_skip=dd_skip, g_ffn=dg_ffn, g_final=dg_final)
    small = _pack_small([small_parts[n] for n in _SMALL])
    rest_names = ("w_in", "w_attn_proj", "w_glu_a", "w_glu_b", "w_out")
    rest_kinds, rest_sizes = group(rest_names)
    rs_in = _exchange_start("scatter_rest_start", [dW_in, dW_ap, dW_ga, dW_gb, dW_out, small], rest_kinds + ["all"],
                            rest_sizes + [0], False)
    w_piece = lambda w, cb: pl.BlockSpec((d, w), lambda i, j, k: (0, cb), pipeline_mode=pl.Buffered(1))
    dx, dg_mix = _mm(
        "d_z_proj", (t // te, 1, 1),
        [("nt", dz_qkv, _bs((te, 3 * QK_W), lambda i, j, k: (i, 0)), W_in, w_piece(3 * QK_W, 0)),
         ("nt", dz_u, _bs((te, SSM_W), lambda i, j, k: (i, 0)), W_in, w_piece(SSM_W, ucol)),
         ("nt", dz_g, _bs((te, d), lambda i, j, k: (i, 0)), W_in, w_piece(d, gcol)),
         ("nt", dz_g, _bs((te, d), lambda i, j, k: (i, 1)), W_in, w_piece(d, gcol + 1))],
        [(td_f32, row_e), (SDS((1, d), F32), vec_d)],
        extras=[(x2, row_e), (g_mix, vec_d), (dh1, row_e)], epilogue=_rms_bwd_epilogue, after=rs_in[3])

    received = dict(zip(ffn_names, _exchange_wait("scatter_ffn_wait", rs_ffn, list(range(len(ffn_names))), *group(ffn_names),
                                                  False, dx)))
    *landed, small_all = _exchange_wait("scatter_rest_wait", rs_in, list(range(len(rest_names) + 1)), rest_kinds + ["all"],
                                        rest_sizes + [0], False, dx)
    received.update(zip(rest_names, landed))

    new = {}
    for n in wnames:
        new[n] = [o.reshape(args[n].shape)
                  for o in _adamw("adamw_" + n, received[n], args[n][0], args["m_" + n][0], args["v_" + n][0])]
    g_mix_all = _gather_small(_pack_small([dg_mix]))
    sm = _adamw("adamw_small", jnp.concatenate([small_all, g_mix_all], axis=1), *packed)
    rows_a = small.shape[0]
    shapes = [args[n].shape for n in _SMALL]
    for n, vals in zip(_SMALL, zip(*[_unpack_small(o[:rows_a], shapes) for o in sm])):
        new[n] = list(vals)
    new["g_mix"] = [_unpack_small(o[rows_a:], [g_mix.shape])[0] for o in sm]

    order = ("g_mix", "w_in", "a_re", "a_im", "log_dt", "b_re", "b_im", "c_re", "c_im", "d_skip", "w_attn_proj", "w_glu_a",
             "w_glu_b", "w_out", "g_ffn", "w_ffn_gate", "w_ffn_up", "w_ffn_down", "w_ple_gate", "w_ple_proj", "g_final")
    return (loss, dx.reshape(x.shape), *[new[n][0] for n in order], *[new[n][1] for n in order],
            *[new[n][2] for n in order], *[new[n][3] for n in order])
```
